```python
import jax, jax.numpy as jnp
from jax import lax
import numpy as np

D_MODEL = 2048
BATCH = 8
SEQ = 4096
DEPTH = 1

GRID_W = 64
CTX_LEN = 256
D_INNER = D_MODEL
W_A = D_INNER // 2
W_B = D_INNER - W_A
HG_HEADS = 8
HG_DK = W_A // HG_HEADS
HG_DV = W_A // HG_HEADS
ML_HEADS = 4
ML_DH = W_B // ML_HEADS
CHUNK = 64
CONV_K = 3
N_IN = 5 * W_A + 5 * W_B + 4 * ML_HEADS
ALPHA = (2 * DEPTH) ** 0.25
BETA = (8 * DEPTH) ** -0.25
LN_EPS = 1e-5
NORM_EPS = 1e-6

kernel_name = "hymba_hgrn2_mlstm_bidir_dit_block"


def layer_norm(a, g, b):
    af = a.astype(jnp.float32)
    mu = jnp.mean(af, axis=-1, keepdims=True)
    var = jnp.mean(jnp.square(af - mu), axis=-1, keepdims=True)
    out = (af - mu) * lax.rsqrt(var + LN_EPS) * g.astype(jnp.float32) + b.astype(jnp.float32)
    return out.astype(a.dtype)


def modulate(a, shift, scale):
    af = a.astype(jnp.float32)
    mu = jnp.mean(af, axis=-1, keepdims=True)
    var = jnp.mean(jnp.square(af - mu), axis=-1, keepdims=True)
    n = (af - mu) * lax.rsqrt(var + LN_EPS)
    return (n * (1.0 + scale.astype(jnp.float32)) + shift.astype(jnp.float32)).astype(a.dtype)


def rms_norm(a):
    return a * lax.rsqrt(jnp.mean(jnp.square(a), axis=-1, keepdims=True) + NORM_EPS)


def head_layer_norm(a):
    mu = jnp.mean(a, axis=-1, keepdims=True)
    var = jnp.mean(jnp.square(a - mu), axis=-1, keepdims=True)
    return (a - mu) * lax.rsqrt(var + NORM_EPS)


def heads(a, n_heads):
    return a.reshape(a.shape[:2] + (n_heads, a.shape[-1] // n_heads))


def flip(a):
    return jnp.flip(a, axis=1)


def to_chunks(a):
    bsz, t = a.shape[:2]
    a = a.reshape((bsz, t // CHUNK, CHUNK) + a.shape[2:])
    return jnp.swapaxes(jnp.moveaxis(a, 1, 0), 2, 3)


def from_chunks(o):
    o = jnp.moveaxis(jnp.swapaxes(o, 2, 3), 0, 1)
    return o.reshape((o.shape[0], o.shape[1] * o.shape[2]) + o.shape[3:])


def hgrn2_scan(q, k, v, logf, s0):
    mask = jnp.tril(jnp.ones((CHUNK, CHUNK), dtype=bool))[:, :, None]

    def step(s, inp):
        qc, kc, vc, gc = inp
        b = jnp.cumsum(gc, axis=2)
        o_inter = jnp.einsum('bhtk,bhkv->bhtv', qc * jnp.exp(b), s)
        diff = b[:, :, :, None, :] - b[:, :, None, :, :]
        decay = jnp.exp(jnp.where(mask, diff, -jnp.inf))
        scores = jnp.einsum('bhtk,bhsk,bhtsk->bhts', qc, kc, decay)
        o = o_inter + jnp.einsum('bhts,bhsv->bhtv', scores, vc)
        b_last = b[:, :, -1]
        k_dec = kc * jnp.exp(b_last[:, :, None, :] - b)
        s_new = jnp.exp(b_last)[..., None] * s + jnp.einsum('bhsk,bhsv->bhkv', k_dec, vc)
        return s_new, o

    s_fin, o = lax.scan(step, s0, (to_chunks(q), to_chunks(k), to_chunks(v), to_chunks(logf)))
    return from_chunks(o), s_fin


def mlstm_scan(q, k, v, log_i, log_f, state):
    mask = jnp.tril(jnp.ones((CHUNK, CHUNK), dtype=bool))

    def step(carry, inp):
        c_mat, n_vec, m = carry
        qc, kc, vc, ic, fc = inp
        b = jnp.cumsum(fc, axis=-1)
        log_w = jnp.where(mask, b[..., :, None] - b[..., None, :] + ic[..., None, :], -jnp.inf)
        m_inter = b + m[..., None]
        m_t = jnp.maximum(m_inter, jnp.max(log_w, axis=-1))
        w_inter = jnp.exp(m_inter - m_t)
        w_qk = jnp.exp(log_w - m_t[..., None]) * jnp.einsum('bhtk,bhsk->bhts', qc, kc)
        num = (w_inter[..., None] * jnp.einsum('bhvk,bhtk->bhtv', c_mat, qc)
               + jnp.einsum('bhts,bhsv->bhtv', w_qk, vc))
        den = w_inter * jnp.einsum('bhk,bhtk->bht', n_vec, qc) + jnp.sum(w_qk, axis=-1)
        h = num / jnp.maximum(jnp.abs(den), jnp.exp(-m_t))[..., None]
        m_new = m_t[..., -1]
        w_s = jnp.exp(b[..., -1:] - b + ic - m_new[..., None])
        decay = jnp.exp(b[..., -1] + m - m_new)
        c_new = decay[..., None, None] * c_mat + jnp.einsum('bhsv,bhsk->bhvk', w_s[..., None] * vc, kc)
        n_new = decay[..., None] * n_vec + jnp.einsum('bhs,bhsk->bhk', w_s, kc)
        return (c_new, n_new, m_new), h

    st, h = lax.scan(step, state, (to_chunks(q), to_chunks(k), to_chunks(v), to_chunks(log_i), to_chunks(log_f)))
    return from_chunks(h), st


def short_conv(a, w, b, grid):
    ch = a.shape[-1]
    w = w.astype(jnp.float32)
    if grid:
        rows = a.shape[1] // GRID_W
        img = a.reshape(a.shape[0], rows, GRID_W, ch)
        out = lax.conv_general_dilated(img, w[:, :, None, :], (1, 1), 'SAME',
                                       dimension_numbers=('NHWC', 'HWIO', 'NHWC'), feature_group_count=ch)
        out = out.reshape(a.shape)
    else:
        out = lax.conv_general_dilated(a, w[1][:, None, :], (1,), 'SAME',
                                       dimension_numbers=('NWC', 'WIO', 'NWC'), feature_group_count=ch)
    return jax.nn.silu(out + b.astype(jnp.float32))


def hgrn2_gates(z, lb):
    f = lb + (1.0 - lb) * jax.nn.sigmoid(z)
    return jnp.log(f), 1.0 - f


def zero_states(bsz):
    hg = jnp.zeros((bsz, HG_HEADS, HG_DK, HG_DV), jnp.float32)
    ml = (jnp.zeros((bsz, ML_HEADS, ML_DH, ML_DH), jnp.float32),
          jnp.zeros((bsz, ML_HEADS, ML_DH), jnp.float32),
          jnp.zeros((bsz, ML_HEADS), jnp.float32))
    return (hg, hg, ml, ml)


def mix(u, grid, states, lb, conv_w_l, conv_b_l, gate_b_l, hg_norm_l, ml_norm_l):
    u = u.astype(jnp.float32)
    bsz, t = u.shape[:2]
    splits = [W_A, 2 * W_A, 3 * W_A, 4 * W_A, 5 * W_A,
              5 * W_A + 2 * W_B, 5 * W_A + 3 * W_B, 5 * W_A + 4 * W_B, 5 * W_A + 5 * W_B]
    a_q, a_ff, a_fb, a_i, a_z, b_qk, b_v, b_o, b_z, b_g = jnp.split(u, splits, axis=-1)
    hg_f0, hg_b0, ml_f0, ml_b0 = states

    q_a = heads(jax.nn.silu(a_q), HG_HEADS)
    v_a = heads(a_i, HG_HEADS)
    logf_f, k_f = hgrn2_gates(a_ff, lb[0])
    logf_b, k_b = hgrn2_gates(a_fb, lb[1])
    o_f, hg_f = hgrn2_scan(q_a, heads(k_f, HG_HEADS), v_a, heads(logf_f, HG_HEADS), hg_f0)
    o_b, hg_b = hgrn2_scan(flip(q_a), flip(heads(k_b, HG_HEADS)), flip(v_a), flip(heads(logf_b, HG_HEADS)), hg_b0)
    o_a = rms_norm(o_f + flip(o_b)) * hg_norm_l.astype(jnp.float32).reshape(HG_HEADS, HG_DV)
    y_a = o_a.reshape(bsz, t, W_A) * jax.nn.silu(a_z)

    qk = short_conv(b_qk, conv_w_l, conv_b_l, grid)
    q_m, k_m = jnp.split(qk, 2, axis=-1)
    q_m = heads(q_m, ML_HEADS)
    k_m = heads(k_m, ML_HEADS) * (ML_DH ** -0.5)
    v_m = heads(b_v, ML_HEADS)
    g = b_g.reshape(bsz, t, 4, ML_HEADS) + gate_b_l.astype(jnp.float32)
    log_i_f, log_i_b = g[:, :, 0], g[:, :, 1]
    log_f_f, log_f_b = jax.nn.log_sigmoid(g[:, :, 2]), jax.nn.log_sigmoid(g[:, :, 3])
    h_f, ml_f = mlstm_scan(q_m, k_m, v_m, log_i_f, log_f_f, ml_f0)
    h_b, ml_b = mlstm_scan(flip(q_m), flip(k_m), flip(v_m), flip(log_i_b), flip(log_f_b), ml_b0)
    h = head_layer_norm(h_f + flip(h_b)) * ml_norm_l.astype(jnp.float32).reshape(ML_HEADS, ML_DH)
    y_b = h.reshape(bsz, t, W_B) * jax.nn.sigmoid(b_o) * jax.nn.silu(b_z)

    return jnp.concatenate([y_a, y_b], axis=-1), (hg_f, hg_b, ml_f, ml_b)


def _fwd_setup_inputs(seed: int = 0) -> dict:
    key = jax.random.key(seed)
    ks = jax.random.split(key, 20)
    f32 = jnp.float32
    x = jax.random.normal(ks[0], (BATCH, SEQ, D_MODEL), f32)
    c = jax.random.normal(ks[1], (BATCH, D_MODEL), f32)
    ctx = jax.random.normal(ks[2], (BATCH, CTX_LEN, D_MODEL), f32)
    c_ctx = jax.random.normal(ks[3], (D_MODEL,), f32)
    w_mod = jax.random.normal(ks[4], (DEPTH, D_MODEL, 3 * D_MODEL), f32) * (0.5 * D_MODEL ** -0.5)
    b_mod = jax.random.normal(ks[5], (DEPTH, 3 * D_MODEL), f32) * 0.02
    w_in = jax.random.normal(ks[6], (DEPTH, D_MODEL, N_IN), f32) * (D_MODEL ** -0.5)
    conv_w = jax.random.normal(ks[7], (DEPTH, CONV_K, CONV_K, 2 * W_B), f32) * (1.0 / CONV_K)
    conv_b = jax.random.normal(ks[8], (DEPTH, 2 * W_B), f32) * 0.02
    hg_lb = jax.random.normal(ks[9], (2, DEPTH + 1, W_A), f32) * 0.1
    ig_b = jax.random.normal(ks[10], (DEPTH, 2, ML_HEADS), f32) * 0.1
    fg_b = jnp.linspace(3.0, 6.0, ML_HEADS, dtype=f32)[None, None, :] + 0.1 * jax.random.normal(ks[11], (DEPTH, 2, ML_HEADS), f32)
    ml_gate_b = jnp.concatenate([ig_b, fg_b], axis=1)
    hg_norm_w = 1.0 + 0.02 * jax.random.normal(ks[12], (DEPTH, W_A), f32)
    ml_norm_w = 1.0 + 0.02 * jax.random.normal(ks[13], (DEPTH, W_B), f32)
    w_out = jax.random.normal(ks[14], (DEPTH, D_INNER, D_MODEL), f32) * (BETA * D_INNER ** -0.5)
    ln_g = 1.0 + 0.02 * jax.random.normal(ks[15], (DEPTH, D_MODEL), f32)
    ln_b = 0.02 * jax.random.normal(ks[16], (DEPTH, D_MODEL), f32)
    return {"x": x, "c": c, "ctx": ctx, "c_ctx": c_ctx, "w_mod": w_mod, "b_mod": b_mod, "w_in": w_in,
            "conv_w": conv_w, "conv_b": conv_b, "hg_lb": hg_lb, "ml_gate_b": ml_gate_b,
            "hg_norm_w": hg_norm_w, "ml_norm_w": ml_norm_w, "w_out": w_out, "ln_g": ln_g, "ln_b": ln_b}


def _fwd_reference(x, c, ctx, c_ctx, w_mod, b_mod, w_in, conv_w, conv_b, hg_lb, ml_gate_b,
              hg_norm_w, ml_norm_w, w_out, ln_g, ln_b):
    lower = jnp.cumsum(jax.nn.softmax(hg_lb.astype(jnp.float32), axis=1), axis=1)
    states0 = zero_states(ctx.shape[0])
    for layer in range(DEPTH):
        mod_x = jax.nn.silu(c) @ w_mod[layer] + b_mod[layer]
        mod_c = jax.nn.silu(c_ctx) @ w_mod[layer] + b_mod[layer]
        shift_x, scale_x, gate_x = jnp.split(mod_x[:, None, :], 3, axis=-1)
        shift_c, scale_c, gate_c = jnp.split(mod_c, 3, axis=-1)
        lp = (lower[:, layer], conv_w[layer], conv_b[layer], ml_gate_b[layer], hg_norm_w[layer], ml_norm_w[layer])
        u_ctx = modulate(ctx, shift_c, scale_c) @ w_in[layer]
        y_ctx, ctx_states = mix(u_ctx, False, states0, *lp)
        u_x = modulate(x, shift_x, scale_x) @ w_in[layer]
        y_x, _ = mix(u_x, True, ctx_states, *lp)
        x = layer_norm(ALPHA * x + gate_x * (y_x.astype(x.dtype) @ w_out[layer]), ln_g[layer], ln_b[layer])
        if layer < DEPTH - 1:
            ctx = layer_norm(ALPHA * ctx + gate_c * (y_ctx.astype(ctx.dtype) @ w_out[layer]), ln_g[layer], ln_b[layer])
    return x


import jax as _jax
import jax.numpy as _jnp

TWIN_FORMAT = 'train_step'
FWD_PARAMS = ['x', 'c', 'ctx', 'c_ctx', 'w_mod', 'b_mod', 'w_in', 'conv_w', 'conv_b', 'hg_lb', 'ml_gate_b', 'hg_norm_w', 'ml_norm_w', 'w_out', 'ln_g', 'ln_b']
TWIN_WEIGHTS = ['c_ctx', 'w_mod', 'b_mod', 'w_in', 'conv_w', 'conv_b', 'hg_lb', 'ml_gate_b', 'hg_norm_w', 'ml_norm_w', 'w_out', 'ln_g', 'ln_b']
TWIN_DIFF_INPUT = 'x'
TWIN_INPUTS = ['x', 'c', 'ctx', 'c_ctx', 'w_mod', 'b_mod', 'w_in', 'conv_w', 'conv_b', 'hg_lb', 'ml_gate_b', 'hg_norm_w', 'ml_norm_w', 'w_out', 'ln_g', 'ln_b', 'loss_target', 'm_c_ctx', 'm_w_mod', 'm_b_mod', 'm_w_in', 'm_conv_w', 'm_conv_b', 'm_hg_lb', 'm_ml_gate_b', 'm_hg_norm_w', 'm_ml_norm_w', 'm_w_out', 'm_ln_g', 'm_ln_b', 'v_c_ctx', 'v_w_mod', 'v_b_mod', 'v_w_in', 'v_conv_w', 'v_conv_b', 'v_hg_lb', 'v_ml_gate_b', 'v_hg_norm_w', 'v_ml_norm_w', 'v_w_out', 'v_ln_g', 'v_ln_b']
TWIN_OUTPUTS = ['loss', 'grad_x', 'grad_c_ctx', 'grad_w_mod', 'grad_b_mod', 'grad_w_in', 'grad_conv_w', 'grad_conv_b', 'grad_hg_lb', 'grad_ml_gate_b', 'grad_hg_norm_w', 'grad_ml_norm_w', 'grad_w_out', 'grad_ln_g', 'grad_ln_b', 'delta_c_ctx', 'delta_w_mod', 'delta_b_mod', 'delta_w_in', 'delta_conv_w', 'delta_conv_b', 'delta_hg_lb', 'delta_ml_gate_b', 'delta_hg_norm_w', 'delta_ml_norm_w', 'delta_w_out', 'delta_ln_g', 'delta_ln_b', 'new_m_c_ctx', 'new_m_w_mod', 'new_m_b_mod', 'new_m_w_in', 'new_m_conv_w', 'new_m_conv_b', 'new_m_hg_lb', 'new_m_ml_gate_b', 'new_m_hg_norm_w', 'new_m_ml_norm_w', 'new_m_w_out', 'new_m_ln_g', 'new_m_ln_b', 'new_v_c_ctx', 'new_v_w_mod', 'new_v_b_mod', 'new_v_w_in', 'new_v_conv_w', 'new_v_conv_b', 'new_v_hg_lb', 'new_v_ml_gate_b', 'new_v_hg_norm_w', 'new_v_ml_norm_w', 'new_v_w_out', 'new_v_ln_g', 'new_v_ln_b']
TWIN_LEAF_KINDS = {'loss': 'loss', 'grad_x': 'grad_x', 'grad_c_ctx': 'grad_w', 'grad_w_mod': 'grad_w', 'grad_b_mod': 'grad_w', 'grad_w_in': 'grad_w', 'grad_conv_w': 'grad_w', 'grad_conv_b': 'grad_w', 'grad_hg_lb': 'grad_w', 'grad_ml_gate_b': 'grad_w', 'grad_hg_norm_w': 'grad_w', 'grad_ml_norm_w': 'grad_w', 'grad_w_out': 'grad_w', 'grad_ln_g': 'grad_w', 'grad_ln_b': 'grad_w', 'delta_c_ctx': 'delta_w', 'delta_w_mod': 'delta_w', 'delta_b_mod': 'delta_w', 'delta_w_in': 'delta_w', 'delta_conv_w': 'delta_w', 'delta_conv_b': 'delta_w', 'delta_hg_lb': 'delta_w', 'delta_ml_gate_b': 'delta_w', 'delta_hg_norm_w': 'delta_w', 'delta_ml_norm_w': 'delta_w', 'delta_w_out': 'delta_w', 'delta_ln_g': 'delta_w', 'delta_ln_b': 'delta_w', 'new_m_c_ctx': 'new_m', 'new_m_w_mod': 'new_m', 'new_m_b_mod': 'new_m', 'new_m_w_in': 'new_m', 'new_m_conv_w': 'new_m', 'new_m_conv_b': 'new_m', 'new_m_hg_lb': 'new_m', 'new_m_ml_gate_b': 'new_m', 'new_m_hg_norm_w': 'new_m', 'new_m_ml_norm_w': 'new_m', 'new_m_w_out': 'new_m', 'new_m_ln_g': 'new_m', 'new_m_ln_b': 'new_m', 'new_v_c_ctx': 'new_v', 'new_v_w_mod': 'new_v', 'new_v_b_mod': 'new_v', 'new_v_w_in': 'new_v', 'new_v_conv_w': 'new_v', 'new_v_conv_b': 'new_v', 'new_v_hg_lb': 'new_v', 'new_v_ml_gate_b': 'new_v', 'new_v_hg_norm_w': 'new_v', 'new_v_ml_norm_w': 'new_v', 'new_v_w_out': 'new_v', 'new_v_ln_g': 'new_v', 'new_v_ln_b': 'new_v'}


def _forward(args):
    return _fwd_reference(*[args[k] for k in FWD_PARAMS])


def _output_shape():
    def fwd():
        inp = _fwd_setup_inputs(0)
        return _fwd_reference(*[inp[k] for k in FWD_PARAMS])
    out = _jax.eval_shape(fwd)
    return out.shape, out.dtype

N_MICROBATCH = 1
ADAM_LR = 0.001
ADAM_B1 = 0.9
ADAM_B2 = 0.999
ADAM_EPS = 1e-08
ADAM_WD = 0.01
ADAM_STEP = 10
PER_EXAMPLE_BATCH_AXIS = {'x': 0, 'c': 0, 'ctx': 0, 'loss_target': 0}
SHARED_INPUTS = []
_WEIGHT_DTYPES = {'c_ctx': _jnp.float32, 'w_mod': _jnp.float32, 'b_mod': _jnp.float32, 'w_in': _jnp.float32, 'conv_w': _jnp.float32, 'conv_b': _jnp.float32, 'hg_lb': _jnp.float32, 'ml_gate_b': _jnp.float32, 'hg_norm_w': _jnp.float32, 'ml_norm_w': _jnp.float32, 'w_out': _jnp.float32, 'ln_g': _jnp.float32, 'ln_b': _jnp.float32}
MOMENT_SCALE = {'c_ctx': 1.077679e-03, 'w_mod': 1.100705e-02, 'b_mod': 1.861943e-02, 'w_in': 4.730752e-03, 'conv_w': 1.404237e-03, 'conv_b': 1.469775e-03, 'hg_lb': 4.090843e-04, 'ml_gate_b': 1.274546e-02, 'hg_norm_w': 9.216450e-03, 'ml_norm_w': 5.132264e-03, 'w_out': 1.224268e-02, 'ln_g': 1.597951e+01, 'ln_b': 3.108284e-01}


def _to_microbatches(a, axis):
    t = _jnp.moveaxis(a, axis, 0)
    t = t.reshape((N_MICROBATCH, t.shape[0] // N_MICROBATCH) + t.shape[1:])
    return _jnp.moveaxis(t, 1, axis + 1)


def setup_inputs(seed: int = 0) -> dict:
    inp = _fwd_setup_inputs(seed)
    key = _jax.random.fold_in(_jax.random.key(seed), 7919)
    shape, _ = _output_shape()
    out = dict(inp)
    out["loss_target"] = _jax.random.normal(_jax.random.fold_in(key, 0), shape, _jnp.float32)
    for i, name in enumerate(TWIN_WEIGHTS):
        w = inp[name].astype(_jnp.float32)
        if MOMENT_SCALE is None:
            s = _jnp.sqrt(_jnp.mean(_jnp.square(w)) + 1e-30)
        else:
            s = MOMENT_SCALE[name]
        km, kv = _jax.random.split(_jax.random.fold_in(key, i + 1))
        out[name] = w
        out["m_" + name] = s * _jax.random.normal(km, w.shape, _jnp.float32)
        out["v_" + name] = (s * s) * _jax.random.uniform(kv, w.shape, _jnp.float32, 0.5, 1.5)
    if N_MICROBATCH > 1:
        for name, axis in PER_EXAMPLE_BATCH_AXIS.items():
            out[name] = _to_microbatches(out[name], axis)
    return {'x': out['x'], 'c': out['c'], 'ctx': out['ctx'], 'c_ctx': out['c_ctx'], 'w_mod': out['w_mod'], 'b_mod': out['b_mod'], 'w_in': out['w_in'], 'conv_w': out['conv_w'], 'conv_b': out['conv_b'], 'hg_lb': out['hg_lb'], 'ml_gate_b': out['ml_gate_b'], 'hg_norm_w': out['hg_norm_w'], 'ml_norm_w': out['ml_norm_w'], 'w_out': out['w_out'], 'ln_g': out['ln_g'], 'ln_b': out['ln_b'], 'loss_target': out['loss_target'], 'm_c_ctx': out['m_c_ctx'], 'm_w_mod': out['m_w_mod'], 'm_b_mod': out['m_b_mod'], 'm_w_in': out['m_w_in'], 'm_conv_w': out['m_conv_w'], 'm_conv_b': out['m_conv_b'], 'm_hg_lb': out['m_hg_lb'], 'm_ml_gate_b': out['m_ml_gate_b'], 'm_hg_norm_w': out['m_hg_norm_w'], 'm_ml_norm_w': out['m_ml_norm_w'], 'm_w_out': out['m_w_out'], 'm_ln_g': out['m_ln_g'], 'm_ln_b': out['m_ln_b'], 'v_c_ctx': out['v_c_ctx'], 'v_w_mod': out['v_w_mod'], 'v_b_mod': out['v_b_mod'], 'v_w_in': out['v_w_in'], 'v_conv_w': out['v_conv_w'], 'v_conv_b': out['v_conv_b'], 'v_hg_lb': out['v_hg_lb'], 'v_ml_gate_b': out['v_ml_gate_b'], 'v_hg_norm_w': out['v_hg_norm_w'], 'v_ml_norm_w': out['v_ml_norm_w'], 'v_w_out': out['v_w_out'], 'v_ln_g': out['v_ln_g'], 'v_ln_b': out['v_ln_b']}


def _loss(weights, diff, rest, loss_target):
    with _jax.named_scope("forward"):
        args = {**rest, TWIN_DIFF_INPUT: diff, **{k: w.astype(_WEIGHT_DTYPES[k]) for k, w in weights.items()}}
        y = _forward(args)
    with _jax.named_scope("loss_head"):
        err = _jnp.square(y.astype(_jnp.float32) - loss_target)
        return 0.5 * _jnp.sum(_jnp.mean(err, axis=-1)) if err.ndim else 0.5 * err


def _adamw(w, g, m, v):
    m = ADAM_B1 * m + (1.0 - ADAM_B1) * g
    v = ADAM_B2 * v + (1.0 - ADAM_B2) * _jnp.square(g)
    m_hat = m / (1.0 - ADAM_B1 ** ADAM_STEP)
    v_hat = v / (1.0 - ADAM_B2 ** ADAM_STEP)
    delta = -ADAM_LR * (m_hat / (_jnp.sqrt(v_hat) + ADAM_EPS) + ADAM_WD * w)
    return delta, m, v


def reference(x, c, ctx, c_ctx, w_mod, b_mod, w_in, conv_w, conv_b, hg_lb, ml_gate_b, hg_norm_w, ml_norm_w, w_out, ln_g, ln_b, loss_target, m_c_ctx, m_w_mod, m_b_mod, m_w_in, m_conv_w, m_conv_b, m_hg_lb, m_ml_gate_b, m_hg_norm_w, m_ml_norm_w, m_w_out, m_ln_g, m_ln_b, v_c_ctx, v_w_mod, v_b_mod, v_w_in, v_conv_w, v_conv_b, v_hg_lb, v_ml_gate_b, v_hg_norm_w, v_ml_norm_w, v_w_out, v_ln_g, v_ln_b):
    given = dict(x=x, c=c, ctx=ctx, c_ctx=c_ctx, w_mod=w_mod, b_mod=b_mod, w_in=w_in, conv_w=conv_w, conv_b=conv_b, hg_lb=hg_lb, ml_gate_b=ml_gate_b, hg_norm_w=hg_norm_w, ml_norm_w=ml_norm_w, w_out=w_out, ln_g=ln_g, ln_b=ln_b, loss_target=loss_target, m_c_ctx=m_c_ctx, m_w_mod=m_w_mod, m_b_mod=m_b_mod, m_w_in=m_w_in, m_conv_w=m_conv_w, m_conv_b=m_conv_b, m_hg_lb=m_hg_lb, m_ml_gate_b=m_ml_gate_b, m_hg_norm_w=m_hg_norm_w, m_ml_norm_w=m_ml_norm_w, m_w_out=m_w_out, m_ln_g=m_ln_g, m_ln_b=m_ln_b, v_c_ctx=v_c_ctx, v_w_mod=v_w_mod, v_b_mod=v_b_mod, v_w_in=v_w_in, v_conv_w=v_conv_w, v_conv_b=v_conv_b, v_hg_lb=v_hg_lb, v_ml_gate_b=v_ml_gate_b, v_hg_norm_w=v_hg_norm_w, v_ml_norm_w=v_ml_norm_w, v_w_out=v_w_out, v_ln_g=v_ln_g, v_ln_b=v_ln_b)
    weights = {n: given[n] for n in TWIN_WEIGHTS}
    shared = {n: given[n] for n in SHARED_INPUTS}
    per_example = {n: given[n] for n in ['x', 'c', 'ctx']}
    grad_fn = _jax.value_and_grad(_loss, argnums=(0, 1))

    def one_microbatch(ex, loss_target):
        ex = dict(ex)
        diff = ex.pop(TWIN_DIFF_INPUT)
        return grad_fn(weights, diff, {**shared, **ex}, loss_target)

    if N_MICROBATCH == 1:
        loss, (grad_w, grad_x) = one_microbatch(per_example, given["loss_target"])
    else:
        def body(carry, xs):
            loss_sum, grad_sum = carry
            l_k, (gw_k, gx_k) = one_microbatch(xs[0], xs[1])
            with _jax.named_scope("update"):
                return (loss_sum + l_k, _jax.tree.map(_jnp.add, grad_sum, gw_k)), gx_k

        init = (_jnp.zeros((), _jnp.float32), _jax.tree.map(_jnp.zeros_like, weights))
        (loss, grad_w), grad_x = _jax.lax.scan(body, init, (per_example, given["loss_target"]))
    with _jax.named_scope("update"):
        delta_w, new_m, new_v = {}, {}, {}
        for n in TWIN_WEIGHTS:
            delta_w[n], new_m[n], new_v[n] = _adamw(weights[n], grad_w[n], given["m_" + n], given["v_" + n])
    return (loss, grad_x, *[grad_w[n] for n in TWIN_WEIGHTS], *[delta_w[n] for n in TWIN_WEIGHTS],
            *[new_m[n] for n in TWIN_WEIGHTS], *[new_v[n] for n in TWIN_WEIGHTS])
```

```python
import functools

import jax
import jax.numpy as jnp
from jax import lax
from jax.experimental import pallas as pl
from jax.experimental.pallas import tpu as pltpu

F32 = jnp.float32
BF16 = jnp.bfloat16

D_MODEL = 2048
W_A = 1024
W_B = 1024
HG_HEADS = 8
HG_D = 128
ML_HEADS = 4
ML_D = 256
CHUNK = 64
N_IN = 10256
LANE = 128
N_U = 81 * LANE
N_DEV = 8
ALPHA = 2.0 ** 0.25
LN_EPS = 1e-5
NORM_EPS = 1e-6
ADAM_LR, ADAM_B1, ADAM_B2, ADAM_EPS, ADAM_WD, ADAM_STEP = 0.001, 0.9, 0.999, 1e-08, 0.01, 10
VMEM_CAP = 60 * 1024 * 1024

SEG_AZ, SEG_BO, SEG_BZ, SEG_AFF, SEG_AQ, SEG_AI, SEG_AFB, SEG_BV = range(8)
BLK_GATE = 64
BLK_QK = 65
_REF_SPLITS = dict(a_q=0, a_ff=1024, a_fb=2048, a_i=3072, a_z=4096, b_qk=5120, b_v=7168, b_o=8192, b_z=9216, b_g=10240)
_U_ORDER = (("a_z", 1024), ("b_o", 1024), ("b_z", 1024), ("a_ff", 1024), ("a_q", 1024), ("a_i", 1024),
            ("a_fb", 1024), ("b_v", 1024), ("b_g", 16), ("pad", 112), ("b_qk", 2048))

MESH = pl.DeviceIdType.MESH


def _vmem(nbytes):
    return pltpu.CompilerParams(vmem_limit_bytes=int(min(VMEM_CAP, max(nbytes, 16 * 1024 * 1024))))


def _sigmoid(x):
    return 1.0 / (1.0 + jnp.exp(-x))


def _silu(x):
    return x * _sigmoid(x)


def _dsilu(x):
    s = _sigmoid(x)
    return s * (1.0 + x * (1.0 - s))


def _bdot(a, b, dims):
    return lax.dot_general(a.astype(BF16), b.astype(BF16), (dims, ((), ())), preferred_element_type=F32)


def _nn(a, b):
    return _bdot(a, b, ((1,), (0,)))


def _nt(a, b):
    return _bdot(a, b, ((1,), (1,)))


def _tn(a, b):
    return _bdot(a, b, ((0,), (0,)))


def _exact_nn(a, b):
    return lax.dot_general(a, b, (((1,), (0,)), ((), ())), precision=lax.Precision.HIGHEST,
                           preferred_element_type=F32)


def _exact_tn(a, b):
    return lax.dot_general(a, b, (((0,), (0,)), ((), ())), precision=lax.Precision.HIGHEST,
                           preferred_element_type=F32)


def _tri(rev):
    t = lax.broadcasted_iota(jnp.int32, (CHUNK, CHUNK), 0)
    s = lax.broadcasted_iota(jnp.int32, (CHUNK, CHUNK), 1)
    return (s >= t) if rev else (s <= t)


def _eye():
    t = lax.broadcasted_iota(jnp.int32, (CHUNK, CHUNK), 0)
    s = lax.broadcasted_iota(jnp.int32, (CHUNK, CHUNK), 1)
    return (s == t).astype(F32)


def _row_to_col(row):
    return jnp.sum(_eye() * row, axis=1, keepdims=True)


def _last_onehot(rev):
    t = lax.broadcasted_iota(jnp.int32, (CHUNK, 1), 0)
    return (t == (0 if rev else CHUNK - 1)).astype(F32)


def _hg_common(zq, zf, lb, rev):
    q = _silu(zq)
    sg = _sigmoid(zf)
    f = lb + (1.0 - lb) * sg
    g = jnp.log(f)
    k = 1.0 - f
    tri = _tri(rev).astype(F32)
    b = _exact_nn(tri, g)
    b_last = jnp.sum(g, axis=0, keepdims=True)
    r = b[CHUNK // 2:CHUNK // 2 + 1, :]
    q_t = q * jnp.exp(b - r)
    k_t = k * jnp.exp(r - b)
    q_s = q * jnp.exp(b)
    k_h = k * jnp.exp(b_last - b)
    return q, sg, f, k, tri, b, b_last, q_t, k_t, q_s, k_h


def hg_chunk_fwd(zq, zf, v, lb, st, rev):
    q, sg, f, k, tri, b, b_last, q_t, k_t, q_s, k_h = _hg_common(zq, zf, lb, rev)
    a = tri * _nt(q_t, k_t)
    o = _nt(q_s, st) + _nn(a, v)
    st_new = st * jnp.exp(b_last) + _tn(v, k_h)
    return o, st_new


def hg_chunk_bwd(zq, zf, v, lb, st, do, dst_new, rev):
    q, sg, f, k, tri, b, b_last, q_t, k_t, q_s, k_h = _hg_common(zq, zf, lb, rev)
    a = tri * _nt(q_t, k_t)
    da = tri * _nt(do, v)
    dv = _tn(a, do) + _nt(k_h, dst_new)
    dq_s = _nn(do, st)
    dq_t = _exact_nn(da, k_t)
    dk_t = _exact_tn(da, q_t)
    dk_h = _nn(v, dst_new)
    e_last = jnp.exp(b_last)
    dst = dst_new * e_last + _tn(do, q_s)
    r = b[CHUNK // 2:CHUNK // 2 + 1, :]
    dq = dq_s * jnp.exp(b) + dq_t * jnp.exp(b - r)
    dk = dk_t * jnp.exp(r - b) + dk_h * jnp.exp(b_last - b)
    db = q * dq - k * dk
    d_all = jnp.sum(dk_h * k_h, axis=0, keepdims=True) + e_last * jnp.sum(dst_new * st, axis=0, keepdims=True)
    dg = _exact_tn(tri, db) + d_all
    dzq = dq * _dsilu(zq)
    df = dg / f - dk
    dzf = df * (1.0 - lb) * sg * (1.0 - sg)
    dlb = jnp.sum(df * (1.0 - sg), axis=0, keepdims=True)
    return dzq, dzf, dv, dlb, dst


def _log_sigmoid(x):
    return jnp.minimum(x, 0.0) - jnp.log(1.0 + jnp.exp(-jnp.abs(x)))


def _ml_forward_parts(qp, kp, v, gi_c, gi_r, gf_c, gf_r, c, n, m, rev):
    q = _silu(qp)
    k = _silu(kp) * (ML_D ** -0.5)
    lf_c = _log_sigmoid(gf_c)
    lf_r = _log_sigmoid(gf_r)
    tri_b = _tri(rev)
    tri = tri_b.astype(F32)
    tri_t = _tri(not rev).astype(F32)
    b_c = jnp.sum(tri * lf_r, axis=1, keepdims=True)
    b_r = jnp.sum(tri_t * lf_c, axis=0, keepdims=True)
    log_w = jnp.where(tri_b, b_c - b_r + gi_r, -jnp.inf)
    m_inter = b_c + m
    m_t = jnp.maximum(m_inter, jnp.max(log_w, axis=1, keepdims=True))
    a = jnp.exp(m_inter - m_t)
    p = jnp.exp(log_w - m_t)
    qk = _nt(q, k)
    w = p * qk
    qc = _nt(q, c)
    qn = jnp.sum(q * n, axis=1, keepdims=True)
    num = a * qc + _nn(w, v)
    den = a * qn + jnp.sum(w, axis=1, keepdims=True)
    floor = jnp.exp(-m_t)
    rinv = 1.0 / jnp.maximum(jnp.abs(den), floor)
    e_last = _last_onehot(rev)
    m_new = jnp.sum(m_t * e_last, axis=0, keepdims=True)
    b_last = jnp.sum(b_c * e_last, axis=0, keepdims=True)
    ws = jnp.exp(b_last - b_c + gi_c - m_new)
    decay = jnp.exp(b_last + m - m_new)
    return dict(q=q, k=k, tri=tri, a=a, p=p, qk=qk, w=w, qc=qc, qn=qn, num=num, den=den, floor=floor,
                rinv=rinv, m_new=m_new, ws=ws, decay=decay, gf_c=gf_c)


def ml_chunk_fwd(qp, kp, v, gi_c, gi_r, gf_c, gf_r, c, n, m, rev):
    f = _ml_forward_parts(qp, kp, v, gi_c, gi_r, gf_c, gf_r, c, n, m, rev)
    h = f["num"] * f["rinv"]
    c_new = f["decay"] * c + _tn(f["ws"] * v, f["k"])
    n_new = f["decay"] * n + jnp.sum(f["ws"] * f["k"], axis=0, keepdims=True)
    return h, c_new, n_new, f["m_new"]


def ml_chunk_bwd(qp, kp, v, gi_c, gi_r, gf_c, gf_r, c, n, m, dh, dc_new, dn_new, rev):
    f = _ml_forward_parts(qp, kp, v, gi_c, gi_r, gf_c, gf_r, c, n, m, rev)
    q, k, a, p, w, ws, decay, rinv = f["q"], f["k"], f["a"], f["p"], f["w"], f["ws"], f["decay"], f["rinv"]
    h = f["num"] * rinv
    dnum = dh * rinv
    signed_live = jnp.where(jnp.abs(f["den"]) > f["floor"], jnp.where(f["den"] >= 0.0, 1.0, -1.0), 0.0)
    dden = -jnp.sum(dh * h, axis=1, keepdims=True) * rinv * signed_live
    dw = _nt(dnum, v) + dden
    dqk = dw * p
    e = dw * w
    adn = a * dnum
    add = a * dden
    dv = _tn(w, dnum) + ws * _nt(k, dc_new)
    dq = _nn(dqk, k) + _nn(adn, c) + add * n
    vdc = _nn(v, dc_new)
    dk = _tn(dqk, q) + ws * vdc + ws * dn_new
    alpha = (jnp.sum(dnum * f["qc"], axis=1, keepdims=True) + dden * f["qn"]) * a
    omega = (jnp.sum(vdc * k, axis=1, keepdims=True) + jnp.sum(k * dn_new, axis=1, keepdims=True)) * ws
    delta = decay * (jnp.sum(jnp.sum(dc_new * c, axis=1, keepdims=True), axis=0, keepdims=True)
                     + jnp.sum(dn_new * n, axis=1, keepdims=True))
    dc = decay * dc_new + _tn(adn, q)
    dn = decay * dn_new + jnp.sum(add * q, axis=0, keepdims=True)
    e_rows = jnp.sum(e, axis=1, keepdims=True)
    e_cols = _row_to_col(jnp.sum(e, axis=0, keepdims=True))
    dgi = e_cols + omega
    db = e_rows + alpha - e_cols - omega
    tail = jnp.sum(omega, axis=0, keepdims=True) + delta
    dlf = _row_to_col(jnp.sum(f["tri"] * db, axis=0, keepdims=True)) + tail
    dgf = dlf * (1.0 - _sigmoid(f["gf_c"]))
    dqp = dq * _dsilu(qp)
    dkp = dk * (ML_D ** -0.5) * _dsilu(kp)
    return dqp, dkp, dv, dgi, dgf, dc, dn


def _pick(n, prefs):
    for p in prefs:
        if n % p == 0:
            return p
    raise ValueError(f"no tile for {n} among {prefs}")


def _mm(a, b, mode, out_dtype, tm, tn, tk, name):
    if mode == "nn":
        (m, k), (k2, n) = a.shape, b.shape
    elif mode == "nt":
        (m, k), (n, k2) = a.shape, b.shape
    else:
        (k, m), (k2, n) = a.shape, b.shape
    assert k == k2 and m % tm == 0 and n % tn == 0 and k % tk == 0, (a.shape, b.shape, mode, tm, tn, tk)
    nk = k // tk
    dims = {"nn": ((1,), (0,)), "nt": ((1,), (1,)), "tn": ((0,), (0,))}[mode]
    a_spec = (pl.BlockSpec((tk, tm), lambda j, i, kk: (kk, i)) if mode == "tn"
              else pl.BlockSpec((tm, tk), lambda j, i, kk: (i, kk)))
    b_spec = (pl.BlockSpec((tn, tk), lambda j, i, kk: (j, kk)) if mode == "nt"
              else pl.BlockSpec((tk, tn), lambda j, i, kk: (kk, j)))

    def body(a_ref, b_ref, o_ref, acc_ref):
        kk = pl.program_id(2)
        part = lax.dot_general(a_ref[...], b_ref[...], (dims, ((), ())), preferred_element_type=F32)

        @pl.when(kk == 0)
        def _():
            acc_ref[...] = part

        @pl.when(kk > 0)
        def _():
            acc_ref[...] += part

        @pl.when(kk == nk - 1)
        def _():
            o_ref[...] = acc_ref[...].astype(o_ref.dtype)

    osz = jnp.dtype(out_dtype).itemsize
    need = 2 * (tm * tk * a.dtype.itemsize + tk * tn * b.dtype.itemsize + tm * tn * osz) + 2 * tm * tn * 4
    return pl.pallas_call(
        body, name=name, grid=(n // tn, m // tm, nk),
        in_specs=[a_spec, b_spec], out_specs=pl.BlockSpec((tm, tn), lambda j, i, kk: (i, j)),
        out_shape=jax.ShapeDtypeStruct((m, n), out_dtype), scratch_shapes=[pltpu.VMEM((tm, tn), F32)],
        compiler_params=_vmem(need + (4 << 20)),
    )(a, b)


ROWS = 256


def _ln_stats(x):
    mu = jnp.mean(x, axis=-1, keepdims=True)
    xc = x - mu
    var = jnp.mean(xc * xc, axis=-1, keepdims=True)
    rstd = lax.rsqrt(var + LN_EPS)
    return xc * rstd, rstd


def _modulate_fwd(xc_all, modp, nbc):
    tt, d = xc_all.shape

    def body(x_ref, mod_ref, o_ref):
        n, _ = _ln_stats(x_ref[...])
        o_ref[...] = (n * (1.0 + mod_ref[0, 1:2, :]) + mod_ref[0, 0:1, :]).astype(BF16)

    return pl.pallas_call(
        body, name="modulate_fwd", grid=(tt // ROWS,),
        in_specs=[pl.BlockSpec((ROWS, d), lambda i: (i, 0)),
                  pl.BlockSpec((1, 3, d), lambda i: (jnp.where(i >= nbc, 1, 0), 0, 0))],
        out_specs=pl.BlockSpec((ROWS, d), lambda i: (i, 0)),
        out_shape=jax.ShapeDtypeStruct((tt, d), BF16),
    )(xc_all, modp)


def _modulate_bwd(dh, xc_all, modp, dxa, nbc):
    tt, d = xc_all.shape
    t = dxa.shape[0]

    def body(dh_ref, x_ref, mod_ref, dxa_ref, gx_ref, sum_ref):
        i = pl.program_id(0)
        n, rstd = _ln_stats(x_ref[...])
        g = dh_ref[...]
        dn = g * (1.0 + mod_ref[0, 1:2, :])
        dx = rstd * (dn - jnp.mean(dn, axis=-1, keepdims=True) - n * jnp.mean(dn * n, axis=-1, keepdims=True))
        gx_ref[...] = dx + dxa_ref[...]
        dshift = jnp.sum(g, axis=0, keepdims=True)
        dscale = jnp.sum(g * n, axis=0, keepdims=True)

        @pl.when(i == 0)
        def _():
            sum_ref[...] = jnp.zeros_like(sum_ref)

        @pl.when(i < nbc)
        def _():
            sum_ref[0:1, :] += dshift
            sum_ref[1:2, :] += dscale

        @pl.when(i >= nbc)
        def _():
            sum_ref[2:3, :] += dshift
            sum_ref[3:4, :] += dscale

    lat = lambda i: (jnp.maximum(i - nbc, 0), 0)
    return pl.pallas_call(
        body, name="modulate_bwd", grid=(tt // ROWS,),
        in_specs=[pl.BlockSpec((ROWS, d), lambda i: (i, 0)), pl.BlockSpec((ROWS, d), lambda i: (i, 0)),
                  pl.BlockSpec((1, 3, d), lambda i: (jnp.where(i >= nbc, 1, 0), 0, 0)),
                  pl.BlockSpec((ROWS, d), lat)],
        out_specs=[pl.BlockSpec((ROWS, d), lat), pl.BlockSpec((8, d), lambda i: (0, 0))],
        out_shape=[jax.ShapeDtypeStruct((t, d), F32), jax.ShapeDtypeStruct((8, d), F32)],
    )(dh, xc_all, modp, dxa)


def _head_slices(width, n_heads):
    hd = width // n_heads
    return [slice(h * hd, (h + 1) * hd) for h in range(n_heads)]


def _post_fwd(o_f, o_b, h_f, h_b, u, hgw, mlw, nbc):
    tt = u.shape[0]
    t = tt - nbc * ROWS

    def body(of_ref, ob_ref, hf_ref, hb_ref, az_ref, bo_ref, bz_ref, hgw_ref, mlw_ref, y_ref):
        o = of_ref[...] + ob_ref[...]
        for sl in _head_slices(W_A, HG_HEADS):
            oh = o[:, sl]
            rs = lax.rsqrt(jnp.mean(oh * oh, axis=-1, keepdims=True) + NORM_EPS)
            y_ref[:, sl] = (oh * rs * hgw_ref[:, sl] * _silu(az_ref[:, sl])).astype(BF16)
        hm = hf_ref[...] + hb_ref[...]
        for sl in _head_slices(W_B, ML_HEADS):
            hh = hm[:, sl]
            mu = jnp.mean(hh, axis=-1, keepdims=True)
            hc = hh - mu
            rstd = lax.rsqrt(jnp.mean(hc * hc, axis=-1, keepdims=True) + NORM_EPS)
            out = hc * rstd * mlw_ref[:, sl] * _sigmoid(bo_ref[:, sl]) * _silu(bz_ref[:, sl])
            y_ref[:, W_A + sl.start:W_A + sl.stop] = out.astype(BF16)

    row = lambda i: (i + nbc, 0)
    seg = lambda s: pl.BlockSpec((ROWS, 1024), lambda i: (i + nbc, s))
    wspec = pl.BlockSpec((1, 1024), lambda i: (0, 0))
    return pl.pallas_call(
        body, name="post_fwd", grid=(t // ROWS,),
        in_specs=[pl.BlockSpec((ROWS, 1024), row)] * 4 + [seg(SEG_AZ), seg(SEG_BO), seg(SEG_BZ), wspec, wspec],
        out_specs=pl.BlockSpec((ROWS, 2048), lambda i: (i, 0)),
        out_shape=jax.ShapeDtypeStruct((t, 2048), BF16),
    )(o_f, o_b, h_f, h_b, u, u, u, hgw, mlw)


def _post_bwd(dy, o_f, o_b, h_f, h_b, u, hgw, mlw, nbc):
    tt = u.shape[0]

    def body(dy_ref, of_ref, ob_ref, hf_ref, hb_ref, az_ref, bo_ref, bz_ref, hgw_ref, mlw_ref,
             do_ref, dhm_ref, du_ref, sum_ref):
        i = pl.program_id(0)
        live = jnp.where(i >= nbc, 1.0, 0.0)

        @pl.when(i == 0)
        def _():
            sum_ref[...] = jnp.zeros_like(sum_ref)

        o = of_ref[...] + ob_ref[...]
        for sl in _head_slices(W_A, HG_HEADS):
            oh = o[:, sl]
            rs = lax.rsqrt(jnp.mean(oh * oh, axis=-1, keepdims=True) + NORM_EPS)
            on = oh * rs
            az = az_ref[:, sl]
            dya = dy_ref[:, sl] * live
            doa = dya * _silu(az)
            du_ref[:, sl] = (dya * on * hgw_ref[:, sl] * _dsilu(az)).astype(BF16)
            sum_ref[0:1, sl] += jnp.sum(doa * on, axis=0, keepdims=True)
            don = doa * hgw_ref[:, sl]
            do_ref[:, sl] = rs * (don - on * jnp.mean(don * on, axis=-1, keepdims=True))
        hm = hf_ref[...] + hb_ref[...]
        for sl in _head_slices(W_B, ML_HEADS):
            hh = hm[:, sl]
            mu = jnp.mean(hh, axis=-1, keepdims=True)
            hc = hh - mu
            rstd = lax.rsqrt(jnp.mean(hc * hc, axis=-1, keepdims=True) + NORM_EPS)
            hn = hc * rstd
            hw = hn * mlw_ref[:, sl]
            bo, bz = bo_ref[:, sl], bz_ref[:, sl]
            sbo, sbz = _sigmoid(bo), _silu(bz)
            dyb = dy_ref[:, W_A + sl.start:W_A + sl.stop] * live
            dhw = dyb * sbo * sbz
            du_ref[:, 1024 + sl.start:1024 + sl.stop] = (dyb * hw * sbz * sbo * (1.0 - sbo)).astype(BF16)
            du_ref[:, 2048 + sl.start:2048 + sl.stop] = (dyb * hw * sbo * _dsilu(bz)).astype(BF16)
            sum_ref[1:2, sl] += jnp.sum(dhw * hn, axis=0, keepdims=True)
            dhn = dhw * mlw_ref[:, sl]
            dhm_ref[:, sl] = rstd * (dhn - jnp.mean(dhn, axis=-1, keepdims=True)
                                     - hn * jnp.mean(dhn * hn, axis=-1, keepdims=True))

    row = lambda i: (i, 0)
    seg = lambda s: pl.BlockSpec((ROWS, 1024), lambda i: (i, s))
    wspec = pl.BlockSpec((1, 1024), lambda i: (0, 0))
    return pl.pallas_call(
        body, name="post_bwd", grid=(tt // ROWS,),
        in_specs=[pl.BlockSpec((ROWS, 2048), lambda i: (jnp.maximum(i - nbc, 0), 0))]
        + [pl.BlockSpec((ROWS, 1024), row)] * 4 + [seg(SEG_AZ), seg(SEG_BO), seg(SEG_BZ), wspec, wspec],
        out_specs=[pl.BlockSpec((ROWS, 1024), row), pl.BlockSpec((ROWS, 1024), row),
                   pl.BlockSpec((ROWS, 3072), row), pl.BlockSpec((8, 1024), lambda i: (0, 0))],
        out_shape=[jax.ShapeDtypeStruct((tt, 1024), F32), jax.ShapeDtypeStruct((tt, 1024), F32),
                   jax.ShapeDtypeStruct((tt, 3072), BF16), jax.ShapeDtypeStruct((8, 1024), F32)],
    )(dy, o_f, o_b, h_f, h_b, u, u, u, hgw, mlw)


def _final(z, xc_all, target, modp, ln_g, ln_b, nbc):
    t, d = z.shape

    def body(z_ref, x_ref, tg_ref, mod_ref, g_ref, b_ref, dz_ref, dxa_ref, sum_ref):
        i = pl.program_id(0)
        zz = z_ref[...]
        gate = mod_ref[0, 2:3, :]
        pre = ALPHA * x_ref[...] + gate * zz
        nh, rstd = _ln_stats(pre)
        err = nh * g_ref[...] + b_ref[...] - tg_ref[...]
        dxo = err * (1.0 / d)
        dnh = dxo * g_ref[...]
        dpre = rstd * (dnh - jnp.mean(dnh, axis=-1, keepdims=True) - nh * jnp.mean(dnh * nh, axis=-1, keepdims=True))
        dz_ref[...] = (gate * dpre).astype(BF16)
        dxa_ref[...] = ALPHA * dpre

        @pl.when(i == 0)
        def _():
            sum_ref[...] = jnp.zeros_like(sum_ref)

        sum_ref[0:1, :] += jnp.sum(dpre * zz, axis=0, keepdims=True)
        sum_ref[1:2, :] += jnp.sum(dxo * nh, axis=0, keepdims=True)
        sum_ref[2:3, :] += jnp.sum(dxo, axis=0, keepdims=True)
        sum_ref[3:4, :] += jnp.sum(err * err, axis=0, keepdims=True)

    row = lambda i: (i, 0)
    vec = pl.BlockSpec((1, d), lambda i: (0, 0))
    return pl.pallas_call(
        body, name="final_ln_loss", grid=(t // ROWS,),
        in_specs=[pl.BlockSpec((ROWS, d), row), pl.BlockSpec((ROWS, d), lambda i: (i + nbc, 0)),
                  pl.BlockSpec((ROWS, d), row), pl.BlockSpec((1, 3, d), lambda i: (1, 0, 0)), vec, vec],
        out_specs=[pl.BlockSpec((ROWS, d), row), pl.BlockSpec((ROWS, d), row), pl.BlockSpec((8, d), lambda i: (0, 0))],
        out_shape=[jax.ShapeDtypeStruct((t, d), BF16), jax.ShapeDtypeStruct((t, d), F32),
                   jax.ShapeDtypeStruct((8, d), F32)],
    )(z, xc_all, target, modp, ln_g, ln_b)


GRID_W = 64


def _shift(x, s, ok):
    n = x.shape[0]
    return jnp.where(ok, pltpu.roll(x, s % n, 0), 0.0)


def _grid_masks(n):
    t = lax.broadcasted_iota(jnp.int32, (n, LANE), 0)
    col = t & (GRID_W - 1)
    return dict(left=col >= 1, right=col <= GRID_W - 2, up=t >= GRID_W, down=t < n - GRID_W)


def _seq_masks(n):
    t = lax.broadcasted_iota(jnp.int32, (n, LANE), 0)
    return dict(left=t >= 1, right=t <= n - 2)


def _conv_fwd(u, w9, cb, tc):
    tt = u.shape[0]
    t = tt - tc

    def body(u_ref, w_ref, b_ref, o_ref):
        w = [w_ref[r:r + 1, :] for r in range(9)]
        xc = u_ref[0:tc, :]
        ms = _seq_masks(tc)
        o_ref[0:tc, :] = (w[3] * _shift(xc, 1, ms["left"]) + w[4] * xc + w[5] * _shift(xc, -1, ms["right"])
                          + b_ref[...])
        x = u_ref[tc:tt, :]
        mg = _grid_masks(t)
        taps = (_shift(x, 1, mg["left"]), x, _shift(x, -1, mg["right"]))
        rows = [w[3 * i] * taps[0] + w[3 * i + 1] * taps[1] + w[3 * i + 2] * taps[2] for i in range(3)]
        o_ref[tc:tt, :] = (rows[1] + _shift(rows[0], GRID_W, mg["up"]) + _shift(rows[2], -GRID_W, mg["down"])
                           + b_ref[...])

    return pl.pallas_call(
        body, name="conv_fwd", grid=(2048 // LANE,),
        in_specs=[pl.BlockSpec((tt, LANE), lambda j: (0, BLK_QK + j)), pl.BlockSpec((9, LANE), lambda j: (0, j)),
                  pl.BlockSpec((1, LANE), lambda j: (0, j))],
        out_specs=pl.BlockSpec((tt, LANE), lambda j: (0, j)),
        out_shape=jax.ShapeDtypeStruct((tt, 2048), F32),
        compiler_params=_vmem(40 * tt * LANE * 4),
    )(u, w9, cb)


def _conv_bwd(dcp, u, w9, tc):
    tt = u.shape[0]
    t = tt - tc

    def body(d_ref, u_ref, w_ref, du_ref, gw_ref, gb_ref):
        w = [w_ref[r:r + 1, :] for r in range(9)]
        csum = lambda a: jnp.sum(a, axis=0, keepdims=True)
        dc = d_ref[0:tc, :]
        xc = u_ref[0:tc, :]
        ms = _seq_masks(tc)
        du_ref[0:tc, :] = (w[3] * _shift(dc, -1, ms["right"]) + w[4] * dc + w[5] * _shift(dc, 1, ms["left"])).astype(BF16)
        gmid = [csum(dc * _shift(xc, 1, ms["left"])), csum(dc * xc), csum(dc * _shift(xc, -1, ms["right"]))]
        d = d_ref[tc:tt, :]
        x = u_ref[tc:tt, :]
        mg = _grid_masks(t)
        dtaps = (_shift(d, -1, mg["right"]), d, _shift(d, 1, mg["left"]))
        rows = [w[3 * i] * dtaps[0] + w[3 * i + 1] * dtaps[1] + w[3 * i + 2] * dtaps[2] for i in range(3)]
        du_ref[tc:tt, :] = (rows[1] + _shift(rows[0], -GRID_W, mg["down"]) + _shift(rows[2], GRID_W, mg["up"])).astype(BF16)
        xtaps = (_shift(x, 1, mg["left"]), x, _shift(x, -1, mg["right"]))
        for j in range(3):
            gw_ref[j:j + 1, :] = csum(d * _shift(xtaps[j], GRID_W, mg["up"]))
            gw_ref[3 + j:4 + j, :] = csum(d * xtaps[j]) + gmid[j]
            gw_ref[6 + j:7 + j, :] = csum(d * _shift(xtaps[j], -GRID_W, mg["down"]))
        gb_ref[...] = csum(d) + csum(dc)

    return pl.pallas_call(
        body, name="conv_bwd", grid=(2048 // LANE,),
        in_specs=[pl.BlockSpec((tt, LANE), lambda j: (0, j)), pl.BlockSpec((tt, LANE), lambda j: (0, BLK_QK + j)),
                  pl.BlockSpec((9, LANE), lambda j: (0, j))],
        out_specs=[pl.BlockSpec((tt, LANE), lambda j: (0, j)), pl.BlockSpec((9, LANE), lambda j: (0, j)),
                   pl.BlockSpec((1, LANE), lambda j: (0, j))],
        out_shape=[jax.ShapeDtypeStruct((tt, 2048), BF16), jax.ShapeDtypeStruct((9, 2048), F32),
                   jax.ShapeDtypeStruct((1, 2048), F32)],
        compiler_params=_vmem(48 * tt * LANE * 4),
    )(dcp, u, w9)


def _chunk_of(pos, ncc, nc, rev):
    if not rev:
        return pos
    return jnp.where(pos < ncc, ncc - 1 - pos, nc - 1 - (pos - ncc))


def _hgrn_fwd(u, lower_d, ncc, rev):
    tt = u.shape[0]
    nc = tt // CHUNK
    seg_f = SEG_AFB if rev else SEG_AFF
    heads = _head_slices(W_A, HG_HEADS)

    def body(zq_ref, zf_ref, v_ref, lb_ref, o_ref, hist_ref, st_ref):
        @pl.when(pl.program_id(0) == 0)
        def _():
            st_ref[...] = jnp.zeros_like(st_ref)

        for sl in heads:
            st = st_ref[sl, :]
            hist_ref[0, sl, :] = st
            o, st_new = hg_chunk_fwd(zq_ref[:, sl], zf_ref[:, sl], v_ref[:, sl], lb_ref[:, sl], st, rev)
            o_ref[:, sl] = o
            st_ref[sl, :] = st_new

    seg = lambda s: pl.BlockSpec((CHUNK, 1024), lambda j: (_chunk_of(j, ncc, nc, rev), s))
    return pl.pallas_call(
        body, name="hgrn_fwd_rev" if rev else "hgrn_fwd", grid=(nc,),
        in_specs=[seg(SEG_AQ), seg(seg_f), seg(SEG_AI), pl.BlockSpec((1, 1024), lambda j: (0, 0))],
        out_specs=[pl.BlockSpec((CHUNK, 1024), lambda j: (_chunk_of(j, ncc, nc, rev), 0)),
                   pl.BlockSpec((1, 1024, HG_D), lambda j: (_chunk_of(j, ncc, nc, rev), 0, 0))],
        out_shape=[jax.ShapeDtypeStruct((tt, 1024), F32), jax.ShapeDtypeStruct((nc, 1024, HG_D), F32)],
        scratch_shapes=[pltpu.VMEM((1024, HG_D), F32)],
    )(u, u, u, lower_d)


def _hgrn_bwd(u, lower_d, hist, do, acc, ncc, rev):
    tt = u.shape[0]
    nc = tt // CHUNK
    seg_f = SEG_AFB if rev else SEG_AFF
    heads = _head_slices(W_A, HG_HEADS)
    has_acc = acc is not None

    def body(zq_ref, zf_ref, v_ref, lb_ref, hist_ref, do_ref, *rest):
        if has_acc:
            aq_ref, av_ref = rest[:2]
            rest = rest[2:]
        dzf_ref, dzq_ref, dv_ref, dlb_ref, dst_ref = rest

        @pl.when(pl.program_id(0) == 0)
        def _():
            dst_ref[...] = jnp.zeros_like(dst_ref)
            dlb_ref[...] = jnp.zeros_like(dlb_ref)

        for sl in heads:
            dzq, dzf, dv, dlb, dst = hg_chunk_bwd(zq_ref[:, sl], zf_ref[:, sl], v_ref[:, sl], lb_ref[:, sl],
                                                  hist_ref[0, sl, :], do_ref[:, sl], dst_ref[sl, :], rev)
            dst_ref[sl, :] = dst
            dzf_ref[:, sl] = dzf.astype(BF16)
            dlb_ref[:, sl] += dlb
            if has_acc:
                dzq = dzq + aq_ref[:, sl]
                dv = dv + av_ref[:, sl]
            dzq_ref[:, sl] = dzq
            dv_ref[:, sl] = dv

    cidx = lambda j: _chunk_of(nc - 1 - j, ncc, nc, rev)
    seg = lambda s: pl.BlockSpec((CHUNK, 1024), lambda j: (cidx(j), s))
    row = pl.BlockSpec((CHUNK, 1024), lambda j: (cidx(j), 0))
    ins = [u, u, u, lower_d, hist, do] + (list(acc) if has_acc else [])
    return pl.pallas_call(
        body, name="hgrn_bwd_rev" if rev else "hgrn_bwd", grid=(nc,),
        in_specs=[seg(SEG_AQ), seg(seg_f), seg(SEG_AI), pl.BlockSpec((1, 1024), lambda j: (0, 0)),
                  pl.BlockSpec((1, 1024, HG_D), lambda j: (cidx(j), 0, 0)), row] + ([row, row] if has_acc else []),
        out_specs=[row, row, row, pl.BlockSpec((1, 1024), lambda j: (0, 0))],
        out_shape=[jax.ShapeDtypeStruct((tt, 1024), BF16), jax.ShapeDtypeStruct((tt, 1024), F32),
                   jax.ShapeDtypeStruct((tt, 1024), F32), jax.ShapeDtypeStruct((1, 1024), F32)],
        scratch_shapes=[pltpu.VMEM((1024, HG_D), F32)],
    )(*ins)


def _gate_views(gc_ref, gr_ref, bc_ref, br_ref, head, rev):
    gc = gc_ref[0] + bc_ref[...]
    gr = gr_ref[0] + br_ref[...]
    lane = lax.broadcasted_iota(jnp.int32, (1, 16), 1)
    sub = lax.broadcasted_iota(jnp.int32, (16, 1), 0)
    d = 1 if rev else 0
    ii, fi = d * ML_HEADS + head, 2 * ML_HEADS + d * ML_HEADS + head
    col = lambda idx: jnp.sum(jnp.where(lane == idx, gc, 0.0), axis=1, keepdims=True)
    row = lambda idx: jnp.sum(jnp.where(sub == idx, gr, 0.0), axis=0, keepdims=True)
    return col(ii), row(ii), col(fi), row(fi)


def _mlstm_fwd(cpre, u, gcol, grow, bias_c, bias_r, ncc, rev):
    tt = u.shape[0]
    nc = tt // CHUNK
    heads = _head_slices(W_B, ML_HEADS)

    def body(q_ref, k_ref, v_ref, gc_ref, gr_ref, bc_ref, br_ref, h_ref, ch_ref, nh_ref, mh_ref, c_ref, n_ref, m_ref):
        @pl.when(pl.program_id(0) == 0)
        def _():
            c_ref[...] = jnp.zeros_like(c_ref)
            n_ref[...] = jnp.zeros_like(n_ref)
            m_ref[...] = jnp.zeros_like(m_ref)

        ch_ref[0] = c_ref[...]
        nh_ref[0] = n_ref[...]
        mh_ref[0] = m_ref[...]
        for hd, sl in enumerate(heads):
            gi_c, gi_r, gf_c, gf_r = _gate_views(gc_ref, gr_ref, bc_ref, br_ref, hd, rev)
            h, c_new, n_new, m_new = ml_chunk_fwd(q_ref[:, sl], k_ref[:, sl], v_ref[:, sl], gi_c, gi_r, gf_c, gf_r,
                                                  c_ref[sl, :], n_ref[hd:hd + 1, :], m_ref[hd:hd + 1, 0:1], rev)
            h_ref[:, sl] = h
            c_ref[sl, :] = c_new
            n_ref[hd:hd + 1, :] = n_new
            m_ref[hd:hd + 1, :] = jnp.broadcast_to(m_new, (1, LANE))

    cidx = lambda j: _chunk_of(j, ncc, nc, rev)
    row = lambda s: pl.BlockSpec((CHUNK, 1024), lambda j: (cidx(j), s))
    st3 = lambda a, b: pl.BlockSpec((1, a, b), lambda j: (cidx(j), 0, 0))
    return pl.pallas_call(
        body, name="mlstm_fwd_rev" if rev else "mlstm_fwd", grid=(nc,),
        in_specs=[row(0), row(1), row(SEG_BV), st3(CHUNK, 16), st3(16, CHUNK),
                  pl.BlockSpec((1, 16), lambda j: (0, 0)), pl.BlockSpec((16, 1), lambda j: (0, 0))],
        out_specs=[row(0), st3(1024, ML_D), st3(8, ML_D), st3(8, LANE)],
        out_shape=[jax.ShapeDtypeStruct((tt, 1024), F32), jax.ShapeDtypeStruct((nc, 1024, ML_D), F32),
                   jax.ShapeDtypeStruct((nc, 8, ML_D), F32), jax.ShapeDtypeStruct((nc, 8, LANE), F32)],
        scratch_shapes=[pltpu.VMEM((1024, ML_D), F32), pltpu.VMEM((8, ML_D), F32), pltpu.VMEM((8, LANE), F32)],
    )(cpre, cpre, u, gcol, grow, bias_c, bias_r)


def _mlstm_bwd(cpre, u, gcol, grow, bias_c, bias_r, chist, nhist, mhist, dh, acc, ncc, rev):
    tt = u.shape[0]
    nc = tt // CHUNK
    heads = _head_slices(W_B, ML_HEADS)
    has_acc = acc is not None
    d = 1 if rev else 0

    def body(q_ref, k_ref, v_ref, gc_ref, gr_ref, bc_ref, br_ref, ch_ref, nh_ref, mh_ref, dh_ref, *rest):
        if has_acc:
            aqk_ref, av_ref, ag_ref = rest[:3]
            rest = rest[3:]
        dqk_ref, dv_ref, dg_ref, gs_ref, dc_ref, dn_ref = rest

        @pl.when(pl.program_id(0) == 0)
        def _():
            dc_ref[...] = jnp.zeros_like(dc_ref)
            dn_ref[...] = jnp.zeros_like(dn_ref)
            gs_ref[...] = jnp.zeros_like(gs_ref)

        lane = lax.broadcasted_iota(jnp.int32, (1, LANE), 1)
        dg = ag_ref[...] if has_acc else jnp.zeros((CHUNK, LANE), F32)
        for hd, sl in enumerate(heads):
            gi_c, gi_r, gf_c, gf_r = _gate_views(gc_ref, gr_ref, bc_ref, br_ref, hd, rev)
            dqp, dkp, dv, dgi, dgf, dc, dn = ml_chunk_bwd(
                q_ref[:, sl], k_ref[:, sl], v_ref[:, sl], gi_c, gi_r, gf_c, gf_r,
                ch_ref[0, sl, :], nh_ref[0, hd:hd + 1, :], mh_ref[0, hd:hd + 1, 0:1],
                dh_ref[:, sl], dc_ref[sl, :], dn_ref[hd:hd + 1, :], rev)
            dc_ref[sl, :] = dc
            dn_ref[hd:hd + 1, :] = dn
            ksl = slice(1024 + sl.start, 1024 + sl.stop)
            if has_acc:
                dqp = dqp + aqk_ref[:, sl]
                dkp = dkp + aqk_ref[:, ksl]
                dv = dv + av_ref[:, sl]
            dqk_ref[:, sl] = dqp
            dqk_ref[:, ksl] = dkp
            dv_ref[:, sl] = dv
            dg = dg + jnp.where(lane == d * ML_HEADS + hd, dgi, 0.0)
            dg = dg + jnp.where(lane == 2 * ML_HEADS + d * ML_HEADS + hd, dgf, 0.0)
        dg_ref[...] = dg
        gs_ref[...] += jnp.sum(dg, axis=0, keepdims=True)

    cidx = lambda j: _chunk_of(nc - 1 - j, ncc, nc, rev)
    row = lambda s: pl.BlockSpec((CHUNK, 1024), lambda j: (cidx(j), s))
    wide = pl.BlockSpec((CHUNK, 2048), lambda j: (cidx(j), 0))
    gate = pl.BlockSpec((CHUNK, LANE), lambda j: (cidx(j), 0))
    st3 = lambda a, b: pl.BlockSpec((1, a, b), lambda j: (cidx(j), 0, 0))
    ins = [cpre, cpre, u, gcol, grow, bias_c, bias_r, chist, nhist, mhist, dh] + (list(acc) if has_acc else [])
    return pl.pallas_call(
        body, name="mlstm_bwd_rev" if rev else "mlstm_bwd", grid=(nc,),
        in_specs=[row(0), row(1), row(SEG_BV), st3(CHUNK, 16), st3(16, CHUNK),
                  pl.BlockSpec((1, 16), lambda j: (0, 0)), pl.BlockSpec((16, 1), lambda j: (0, 0)),
                  st3(1024, ML_D), st3(8, ML_D), st3(8, LANE), row(0)] + ([wide, row(0), gate] if has_acc else []),
        out_specs=[wide, row(0), gate, pl.BlockSpec((1, LANE), lambda j: (0, 0))],
        out_shape=[jax.ShapeDtypeStruct((tt, 2048), F32), jax.ShapeDtypeStruct((tt, 1024), F32),
                   jax.ShapeDtypeStruct((tt, LANE), F32), jax.ShapeDtypeStruct((1, LANE), F32)],
        scratch_shapes=[pltpu.VMEM((1024, ML_D), F32), pltpu.VMEM((8, ML_D), F32)],
    )(*ins)


def _whole(body, out_shape, name, *args, nbytes=0):
    return pl.pallas_call(body, name=name, out_shape=out_shape, compiler_params=_vmem(nbytes))(*args)


def _mod_fwd(cs, w_cols, b_cols):
    def body(c_ref, w_ref, b_ref, o_ref):
        o_ref[...] = _exact_nn(_silu(c_ref[...]), w_ref[...]) + b_ref[...]

    return _whole(body, jax.ShapeDtypeStruct((16, w_cols.shape[1]), F32), "mod_fwd", cs, w_cols, b_cols,
                  nbytes=4 * w_cols.size * 4)


def _mod_bwd_w(cs, d9, w_cols):
    def body(c_ref, d_ref, w_ref, gw_ref, pc_ref):
        gw_ref[...] = _exact_tn(_silu(c_ref[...]), d_ref[...])
        pc = lax.dot_general(d_ref[8:16, :], w_ref[...], (((1,), (1,)), ((), ())), precision=lax.Precision.HIGHEST,
                             preferred_element_type=F32)
        row = lax.broadcasted_iota(jnp.int32, pc.shape, 0)
        pc_ref[...] = jnp.where(row == 0, pc, 0.0)

    return _whole(body, [jax.ShapeDtypeStruct(w_cols.shape, F32), jax.ShapeDtypeStruct((8, w_cols.shape[0]), F32)],
                  "mod_bwd_w", cs, d9, w_cols, nbytes=6 * w_cols.size * 4)


def _lower_fwd(lb4):
    def body(l_ref, o_ref):
        o_ref[...] = jnp.zeros_like(o_ref)
        o_ref[0:1, :] = 1.0 / (1.0 + jnp.exp(l_ref[1:2, :] - l_ref[0:1, :]))
        o_ref[1:2, :] = 1.0 / (1.0 + jnp.exp(l_ref[3:4, :] - l_ref[2:3, :]))

    return _whole(body, jax.ShapeDtypeStruct((8, lb4.shape[1]), F32), "lower_fwd", lb4)


def _reduce8(g, name):
    def body(g_ref, o_ref):
        acc = g_ref[0]
        for k in range(1, N_DEV):
            acc = acc + g_ref[k]
        o_ref[...] = acc

    return _whole(body, jax.ShapeDtypeStruct(g.shape[1:], F32), name, g, nbytes=4 * g.size * 4)


_PACK = (("dmodx", 48), ("dmodc", 48), ("gconvw", 144), ("gconvb", 16), ("dlower", 16), ("ghgw", 8), ("gmlw", 8),
         ("glng", 16), ("glnb", 16), ("losssq", 16), ("ggate", 8))
_PACK_ROWS = sum(r for _, r in _PACK)


def _pack_offsets():
    off, out = 0, {}
    for name, rows in _PACK:
        out[name] = (off, rows)
        off += rows
    return out


def _small_finish(total, p0, d_feat):
    offs = _pack_offsets()

    def body(t_ref, p_ref, gb_ref, a0_ref, a1_ref, loss_ref):
        ox, oc, ol, oq = offs["dmodx"][0], offs["dmodc"][0], offs["dlower"][0], offs["losssq"][0]
        gb_ref[...] = t_ref[ox:ox + 48, :] + t_ref[oc:oc + 48, :]
        p = p_ref[...]
        da0 = t_ref[ol:ol + 16, :] * p * (1.0 - p)
        a0_ref[...] = da0
        a1_ref[...] = -da0
        sq = t_ref[oq:oq + 16, :]
        tot = jnp.sum(jnp.sum(sq, axis=1, keepdims=True), axis=0, keepdims=True)
        loss_ref[...] = jnp.broadcast_to(tot * (0.5 / d_feat), loss_ref.shape)

    s = jax.ShapeDtypeStruct
    return _whole(body, [s((48, LANE), F32), s((16, LANE), F32), s((16, LANE), F32), s((8, LANE), F32)],
                  "small_finish", total, p0)


def _cctx_grad(parts, c_ctx8):
    def body(p_ref, c_ref, o_ref):
        acc = p_ref[0]
        for k in range(1, N_DEV):
            acc = acc + p_ref[k]
        o_ref[...] = acc * _dsilu(c_ref[...])

    return _whole(body, jax.ShapeDtypeStruct(c_ctx8.shape, F32), "cctx_grad", parts, c_ctx8)


def _adam_math(w, g, m, v):
    m = ADAM_B1 * m + (1.0 - ADAM_B1) * g
    v = ADAM_B2 * v + (1.0 - ADAM_B2) * (g * g)
    m_hat = m / (1.0 - ADAM_B1 ** ADAM_STEP)
    v_hat = v / (1.0 - ADAM_B2 ** ADAM_STEP)
    delta = -ADAM_LR * (m_hat / (jnp.sqrt(v_hat) + ADAM_EPS) + ADAM_WD * w)
    return delta, m, v


def _adamw(w, g, m, v, rows, name):
    r, c = w.shape

    def body(w_ref, g_ref, m_ref, v_ref, d_ref, mo_ref, vo_ref):
        d_ref[...], mo_ref[...], vo_ref[...] = _adam_math(w_ref[...], g_ref[...], m_ref[...], v_ref[...])

    spec = pl.BlockSpec((rows, c), lambda i: (i, 0))
    return pl.pallas_call(
        body, name=name, grid=(r // rows,), in_specs=[spec] * 4, out_specs=[spec] * 3,
        out_shape=[jax.ShapeDtypeStruct((r, c), F32)] * 3,
        compiler_params=_vmem(16 * rows * (c + LANE) * 4),
    )(w, g, m, v)


def _rs_adamw(recv, w, m, v, rows, name):
    _, r, c = recv.shape

    def body(r_ref, w_ref, m_ref, v_ref, g_ref, d_ref, mo_ref, vo_ref):
        g = r_ref[0].astype(F32)
        for k in range(1, N_DEV):
            g = g + r_ref[k].astype(F32)
        g_ref[...] = g
        d_ref[...], mo_ref[...], vo_ref[...] = _adam_math(w_ref[...], g, m_ref[...], v_ref[...])

    spec = pl.BlockSpec((rows, c), lambda i: (i, 0))
    return pl.pallas_call(
        body, name=name, grid=(r // rows,),
        in_specs=[pl.BlockSpec((N_DEV, rows, c), lambda i: (0, i, 0))] + [spec] * 3, out_specs=[spec] * 4,
        out_shape=[jax.ShapeDtypeStruct((r, c), F32)] * 4,
        compiler_params=_vmem(2 * rows * (c + LANE) * (N_DEV * 2 + 7 * 4) + (4 << 20)),
    )(recv, w, m, v)


def _position():
    return lax.axis_index("x"), lax.axis_index("y"), lax.axis_index("c")


def _all_gather(x, name):
    r, c = x.shape

    def body(x_ref, out_ref, send_sems, recv_sems, local_sem):
        px, py, pc = _position()
        me, sibling = (px, py, pc), (px, py, 1 - pc)
        chips = [(1 - px, py), (px, 1 - py), (1 - px, 1 - py)]

        def slot(qx, qy, qc):
            return out_ref.at[4 * qx + 2 * qy + qc]

        def copy(k, block, to, src=None):
            return pltpu.make_async_remote_copy(
                src_ref=slot(*block) if src is None else src, dst_ref=slot(*block),
                send_sem=send_sems.at[k], recv_sem=recv_sems.at[k], device_id=to, device_id_type=MESH)

        mine = pltpu.make_async_copy(x_ref, slot(*me), local_sem)
        mine.start()
        first = [copy(1 + j, me, (*chip, pc), src=x_ref) for j, chip in enumerate(chips)]
        first.append(copy(0, me, sibling, src=x_ref))
        for cp in first:
            cp.start()
        passed = [copy(4 + j, (*chip, pc), sibling) for j, chip in enumerate(chips)]
        for j, chip in enumerate(chips):
            copy(1 + j, (*chip, pc), me).wait_recv()
            passed[j].start()
        copy(0, sibling, me).wait_recv()
        for j, chip in enumerate(chips):
            copy(4 + j, (*chip, 1 - pc), me).wait_recv()
        for cp in first + passed:
            cp.wait_send()
        mine.wait()

    return pl.pallas_call(
        body, name=name, out_shape=jax.ShapeDtypeStruct((N_DEV, r, c), x.dtype),
        in_specs=[pl.BlockSpec(memory_space=pl.ANY)], out_specs=pl.BlockSpec(memory_space=pl.ANY),
        scratch_shapes=[pltpu.SemaphoreType.DMA((7,)), pltpu.SemaphoreType.DMA((7,)), pltpu.SemaphoreType.DMA],
    )(x)


def _all_to_all(x, name):
    _, r, c = x.shape

    def body(x_ref, out_ref, send_sems, recv_sems, local_sem):
        px, py, pc = _position()
        me = 4 * px + 2 * py + pc
        mine = pltpu.make_async_copy(x_ref.at[me], out_ref.at[me], local_sem)
        mine.start()
        sends, recvs = [], []
        for k, (fx, fy, fc) in enumerate([(1, 0, 0), (0, 1, 0), (1, 1, 0), (1, 0, 1), (0, 1, 1), (1, 1, 1), (0, 0, 1)]):
            qx, qy, qc = (1 - px if fx else px), (1 - py if fy else py), (1 - pc if fc else pc)
            peer = 4 * qx + 2 * qy + qc
            sends.append(pltpu.make_async_remote_copy(
                src_ref=x_ref.at[peer], dst_ref=out_ref.at[me], send_sem=send_sems.at[k], recv_sem=recv_sems.at[k],
                device_id=(qx, qy, qc), device_id_type=MESH))
            recvs.append(pltpu.make_async_remote_copy(
                src_ref=x_ref.at[me], dst_ref=out_ref.at[peer], send_sem=send_sems.at[k], recv_sem=recv_sems.at[k],
                device_id=(qx, qy, qc), device_id_type=MESH))
        for cp in sends:
            cp.start()
        for cp in recvs:
            cp.wait_recv()
        for cp in sends:
            cp.wait_send()
        mine.wait()

    return pl.pallas_call(
        body, name=name, out_shape=jax.ShapeDtypeStruct(x.shape, x.dtype),
        in_specs=[pl.BlockSpec(memory_space=pl.ANY)], out_specs=pl.BlockSpec(memory_space=pl.ANY),
        scratch_shapes=[pltpu.SemaphoreType.DMA((7,)), pltpu.SemaphoreType.DMA((7,)), pltpu.SemaphoreType.DMA],
    )(x)


def _to_u_order(w_ref_cols, axis):
    parts = []
    for name, width in _U_ORDER:
        if name == "pad":
            shape = list(w_ref_cols.shape)
            shape[axis] = width
            parts.append(jnp.zeros(shape, w_ref_cols.dtype))
        else:
            parts.append(lax.slice_in_dim(w_ref_cols, _REF_SPLITS[name], _REF_SPLITS[name] + width, axis=axis))
    return jnp.concatenate(parts, axis=axis)


def _from_u_order(w_u_cols, axis):
    offs, off = {}, 0
    for name, width in _U_ORDER:
        offs[name] = (off, width)
        off += width
    order = sorted((n for n in offs if n != "pad"), key=lambda n: _REF_SPLITS[n])
    return jnp.concatenate([lax.slice_in_dim(w_u_cols, offs[n][0], offs[n][0] + offs[n][1], axis=axis) for n in order],
                           axis=axis)


def _local_step(xc_all, target, modp, lower, w_u, w_o, w9, conv_b, gate_b, hgw, mlw, ln_g, ln_b, tc):
    tt = xc_all.shape[0]
    nbc, ncc, nc = tc // ROWS, tc // CHUNK, tt // CHUNK
    lower_f, lower_b = lower[0:1], lower[1:2]

    hc = _modulate_fwd(xc_all, modp, nbc)
    u = _mm(hc, w_u, "nn", F32, ROWS, 1152, D_MODEL, "mm_u")
    cpre = _conv_fwd(u, w9, conv_b, tc)
    gates = u[:, BLK_GATE * LANE:BLK_GATE * LANE + 16].reshape(nc, CHUNK, 16)
    gcol, grow = gates, jnp.swapaxes(gates, 1, 2)
    bias_c, bias_r = gate_b.reshape(1, 16), gate_b.reshape(16, 1)

    o_f, hist_f = _hgrn_fwd(u, lower_f, ncc, False)
    o_b, hist_b = _hgrn_fwd(u, lower_b, ncc, True)
    h_f, ch_f, nh_f, mh_f = _mlstm_fwd(cpre, u, gcol, grow, bias_c, bias_r, ncc, False)
    h_b, ch_b, nh_b, mh_b = _mlstm_fwd(cpre, u, gcol, grow, bias_c, bias_r, ncc, True)
    y = _post_fwd(o_f, o_b, h_f, h_b, u, hgw, mlw, nbc)
    tm = _pick(y.shape[0], (512, 256))
    z = _mm(y, w_o, "nn", F32, tm, D_MODEL, D_MODEL, "mm_z")
    dz, dxa, fsum = _final(z, xc_all, target, modp, ln_g, ln_b, nbc)

    dy = _mm(dz, w_o, "nt", F32, tm, D_MODEL, D_MODEL, "mm_dy")
    dw_o = _mm(y, dz, "tn", F32, D_MODEL, 1024, tm, "mm_dwo")
    do, dhm, du1, psum = _post_bwd(dy, o_f, o_b, h_f, h_b, u, hgw, mlw, nbc)
    dzf_f, dzq, dv_a, dlb_f = _hgrn_bwd(u, lower_f, hist_f, do, None, ncc, False)
    dzf_b, dzq, dv_a, dlb_b = _hgrn_bwd(u, lower_b, hist_b, do, (dzq, dv_a), ncc, True)
    dqk, dv_m, dg, _ = _mlstm_bwd(cpre, u, gcol, grow, bias_c, bias_r, ch_f, nh_f, mh_f, dhm, None, ncc, False)
    dqk, dv_m, dg, gsum = _mlstm_bwd(cpre, u, gcol, grow, bias_c, bias_r, ch_b, nh_b, mh_b, dhm, (dqk, dv_m, dg), ncc, True)
    du5, gconvw, gconvb = _conv_bwd(dqk, u, w9, tc)
    du = jnp.concatenate([du1, dzf_f, dzq.astype(BF16), dv_a.astype(BF16), dzf_b, dv_m.astype(BF16),
                          dg.astype(BF16), du5], axis=1)
    tmh = _pick(tt, (1088, 768, 512, 256))
    dh = _mm(du, w_u, "nt", F32, tmh, D_MODEL, 1152, "mm_dh")
    dw_u = _mm(hc, du, "tn", F32, D_MODEL, 1152, ROWS, "mm_dwu")
    gx, msum = _modulate_bwd(dh, xc_all, modp, dxa, nbc)

    zero_row = jnp.zeros((1, D_MODEL), F32)
    small = dict(
        dmodx=jnp.concatenate([msum[2:3], msum[3:4], fsum[0:1]], axis=0),
        dmodc=jnp.concatenate([msum[0:1], msum[1:2], zero_row], axis=0),
        gconvw=gconvw, gconvb=gconvb, dlower=jnp.concatenate([dlb_f, dlb_b], axis=0),
        ghgw=psum[0:1], gmlw=psum[1:2], glng=fsum[1:2], glnb=fsum[2:3], losssq=fsum[3:4],
        ggate=jnp.concatenate([gsum, jnp.zeros((7, LANE), F32)], axis=0))
    return gx, dw_u, dw_o, small


def _pack_small(small):
    return jnp.concatenate([small[name].reshape(rows, LANE) for name, rows in _PACK], axis=0)


def _flat_pad(a, rows):
    flat = a.reshape(-1)
    return jnp.pad(flat, (0, rows * LANE - flat.shape[0])).reshape(rows, LANE)


def kernel(x, c, ctx, c_ctx, w_mod, b_mod, w_in, conv_w, conv_b, hg_lb, ml_gate_b, hg_norm_w, ml_norm_w, w_out, ln_g, ln_b, loss_target, m_c_ctx, m_w_mod, m_b_mod, m_w_in, m_conv_w, m_conv_b, m_hg_lb, m_ml_gate_b, m_hg_norm_w, m_ml_norm_w, m_w_out, m_ln_g, m_ln_b, v_c_ctx, v_w_mod, v_b_mod, v_w_in, v_conv_w, v_conv_b, v_hg_lb, v_ml_gate_b, v_hg_norm_w, v_ml_norm_w, v_w_out, v_ln_g, v_ln_b):
    px, py, pc = _position()
    me = 4 * px + 2 * py + pc
    d = D_MODEL
    tc = ctx.shape[1]
    n_mod = w_mod.shape[2]
    n_wi = w_in.shape[2]
    n_cv = conv_w.shape[3]
    n_lb = hg_lb.shape[2]

    pack0 = jnp.concatenate([c.reshape(-1), conv_w.reshape(-1), hg_lb.reshape(-1)]).reshape(1, -1)
    g0 = _all_gather(pack0, "gather_small_inputs")[:, 0, :]
    c_all = g0[:, :d]
    w9 = jnp.transpose(g0[:, d:d + 9 * n_cv].reshape(N_DEV, 9, n_cv), (1, 0, 2)).reshape(9, N_DEV * n_cv)
    lb4 = jnp.transpose(g0[:, d + 9 * n_cv:].reshape(N_DEV, 4, n_lb), (1, 0, 2)).reshape(4, N_DEV * n_lb)
    lower = _lower_fwd(lb4)

    cs = jnp.concatenate([c_all, c_ctx.reshape(1, d), jnp.zeros((7, d), F32)], axis=0)
    b_cols = lax.dynamic_slice(b_mod, (0, me * n_mod), (1, n_mod))
    slab = _mod_fwd(cs, w_mod[0], b_cols)
    mod_all = jnp.transpose(_all_gather(slab, "gather_mod"), (1, 0, 2)).reshape(16, N_DEV * n_mod)
    mod_x = lax.dynamic_slice(mod_all, (me, 0), (1, 3 * d)).reshape(3, d)
    modp = jnp.stack([mod_all[8].reshape(3, d), mod_x])

    w_ref = jnp.transpose(_all_gather(w_in[0].astype(BF16), "gather_w_in"), (1, 0, 2)).reshape(d, N_DEV * n_wi)
    w_u = _to_u_order(w_ref, 1)
    w_o = _all_gather(w_out[0].astype(BF16), "gather_w_out").reshape(d, d)

    xc_all = jnp.concatenate([ctx[0], x[0]], axis=0)
    gx, dw_u, dw_o, small = _local_step(xc_all, loss_target[0], modp, lower, w_u, w_o, w9, conv_b, ml_gate_b[0],
                                        hg_norm_w, ml_norm_w, ln_g, ln_b, tc)

    dw_ref = _from_u_order(dw_u, 1).astype(BF16)
    dw_blocks = jnp.transpose(dw_ref.reshape(d, N_DEV, n_wi), (1, 0, 2))
    recv_wi = _all_to_all(dw_blocks, "scatter_dw_in")
    g_wi, d_wi, nm_wi, nv_wi = _rs_adamw(recv_wi, w_in[0], m_w_in[0], v_w_in[0], 128, "adamw_w_in")
    recv_wo = _all_to_all(dw_o.astype(BF16).reshape(N_DEV, d // N_DEV, d), "scatter_dw_out")
    g_wo, d_wo, nm_wo, nv_wo = _rs_adamw(recv_wo, w_out[0], m_w_out[0], v_w_out[0], 64, "adamw_w_out")

    packs = _all_gather(_pack_small(small), "gather_small_grads")
    total = _reduce8(packs, "reduce_small_grads")
    offs = _pack_offsets()
    piece = lambda name: total[offs[name][0]:offs[name][0] + offs[name][1]]
    g_bmod, g_lb0, g_lb1, loss8 = _small_finish(total, lower[0:2].reshape(16, LANE), float(d))

    ox = offs["dmodx"][0]
    dmodx_all = packs[:, ox:ox + 48, :].reshape(N_DEV, 3 * d)
    dmodc_tot = piece("dmodc").reshape(1, 3 * d)
    d9 = jnp.concatenate([dmodx_all, dmodc_tot, jnp.zeros((7, 3 * d), F32)], axis=0)
    d9_cols = lax.dynamic_slice(d9, (0, me * n_mod), (16, n_mod))
    g_wmod, pc_part = _mod_bwd_w(cs, d9_cols, w_mod[0])
    c_ctx8 = jnp.concatenate([c_ctx.reshape(1, d), jnp.zeros((7, d), F32)], axis=0)
    g_cctx = _cctx_grad(_all_gather(pc_part, "gather_cctx"), c_ctx8)[0]
    d_wmod, nm_wmod, nv_wmod = _adamw(w_mod[0], g_wmod, m_w_mod[0], v_w_mod[0], 256, "adamw_w_mod")

    g_convw_full = piece("gconvw").reshape(9, d)
    g_convw = lax.dynamic_slice(g_convw_full, (0, me * n_cv), (9, n_cv)).reshape(conv_w.shape)
    lb_full = jnp.stack([jnp.stack([g_lb0[0:8].reshape(-1), g_lb1[0:8].reshape(-1)]),
                         jnp.stack([g_lb0[8:16].reshape(-1), g_lb1[8:16].reshape(-1)])])
    g_hglb = lax.dynamic_slice(lb_full, (0, 0, me * n_lb), (2, 2, n_lb))
    grads = dict(
        c_ctx=g_cctx, b_mod=g_bmod.reshape(b_mod.shape), conv_w=g_convw, conv_b=piece("gconvb").reshape(conv_b.shape),
        hg_lb=g_hglb, ml_gate_b=piece("ggate")[0, :16].reshape(ml_gate_b.shape),
        hg_norm_w=piece("ghgw").reshape(hg_norm_w.shape), ml_norm_w=piece("gmlw").reshape(ml_norm_w.shape),
        ln_g=piece("glng").reshape(ln_g.shape), ln_b=piece("glnb").reshape(ln_b.shape))
    params = dict(c_ctx=(c_ctx, m_c_ctx, v_c_ctx), b_mod=(b_mod, m_b_mod, v_b_mod), conv_w=(conv_w, m_conv_w, v_conv_w),
                  conv_b=(conv_b, m_conv_b, v_conv_b), hg_lb=(hg_lb, m_hg_lb, v_hg_lb),
                  ml_gate_b=(ml_gate_b, m_ml_gate_b, v_ml_gate_b), hg_norm_w=(hg_norm_w, m_hg_norm_w, v_hg_norm_w),
                  ml_norm_w=(ml_norm_w, m_ml_norm_w, v_ml_norm_w), ln_g=(ln_g, m_ln_g, v_ln_g), ln_b=(ln_b, m_ln_b, v_ln_b))
    names = list(params)
    rows_of = {n: -(-params[n][0].size // LANE) for n in names}
    rows_tot = -(-sum(rows_of.values()) // 8) * 8
    cat = lambda arrs: jnp.concatenate(
        [_flat_pad(a, rows_of[n]) for n, a in zip(names, arrs)]
        + [jnp.ones((rows_tot - sum(rows_of.values()), LANE), F32)], axis=0)
    d_s, m_s, v_s = _adamw(cat([params[n][0] for n in names]), cat([grads[n] for n in names]),
                           cat([params[n][1] for n in names]), cat([params[n][2] for n in names]), rows_tot, "adamw_small")
    delta, new_m, new_v, off = {}, {}, {}, 0
    for n in names:
        shape, size = params[n][0].shape, params[n][0].size
        take = lambda a: a[off:off + rows_of[n]].reshape(-1)[:size].reshape(shape)
        delta[n], new_m[n], new_v[n] = take(d_s), take(m_s), take(v_s)
        off += rows_of[n]
    grads.update(w_mod=g_wmod[None], w_in=g_wi[None], w_out=g_wo[None])
    delta.update(w_mod=d_wmod[None], w_in=d_wi[None], w_out=d_wo[None])
    new_m.update(w_mod=nm_wmod[None], w_in=nm_wi[None], w_out=nm_wo[None])
    new_v.update(w_mod=nv_wmod[None], w_in=nv_wi[None], w_out=nv_wo[None])

    order = ("c_ctx", "w_mod", "b_mod", "w_in", "conv_w", "conv_b", "hg_lb", "ml_gate_b", "hg_norm_w", "ml_norm_w",
             "w_out", "ln_g", "ln_b")
    return (loss8[0, 0], gx[None], *[grads[n] for n in order], *[delta[n] for n in order],
            *[new_m[n] for n in order], *[new_v[n] for n in order])
```

```python
import functools

import jax
import jax.numpy as jnp
from jax import lax
from jax.experimental import pallas as pl
from jax.experimental.pallas import tpu as pltpu

F32 = jnp.float32
BF16 = jnp.bfloat16

D_MODEL = 2048
W_A = 1024
W_B = 1024
HG_HEADS = 8
HG_D = 128
ML_HEADS = 4
ML_D = 256
CHUNK = 64
N_IN = 10256
LANE = 128
N_U = 81 * LANE
N_DEV = 8
ALPHA = 2.0 ** 0.25
LN_EPS = 1e-5
NORM_EPS = 1e-6
ADAM_LR, ADAM_B1, ADAM_B2, ADAM_EPS, ADAM_WD, ADAM_STEP = 0.001, 0.9, 0.999, 1e-08, 0.01, 10
VMEM_CAP = 60 * 1024 * 1024

SEG_AQ, SEG_AFF, SEG_AFB, SEG_AI, SEG_AZ = range(5)
BLK_QK = 40
SEG_BV, SEG_BO, SEG_BZ = 7, 8, 9
BLK_GATE = 80

MESH = pl.DeviceIdType.MESH


def _vmem(nbytes):
    return pltpu.CompilerParams(vmem_limit_bytes=int(min(VMEM_CAP, max(nbytes, 16 * 1024 * 1024))))


def _sigmoid(x):
    return 1.0 / (1.0 + jnp.exp(-x))


def _silu(x):
    return x * _sigmoid(x)


def _dsilu(x):
    s = _sigmoid(x)
    return s * (1.0 + x * (1.0 - s))


def _bdot(a, b, dims):
    return lax.dot_general(a.astype(BF16), b.astype(BF16), (dims, ((), ())), preferred_element_type=F32)


def _nn(a, b):
    return _bdot(a, b, ((1,), (0,)))


def _nt(a, b):
    return _bdot(a, b, ((1,), (1,)))


def _tn(a, b):
    return _bdot(a, b, ((0,), (0,)))


def _exact_nn(a, b):
    return lax.dot_general(a, b, (((1,), (0,)), ((), ())), precision=lax.Precision.HIGHEST,
                           preferred_element_type=F32)


def _exact_tn(a, b):
    return lax.dot_general(a, b, (((0,), (0,)), ((), ())), precision=lax.Precision.HIGHEST,
                           preferred_element_type=F32)


def _tri(rev):
    t = lax.broadcasted_iota(jnp.int32, (CHUNK, CHUNK), 0)
    s = lax.broadcasted_iota(jnp.int32, (CHUNK, CHUNK), 1)
    return (s >= t) if rev else (s <= t)


def _eye():
    t = lax.broadcasted_iota(jnp.int32, (CHUNK, CHUNK), 0)
    s = lax.broadcasted_iota(jnp.int32, (CHUNK, CHUNK), 1)
    return (s == t).astype(F32)


def _row_to_col(row):
    return jnp.sum(_eye() * row, axis=1, keepdims=True)


def _last_onehot(rev):
    t = lax.broadcasted_iota(jnp.int32, (CHUNK, 1), 0)
    return (t == (0 if rev else CHUNK - 1)).astype(F32)


def _head_slices(width, n_heads):
    hd = width // n_heads
    return [slice(h * hd, (h + 1) * hd) for h in range(n_heads)]


def _scan_sum(x, rev):
    n = x.shape[0]
    t = lax.broadcasted_iota(jnp.int32, x.shape, 0)
    s = 1
    while s < n:
        if rev:
            x = x + jnp.where(t < n - s, pltpu.roll(x, n - s, 0), 0.0)
        else:
            x = x + jnp.where(t >= s, pltpu.roll(x, s, 0), 0.0)
        s *= 2
    return x


def _dot3(a, b, dims):
    a_hi, b_hi = a.astype(BF16), b.astype(BF16)
    a_lo, b_lo = (a - a_hi.astype(F32)).astype(BF16), (b - b_hi.astype(F32)).astype(BF16)
    dot = lambda x, y: lax.dot_general(x, y, (dims, ((), ())), preferred_element_type=F32)
    return dot(a_hi, b_hi) + (dot(a_hi, b_lo) + dot(a_lo, b_hi))


def _hg_common(zq, zf, lb, rev):
    q = _silu(zq)
    sg = _sigmoid(zf)
    f = lb + (1.0 - lb) * sg
    g = jnp.log(f)
    k = 1.0 - f
    b = _scan_sum(g, rev)
    b_last = jnp.sum(g, axis=0, keepdims=True)
    r = b[CHUNK // 2:CHUNK // 2 + 1, :]
    e_up = jnp.exp(b - r)
    e_dn = jnp.exp(r - b)
    e_b = e_up * jnp.exp(r)
    e_lb = e_dn * jnp.exp(b_last - r)
    return dict(q=q, sg=sg, f=f, k=k, e_up=e_up, e_dn=e_dn, e_b=e_b, e_lb=e_lb, e_last=jnp.exp(b_last),
                q_t=q * e_up, k_t=k * e_dn, q_s=q * e_b, k_h=k * e_lb, tri=_tri(rev).astype(F32))


def hg_chunk_fwd(zq, zf, v, lb, st, rev):
    c = _hg_common(zq, zf, lb, rev)
    hs = _head_slices(zq.shape[1], zq.shape[1] // HG_D)
    s = [_nt(c["q_t"][:, sl], c["k_t"][:, sl]) for sl in hs]
    oi = [_nt(c["q_s"][:, sl], st[sl, :]) for sl in hs]
    ds = [_tn(v[:, sl], c["k_h"][:, sl]) for sl in hs]
    oa = [_nn(c["tri"] * s_h, v[:, sl]) for s_h, sl in zip(s, hs)]
    o = jnp.concatenate([x + y for x, y in zip(oi, oa)], axis=1)
    st_new = jnp.concatenate([st[sl, :] * c["e_last"][:, sl] + d for sl, d in zip(hs, ds)], axis=0)
    return o, st_new


def hg_chunk_bwd(zq, zf, v, lb, st, do, dst_new, rev):
    c = _hg_common(zq, zf, lb, rev)
    hs = _head_slices(zq.shape[1], zq.shape[1] // HG_D)
    tri, q_t, k_t, q_s, k_h = c["tri"], c["q_t"], c["k_t"], c["q_s"], c["k_h"]
    s = [_nt(q_t[:, sl], k_t[:, sl]) for sl in hs]
    da = [tri * _nt(do[:, sl], v[:, sl]) for sl in hs]
    dq_s = [_nn(do[:, sl], st[sl, :]) for sl in hs]
    dk_h = [_nn(v[:, sl], dst_new[sl, :]) for sl in hs]
    dv_s = [_nt(k_h[:, sl], dst_new[sl, :]) for sl in hs]
    dst_q = [_tn(do[:, sl], q_s[:, sl]) for sl in hs]
    dq_t = [_dot3(da_h, k_t[:, sl], ((1,), (0,))) for da_h, sl in zip(da, hs)]
    dk_t = [_dot3(da_h, q_t[:, sl], ((0,), (0,))) for da_h, sl in zip(da, hs)]
    dv_a = [_tn(tri * s_h, do[:, sl]) for s_h, sl in zip(s, hs)]
    cat = lambda parts: jnp.concatenate(parts, axis=1)
    dq_s, dk_h, dq_t, dk_t = cat(dq_s), cat(dk_h), cat(dq_t), cat(dk_t)
    dv = cat([x + y for x, y in zip(dv_a, dv_s)])
    dst = jnp.concatenate([dst_new[sl, :] * c["e_last"][:, sl] + d for sl, d in zip(hs, dst_q)], axis=0)
    dq = dq_s * c["e_b"] + dq_t * c["e_up"]
    dk = dk_t * c["e_dn"] + dk_h * c["e_lb"]
    db = c["q"] * dq - c["k"] * dk
    ss = cat([jnp.sum(dst_new[sl, :] * st[sl, :], axis=0, keepdims=True) for sl in hs])
    d_all = jnp.sum(dk_h * k_h, axis=0, keepdims=True) + c["e_last"] * ss
    dg = _scan_sum(db, not rev) + d_all
    dzq = dq * _dsilu(zq)
    df = dg / c["f"] - dk
    dzf = df * (1.0 - lb) * c["sg"] * (1.0 - c["sg"])
    dlb = jnp.sum(df * (1.0 - c["sg"]), axis=0, keepdims=True)
    return dzq, dzf, dv, dlb, dst


def _log_sigmoid(x):
    return jnp.minimum(x, 0.0) - jnp.log(1.0 + jnp.exp(-jnp.abs(x)))


def _each(fn, *lists):
    return [fn(*xs) for xs in zip(*lists)]


def _ml_forward_parts(qp, kp, v, gates, c, n, m, rev):
    hs = _head_slices(qp.shape[1], qp.shape[1] // ML_D)
    q_all = _silu(qp)
    k_all = _silu(kp) * (ML_D ** -0.5)
    q = [q_all[:, sl] for sl in hs]
    k = [k_all[:, sl] for sl in hs]
    vv = [v[:, sl] for sl in hs]
    cc = [c[sl, :] for sl in hs]
    tri_b = _tri(rev)
    tri = tri_b.astype(F32)
    tri_t = _tri(not rev).astype(F32)
    e_last = _last_onehot(rev)
    qk = _each(_nt, q, k)
    qc = _each(_nt, q, cc)
    parts = []
    for (gi_c, gi_r, gf_c, gf_r), m_h in zip(gates, m):
        lf_c, lf_r = _log_sigmoid(gf_c), _log_sigmoid(gf_r)
        b_c = jnp.sum(tri * lf_r, axis=1, keepdims=True)
        b_r = jnp.sum(tri_t * lf_c, axis=0, keepdims=True)
        log_w = jnp.where(tri_b, b_c - b_r + gi_r, -jnp.inf)
        m_inter = b_c + m_h
        m_t = jnp.maximum(m_inter, jnp.max(log_w, axis=1, keepdims=True))
        m_new = jnp.sum(m_t * e_last, axis=0, keepdims=True)
        b_last = jnp.sum(b_c * e_last, axis=0, keepdims=True)
        parts.append(dict(a=jnp.exp(m_inter - m_t), p=jnp.exp(log_w - m_t), floor=jnp.exp(-m_t), m_new=m_new,
                          ws=jnp.exp(b_last - b_c + gi_c - m_new), decay=jnp.exp(b_last + m_h - m_new), gf_c=gf_c))
    w = [pt["p"] * x for pt, x in zip(parts, qk)]
    wv = _each(_nn, w, vv)
    for pt, q_h, n_h, w_h, qc_h, wv_h in zip(parts, q, n, w, qc, wv):
        qn = jnp.sum(q_h * n_h, axis=1, keepdims=True)
        num = pt["a"] * qc_h + wv_h
        den = pt["a"] * qn + jnp.sum(w_h, axis=1, keepdims=True)
        pt.update(qn=qn, num=num, den=den, rinv=1.0 / jnp.maximum(jnp.abs(den), pt["floor"]), w=w_h, qc=qc_h)
    return hs, q, k, vv, cc, tri, parts


def ml_chunk_fwd(qp, kp, v, gates, c, n, m, rev):
    hs, q, k, vv, cc, tri, parts = _ml_forward_parts(qp, kp, v, gates, c, n, m, rev)
    h = jnp.concatenate([pt["num"] * pt["rinv"] for pt in parts], axis=1)
    upd = _each(_tn, [pt["ws"] * v_h for pt, v_h in zip(parts, vv)], k)
    c_new = jnp.concatenate([pt["decay"] * c_h + u for pt, c_h, u in zip(parts, cc, upd)], axis=0)
    n_new = [pt["decay"] * n_h + jnp.sum(pt["ws"] * k_h, axis=0, keepdims=True) for pt, n_h, k_h in zip(parts, n, k)]
    return h, c_new, n_new, [pt["m_new"] for pt in parts]


def ml_chunk_bwd(qp, kp, v, gates, c, n, m, dh, dc_new, dn_new, rev):
    hs, q, k, vv, cc, tri, parts = _ml_forward_parts(qp, kp, v, gates, c, n, m, rev)
    dcn = [dc_new[sl, :] for sl in hs]
    dnum, dden = [], []
    for pt, sl in zip(parts, hs):
        dh_h = dh[:, sl]
        h = pt["num"] * pt["rinv"]
        signed_live = jnp.where(jnp.abs(pt["den"]) > pt["floor"], jnp.where(pt["den"] >= 0.0, 1.0, -1.0), 0.0)
        dnum.append(dh_h * pt["rinv"])
        dden.append(-jnp.sum(dh_h * h, axis=1, keepdims=True) * pt["rinv"] * signed_live)
    dw = [x + y for x, y in zip(_each(_nt, dnum, vv), dden)]
    kdc = _each(_nt, k, dcn)
    vdc = _each(_nn, vv, dcn)
    dqk = [x * pt["p"] for x, pt in zip(dw, parts)]
    adn = [pt["a"] * x for pt, x in zip(parts, dnum)]
    dv_w = _each(_tn, [pt["w"] for pt in parts], dnum)
    dq_k = _each(_nn, dqk, k)
    dq_c = _each(_nn, adn, cc)
    dk_q = _each(_tn, dqk, q)
    dc_q = _each(_tn, adn, q)
    dq, dk, dv, dgi, dgf, dc, dn = [], [], [], [], [], [], []
    for i, pt in enumerate(parts):
        a, ws, decay = pt["a"], pt["ws"], pt["decay"]
        add = a * dden[i]
        e = dw[i] * pt["w"]
        dv.append(dv_w[i] + ws * kdc[i])
        dq.append(dq_k[i] + dq_c[i] + add * n[i])
        dk.append(dk_q[i] + ws * vdc[i] + ws * dn_new[i])
        alpha = (jnp.sum(dnum[i] * pt["qc"], axis=1, keepdims=True) + dden[i] * pt["qn"]) * a
        omega = (jnp.sum(vdc[i] * k[i], axis=1, keepdims=True) + jnp.sum(k[i] * dn_new[i], axis=1, keepdims=True)) * ws
        delta = decay * (jnp.sum(jnp.sum(dcn[i] * cc[i], axis=1, keepdims=True), axis=0, keepdims=True)
                         + jnp.sum(dn_new[i] * n[i], axis=1, keepdims=True))
        dc.append(decay * dcn[i] + dc_q[i])
        dn.append(decay * dn_new[i] + jnp.sum(add * q[i], axis=0, keepdims=True))
        e_rows = jnp.sum(e, axis=1, keepdims=True)
        e_cols = _row_to_col(jnp.sum(e, axis=0, keepdims=True))
        dgi.append(e_cols + omega)
        db = e_rows + alpha - e_cols - omega
        tail = jnp.sum(omega, axis=0, keepdims=True) + delta
        dlf = _row_to_col(jnp.sum(tri * db, axis=0, keepdims=True)) + tail
        dgf.append(dlf * (1.0 - _sigmoid(pt["gf_c"])))
    cat = lambda xs: jnp.concatenate(xs, axis=1)
    dqp = cat(dq) * _dsilu(qp)
    dkp = cat(dk) * (ML_D ** -0.5) * _dsilu(kp)
    return dqp, dkp, cat(dv), dgi, dgf, jnp.concatenate(dc, axis=0), dn


def _pick(n, prefs):
    for p in prefs:
        if n % p == 0:
            return p
    raise ValueError(f"no tile for {n} among {prefs}")


def _position():
    return lax.axis_index("x"), lax.axis_index("y"), lax.axis_index("c")


def _a2a_copies(x_ref, out_ref, send_sems, recv_sems, local_sem):
    px, py, pc = _position()
    me = 4 * px + 2 * py + pc
    mine = pltpu.make_async_copy(x_ref.at[me], out_ref.at[me], local_sem)
    sends, recvs = [], []
    for k, (fx, fy, fc) in enumerate([(1, 0, 0), (0, 1, 0), (1, 1, 0), (1, 0, 1), (0, 1, 1), (1, 1, 1), (0, 0, 1)]):
        qx, qy, qc = (1 - px if fx else px), (1 - py if fy else py), (1 - pc if fc else pc)
        peer = 4 * qx + 2 * qy + qc
        sends.append(pltpu.make_async_remote_copy(
            src_ref=x_ref.at[peer], dst_ref=out_ref.at[me], send_sem=send_sems.at[k], recv_sem=recv_sems.at[k],
            device_id=(qx, qy, qc), device_id_type=MESH))
        recvs.append(pltpu.make_async_remote_copy(
            src_ref=x_ref.at[me], dst_ref=out_ref.at[peer], send_sem=send_sems.at[k], recv_sem=recv_sems.at[k],
            device_id=(qx, qy, qc), device_id_type=MESH))
    return mine, sends, recvs


def _a2a_start(*refs):
    mine, sends, _ = _a2a_copies(*refs)
    mine.start()
    for cp in sends:
        cp.start()


def _a2a_wait(*refs):
    mine, sends, recvs = _a2a_copies(*refs)
    for cp in recvs:
        cp.wait_recv()
    for cp in sends:
        cp.wait_send()
    mine.wait()


_A2A_SCRATCH = [pltpu.SemaphoreType.DMA((7,)), pltpu.SemaphoreType.DMA((7,)), pltpu.SemaphoreType.DMA]
_ANY = pl.BlockSpec(memory_space=pl.ANY)


def _mm(a, b, mode, out_dtype, tm, tn, tk, name, a2a=None):
    if mode == "nn":
        (m, k), (k2, n) = a.shape, b.shape
    elif mode == "nt":
        (m, k), (n, k2) = a.shape, b.shape
    else:
        (k, m), (k2, n) = a.shape, b.shape
    assert k == k2 and m % tm == 0 and n % tn == 0 and k % tk == 0, (a.shape, b.shape, mode, tm, tn, tk)
    nk = k // tk
    dims = {"nn": ((1,), (0,)), "nt": ((1,), (1,)), "tn": ((0,), (0,))}[mode]
    a_spec = (pl.BlockSpec((tk, tm), lambda j, i, kk: (kk, i)) if mode == "tn"
              else pl.BlockSpec((tm, tk), lambda j, i, kk: (i, kk)))
    b_spec = (pl.BlockSpec((tn, tk), lambda j, i, kk: (j, kk)) if mode == "nt"
              else pl.BlockSpec((tk, tn), lambda j, i, kk: (kk, j)))

    grid = (n // tn, m // tm, nk)

    def body(a_ref, b_ref, *rest):
        if a2a is not None:
            x_ref, o_ref, got_ref, acc_ref = rest[:4]
            comm = (x_ref, got_ref) + tuple(rest[4:])
        else:
            o_ref, acc_ref = rest
        kk = pl.program_id(2)
        step = (pl.program_id(0) * grid[1] + pl.program_id(1)) * nk + kk
        if a2a is not None:
            @pl.when(step == 0)
            def _():
                _a2a_start(*comm)

        part = lax.dot_general(a_ref[...], b_ref[...], (dims, ((), ())), preferred_element_type=F32)

        @pl.when(kk == 0)
        def _():
            acc_ref[...] = part

        @pl.when(kk > 0)
        def _():
            acc_ref[...] += part

        @pl.when(kk == nk - 1)
        def _():
            o_ref[...] = acc_ref[...].astype(o_ref.dtype)

        if a2a is not None:
            @pl.when(step == grid[0] * grid[1] * nk - 1)
            def _():
                _a2a_wait(*comm)

    osz = jnp.dtype(out_dtype).itemsize
    need = 2 * (tm * tk * a.dtype.itemsize + tk * tn * b.dtype.itemsize + tm * tn * osz) + 2 * tm * tn * 4
    o_spec = pl.BlockSpec((tm, tn), lambda j, i, kk: (i, j))
    o_shape = jax.ShapeDtypeStruct((m, n), out_dtype)
    extra = a2a is not None
    return pl.pallas_call(
        body, name=name, grid=grid,
        in_specs=[a_spec, b_spec] + ([_ANY] if extra else []),
        out_specs=[o_spec, _ANY] if extra else o_spec,
        out_shape=[o_shape, jax.ShapeDtypeStruct(a2a.shape, a2a.dtype)] if extra else o_shape,
        scratch_shapes=[pltpu.VMEM((tm, tn), F32)] + (_A2A_SCRATCH if extra else []),
        compiler_params=_vmem(need + (4 << 20)),
    )(*((a, b, a2a) if extra else (a, b)))


ROWS = 256


def _ln_stats(x):
    mu = jnp.mean(x, axis=-1, keepdims=True)
    xc = x - mu
    var = jnp.mean(xc * xc, axis=-1, keepdims=True)
    rstd = lax.rsqrt(var + LN_EPS)
    return xc * rstd, rstd


def _modulate_fwd(xc_all, modp, nbc):
    tt, d = xc_all.shape

    def body(x_ref, mod_ref, o_ref):
        n, _ = _ln_stats(x_ref[...])
        o_ref[...] = (n * (1.0 + mod_ref[0, 1:2, :]) + mod_ref[0, 0:1, :]).astype(BF16)

    return pl.pallas_call(
        body, name="modulate_fwd", grid=(tt // ROWS,),
        in_specs=[pl.BlockSpec((ROWS, d), lambda i: (i, 0)),
                  pl.BlockSpec((1, 3, d), lambda i: (jnp.where(i >= nbc, 1, 0), 0, 0))],
        out_specs=pl.BlockSpec((ROWS, d), lambda i: (i, 0)),
        out_shape=jax.ShapeDtypeStruct((tt, d), BF16),
    )(xc_all, modp)


def _modulate_bwd(dh, xc_all, modp, dxa, nbc):
    tt, d = xc_all.shape
    t = dxa.shape[0]

    def body(dh_ref, x_ref, mod_ref, dxa_ref, gx_ref, sum_ref):
        i = pl.program_id(0)
        n, rstd = _ln_stats(x_ref[...])
        g = dh_ref[...]
        dn = g * (1.0 + mod_ref[0, 1:2, :])
        dx = rstd * (dn - jnp.mean(dn, axis=-1, keepdims=True) - n * jnp.mean(dn * n, axis=-1, keepdims=True))
        gx_ref[...] = dx + dxa_ref[...]
        dshift = jnp.sum(g, axis=0, keepdims=True)
        dscale = jnp.sum(g * n, axis=0, keepdims=True)

        @pl.when(i == 0)
        def _():
            sum_ref[...] = jnp.zeros_like(sum_ref)

        @pl.when(i < nbc)
        def _():
            sum_ref[0:1, :] += dshift
            sum_ref[1:2, :] += dscale

        @pl.when(i >= nbc)
        def _():
            sum_ref[2:3, :] += dshift
            sum_ref[3:4, :] += dscale

    lat = lambda i: (jnp.maximum(i - nbc, 0), 0)
    return pl.pallas_call(
        body, name="modulate_bwd", grid=(tt // ROWS,),
        in_specs=[pl.BlockSpec((ROWS, d), lambda i: (i, 0)), pl.BlockSpec((ROWS, d), lambda i: (i, 0)),
                  pl.BlockSpec((1, 3, d), lambda i: (jnp.where(i >= nbc, 1, 0), 0, 0)),
                  pl.BlockSpec((ROWS, d), lat)],
        out_specs=[pl.BlockSpec((ROWS, d), lat), pl.BlockSpec((8, d), lambda i: (0, 0))],
        out_shape=[jax.ShapeDtypeStruct((t, d), F32), jax.ShapeDtypeStruct((8, d), F32)],
    )(dh, xc_all, modp, dxa)


def _post_fwd(o_f, o_b, h_f, h_b, u, hgw, mlw, nbc):
    tt = u.shape[0]
    t = tt - nbc * ROWS

    def body(of_ref, ob_ref, hf_ref, hb_ref, az_ref, bo_ref, bz_ref, hgw_ref, mlw_ref, y_ref):
        o = of_ref[...] + ob_ref[...]
        for sl in _head_slices(W_A, HG_HEADS):
            oh = o[:, sl]
            rs = lax.rsqrt(jnp.mean(oh * oh, axis=-1, keepdims=True) + NORM_EPS)
            y_ref[:, sl] = (oh * rs * hgw_ref[:, sl] * _silu(az_ref[:, sl])).astype(BF16)
        hm = hf_ref[...] + hb_ref[...]
        for sl in _head_slices(W_B, ML_HEADS):
            hh = hm[:, sl]
            mu = jnp.mean(hh, axis=-1, keepdims=True)
            hc = hh - mu
            rstd = lax.rsqrt(jnp.mean(hc * hc, axis=-1, keepdims=True) + NORM_EPS)
            out = hc * rstd * mlw_ref[:, sl] * _sigmoid(bo_ref[:, sl]) * _silu(bz_ref[:, sl])
            y_ref[:, W_A + sl.start:W_A + sl.stop] = out.astype(BF16)

    row = lambda i: (i + nbc, 0)
    seg = lambda s: pl.BlockSpec((ROWS, 1024), lambda i: (i + nbc, s))
    wspec = pl.BlockSpec((1, 1024), lambda i: (0, 0))
    return pl.pallas_call(
        body, name="post_fwd", grid=(t // ROWS,),
        in_specs=[pl.BlockSpec((ROWS, 1024), row)] * 4 + [seg(SEG_AZ), seg(SEG_BO), seg(SEG_BZ), wspec, wspec],
        out_specs=pl.BlockSpec((ROWS, 2048), lambda i: (i, 0)),
        out_shape=jax.ShapeDtypeStruct((t, 2048), BF16),
    )(o_f, o_b, h_f, h_b, u, u, u, hgw, mlw)


def _post_bwd(dy, o_f, o_b, h_f, h_b, u, hgw, mlw, nbc):
    tt = u.shape[0]

    def body(dy_ref, of_ref, ob_ref, hf_ref, hb_ref, az_ref, bo_ref, bz_ref, hgw_ref, mlw_ref,
             do_ref, dhm_ref, du_ref, sum_ref):
        i = pl.program_id(0)
        live = jnp.where(i >= nbc, 1.0, 0.0)

        @pl.when(i == 0)
        def _():
            sum_ref[...] = jnp.zeros_like(sum_ref)

        o = of_ref[...] + ob_ref[...]
        for sl in _head_slices(W_A, HG_HEADS):
            oh = o[:, sl]
            rs = lax.rsqrt(jnp.mean(oh * oh, axis=-1, keepdims=True) + NORM_EPS)
            on = oh * rs
            az = az_ref[:, sl]
            dya = dy_ref[:, sl] * live
            doa = dya * _silu(az)
            du_ref[:, sl] = (dya * on * hgw_ref[:, sl] * _dsilu(az)).astype(BF16)
            sum_ref[0:1, sl] += jnp.sum(doa * on, axis=0, keepdims=True)
            don = doa * hgw_ref[:, sl]
            do_ref[:, sl] = rs * (don - on * jnp.mean(don * on, axis=-1, keepdims=True))
        hm = hf_ref[...] + hb_ref[...]
        for sl in _head_slices(W_B, ML_HEADS):
            hh = hm[:, sl]
            mu = jnp.mean(hh, axis=-1, keepdims=True)
            hc = hh - mu
            rstd = lax.rsqrt(jnp.mean(hc * hc, axis=-1, keepdims=True) + NORM_EPS)
            hn = hc * rstd
            hw = hn * mlw_ref[:, sl]
            bo, bz = bo_ref[:, sl], bz_ref[:, sl]
            sbo, sbz = _sigmoid(bo), _silu(bz)
            dyb = dy_ref[:, W_A + sl.start:W_A + sl.stop] * live
            dhw = dyb * sbo * sbz
            du_ref[:, 1024 + sl.start:1024 + sl.stop] = (dyb * hw * sbz * sbo * (1.0 - sbo)).astype(BF16)
            du_ref[:, 2048 + sl.start:2048 + sl.stop] = (dyb * hw * sbo * _dsilu(bz)).astype(BF16)
            sum_ref[1:2, sl] += jnp.sum(dhw * hn, axis=0, keepdims=True)
            dhn = dhw * mlw_ref[:, sl]
            dhm_ref[:, sl] = rstd * (dhn - jnp.mean(dhn, axis=-1, keepdims=True)
                                     - hn * jnp.mean(dhn * hn, axis=-1, keepdims=True))

    row = lambda i: (i, 0)
    seg = lambda s: pl.BlockSpec((ROWS, 1024), lambda i: (i, s))
    wspec = pl.BlockSpec((1, 1024), lambda i: (0, 0))
    return pl.pallas_call(
        body, name="post_bwd", grid=(tt // ROWS,),
        in_specs=[pl.BlockSpec((ROWS, 2048), lambda i: (jnp.maximum(i - nbc, 0), 0))]
        + [pl.BlockSpec((ROWS, 1024), row)] * 4 + [seg(SEG_AZ), seg(SEG_BO), seg(SEG_BZ), wspec, wspec],
        out_specs=[pl.BlockSpec((ROWS, 1024), row), pl.BlockSpec((ROWS, 1024), row),
                   pl.BlockSpec((ROWS, 3072), row), pl.BlockSpec((8, 1024), lambda i: (0, 0))],
        out_shape=[jax.ShapeDtypeStruct((tt, 1024), F32), jax.ShapeDtypeStruct((tt, 1024), F32),
                   jax.ShapeDtypeStruct((tt, 3072), BF16), jax.ShapeDtypeStruct((8, 1024), F32)],
    )(dy, o_f, o_b, h_f, h_b, u, u, u, hgw, mlw)


def _final(z, xc_all, target, modp, ln_g, ln_b, nbc):
    t, d = z.shape

    def body(z_ref, x_ref, tg_ref, mod_ref, g_ref, b_ref, dz_ref, dxa_ref, sum_ref):
        i = pl.program_id(0)
        zz = z_ref[...]
        gate = mod_ref[0, 2:3, :]
        pre = ALPHA * x_ref[...] + gate * zz
        nh, rstd = _ln_stats(pre)
        err = nh * g_ref[...] + b_ref[...] - tg_ref[...]
        dxo = err * (1.0 / d)
        dnh = dxo * g_ref[...]
        dpre = rstd * (dnh - jnp.mean(dnh, axis=-1, keepdims=True) - nh * jnp.mean(dnh * nh, axis=-1, keepdims=True))
        dz_ref[...] = (gate * dpre).astype(BF16)
        dxa_ref[...] = ALPHA * dpre

        @pl.when(i == 0)
        def _():
            sum_ref[...] = jnp.zeros_like(sum_ref)

        sum_ref[0:1, :] += jnp.sum(dpre * zz, axis=0, keepdims=True)
        sum_ref[1:2, :] += jnp.sum(dxo * nh, axis=0, keepdims=True)
        sum_ref[2:3, :] += jnp.sum(dxo, axis=0, keepdims=True)
        sum_ref[3:4, :] += jnp.sum(err * err, axis=0, keepdims=True)

    row = lambda i: (i, 0)
    vec = pl.BlockSpec((1, d), lambda i: (0, 0))
    return pl.pallas_call(
        body, name="final_ln_loss", grid=(t // ROWS,),
        in_specs=[pl.BlockSpec((ROWS, d), row), pl.BlockSpec((ROWS, d), lambda i: (i + nbc, 0)),
                  pl.BlockSpec((ROWS, d), row), pl.BlockSpec((1, 3, d), lambda i: (1, 0, 0)), vec, vec],
        out_specs=[pl.BlockSpec((ROWS, d), row), pl.BlockSpec((ROWS, d), row), pl.BlockSpec((8, d), lambda i: (0, 0))],
        out_shape=[jax.ShapeDtypeStruct((t, d), BF16), jax.ShapeDtypeStruct((t, d), F32),
                   jax.ShapeDtypeStruct((8, d), F32)],
    )(z, xc_all, target, modp, ln_g, ln_b)


GRID_W = 64


def _shift(x, s, ok):
    n = x.shape[0]
    return jnp.where(ok, pltpu.roll(x, s % n, 0), 0.0)


def _grid_masks(n):
    t = lax.broadcasted_iota(jnp.int32, (n, LANE), 0)
    col = t & (GRID_W - 1)
    return dict(left=col >= 1, right=col <= GRID_W - 2, up=t >= GRID_W, down=t < n - GRID_W)


def _seq_masks(n):
    t = lax.broadcasted_iota(jnp.int32, (n, LANE), 0)
    return dict(left=t >= 1, right=t <= n - 2)


def _conv_fwd(u, w9, cb, tc):
    tt = u.shape[0]
    t = tt - tc

    def body(u_ref, w_ref, b_ref, o_ref):
        w = [w_ref[r:r + 1, :] for r in range(9)]
        xc = u_ref[0:tc, :]
        ms = _seq_masks(tc)
        o_ref[0:tc, :] = (w[3] * _shift(xc, 1, ms["left"]) + w[4] * xc + w[5] * _shift(xc, -1, ms["right"])
                          + b_ref[...])
        x = u_ref[tc:tt, :]
        mg = _grid_masks(t)
        taps = (_shift(x, 1, mg["left"]), x, _shift(x, -1, mg["right"]))
        rows = [w[3 * i] * taps[0] + w[3 * i + 1] * taps[1] + w[3 * i + 2] * taps[2] for i in range(3)]
        o_ref[tc:tt, :] = (rows[1] + _shift(rows[0], GRID_W, mg["up"]) + _shift(rows[2], -GRID_W, mg["down"])
                           + b_ref[...])

    return pl.pallas_call(
        body, name="conv_fwd", grid=(2048 // LANE,),
        in_specs=[pl.BlockSpec((tt, LANE), lambda j: (0, BLK_QK + j)), pl.BlockSpec((9, LANE), lambda j: (0, j)),
                  pl.BlockSpec((1, LANE), lambda j: (0, j))],
        out_specs=pl.BlockSpec((tt, LANE), lambda j: (0, j)),
        out_shape=jax.ShapeDtypeStruct((tt, 2048), F32),
        compiler_params=_vmem(40 * tt * LANE * 4),
    )(u, w9, cb)


def _conv_bwd(dcp, u, w9, tc):
    tt = u.shape[0]
    t = tt - tc

    def body(d_ref, u_ref, w_ref, du_ref, gw_ref, gb_ref):
        w = [w_ref[r:r + 1, :] for r in range(9)]
        csum = lambda a: jnp.sum(a, axis=0, keepdims=True)
        dc = d_ref[0:tc, :]
        xc = u_ref[0:tc, :]
        ms = _seq_masks(tc)
        du_ref[0:tc, :] = (w[3] * _shift(dc, -1, ms["right"]) + w[4] * dc + w[5] * _shift(dc, 1, ms["left"])).astype(BF16)
        gmid = [csum(dc * _shift(xc, 1, ms["left"])), csum(dc * xc), csum(dc * _shift(xc, -1, ms["right"]))]
        d = d_ref[tc:tt, :]
        x = u_ref[tc:tt, :]
        mg = _grid_masks(t)
        dtaps = (_shift(d, -1, mg["right"]), d, _shift(d, 1, mg["left"]))
        rows = [w[3 * i] * dtaps[0] + w[3 * i + 1] * dtaps[1] + w[3 * i + 2] * dtaps[2] for i in range(3)]
        du_ref[tc:tt, :] = (rows[1] + _shift(rows[0], -GRID_W, mg["down"]) + _shift(rows[2], GRID_W, mg["up"])).astype(BF16)
        xtaps = (_shift(x, 1, mg["left"]), x, _shift(x, -1, mg["right"]))
        for j in range(3):
            gw_ref[j:j + 1, :] = csum(d * _shift(xtaps[j], GRID_W, mg["up"]))
            gw_ref[3 + j:4 + j, :] = csum(d * xtaps[j]) + gmid[j]
            gw_ref[6 + j:7 + j, :] = csum(d * _shift(xtaps[j], -GRID_W, mg["down"]))
        gb_ref[...] = csum(d) + csum(dc)

    return pl.pallas_call(
        body, name="conv_bwd", grid=(2048 // LANE,),
        in_specs=[pl.BlockSpec((tt, LANE), lambda j: (0, j)), pl.BlockSpec((tt, LANE), lambda j: (0, BLK_QK + j)),
                  pl.BlockSpec((9, LANE), lambda j: (0, j))],
        out_specs=[pl.BlockSpec((tt, LANE), lambda j: (0, j)), pl.BlockSpec((9, LANE), lambda j: (0, j)),
                   pl.BlockSpec((1, LANE), lambda j: (0, j))],
        out_shape=[jax.ShapeDtypeStruct((tt, 2048), BF16), jax.ShapeDtypeStruct((9, 2048), F32),
                   jax.ShapeDtypeStruct((1, 2048), F32)],
        compiler_params=_vmem(48 * tt * LANE * 4),
    )(dcp, u, w9)


def _chunk_of(pos, ncc, nc, rev):
    if not rev:
        return pos
    return jnp.where(pos < ncc, ncc - 1 - pos, nc - 1 - (pos - ncc))


def _hgrn_fwd(u, lower_d, ncc, rev):
    tt = u.shape[0]
    nc = tt // CHUNK
    seg_f = SEG_AFB if rev else SEG_AFF
    heads = _head_slices(W_A, HG_HEADS)

    def body(zq_ref, zf_ref, v_ref, lb_ref, o_ref, hist_ref, st_ref):
        @pl.when(pl.program_id(0) == 0)
        def _():
            st_ref[...] = jnp.zeros_like(st_ref)

        st = st_ref[...]
        hist_ref[0] = st
        o, st_new = hg_chunk_fwd(zq_ref[...], zf_ref[...], v_ref[...], lb_ref[...], st, rev)
        o_ref[...] = o
        st_ref[...] = st_new

    seg = lambda s: pl.BlockSpec((CHUNK, 1024), lambda j: (_chunk_of(j, ncc, nc, rev), s))
    return pl.pallas_call(
        body, name="hgrn_fwd_rev" if rev else "hgrn_fwd", grid=(nc,),
        in_specs=[seg(SEG_AQ), seg(seg_f), seg(SEG_AI), pl.BlockSpec((1, 1024), lambda j: (0, 0))],
        out_specs=[pl.BlockSpec((CHUNK, 1024), lambda j: (_chunk_of(j, ncc, nc, rev), 0)),
                   pl.BlockSpec((1, 1024, HG_D), lambda j: (_chunk_of(j, ncc, nc, rev), 0, 0))],
        out_shape=[jax.ShapeDtypeStruct((tt, 1024), F32), jax.ShapeDtypeStruct((nc, 1024, HG_D), F32)],
        scratch_shapes=[pltpu.VMEM((1024, HG_D), F32)],
    )(u, u, u, lower_d)


def _hgrn_bwd(u, lower_d, hist, do, acc, ncc, rev, a2a=None):
    tt = u.shape[0]
    nc = tt // CHUNK
    seg_f = SEG_AFB if rev else SEG_AFF
    has_acc = acc is not None
    has_a2a = a2a is not None

    def body(zq_ref, zf_ref, v_ref, lb_ref, hist_ref, do_ref, *rest):
        if has_acc:
            aq_ref, av_ref = rest[:2]
            rest = rest[2:]
        if has_a2a:
            x_ref, rest = rest[0], rest[1:]
        dzf_ref, dzq_ref, dv_ref, dlb_ref = rest[:4]
        rest = rest[4:]
        if has_a2a:
            comm = (x_ref, rest[0]) + tuple(rest[2:])
            dst_ref = rest[1]
        else:
            dst_ref = rest[0]

        @pl.when(pl.program_id(0) == 0)
        def _():
            dst_ref[...] = jnp.zeros_like(dst_ref)
            dlb_ref[...] = jnp.zeros_like(dlb_ref)
            if has_a2a:
                _a2a_start(*comm)

        dzq, dzf, dv, dlb, dst = hg_chunk_bwd(zq_ref[...], zf_ref[...], v_ref[...], lb_ref[...],
                                              hist_ref[0], do_ref[...], dst_ref[...], rev)
        dst_ref[...] = dst
        dzf_ref[...] = dzf.astype(BF16)
        dlb_ref[...] += dlb
        if has_acc:
            dzq = dzq + aq_ref[...]
            dv = dv + av_ref[...]
        dzq_ref[...] = dzq
        dv_ref[...] = dv

        if has_a2a:
            @pl.when(pl.program_id(0) == nc - 1)
            def _():
                _a2a_wait(*comm)

    cidx = lambda j: _chunk_of(nc - 1 - j, ncc, nc, rev)
    seg = lambda s: pl.BlockSpec((CHUNK, 1024), lambda j: (cidx(j), s))
    row = pl.BlockSpec((CHUNK, 1024), lambda j: (cidx(j), 0))
    ins = [u, u, u, lower_d, hist, do] + (list(acc) if has_acc else []) + ([a2a] if has_a2a else [])
    return pl.pallas_call(
        body, name="hgrn_bwd_rev" if rev else "hgrn_bwd", grid=(nc,),
        in_specs=[seg(SEG_AQ), seg(seg_f), seg(SEG_AI), pl.BlockSpec((1, 1024), lambda j: (0, 0)),
                  pl.BlockSpec((1, 1024, HG_D), lambda j: (cidx(j), 0, 0)), row] + ([row, row] if has_acc else [])
        + ([_ANY] if has_a2a else []),
        out_specs=[row, row, row, pl.BlockSpec((1, 1024), lambda j: (0, 0))] + ([_ANY] if has_a2a else []),
        out_shape=[jax.ShapeDtypeStruct((tt, 1024), BF16), jax.ShapeDtypeStruct((tt, 1024), F32),
                   jax.ShapeDtypeStruct((tt, 1024), F32), jax.ShapeDtypeStruct((1, 1024), F32)]
        + ([jax.ShapeDtypeStruct(a2a.shape, a2a.dtype)] if has_a2a else []),
        scratch_shapes=[pltpu.VMEM((1024, HG_D), F32)] + (_A2A_SCRATCH if has_a2a else []),
    )(*ins)


def _gate_views(gc_ref, gr_ref, bc_ref, br_ref, head, rev):
    gc = gc_ref[0] + bc_ref[...]
    gr = gr_ref[0] + br_ref[...]
    lane = lax.broadcasted_iota(jnp.int32, (1, 16), 1)
    sub = lax.broadcasted_iota(jnp.int32, (16, 1), 0)
    d = 1 if rev else 0
    ii, fi = d * ML_HEADS + head, 2 * ML_HEADS + d * ML_HEADS + head
    col = lambda idx: jnp.sum(jnp.where(lane == idx, gc, 0.0), axis=1, keepdims=True)
    row = lambda idx: jnp.sum(jnp.where(sub == idx, gr, 0.0), axis=0, keepdims=True)
    return col(ii), row(ii), col(fi), row(fi)


def _mlstm_fwd(cpre, u, gcol, grow, bias_c, bias_r, ncc, rev):
    tt = u.shape[0]
    nc = tt // CHUNK
    heads = _head_slices(W_B, ML_HEADS)

    def body(q_ref, k_ref, v_ref, gc_ref, gr_ref, bc_ref, br_ref, h_ref, ch_ref, nh_ref, mh_ref, c_ref, n_ref, m_ref):
        @pl.when(pl.program_id(0) == 0)
        def _():
            c_ref[...] = jnp.zeros_like(c_ref)
            n_ref[...] = jnp.zeros_like(n_ref)
            m_ref[...] = jnp.zeros_like(m_ref)

        c, n_all, m_all = c_ref[...], n_ref[...], m_ref[...]
        ch_ref[0] = c
        nh_ref[0] = n_all
        mh_ref[0] = m_all
        nhd = len(heads)
        gates = [_gate_views(gc_ref, gr_ref, bc_ref, br_ref, hd, rev) for hd in range(nhd)]
        h, c_new, n_new, m_new = ml_chunk_fwd(q_ref[...], k_ref[...], v_ref[...], gates, c,
                                              [n_all[hd:hd + 1, :] for hd in range(nhd)],
                                              [m_all[hd:hd + 1, 0:1] for hd in range(nhd)], rev)
        h_ref[...] = h
        c_ref[...] = c_new
        for hd in range(nhd):
            n_ref[hd:hd + 1, :] = n_new[hd]
            m_ref[hd:hd + 1, :] = jnp.broadcast_to(m_new[hd], (1, LANE))

    cidx = lambda j: _chunk_of(j, ncc, nc, rev)
    row = lambda s: pl.BlockSpec((CHUNK, 1024), lambda j: (cidx(j), s))
    st3 = lambda a, b: pl.BlockSpec((1, a, b), lambda j: (cidx(j), 0, 0))
    return pl.pallas_call(
        body, name="mlstm_fwd_rev" if rev else "mlstm_fwd", grid=(nc,),
        in_specs=[row(0), row(1), row(SEG_BV), st3(CHUNK, 16), st3(16, CHUNK),
                  pl.BlockSpec((1, 16), lambda j: (0, 0)), pl.BlockSpec((16, 1), lambda j: (0, 0))],
        out_specs=[row(0), st3(1024, ML_D), st3(8, ML_D), st3(8, LANE)],
        out_shape=[jax.ShapeDtypeStruct((tt, 1024), F32), jax.ShapeDtypeStruct((nc, 1024, ML_D), F32),
                   jax.ShapeDtypeStruct((nc, 8, ML_D), F32), jax.ShapeDtypeStruct((nc, 8, LANE), F32)],
        scratch_shapes=[pltpu.VMEM((1024, ML_D), F32), pltpu.VMEM((8, ML_D), F32), pltpu.VMEM((8, LANE), F32)],
    )(cpre, cpre, u, gcol, grow, bias_c, bias_r)


def _mlstm_bwd(cpre, u, gcol, grow, bias_c, bias_r, chist, nhist, mhist, dh, acc, ncc, rev):
    tt = u.shape[0]
    nc = tt // CHUNK
    heads = _head_slices(W_B, ML_HEADS)
    has_acc = acc is not None
    d = 1 if rev else 0

    def body(q_ref, k_ref, v_ref, gc_ref, gr_ref, bc_ref, br_ref, ch_ref, nh_ref, mh_ref, dh_ref, *rest):
        if has_acc:
            aqk_ref, av_ref, ag_ref = rest[:3]
            rest = rest[3:]
        dqk_ref, dv_ref, dg_ref, gs_ref, dc_ref, dn_ref = rest

        @pl.when(pl.program_id(0) == 0)
        def _():
            dc_ref[...] = jnp.zeros_like(dc_ref)
            dn_ref[...] = jnp.zeros_like(dn_ref)
            gs_ref[...] = jnp.zeros_like(gs_ref)

        lane = lax.broadcasted_iota(jnp.int32, (1, LANE), 1)
        dg = ag_ref[...] if has_acc else jnp.zeros((CHUNK, LANE), F32)
        nhd = len(heads)
        gates = [_gate_views(gc_ref, gr_ref, bc_ref, br_ref, hd, rev) for hd in range(nhd)]
        n_all, m_all, dn_all = nh_ref[0], mh_ref[0], dn_ref[...]
        dqp, dkp, dv, dgi, dgf, dc, dn = ml_chunk_bwd(
            q_ref[...], k_ref[...], v_ref[...], gates, ch_ref[0],
            [n_all[hd:hd + 1, :] for hd in range(nhd)], [m_all[hd:hd + 1, 0:1] for hd in range(nhd)],
            dh_ref[...], dc_ref[...], [dn_all[hd:hd + 1, :] for hd in range(nhd)], rev)
        dc_ref[...] = dc
        for hd in range(nhd):
            dn_ref[hd:hd + 1, :] = dn[hd]
            dg = dg + jnp.where(lane == d * ML_HEADS + hd, dgi[hd], 0.0)
            dg = dg + jnp.where(lane == 2 * ML_HEADS + d * ML_HEADS + hd, dgf[hd], 0.0)
        if has_acc:
            dqp = dqp + aqk_ref[:, 0:W_B]
            dkp = dkp + aqk_ref[:, W_B:2 * W_B]
            dv = dv + av_ref[...]
        dqk_ref[:, 0:W_B] = dqp
        dqk_ref[:, W_B:2 * W_B] = dkp
        dv_ref[...] = dv
        dg_ref[...] = dg
        gs_ref[...] += jnp.sum(dg, axis=0, keepdims=True)

    cidx = lambda j: _chunk_of(nc - 1 - j, ncc, nc, rev)
    row = lambda s: pl.BlockSpec((CHUNK, 1024), lambda j: (cidx(j), s))
    wide = pl.BlockSpec((CHUNK, 2048), lambda j: (cidx(j), 0))
    gate = pl.BlockSpec((CHUNK, LANE), lambda j: (cidx(j), 0))
    st3 = lambda a, b: pl.BlockSpec((1, a, b), lambda j: (cidx(j), 0, 0))
    ins = [cpre, cpre, u, gcol, grow, bias_c, bias_r, chist, nhist, mhist, dh] + (list(acc) if has_acc else [])
    return pl.pallas_call(
        body, name="mlstm_bwd_rev" if rev else "mlstm_bwd", grid=(nc,),
        in_specs=[row(0), row(1), row(SEG_BV), st3(CHUNK, 16), st3(16, CHUNK),
                  pl.BlockSpec((1, 16), lambda j: (0, 0)), pl.BlockSpec((16, 1), lambda j: (0, 0)),
                  st3(1024, ML_D), st3(8, ML_D), st3(8, LANE), row(0)] + ([wide, row(0), gate] if has_acc else []),
        out_specs=[wide, row(0), gate, pl.BlockSpec((1, LANE), lambda j: (0, 0))],
        out_shape=[jax.ShapeDtypeStruct((tt, 2048), F32), jax.ShapeDtypeStruct((tt, 1024), F32),
                   jax.ShapeDtypeStruct((tt, LANE), F32), jax.ShapeDtypeStruct((1, LANE), F32)],
        scratch_shapes=[pltpu.VMEM((1024, ML_D), F32), pltpu.VMEM((8, ML_D), F32)],
    )(*ins)


def _whole(body, out_shape, name, *args, nbytes=0):
    return pl.pallas_call(body, name=name, out_shape=out_shape, compiler_params=_vmem(nbytes))(*args)


def _mod_fwd(cs, w_cols, b_cols):
    def body(c_ref, w_ref, b_ref, o_ref):
        o_ref[...] = _exact_nn(_silu(c_ref[...]), w_ref[...]) + b_ref[...]

    return _whole(body, jax.ShapeDtypeStruct((16, w_cols.shape[1]), F32), "mod_fwd", cs, w_cols, b_cols,
                  nbytes=4 * w_cols.size * 4)


def _mod_bwd_w(cs, d9, w_cols):
    def body(c_ref, d_ref, w_ref, gw_ref, pc_ref):
        gw_ref[...] = _exact_tn(_silu(c_ref[...]), d_ref[...])
        pc = lax.dot_general(d_ref[8:16, :], w_ref[...], (((1,), (1,)), ((), ())), precision=lax.Precision.HIGHEST,
                             preferred_element_type=F32)
        row = lax.broadcasted_iota(jnp.int32, pc.shape, 0)
        pc_ref[...] = jnp.where(row == 0, pc, 0.0)

    return _whole(body, [jax.ShapeDtypeStruct(w_cols.shape, F32), jax.ShapeDtypeStruct((8, w_cols.shape[0]), F32)],
                  "mod_bwd_w", cs, d9, w_cols, nbytes=6 * w_cols.size * 4)


def _lower_fwd(lb4):
    def body(l_ref, o_ref):
        o_ref[...] = jnp.zeros_like(o_ref)
        o_ref[0:1, :] = 1.0 / (1.0 + jnp.exp(l_ref[1:2, :] - l_ref[0:1, :]))
        o_ref[1:2, :] = 1.0 / (1.0 + jnp.exp(l_ref[3:4, :] - l_ref[2:3, :]))

    return _whole(body, jax.ShapeDtypeStruct((8, lb4.shape[1]), F32), "lower_fwd", lb4)


def _reduce8(g, name):
    def body(g_ref, o_ref):
        acc = g_ref[0]
        for k in range(1, N_DEV):
            acc = acc + g_ref[k]
        o_ref[...] = acc

    return _whole(body, jax.ShapeDtypeStruct(g.shape[1:], F32), name, g, nbytes=4 * g.size * 4)


_PACK = (("dmodx", 48), ("dmodc", 48), ("gconvw", 144), ("gconvb", 16), ("dlower", 16), ("ghgw", 8), ("gmlw", 8),
         ("glng", 16), ("glnb", 16), ("losssq", 16), ("ggate", 8))
_PACK_ROWS = sum(r for _, r in _PACK)


def _pack_offsets():
    off, out = 0, {}
    for name, rows in _PACK:
        out[name] = (off, rows)
        off += rows
    return out


def _small_finish(total, p0, d_feat):
    offs = _pack_offsets()

    def body(t_ref, p_ref, gb_ref, a0_ref, a1_ref, loss_ref):
        ox, oc, ol, oq = offs["dmodx"][0], offs["dmodc"][0], offs["dlower"][0], offs["losssq"][0]
        gb_ref[...] = t_ref[ox:ox + 48, :] + t_ref[oc:oc + 48, :]
        p = p_ref[...]
        da0 = t_ref[ol:ol + 16, :] * p * (1.0 - p)
        a0_ref[...] = da0
        a1_ref[...] = -da0
        sq = t_ref[oq:oq + 16, :]
        tot = jnp.sum(jnp.sum(sq, axis=1, keepdims=True), axis=0, keepdims=True)
        loss_ref[...] = jnp.broadcast_to(tot * (0.5 / d_feat), loss_ref.shape)

    s = jax.ShapeDtypeStruct
    return _whole(body, [s((48, LANE), F32), s((16, LANE), F32), s((16, LANE), F32), s((8, LANE), F32)],
                  "small_finish", total, p0)


def _cctx_grad(parts, c_ctx8):
    def body(p_ref, c_ref, o_ref):
        acc = p_ref[0]
        for k in range(1, N_DEV):
            acc = acc + p_ref[k]
        o_ref[...] = acc * _dsilu(c_ref[...])

    return _whole(body, jax.ShapeDtypeStruct(c_ctx8.shape, F32), "cctx_grad", parts, c_ctx8)


def _adam_math(w, g, m, v):
    m = ADAM_B1 * m + (1.0 - ADAM_B1) * g
    v = ADAM_B2 * v + (1.0 - ADAM_B2) * (g * g)
    m_hat = m / (1.0 - ADAM_B1 ** ADAM_STEP)
    v_hat = v / (1.0 - ADAM_B2 ** ADAM_STEP)
    delta = -ADAM_LR * (m_hat / (jnp.sqrt(v_hat) + ADAM_EPS) + ADAM_WD * w)
    return delta, m, v


def _adamw(w, g, m, v, rows, name):
    r, c = w.shape

    def body(w_ref, g_ref, m_ref, v_ref, d_ref, mo_ref, vo_ref):
        d_ref[...], mo_ref[...], vo_ref[...] = _adam_math(w_ref[...], g_ref[...], m_ref[...], v_ref[...])

    spec = pl.BlockSpec((rows, c), lambda i: (i, 0))
    return pl.pallas_call(
        body, name=name, grid=(r // rows,), in_specs=[spec] * 4, out_specs=[spec] * 3,
        out_shape=[jax.ShapeDtypeStruct((r, c), F32)] * 3,
        compiler_params=_vmem(16 * rows * (c + LANE) * 4),
    )(w, g, m, v)


def _sum8(recv, cols, name):
    _, r, c = recv.shape

    def body(r_ref, o_ref):
        g = r_ref[0].astype(F32)
        for k in range(1, N_DEV):
            g = g + r_ref[k].astype(F32)
        o_ref[...] = g

    return pl.pallas_call(
        body, name=name, grid=(c // cols,),
        in_specs=[pl.BlockSpec((N_DEV, r, cols), lambda i: (0, 0, i))], out_specs=pl.BlockSpec((r, cols), lambda i: (0, i)),
        out_shape=jax.ShapeDtypeStruct((r, c), F32),
        compiler_params=_vmem(2 * (r + 16) * cols * (N_DEV * 2 + 4) + (4 << 20)),
    )(recv)


def _rs_adamw(recv, w, m, v, rows, name):
    _, r, c = recv.shape

    def body(r_ref, w_ref, m_ref, v_ref, g_ref, d_ref, mo_ref, vo_ref):
        g = r_ref[0].astype(F32)
        for k in range(1, N_DEV):
            g = g + r_ref[k].astype(F32)
        g_ref[...] = g
        d_ref[...], mo_ref[...], vo_ref[...] = _adam_math(w_ref[...], g, m_ref[...], v_ref[...])

    spec = pl.BlockSpec((rows, c), lambda i: (i, 0))
    return pl.pallas_call(
        body, name=name, grid=(r // rows,),
        in_specs=[pl.BlockSpec((N_DEV, rows, c), lambda i: (0, i, 0))] + [spec] * 3, out_specs=[spec] * 4,
        out_shape=[jax.ShapeDtypeStruct((r, c), F32)] * 4,
        compiler_params=_vmem(2 * rows * (c + LANE) * (N_DEV * 2 + 7 * 4) + (4 << 20)),
    )(recv, w, m, v)


def _all_gather(x, name):
    r, c = x.shape

    def body(x_ref, out_ref, send_sems, recv_sems, local_sem):
        px, py, pc = _position()
        me, sibling = (px, py, pc), (px, py, 1 - pc)
        chips = [(1 - px, py), (px, 1 - py), (1 - px, 1 - py)]

        def slot(qx, qy, qc):
            return out_ref.at[4 * qx + 2 * qy + qc]

        def copy(k, block, to, src=None):
            return pltpu.make_async_remote_copy(
                src_ref=slot(*block) if src is None else src, dst_ref=slot(*block),
                send_sem=send_sems.at[k], recv_sem=recv_sems.at[k], device_id=to, device_id_type=MESH)

        mine = pltpu.make_async_copy(x_ref, slot(*me), local_sem)
        mine.start()
        first = [copy(1 + j, me, (*chip, pc), src=x_ref) for j, chip in enumerate(chips)]
        first.append(copy(0, me, sibling, src=x_ref))
        for cp in first:
            cp.start()
        passed = [copy(4 + j, (*chip, pc), sibling) for j, chip in enumerate(chips)]
        for j, chip in enumerate(chips):
            copy(1 + j, (*chip, pc), me).wait_recv()
            passed[j].start()
        copy(0, sibling, me).wait_recv()
        for j, chip in enumerate(chips):
            copy(4 + j, (*chip, 1 - pc), me).wait_recv()
        for cp in first + passed:
            cp.wait_send()
        mine.wait()

    return pl.pallas_call(
        body, name=name, out_shape=jax.ShapeDtypeStruct((N_DEV, r, c), x.dtype),
        in_specs=[pl.BlockSpec(memory_space=pl.ANY)], out_specs=pl.BlockSpec(memory_space=pl.ANY),
        scratch_shapes=[pltpu.SemaphoreType.DMA((7,)), pltpu.SemaphoreType.DMA((7,)), pltpu.SemaphoreType.DMA],
    )(x)


def _all_to_all(x, name):
    _, r, c = x.shape

    def body(*refs):
        _a2a_start(*refs)
        _a2a_wait(*refs)

    return pl.pallas_call(
        body, name=name, out_shape=jax.ShapeDtypeStruct(x.shape, x.dtype),
        in_specs=[_ANY], out_specs=_ANY, scratch_shapes=_A2A_SCRATCH,
    )(x)


def _local_step(xc_all, target, modp, lower, wt_u, w_o, w9, conv_b, gate_b, hgw, mlw, ln_g, ln_b, tc, exchange):
    tt = xc_all.shape[0]
    nbc, ncc, nc = tc // ROWS, tc // CHUNK, tt // CHUNK
    lower_f, lower_b = lower[0:1], lower[1:2]

    hc = _modulate_fwd(xc_all, modp, nbc)
    u = _mm(hc, wt_u, "nt", F32, ROWS, 1152, D_MODEL, "mm_u")
    cpre = _conv_fwd(u, w9, conv_b, tc)
    gates = u[:, BLK_GATE * LANE:BLK_GATE * LANE + 16].reshape(nc, CHUNK, 16)
    gcol, grow = gates, jnp.swapaxes(gates, 1, 2)
    bias_c, bias_r = gate_b.reshape(1, 16), gate_b.reshape(16, 1)

    o_f, hist_f = _hgrn_fwd(u, lower_f, ncc, False)
    o_b, hist_b = _hgrn_fwd(u, lower_b, ncc, True)
    h_f, ch_f, nh_f, mh_f = _mlstm_fwd(cpre, u, gcol, grow, bias_c, bias_r, ncc, False)
    h_b, ch_b, nh_b, mh_b = _mlstm_fwd(cpre, u, gcol, grow, bias_c, bias_r, ncc, True)
    y = _post_fwd(o_f, o_b, h_f, h_b, u, hgw, mlw, nbc)
    tm = _pick(y.shape[0], (512, 256))
    z = _mm(y, w_o, "nn", F32, tm, D_MODEL, D_MODEL, "mm_z")
    dz, dxa, fsum = _final(z, xc_all, target, modp, ln_g, ln_b, nbc)

    dy = _mm(dz, w_o, "nt", F32, tm, D_MODEL, D_MODEL, "mm_dy")
    dw_o = _mm(y, dz, "tn", BF16, D_MODEL, 1024, tm, "mm_dwo")
    do, dhm, du1, psum = _post_bwd(dy, o_f, o_b, h_f, h_b, u, hgw, mlw, nbc)
    if exchange:
        dzf_f, dzq, dv_a, dlb_f, dw_o = _hgrn_bwd(u, lower_f, hist_f, do, None, ncc, False,
                                                   a2a=dw_o.reshape(N_DEV, D_MODEL // N_DEV, D_MODEL))
    else:
        dzf_f, dzq, dv_a, dlb_f = _hgrn_bwd(u, lower_f, hist_f, do, None, ncc, False)
    dzf_b, dzq, dv_a, dlb_b = _hgrn_bwd(u, lower_b, hist_b, do, (dzq, dv_a), ncc, True)
    dqk, dv_m, dg, _ = _mlstm_bwd(cpre, u, gcol, grow, bias_c, bias_r, ch_f, nh_f, mh_f, dhm, None, ncc, False)
    dqk, dv_m, dg, gsum = _mlstm_bwd(cpre, u, gcol, grow, bias_c, bias_r, ch_b, nh_b, mh_b, dhm, (dqk, dv_m, dg), ncc, True)
    du5, gconvw, gconvb = _conv_bwd(dqk, u, w9, tc)
    du = jnp.concatenate([dzq.astype(BF16), dzf_f, dzf_b, dv_a.astype(BF16), du1[:, 0:1024], du5, dv_m.astype(BF16),
                          du1[:, 1024:3072], dg.astype(BF16)], axis=1)
    dwt_u = _mm(du, hc, "tn", BF16, 1152, D_MODEL, ROWS, "mm_dwu")
    tmh = _pick(tt, (1088, 768, 512, 256))
    if exchange:
        dh, dwt_u = _mm(du, wt_u, "nn", F32, tmh, D_MODEL, 1152, "mm_dh",
                        a2a=dwt_u[:N_IN].reshape(N_DEV, N_IN // N_DEV, D_MODEL))
    else:
        dh = _mm(du, wt_u, "nn", F32, tmh, D_MODEL, 1152, "mm_dh")
    gx, msum = _modulate_bwd(dh, xc_all, modp, dxa, nbc)

    zero_row = jnp.zeros((1, D_MODEL), F32)
    small = dict(
        dmodx=jnp.concatenate([msum[2:3], msum[3:4], fsum[0:1]], axis=0),
        dmodc=jnp.concatenate([msum[0:1], msum[1:2], zero_row], axis=0),
        gconvw=gconvw, gconvb=gconvb, dlower=jnp.concatenate([dlb_f, dlb_b], axis=0),
        ghgw=psum[0:1], gmlw=psum[1:2], glng=fsum[1:2], glnb=fsum[2:3], losssq=fsum[3:4],
        ggate=jnp.concatenate([gsum, jnp.zeros((7, LANE), F32)], axis=0))
    return gx, dwt_u, dw_o, small


def _pack_small(small):
    return jnp.concatenate([small[name].reshape(rows, LANE) for name, rows in _PACK], axis=0)


def _flat_pad(a, rows):
    flat = a.reshape(-1)
    return jnp.pad(flat, (0, rows * LANE - flat.shape[0])).reshape(rows, LANE)


def kernel(x, c, ctx, c_ctx, w_mod, b_mod, w_in, conv_w, conv_b, hg_lb, ml_gate_b, hg_norm_w, ml_norm_w, w_out, ln_g, ln_b, loss_target, m_c_ctx, m_w_mod, m_b_mod, m_w_in, m_conv_w, m_conv_b, m_hg_lb, m_ml_gate_b, m_hg_norm_w, m_ml_norm_w, m_w_out, m_ln_g, m_ln_b, v_c_ctx, v_w_mod, v_b_mod, v_w_in, v_conv_w, v_conv_b, v_hg_lb, v_ml_gate_b, v_hg_norm_w, v_ml_norm_w, v_w_out, v_ln_g, v_ln_b):
    px, py, pc = _position()
    me = 4 * px + 2 * py + pc
    d = D_MODEL
    tc = ctx.shape[1]
    n_mod = w_mod.shape[2]
    n_wi = w_in.shape[2]
    n_cv = conv_w.shape[3]
    n_lb = hg_lb.shape[2]

    pack0 = jnp.concatenate([c.reshape(-1), conv_w.reshape(-1), hg_lb.reshape(-1)]).reshape(1, -1)
    g0 = _all_gather(pack0, "gather_small_inputs")[:, 0, :]
    c_all = g0[:, :d]
    w9 = jnp.transpose(g0[:, d:d + 9 * n_cv].reshape(N_DEV, 9, n_cv), (1, 0, 2)).reshape(9, N_DEV * n_cv)
    lb4 = jnp.transpose(g0[:, d + 9 * n_cv:].reshape(N_DEV, 4, n_lb), (1, 0, 2)).reshape(4, N_DEV * n_lb)
    lower = _lower_fwd(lb4)

    cs = jnp.concatenate([c_all, c_ctx.reshape(1, d), jnp.zeros((7, d), F32)], axis=0)
    b_cols = lax.dynamic_slice(b_mod, (0, me * n_mod), (1, n_mod))
    slab = _mod_fwd(cs, w_mod[0], b_cols)
    mod_all = jnp.transpose(_all_gather(slab, "gather_mod"), (1, 0, 2)).reshape(16, N_DEV * n_mod)
    mod_x = lax.dynamic_slice(mod_all, (me, 0), (1, 3 * d)).reshape(3, d)
    modp = jnp.stack([mod_all[8].reshape(3, d), mod_x])

    wt = _all_gather(w_in[0].T.astype(BF16), "gather_w_in").reshape(N_DEV * n_wi, d)
    wt_u = jnp.pad(wt, ((0, N_U - N_DEV * n_wi), (0, 0)))
    w_o = _all_gather(w_out[0].astype(BF16), "gather_w_out").reshape(d, d)

    xc_all = jnp.concatenate([ctx[0], x[0]], axis=0)
    gx, recv_wi, recv_wo, small = _local_step(xc_all, loss_target[0], modp, lower, wt_u, w_o, w9, conv_b, ml_gate_b[0],
                                              hg_norm_w, ml_norm_w, ln_g, ln_b, tc, True)
    g_wi = _sum8(recv_wi, 256, "sum_dw_in").T
    d_wi, nm_wi, nv_wi = _adamw(w_in[0], g_wi, m_w_in[0], v_w_in[0], 256, "adamw_w_in")
    g_wo, d_wo, nm_wo, nv_wo = _rs_adamw(recv_wo, w_out[0], m_w_out[0], v_w_out[0], 64, "adamw_w_out")

    packs = _all_gather(_pack_small(small), "gather_small_grads")
    total = _reduce8(packs, "reduce_small_grads")
    offs = _pack_offsets()
    piece = lambda name: total[offs[name][0]:offs[name][0] + offs[name][1]]
    g_bmod, g_lb0, g_lb1, loss8 = _small_finish(total, lower[0:2].reshape(16, LANE), float(d))

    ox = offs["dmodx"][0]
    dmodx_all = packs[:, ox:ox + 48, :].reshape(N_DEV, 3 * d)
    dmodc_tot = piece("dmodc").reshape(1, 3 * d)
    d9 = jnp.concatenate([dmodx_all, dmodc_tot, jnp.zeros((7, 3 * d), F32)], axis=0)
    d9_cols = lax.dynamic_slice(d9, (0, me * n_mod), (16, n_mod))
    g_wmod, pc_part = _mod_bwd_w(cs, d9_cols, w_mod[0])
    c_ctx8 = jnp.concatenate([c_ctx.reshape(1, d), jnp.zeros((7, d), F32)], axis=0)
    g_cctx = _cctx_grad(_all_gather(pc_part, "gather_cctx"), c_ctx8)[0]
    d_wmod, nm_wmod, nv_wmod = _adamw(w_mod[0], g_wmod, m_w_mod[0], v_w_mod[0], 256, "adamw_w_mod")

    g_convw_full = piece("gconvw").reshape(9, d)
    g_convw = lax.dynamic_slice(g_convw_full, (0, me * n_cv), (9, n_cv)).reshape(conv_w.shape)
    lb_full = jnp.stack([jnp.stack([g_lb0[0:8].reshape(-1), g_lb1[0:8].reshape(-1)]),
                         jnp.stack([g_lb0[8:16].reshape(-1), g_lb1[8:16].reshape(-1)])])
    g_hglb = lax.dynamic_slice(lb_full, (0, 0, me * n_lb), (2, 2, n_lb))
    grads = dict(
        c_ctx=g_cctx, b_mod=g_bmod.reshape(b_mod.shape), conv_w=g_convw, conv_b=piece("gconvb").reshape(conv_b.shape),
        hg_lb=g_hglb, ml_gate_b=piece("ggate")[0, :16].reshape(ml_gate_b.shape),
        hg_norm_w=piece("ghgw").reshape(hg_norm_w.shape), ml_norm_w=piece("gmlw").reshape(ml_norm_w.shape),
        ln_g=piece("glng").reshape(ln_g.shape), ln_b=piece("glnb").reshape(ln_b.shape))
    params = dict(c_ctx=(c_ctx, m_c_ctx, v_c_ctx), b_mod=(b_mod, m_b_mod, v_b_mod), conv_w=(conv_w, m_conv_w, v_conv_w),
                  conv_b=(conv_b, m_conv_b, v_conv_b), hg_lb=(hg_lb, m_hg_lb, v_hg_lb),
                  ml_gate_b=(ml_gate_b, m_ml_gate_b, v_ml_gate_b), hg_norm_w=(hg_norm_w, m_hg_norm_w, v_hg_norm_w),
                  ml_norm_w=(ml_norm_w, m_ml_norm_w, v_ml_norm_w), ln_g=(ln_g, m_ln_g, v_ln_g), ln_b=(ln_b, m_ln_b, v_ln_b))
    names = list(params)
    rows_of = {n: -(-params[n][0].size // LANE) for n in names}
    rows_tot = -(-sum(rows_of.values()) // 8) * 8
    cat = lambda arrs: jnp.concatenate(
        [_flat_pad(a, rows_of[n]) for n, a in zip(names, arrs)]
        + [jnp.ones((rows_tot - sum(rows_of.values()), LANE), F32)], axis=0)
    d_s, m_s, v_s = _adamw(cat([params[n][0] for n in names]), cat([grads[n] for n in names]),
                           cat([params[n][1] for n in names]), cat([params[n][2] for n in names]), rows_tot, "adamw_small")
    delta, new_m, new_v, off = {}, {}, {}, 0
    for n in names:
        shape, size = params[n][0].shape, params[n][0].size
        take = lambda a: a[off:off + rows_of[n]].reshape(-1)[:size].reshape(shape)
        delta[n], new_m[n], new_v[n] = take(d_s), take(m_s), take(v_s)
        off += rows_of[n]
    grads.update(w_mod=g_wmod[None], w_in=g_wi[None], w_out=g_wo[None])
    delta.update(w_mod=d_wmod[None], w_in=d_wi[None], w_out=d_wo[None])
    new_m.update(w_mod=nm_wmod[None], w_in=nm_wi[None], w_out=nm_wo[None])
    new_v.update(w_mod=nv_wmod[None], w_in=nv_wi[None], w_out=nv_wo[None])

    order = ("c_ctx", "w_mod", "b_mod", "w_in", "conv_w", "conv_b", "hg_lb", "ml_gate_b", "hg_norm_w", "ml_norm_w",
             "w_out", "ln_g", "ln_b")
    return (loss8[0, 0], gx[None], *[grads[n] for n in order], *[delta[n] for n in order],
            *[new_m[n] for n in order], *[new_v[n] for n in order])
```

```python
import functools

import jax
import jax.numpy as jnp
from jax import lax
from jax.experimental import pallas as pl
from jax.experimental.pallas import tpu as pltpu

F32 = jnp.float32
BF16 = jnp.bfloat16

D_MODEL = 2048
W_A = 1024
W_B = 1024
HG_HEADS = 8
HG_D = 128
ML_HEADS = 4
ML_D = 256
CHUNK = 64
N_IN = 10256
LANE = 128
N_U = 81 * LANE
N_DEV = 8
ALPHA = 2.0 ** 0.25
LN_EPS = 1e-5
NORM_EPS = 1e-6
ADAM_LR, ADAM_B1, ADAM_B2, ADAM_EPS, ADAM_WD, ADAM_STEP = 0.001, 0.9, 0.999, 1e-08, 0.01, 10
VMEM_CAP = 60 * 1024 * 1024

SEG_AQ, SEG_AFF, SEG_AFB, SEG_AI, SEG_AZ = range(5)
BLK_QK = 40
SEG_BV, SEG_BO, SEG_BZ = 7, 8, 9
BLK_GATE = 80

MESH = pl.DeviceIdType.MESH


def _vmem(nbytes):
    return pltpu.CompilerParams(vmem_limit_bytes=int(min(VMEM_CAP, max(nbytes, 16 * 1024 * 1024))))


def _sigmoid(x):
    return 1.0 / (1.0 + jnp.exp(-x))


def _silu(x):
    return x * _sigmoid(x)


def _dsilu(x):
    s = _sigmoid(x)
    return s * (1.0 + x * (1.0 - s))


def _bdot(a, b, dims):
    return lax.dot_general(a.astype(BF16), b.astype(BF16), (dims, ((), ())), preferred_element_type=F32)


def _nn(a, b):
    return _bdot(a, b, ((1,), (0,)))


def _nt(a, b):
    return _bdot(a, b, ((1,), (1,)))


def _tn(a, b):
    return _bdot(a, b, ((0,), (0,)))


def _exact_nn(a, b):
    return lax.dot_general(a, b, (((1,), (0,)), ((), ())), precision=lax.Precision.HIGHEST,
                           preferred_element_type=F32)


def _exact_tn(a, b):
    return lax.dot_general(a, b, (((0,), (0,)), ((), ())), precision=lax.Precision.HIGHEST,
                           preferred_element_type=F32)


def _tri(rev):
    t = lax.broadcasted_iota(jnp.int32, (CHUNK, CHUNK), 0)
    s = lax.broadcasted_iota(jnp.int32, (CHUNK, CHUNK), 1)
    return (s >= t) if rev else (s <= t)


def _eye():
    t = lax.broadcasted_iota(jnp.int32, (CHUNK, CHUNK), 0)
    s = lax.broadcasted_iota(jnp.int32, (CHUNK, CHUNK), 1)
    return (s == t).astype(F32)


def _row_to_col(row):
    return jnp.sum(_eye() * row, axis=1, keepdims=True)


def _last_onehot(rev):
    t = lax.broadcasted_iota(jnp.int32, (CHUNK, 1), 0)
    return (t == (0 if rev else CHUNK - 1)).astype(F32)


def _head_slices(width, n_heads):
    hd = width // n_heads
    return [slice(h * hd, (h + 1) * hd) for h in range(n_heads)]


def _scan_sum(x, rev):
    n = x.shape[0]
    t = lax.broadcasted_iota(jnp.int32, x.shape, 0)
    s = 1
    while s < n:
        if rev:
            x = x + jnp.where(t < n - s, pltpu.roll(x, n - s, 0), 0.0)
        else:
            x = x + jnp.where(t >= s, pltpu.roll(x, s, 0), 0.0)
        s *= 2
    return x


def _dot3(a, b, dims):
    a_hi, b_hi = a.astype(BF16), b.astype(BF16)
    a_lo, b_lo = (a - a_hi.astype(F32)).astype(BF16), (b - b_hi.astype(F32)).astype(BF16)
    dot = lambda x, y: lax.dot_general(x, y, (dims, ((), ())), preferred_element_type=F32)
    return dot(a_hi, b_hi) + (dot(a_hi, b_lo) + dot(a_lo, b_hi))


def _hg_common(zq, zf, lb, rev):
    q = _silu(zq)
    sg = _sigmoid(zf)
    f = lb + (1.0 - lb) * sg
    g = jnp.log(f)
    k = 1.0 - f
    b = _scan_sum(g, rev)
    b_last = jnp.sum(g, axis=0, keepdims=True)
    r = b[CHUNK // 2:CHUNK // 2 + 1, :]
    e_up = jnp.exp(b - r)
    e_dn = jnp.exp(r - b)
    e_b = e_up * jnp.exp(r)
    e_lb = e_dn * jnp.exp(b_last - r)
    return dict(q=q, sg=sg, f=f, k=k, e_up=e_up, e_dn=e_dn, e_b=e_b, e_lb=e_lb, e_last=jnp.exp(b_last),
                q_t=q * e_up, k_t=k * e_dn, q_s=q * e_b, k_h=k * e_lb, tri=_tri(rev).astype(F32))


def hg_chunk_fwd(zq, zf, v, lb, st, rev):
    c = _hg_common(zq, zf, lb, rev)
    hs = _head_slices(zq.shape[1], zq.shape[1] // HG_D)
    s = [_nt(c["q_t"][:, sl], c["k_t"][:, sl]) for sl in hs]
    oi = [_nt(c["q_s"][:, sl], st[sl, :]) for sl in hs]
    ds = [_tn(v[:, sl], c["k_h"][:, sl]) for sl in hs]
    oa = [_nn(c["tri"] * s_h, v[:, sl]) for s_h, sl in zip(s, hs)]
    o = jnp.concatenate([x + y for x, y in zip(oi, oa)], axis=1)
    st_new = jnp.concatenate([st[sl, :] * c["e_last"][:, sl] + d for sl, d in zip(hs, ds)], axis=0)
    return o, st_new


def hg_chunk_bwd(zq, zf, v, lb, st, do, dst_new, rev):
    c = _hg_common(zq, zf, lb, rev)
    hs = _head_slices(zq.shape[1], zq.shape[1] // HG_D)
    tri, q_t, k_t, q_s, k_h = c["tri"], c["q_t"], c["k_t"], c["q_s"], c["k_h"]
    s = [_nt(q_t[:, sl], k_t[:, sl]) for sl in hs]
    da = [tri * _nt(do[:, sl], v[:, sl]) for sl in hs]
    dq_s = [_nn(do[:, sl], st[sl, :]) for sl in hs]
    dk_h = [_nn(v[:, sl], dst_new[sl, :]) for sl in hs]
    dv_s = [_nt(k_h[:, sl], dst_new[sl, :]) for sl in hs]
    dst_q = [_tn(do[:, sl], q_s[:, sl]) for sl in hs]
    dq_t = [_dot3(da_h, k_t[:, sl], ((1,), (0,))) for da_h, sl in zip(da, hs)]
    dk_t = [_dot3(da_h, q_t[:, sl], ((0,), (0,))) for da_h, sl in zip(da, hs)]
    dv_a = [_tn(tri * s_h, do[:, sl]) for s_h, sl in zip(s, hs)]
    cat = lambda parts: jnp.concatenate(parts, axis=1)
    dq_s, dk_h, dq_t, dk_t = cat(dq_s), cat(dk_h), cat(dq_t), cat(dk_t)
    dv = cat([x + y for x, y in zip(dv_a, dv_s)])
    dst = jnp.concatenate([dst_new[sl, :] * c["e_last"][:, sl] + d for sl, d in zip(hs, dst_q)], axis=0)
    dq = dq_s * c["e_b"] + dq_t * c["e_up"]
    dk = dk_t * c["e_dn"] + dk_h * c["e_lb"]
    db = c["q"] * dq - c["k"] * dk
    ss = cat([jnp.sum(dst_new[sl, :] * st[sl, :], axis=0, keepdims=True) for sl in hs])
    d_all = jnp.sum(dk_h * k_h, axis=0, keepdims=True) + c["e_last"] * ss
    dg = _scan_sum(db, not rev) + d_all
    dzq = dq * _dsilu(zq)
    df = dg / c["f"] - dk
    dzf = df * (1.0 - lb) * c["sg"] * (1.0 - c["sg"])
    dlb = jnp.sum(df * (1.0 - c["sg"]), axis=0, keepdims=True)
    return dzq, dzf, dv, dlb, dst


def _log_sigmoid(x):
    return jnp.minimum(x, 0.0) - jnp.log(1.0 + jnp.exp(-jnp.abs(x)))


def _each(fn, *lists):
    return [fn(*xs) for xs in zip(*lists)]


def _bf(xs):
    return [x.astype(BF16) for x in xs]


def _ml_forward_parts(qp, kp, v, gates, c, n, m, rev):
    hs = _head_slices(qp.shape[1], qp.shape[1] // ML_D)
    q_all = _silu(qp)
    k_all = _silu(kp) * (ML_D ** -0.5)
    q = [q_all[:, sl] for sl in hs]
    k = [k_all[:, sl] for sl in hs]
    vv = [v[:, sl] for sl in hs]
    cc = [c[sl, :] for sl in hs]
    tri_b = _tri(rev)
    tri = tri_b.astype(F32)
    tri_t = _tri(not rev).astype(F32)
    e_last = _last_onehot(rev)
    qb, kb, vb, cb = _bf(q), _bf(k), _bf(vv), _bf(cc)
    qk = _each(_nt, qb, kb)
    qc = _each(_nt, qb, cb)
    parts = []
    for (gi_c, gi_r, gf_c, gf_r), m_h in zip(gates, m):
        lf_c, lf_r = _log_sigmoid(gf_c), _log_sigmoid(gf_r)
        b_c = jnp.sum(tri * lf_r, axis=1, keepdims=True)
        b_r = jnp.sum(tri_t * lf_c, axis=0, keepdims=True)
        log_w = jnp.where(tri_b, b_c - b_r + gi_r, -jnp.inf)
        m_inter = b_c + m_h
        m_t = jnp.maximum(m_inter, jnp.max(log_w, axis=1, keepdims=True))
        m_new = jnp.sum(m_t * e_last, axis=0, keepdims=True)
        b_last = jnp.sum(b_c * e_last, axis=0, keepdims=True)
        parts.append(dict(a=jnp.exp(m_inter - m_t), p=jnp.exp(log_w - m_t), floor=jnp.exp(-m_t), m_new=m_new,
                          ws=jnp.exp(b_last - b_c + gi_c - m_new), decay=jnp.exp(b_last + m_h - m_new), gf_c=gf_c))
    w = [pt["p"] * x for pt, x in zip(parts, qk)]
    wb = _bf(w)
    wv = _each(_nn, wb, vb)
    for pt, q_h, n_h, w_h, qc_h, wv_h in zip(parts, q, n, w, qc, wv):
        qn = jnp.sum(q_h * n_h, axis=1, keepdims=True)
        num = pt["a"] * qc_h + wv_h
        den = pt["a"] * qn + jnp.sum(w_h, axis=1, keepdims=True)
        pt.update(qn=qn, num=num, den=den, rinv=1.0 / jnp.maximum(jnp.abs(den), pt["floor"]), w=w_h, qc=qc_h)
    return hs, q, k, vv, cc, tri, parts, dict(q=qb, k=kb, v=vb, c=cb, w=wb)


def ml_chunk_fwd(qp, kp, v, gates, c, n, m, rev):
    hs, q, k, vv, cc, tri, parts, bf = _ml_forward_parts(qp, kp, v, gates, c, n, m, rev)
    h = jnp.concatenate([pt["num"] * pt["rinv"] for pt in parts], axis=1)
    upd = _each(_tn, [pt["ws"] * v_h for pt, v_h in zip(parts, vv)], bf["k"])
    c_new = jnp.concatenate([pt["decay"] * c_h + u for pt, c_h, u in zip(parts, cc, upd)], axis=0)
    n_new = [pt["decay"] * n_h + jnp.sum(pt["ws"] * k_h, axis=0, keepdims=True) for pt, n_h, k_h in zip(parts, n, k)]
    return h, c_new, n_new, [pt["m_new"] for pt in parts]


def ml_chunk_bwd(qp, kp, v, gates, c, n, m, dh, dc_new, dn_new, rev):
    hs, q, k, vv, cc, tri, parts, bf = _ml_forward_parts(qp, kp, v, gates, c, n, m, rev)
    dcn = [dc_new[sl, :] for sl in hs]
    dcb = _bf(dcn)
    dnum, dden = [], []
    for pt, sl in zip(parts, hs):
        dh_h = dh[:, sl]
        h = pt["num"] * pt["rinv"]
        signed_live = jnp.where(jnp.abs(pt["den"]) > pt["floor"], jnp.where(pt["den"] >= 0.0, 1.0, -1.0), 0.0)
        dnum.append(dh_h * pt["rinv"])
        dden.append(-jnp.sum(dh_h * h, axis=1, keepdims=True) * pt["rinv"] * signed_live)
    dnb = _bf(dnum)
    dw = [x + y for x, y in zip(_each(_nt, dnb, bf["v"]), dden)]
    kdc = _each(_nt, bf["k"], dcb)
    vdc = _each(_nn, bf["v"], dcb)
    dqk = [x * pt["p"] for x, pt in zip(dw, parts)]
    adn = [pt["a"] * x for pt, x in zip(parts, dnum)]
    dqkb, adnb = _bf(dqk), _bf(adn)
    dv_w = _each(_tn, bf["w"], dnb)
    dq_k = _each(_nn, dqkb, bf["k"])
    dq_c = _each(_nn, adnb, bf["c"])
    dk_q = _each(_tn, dqkb, bf["q"])
    dc_q = _each(_tn, adnb, bf["q"])
    dq, dk, dv, dgi, dgf, dc, dn = [], [], [], [], [], [], []
    for i, pt in enumerate(parts):
        a, ws, decay = pt["a"], pt["ws"], pt["decay"]
        add = a * dden[i]
        e = dw[i] * pt["w"]
        dv.append(dv_w[i] + ws * kdc[i])
        dq.append(dq_k[i] + dq_c[i] + add * n[i])
        dk.append(dk_q[i] + ws * vdc[i] + ws * dn_new[i])
        alpha = (jnp.sum(dnum[i] * pt["qc"], axis=1, keepdims=True) + dden[i] * pt["qn"]) * a
        omega = (jnp.sum(vdc[i] * k[i], axis=1, keepdims=True) + jnp.sum(k[i] * dn_new[i], axis=1, keepdims=True)) * ws
        delta = decay * (jnp.sum(jnp.sum(dcn[i] * cc[i], axis=1, keepdims=True), axis=0, keepdims=True)
                         + jnp.sum(dn_new[i] * n[i], axis=1, keepdims=True))
        dc.append(decay * dcn[i] + dc_q[i])
        dn.append(decay * dn_new[i] + jnp.sum(add * q[i], axis=0, keepdims=True))
        e_rows = jnp.sum(e, axis=1, keepdims=True)
        e_cols = _row_to_col(jnp.sum(e, axis=0, keepdims=True))
        dgi.append(e_cols + omega)
        db = e_rows + alpha - e_cols - omega
        tail = jnp.sum(omega, axis=0, keepdims=True) + delta
        dlf = _row_to_col(jnp.sum(tri * db, axis=0, keepdims=True)) + tail
        dgf.append(dlf * (1.0 - _sigmoid(pt["gf_c"])))
    cat = lambda xs: jnp.concatenate(xs, axis=1)
    dqp = cat(dq) * _dsilu(qp)
    dkp = cat(dk) * (ML_D ** -0.5) * _dsilu(kp)
    return dqp, dkp, cat(dv), dgi, dgf, jnp.concatenate(dc, axis=0), dn


def _pick(n, prefs):
    for p in prefs:
        if n % p == 0:
            return p
    raise ValueError(f"no tile for {n} among {prefs}")


def _position():
    return lax.axis_index("x"), lax.axis_index("y"), lax.axis_index("c")


class _Ride:
    def __init__(self, kind, x, cols=None, into=None):
        self.kind, self.x, self.cols, self.into = kind, x, cols, into
        r, c = x.shape[-2:]
        self.out_shape = jax.ShapeDtypeStruct((N_DEV, r, c if cols is None else cols[1]), x.dtype)
        self.width = c

    def _copies(self, x_ref, out_ref, send_sems, recv_sems, local_sem):
        px, py, pc = _position()
        me = 4 * px + 2 * py + pc
        src = (lambda slot: x_ref) if self.kind == "gather" else (lambda slot: x_ref.at[slot])
        dst = ((lambda slot: out_ref.at[slot]) if self.cols is None
               else (lambda slot: out_ref.at[slot, :, pl.ds(self.cols[0], self.width)]))
        mine = pltpu.make_async_copy(src(me), dst(me), local_sem)
        sends, recvs = [], []
        for k, (fx, fy, fc) in enumerate([(1, 0, 0), (0, 1, 0), (1, 1, 0), (1, 0, 1), (0, 1, 1), (1, 1, 1), (0, 0, 1)]):
            qx, qy, qc = (1 - px if fx else px), (1 - py if fy else py), (1 - pc if fc else pc)
            peer = 4 * qx + 2 * qy + qc
            sends.append(pltpu.make_async_remote_copy(
                src_ref=src(peer), dst_ref=dst(me), send_sem=send_sems.at[k], recv_sem=recv_sems.at[k],
                device_id=(qx, qy, qc), device_id_type=MESH))
            recvs.append(pltpu.make_async_remote_copy(
                src_ref=src(me), dst_ref=dst(peer), send_sem=send_sems.at[k], recv_sem=recv_sems.at[k],
                device_id=(qx, qy, qc), device_id_type=MESH))
        return mine, sends, recvs

    def start(self, *refs):
        mine, sends, _ = self._copies(*refs)
        mine.start()
        for cp in sends:
            cp.start()

    def wait(self, *refs):
        mine, sends, recvs = self._copies(*refs)
        for cp in recvs:
            cp.wait_recv()
        for cp in sends:
            cp.wait_send()
        mine.wait()

    def operands(self):
        return [self.x] + ([self.into] if self.into is not None else [])


_RIDE_SCRATCH = [pltpu.SemaphoreType.DMA((7,)), pltpu.SemaphoreType.DMA((7,)), pltpu.SemaphoreType.DMA]
_ANY = pl.BlockSpec(memory_space=pl.ANY)


def _mm(a, b, mode, out_dtype, tm, tn, tk, name, ride=None, n_cols=None, n_off=0):
    if mode == "nn":
        (m, k), (k2, n) = a.shape, b.shape
    elif mode == "nt":
        (m, k), (n, k2) = a.shape, b.shape
    else:
        (k, m), (k2, n) = a.shape, b.shape
    n = n if n_cols is None else n_cols
    assert k == k2 and m % tm == 0 and n % tn == 0 and k % tk == 0, (a.shape, b.shape, mode, tm, tn, tk)
    nk = k // tk
    joff = n_off * (n // tn)
    dims = {"nn": ((1,), (0,)), "nt": ((1,), (1,)), "tn": ((0,), (0,))}[mode]
    a_spec = (pl.BlockSpec((tk, tm), lambda j, i, kk: (kk, i)) if mode == "tn"
              else pl.BlockSpec((tm, tk), lambda j, i, kk: (i, kk)))
    b_spec = (pl.BlockSpec((tn, tk), lambda j, i, kk: (j + joff, kk)) if mode == "nt"
              else pl.BlockSpec((tk, tn), lambda j, i, kk: (kk, j + joff)))

    grid = (n // tn, m // tm, nk)
    n_ride_in = len(ride.operands()) if ride is not None else 0

    def body(a_ref, b_ref, *rest):
        if ride is not None:
            x_ref = rest[0]
            o_ref, got_ref, acc_ref = rest[n_ride_in:n_ride_in + 3]
            comm = (x_ref, got_ref) + tuple(rest[n_ride_in + 3:])
        else:
            o_ref, acc_ref = rest
        kk = pl.program_id(2)
        step = (pl.program_id(0) * grid[1] + pl.program_id(1)) * nk + kk
        if ride is not None:
            @pl.when(step == 0)
            def _():
                ride.start(*comm)

        part = lax.dot_general(a_ref[...], b_ref[...], (dims, ((), ())), preferred_element_type=F32)

        @pl.when(kk == 0)
        def _():
            acc_ref[...] = part

        @pl.when(kk > 0)
        def _():
            acc_ref[...] += part

        @pl.when(kk == nk - 1)
        def _():
            o_ref[...] = acc_ref[...].astype(o_ref.dtype)

        if ride is not None:
            @pl.when(step == grid[0] * grid[1] * nk - 1)
            def _():
                ride.wait(*comm)

    osz = jnp.dtype(out_dtype).itemsize
    need = 2 * (tm * tk * a.dtype.itemsize + tk * tn * b.dtype.itemsize + tm * tn * osz) + tm * tn * 4
    o_spec = pl.BlockSpec((tm, tn), lambda j, i, kk: (i, j))
    o_shape = jax.ShapeDtypeStruct((m, n), out_dtype)
    extra = ride is not None
    return pl.pallas_call(
        body, name=name, grid=grid,
        in_specs=[a_spec, b_spec] + [_ANY] * n_ride_in,
        out_specs=[o_spec, _ANY] if extra else o_spec,
        out_shape=[o_shape, ride.out_shape] if extra else o_shape,
        scratch_shapes=[pltpu.VMEM((tm, tn), F32)] + (_RIDE_SCRATCH if extra else []),
        input_output_aliases={3: 1} if extra and ride.into is not None else {},
        compiler_params=_vmem(need + (12 << 20)),
    )(a, b, *(ride.operands() if extra else []))


ROWS = 256


def _ln_stats(x):
    mu = jnp.mean(x, axis=-1, keepdims=True)
    xc = x - mu
    var = jnp.mean(xc * xc, axis=-1, keepdims=True)
    rstd = lax.rsqrt(var + LN_EPS)
    return xc * rstd, rstd


def _modulate_fwd(xc_all, modp, nbc):
    tt, d = xc_all.shape

    def body(x_ref, mod_ref, o_ref):
        n, _ = _ln_stats(x_ref[...])
        o_ref[...] = (n * (1.0 + mod_ref[0, 1:2, :]) + mod_ref[0, 0:1, :]).astype(BF16)

    return pl.pallas_call(
        body, name="modulate_fwd", grid=(tt // ROWS,),
        in_specs=[pl.BlockSpec((ROWS, d), lambda i: (i, 0)),
                  pl.BlockSpec((1, 3, d), lambda i: (jnp.where(i >= nbc, 1, 0), 0, 0))],
        out_specs=pl.BlockSpec((ROWS, d), lambda i: (i, 0)),
        out_shape=jax.ShapeDtypeStruct((tt, d), BF16),
    )(xc_all, modp)


def _modulate_bwd(dh, xc_all, modp, dxa, nbc):
    tt, d = xc_all.shape
    t = dxa.shape[0]

    def body(dh_ref, x_ref, mod_ref, dxa_ref, gx_ref, sum_ref):
        i = pl.program_id(0)
        n, rstd = _ln_stats(x_ref[...])
        g = dh_ref[...]
        dn = g * (1.0 + mod_ref[0, 1:2, :])
        dx = rstd * (dn - jnp.mean(dn, axis=-1, keepdims=True) - n * jnp.mean(dn * n, axis=-1, keepdims=True))
        gx_ref[...] = dx + dxa_ref[...]
        dshift = jnp.sum(g, axis=0, keepdims=True)
        dscale = jnp.sum(g * n, axis=0, keepdims=True)

        @pl.when(i == 0)
        def _():
            sum_ref[...] = jnp.zeros_like(sum_ref)

        @pl.when(i < nbc)
        def _():
            sum_ref[0:1, :] += dshift
            sum_ref[1:2, :] += dscale

        @pl.when(i >= nbc)
        def _():
            sum_ref[2:3, :] += dshift
            sum_ref[3:4, :] += dscale

    lat = lambda i: (jnp.maximum(i - nbc, 0), 0)
    return pl.pallas_call(
        body, name="modulate_bwd", grid=(tt // ROWS,),
        in_specs=[pl.BlockSpec((ROWS, d), lambda i: (i, 0)), pl.BlockSpec((ROWS, d), lambda i: (i, 0)),
                  pl.BlockSpec((1, 3, d), lambda i: (jnp.where(i >= nbc, 1, 0), 0, 0)),
                  pl.BlockSpec((ROWS, d), lat)],
        out_specs=[pl.BlockSpec((ROWS, d), lat), pl.BlockSpec((8, d), lambda i: (0, 0))],
        out_shape=[jax.ShapeDtypeStruct((t, d), F32), jax.ShapeDtypeStruct((8, d), F32)],
    )(dh, xc_all, modp, dxa)


def _post_fwd(o_f, o_b, h_f, h_b, u, hgw, mlw, nbc):
    tt = u.shape[0]
    t = tt - nbc * ROWS

    def body(of_ref, ob_ref, hf_ref, hb_ref, az_ref, bo_ref, bz_ref, hgw_ref, mlw_ref, y_ref):
        o = of_ref[...] + ob_ref[...]
        for sl in _head_slices(W_A, HG_HEADS):
            oh = o[:, sl]
            rs = lax.rsqrt(jnp.mean(oh * oh, axis=-1, keepdims=True) + NORM_EPS)
            y_ref[:, sl] = (oh * rs * hgw_ref[:, sl] * _silu(az_ref[:, sl])).astype(BF16)
        hm = hf_ref[...] + hb_ref[...]
        for sl in _head_slices(W_B, ML_HEADS):
            hh = hm[:, sl]
            mu = jnp.mean(hh, axis=-1, keepdims=True)
            hc = hh - mu
            rstd = lax.rsqrt(jnp.mean(hc * hc, axis=-1, keepdims=True) + NORM_EPS)
            out = hc * rstd * mlw_ref[:, sl] * _sigmoid(bo_ref[:, sl]) * _silu(bz_ref[:, sl])
            y_ref[:, W_A + sl.start:W_A + sl.stop] = out.astype(BF16)

    row = lambda i: (i + nbc, 0)
    seg = lambda s: pl.BlockSpec((ROWS, 1024), lambda i: (i + nbc, s))
    wspec = pl.BlockSpec((1, 1024), lambda i: (0, 0))
    return pl.pallas_call(
        body, name="post_fwd", grid=(t // ROWS,),
        in_specs=[pl.BlockSpec((ROWS, 1024), row)] * 4 + [seg(SEG_AZ), seg(SEG_BO), seg(SEG_BZ), wspec, wspec],
        out_specs=pl.BlockSpec((ROWS, 2048), lambda i: (i, 0)),
        out_shape=jax.ShapeDtypeStruct((t, 2048), BF16),
    )(o_f, o_b, h_f, h_b, u, u, u, hgw, mlw)


def _post_bwd(dy, o_f, o_b, h_f, h_b, u, hgw, mlw, nbc):
    tt = u.shape[0]

    def body(dy_ref, of_ref, ob_ref, hf_ref, hb_ref, az_ref, bo_ref, bz_ref, hgw_ref, mlw_ref,
             do_ref, dhm_ref, du_ref, sum_ref):
        i = pl.program_id(0)
        live = jnp.where(i >= nbc, 1.0, 0.0)

        @pl.when(i == 0)
        def _():
            sum_ref[...] = jnp.zeros_like(sum_ref)

        o = of_ref[...] + ob_ref[...]
        for sl in _head_slices(W_A, HG_HEADS):
            oh = o[:, sl]
            rs = lax.rsqrt(jnp.mean(oh * oh, axis=-1, keepdims=True) + NORM_EPS)
            on = oh * rs
            az = az_ref[:, sl]
            dya = dy_ref[:, sl] * live
            doa = dya * _silu(az)
            du_ref[:, sl] = (dya * on * hgw_ref[:, sl] * _dsilu(az)).astype(BF16)
            sum_ref[0:1, sl] += jnp.sum(doa * on, axis=0, keepdims=True)
            don = doa * hgw_ref[:, sl]
            do_ref[:, sl] = rs * (don - on * jnp.mean(don * on, axis=-1, keepdims=True))
        hm = hf_ref[...] + hb_ref[...]
        for sl in _head_slices(W_B, ML_HEADS):
            hh = hm[:, sl]
            mu = jnp.mean(hh, axis=-1, keepdims=True)
            hc = hh - mu
            rstd = lax.rsqrt(jnp.mean(hc * hc, axis=-1, keepdims=True) + NORM_EPS)
            hn = hc * rstd
            hw = hn * mlw_ref[:, sl]
            bo, bz = bo_ref[:, sl], bz_ref[:, sl]
            sbo, sbz = _sigmoid(bo), _silu(bz)
            dyb = dy_ref[:, W_A + sl.start:W_A + sl.stop] * live
            dhw = dyb * sbo * sbz
            du_ref[:, 1024 + sl.start:1024 + sl.stop] = (dyb * hw * sbz * sbo * (1.0 - sbo)).astype(BF16)
            du_ref[:, 2048 + sl.start:2048 + sl.stop] = (dyb * hw * sbo * _dsilu(bz)).astype(BF16)
            sum_ref[1:2, sl] += jnp.sum(dhw * hn, axis=0, keepdims=True)
            dhn = dhw * mlw_ref[:, sl]
            dhm_ref[:, sl] = rstd * (dhn - jnp.mean(dhn, axis=-1, keepdims=True)
                                     - hn * jnp.mean(dhn * hn, axis=-1, keepdims=True))

    row = lambda i: (i, 0)
    seg = lambda s: pl.BlockSpec((ROWS, 1024), lambda i: (i, s))
    wspec = pl.BlockSpec((1, 1024), lambda i: (0, 0))
    return pl.pallas_call(
        body, name="post_bwd", grid=(tt // ROWS,),
        in_specs=[pl.BlockSpec((ROWS, 2048), lambda i: (jnp.maximum(i - nbc, 0), 0))]
        + [pl.BlockSpec((ROWS, 1024), row)] * 4 + [seg(SEG_AZ), seg(SEG_BO), seg(SEG_BZ), wspec, wspec],
        out_specs=[pl.BlockSpec((ROWS, 1024), row), pl.BlockSpec((ROWS, 1024), row),
                   pl.BlockSpec((ROWS, 3072), row), pl.BlockSpec((8, 1024), lambda i: (0, 0))],
        out_shape=[jax.ShapeDtypeStruct((tt, 1024), F32), jax.ShapeDtypeStruct((tt, 1024), F32),
                   jax.ShapeDtypeStruct((tt, 3072), BF16), jax.ShapeDtypeStruct((8, 1024), F32)],
    )(dy, o_f, o_b, h_f, h_b, u, u, u, hgw, mlw)


def _final(z, xc_all, target, modp, ln_g, ln_b, nbc):
    t, d = z.shape

    def body(z_ref, x_ref, tg_ref, mod_ref, g_ref, b_ref, dz_ref, dxa_ref, sum_ref):
        i = pl.program_id(0)
        zz = z_ref[...]
        gate = mod_ref[0, 2:3, :]
        pre = ALPHA * x_ref[...] + gate * zz
        nh, rstd = _ln_stats(pre)
        err = nh * g_ref[...] + b_ref[...] - tg_ref[...]
        dxo = err * (1.0 / d)
        dnh = dxo * g_ref[...]
        dpre = rstd * (dnh - jnp.mean(dnh, axis=-1, keepdims=True) - nh * jnp.mean(dnh * nh, axis=-1, keepdims=True))
        dz_ref[...] = (gate * dpre).astype(BF16)
        dxa_ref[...] = ALPHA * dpre

        @pl.when(i == 0)
        def _():
            sum_ref[...] = jnp.zeros_like(sum_ref)

        sum_ref[0:1, :] += jnp.sum(dpre * zz, axis=0, keepdims=True)
        sum_ref[1:2, :] += jnp.sum(dxo * nh, axis=0, keepdims=True)
        sum_ref[2:3, :] += jnp.sum(dxo, axis=0, keepdims=True)
        sum_ref[3:4, :] += jnp.sum(err * err, axis=0, keepdims=True)

    row = lambda i: (i, 0)
    vec = pl.BlockSpec((1, d), lambda i: (0, 0))
    return pl.pallas_call(
        body, name="final_ln_loss", grid=(t // ROWS,),
        in_specs=[pl.BlockSpec((ROWS, d), row), pl.BlockSpec((ROWS, d), lambda i: (i + nbc, 0)),
                  pl.BlockSpec((ROWS, d), row), pl.BlockSpec((1, 3, d), lambda i: (1, 0, 0)), vec, vec],
        out_specs=[pl.BlockSpec((ROWS, d), row), pl.BlockSpec((ROWS, d), row), pl.BlockSpec((8, d), lambda i: (0, 0))],
        out_shape=[jax.ShapeDtypeStruct((t, d), BF16), jax.ShapeDtypeStruct((t, d), F32),
                   jax.ShapeDtypeStruct((8, d), F32)],
    )(z, xc_all, target, modp, ln_g, ln_b)


GRID_W = 64


def _shift(x, s, ok):
    n = x.shape[0]
    return jnp.where(ok, pltpu.roll(x, s % n, 0), 0.0)


def _grid_masks(n):
    t = lax.broadcasted_iota(jnp.int32, (n, LANE), 0)
    col = t & (GRID_W - 1)
    return dict(left=col >= 1, right=col <= GRID_W - 2, up=t >= GRID_W, down=t < n - GRID_W)


def _seq_masks(n):
    t = lax.broadcasted_iota(jnp.int32, (n, LANE), 0)
    return dict(left=t >= 1, right=t <= n - 2)


def _conv_fwd(u, w9, cb, tc):
    tt = u.shape[0]
    t = tt - tc

    def body(u_ref, w_ref, b_ref, o_ref):
        w = [w_ref[r:r + 1, :] for r in range(9)]
        xc = u_ref[0:tc, :]
        ms = _seq_masks(tc)
        o_ref[0:tc, :] = (w[3] * _shift(xc, 1, ms["left"]) + w[4] * xc + w[5] * _shift(xc, -1, ms["right"])
                          + b_ref[...])
        x = u_ref[tc:tt, :]
        mg = _grid_masks(t)
        taps = (_shift(x, 1, mg["left"]), x, _shift(x, -1, mg["right"]))
        rows = [w[3 * i] * taps[0] + w[3 * i + 1] * taps[1] + w[3 * i + 2] * taps[2] for i in range(3)]
        o_ref[tc:tt, :] = (rows[1] + _shift(rows[0], GRID_W, mg["up"]) + _shift(rows[2], -GRID_W, mg["down"])
                           + b_ref[...])

    return pl.pallas_call(
        body, name="conv_fwd", grid=(2048 // LANE,),
        in_specs=[pl.BlockSpec((tt, LANE), lambda j: (0, BLK_QK + j)), pl.BlockSpec((9, LANE), lambda j: (0, j)),
                  pl.BlockSpec((1, LANE), lambda j: (0, j))],
        out_specs=pl.BlockSpec((tt, LANE), lambda j: (0, j)),
        out_shape=jax.ShapeDtypeStruct((tt, 2048), F32),
        compiler_params=_vmem(40 * tt * LANE * 4),
    )(u, w9, cb)


def _conv_bwd(dcp, u, w9, tc):
    tt = u.shape[0]
    t = tt - tc

    def body(d_ref, u_ref, w_ref, du_ref, gw_ref, gb_ref):
        w = [w_ref[r:r + 1, :] for r in range(9)]
        csum = lambda a: jnp.sum(a, axis=0, keepdims=True)
        dc = d_ref[0:tc, :]
        xc = u_ref[0:tc, :]
        ms = _seq_masks(tc)
        du_ref[0:tc, :] = (w[3] * _shift(dc, -1, ms["right"]) + w[4] * dc + w[5] * _shift(dc, 1, ms["left"])).astype(BF16)
        gmid = [csum(dc * _shift(xc, 1, ms["left"])), csum(dc * xc), csum(dc * _shift(xc, -1, ms["right"]))]
        d = d_ref[tc:tt, :]
        x = u_ref[tc:tt, :]
        mg = _grid_masks(t)
        dtaps = (_shift(d, -1, mg["right"]), d, _shift(d, 1, mg["left"]))
        rows = [w[3 * i] * dtaps[0] + w[3 * i + 1] * dtaps[1] + w[3 * i + 2] * dtaps[2] for i in range(3)]
        du_ref[tc:tt, :] = (rows[1] + _shift(rows[0], -GRID_W, mg["down"]) + _shift(rows[2], GRID_W, mg["up"])).astype(BF16)
        xtaps = (_shift(x, 1, mg["left"]), x, _shift(x, -1, mg["right"]))
        for j in range(3):
            gw_ref[j:j + 1, :] = csum(d * _shift(xtaps[j], GRID_W, mg["up"]))
            gw_ref[3 + j:4 + j, :] = csum(d * xtaps[j]) + gmid[j]
            gw_ref[6 + j:7 + j, :] = csum(d * _shift(xtaps[j], -GRID_W, mg["down"]))
        gb_ref[...] = csum(d) + csum(dc)

    return pl.pallas_call(
        body, name="conv_bwd", grid=(2048 // LANE,),
        in_specs=[pl.BlockSpec((tt, LANE), lambda j: (0, j)), pl.BlockSpec((tt, LANE), lambda j: (0, BLK_QK + j)),
                  pl.BlockSpec((9, LANE), lambda j: (0, j))],
        out_specs=[pl.BlockSpec((tt, LANE), lambda j: (0, j)), pl.BlockSpec((9, LANE), lambda j: (0, j)),
                   pl.BlockSpec((1, LANE), lambda j: (0, j))],
        out_shape=[jax.ShapeDtypeStruct((tt, 2048), BF16), jax.ShapeDtypeStruct((9, 2048), F32),
                   jax.ShapeDtypeStruct((1, 2048), F32)],
        compiler_params=_vmem(48 * tt * LANE * 4),
    )(dcp, u, w9)


def _chunk_of(pos, ncc, nc, rev):
    if not rev:
        return pos
    return jnp.where(pos < ncc, ncc - 1 - pos, nc - 1 - (pos - ncc))


def _hgrn_fwd(u, lower_d, ncc, rev):
    tt = u.shape[0]
    nc = tt // CHUNK
    seg_f = SEG_AFB if rev else SEG_AFF
    heads = _head_slices(W_A, HG_HEADS)

    def body(zq_ref, zf_ref, v_ref, lb_ref, o_ref, hist_ref, st_ref):
        @pl.when(pl.program_id(0) == 0)
        def _():
            st_ref[...] = jnp.zeros_like(st_ref)

        st = st_ref[...]
        hist_ref[0] = st
        o, st_new = hg_chunk_fwd(zq_ref[...], zf_ref[...], v_ref[...], lb_ref[...], st, rev)
        o_ref[...] = o
        st_ref[...] = st_new

    seg = lambda s: pl.BlockSpec((CHUNK, 1024), lambda j: (_chunk_of(j, ncc, nc, rev), s))
    return pl.pallas_call(
        body, name="hgrn_fwd_rev" if rev else "hgrn_fwd", grid=(nc,),
        in_specs=[seg(SEG_AQ), seg(seg_f), seg(SEG_AI), pl.BlockSpec((1, 1024), lambda j: (0, 0))],
        out_specs=[pl.BlockSpec((CHUNK, 1024), lambda j: (_chunk_of(j, ncc, nc, rev), 0)),
                   pl.BlockSpec((1, 1024, HG_D), lambda j: (_chunk_of(j, ncc, nc, rev), 0, 0))],
        out_shape=[jax.ShapeDtypeStruct((tt, 1024), F32), jax.ShapeDtypeStruct((nc, 1024, HG_D), F32)],
        scratch_shapes=[pltpu.VMEM((1024, HG_D), F32)],
    )(u, u, u, lower_d)


def _hgrn_bwd(u, lower_d, hist, do, acc, ncc, rev, ride=None):
    tt = u.shape[0]
    nc = tt // CHUNK
    seg_f = SEG_AFB if rev else SEG_AFF
    has_acc = acc is not None
    has_a2a = ride is not None

    def body(zq_ref, zf_ref, v_ref, lb_ref, hist_ref, do_ref, *rest):
        if has_acc:
            aq_ref, av_ref = rest[:2]
            rest = rest[2:]
        if has_a2a:
            x_ref, rest = rest[0], rest[1:]
        dzf_ref, dzq_ref, dv_ref, dlb_ref = rest[:4]
        rest = rest[4:]
        if has_a2a:
            comm = (x_ref, rest[0]) + tuple(rest[2:])
            dst_ref = rest[1]
        else:
            dst_ref = rest[0]

        @pl.when(pl.program_id(0) == 0)
        def _():
            dst_ref[...] = jnp.zeros_like(dst_ref)
            dlb_ref[...] = jnp.zeros_like(dlb_ref)
            if has_a2a:
                ride.start(*comm)

        dzq, dzf, dv, dlb, dst = hg_chunk_bwd(zq_ref[...], zf_ref[...], v_ref[...], lb_ref[...],
                                              hist_ref[0], do_ref[...], dst_ref[...], rev)
        dst_ref[...] = dst
        dzf_ref[...] = dzf.astype(BF16)
        dlb_ref[...] += dlb
        if has_acc:
            dzq = dzq + aq_ref[...]
            dv = dv + av_ref[...]
        dzq_ref[...] = dzq
        dv_ref[...] = dv

        if has_a2a:
            @pl.when(pl.program_id(0) == nc - 1)
            def _():
                ride.wait(*comm)

    cidx = lambda j: _chunk_of(nc - 1 - j, ncc, nc, rev)
    seg = lambda s: pl.BlockSpec((CHUNK, 1024), lambda j: (cidx(j), s))
    row = pl.BlockSpec((CHUNK, 1024), lambda j: (cidx(j), 0))
    ins = [u, u, u, lower_d, hist, do] + (list(acc) if has_acc else []) + ([ride.x] if has_a2a else [])
    return pl.pallas_call(
        body, name="hgrn_bwd_rev" if rev else "hgrn_bwd", grid=(nc,),
        in_specs=[seg(SEG_AQ), seg(seg_f), seg(SEG_AI), pl.BlockSpec((1, 1024), lambda j: (0, 0)),
                  pl.BlockSpec((1, 1024, HG_D), lambda j: (cidx(j), 0, 0)), row] + ([row, row] if has_acc else [])
        + ([_ANY] if has_a2a else []),
        out_specs=[row, row, row, pl.BlockSpec((1, 1024), lambda j: (0, 0))] + ([_ANY] if has_a2a else []),
        out_shape=[jax.ShapeDtypeStruct((tt, 1024), BF16), jax.ShapeDtypeStruct((tt, 1024), F32),
                   jax.ShapeDtypeStruct((tt, 1024), F32), jax.ShapeDtypeStruct((1, 1024), F32)]
        + ([ride.out_shape] if has_a2a else []),
        scratch_shapes=[pltpu.VMEM((1024, HG_D), F32)] + (_RIDE_SCRATCH if has_a2a else []),
    )(*ins)


def _gate_views(gc_ref, gr_ref, bc_ref, br_ref, head, rev):
    gc = gc_ref[0] + bc_ref[...]
    gr = gr_ref[0] + br_ref[...]
    lane = lax.broadcasted_iota(jnp.int32, (1, 16), 1)
    sub = lax.broadcasted_iota(jnp.int32, (16, 1), 0)
    d = 1 if rev else 0
    ii, fi = d * ML_HEADS + head, 2 * ML_HEADS + d * ML_HEADS + head
    col = lambda idx: jnp.sum(jnp.where(lane == idx, gc, 0.0), axis=1, keepdims=True)
    row = lambda idx: jnp.sum(jnp.where(sub == idx, gr, 0.0), axis=0, keepdims=True)
    return col(ii), row(ii), col(fi), row(fi)


def _mlstm_fwd(cpre, u, gcol, grow, bias_c, bias_r, ncc, rev):
    tt = u.shape[0]
    nc = tt // CHUNK
    heads = _head_slices(W_B, ML_HEADS)

    def body(q_ref, k_ref, v_ref, gc_ref, gr_ref, bc_ref, br_ref, h_ref, ch_ref, nh_ref, mh_ref, c_ref, n_ref, m_ref):
        @pl.when(pl.program_id(0) == 0)
        def _():
            c_ref[...] = jnp.zeros_like(c_ref)
            n_ref[...] = jnp.zeros_like(n_ref)
            m_ref[...] = jnp.zeros_like(m_ref)

        c, n_all, m_all = c_ref[...], n_ref[...], m_ref[...]
        ch_ref[0] = c
        nh_ref[0] = n_all
        mh_ref[0] = m_all
        nhd = len(heads)
        gates = [_gate_views(gc_ref, gr_ref, bc_ref, br_ref, hd, rev) for hd in range(nhd)]
        h, c_new, n_new, m_new = ml_chunk_fwd(q_ref[...], k_ref[...], v_ref[...], gates, c,
                                              [n_all[hd:hd + 1, :] for hd in range(nhd)],
                                              [m_all[hd:hd + 1, 0:1] for hd in range(nhd)], rev)
        h_ref[...] = h
        c_ref[...] = c_new
        for hd in range(nhd):
            n_ref[hd:hd + 1, :] = n_new[hd]
            m_ref[hd:hd + 1, :] = jnp.broadcast_to(m_new[hd], (1, LANE))

    cidx = lambda j: _chunk_of(j, ncc, nc, rev)
    row = lambda s: pl.BlockSpec((CHUNK, 1024), lambda j: (cidx(j), s))
    st3 = lambda a, b: pl.BlockSpec((1, a, b), lambda j: (cidx(j), 0, 0))
    return pl.pallas_call(
        body, name="mlstm_fwd_rev" if rev else "mlstm_fwd", grid=(nc,),
        in_specs=[row(0), row(1), row(SEG_BV), st3(CHUNK, 16), st3(16, CHUNK),
                  pl.BlockSpec((1, 16), lambda j: (0, 0)), pl.BlockSpec((16, 1), lambda j: (0, 0))],
        out_specs=[row(0), st3(1024, ML_D), st3(8, ML_D), st3(8, LANE)],
        out_shape=[jax.ShapeDtypeStruct((tt, 1024), F32), jax.ShapeDtypeStruct((nc, 1024, ML_D), F32),
                   jax.ShapeDtypeStruct((nc, 8, ML_D), F32), jax.ShapeDtypeStruct((nc, 8, LANE), F32)],
        scratch_shapes=[pltpu.VMEM((1024, ML_D), F32), pltpu.VMEM((8, ML_D), F32), pltpu.VMEM((8, LANE), F32)],
    )(cpre, cpre, u, gcol, grow, bias_c, bias_r)


def _mlstm_bwd(cpre, u, gcol, grow, bias_c, bias_r, chist, nhist, mhist, dh, acc, ncc, rev):
    tt = u.shape[0]
    nc = tt // CHUNK
    heads = _head_slices(W_B, ML_HEADS)
    has_acc = acc is not None
    d = 1 if rev else 0

    def body(q_ref, k_ref, v_ref, gc_ref, gr_ref, bc_ref, br_ref, ch_ref, nh_ref, mh_ref, dh_ref, *rest):
        if has_acc:
            aqk_ref, av_ref, ag_ref = rest[:3]
            rest = rest[3:]
        dqk_ref, dv_ref, dg_ref, gs_ref, dc_ref, dn_ref = rest

        @pl.when(pl.program_id(0) == 0)
        def _():
            dc_ref[...] = jnp.zeros_like(dc_ref)
            dn_ref[...] = jnp.zeros_like(dn_ref)
            gs_ref[...] = jnp.zeros_like(gs_ref)

        lane = lax.broadcasted_iota(jnp.int32, (1, LANE), 1)
        dg = ag_ref[...] if has_acc else jnp.zeros((CHUNK, LANE), F32)
        nhd = len(heads)
        gates = [_gate_views(gc_ref, gr_ref, bc_ref, br_ref, hd, rev) for hd in range(nhd)]
        n_all, m_all, dn_all = nh_ref[0], mh_ref[0], dn_ref[...]
        dqp, dkp, dv, dgi, dgf, dc, dn = ml_chunk_bwd(
            q_ref[...], k_ref[...], v_ref[...], gates, ch_ref[0],
            [n_all[hd:hd + 1, :] for hd in range(nhd)], [m_all[hd:hd + 1, 0:1] for hd in range(nhd)],
            dh_ref[...], dc_ref[...], [dn_all[hd:hd + 1, :] for hd in range(nhd)], rev)
        dc_ref[...] = dc
        for hd in range(nhd):
            dn_ref[hd:hd + 1, :] = dn[hd]
            dg = dg + jnp.where(lane == d * ML_HEADS + hd, dgi[hd], 0.0)
            dg = dg + jnp.where(lane == 2 * ML_HEADS + d * ML_HEADS + hd, dgf[hd], 0.0)
        if has_acc:
            dqp = dqp + aqk_ref[:, 0:W_B]
            dkp = dkp + aqk_ref[:, W_B:2 * W_B]
            dv = dv + av_ref[...]
        dqk_ref[:, 0:W_B] = dqp
        dqk_ref[:, W_B:2 * W_B] = dkp
        dv_ref[...] = dv
        dg_ref[...] = dg
        gs_ref[...] += jnp.sum(dg, axis=0, keepdims=True)

    cidx = lambda j: _chunk_of(nc - 1 - j, ncc, nc, rev)
    row = lambda s: pl.BlockSpec((CHUNK, 1024), lambda j: (cidx(j), s))
    wide = pl.BlockSpec((CHUNK, 2048), lambda j: (cidx(j), 0))
    gate = pl.BlockSpec((CHUNK, LANE), lambda j: (cidx(j), 0))
    st3 = lambda a, b: pl.BlockSpec((1, a, b), lambda j: (cidx(j), 0, 0))
    ins = [cpre, cpre, u, gcol, grow, bias_c, bias_r, chist, nhist, mhist, dh] + (list(acc) if has_acc else [])
    return pl.pallas_call(
        body, name="mlstm_bwd_rev" if rev else "mlstm_bwd", grid=(nc,),
        in_specs=[row(0), row(1), row(SEG_BV), st3(CHUNK, 16), st3(16, CHUNK),
                  pl.BlockSpec((1, 16), lambda j: (0, 0)), pl.BlockSpec((16, 1), lambda j: (0, 0)),
                  st3(1024, ML_D), st3(8, ML_D), st3(8, LANE), row(0)] + ([wide, row(0), gate] if has_acc else []),
        out_specs=[wide, row(0), gate, pl.BlockSpec((1, LANE), lambda j: (0, 0))],
        out_shape=[jax.ShapeDtypeStruct((tt, 2048), F32), jax.ShapeDtypeStruct((tt, 1024), F32),
                   jax.ShapeDtypeStruct((tt, LANE), F32), jax.ShapeDtypeStruct((1, LANE), F32)],
        scratch_shapes=[pltpu.VMEM((1024, ML_D), F32), pltpu.VMEM((8, ML_D), F32)],
    )(*ins)


def _whole(body, out_shape, name, *args, nbytes=0):
    return pl.pallas_call(body, name=name, out_shape=out_shape, compiler_params=_vmem(nbytes))(*args)


def _mod_fwd(cs, w_cols, b_cols):
    def body(c_ref, w_ref, b_ref, o_ref):
        o_ref[...] = _exact_nn(_silu(c_ref[...]), w_ref[...]) + b_ref[...]

    return _whole(body, jax.ShapeDtypeStruct((16, w_cols.shape[1]), F32), "mod_fwd", cs, w_cols, b_cols,
                  nbytes=4 * w_cols.size * 4)


def _mod_bwd_w(cs, d9, w_cols):
    def body(c_ref, d_ref, w_ref, gw_ref, pc_ref):
        gw_ref[...] = _exact_tn(_silu(c_ref[...]), d_ref[...])
        pc = lax.dot_general(d_ref[8:16, :], w_ref[...], (((1,), (1,)), ((), ())), precision=lax.Precision.HIGHEST,
                             preferred_element_type=F32)
        row = lax.broadcasted_iota(jnp.int32, pc.shape, 0)
        pc_ref[...] = jnp.where(row == 0, pc, 0.0)

    return _whole(body, [jax.ShapeDtypeStruct(w_cols.shape, F32), jax.ShapeDtypeStruct((8, w_cols.shape[0]), F32)],
                  "mod_bwd_w", cs, d9, w_cols, nbytes=6 * w_cols.size * 4)


def _lower_fwd(lb4):
    def body(l_ref, o_ref):
        o_ref[...] = jnp.zeros_like(o_ref)
        o_ref[0:1, :] = 1.0 / (1.0 + jnp.exp(l_ref[1:2, :] - l_ref[0:1, :]))
        o_ref[1:2, :] = 1.0 / (1.0 + jnp.exp(l_ref[3:4, :] - l_ref[2:3, :]))

    return _whole(body, jax.ShapeDtypeStruct((8, lb4.shape[1]), F32), "lower_fwd", lb4)


def _reduce8(g, name):
    def body(g_ref, o_ref):
        acc = g_ref[0]
        for k in range(1, N_DEV):
            acc = acc + g_ref[k]
        o_ref[...] = acc

    return _whole(body, jax.ShapeDtypeStruct(g.shape[1:], F32), name, g, nbytes=4 * g.size * 4)


_PACK = (("dmodx", 48), ("dmodc", 48), ("gconvw", 144), ("gconvb", 16), ("dlower", 16), ("ghgw", 8), ("gmlw", 8),
         ("glng", 16), ("glnb", 16), ("losssq", 16), ("ggate", 8))
_PACK_ROWS = sum(r for _, r in _PACK)


def _pack_offsets():
    off, out = 0, {}
    for name, rows in _PACK:
        out[name] = (off, rows)
        off += rows
    return out


def _small_finish(total, p0, d_feat):
    offs = _pack_offsets()

    def body(t_ref, p_ref, gb_ref, a0_ref, a1_ref, loss_ref):
        ox, oc, ol, oq = offs["dmodx"][0], offs["dmodc"][0], offs["dlower"][0], offs["losssq"][0]
        gb_ref[...] = t_ref[ox:ox + 48, :] + t_ref[oc:oc + 48, :]
        p = p_ref[...]
        da0 = t_ref[ol:ol + 16, :] * p * (1.0 - p)
        a0_ref[...] = da0
        a1_ref[...] = -da0
        sq = t_ref[oq:oq + 16, :]
        tot = jnp.sum(jnp.sum(sq, axis=1, keepdims=True), axis=0, keepdims=True)
        loss_ref[...] = jnp.broadcast_to(tot * (0.5 / d_feat), loss_ref.shape)

    s = jax.ShapeDtypeStruct
    return _whole(body, [s((48, LANE), F32), s((16, LANE), F32), s((16, LANE), F32), s((8, LANE), F32)],
                  "small_finish", total, p0)


def _cctx_grad(parts, c_ctx8):
    def body(p_ref, c_ref, o_ref):
        acc = p_ref[0]
        for k in range(1, N_DEV):
            acc = acc + p_ref[k]
        o_ref[...] = acc * _dsilu(c_ref[...])

    return _whole(body, jax.ShapeDtypeStruct(c_ctx8.shape, F32), "cctx_grad", parts, c_ctx8)


def _adam_math(w, g, m, v):
    m = ADAM_B1 * m + (1.0 - ADAM_B1) * g
    v = ADAM_B2 * v + (1.0 - ADAM_B2) * (g * g)
    m_hat = m / (1.0 - ADAM_B1 ** ADAM_STEP)
    v_hat = v / (1.0 - ADAM_B2 ** ADAM_STEP)
    delta = -ADAM_LR * (m_hat / (jnp.sqrt(v_hat) + ADAM_EPS) + ADAM_WD * w)
    return delta, m, v


def _adamw(w, g, m, v, rows, name):
    r, c = w.shape

    def body(w_ref, g_ref, m_ref, v_ref, d_ref, mo_ref, vo_ref):
        d_ref[...], mo_ref[...], vo_ref[...] = _adam_math(w_ref[...], g_ref[...], m_ref[...], v_ref[...])

    spec = pl.BlockSpec((rows, c), lambda i: (i, 0))
    return pl.pallas_call(
        body, name=name, grid=(r // rows,), in_specs=[spec] * 4, out_specs=[spec] * 3,
        out_shape=[jax.ShapeDtypeStruct((r, c), F32)] * 3,
        compiler_params=_vmem(16 * rows * (c + LANE) * 4),
    )(w, g, m, v)


def _rs_adamw(recv, w, m, v, tile, name, by_cols=False):
    _, r, c = recv.shape

    def body(r_ref, w_ref, m_ref, v_ref, g_ref, d_ref, mo_ref, vo_ref):
        g = r_ref[0].astype(F32)
        for k in range(1, N_DEV):
            g = g + r_ref[k].astype(F32)
        g_ref[...] = g
        d_ref[...], mo_ref[...], vo_ref[...] = _adam_math(w_ref[...], g, m_ref[...], v_ref[...])

    if by_cols:
        spec = pl.BlockSpec((r, tile), lambda i: (0, i))
        rspec = pl.BlockSpec((N_DEV, r, tile), lambda i: (0, 0, i))
        steps, elems = c // tile, (r + 16) * tile
    else:
        spec = pl.BlockSpec((tile, c), lambda i: (i, 0))
        rspec = pl.BlockSpec((N_DEV, tile, c), lambda i: (0, i, 0))
        steps, elems = r // tile, tile * (c + LANE)
    return pl.pallas_call(
        body, name=name, grid=(steps,), in_specs=[rspec] + [spec] * 3, out_specs=[spec] * 4,
        out_shape=[jax.ShapeDtypeStruct((r, c), F32)] * 4,
        compiler_params=_vmem(2 * elems * (N_DEV * 2 + 7 * 4) + (4 << 20)),
    )(recv, w, m, v)


def _all_gather(x, name):
    r, c = x.shape

    def body(x_ref, out_ref, send_sems, recv_sems, local_sem):
        px, py, pc = _position()
        me, sibling = (px, py, pc), (px, py, 1 - pc)
        chips = [(1 - px, py), (px, 1 - py), (1 - px, 1 - py)]

        def slot(qx, qy, qc):
            return out_ref.at[4 * qx + 2 * qy + qc]

        def copy(k, block, to, src=None):
            return pltpu.make_async_remote_copy(
                src_ref=slot(*block) if src is None else src, dst_ref=slot(*block),
                send_sem=send_sems.at[k], recv_sem=recv_sems.at[k], device_id=to, device_id_type=MESH)

        mine = pltpu.make_async_copy(x_ref, slot(*me), local_sem)
        mine.start()
        first = [copy(1 + j, me, (*chip, pc), src=x_ref) for j, chip in enumerate(chips)]
        first.append(copy(0, me, sibling, src=x_ref))
        for cp in first:
            cp.start()
        passed = [copy(4 + j, (*chip, pc), sibling) for j, chip in enumerate(chips)]
        for j, chip in enumerate(chips):
            copy(1 + j, (*chip, pc), me).wait_recv()
            passed[j].start()
        copy(0, sibling, me).wait_recv()
        for j, chip in enumerate(chips):
            copy(4 + j, (*chip, 1 - pc), me).wait_recv()
        for cp in first + passed:
            cp.wait_send()
        mine.wait()

    return pl.pallas_call(
        body, name=name, out_shape=jax.ShapeDtypeStruct((N_DEV, r, c), x.dtype),
        in_specs=[pl.BlockSpec(memory_space=pl.ANY)], out_specs=pl.BlockSpec(memory_space=pl.ANY),
        scratch_shapes=[pltpu.SemaphoreType.DMA((7,)), pltpu.SemaphoreType.DMA((7,)), pltpu.SemaphoreType.DMA],
    )(x)


def _local_step(xc_all, target, modp, lower, wt_u, w_o, w9, conv_b, gate_b, hgw, mlw, ln_g, ln_b, tc, exchange):
    tt = xc_all.shape[0]
    nbc, ncc, nc = tc // ROWS, tc // CHUNK, tt // CHUNK
    lower_f, lower_b = lower[0:1], lower[1:2]

    hc = _modulate_fwd(xc_all, modp, nbc)
    tmh = _pick(tt, (1088, 768, 512, 256))
    if exchange:
        u, w_o = _mm(hc, wt_u, "nt", F32, tmh, 1152, D_MODEL, "mm_u", ride=_Ride("gather", w_o))
        w_o = w_o.reshape(D_MODEL, D_MODEL)
    else:
        u = _mm(hc, wt_u, "nt", F32, tmh, 1152, D_MODEL, "mm_u")
    cpre = _conv_fwd(u, w9, conv_b, tc)
    gates = u[:, BLK_GATE * LANE:BLK_GATE * LANE + 16].reshape(nc, CHUNK, 16)
    gcol, grow = gates, jnp.swapaxes(gates, 1, 2)
    bias_c, bias_r = gate_b.reshape(1, 16), gate_b.reshape(16, 1)

    o_f, hist_f = _hgrn_fwd(u, lower_f, ncc, False)
    o_b, hist_b = _hgrn_fwd(u, lower_b, ncc, True)
    h_f, ch_f, nh_f, mh_f = _mlstm_fwd(cpre, u, gcol, grow, bias_c, bias_r, ncc, False)
    h_b, ch_b, nh_b, mh_b = _mlstm_fwd(cpre, u, gcol, grow, bias_c, bias_r, ncc, True)
    y = _post_fwd(o_f, o_b, h_f, h_b, u, hgw, mlw, nbc)
    tm = _pick(y.shape[0], (512, 256))
    z = _mm(y, w_o, "nn", F32, tm, D_MODEL, D_MODEL, "mm_z")
    dz, dxa, fsum = _final(z, xc_all, target, modp, ln_g, ln_b, nbc)

    dy = _mm(dz, w_o, "nt", F32, tm, D_MODEL, D_MODEL, "mm_dy")
    dw_o = _mm(y, dz, "tn", BF16, D_MODEL, 1024, tm, "mm_dwo")
    do, dhm, du1, psum = _post_bwd(dy, o_f, o_b, h_f, h_b, u, hgw, mlw, nbc)
    if exchange:
        dzf_f, dzq, dv_a, dlb_f, dw_o = _hgrn_bwd(
            u, lower_f, hist_f, do, None, ncc, False,
            ride=_Ride("a2a", dw_o.reshape(N_DEV, D_MODEL // N_DEV, D_MODEL)))
    else:
        dzf_f, dzq, dv_a, dlb_f = _hgrn_bwd(u, lower_f, hist_f, do, None, ncc, False)
    dzf_b, dzq, dv_a, dlb_b = _hgrn_bwd(u, lower_b, hist_b, do, (dzq, dv_a), ncc, True)
    dqk, dv_m, dg, _ = _mlstm_bwd(cpre, u, gcol, grow, bias_c, bias_r, ch_f, nh_f, mh_f, dhm, None, ncc, False)
    dqk, dv_m, dg, gsum = _mlstm_bwd(cpre, u, gcol, grow, bias_c, bias_r, ch_b, nh_b, mh_b, dhm, (dqk, dv_m, dg), ncc, True)
    du5, gconvw, gconvb = _conv_bwd(dqk, u, w9, tc)
    du = jnp.concatenate([dzq.astype(BF16), dzf_f, dzf_b, dv_a.astype(BF16), du1[:, 0:1024], du5, dv_m.astype(BF16),
                          du1[:, 1024:3072], dg.astype(BF16)], axis=1)
    half = D_MODEL // 2
    blocks = lambda g: g[:N_IN].reshape(N_DEV, N_IN // N_DEV, half)
    dwt_a = _mm(du, hc, "tn", BF16, 1152, half, tmh, "mm_dwu_a", n_cols=half, n_off=0)
    if exchange:
        dwt_b, got = _mm(du, hc, "tn", BF16, 1152, half, tmh, "mm_dwu_b", n_cols=half, n_off=1,
                         ride=_Ride("a2a", blocks(dwt_a), cols=(0, D_MODEL)))
        dh, dwt_u = _mm(du, wt_u, "nn", F32, tmh, D_MODEL, 1152, "mm_dh",
                        ride=_Ride("a2a", blocks(dwt_b), cols=(half, D_MODEL), into=got))
    else:
        dwt_b = _mm(du, hc, "tn", BF16, 1152, half, tmh, "mm_dwu_b", n_cols=half, n_off=1)
        dwt_u = jnp.concatenate([dwt_a, dwt_b], axis=1)
        dh = _mm(du, wt_u, "nn", F32, tmh, D_MODEL, 1152, "mm_dh")
    gx, msum = _modulate_bwd(dh, xc_all, modp, dxa, nbc)

    zero_row = jnp.zeros((1, D_MODEL), F32)
    small = dict(
        dmodx=jnp.concatenate([msum[2:3], msum[3:4], fsum[0:1]], axis=0),
        dmodc=jnp.concatenate([msum[0:1], msum[1:2], zero_row], axis=0),
        gconvw=gconvw, gconvb=gconvb, dlower=jnp.concatenate([dlb_f, dlb_b], axis=0),
        ghgw=psum[0:1], gmlw=psum[1:2], glng=fsum[1:2], glnb=fsum[2:3], losssq=fsum[3:4],
        ggate=jnp.concatenate([gsum, jnp.zeros((7, LANE), F32)], axis=0))
    return gx, dwt_u, dw_o, small


def _pack_small(small):
    return jnp.concatenate([small[name].reshape(rows, LANE) for name, rows in _PACK], axis=0)


def _flat_pad(a, rows):
    flat = a.reshape(-1)
    return jnp.pad(flat, (0, rows * LANE - flat.shape[0])).reshape(rows, LANE)


def kernel(x, c, ctx, c_ctx, w_mod, b_mod, w_in, conv_w, conv_b, hg_lb, ml_gate_b, hg_norm_w, ml_norm_w, w_out, ln_g, ln_b, loss_target, m_c_ctx, m_w_mod, m_b_mod, m_w_in, m_conv_w, m_conv_b, m_hg_lb, m_ml_gate_b, m_hg_norm_w, m_ml_norm_w, m_w_out, m_ln_g, m_ln_b, v_c_ctx, v_w_mod, v_b_mod, v_w_in, v_conv_w, v_conv_b, v_hg_lb, v_ml_gate_b, v_hg_norm_w, v_ml_norm_w, v_w_out, v_ln_g, v_ln_b):
    px, py, pc = _position()
    me = 4 * px + 2 * py + pc
    d = D_MODEL
    tc = ctx.shape[1]
    n_mod = w_mod.shape[2]
    n_wi = w_in.shape[2]
    n_cv = conv_w.shape[3]
    n_lb = hg_lb.shape[2]

    pack0 = jnp.concatenate([c.reshape(-1), conv_w.reshape(-1), hg_lb.reshape(-1)]).reshape(1, -1)
    g0 = _all_gather(pack0, "gather_small_inputs")[:, 0, :]
    c_all = g0[:, :d]
    w9 = jnp.transpose(g0[:, d:d + 9 * n_cv].reshape(N_DEV, 9, n_cv), (1, 0, 2)).reshape(9, N_DEV * n_cv)
    lb4 = jnp.transpose(g0[:, d + 9 * n_cv:].reshape(N_DEV, 4, n_lb), (1, 0, 2)).reshape(4, N_DEV * n_lb)
    lower = _lower_fwd(lb4)

    cs = jnp.concatenate([c_all, c_ctx.reshape(1, d), jnp.zeros((7, d), F32)], axis=0)
    b_cols = lax.dynamic_slice(b_mod, (0, me * n_mod), (1, n_mod))
    slab = _mod_fwd(cs, w_mod[0], b_cols)
    mod_all = jnp.transpose(_all_gather(slab, "gather_mod"), (1, 0, 2)).reshape(16, N_DEV * n_mod)
    mod_x = lax.dynamic_slice(mod_all, (me, 0), (1, 3 * d)).reshape(3, d)
    modp = jnp.stack([mod_all[8].reshape(3, d), mod_x])

    wt = _all_gather(w_in[0].T.astype(BF16), "gather_w_in").reshape(N_DEV * n_wi, d)
    wt_u = jnp.pad(wt, ((0, N_U - N_DEV * n_wi), (0, 0)))

    xc_all = jnp.concatenate([ctx[0], x[0]], axis=0)
    gx, recv_wi, recv_wo, small = _local_step(xc_all, loss_target[0], modp, lower, wt_u, w_out[0].astype(BF16), w9,
                                              conv_b, ml_gate_b[0], hg_norm_w, ml_norm_w, ln_g, ln_b, tc, True)
    g_wi, d_wi, nm_wi, nv_wi = [a.T for a in _rs_adamw(recv_wi, w_in[0].T, m_w_in[0].T, v_w_in[0].T, 256,
                                                       "adamw_w_in", by_cols=True)]
    g_wo, d_wo, nm_wo, nv_wo = _rs_adamw(recv_wo, w_out[0], m_w_out[0], v_w_out[0], 64, "adamw_w_out")

    packs = _all_gather(_pack_small(small), "gather_small_grads")
    total = _reduce8(packs, "reduce_small_grads")
    offs = _pack_offsets()
    piece = lambda name: total[offs[name][0]:offs[name][0] + offs[name][1]]
    g_bmod, g_lb0, g_lb1, loss8 = _small_finish(total, lower[0:2].reshape(16, LANE), float(d))

    ox = offs["dmodx"][0]
    dmodx_all = packs[:, ox:ox + 48, :].reshape(N_DEV, 3 * d)
    dmodc_tot = piece("dmodc").reshape(1, 3 * d)
    d9 = jnp.concatenate([dmodx_all, dmodc_tot, jnp.zeros((7, 3 * d), F32)], axis=0)
    d9_cols = lax.dynamic_slice(d9, (0, me * n_mod), (16, n_mod))
    g_wmod, pc_part = _mod_bwd_w(cs, d9_cols, w_mod[0])
    c_ctx8 = jnp.concatenate([c_ctx.reshape(1, d), jnp.zeros((7, d), F32)], axis=0)
    g_cctx = _cctx_grad(_all_gather(pc_part, "gather_cctx"), c_ctx8)[0]
    d_wmod, nm_wmod, nv_wmod = _adamw(w_mod[0], g_wmod, m_w_mod[0], v_w_mod[0], 256, "adamw_w_mod")

    g_convw_full = piece("gconvw").reshape(9, d)
    g_convw = lax.dynamic_slice(g_convw_full, (0, me * n_cv), (9, n_cv)).reshape(conv_w.shape)
    lb_full = jnp.stack([jnp.stack([g_lb0[0:8].reshape(-1), g_lb1[0:8].reshape(-1)]),
                         jnp.stack([g_lb0[8:16].reshape(-1), g_lb1[8:16].reshape(-1)])])
    g_hglb = lax.dynamic_slice(lb_full, (0, 0, me * n_lb), (2, 2, n_lb))
    grads = dict(
        c_ctx=g_cctx, b_mod=g_bmod.reshape(b_mod.shape), conv_w=g_convw, conv_b=piece("gconvb").reshape(conv_b.shape),
        hg_lb=g_hglb, ml_gate_b=piece("ggate")[0, :16].reshape(ml_gate_b.shape),
        hg_norm_w=piece("ghgw").reshape(hg_norm_w.shape), ml_norm_w=piece("gmlw").reshape(ml_norm_w.shape),
        ln_g=piece("glng").reshape(ln_g.shape), ln_b=piece("glnb").reshape(ln_b.shape))
    params = dict(c_ctx=(c_ctx, m_c_ctx, v_c_ctx), b_mod=(b_mod, m_b_mod, v_b_mod), conv_w=(conv_w, m_conv_w, v_conv_w),
                  conv_b=(conv_b, m_conv_b, v_conv_b), hg_lb=(hg_lb, m_hg_lb, v_hg_lb),
                  ml_gate_b=(ml_gate_b, m_ml_gate_b, v_ml_gate_b), hg_norm_w=(hg_norm_w, m_hg_norm_w, v_hg_norm_w),
                  ml_norm_w=(ml_norm_w, m_ml_norm_w, v_ml_norm_w), ln_g=(ln_g, m_ln_g, v_ln_g), ln_b=(ln_b, m_ln_b, v_ln_b))
    names = list(params)
    rows_of = {n: -(-params[n][0].size // LANE) for n in names}
    rows_tot = -(-sum(rows_of.values()) // 8) * 8
    cat = lambda arrs: jnp.concatenate(
        [_flat_pad(a, rows_of[n]) for n, a in zip(names, arrs)]
        + [jnp.ones((rows_tot - sum(rows_of.values()), LANE), F32)], axis=0)
    d_s, m_s, v_s = _adamw(cat([params[n][0] for n in names]), cat([grads[n] for n in names]),
                           cat([params[n][1] for n in names]), cat([params[n][2] for n in names]), rows_tot, "adamw_small")
    delta, new_m, new_v, off = {}, {}, {}, 0
    for n in names:
        shape, size = params[n][0].shape, params[n][0].size
        take = lambda a: a[off:off + rows_of[n]].reshape(-1)[:size].reshape(shape)
        delta[n], new_m[n], new_v[n] = take(d_s), take(m_s), take(v_s)
        off += rows_of[n]
    grads.update(w_mod=g_wmod[None], w_in=g_wi[None], w_out=g_wo[None])
    delta.update(w_mod=d_wmod[None], w_in=d_wi[None], w_out=d_wo[None])
    new_m.update(w_mod=nm_wmod[None], w_in=nm_wi[None], w_out=nm_wo[None])
    new_v.update(w_mod=nv_wmod[None], w_in=nv_wi[None], w_out=nv_wo[None])

    order = ("c_ctx", "w_mod", "b_mod", "w_in", "conv_w", "conv_b", "hg_lb", "ml_gate_b", "hg_norm_w", "ml_norm_w",
             "w_out", "ln_g", "ln_b")
    return (loss8[0, 0], gx[None], *[grads[n] for n in order], *[delta[n] for n in order],
            *[new_m[n] for n in order], *[new_v[n] for n in order])
```

```python
import functools

import jax
import jax.numpy as jnp
from jax import lax
from jax.experimental import pallas as pl
from jax.experimental.pallas import tpu as pltpu

F32 = jnp.float32
BF16 = jnp.bfloat16

D_MODEL = 2048
W_A = 1024
W_B = 1024
HG_HEADS = 8
HG_D = 128
ML_HEADS = 4
ML_D = 256
CHUNK = 64
N_IN = 10256
LANE = 128
N_U = 81 * LANE
N_DEV = 8
ALPHA = 2.0 ** 0.25
LN_EPS = 1e-5
NORM_EPS = 1e-6
ADAM_LR, ADAM_B1, ADAM_B2, ADAM_EPS, ADAM_WD, ADAM_STEP = 0.001, 0.9, 0.999, 1e-08, 0.01, 10
VMEM_CAP = 60 * 1024 * 1024

SEG_AQ, SEG_AFF, SEG_AFB, SEG_AI, SEG_AZ = range(5)
BLK_QK = 40
SEG_BV, SEG_BO, SEG_BZ = 7, 8, 9
BLK_GATE = 80

MESH = pl.DeviceIdType.MESH


def _vmem(nbytes):
    return pltpu.CompilerParams(vmem_limit_bytes=int(min(VMEM_CAP, max(nbytes, 16 * 1024 * 1024))))


def _sigmoid(x):
    return 1.0 / (1.0 + jnp.exp(-x))


def _silu(x):
    return x * _sigmoid(x)


def _dsilu(x):
    s = _sigmoid(x)
    return s * (1.0 + x * (1.0 - s))


def _bdot(a, b, dims):
    return lax.dot_general(a.astype(BF16), b.astype(BF16), (dims, ((), ())), preferred_element_type=F32)


def _nn(a, b):
    return _bdot(a, b, ((1,), (0,)))


def _nt(a, b):
    return _bdot(a, b, ((1,), (1,)))


def _tn(a, b):
    return _bdot(a, b, ((0,), (0,)))


def _exact_nn(a, b):
    return lax.dot_general(a, b, (((1,), (0,)), ((), ())), precision=lax.Precision.HIGHEST,
                           preferred_element_type=F32)


def _exact_tn(a, b):
    return lax.dot_general(a, b, (((0,), (0,)), ((), ())), precision=lax.Precision.HIGHEST,
                           preferred_element_type=F32)


def _tri(rev):
    t = lax.broadcasted_iota(jnp.int32, (CHUNK, CHUNK), 0)
    s = lax.broadcasted_iota(jnp.int32, (CHUNK, CHUNK), 1)
    return (s >= t) if rev else (s <= t)


def _eye():
    t = lax.broadcasted_iota(jnp.int32, (CHUNK, CHUNK), 0)
    s = lax.broadcasted_iota(jnp.int32, (CHUNK, CHUNK), 1)
    return (s == t).astype(F32)


def _row_to_col(row):
    return jnp.sum(_eye() * row, axis=1, keepdims=True)


def _last_onehot(rev):
    t = lax.broadcasted_iota(jnp.int32, (CHUNK, 1), 0)
    return (t == (0 if rev else CHUNK - 1)).astype(F32)


def _head_slices(width, n_heads):
    hd = width // n_heads
    return [slice(h * hd, (h + 1) * hd) for h in range(n_heads)]


def _scan_sum(x, rev):
    n = x.shape[0]
    t = lax.broadcasted_iota(jnp.int32, x.shape, 0)
    s = 1
    while s < n:
        if rev:
            x = x + jnp.where(t < n - s, pltpu.roll(x, n - s, 0), 0.0)
        else:
            x = x + jnp.where(t >= s, pltpu.roll(x, s, 0), 0.0)
        s *= 2
    return x


def _dot3(a, b, dims):
    a_hi, b_hi = a.astype(BF16), b.astype(BF16)
    a_lo, b_lo = (a - a_hi.astype(F32)).astype(BF16), (b - b_hi.astype(F32)).astype(BF16)
    dot = lambda x, y: lax.dot_general(x, y, (dims, ((), ())), preferred_element_type=F32)
    return dot(a_hi, b_hi) + (dot(a_hi, b_lo) + dot(a_lo, b_hi))


def _hg_common(zq, zf, lb, rev):
    q = _silu(zq)
    sg = _sigmoid(zf)
    f = lb + (1.0 - lb) * sg
    g = jnp.log(f)
    k = 1.0 - f
    b = _scan_sum(g, rev)
    b_last = jnp.sum(g, axis=0, keepdims=True)
    r = b[CHUNK // 2:CHUNK // 2 + 1, :]
    e_up = jnp.exp(b - r)
    e_dn = jnp.exp(r - b)
    e_b = e_up * jnp.exp(r)
    e_lb = e_dn * jnp.exp(b_last - r)
    return dict(q=q, sg=sg, f=f, k=k, e_up=e_up, e_dn=e_dn, e_b=e_b, e_lb=e_lb, e_last=jnp.exp(b_last),
                q_t=q * e_up, k_t=k * e_dn, q_s=q * e_b, k_h=k * e_lb, tri=_tri(rev).astype(F32))


def hg_chunk_fwd(zq, zf, v, lb, st, rev):
    c = _hg_common(zq, zf, lb, rev)
    hs = _head_slices(zq.shape[1], zq.shape[1] // HG_D)
    s = [_nt(c["q_t"][:, sl], c["k_t"][:, sl]) for sl in hs]
    oi = [_nt(c["q_s"][:, sl], st[sl, :]) for sl in hs]
    ds = [_tn(v[:, sl], c["k_h"][:, sl]) for sl in hs]
    oa = [_nn(c["tri"] * s_h, v[:, sl]) for s_h, sl in zip(s, hs)]
    o = jnp.concatenate([x + y for x, y in zip(oi, oa)], axis=1)
    st_new = jnp.concatenate([st[sl, :] * c["e_last"][:, sl] + d for sl, d in zip(hs, ds)], axis=0)
    return o, st_new


def hg_chunk_bwd(zq, zf, v, lb, st, do, dst_new, rev):
    c = _hg_common(zq, zf, lb, rev)
    hs = _head_slices(zq.shape[1], zq.shape[1] // HG_D)
    tri, q_t, k_t, q_s, k_h = c["tri"], c["q_t"], c["k_t"], c["q_s"], c["k_h"]
    s = [_nt(q_t[:, sl], k_t[:, sl]) for sl in hs]
    da = [tri * _nt(do[:, sl], v[:, sl]) for sl in hs]
    dq_s = [_nn(do[:, sl], st[sl, :]) for sl in hs]
    dk_h = [_nn(v[:, sl], dst_new[sl, :]) for sl in hs]
    dv_s = [_nt(k_h[:, sl], dst_new[sl, :]) for sl in hs]
    dst_q = [_tn(do[:, sl], q_s[:, sl]) for sl in hs]
    dq_t = [_dot3(da_h, k_t[:, sl], ((1,), (0,))) for da_h, sl in zip(da, hs)]
    dk_t = [_dot3(da_h, q_t[:, sl], ((0,), (0,))) for da_h, sl in zip(da, hs)]
    dv_a = [_tn(tri * s_h, do[:, sl]) for s_h, sl in zip(s, hs)]
    cat = lambda parts: jnp.concatenate(parts, axis=1)
    dq_s, dk_h, dq_t, dk_t = cat(dq_s), cat(dk_h), cat(dq_t), cat(dk_t)
    dv = cat([x + y for x, y in zip(dv_a, dv_s)])
    dst = jnp.concatenate([dst_new[sl, :] * c["e_last"][:, sl] + d for sl, d in zip(hs, dst_q)], axis=0)
    dq = dq_s * c["e_b"] + dq_t * c["e_up"]
    dk = dk_t * c["e_dn"] + dk_h * c["e_lb"]
    db = c["q"] * dq - c["k"] * dk
    ss = cat([jnp.sum(dst_new[sl, :] * st[sl, :], axis=0, keepdims=True) for sl in hs])
    d_all = jnp.sum(dk_h * k_h, axis=0, keepdims=True) + c["e_last"] * ss
    dg = _scan_sum(db, not rev) + d_all
    dzq = dq * _dsilu(zq)
    df = dg / c["f"] - dk
    dzf = df * (1.0 - lb) * c["sg"] * (1.0 - c["sg"])
    dlb = jnp.sum(df * (1.0 - c["sg"]), axis=0, keepdims=True)
    return dzq, dzf, dv, dlb, dst


def _log_sigmoid(x):
    return jnp.minimum(x, 0.0) - jnp.log(1.0 + jnp.exp(-jnp.abs(x)))


def _each(fn, *lists):
    return [fn(*xs) for xs in zip(*lists)]


def _bf(xs):
    return [x.astype(BF16) for x in xs]


def _ml_forward_parts(qp, kp, v, gates, c, n, m, rev):
    hs = _head_slices(qp.shape[1], qp.shape[1] // ML_D)
    q_all = _silu(qp)
    k_all = _silu(kp) * (ML_D ** -0.5)
    q = [q_all[:, sl] for sl in hs]
    k = [k_all[:, sl] for sl in hs]
    vv = [v[:, sl] for sl in hs]
    cc = [c[sl, :] for sl in hs]
    tri_b = _tri(rev)
    tri = tri_b.astype(F32)
    tri_t = _tri(not rev).astype(F32)
    e_last = _last_onehot(rev)
    qb, kb, vb, cb = _bf(q), _bf(k), _bf(vv), _bf(cc)
    qk = _each(_nt, qb, kb)
    qc = _each(_nt, qb, cb)
    parts = []
    for (gi_c, gi_r, gf_c, gf_r), m_h in zip(gates, m):
        lf_c, lf_r = _log_sigmoid(gf_c), _log_sigmoid(gf_r)
        b_c = jnp.sum(tri * lf_r, axis=1, keepdims=True)
        b_r = jnp.sum(tri_t * lf_c, axis=0, keepdims=True)
        log_w = jnp.where(tri_b, b_c - b_r + gi_r, -jnp.inf)
        m_inter = b_c + m_h
        m_t = jnp.maximum(m_inter, jnp.max(log_w, axis=1, keepdims=True))
        m_new = jnp.sum(m_t * e_last, axis=0, keepdims=True)
        b_last = jnp.sum(b_c * e_last, axis=0, keepdims=True)
        parts.append(dict(a=jnp.exp(m_inter - m_t), p=jnp.exp(log_w - m_t), floor=jnp.exp(-m_t), m_new=m_new,
                          ws=jnp.exp(b_last - b_c + gi_c - m_new), decay=jnp.exp(b_last + m_h - m_new), gf_c=gf_c))
    w = [pt["p"] * x for pt, x in zip(parts, qk)]
    wb = _bf(w)
    wv = _each(_nn, wb, vb)
    for pt, q_h, n_h, w_h, qc_h, wv_h in zip(parts, q, n, w, qc, wv):
        qn = jnp.sum(q_h * n_h, axis=1, keepdims=True)
        num = pt["a"] * qc_h + wv_h
        den = pt["a"] * qn + jnp.sum(w_h, axis=1, keepdims=True)
        pt.update(qn=qn, num=num, den=den, rinv=1.0 / jnp.maximum(jnp.abs(den), pt["floor"]), w=w_h, qc=qc_h)
    return hs, q, k, vv, cc, tri, parts, dict(q=qb, k=kb, v=vb, c=cb, w=wb)


def ml_chunk_fwd(qp, kp, v, gates, c, n, m, rev):
    hs, q, k, vv, cc, tri, parts, bf = _ml_forward_parts(qp, kp, v, gates, c, n, m, rev)
    h = jnp.concatenate([pt["num"] * pt["rinv"] for pt in parts], axis=1)
    upd = _each(_tn, [pt["ws"] * v_h for pt, v_h in zip(parts, vv)], bf["k"])
    c_new = jnp.concatenate([pt["decay"] * c_h + u for pt, c_h, u in zip(parts, cc, upd)], axis=0)
    n_new = [pt["decay"] * n_h + jnp.sum(pt["ws"] * k_h, axis=0, keepdims=True) for pt, n_h, k_h in zip(parts, n, k)]
    return h, c_new, n_new, [pt["m_new"] for pt in parts]


def ml_chunk_bwd(qp, kp, v, gates, c, n, m, dh, dc_new, dn_new, rev):
    hs, q, k, vv, cc, tri, parts, bf = _ml_forward_parts(qp, kp, v, gates, c, n, m, rev)
    dcn = [dc_new[sl, :] for sl in hs]
    dcb = _bf(dcn)
    dnum, dden = [], []
    for pt, sl in zip(parts, hs):
        dh_h = dh[:, sl]
        h = pt["num"] * pt["rinv"]
        signed_live = jnp.where(jnp.abs(pt["den"]) > pt["floor"], jnp.where(pt["den"] >= 0.0, 1.0, -1.0), 0.0)
        dnum.append(dh_h * pt["rinv"])
        dden.append(-jnp.sum(dh_h * h, axis=1, keepdims=True) * pt["rinv"] * signed_live)
    dnb = _bf(dnum)
    dw = [x + y for x, y in zip(_each(_nt, dnb, bf["v"]), dden)]
    kdc = _each(_nt, bf["k"], dcb)
    vdc = _each(_nn, bf["v"], dcb)
    dqk = [x * pt["p"] for x, pt in zip(dw, parts)]
    adn = [pt["a"] * x for pt, x in zip(parts, dnum)]
    dqkb, adnb = _bf(dqk), _bf(adn)
    dv_w = _each(_tn, bf["w"], dnb)
    dq_k = _each(_nn, dqkb, bf["k"])
    dq_c = _each(_nn, adnb, bf["c"])
    dk_q = _each(_tn, dqkb, bf["q"])
    dc_q = _each(_tn, adnb, bf["q"])
    dq, dk, dv, dgi, dgf, dc, dn = [], [], [], [], [], [], []
    for i, pt in enumerate(parts):
        a, ws, decay = pt["a"], pt["ws"], pt["decay"]
        add = a * dden[i]
        e = dw[i] * pt["w"]
        dv.append(dv_w[i] + ws * kdc[i])
        dq.append(dq_k[i] + dq_c[i] + add * n[i])
        dk.append(dk_q[i] + ws * vdc[i] + ws * dn_new[i])
        alpha = (jnp.sum(dnum[i] * pt["qc"], axis=1, keepdims=True) + dden[i] * pt["qn"]) * a
        omega = (jnp.sum(vdc[i] * k[i], axis=1, keepdims=True) + jnp.sum(k[i] * dn_new[i], axis=1, keepdims=True)) * ws
        delta = decay * (jnp.sum(jnp.sum(dcn[i] * cc[i], axis=1, keepdims=True), axis=0, keepdims=True)
                         + jnp.sum(dn_new[i] * n[i], axis=1, keepdims=True))
        dc.append(decay * dcn[i] + dc_q[i])
        dn.append(decay * dn_new[i] + jnp.sum(add * q[i], axis=0, keepdims=True))
        e_rows = jnp.sum(e, axis=1, keepdims=True)
        e_cols = _row_to_col(jnp.sum(e, axis=0, keepdims=True))
        dgi.append(e_cols + omega)
        db = e_rows + alpha - e_cols - omega
        tail = jnp.sum(omega, axis=0, keepdims=True) + delta
        dlf = _row_to_col(jnp.sum(tri * db, axis=0, keepdims=True)) + tail
        dgf.append(dlf * (1.0 - _sigmoid(pt["gf_c"])))
    cat = lambda xs: jnp.concatenate(xs, axis=1)
    dqp = cat(dq) * _dsilu(qp)
    dkp = cat(dk) * (ML_D ** -0.5) * _dsilu(kp)
    return dqp, dkp, cat(dv), dgi, dgf, jnp.concatenate(dc, axis=0), dn


def _pick(n, prefs):
    for p in prefs:
        if n % p == 0:
            return p
    raise ValueError(f"no tile for {n} among {prefs}")


def _position():
    return lax.axis_index("x"), lax.axis_index("y"), lax.axis_index("c")


class _Ride:
    def __init__(self, kind, x, cols=None, into=None):
        self.kind, self.x, self.cols, self.into = kind, x, cols, into
        r, c = x.shape[-2:]
        self.out_shape = jax.ShapeDtypeStruct((N_DEV, r, c if cols is None else cols[1]), x.dtype)
        self.width = c

    def _copies(self, x_ref, out_ref, send_sems, recv_sems, local_sem):
        px, py, pc = _position()
        me = 4 * px + 2 * py + pc
        src = (lambda slot: x_ref) if self.kind == "gather" else (lambda slot: x_ref.at[slot])
        dst = ((lambda slot: out_ref.at[slot]) if self.cols is None
               else (lambda slot: out_ref.at[slot, :, pl.ds(self.cols[0], self.width)]))
        mine = pltpu.make_async_copy(src(me), dst(me), local_sem)
        sends, recvs = [], []
        for k, (fx, fy, fc) in enumerate([(1, 0, 0), (0, 1, 0), (1, 1, 0), (1, 0, 1), (0, 1, 1), (1, 1, 1), (0, 0, 1)]):
            qx, qy, qc = (1 - px if fx else px), (1 - py if fy else py), (1 - pc if fc else pc)
            peer = 4 * qx + 2 * qy + qc
            sends.append(pltpu.make_async_remote_copy(
                src_ref=src(peer), dst_ref=dst(me), send_sem=send_sems.at[k], recv_sem=recv_sems.at[k],
                device_id=(qx, qy, qc), device_id_type=MESH))
            recvs.append(pltpu.make_async_remote_copy(
                src_ref=src(me), dst_ref=dst(peer), send_sem=send_sems.at[k], recv_sem=recv_sems.at[k],
                device_id=(qx, qy, qc), device_id_type=MESH))
        return mine, sends, recvs

    def start(self, *refs):
        mine, sends, _ = self._copies(*refs)
        mine.start()
        for cp in sends:
            cp.start()

    def wait(self, *refs):
        mine, sends, recvs = self._copies(*refs)
        for cp in recvs:
            cp.wait_recv()
        for cp in sends:
            cp.wait_send()
        mine.wait()

    def operands(self):
        return [self.x] + ([self.into] if self.into is not None else [])


_RIDE_SCRATCH = [pltpu.SemaphoreType.DMA((7,)), pltpu.SemaphoreType.DMA((7,)), pltpu.SemaphoreType.DMA]
_ANY = pl.BlockSpec(memory_space=pl.ANY)


def _mm(a, b, mode, out_dtype, tm, tn, tk, name, ride=None, n_cols=None, n_off=0):
    if mode == "nn":
        (m, k), (k2, n) = a.shape, b.shape
    elif mode == "nt":
        (m, k), (n, k2) = a.shape, b.shape
    else:
        (k, m), (k2, n) = a.shape, b.shape
    n = n if n_cols is None else n_cols
    assert k == k2 and m % tm == 0 and n % tn == 0 and k % tk == 0, (a.shape, b.shape, mode, tm, tn, tk)
    nk = k // tk
    joff = n_off * (n // tn)
    dims = {"nn": ((1,), (0,)), "nt": ((1,), (1,)), "tn": ((0,), (0,))}[mode]
    a_spec = (pl.BlockSpec((tk, tm), lambda j, i, kk: (kk, i)) if mode == "tn"
              else pl.BlockSpec((tm, tk), lambda j, i, kk: (i, kk)))
    b_spec = (pl.BlockSpec((tn, tk), lambda j, i, kk: (j + joff, kk)) if mode == "nt"
              else pl.BlockSpec((tk, tn), lambda j, i, kk: (kk, j + joff)))

    grid = (n // tn, m // tm, nk)
    n_ride_in = len(ride.operands()) if ride is not None else 0

    def body(a_ref, b_ref, *rest):
        if ride is not None:
            x_ref = rest[0]
            o_ref, got_ref, acc_ref = rest[n_ride_in:n_ride_in + 3]
            comm = (x_ref, got_ref) + tuple(rest[n_ride_in + 3:])
        else:
            o_ref, acc_ref = rest
        kk = pl.program_id(2)
        step = (pl.program_id(0) * grid[1] + pl.program_id(1)) * nk + kk
        if ride is not None:
            @pl.when(step == 0)
            def _():
                ride.start(*comm)

        part = lax.dot_general(a_ref[...], b_ref[...], (dims, ((), ())), preferred_element_type=F32)

        @pl.when(kk == 0)
        def _():
            acc_ref[...] = part

        @pl.when(kk > 0)
        def _():
            acc_ref[...] += part

        @pl.when(kk == nk - 1)
        def _():
            o_ref[...] = acc_ref[...].astype(o_ref.dtype)

        if ride is not None:
            @pl.when(step == grid[0] * grid[1] * nk - 1)
            def _():
                ride.wait(*comm)

    osz = jnp.dtype(out_dtype).itemsize
    need = 2 * (tm * tk * a.dtype.itemsize + tk * tn * b.dtype.itemsize + tm * tn * osz) + tm * tn * 4
    o_spec = pl.BlockSpec((tm, tn), lambda j, i, kk: (i, j))
    o_shape = jax.ShapeDtypeStruct((m, n), out_dtype)
    extra = ride is not None
    return pl.pallas_call(
        body, name=name, grid=grid,
        in_specs=[a_spec, b_spec] + [_ANY] * n_ride_in,
        out_specs=[o_spec, _ANY] if extra else o_spec,
        out_shape=[o_shape, ride.out_shape] if extra else o_shape,
        scratch_shapes=[pltpu.VMEM((tm, tn), F32)] + (_RIDE_SCRATCH if extra else []),
        input_output_aliases={3: 1} if extra and ride.into is not None else {},
        compiler_params=_vmem(need + (12 << 20)),
    )(a, b, *(ride.operands() if extra else []))


ROWS = 256


def _ln_stats(x):
    mu = jnp.mean(x, axis=-1, keepdims=True)
    xc = x - mu
    var = jnp.mean(xc * xc, axis=-1, keepdims=True)
    rstd = lax.rsqrt(var + LN_EPS)
    return xc * rstd, rstd


def _token_specs(nbc, nbx, d):
    return [pl.BlockSpec((ROWS, d), lambda i: (jnp.minimum(i, nbc - 1), 0)),
            pl.BlockSpec((ROWS, d), lambda i: (jnp.maximum(i - nbc, 0), 0))]


def _tokens(c_ref, x_ref, nbc):
    return jnp.where(pl.program_id(0) < nbc, c_ref[...], x_ref[...])


def _modulate_fwd(ctx, x, modp):
    d = x.shape[1]
    nbc, nbx = ctx.shape[0] // ROWS, x.shape[0] // ROWS

    def body(c_ref, x_ref, mod_ref, o_ref):
        n, _ = _ln_stats(_tokens(c_ref, x_ref, nbc))
        o_ref[...] = (n * (1.0 + mod_ref[0, 1:2, :]) + mod_ref[0, 0:1, :]).astype(BF16)

    return pl.pallas_call(
        body, name="modulate_fwd", grid=(nbc + nbx,),
        in_specs=_token_specs(nbc, nbx, d) + [pl.BlockSpec((1, 3, d), lambda i: (jnp.where(i >= nbc, 1, 0), 0, 0))],
        out_specs=pl.BlockSpec((ROWS, d), lambda i: (i, 0)),
        out_shape=jax.ShapeDtypeStruct((ctx.shape[0] + x.shape[0], d), BF16),
    )(ctx, x, modp)


def _modulate_bwd(dh, ctx, x, modp, dxa):
    t, d = x.shape
    nbc, nbx = ctx.shape[0] // ROWS, t // ROWS

    def body(dh_ref, c_ref, x_ref, mod_ref, dxa_ref, gx_ref, sum_ref):
        i = pl.program_id(0)
        n, rstd = _ln_stats(_tokens(c_ref, x_ref, nbc))
        g = dh_ref[...]
        dn = g * (1.0 + mod_ref[0, 1:2, :])
        dx = rstd * (dn - jnp.mean(dn, axis=-1, keepdims=True) - n * jnp.mean(dn * n, axis=-1, keepdims=True))
        gx_ref[...] = dx + dxa_ref[...]
        dshift = jnp.sum(g, axis=0, keepdims=True)
        dscale = jnp.sum(g * n, axis=0, keepdims=True)

        @pl.when(i == 0)
        def _():
            sum_ref[...] = jnp.zeros_like(sum_ref)

        @pl.when(i < nbc)
        def _():
            sum_ref[0:1, :] += dshift
            sum_ref[1:2, :] += dscale

        @pl.when(i >= nbc)
        def _():
            sum_ref[2:3, :] += dshift
            sum_ref[3:4, :] += dscale

    lat = lambda i: (jnp.maximum(i - nbc, 0), 0)
    return pl.pallas_call(
        body, name="modulate_bwd", grid=(nbc + nbx,),
        in_specs=[pl.BlockSpec((ROWS, d), lambda i: (i, 0))] + _token_specs(nbc, nbx, d)
        + [pl.BlockSpec((1, 3, d), lambda i: (jnp.where(i >= nbc, 1, 0), 0, 0)), pl.BlockSpec((ROWS, d), lat)],
        out_specs=[pl.BlockSpec((ROWS, d), lat), pl.BlockSpec((8, d), lambda i: (0, 0))],
        out_shape=[jax.ShapeDtypeStruct((t, d), F32), jax.ShapeDtypeStruct((8, d), F32)],
    )(dh, ctx, x, modp, dxa)


def _post_fwd(o_f, o_b, h_f, h_b, u, hgw, mlw, nbc):
    tt = u.shape[0]
    t = tt - nbc * ROWS

    def body(of_ref, ob_ref, hf_ref, hb_ref, az_ref, bo_ref, bz_ref, hgw_ref, mlw_ref, y_ref):
        o = of_ref[...] + ob_ref[...]
        for sl in _head_slices(W_A, HG_HEADS):
            oh = o[:, sl]
            rs = lax.rsqrt(jnp.mean(oh * oh, axis=-1, keepdims=True) + NORM_EPS)
            y_ref[:, sl] = (oh * rs * hgw_ref[:, sl] * _silu(az_ref[:, sl])).astype(BF16)
        hm = hf_ref[...] + hb_ref[...]
        for sl in _head_slices(W_B, ML_HEADS):
            hh = hm[:, sl]
            mu = jnp.mean(hh, axis=-1, keepdims=True)
            hc = hh - mu
            rstd = lax.rsqrt(jnp.mean(hc * hc, axis=-1, keepdims=True) + NORM_EPS)
            out = hc * rstd * mlw_ref[:, sl] * _sigmoid(bo_ref[:, sl]) * _silu(bz_ref[:, sl])
            y_ref[:, W_A + sl.start:W_A + sl.stop] = out.astype(BF16)

    row = lambda i: (i + nbc, 0)
    seg = lambda s: pl.BlockSpec((ROWS, 1024), lambda i: (i + nbc, s))
    wspec = pl.BlockSpec((1, 1024), lambda i: (0, 0))
    return pl.pallas_call(
        body, name="post_fwd", grid=(t // ROWS,),
        in_specs=[pl.BlockSpec((ROWS, 1024), row)] * 4 + [seg(SEG_AZ), seg(SEG_BO), seg(SEG_BZ), wspec, wspec],
        out_specs=pl.BlockSpec((ROWS, 2048), lambda i: (i, 0)),
        out_shape=jax.ShapeDtypeStruct((t, 2048), BF16),
    )(o_f, o_b, h_f, h_b, u, u, u, hgw, mlw)


def _post_bwd(dz, w_o, o_f, o_b, h_f, h_b, u, hgw, mlw, nbc):
    tt = u.shape[0]
    d = w_o.shape[0]

    def body(dz_ref, w_ref, of_ref, ob_ref, hf_ref, hb_ref, az_ref, bo_ref, bz_ref, hgw_ref, mlw_ref,
             do_ref, dhm_ref, du_ref, sum_ref):
        i = pl.program_id(0)
        live = jnp.where(i >= nbc, 1.0, 0.0)
        dy = lax.dot_general(dz_ref[...], w_ref[...], (((1,), (1,)), ((), ())), preferred_element_type=F32) * live

        @pl.when(i == 0)
        def _():
            sum_ref[...] = jnp.zeros_like(sum_ref)

        o = of_ref[...] + ob_ref[...]
        for sl in _head_slices(W_A, HG_HEADS):
            oh = o[:, sl]
            rs = lax.rsqrt(jnp.mean(oh * oh, axis=-1, keepdims=True) + NORM_EPS)
            on = oh * rs
            az = az_ref[:, sl]
            dya = dy[:, sl]
            doa = dya * _silu(az)
            du_ref[:, sl] = (dya * on * hgw_ref[:, sl] * _dsilu(az)).astype(BF16)
            sum_ref[0:1, sl] += jnp.sum(doa * on, axis=0, keepdims=True)
            don = doa * hgw_ref[:, sl]
            do_ref[:, sl] = rs * (don - on * jnp.mean(don * on, axis=-1, keepdims=True))
        hm = hf_ref[...] + hb_ref[...]
        for sl in _head_slices(W_B, ML_HEADS):
            hh = hm[:, sl]
            mu = jnp.mean(hh, axis=-1, keepdims=True)
            hc = hh - mu
            rstd = lax.rsqrt(jnp.mean(hc * hc, axis=-1, keepdims=True) + NORM_EPS)
            hn = hc * rstd
            hw = hn * mlw_ref[:, sl]
            bo, bz = bo_ref[:, sl], bz_ref[:, sl]
            sbo, sbz = _sigmoid(bo), _silu(bz)
            dyb = dy[:, W_A + sl.start:W_A + sl.stop]
            dhw = dyb * sbo * sbz
            du_ref[:, 1024 + sl.start:1024 + sl.stop] = (dyb * hw * sbz * sbo * (1.0 - sbo)).astype(BF16)
            du_ref[:, 2048 + sl.start:2048 + sl.stop] = (dyb * hw * sbo * _dsilu(bz)).astype(BF16)
            sum_ref[1:2, sl] += jnp.sum(dhw * hn, axis=0, keepdims=True)
            dhn = dhw * mlw_ref[:, sl]
            dhm_ref[:, sl] = rstd * (dhn - jnp.mean(dhn, axis=-1, keepdims=True)
                                     - hn * jnp.mean(dhn * hn, axis=-1, keepdims=True))

    row = lambda i: (i, 0)
    seg = lambda s: pl.BlockSpec((ROWS, 1024), lambda i: (i, s))
    wspec = pl.BlockSpec((1, 1024), lambda i: (0, 0))
    return pl.pallas_call(
        body, name="post_bwd", grid=(tt // ROWS,),
        in_specs=[pl.BlockSpec((ROWS, 2048), lambda i: (jnp.maximum(i - nbc, 0), 0)), pl.BlockSpec((d, d), lambda i: (0, 0))]
        + [pl.BlockSpec((ROWS, 1024), row)] * 4 + [seg(SEG_AZ), seg(SEG_BO), seg(SEG_BZ), wspec, wspec],
        out_specs=[pl.BlockSpec((ROWS, 1024), row), pl.BlockSpec((ROWS, 1024), row),
                   pl.BlockSpec((ROWS, 3072), row), pl.BlockSpec((8, 1024), lambda i: (0, 0))],
        out_shape=[jax.ShapeDtypeStruct((tt, 1024), F32), jax.ShapeDtypeStruct((tt, 1024), F32),
                   jax.ShapeDtypeStruct((tt, 3072), BF16), jax.ShapeDtypeStruct((8, 1024), F32)],
        compiler_params=_vmem(4 * d * d + 30 * ROWS * 2048 * 4),
    )(dz, w_o, o_f, o_b, h_f, h_b, u, u, u, hgw, mlw)


def _final(y, w_o, x, target, modp, ln_g, ln_b):
    t, d = x.shape

    def body(y_ref, w_ref, x_ref, tg_ref, mod_ref, g_ref, b_ref, dz_ref, dxa_ref, sum_ref):
        i = pl.program_id(0)
        zz = lax.dot_general(y_ref[...], w_ref[...], (((1,), (0,)), ((), ())), preferred_element_type=F32)
        gate = mod_ref[0, 2:3, :]
        pre = ALPHA * x_ref[...] + gate * zz
        nh, rstd = _ln_stats(pre)
        err = nh * g_ref[...] + b_ref[...] - tg_ref[...]
        dxo = err * (1.0 / d)
        dnh = dxo * g_ref[...]
        dpre = rstd * (dnh - jnp.mean(dnh, axis=-1, keepdims=True) - nh * jnp.mean(dnh * nh, axis=-1, keepdims=True))
        dz_ref[...] = (gate * dpre).astype(BF16)
        dxa_ref[...] = ALPHA * dpre

        @pl.when(i == 0)
        def _():
            sum_ref[...] = jnp.zeros_like(sum_ref)

        sum_ref[0:1, :] += jnp.sum(dpre * zz, axis=0, keepdims=True)
        sum_ref[1:2, :] += jnp.sum(dxo * nh, axis=0, keepdims=True)
        sum_ref[2:3, :] += jnp.sum(dxo, axis=0, keepdims=True)
        sum_ref[3:4, :] += jnp.sum(err * err, axis=0, keepdims=True)

    row = lambda i: (i, 0)
    vec = pl.BlockSpec((1, d), lambda i: (0, 0))
    return pl.pallas_call(
        body, name="final_ln_loss", grid=(t // ROWS,),
        in_specs=[pl.BlockSpec((ROWS, d), row), pl.BlockSpec((d, d), lambda i: (0, 0)), pl.BlockSpec((ROWS, d), row),
                  pl.BlockSpec((ROWS, d), row), pl.BlockSpec((1, 3, d), lambda i: (1, 0, 0)), vec, vec],
        out_specs=[pl.BlockSpec((ROWS, d), row), pl.BlockSpec((ROWS, d), row), pl.BlockSpec((8, d), lambda i: (0, 0))],
        out_shape=[jax.ShapeDtypeStruct((t, d), BF16), jax.ShapeDtypeStruct((t, d), F32),
                   jax.ShapeDtypeStruct((8, d), F32)],
        compiler_params=_vmem(4 * d * d + 24 * ROWS * d * 4),
    )(y, w_o, x, target, modp, ln_g, ln_b)


GRID_W = 64


def _shift(x, s, ok):
    n = x.shape[0]
    return jnp.where(ok, pltpu.roll(x, s % n, 0), 0.0)


def _grid_masks(n):
    t = lax.broadcasted_iota(jnp.int32, (n, LANE), 0)
    col = t & (GRID_W - 1)
    return dict(left=col >= 1, right=col <= GRID_W - 2, up=t >= GRID_W, down=t < n - GRID_W)


def _seq_masks(n):
    t = lax.broadcasted_iota(jnp.int32, (n, LANE), 0)
    return dict(left=t >= 1, right=t <= n - 2)


def _conv_fwd(u, w9, cb, tc):
    tt = u.shape[0]
    t = tt - tc

    def body(u_ref, w_ref, b_ref, o_ref):
        w = [w_ref[r:r + 1, :] for r in range(9)]
        xc = u_ref[0:tc, :]
        ms = _seq_masks(tc)
        o_ref[0:tc, :] = (w[3] * _shift(xc, 1, ms["left"]) + w[4] * xc + w[5] * _shift(xc, -1, ms["right"])
                          + b_ref[...])
        x = u_ref[tc:tt, :]
        mg = _grid_masks(t)
        taps = (_shift(x, 1, mg["left"]), x, _shift(x, -1, mg["right"]))
        rows = [w[3 * i] * taps[0] + w[3 * i + 1] * taps[1] + w[3 * i + 2] * taps[2] for i in range(3)]
        o_ref[tc:tt, :] = (rows[1] + _shift(rows[0], GRID_W, mg["up"]) + _shift(rows[2], -GRID_W, mg["down"])
                           + b_ref[...])

    return pl.pallas_call(
        body, name="conv_fwd", grid=(2048 // LANE,),
        in_specs=[pl.BlockSpec((tt, LANE), lambda j: (0, BLK_QK + j)), pl.BlockSpec((9, LANE), lambda j: (0, j)),
                  pl.BlockSpec((1, LANE), lambda j: (0, j))],
        out_specs=pl.BlockSpec((tt, LANE), lambda j: (0, j)),
        out_shape=jax.ShapeDtypeStruct((tt, 2048), F32),
        compiler_params=_vmem(40 * tt * LANE * 4),
    )(u, w9, cb)


def _conv_bwd(dcp, u, w9, tc):
    tt = u.shape[0]
    t = tt - tc

    def body(d_ref, u_ref, w_ref, du_ref, gw_ref, gb_ref):
        w = [w_ref[r:r + 1, :] for r in range(9)]
        csum = lambda a: jnp.sum(a, axis=0, keepdims=True)
        dc = d_ref[0:tc, :]
        xc = u_ref[0:tc, :]
        ms = _seq_masks(tc)
        du_ref[0:tc, :] = (w[3] * _shift(dc, -1, ms["right"]) + w[4] * dc + w[5] * _shift(dc, 1, ms["left"])).astype(BF16)
        gmid = [csum(dc * _shift(xc, 1, ms["left"])), csum(dc * xc), csum(dc * _shift(xc, -1, ms["right"]))]
        d = d_ref[tc:tt, :]
        x = u_ref[tc:tt, :]
        mg = _grid_masks(t)
        dtaps = (_shift(d, -1, mg["right"]), d, _shift(d, 1, mg["left"]))
        rows = [w[3 * i] * dtaps[0] + w[3 * i + 1] * dtaps[1] + w[3 * i + 2] * dtaps[2] for i in range(3)]
        du_ref[tc:tt, :] = (rows[1] + _shift(rows[0], -GRID_W, mg["down"]) + _shift(rows[2], GRID_W, mg["up"])).astype(BF16)
        xtaps = (_shift(x, 1, mg["left"]), x, _shift(x, -1, mg["right"]))
        for j in range(3):
            gw_ref[j:j + 1, :] = csum(d * _shift(xtaps[j], GRID_W, mg["up"]))
            gw_ref[3 + j:4 + j, :] = csum(d * xtaps[j]) + gmid[j]
            gw_ref[6 + j:7 + j, :] = csum(d * _shift(xtaps[j], -GRID_W, mg["down"]))
        gb_ref[...] = csum(d) + csum(dc)

    return pl.pallas_call(
        body, name="conv_bwd", grid=(2048 // LANE,),
        in_specs=[pl.BlockSpec((tt, LANE), lambda j: (0, j)), pl.BlockSpec((tt, LANE), lambda j: (0, BLK_QK + j)),
                  pl.BlockSpec((9, LANE), lambda j: (0, j))],
        out_specs=[pl.BlockSpec((tt, LANE), lambda j: (0, j)), pl.BlockSpec((9, LANE), lambda j: (0, j)),
                   pl.BlockSpec((1, LANE), lambda j: (0, j))],
        out_shape=[jax.ShapeDtypeStruct((tt, 2048), BF16), jax.ShapeDtypeStruct((9, 2048), F32),
                   jax.ShapeDtypeStruct((1, 2048), F32)],
        compiler_params=_vmem(48 * tt * LANE * 4),
    )(dcp, u, w9)


SUB = 2
STEP = SUB * CHUNK


def _chunk_of(pos, ncc, nc, rev):
    if not rev:
        return pos
    return jnp.where(pos < ncc, ncc - 1 - pos, nc - 1 - (pos - ncc))


def _sub_rows(rev):
    order = range(SUB - 1, -1, -1) if rev else range(SUB)
    return [(s, slice(s * CHUNK, (s + 1) * CHUNK)) for s in order]


def _hgrn_fwd(u, lower_d, ncc, rev):
    tt = u.shape[0]
    nc, ncc = tt // STEP, ncc // SUB
    seg_f = SEG_AFB if rev else SEG_AFF

    def body(zq_ref, zf_ref, v_ref, lb_ref, o_ref, hist_ref, st_ref):
        @pl.when(pl.program_id(0) == 0)
        def _():
            st_ref[...] = jnp.zeros_like(st_ref)

        st = st_ref[...]
        for s, r in _sub_rows(rev):
            hist_ref[s] = st
            o, st = hg_chunk_fwd(zq_ref[r, :], zf_ref[r, :], v_ref[r, :], lb_ref[...], st, rev)
            o_ref[r, :] = o
        st_ref[...] = st

    seg = lambda s: pl.BlockSpec((STEP, 1024), lambda j: (_chunk_of(j, ncc, nc, rev), s))
    return pl.pallas_call(
        body, name="hgrn_fwd_rev" if rev else "hgrn_fwd", grid=(nc,),
        in_specs=[seg(SEG_AQ), seg(seg_f), seg(SEG_AI), pl.BlockSpec((1, 1024), lambda j: (0, 0))],
        out_specs=[pl.BlockSpec((STEP, 1024), lambda j: (_chunk_of(j, ncc, nc, rev), 0)),
                   pl.BlockSpec((SUB, 1024, HG_D), lambda j: (_chunk_of(j, ncc, nc, rev), 0, 0))],
        out_shape=[jax.ShapeDtypeStruct((tt, 1024), F32), jax.ShapeDtypeStruct((nc * SUB, 1024, HG_D), F32)],
        scratch_shapes=[pltpu.VMEM((1024, HG_D), F32)],
    )(u, u, u, lower_d)


def _hgrn_bwd(u, lower_d, hist, do, acc, ncc, rev, ride=None):
    tt = u.shape[0]
    nc, ncc = tt // STEP, ncc // SUB
    seg_f = SEG_AFB if rev else SEG_AFF
    has_acc = acc is not None
    has_a2a = ride is not None
    odt = BF16 if has_acc else F32

    def body(zq_ref, zf_ref, v_ref, lb_ref, hist_ref, do_ref, *rest):
        if has_acc:
            aq_ref, av_ref = rest[:2]
            rest = rest[2:]
        if has_a2a:
            x_ref, rest = rest[0], rest[1:]
        dzf_ref, dzq_ref, dv_ref, dlb_ref = rest[:4]
        rest = rest[4:]
        if has_a2a:
            comm = (x_ref, rest[0]) + tuple(rest[2:])
            dst_ref = rest[1]
        else:
            dst_ref = rest[0]

        @pl.when(pl.program_id(0) == 0)
        def _():
            dst_ref[...] = jnp.zeros_like(dst_ref)
            dlb_ref[...] = jnp.zeros_like(dlb_ref)
            if has_a2a:
                ride.start(*comm)

        dst = dst_ref[...]
        dlb_sum = dlb_ref[...]
        for s, r in reversed(_sub_rows(rev)):
            dzq, dzf, dv, dlb, dst = hg_chunk_bwd(zq_ref[r, :], zf_ref[r, :], v_ref[r, :], lb_ref[...],
                                                  hist_ref[s], do_ref[r, :], dst, rev)
            dlb_sum = dlb_sum + dlb
            dzf_ref[r, :] = dzf.astype(BF16)
            if has_acc:
                dzq = dzq + aq_ref[r, :]
                dv = dv + av_ref[r, :]
            dzq_ref[r, :] = dzq.astype(odt)
            dv_ref[r, :] = dv.astype(odt)
        dst_ref[...] = dst
        dlb_ref[...] = dlb_sum

        if has_a2a:
            @pl.when(pl.program_id(0) == nc - 1)
            def _():
                ride.wait(*comm)

    cidx = lambda j: _chunk_of(nc - 1 - j, ncc, nc, rev)
    seg = lambda s: pl.BlockSpec((STEP, 1024), lambda j: (cidx(j), s))
    row = pl.BlockSpec((STEP, 1024), lambda j: (cidx(j), 0))
    ins = [u, u, u, lower_d, hist, do] + (list(acc) if has_acc else []) + ([ride.x] if has_a2a else [])
    return pl.pallas_call(
        body, name="hgrn_bwd_rev" if rev else "hgrn_bwd", grid=(nc,),
        in_specs=[seg(SEG_AQ), seg(seg_f), seg(SEG_AI), pl.BlockSpec((1, 1024), lambda j: (0, 0)),
                  pl.BlockSpec((SUB, 1024, HG_D), lambda j: (cidx(j), 0, 0)), row] + ([row, row] if has_acc else [])
        + ([_ANY] if has_a2a else []),
        out_specs=[row, row, row, pl.BlockSpec((1, 1024), lambda j: (0, 0))] + ([_ANY] if has_a2a else []),
        out_shape=[jax.ShapeDtypeStruct((tt, 1024), BF16), jax.ShapeDtypeStruct((tt, 1024), odt),
                   jax.ShapeDtypeStruct((tt, 1024), odt), jax.ShapeDtypeStruct((1, 1024), F32)]
        + ([ride.out_shape] if has_a2a else []),
        scratch_shapes=[pltpu.VMEM((1024, HG_D), F32)] + (_RIDE_SCRATCH if has_a2a else []),
    )(*ins)


def _gate_views(gc_ref, gr_ref, bc_ref, br_ref, s, head, rev):
    gc = gc_ref[s] + bc_ref[...]
    gr = gr_ref[s] + br_ref[...]
    lane = lax.broadcasted_iota(jnp.int32, (1, 16), 1)
    sub = lax.broadcasted_iota(jnp.int32, (16, 1), 0)
    d = 1 if rev else 0
    ii, fi = d * ML_HEADS + head, 2 * ML_HEADS + d * ML_HEADS + head
    col = lambda idx: jnp.sum(jnp.where(lane == idx, gc, 0.0), axis=1, keepdims=True)
    row = lambda idx: jnp.sum(jnp.where(sub == idx, gr, 0.0), axis=0, keepdims=True)
    return col(ii), row(ii), col(fi), row(fi)


def _mlstm_fwd(cpre, u, gcol, grow, bias_c, bias_r, ncc, rev):
    tt = u.shape[0]
    nc, ncc = tt // STEP, ncc // SUB
    nhd = ML_HEADS

    def body(q_ref, k_ref, v_ref, gc_ref, gr_ref, bc_ref, br_ref, h_ref, ch_ref, nh_ref, mh_ref, c_ref, n_ref, m_ref):
        @pl.when(pl.program_id(0) == 0)
        def _():
            c_ref[...] = jnp.zeros_like(c_ref)
            n_ref[...] = jnp.zeros_like(n_ref)
            m_ref[...] = jnp.zeros_like(m_ref)

        c, n_all, m_all = c_ref[...], n_ref[...], m_ref[...]
        n = [n_all[hd:hd + 1, :] for hd in range(nhd)]
        m = [m_all[hd:hd + 1, 0:1] for hd in range(nhd)]
        for s, r in _sub_rows(rev):
            ch_ref[s] = c
            for hd in range(nhd):
                nh_ref[s, hd:hd + 1, :] = n[hd]
                mh_ref[s, hd:hd + 1, :] = jnp.broadcast_to(m[hd], (1, LANE))
            gates = [_gate_views(gc_ref, gr_ref, bc_ref, br_ref, s, hd, rev) for hd in range(nhd)]
            h, c, n, m = ml_chunk_fwd(q_ref[r, :], k_ref[r, :], v_ref[r, :], gates, c, n, m, rev)
            h_ref[r, :] = h
        c_ref[...] = c
        for hd in range(nhd):
            n_ref[hd:hd + 1, :] = n[hd]
            m_ref[hd:hd + 1, :] = jnp.broadcast_to(m[hd], (1, LANE))

    cidx = lambda j: _chunk_of(j, ncc, nc, rev)
    row = lambda s: pl.BlockSpec((STEP, 1024), lambda j: (cidx(j), s))
    st3 = lambda a, b: pl.BlockSpec((SUB, a, b), lambda j: (cidx(j), 0, 0))
    return pl.pallas_call(
        body, name="mlstm_fwd_rev" if rev else "mlstm_fwd", grid=(nc,),
        in_specs=[row(0), row(1), row(SEG_BV), st3(CHUNK, 16), st3(16, CHUNK),
                  pl.BlockSpec((1, 16), lambda j: (0, 0)), pl.BlockSpec((16, 1), lambda j: (0, 0))],
        out_specs=[row(0), st3(1024, ML_D), st3(8, ML_D), st3(8, LANE)],
        out_shape=[jax.ShapeDtypeStruct((tt, 1024), F32), jax.ShapeDtypeStruct((nc * SUB, 1024, ML_D), F32),
                   jax.ShapeDtypeStruct((nc * SUB, 8, ML_D), F32), jax.ShapeDtypeStruct((nc * SUB, 8, LANE), F32)],
        scratch_shapes=[pltpu.VMEM((1024, ML_D), F32), pltpu.VMEM((8, ML_D), F32), pltpu.VMEM((8, LANE), F32)],
    )(cpre, cpre, u, gcol, grow, bias_c, bias_r)


def _mlstm_bwd(cpre, u, gcol, grow, bias_c, bias_r, chist, nhist, mhist, dh, acc, ncc, rev):
    tt = u.shape[0]
    nc, ncc = tt // STEP, ncc // SUB
    nhd = ML_HEADS
    has_acc = acc is not None
    odt = BF16 if has_acc else F32
    d = 1 if rev else 0

    def body(q_ref, k_ref, v_ref, gc_ref, gr_ref, bc_ref, br_ref, ch_ref, nh_ref, mh_ref, dh_ref, *rest):
        if has_acc:
            aqk_ref, av_ref, ag_ref = rest[:3]
            rest = rest[3:]
        dqk_ref, dv_ref, dg_ref, gs_ref, dc_ref, dn_ref = rest

        @pl.when(pl.program_id(0) == 0)
        def _():
            dc_ref[...] = jnp.zeros_like(dc_ref)
            dn_ref[...] = jnp.zeros_like(dn_ref)
            gs_ref[...] = jnp.zeros_like(gs_ref)

        lane = lax.broadcasted_iota(jnp.int32, (1, LANE), 1)
        dc, dn_all, gs = dc_ref[...], dn_ref[...], gs_ref[...]
        dn = [dn_all[hd:hd + 1, :] for hd in range(nhd)]
        for s, r in reversed(_sub_rows(rev)):
            gates = [_gate_views(gc_ref, gr_ref, bc_ref, br_ref, s, hd, rev) for hd in range(nhd)]
            n_all, m_all = nh_ref[s], mh_ref[s]
            dqp, dkp, dv, dgi, dgf, dc, dn = ml_chunk_bwd(
                q_ref[r, :], k_ref[r, :], v_ref[r, :], gates, ch_ref[s],
                [n_all[hd:hd + 1, :] for hd in range(nhd)], [m_all[hd:hd + 1, 0:1] for hd in range(nhd)],
                dh_ref[r, :], dc, dn, rev)
            dg = ag_ref[r, :] if has_acc else jnp.zeros((CHUNK, LANE), F32)
            for hd in range(nhd):
                dg = dg + jnp.where(lane == d * ML_HEADS + hd, dgi[hd], 0.0)
                dg = dg + jnp.where(lane == 2 * ML_HEADS + d * ML_HEADS + hd, dgf[hd], 0.0)
            if has_acc:
                dqp = dqp + aqk_ref[r, 0:W_B]
                dkp = dkp + aqk_ref[r, W_B:2 * W_B]
                dv = dv + av_ref[r, :]
            dqk_ref[r, 0:W_B] = dqp
            dqk_ref[r, W_B:2 * W_B] = dkp
            dv_ref[r, :] = dv.astype(odt)
            dg_ref[r, :] = dg.astype(odt)
            gs = gs + jnp.sum(dg, axis=0, keepdims=True)
        dc_ref[...] = dc
        gs_ref[...] = gs
        for hd in range(nhd):
            dn_ref[hd:hd + 1, :] = dn[hd]

    cidx = lambda j: _chunk_of(nc - 1 - j, ncc, nc, rev)
    row = lambda s: pl.BlockSpec((STEP, 1024), lambda j: (cidx(j), s))
    wide = pl.BlockSpec((STEP, 2048), lambda j: (cidx(j), 0))
    gate = pl.BlockSpec((STEP, LANE), lambda j: (cidx(j), 0))
    st3 = lambda a, b: pl.BlockSpec((SUB, a, b), lambda j: (cidx(j), 0, 0))
    ins = [cpre, cpre, u, gcol, grow, bias_c, bias_r, chist, nhist, mhist, dh] + (list(acc) if has_acc else [])
    return pl.pallas_call(
        body, name="mlstm_bwd_rev" if rev else "mlstm_bwd", grid=(nc,),
        in_specs=[row(0), row(1), row(SEG_BV), st3(CHUNK, 16), st3(16, CHUNK),
                  pl.BlockSpec((1, 16), lambda j: (0, 0)), pl.BlockSpec((16, 1), lambda j: (0, 0)),
                  st3(1024, ML_D), st3(8, ML_D), st3(8, LANE), row(0)] + ([wide, row(0), gate] if has_acc else []),
        out_specs=[wide, row(0), gate, pl.BlockSpec((1, LANE), lambda j: (0, 0))],
        out_shape=[jax.ShapeDtypeStruct((tt, 2048), F32), jax.ShapeDtypeStruct((tt, 1024), odt),
                   jax.ShapeDtypeStruct((tt, LANE), odt), jax.ShapeDtypeStruct((1, LANE), F32)],
        scratch_shapes=[pltpu.VMEM((1024, ML_D), F32), pltpu.VMEM((8, ML_D), F32)],
    )(*ins)


def _whole(body, out_shape, name, *args, nbytes=0):
    return pl.pallas_call(body, name=name, out_shape=out_shape, compiler_params=_vmem(nbytes))(*args)


def _mod_fwd(cs, w_cols, b_cols):
    def body(c_ref, w_ref, b_ref, o_ref):
        o_ref[...] = _exact_nn(_silu(c_ref[...]), w_ref[...]) + b_ref[...]

    return _whole(body, jax.ShapeDtypeStruct((16, w_cols.shape[1]), F32), "mod_fwd", cs, w_cols, b_cols,
                  nbytes=4 * w_cols.size * 4)


def _mod_bwd_w(cs, d9, w_cols):
    def body(c_ref, d_ref, w_ref, gw_ref, pc_ref):
        gw_ref[...] = _exact_tn(_silu(c_ref[...]), d_ref[...])
        pc = lax.dot_general(d_ref[8:16, :], w_ref[...], (((1,), (1,)), ((), ())), precision=lax.Precision.HIGHEST,
                             preferred_element_type=F32)
        row = lax.broadcasted_iota(jnp.int32, pc.shape, 0)
        pc_ref[...] = jnp.where(row == 0, pc, 0.0)

    return _whole(body, [jax.ShapeDtypeStruct(w_cols.shape, F32), jax.ShapeDtypeStruct((8, w_cols.shape[0]), F32)],
                  "mod_bwd_w", cs, d9, w_cols, nbytes=6 * w_cols.size * 4)


def _lower_fwd(lb4):
    def body(l_ref, o_ref):
        o_ref[...] = jnp.zeros_like(o_ref)
        o_ref[0:1, :] = 1.0 / (1.0 + jnp.exp(l_ref[1:2, :] - l_ref[0:1, :]))
        o_ref[1:2, :] = 1.0 / (1.0 + jnp.exp(l_ref[3:4, :] - l_ref[2:3, :]))

    return _whole(body, jax.ShapeDtypeStruct((8, lb4.shape[1]), F32), "lower_fwd", lb4)


def _reduce8(g, name):
    def body(g_ref, o_ref):
        acc = g_ref[0]
        for k in range(1, N_DEV):
            acc = acc + g_ref[k]
        o_ref[...] = acc

    return _whole(body, jax.ShapeDtypeStruct(g.shape[1:], F32), name, g, nbytes=4 * g.size * 4)


_PACK = (("dmodx", 48), ("dmodc", 48), ("gconvw", 144), ("gconvb", 16), ("dlower", 16), ("ghgw", 8), ("gmlw", 8),
         ("glng", 16), ("glnb", 16), ("losssq", 16), ("ggate", 8))
_PACK_ROWS = sum(r for _, r in _PACK)


def _pack_offsets():
    off, out = 0, {}
    for name, rows in _PACK:
        out[name] = (off, rows)
        off += rows
    return out


def _small_finish(total, p0, d_feat):
    offs = _pack_offsets()

    def body(t_ref, p_ref, gb_ref, a0_ref, a1_ref, loss_ref):
        ox, oc, ol, oq = offs["dmodx"][0], offs["dmodc"][0], offs["dlower"][0], offs["losssq"][0]
        gb_ref[...] = t_ref[ox:ox + 48, :] + t_ref[oc:oc + 48, :]
        p = p_ref[...]
        da0 = t_ref[ol:ol + 16, :] * p * (1.0 - p)
        a0_ref[...] = da0
        a1_ref[...] = -da0
        sq = t_ref[oq:oq + 16, :]
        tot = jnp.sum(jnp.sum(sq, axis=1, keepdims=True), axis=0, keepdims=True)
        loss_ref[...] = jnp.broadcast_to(tot * (0.5 / d_feat), loss_ref.shape)

    s = jax.ShapeDtypeStruct
    return _whole(body, [s((48, LANE), F32), s((16, LANE), F32), s((16, LANE), F32), s((8, LANE), F32)],
                  "small_finish", total, p0)


def _cctx_grad(parts, c_ctx8):
    def body(p_ref, c_ref, o_ref):
        acc = p_ref[0]
        for k in range(1, N_DEV):
            acc = acc + p_ref[k]
        o_ref[...] = acc * _dsilu(c_ref[...])

    return _whole(body, jax.ShapeDtypeStruct(c_ctx8.shape, F32), "cctx_grad", parts, c_ctx8)


def _adam_math(w, g, m, v):
    m = ADAM_B1 * m + (1.0 - ADAM_B1) * g
    v = ADAM_B2 * v + (1.0 - ADAM_B2) * (g * g)
    m_hat = m / (1.0 - ADAM_B1 ** ADAM_STEP)
    v_hat = v / (1.0 - ADAM_B2 ** ADAM_STEP)
    delta = -ADAM_LR * (m_hat / (jnp.sqrt(v_hat) + ADAM_EPS) + ADAM_WD * w)
    return delta, m, v


def _adamw(w, g, m, v, rows, name):
    r, c = w.shape

    def body(w_ref, g_ref, m_ref, v_ref, d_ref, mo_ref, vo_ref):
        d_ref[...], mo_ref[...], vo_ref[...] = _adam_math(w_ref[...], g_ref[...], m_ref[...], v_ref[...])

    spec = pl.BlockSpec((rows, c), lambda i: (i, 0))
    return pl.pallas_call(
        body, name=name, grid=(r // rows,), in_specs=[spec] * 4, out_specs=[spec] * 3,
        out_shape=[jax.ShapeDtypeStruct((r, c), F32)] * 3,
        compiler_params=_vmem(16 * rows * (c + LANE) * 4),
    )(w, g, m, v)


def _rs_adamw(recv, w, m, v, tile, name, by_cols=False):
    _, r, c = recv.shape

    def body(r_ref, w_ref, m_ref, v_ref, g_ref, d_ref, mo_ref, vo_ref):
        g = r_ref[0].astype(F32)
        for k in range(1, N_DEV):
            g = g + r_ref[k].astype(F32)
        g_ref[...] = g
        d_ref[...], mo_ref[...], vo_ref[...] = _adam_math(w_ref[...], g, m_ref[...], v_ref[...])

    if by_cols:
        spec = pl.BlockSpec((r, tile), lambda i: (0, i))
        rspec = pl.BlockSpec((N_DEV, r, tile), lambda i: (0, 0, i))
        steps, elems = c // tile, (r + 16) * tile
    else:
        spec = pl.BlockSpec((tile, c), lambda i: (i, 0))
        rspec = pl.BlockSpec((N_DEV, tile, c), lambda i: (0, i, 0))
        steps, elems = r // tile, tile * (c + LANE)
    return pl.pallas_call(
        body, name=name, grid=(steps,), in_specs=[rspec] + [spec] * 3, out_specs=[spec] * 4,
        out_shape=[jax.ShapeDtypeStruct((r, c), F32)] * 4,
        compiler_params=_vmem(2 * elems * (N_DEV * 2 + 7 * 4) + (4 << 20)),
    )(recv, w, m, v)


def _all_gather(x, name):
    r, c = x.shape

    def body(x_ref, out_ref, send_sems, recv_sems, local_sem):
        px, py, pc = _position()
        me, sibling = (px, py, pc), (px, py, 1 - pc)
        chips = [(1 - px, py), (px, 1 - py), (1 - px, 1 - py)]

        def slot(qx, qy, qc):
            return out_ref.at[4 * qx + 2 * qy + qc]

        def copy(k, block, to, src=None):
            return pltpu.make_async_remote_copy(
                src_ref=slot(*block) if src is None else src, dst_ref=slot(*block),
                send_sem=send_sems.at[k], recv_sem=recv_sems.at[k], device_id=to, device_id_type=MESH)

        mine = pltpu.make_async_copy(x_ref, slot(*me), local_sem)
        mine.start()
        first = [copy(1 + j, me, (*chip, pc), src=x_ref) for j, chip in enumerate(chips)]
        first.append(copy(0, me, sibling, src=x_ref))
        for cp in first:
            cp.start()
        passed = [copy(4 + j, (*chip, pc), sibling) for j, chip in enumerate(chips)]
        for j, chip in enumerate(chips):
            copy(1 + j, (*chip, pc), me).wait_recv()
            passed[j].start()
        copy(0, sibling, me).wait_recv()
        for j, chip in enumerate(chips):
            copy(4 + j, (*chip, 1 - pc), me).wait_recv()
        for cp in first + passed:
            cp.wait_send()
        mine.wait()

    return pl.pallas_call(
        body, name=name, out_shape=jax.ShapeDtypeStruct((N_DEV, r, c), x.dtype),
        in_specs=[pl.BlockSpec(memory_space=pl.ANY)], out_specs=pl.BlockSpec(memory_space=pl.ANY),
        scratch_shapes=[pltpu.SemaphoreType.DMA((7,)), pltpu.SemaphoreType.DMA((7,)), pltpu.SemaphoreType.DMA],
    )(x)


def _local_step(ctx, x, target, modp, lower, wt_u, w_o, w9, conv_b, gate_b, hgw, mlw, ln_g, ln_b, exchange):
    tc = ctx.shape[0]
    tt = tc + x.shape[0]
    nbc, ncc, nc = tc // ROWS, tc // CHUNK, tt // CHUNK
    lower_f, lower_b = lower[0:1], lower[1:2]

    hc = _modulate_fwd(ctx, x, modp)
    tmh = _pick(tt, (1088, 768, 512, 256))
    if exchange:
        u, w_o = _mm(hc, wt_u, "nt", F32, tmh, 1152, D_MODEL, "mm_u", ride=_Ride("gather", w_o))
        w_o = w_o.reshape(D_MODEL, D_MODEL)
    else:
        u = _mm(hc, wt_u, "nt", F32, tmh, 1152, D_MODEL, "mm_u")
    cpre = _conv_fwd(u, w9, conv_b, tc)
    gates = u[:, BLK_GATE * LANE:BLK_GATE * LANE + 16].reshape(nc, CHUNK, 16)
    gcol, grow = gates, jnp.swapaxes(gates, 1, 2)
    bias_c, bias_r = gate_b.reshape(1, 16), gate_b.reshape(16, 1)

    o_f, hist_f = _hgrn_fwd(u, lower_f, ncc, False)
    o_b, hist_b = _hgrn_fwd(u, lower_b, ncc, True)
    h_f, ch_f, nh_f, mh_f = _mlstm_fwd(cpre, u, gcol, grow, bias_c, bias_r, ncc, False)
    h_b, ch_b, nh_b, mh_b = _mlstm_fwd(cpre, u, gcol, grow, bias_c, bias_r, ncc, True)
    y = _post_fwd(o_f, o_b, h_f, h_b, u, hgw, mlw, nbc)
    dz, dxa, fsum = _final(y, w_o, x, target, modp, ln_g, ln_b)

    dw_o = _mm(y, dz, "tn", BF16, D_MODEL, 1024, _pick(y.shape[0], (512, 256)), "mm_dwo")
    do, dhm, du1, psum = _post_bwd(dz, w_o, o_f, o_b, h_f, h_b, u, hgw, mlw, nbc)
    if exchange:
        dzf_f, dzq, dv_a, dlb_f, dw_o = _hgrn_bwd(
            u, lower_f, hist_f, do, None, ncc, False,
            ride=_Ride("a2a", dw_o.reshape(N_DEV, D_MODEL // N_DEV, D_MODEL)))
    else:
        dzf_f, dzq, dv_a, dlb_f = _hgrn_bwd(u, lower_f, hist_f, do, None, ncc, False)
    dzf_b, dzq, dv_a, dlb_b = _hgrn_bwd(u, lower_b, hist_b, do, (dzq, dv_a), ncc, True)
    dqk, dv_m, dg, _ = _mlstm_bwd(cpre, u, gcol, grow, bias_c, bias_r, ch_f, nh_f, mh_f, dhm, None, ncc, False)
    dqk, dv_m, dg, gsum = _mlstm_bwd(cpre, u, gcol, grow, bias_c, bias_r, ch_b, nh_b, mh_b, dhm, (dqk, dv_m, dg), ncc, True)
    du5, gconvw, gconvb = _conv_bwd(dqk, u, w9, tc)
    du = jnp.concatenate([dzq, dzf_f, dzf_b, dv_a, du1[:, 0:1024], du5, dv_m, du1[:, 1024:3072], dg], axis=1)
    half = D_MODEL // 2
    blocks = lambda g: g[:N_IN].reshape(N_DEV, N_IN // N_DEV, half)
    dwt_a = _mm(du, hc, "tn", BF16, 1152, half, tmh, "mm_dwu_a", n_cols=half, n_off=0)
    if exchange:
        dwt_b, got = _mm(du, hc, "tn", BF16, 1152, half, tmh, "mm_dwu_b", n_cols=half, n_off=1,
                         ride=_Ride("a2a", blocks(dwt_a), cols=(0, D_MODEL)))
        dh, dwt_u = _mm(du, wt_u, "nn", F32, tmh, D_MODEL, 1152, "mm_dh",
                        ride=_Ride("a2a", blocks(dwt_b), cols=(half, D_MODEL), into=got))
    else:
        dwt_b = _mm(du, hc, "tn", BF16, 1152, half, tmh, "mm_dwu_b", n_cols=half, n_off=1)
        dwt_u = jnp.concatenate([dwt_a, dwt_b], axis=1)
        dh = _mm(du, wt_u, "nn", F32, tmh, D_MODEL, 1152, "mm_dh")
    gx, msum = _modulate_bwd(dh, ctx, x, modp, dxa)

    zero_row = jnp.zeros((1, D_MODEL), F32)
    small = dict(
        dmodx=jnp.concatenate([msum[2:3], msum[3:4], fsum[0:1]], axis=0),
        dmodc=jnp.concatenate([msum[0:1], msum[1:2], zero_row], axis=0),
        gconvw=gconvw, gconvb=gconvb, dlower=jnp.concatenate([dlb_f, dlb_b], axis=0),
        ghgw=psum[0:1], gmlw=psum[1:2], glng=fsum[1:2], glnb=fsum[2:3], losssq=fsum[3:4],
        ggate=jnp.concatenate([gsum, jnp.zeros((7, LANE), F32)], axis=0))
    return gx, dwt_u, dw_o, small


def _pack_small(small):
    return jnp.concatenate([small[name].reshape(rows, LANE) for name, rows in _PACK], axis=0)


def _flat_pad(a, rows):
    flat = a.reshape(-1)
    return jnp.pad(flat, (0, rows * LANE - flat.shape[0])).reshape(rows, LANE)


def kernel(x, c, ctx, c_ctx, w_mod, b_mod, w_in, conv_w, conv_b, hg_lb, ml_gate_b, hg_norm_w, ml_norm_w, w_out, ln_g, ln_b, loss_target, m_c_ctx, m_w_mod, m_b_mod, m_w_in, m_conv_w, m_conv_b, m_hg_lb, m_ml_gate_b, m_hg_norm_w, m_ml_norm_w, m_w_out, m_ln_g, m_ln_b, v_c_ctx, v_w_mod, v_b_mod, v_w_in, v_conv_w, v_conv_b, v_hg_lb, v_ml_gate_b, v_hg_norm_w, v_ml_norm_w, v_w_out, v_ln_g, v_ln_b):
    px, py, pc = _position()
    me = 4 * px + 2 * py + pc
    d = D_MODEL
    n_mod = w_mod.shape[2]
    n_wi = w_in.shape[2]
    n_cv = conv_w.shape[3]
    n_lb = hg_lb.shape[2]

    pack0 = jnp.concatenate([c.reshape(-1), conv_w.reshape(-1), hg_lb.reshape(-1)]).reshape(1, -1)
    g0 = _all_gather(pack0, "gather_small_inputs")[:, 0, :]
    c_all = g0[:, :d]
    w9 = jnp.transpose(g0[:, d:d + 9 * n_cv].reshape(N_DEV, 9, n_cv), (1, 0, 2)).reshape(9, N_DEV * n_cv)
    lb4 = jnp.transpose(g0[:, d + 9 * n_cv:].reshape(N_DEV, 4, n_lb), (1, 0, 2)).reshape(4, N_DEV * n_lb)
    lower = _lower_fwd(lb4)

    cs = jnp.concatenate([c_all, c_ctx.reshape(1, d), jnp.zeros((7, d), F32)], axis=0)
    b_cols = lax.dynamic_slice(b_mod, (0, me * n_mod), (1, n_mod))
    slab = _mod_fwd(cs, w_mod[0], b_cols)
    mod_all = jnp.transpose(_all_gather(slab, "gather_mod"), (1, 0, 2)).reshape(16, N_DEV * n_mod)
    mod_x = lax.dynamic_slice(mod_all, (me, 0), (1, 3 * d)).reshape(3, d)
    modp = jnp.stack([mod_all[8].reshape(3, d), mod_x])

    wt = _all_gather(w_in[0].T.astype(BF16), "gather_w_in").reshape(N_DEV * n_wi, d)
    wt_u = jnp.pad(wt, ((0, N_U - N_DEV * n_wi), (0, 0)))

    gx, recv_wi, recv_wo, small = _local_step(ctx[0], x[0], loss_target[0], modp, lower, wt_u, w_out[0].astype(BF16),
                                              w9, conv_b, ml_gate_b[0], hg_norm_w, ml_norm_w, ln_g, ln_b, True)
    g_wi, d_wi, nm_wi, nv_wi = [a.T for a in _rs_adamw(recv_wi, w_in[0].T, m_w_in[0].T, v_w_in[0].T, 256,
                                                       "adamw_w_in", by_cols=True)]
    g_wo, d_wo, nm_wo, nv_wo = _rs_adamw(recv_wo, w_out[0], m_w_out[0], v_w_out[0], 64, "adamw_w_out")

    packs = _all_gather(_pack_small(small), "gather_small_grads")
    total = _reduce8(packs, "reduce_small_grads")
    offs = _pack_offsets()
    piece = lambda name: total[offs[name][0]:offs[name][0] + offs[name][1]]
    g_bmod, g_lb0, g_lb1, loss8 = _small_finish(total, lower[0:2].reshape(16, LANE), float(d))

    ox = offs["dmodx"][0]
    dmodx_all = packs[:, ox:ox + 48, :].reshape(N_DEV, 3 * d)
    dmodc_tot = piece("dmodc").reshape(1, 3 * d)
    d9 = jnp.concatenate([dmodx_all, dmodc_tot, jnp.zeros((7, 3 * d), F32)], axis=0)
    d9_cols = lax.dynamic_slice(d9, (0, me * n_mod), (16, n_mod))
    g_wmod, pc_part = _mod_bwd_w(cs, d9_cols, w_mod[0])
    c_ctx8 = jnp.concatenate([c_ctx.reshape(1, d), jnp.zeros((7, d), F32)], axis=0)
    g_cctx = _cctx_grad(_all_gather(pc_part, "gather_cctx"), c_ctx8)[0]
    d_wmod, nm_wmod, nv_wmod = _adamw(w_mod[0], g_wmod, m_w_mod[0], v_w_mod[0], 256, "adamw_w_mod")

    g_convw_full = piece("gconvw").reshape(9, d)
    g_convw = lax.dynamic_slice(g_convw_full, (0, me * n_cv), (9, n_cv)).reshape(conv_w.shape)
    lb_full = jnp.stack([jnp.stack([g_lb0[0:8].reshape(-1), g_lb1[0:8].reshape(-1)]),
                         jnp.stack([g_lb0[8:16].reshape(-1), g_lb1[8:16].reshape(-1)])])
    g_hglb = lax.dynamic_slice(lb_full, (0, 0, me * n_lb), (2, 2, n_lb))
    grads = dict(
        c_ctx=g_cctx, b_mod=g_bmod.reshape(b_mod.shape), conv_w=g_convw, conv_b=piece("gconvb").reshape(conv_b.shape),
        hg_lb=g_hglb, ml_gate_b=piece("ggate")[0, :16].reshape(ml_gate_b.shape),
        hg_norm_w=piece("ghgw").reshape(hg_norm_w.shape), ml_norm_w=piece("gmlw").reshape(ml_norm_w.shape),
        ln_g=piece("glng").reshape(ln_g.shape), ln_b=piece("glnb").reshape(ln_b.shape))
    params = dict(c_ctx=(c_ctx, m_c_ctx, v_c_ctx), b_mod=(b_mod, m_b_mod, v_b_mod), conv_w=(conv_w, m_conv_w, v_conv_w),
                  conv_b=(conv_b, m_conv_b, v_conv_b), hg_lb=(hg_lb, m_hg_lb, v_hg_lb),
                  ml_gate_b=(ml_gate_b, m_ml_gate_b, v_ml_gate_b), hg_norm_w=(hg_norm_w, m_hg_norm_w, v_hg_norm_w),
                  ml_norm_w=(ml_norm_w, m_ml_norm_w, v_ml_norm_w), ln_g=(ln_g, m_ln_g, v_ln_g), ln_b=(ln_b, m_ln_b, v_ln_b))
    names = list(params)
    rows_of = {n: -(-params[n][0].size // LANE) for n in names}
    rows_tot = -(-sum(rows_of.values()) // 8) * 8
    cat = lambda arrs: jnp.concatenate(
        [_flat_pad(a, rows_of[n]) for n, a in zip(names, arrs)]
        + [jnp.ones((rows_tot - sum(rows_of.values()), LANE), F32)], axis=0)
    d_s, m_s, v_s = _adamw(cat([params[n][0] for n in names]), cat([grads[n] for n in names]),
                           cat([params[n][1] for n in names]), cat([params[n][2] for n in names]), rows_tot, "adamw_small")
    delta, new_m, new_v, off = {}, {}, {}, 0
    for n in names:
        shape, size = params[n][0].shape, params[n][0].size
        take = lambda a: a[off:off + rows_of[n]].reshape(-1)[:size].reshape(shape)
        delta[n], new_m[n], new_v[n] = take(d_s), take(m_s), take(v_s)
        off += rows_of[n]
    grads.update(w_mod=g_wmod[None], w_in=g_wi[None], w_out=g_wo[None])
    delta.update(w_mod=d_wmod[None], w_in=d_wi[None], w_out=d_wo[None])
    new_m.update(w_mod=nm_wmod[None], w_in=nm_wi[None], w_out=nm_wo[None])
    new_v.update(w_mod=nv_wmod[None], w_in=nv_wi[None], w_out=nv_wo[None])

    order = ("c_ctx", "w_mod", "b_mod", "w_in", "conv_w", "conv_b", "hg_lb", "ml_gate_b", "hg_norm_w", "ml_norm_w",
             "w_out", "ln_g", "ln_b")
    return (loss8[0, 0], gx[None], *[grads[n] for n in order], *[delta[n] for n in order],
            *[new_m[n] for n in order], *[new_v[n] for n in order])
```

```python
import functools

import jax
import jax.numpy as jnp
from jax import lax
from jax.experimental import pallas as pl
from jax.experimental.pallas import tpu as pltpu

F32 = jnp.float32
BF16 = jnp.bfloat16

D_MODEL = 2048
W_A = 1024
W_B = 1024
HG_HEADS = 8
HG_D = 128
ML_HEADS = 4
ML_D = 256
CHUNK = 64
N_IN = 10256
LANE = 128
N_U = 81 * LANE
N_DEV = 8
ALPHA = 2.0 ** 0.25
LN_EPS = 1e-5
NORM_EPS = 1e-6
ADAM_LR, ADAM_B1, ADAM_B2, ADAM_EPS, ADAM_WD, ADAM_STEP = 0.001, 0.9, 0.999, 1e-08, 0.01, 10
VMEM_CAP = 60 * 1024 * 1024

SEG_AQ, SEG_AFF, SEG_AFB, SEG_AI, SEG_AZ = range(5)
BLK_QK = 40
SEG_BV, SEG_BO, SEG_BZ = 7, 8, 9
BLK_GATE = 80

MESH = pl.DeviceIdType.MESH


def _vmem(nbytes):
    return pltpu.CompilerParams(vmem_limit_bytes=int(min(VMEM_CAP, max(nbytes, 16 * 1024 * 1024))))


def _sigmoid(x):
    return 1.0 / (1.0 + jnp.exp(-x))


def _silu(x):
    return x * _sigmoid(x)


def _dsilu(x):
    s = _sigmoid(x)
    return s * (1.0 + x * (1.0 - s))


def _silu_both(x):
    s = _sigmoid(x)
    return x * s, s * (1.0 + x * (1.0 - s))


def _bdot(a, b, dims):
    return lax.dot_general(a.astype(BF16), b.astype(BF16), (dims, ((), ())), preferred_element_type=F32)


def _nn(a, b):
    return _bdot(a, b, ((1,), (0,)))


def _nt(a, b):
    return _bdot(a, b, ((1,), (1,)))


def _tn(a, b):
    return _bdot(a, b, ((0,), (0,)))


def _exact_nn(a, b):
    return lax.dot_general(a, b, (((1,), (0,)), ((), ())), precision=lax.Precision.HIGHEST,
                           preferred_element_type=F32)


def _exact_tn(a, b):
    return lax.dot_general(a, b, (((0,), (0,)), ((), ())), precision=lax.Precision.HIGHEST,
                           preferred_element_type=F32)


def _tri(rev):
    t = lax.broadcasted_iota(jnp.int32, (CHUNK, CHUNK), 0)
    s = lax.broadcasted_iota(jnp.int32, (CHUNK, CHUNK), 1)
    return (s >= t) if rev else (s <= t)


def _eye():
    t = lax.broadcasted_iota(jnp.int32, (CHUNK, CHUNK), 0)
    s = lax.broadcasted_iota(jnp.int32, (CHUNK, CHUNK), 1)
    return (s == t).astype(F32)


def _row_to_col(row):
    return jnp.sum(_eye() * row, axis=1, keepdims=True)


def _last_onehot(rev):
    t = lax.broadcasted_iota(jnp.int32, (CHUNK, 1), 0)
    return (t == (0 if rev else CHUNK - 1)).astype(F32)


def _head_slices(width, n_heads):
    hd = width // n_heads
    return [slice(h * hd, (h + 1) * hd) for h in range(n_heads)]


def _scan_sum(x, rev):
    n = x.shape[0]
    t = lax.broadcasted_iota(jnp.int32, x.shape, 0)
    s = 1
    while s < n:
        if rev:
            x = x + jnp.where(t < n - s, pltpu.roll(x, n - s, 0), 0.0)
        else:
            x = x + jnp.where(t >= s, pltpu.roll(x, s, 0), 0.0)
        s *= 2
    return x


def _dot3(a, b, dims):
    a_hi, b_hi = a.astype(BF16), b.astype(BF16)
    a_lo, b_lo = (a - a_hi.astype(F32)).astype(BF16), (b - b_hi.astype(F32)).astype(BF16)
    dot = lambda x, y: lax.dot_general(x, y, (dims, ((), ())), preferred_element_type=F32)
    return dot(a_hi, b_hi) + (dot(a_hi, b_lo) + dot(a_lo, b_hi))


def _hg_common(zq, zf, lb, rev):
    q, dq_dz = _silu_both(zq)
    sg = _sigmoid(zf)
    f = lb + (1.0 - lb) * sg
    g = jnp.log(f)
    k = 1.0 - f
    b = _scan_sum(g, rev)
    b_last = jnp.sum(g, axis=0, keepdims=True)
    r = b[CHUNK // 2:CHUNK // 2 + 1, :]
    e_up = jnp.exp(b - r)
    e_dn = jnp.exp(r - b)
    e_b = e_up * jnp.exp(r)
    e_lb = e_dn * jnp.exp(b_last - r)
    return dict(q=q, dq_dz=dq_dz, sg=sg, f=f, k=k, e_up=e_up, e_dn=e_dn, e_b=e_b, e_lb=e_lb, e_last=jnp.exp(b_last),
                q_t=q * e_up, k_t=k * e_dn, q_s=q * e_b, k_h=k * e_lb, tri=_tri(rev).astype(F32))


def hg_chunk_fwd(zq, zf, v, lb, st, rev):
    c = _hg_common(zq, zf, lb, rev)
    hs = _head_slices(zq.shape[1], zq.shape[1] // HG_D)
    s = [_nt(c["q_t"][:, sl], c["k_t"][:, sl]) for sl in hs]
    oi = [_nt(c["q_s"][:, sl], st[sl, :]) for sl in hs]
    ds = [_tn(v[:, sl], c["k_h"][:, sl]) for sl in hs]
    oa = [_nn(c["tri"] * s_h, v[:, sl]) for s_h, sl in zip(s, hs)]
    o = jnp.concatenate([x + y for x, y in zip(oi, oa)], axis=1)
    st_new = jnp.concatenate([st[sl, :] * c["e_last"][:, sl] + d for sl, d in zip(hs, ds)], axis=0)
    return o, st_new


def hg_chunk_bwd(zq, zf, v, lb, st, do, dst_new, rev):
    c = _hg_common(zq, zf, lb, rev)
    hs = _head_slices(zq.shape[1], zq.shape[1] // HG_D)
    tri, q_t, k_t, q_s, k_h = c["tri"], c["q_t"], c["k_t"], c["q_s"], c["k_h"]
    s = [_nt(q_t[:, sl], k_t[:, sl]) for sl in hs]
    da = [tri * _nt(do[:, sl], v[:, sl]) for sl in hs]
    dq_s = [_nn(do[:, sl], st[sl, :]) for sl in hs]
    dk_h = [_nn(v[:, sl], dst_new[sl, :]) for sl in hs]
    dv_s = [_nt(k_h[:, sl], dst_new[sl, :]) for sl in hs]
    dst_q = [_tn(do[:, sl], q_s[:, sl]) for sl in hs]
    dq_t = [_dot3(da_h, k_t[:, sl], ((1,), (0,))) for da_h, sl in zip(da, hs)]
    dk_t = [_dot3(da_h, q_t[:, sl], ((0,), (0,))) for da_h, sl in zip(da, hs)]
    dv_a = [_tn(tri * s_h, do[:, sl]) for s_h, sl in zip(s, hs)]
    cat = lambda parts: jnp.concatenate(parts, axis=1)
    dq_s, dk_h, dq_t, dk_t = cat(dq_s), cat(dk_h), cat(dq_t), cat(dk_t)
    dv = cat([x + y for x, y in zip(dv_a, dv_s)])
    dst = jnp.concatenate([dst_new[sl, :] * c["e_last"][:, sl] + d for sl, d in zip(hs, dst_q)], axis=0)
    dq = dq_s * c["e_b"] + dq_t * c["e_up"]
    dk = dk_t * c["e_dn"] + dk_h * c["e_lb"]
    db = c["q"] * dq - c["k"] * dk
    ss = cat([jnp.sum(dst_new[sl, :] * st[sl, :], axis=0, keepdims=True) for sl in hs])
    d_all = jnp.sum(dk_h * k_h, axis=0, keepdims=True) + c["e_last"] * ss
    dg = _scan_sum(db, not rev) + d_all
    dzq = dq * c["dq_dz"]
    df = dg / c["f"] - dk
    dzf = df * (1.0 - lb) * c["sg"] * (1.0 - c["sg"])
    dlb = jnp.sum(df * (1.0 - c["sg"]), axis=0, keepdims=True)
    return dzq, dzf, dv, dlb, dst


def _log_sigmoid(x):
    return jnp.minimum(x, 0.0) - jnp.log(1.0 + jnp.exp(-jnp.abs(x)))


def _each(fn, *lists):
    return [fn(*xs) for xs in zip(*lists)]


def _bf(xs):
    return [x.astype(BF16) for x in xs]


def _ml_forward_parts(qp, kp, v, gates, c, n, m, rev):
    hs = _head_slices(qp.shape[1], qp.shape[1] // ML_D)
    q_all, dq_dp = _silu_both(qp)
    k_all, dk_dp = _silu_both(kp)
    k_all = k_all * (ML_D ** -0.5)
    q = [q_all[:, sl] for sl in hs]
    k = [k_all[:, sl] for sl in hs]
    vv = [v[:, sl] for sl in hs]
    cc = [c[sl, :] for sl in hs]
    tri_b = _tri(rev)
    tri = tri_b.astype(F32)
    tri_t = _tri(not rev).astype(F32)
    e_last = _last_onehot(rev)
    qb, kb, vb, cb = _bf(q), _bf(k), _bf(vv), _bf(cc)
    qk = _each(_nt, qb, kb)
    qc = _each(_nt, qb, cb)
    parts = []
    for (gi_c, gi_r, gf_c, gf_r), m_h in zip(gates, m):
        lf_c, lf_r = _log_sigmoid(gf_c), _log_sigmoid(gf_r)
        b_c = jnp.sum(tri * lf_r, axis=1, keepdims=True)
        b_r = jnp.sum(tri_t * lf_c, axis=0, keepdims=True)
        log_w = jnp.where(tri_b, b_c - b_r + gi_r, -jnp.inf)
        m_inter = b_c + m_h
        m_t = jnp.maximum(m_inter, jnp.max(log_w, axis=1, keepdims=True))
        m_new = jnp.sum(m_t * e_last, axis=0, keepdims=True)
        b_last = jnp.sum(b_c * e_last, axis=0, keepdims=True)
        parts.append(dict(a=jnp.exp(m_inter - m_t), p=jnp.exp(log_w - m_t), floor=jnp.exp(-m_t), m_new=m_new,
                          ws=jnp.exp(b_last - b_c + gi_c - m_new), decay=jnp.exp(b_last + m_h - m_new), gf_c=gf_c))
    w = [pt["p"] * x for pt, x in zip(parts, qk)]
    wb = _bf(w)
    wv = _each(_nn, wb, vb)
    for pt, q_h, n_h, w_h, qc_h, wv_h in zip(parts, q, n, w, qc, wv):
        qn = jnp.sum(q_h * n_h, axis=1, keepdims=True)
        num = pt["a"] * qc_h + wv_h
        den = pt["a"] * qn + jnp.sum(w_h, axis=1, keepdims=True)
        pt.update(qn=qn, num=num, den=den, rinv=1.0 / jnp.maximum(jnp.abs(den), pt["floor"]), w=w_h, qc=qc_h)
    return hs, q, k, vv, cc, tri, parts, dict(q=qb, k=kb, v=vb, c=cb, w=wb, dq_dp=dq_dp, dk_dp=dk_dp)


def ml_chunk_fwd(qp, kp, v, gates, c, n, m, rev):
    hs, q, k, vv, cc, tri, parts, bf = _ml_forward_parts(qp, kp, v, gates, c, n, m, rev)
    h = jnp.concatenate([pt["num"] * pt["rinv"] for pt in parts], axis=1)
    upd = _each(_tn, [pt["ws"] * v_h for pt, v_h in zip(parts, vv)], bf["k"])
    c_new = jnp.concatenate([pt["decay"] * c_h + u for pt, c_h, u in zip(parts, cc, upd)], axis=0)
    n_new = [pt["decay"] * n_h + jnp.sum(pt["ws"] * k_h, axis=0, keepdims=True) for pt, n_h, k_h in zip(parts, n, k)]
    return h, c_new, n_new, [pt["m_new"] for pt in parts]


def ml_chunk_bwd(qp, kp, v, gates, c, n, m, dh, dc_new, dn_new, rev):
    hs, q, k, vv, cc, tri, parts, bf = _ml_forward_parts(qp, kp, v, gates, c, n, m, rev)
    dcn = [dc_new[sl, :] for sl in hs]
    dcb = _bf(dcn)
    dnum, dden = [], []
    for pt, sl in zip(parts, hs):
        dh_h = dh[:, sl]
        h = pt["num"] * pt["rinv"]
        signed_live = jnp.where(jnp.abs(pt["den"]) > pt["floor"], jnp.where(pt["den"] >= 0.0, 1.0, -1.0), 0.0)
        dnum.append(dh_h * pt["rinv"])
        dden.append(-jnp.sum(dh_h * h, axis=1, keepdims=True) * pt["rinv"] * signed_live)
    dnb = _bf(dnum)
    dw = [x + y for x, y in zip(_each(_nt, dnb, bf["v"]), dden)]
    kdc = _each(_nt, bf["k"], dcb)
    vdc = _each(_nn, bf["v"], dcb)
    dqk = [x * pt["p"] for x, pt in zip(dw, parts)]
    adn = [pt["a"] * x for pt, x in zip(parts, dnum)]
    dqkb, adnb = _bf(dqk), _bf(adn)
    dv_w = _each(_tn, bf["w"], dnb)
    dq_k = _each(_nn, dqkb, bf["k"])
    dq_c = _each(_nn, adnb, bf["c"])
    dk_q = _each(_tn, dqkb, bf["q"])
    dc_q = _each(_tn, adnb, bf["q"])
    dq, dk, dv, dgi, dgf, dc, dn = [], [], [], [], [], [], []
    for i, pt in enumerate(parts):
        a, ws, decay = pt["a"], pt["ws"], pt["decay"]
        add = a * dden[i]
        e = dw[i] * pt["w"]
        dv.append(dv_w[i] + ws * kdc[i])
        dq.append(dq_k[i] + dq_c[i] + add * n[i])
        dk.append(dk_q[i] + ws * vdc[i] + ws * dn_new[i])
        alpha = (jnp.sum(dnum[i] * pt["qc"], axis=1, keepdims=True) + dden[i] * pt["qn"]) * a
        omega = (jnp.sum(vdc[i] * k[i], axis=1, keepdims=True) + jnp.sum(k[i] * dn_new[i], axis=1, keepdims=True)) * ws
        delta = decay * (jnp.sum(jnp.sum(dcn[i] * cc[i], axis=1, keepdims=True), axis=0, keepdims=True)
                         + jnp.sum(dn_new[i] * n[i], axis=1, keepdims=True))
        dc.append(decay * dcn[i] + dc_q[i])
        dn.append(decay * dn_new[i] + jnp.sum(add * q[i], axis=0, keepdims=True))
        e_rows = jnp.sum(e, axis=1, keepdims=True)
        e_cols = _row_to_col(jnp.sum(e, axis=0, keepdims=True))
        dgi.append(e_cols + omega)
        db = e_rows + alpha - e_cols - omega
        tail = jnp.sum(omega, axis=0, keepdims=True) + delta
        dlf = _row_to_col(jnp.sum(tri * db, axis=0, keepdims=True)) + tail
        dgf.append(dlf * (1.0 - _sigmoid(pt["gf_c"])))
    cat = lambda xs: jnp.concatenate(xs, axis=1)
    dqp = cat(dq) * bf["dq_dp"]
    dkp = cat(dk) * (ML_D ** -0.5) * bf["dk_dp"]
    return dqp, dkp, cat(dv), dgi, dgf, jnp.concatenate(dc, axis=0), dn


def _pick(n, prefs):
    for p in prefs:
        if n % p == 0:
            return p
    raise ValueError(f"no tile for {n} among {prefs}")


def _position():
    return lax.axis_index("x"), lax.axis_index("y"), lax.axis_index("c")


class _Ride:
    def __init__(self, kind, x, cols=None, into=None):
        self.kind, self.x, self.cols, self.into = kind, x, cols, into
        r, c = x.shape[-2:]
        self.out_shape = jax.ShapeDtypeStruct((N_DEV, r, c if cols is None else cols[1]), x.dtype)
        self.width = c

    def _copies(self, x_ref, out_ref, send_sems, recv_sems, local_sem):
        px, py, pc = _position()
        me = 4 * px + 2 * py + pc
        src = (lambda slot: x_ref) if self.kind == "gather" else (lambda slot: x_ref.at[slot])
        dst = ((lambda slot: out_ref.at[slot]) if self.cols is None
               else (lambda slot: out_ref.at[slot, :, pl.ds(self.cols[0], self.width)]))
        mine = pltpu.make_async_copy(src(me), dst(me), local_sem)
        sends, recvs = [], []
        for k, (fx, fy, fc) in enumerate([(1, 0, 0), (0, 1, 0), (1, 1, 0), (1, 0, 1), (0, 1, 1), (1, 1, 1), (0, 0, 1)]):
            qx, qy, qc = (1 - px if fx else px), (1 - py if fy else py), (1 - pc if fc else pc)
            peer = 4 * qx + 2 * qy + qc
            sends.append(pltpu.make_async_remote_copy(
                src_ref=src(peer), dst_ref=dst(me), send_sem=send_sems.at[k], recv_sem=recv_sems.at[k],
                device_id=(qx, qy, qc), device_id_type=MESH))
            recvs.append(pltpu.make_async_remote_copy(
                src_ref=src(me), dst_ref=dst(peer), send_sem=send_sems.at[k], recv_sem=recv_sems.at[k],
                device_id=(qx, qy, qc), device_id_type=MESH))
        return mine, sends, recvs

    def start(self, *refs):
        mine, sends, _ = self._copies(*refs)
        mine.start()
        for cp in sends:
            cp.start()

    def wait(self, *refs):
        mine, sends, recvs = self._copies(*refs)
        for cp in recvs:
            cp.wait_recv()
        for cp in sends:
            cp.wait_send()
        mine.wait()

    def operands(self):
        return [self.x] + ([self.into] if self.into is not None else [])


_RIDE_SCRATCH = [pltpu.SemaphoreType.DMA((7,)), pltpu.SemaphoreType.DMA((7,)), pltpu.SemaphoreType.DMA]
_ANY = pl.BlockSpec(memory_space=pl.ANY)


def _mm(a, b, mode, out_dtype, tm, tn, tk, name, ride=None, n_cols=None, n_off=0, m_out=None):
    if mode == "nn":
        (m, k), (k2, n) = a.shape, b.shape
    elif mode == "nt":
        (m, k), (n, k2) = a.shape, b.shape
    else:
        (k, m), (k2, n) = a.shape, b.shape
    n = n if n_cols is None else n_cols
    assert k == k2 and m % tm == 0 and n % tn == 0 and k % tk == 0, (a.shape, b.shape, mode, tm, tn, tk)
    nk = k // tk
    joff = n_off * (n // tn)
    dims = {"nn": ((1,), (0,)), "nt": ((1,), (1,)), "tn": ((0,), (0,))}[mode]
    a_spec = (pl.BlockSpec((tk, tm), lambda j, i, kk: (kk, i)) if mode == "tn"
              else pl.BlockSpec((tm, tk), lambda j, i, kk: (i, kk)))
    b_spec = (pl.BlockSpec((tn, tk), lambda j, i, kk: (j + joff, kk)) if mode == "nt"
              else pl.BlockSpec((tk, tn), lambda j, i, kk: (kk, j + joff)))

    grid = (n // tn, m // tm, nk)
    n_ride_in = len(ride.operands()) if ride is not None else 0

    def body(a_ref, b_ref, *rest):
        if ride is not None:
            x_ref = rest[0]
            o_ref, got_ref, acc_ref = rest[n_ride_in:n_ride_in + 3]
            comm = (x_ref, got_ref) + tuple(rest[n_ride_in + 3:])
        else:
            o_ref, acc_ref = rest
        kk = pl.program_id(2)
        step = (pl.program_id(0) * grid[1] + pl.program_id(1)) * nk + kk
        if ride is not None:
            @pl.when(step == 0)
            def _():
                ride.start(*comm)

        part = lax.dot_general(a_ref[...], b_ref[...], (dims, ((), ())), preferred_element_type=F32)
        if nk == 1:
            o_ref[...] = part.astype(o_ref.dtype)
        else:
            @pl.when(kk == 0)
            def _():
                acc_ref[...] = part

            @pl.when(jnp.logical_and(kk > 0, kk < nk - 1))
            def _():
                acc_ref[...] += part

            @pl.when(kk == nk - 1)
            def _():
                o_ref[...] = (acc_ref[...] + part).astype(o_ref.dtype)

        if ride is not None:
            @pl.when(step == grid[0] * grid[1] * nk - 1)
            def _():
                ride.wait(*comm)

    osz = jnp.dtype(out_dtype).itemsize
    need = 2 * (tm * tk * a.dtype.itemsize + tk * tn * b.dtype.itemsize + tm * tn * osz) + tm * tn * 4
    o_spec = pl.BlockSpec((tm, tn), lambda j, i, kk: (i, j))
    o_shape = jax.ShapeDtypeStruct((m if m_out is None else m_out, n), out_dtype)
    extra = ride is not None
    return pl.pallas_call(
        body, name=name, grid=grid,
        in_specs=[a_spec, b_spec] + [_ANY] * n_ride_in,
        out_specs=[o_spec, _ANY] if extra else o_spec,
        out_shape=[o_shape, ride.out_shape] if extra else o_shape,
        scratch_shapes=[pltpu.VMEM((tm, tn) if nk > 1 else (8, LANE), F32)] + (_RIDE_SCRATCH if extra else []),
        input_output_aliases={3: 1} if extra and ride.into is not None else {},
        compiler_params=_vmem(need + (12 << 20)),
    )(a, b, *(ride.operands() if extra else []))


ROWS = 256


def _ln_stats(x):
    mu = jnp.mean(x, axis=-1, keepdims=True)
    xc = x - mu
    var = jnp.mean(xc * xc, axis=-1, keepdims=True)
    rstd = lax.rsqrt(var + LN_EPS)
    return xc * rstd, rstd


def _token_specs(nbc, nbx, d):
    return [pl.BlockSpec((ROWS, d), lambda i: (jnp.minimum(i, nbc - 1), 0)),
            pl.BlockSpec((ROWS, d), lambda i: (jnp.maximum(i - nbc, 0), 0))]


def _tokens(c_ref, x_ref, nbc):
    return jnp.where(pl.program_id(0) < nbc, c_ref[...], x_ref[...])


def _modulate_fwd(ctx, x, modp):
    d = x.shape[1]
    nbc, nbx = ctx.shape[0] // ROWS, x.shape[0] // ROWS

    def body(c_ref, x_ref, mod_ref, o_ref):
        n, _ = _ln_stats(_tokens(c_ref, x_ref, nbc))
        o_ref[...] = (n * (1.0 + mod_ref[0, 1:2, :]) + mod_ref[0, 0:1, :]).astype(BF16)

    return pl.pallas_call(
        body, name="modulate_fwd", grid=(nbc + nbx,),
        in_specs=_token_specs(nbc, nbx, d) + [pl.BlockSpec((1, 3, d), lambda i: (jnp.where(i >= nbc, 1, 0), 0, 0))],
        out_specs=pl.BlockSpec((ROWS, d), lambda i: (i, 0)),
        out_shape=jax.ShapeDtypeStruct((ctx.shape[0] + x.shape[0], d), BF16),
    )(ctx, x, modp)


def _modulate_bwd(dh, ctx, x, modp, dxa):
    t, d = x.shape
    nbc, nbx = ctx.shape[0] // ROWS, t // ROWS

    def body(dh_ref, c_ref, x_ref, mod_ref, dxa_ref, gx_ref, sum_ref):
        i = pl.program_id(0)
        n, rstd = _ln_stats(_tokens(c_ref, x_ref, nbc))
        g = dh_ref[...]
        dn = g * (1.0 + mod_ref[0, 1:2, :])
        dx = rstd * (dn - jnp.mean(dn, axis=-1, keepdims=True) - n * jnp.mean(dn * n, axis=-1, keepdims=True))
        gx_ref[...] = dx + dxa_ref[...]
        dshift = jnp.sum(g, axis=0, keepdims=True)
        dscale = jnp.sum(g * n, axis=0, keepdims=True)

        @pl.when(i == 0)
        def _():
            sum_ref[...] = jnp.zeros_like(sum_ref)

        @pl.when(i < nbc)
        def _():
            sum_ref[0:1, :] += dshift
            sum_ref[1:2, :] += dscale

        @pl.when(i >= nbc)
        def _():
            sum_ref[2:3, :] += dshift
            sum_ref[3:4, :] += dscale

    lat = lambda i: (jnp.maximum(i - nbc, 0), 0)
    return pl.pallas_call(
        body, name="modulate_bwd", grid=(nbc + nbx,),
        in_specs=[pl.BlockSpec((ROWS, d), lambda i: (i, 0))] + _token_specs(nbc, nbx, d)
        + [pl.BlockSpec((1, 3, d), lambda i: (jnp.where(i >= nbc, 1, 0), 0, 0)), pl.BlockSpec((ROWS, d), lat)],
        out_specs=[pl.BlockSpec((ROWS, d), lat), pl.BlockSpec((8, d), lambda i: (0, 0))],
        out_shape=[jax.ShapeDtypeStruct((t, d), F32), jax.ShapeDtypeStruct((8, d), F32)],
    )(dh, ctx, x, modp, dxa)


def _post_fwd(o_f, o_b, h_f, h_b, u, hgw, mlw, nbc):
    tt = u.shape[0]
    t = tt - nbc * ROWS

    def body(of_ref, ob_ref, hf_ref, hb_ref, az_ref, bo_ref, bz_ref, hgw_ref, mlw_ref, y_ref):
        o = of_ref[...] + ob_ref[...]
        for sl in _head_slices(W_A, HG_HEADS):
            oh = o[:, sl]
            rs = lax.rsqrt(jnp.mean(oh * oh, axis=-1, keepdims=True) + NORM_EPS)
            y_ref[:, sl] = (oh * rs * hgw_ref[:, sl] * _silu(az_ref[:, sl])).astype(BF16)
        hm = hf_ref[...] + hb_ref[...]
        for sl in _head_slices(W_B, ML_HEADS):
            hh = hm[:, sl]
            mu = jnp.mean(hh, axis=-1, keepdims=True)
            hc = hh - mu
            rstd = lax.rsqrt(jnp.mean(hc * hc, axis=-1, keepdims=True) + NORM_EPS)
            out = hc * rstd * mlw_ref[:, sl] * _sigmoid(bo_ref[:, sl]) * _silu(bz_ref[:, sl])
            y_ref[:, W_A + sl.start:W_A + sl.stop] = out.astype(BF16)

    row = lambda i: (i + nbc, 0)
    seg = lambda s: pl.BlockSpec((ROWS, 1024), lambda i: (i + nbc, s))
    wspec = pl.BlockSpec((1, 1024), lambda i: (0, 0))
    return pl.pallas_call(
        body, name="post_fwd", grid=(t // ROWS,),
        in_specs=[pl.BlockSpec((ROWS, 1024), row)] * 4 + [seg(SEG_AZ), seg(SEG_BO), seg(SEG_BZ), wspec, wspec],
        out_specs=pl.BlockSpec((ROWS, 2048), lambda i: (i, 0)),
        out_shape=jax.ShapeDtypeStruct((t, 2048), BF16),
    )(o_f, o_b, h_f, h_b, u, u, u, hgw, mlw)


def _post_bwd(dz, w_o, o_f, o_b, h_f, h_b, u, hgw, mlw, nbc):
    tt = u.shape[0]
    d = w_o.shape[0]

    def body(dz_ref, w_ref, of_ref, ob_ref, hf_ref, hb_ref, az_ref, bo_ref, bz_ref, hgw_ref, mlw_ref,
             do_ref, dhm_ref, du_ref, sum_ref):
        i = pl.program_id(0)
        live = jnp.where(i >= nbc, 1.0, 0.0)
        dy = lax.dot_general(dz_ref[...], w_ref[...], (((1,), (1,)), ((), ())), preferred_element_type=F32) * live

        @pl.when(i == 0)
        def _():
            sum_ref[...] = jnp.zeros_like(sum_ref)

        o = of_ref[...] + ob_ref[...]
        for sl in _head_slices(W_A, HG_HEADS):
            oh = o[:, sl]
            rs = lax.rsqrt(jnp.mean(oh * oh, axis=-1, keepdims=True) + NORM_EPS)
            on = oh * rs
            az = az_ref[:, sl]
            dya = dy[:, sl]
            saz, daz = _silu_both(az)
            doa = dya * saz
            du_ref[:, sl] = (dya * on * hgw_ref[:, sl] * daz).astype(BF16)
            sum_ref[0:1, sl] += jnp.sum(doa * on, axis=0, keepdims=True)
            don = doa * hgw_ref[:, sl]
            do_ref[:, sl] = rs * (don - on * jnp.mean(don * on, axis=-1, keepdims=True))
        hm = hf_ref[...] + hb_ref[...]
        for sl in _head_slices(W_B, ML_HEADS):
            hh = hm[:, sl]
            mu = jnp.mean(hh, axis=-1, keepdims=True)
            hc = hh - mu
            rstd = lax.rsqrt(jnp.mean(hc * hc, axis=-1, keepdims=True) + NORM_EPS)
            hn = hc * rstd
            hw = hn * mlw_ref[:, sl]
            bo, bz = bo_ref[:, sl], bz_ref[:, sl]
            sbo = _sigmoid(bo)
            sbz, dbz = _silu_both(bz)
            dyb = dy[:, W_A + sl.start:W_A + sl.stop]
            dhw = dyb * sbo * sbz
            du_ref[:, 1024 + sl.start:1024 + sl.stop] = (dyb * hw * sbz * sbo * (1.0 - sbo)).astype(BF16)
            du_ref[:, 2048 + sl.start:2048 + sl.stop] = (dyb * hw * sbo * dbz).astype(BF16)
            sum_ref[1:2, sl] += jnp.sum(dhw * hn, axis=0, keepdims=True)
            dhn = dhw * mlw_ref[:, sl]
            dhm_ref[:, sl] = rstd * (dhn - jnp.mean(dhn, axis=-1, keepdims=True)
                                     - hn * jnp.mean(dhn * hn, axis=-1, keepdims=True))

    row = lambda i: (i, 0)
    seg = lambda s: pl.BlockSpec((ROWS, 1024), lambda i: (i, s))
    wspec = pl.BlockSpec((1, 1024), lambda i: (0, 0))
    return pl.pallas_call(
        body, name="post_bwd", grid=(tt // ROWS,),
        in_specs=[pl.BlockSpec((ROWS, 2048), lambda i: (jnp.maximum(i - nbc, 0), 0)), pl.BlockSpec((d, d), lambda i: (0, 0))]
        + [pl.BlockSpec((ROWS, 1024), row)] * 4 + [seg(SEG_AZ), seg(SEG_BO), seg(SEG_BZ), wspec, wspec],
        out_specs=[pl.BlockSpec((ROWS, 1024), row), pl.BlockSpec((ROWS, 1024), row),
                   pl.BlockSpec((ROWS, 3072), row), pl.BlockSpec((8, 1024), lambda i: (0, 0))],
        out_shape=[jax.ShapeDtypeStruct((tt, 1024), F32), jax.ShapeDtypeStruct((tt, 1024), F32),
                   jax.ShapeDtypeStruct((tt, 3072), BF16), jax.ShapeDtypeStruct((8, 1024), F32)],
        compiler_params=_vmem(4 * d * d + 30 * ROWS * 2048 * 4),
    )(dz, w_o, o_f, o_b, h_f, h_b, u, u, u, hgw, mlw)


def _final(y, w_o, x, target, modp, ln_g, ln_b):
    t, d = x.shape

    def body(y_ref, w_ref, x_ref, tg_ref, mod_ref, g_ref, b_ref, dz_ref, dxa_ref, sum_ref):
        i = pl.program_id(0)
        zz = lax.dot_general(y_ref[...], w_ref[...], (((1,), (0,)), ((), ())), preferred_element_type=F32)
        gate = mod_ref[0, 2:3, :]
        pre = ALPHA * x_ref[...] + gate * zz
        nh, rstd = _ln_stats(pre)
        err = nh * g_ref[...] + b_ref[...] - tg_ref[...]
        dxo = err * (1.0 / d)
        dnh = dxo * g_ref[...]
        dpre = rstd * (dnh - jnp.mean(dnh, axis=-1, keepdims=True) - nh * jnp.mean(dnh * nh, axis=-1, keepdims=True))
        dz_ref[...] = (gate * dpre).astype(BF16)
        dxa_ref[...] = ALPHA * dpre

        @pl.when(i == 0)
        def _():
            sum_ref[...] = jnp.zeros_like(sum_ref)

        sum_ref[0:1, :] += jnp.sum(dpre * zz, axis=0, keepdims=True)
        sum_ref[1:2, :] += jnp.sum(dxo * nh, axis=0, keepdims=True)
        sum_ref[2:3, :] += jnp.sum(dxo, axis=0, keepdims=True)
        sum_ref[3:4, :] += jnp.sum(err * err, axis=0, keepdims=True)

    row = lambda i: (i, 0)
    vec = pl.BlockSpec((1, d), lambda i: (0, 0))
    return pl.pallas_call(
        body, name="final_ln_loss", grid=(t // ROWS,),
        in_specs=[pl.BlockSpec((ROWS, d), row), pl.BlockSpec((d, d), lambda i: (0, 0)), pl.BlockSpec((ROWS, d), row),
                  pl.BlockSpec((ROWS, d), row), pl.BlockSpec((1, 3, d), lambda i: (1, 0, 0)), vec, vec],
        out_specs=[pl.BlockSpec((ROWS, d), row), pl.BlockSpec((ROWS, d), row), pl.BlockSpec((8, d), lambda i: (0, 0))],
        out_shape=[jax.ShapeDtypeStruct((t, d), BF16), jax.ShapeDtypeStruct((t, d), F32),
                   jax.ShapeDtypeStruct((8, d), F32)],
        compiler_params=_vmem(4 * d * d + 24 * ROWS * d * 4),
    )(y, w_o, x, target, modp, ln_g, ln_b)


GRID_W = 64


def _shift(x, s, ok):
    n = x.shape[0]
    return jnp.where(ok, pltpu.roll(x, s % n, 0), 0.0)


def _grid_masks(n):
    t = lax.broadcasted_iota(jnp.int32, (n, LANE), 0)
    col = t & (GRID_W - 1)
    return dict(left=col >= 1, right=col <= GRID_W - 2, up=t >= GRID_W, down=t < n - GRID_W)


def _seq_masks(n):
    t = lax.broadcasted_iota(jnp.int32, (n, LANE), 0)
    return dict(left=t >= 1, right=t <= n - 2)


def _conv_fwd(u, w9, cb, tc):
    tt = u.shape[0]
    t = tt - tc

    def body(u_ref, w_ref, b_ref, o_ref):
        w = [w_ref[r:r + 1, :] for r in range(9)]
        xc = u_ref[0:tc, :]
        ms = _seq_masks(tc)
        o_ref[0:tc, :] = (w[3] * _shift(xc, 1, ms["left"]) + w[4] * xc + w[5] * _shift(xc, -1, ms["right"])
                          + b_ref[...])
        x = u_ref[tc:tt, :]
        mg = _grid_masks(t)
        taps = (_shift(x, 1, mg["left"]), x, _shift(x, -1, mg["right"]))
        rows = [w[3 * i] * taps[0] + w[3 * i + 1] * taps[1] + w[3 * i + 2] * taps[2] for i in range(3)]
        o_ref[tc:tt, :] = (rows[1] + _shift(rows[0], GRID_W, mg["up"]) + _shift(rows[2], -GRID_W, mg["down"])
                           + b_ref[...])

    return pl.pallas_call(
        body, name="conv_fwd", grid=(2048 // LANE,),
        in_specs=[pl.BlockSpec((tt, LANE), lambda j: (0, BLK_QK + j)), pl.BlockSpec((9, LANE), lambda j: (0, j)),
                  pl.BlockSpec((1, LANE), lambda j: (0, j))],
        out_specs=pl.BlockSpec((tt, LANE), lambda j: (0, j)),
        out_shape=jax.ShapeDtypeStruct((tt, 2048), F32),
        compiler_params=_vmem(40 * tt * LANE * 4),
    )(u, w9, cb)


def _conv_bwd(dcp, u, w9, tc):
    tt = u.shape[0]
    t = tt - tc

    def body(d_ref, u_ref, w_ref, du_ref, gw_ref, gb_ref):
        w = [w_ref[r:r + 1, :] for r in range(9)]
        csum = lambda a: jnp.sum(a, axis=0, keepdims=True)
        dc = d_ref[0:tc, :]
        xc = u_ref[0:tc, :]
        ms = _seq_masks(tc)
        du_ref[0:tc, :] = (w[3] * _shift(dc, -1, ms["right"]) + w[4] * dc + w[5] * _shift(dc, 1, ms["left"])).astype(BF16)
        gmid = [csum(dc * _shift(xc, 1, ms["left"])), csum(dc * xc), csum(dc * _shift(xc, -1, ms["right"]))]
        d = d_ref[tc:tt, :]
        x = u_ref[tc:tt, :]
        mg = _grid_masks(t)
        dtaps = (_shift(d, -1, mg["right"]), d, _shift(d, 1, mg["left"]))
        rows = [w[3 * i] * dtaps[0] + w[3 * i + 1] * dtaps[1] + w[3 * i + 2] * dtaps[2] for i in range(3)]
        du_ref[tc:tt, :] = (rows[1] + _shift(rows[0], -GRID_W, mg["down"]) + _shift(rows[2], GRID_W, mg["up"])).astype(BF16)
        xtaps = (_shift(x, 1, mg["left"]), x, _shift(x, -1, mg["right"]))
        for j in range(3):
            gw_ref[j:j + 1, :] = csum(d * _shift(xtaps[j], GRID_W, mg["up"]))
            gw_ref[3 + j:4 + j, :] = csum(d * xtaps[j]) + gmid[j]
            gw_ref[6 + j:7 + j, :] = csum(d * _shift(xtaps[j], -GRID_W, mg["down"]))
        gb_ref[...] = csum(d) + csum(dc)

    return pl.pallas_call(
        body, name="conv_bwd", grid=(2048 // LANE,),
        in_specs=[pl.BlockSpec((tt, LANE), lambda j: (0, j)), pl.BlockSpec((tt, LANE), lambda j: (0, BLK_QK + j)),
                  pl.BlockSpec((9, LANE), lambda j: (0, j))],
        out_specs=[pl.BlockSpec((tt, LANE), lambda j: (0, j)), pl.BlockSpec((9, LANE), lambda j: (0, j)),
                   pl.BlockSpec((1, LANE), lambda j: (0, j))],
        out_shape=[jax.ShapeDtypeStruct((tt, 2048), BF16), jax.ShapeDtypeStruct((9, 2048), F32),
                   jax.ShapeDtypeStruct((1, 2048), F32)],
        compiler_params=_vmem(48 * tt * LANE * 4),
    )(dcp, u, w9)


SUB = 2
STEP = SUB * CHUNK


def _chunk_of(pos, ncc, nc, rev):
    if not rev:
        return pos
    return jnp.where(pos < ncc, ncc - 1 - pos, nc - 1 - (pos - ncc))


def _sub_rows(rev):
    order = range(SUB - 1, -1, -1) if rev else range(SUB)
    return [(s, slice(s * CHUNK, (s + 1) * CHUNK)) for s in order]


def _hgrn_fwd(u, lower_d, ncc, rev):
    tt = u.shape[0]
    nc, ncc = tt // STEP, ncc // SUB
    seg_f = SEG_AFB if rev else SEG_AFF

    def body(zq_ref, zf_ref, v_ref, lb_ref, o_ref, hist_ref, st_ref):
        @pl.when(pl.program_id(0) == 0)
        def _():
            st_ref[...] = jnp.zeros_like(st_ref)

        st = st_ref[...]
        for s, r in _sub_rows(rev):
            hist_ref[s] = st
            o, st = hg_chunk_fwd(zq_ref[r, :], zf_ref[r, :], v_ref[r, :], lb_ref[...], st, rev)
            o_ref[r, :] = o
        st_ref[...] = st

    seg = lambda s: pl.BlockSpec((STEP, 1024), lambda j: (_chunk_of(j, ncc, nc, rev), s))
    return pl.pallas_call(
        body, name="hgrn_fwd_rev" if rev else "hgrn_fwd", grid=(nc,),
        in_specs=[seg(SEG_AQ), seg(seg_f), seg(SEG_AI), pl.BlockSpec((1, 1024), lambda j: (0, 0))],
        out_specs=[pl.BlockSpec((STEP, 1024), lambda j: (_chunk_of(j, ncc, nc, rev), 0)),
                   pl.BlockSpec((SUB, 1024, HG_D), lambda j: (_chunk_of(j, ncc, nc, rev), 0, 0))],
        out_shape=[jax.ShapeDtypeStruct((tt, 1024), F32), jax.ShapeDtypeStruct((nc * SUB, 1024, HG_D), F32)],
        scratch_shapes=[pltpu.VMEM((1024, HG_D), F32)],
    )(u, u, u, lower_d)


def _hgrn_bwd(u, lower_d, hist, do, acc, ncc, rev, ride=None):
    tt = u.shape[0]
    nc, ncc = tt // STEP, ncc // SUB
    seg_f = SEG_AFB if rev else SEG_AFF
    has_acc = acc is not None
    has_a2a = ride is not None
    odt = BF16 if has_acc else F32

    def body(zq_ref, zf_ref, v_ref, lb_ref, hist_ref, do_ref, *rest):
        if has_acc:
            aq_ref, av_ref = rest[:2]
            rest = rest[2:]
        if has_a2a:
            x_ref, rest = rest[0], rest[1:]
        dzf_ref, dzq_ref, dv_ref, dlb_ref = rest[:4]
        rest = rest[4:]
        if has_a2a:
            comm = (x_ref, rest[0]) + tuple(rest[2:])
            dst_ref = rest[1]
        else:
            dst_ref = rest[0]

        @pl.when(pl.program_id(0) == 0)
        def _():
            dst_ref[...] = jnp.zeros_like(dst_ref)
            dlb_ref[...] = jnp.zeros_like(dlb_ref)
            if has_a2a:
                ride.start(*comm)

        dst = dst_ref[...]
        dlb_sum = dlb_ref[...]
        for s, r in reversed(_sub_rows(rev)):
            dzq, dzf, dv, dlb, dst = hg_chunk_bwd(zq_ref[r, :], zf_ref[r, :], v_ref[r, :], lb_ref[...],
                                                  hist_ref[s], do_ref[r, :], dst, rev)
            dlb_sum = dlb_sum + dlb
            dzf_ref[r, :] = dzf.astype(BF16)
            if has_acc:
                dzq = dzq + aq_ref[r, :]
                dv = dv + av_ref[r, :]
            dzq_ref[r, :] = dzq.astype(odt)
            dv_ref[r, :] = dv.astype(odt)
        dst_ref[...] = dst
        dlb_ref[...] = dlb_sum

        if has_a2a:
            @pl.when(pl.program_id(0) == nc - 1)
            def _():
                ride.wait(*comm)

    cidx = lambda j: _chunk_of(nc - 1 - j, ncc, nc, rev)
    seg = lambda s: pl.BlockSpec((STEP, 1024), lambda j: (cidx(j), s))
    row = pl.BlockSpec((STEP, 1024), lambda j: (cidx(j), 0))
    ins = [u, u, u, lower_d, hist, do] + (list(acc) if has_acc else []) + ([ride.x] if has_a2a else [])
    return pl.pallas_call(
        body, name="hgrn_bwd_rev" if rev else "hgrn_bwd", grid=(nc,),
        in_specs=[seg(SEG_AQ), seg(seg_f), seg(SEG_AI), pl.BlockSpec((1, 1024), lambda j: (0, 0)),
                  pl.BlockSpec((SUB, 1024, HG_D), lambda j: (cidx(j), 0, 0)), row] + ([row, row] if has_acc else [])
        + ([_ANY] if has_a2a else []),
        out_specs=[row, row, row, pl.BlockSpec((1, 1024), lambda j: (0, 0))] + ([_ANY] if has_a2a else []),
        out_shape=[jax.ShapeDtypeStruct((tt, 1024), BF16), jax.ShapeDtypeStruct((tt, 1024), odt),
                   jax.ShapeDtypeStruct((tt, 1024), odt), jax.ShapeDtypeStruct((1, 1024), F32)]
        + ([ride.out_shape] if has_a2a else []),
        scratch_shapes=[pltpu.VMEM((1024, HG_D), F32)] + (_RIDE_SCRATCH if has_a2a else []),
    )(*ins)


def _gate_views(g_ref, b_ref, r, head, rev):
    gc = g_ref[r, :] + b_ref[...]
    lane = lax.broadcasted_iota(jnp.int32, (1, LANE), 1)
    eye = _eye()
    d = 1 if rev else 0
    ii, fi = d * ML_HEADS + head, 2 * ML_HEADS + d * ML_HEADS + head
    col = lambda idx: jnp.sum(jnp.where(lane == idx, gc, 0.0), axis=1, keepdims=True)
    row = lambda c: jnp.sum(eye * c, axis=0, keepdims=True)
    gi, gf = col(ii), col(fi)
    return gi, row(gi), gf, row(gf)


def _mlstm_fwd(cpre, u, bias, ncc, rev):
    tt = u.shape[0]
    nc, ncc = tt // STEP, ncc // SUB
    nhd = ML_HEADS

    def body(q_ref, k_ref, v_ref, g_ref, b_ref, h_ref, ch_ref, nh_ref, mh_ref, c_ref, n_ref, m_ref):
        @pl.when(pl.program_id(0) == 0)
        def _():
            c_ref[...] = jnp.zeros_like(c_ref)
            n_ref[...] = jnp.zeros_like(n_ref)
            m_ref[...] = jnp.zeros_like(m_ref)

        c, n_all, m_all = c_ref[...], n_ref[...], m_ref[...]
        n = [n_all[hd:hd + 1, :] for hd in range(nhd)]
        m = [m_all[hd:hd + 1, 0:1] for hd in range(nhd)]
        for s, r in _sub_rows(rev):
            ch_ref[s] = c
            for hd in range(nhd):
                nh_ref[s, hd:hd + 1, :] = n[hd]
                mh_ref[s, hd:hd + 1, :] = jnp.broadcast_to(m[hd], (1, LANE))
            gates = [_gate_views(g_ref, b_ref, r, hd, rev) for hd in range(nhd)]
            h, c, n, m = ml_chunk_fwd(q_ref[r, :], k_ref[r, :], v_ref[r, :], gates, c, n, m, rev)
            h_ref[r, :] = h
        c_ref[...] = c
        for hd in range(nhd):
            n_ref[hd:hd + 1, :] = n[hd]
            m_ref[hd:hd + 1, :] = jnp.broadcast_to(m[hd], (1, LANE))

    cidx = lambda j: _chunk_of(j, ncc, nc, rev)
    row = lambda s: pl.BlockSpec((STEP, 1024), lambda j: (cidx(j), s))
    st3 = lambda a, b: pl.BlockSpec((SUB, a, b), lambda j: (cidx(j), 0, 0))
    return pl.pallas_call(
        body, name="mlstm_fwd_rev" if rev else "mlstm_fwd", grid=(nc,),
        in_specs=[row(0), row(1), row(SEG_BV), pl.BlockSpec((STEP, LANE), lambda j: (cidx(j), BLK_GATE)),
                  pl.BlockSpec((1, LANE), lambda j: (0, 0))],
        out_specs=[row(0), st3(1024, ML_D), st3(8, ML_D), st3(8, LANE)],
        out_shape=[jax.ShapeDtypeStruct((tt, 1024), F32), jax.ShapeDtypeStruct((nc * SUB, 1024, ML_D), F32),
                   jax.ShapeDtypeStruct((nc * SUB, 8, ML_D), F32), jax.ShapeDtypeStruct((nc * SUB, 8, LANE), F32)],
        scratch_shapes=[pltpu.VMEM((1024, ML_D), F32), pltpu.VMEM((8, ML_D), F32), pltpu.VMEM((8, LANE), F32)],
    )(cpre, cpre, u, u, bias)


def _mlstm_bwd(cpre, u, bias, chist, nhist, mhist, dh, acc, ncc, rev):
    tt = u.shape[0]
    nc, ncc = tt // STEP, ncc // SUB
    nhd = ML_HEADS
    has_acc = acc is not None
    odt = BF16 if has_acc else F32
    d = 1 if rev else 0

    def body(q_ref, k_ref, v_ref, g_ref, b_ref, ch_ref, nh_ref, mh_ref, dh_ref, *rest):
        if has_acc:
            aqk_ref, av_ref, ag_ref = rest[:3]
            rest = rest[3:]
        dqk_ref, dv_ref, dg_ref, gs_ref, dc_ref, dn_ref = rest

        @pl.when(pl.program_id(0) == 0)
        def _():
            dc_ref[...] = jnp.zeros_like(dc_ref)
            dn_ref[...] = jnp.zeros_like(dn_ref)
            gs_ref[...] = jnp.zeros_like(gs_ref)

        lane = lax.broadcasted_iota(jnp.int32, (1, LANE), 1)
        dc, dn_all, gs = dc_ref[...], dn_ref[...], gs_ref[...]
        dn = [dn_all[hd:hd + 1, :] for hd in range(nhd)]
        for s, r in reversed(_sub_rows(rev)):
            gates = [_gate_views(g_ref, b_ref, r, hd, rev) for hd in range(nhd)]
            n_all, m_all = nh_ref[s], mh_ref[s]
            dqp, dkp, dv, dgi, dgf, dc, dn = ml_chunk_bwd(
                q_ref[r, :], k_ref[r, :], v_ref[r, :], gates, ch_ref[s],
                [n_all[hd:hd + 1, :] for hd in range(nhd)], [m_all[hd:hd + 1, 0:1] for hd in range(nhd)],
                dh_ref[r, :], dc, dn, rev)
            dg = ag_ref[r, :] if has_acc else jnp.zeros((CHUNK, LANE), F32)
            for hd in range(nhd):
                dg = dg + jnp.where(lane == d * ML_HEADS + hd, dgi[hd], 0.0)
                dg = dg + jnp.where(lane == 2 * ML_HEADS + d * ML_HEADS + hd, dgf[hd], 0.0)
            if has_acc:
                dqp = dqp + aqk_ref[r, 0:W_B]
                dkp = dkp + aqk_ref[r, W_B:2 * W_B]
                dv = dv + av_ref[r, :]
            dqk_ref[r, 0:W_B] = dqp
            dqk_ref[r, W_B:2 * W_B] = dkp
            dv_ref[r, :] = dv.astype(odt)
            dg_ref[r, :] = dg.astype(odt)
            gs = gs + jnp.sum(dg, axis=0, keepdims=True)
        dc_ref[...] = dc
        gs_ref[...] = gs
        for hd in range(nhd):
            dn_ref[hd:hd + 1, :] = dn[hd]

    cidx = lambda j: _chunk_of(nc - 1 - j, ncc, nc, rev)
    row = lambda s: pl.BlockSpec((STEP, 1024), lambda j: (cidx(j), s))
    wide = pl.BlockSpec((STEP, 2048), lambda j: (cidx(j), 0))
    gate = pl.BlockSpec((STEP, LANE), lambda j: (cidx(j), 0))
    st3 = lambda a, b: pl.BlockSpec((SUB, a, b), lambda j: (cidx(j), 0, 0))
    ins = [cpre, cpre, u, u, bias, chist, nhist, mhist, dh] + (list(acc) if has_acc else [])
    return pl.pallas_call(
        body, name="mlstm_bwd_rev" if rev else "mlstm_bwd", grid=(nc,),
        in_specs=[row(0), row(1), row(SEG_BV), pl.BlockSpec((STEP, LANE), lambda j: (cidx(j), BLK_GATE)),
                  pl.BlockSpec((1, LANE), lambda j: (0, 0)),
                  st3(1024, ML_D), st3(8, ML_D), st3(8, LANE), row(0)] + ([wide, row(0), gate] if has_acc else []),
        out_specs=[wide, row(0), gate, pl.BlockSpec((1, LANE), lambda j: (0, 0))],
        out_shape=[jax.ShapeDtypeStruct((tt, 2048), F32), jax.ShapeDtypeStruct((tt, 1024), odt),
                   jax.ShapeDtypeStruct((tt, LANE), odt), jax.ShapeDtypeStruct((1, LANE), F32)],
        scratch_shapes=[pltpu.VMEM((1024, ML_D), F32), pltpu.VMEM((8, ML_D), F32)],
    )(*ins)


def _whole(body, out_shape, name, *args, nbytes=0):
    return pl.pallas_call(body, name=name, out_shape=out_shape, compiler_params=_vmem(nbytes))(*args)


def _mod_fwd(cs, w_cols, b_cols):
    def body(c_ref, w_ref, b_ref, o_ref):
        o_ref[...] = _exact_nn(_silu(c_ref[...]), w_ref[...]) + b_ref[...]

    return _whole(body, jax.ShapeDtypeStruct((16, w_cols.shape[1]), F32), "mod_fwd", cs, w_cols, b_cols,
                  nbytes=4 * w_cols.size * 4)


def _mod_bwd_w(cs, d9, w_cols):
    def body(c_ref, d_ref, w_ref, gw_ref, pc_ref):
        gw_ref[...] = _exact_tn(_silu(c_ref[...]), d_ref[...])
        pc = lax.dot_general(d_ref[8:16, :], w_ref[...], (((1,), (1,)), ((), ())), precision=lax.Precision.HIGHEST,
                             preferred_element_type=F32)
        row = lax.broadcasted_iota(jnp.int32, pc.shape, 0)
        pc_ref[...] = jnp.where(row == 0, pc, 0.0)

    return _whole(body, [jax.ShapeDtypeStruct(w_cols.shape, F32), jax.ShapeDtypeStruct((8, w_cols.shape[0]), F32)],
                  "mod_bwd_w", cs, d9, w_cols, nbytes=6 * w_cols.size * 4)


def _lower_fwd(lb4):
    def body(l_ref, o_ref):
        o_ref[...] = jnp.zeros_like(o_ref)
        o_ref[0:1, :] = 1.0 / (1.0 + jnp.exp(l_ref[1:2, :] - l_ref[0:1, :]))
        o_ref[1:2, :] = 1.0 / (1.0 + jnp.exp(l_ref[3:4, :] - l_ref[2:3, :]))

    return _whole(body, jax.ShapeDtypeStruct((8, lb4.shape[1]), F32), "lower_fwd", lb4)


def _reduce8(g, name):
    def body(g_ref, o_ref):
        acc = g_ref[0]
        for k in range(1, N_DEV):
            acc = acc + g_ref[k]
        o_ref[...] = acc

    return _whole(body, jax.ShapeDtypeStruct(g.shape[1:], F32), name, g, nbytes=4 * g.size * 4)


_PACK = (("dmodx", 48), ("dmodc", 48), ("gconvw", 144), ("gconvb", 16), ("dlower", 16), ("ghgw", 8), ("gmlw", 8),
         ("glng", 16), ("glnb", 16), ("losssq", 16), ("ggate", 8))
_PACK_ROWS = sum(r for _, r in _PACK)


def _pack_offsets():
    off, out = 0, {}
    for name, rows in _PACK:
        out[name] = (off, rows)
        off += rows
    return out


def _small_finish(total, p0, d_feat):
    offs = _pack_offsets()

    def body(t_ref, p_ref, gb_ref, a0_ref, a1_ref, loss_ref):
        ox, oc, ol, oq = offs["dmodx"][0], offs["dmodc"][0], offs["dlower"][0], offs["losssq"][0]
        gb_ref[...] = t_ref[ox:ox + 48, :] + t_ref[oc:oc + 48, :]
        p = p_ref[...]
        da0 = t_ref[ol:ol + 16, :] * p * (1.0 - p)
        a0_ref[...] = da0
        a1_ref[...] = -da0
        sq = t_ref[oq:oq + 16, :]
        tot = jnp.sum(jnp.sum(sq, axis=1, keepdims=True), axis=0, keepdims=True)
        loss_ref[...] = jnp.broadcast_to(tot * (0.5 / d_feat), loss_ref.shape)

    s = jax.ShapeDtypeStruct
    return _whole(body, [s((48, LANE), F32), s((16, LANE), F32), s((16, LANE), F32), s((8, LANE), F32)],
                  "small_finish", total, p0)


def _cctx_grad(parts, c_ctx8):
    def body(p_ref, c_ref, o_ref):
        acc = p_ref[0]
        for k in range(1, N_DEV):
            acc = acc + p_ref[k]
        o_ref[...] = acc * _dsilu(c_ref[...])

    return _whole(body, jax.ShapeDtypeStruct(c_ctx8.shape, F32), "cctx_grad", parts, c_ctx8)


def _adam_math(w, g, m, v):
    m = ADAM_B1 * m + (1.0 - ADAM_B1) * g
    v = ADAM_B2 * v + (1.0 - ADAM_B2) * (g * g)
    m_hat = m / (1.0 - ADAM_B1 ** ADAM_STEP)
    v_hat = v / (1.0 - ADAM_B2 ** ADAM_STEP)
    delta = -ADAM_LR * (m_hat / (jnp.sqrt(v_hat) + ADAM_EPS) + ADAM_WD * w)
    return delta, m, v


def _adamw(w, g, m, v, rows, name):
    r, c = w.shape

    def body(w_ref, g_ref, m_ref, v_ref, d_ref, mo_ref, vo_ref):
        d_ref[...], mo_ref[...], vo_ref[...] = _adam_math(w_ref[...], g_ref[...], m_ref[...], v_ref[...])

    spec = pl.BlockSpec((rows, c), lambda i: (i, 0))
    return pl.pallas_call(
        body, name=name, grid=(r // rows,), in_specs=[spec] * 4, out_specs=[spec] * 3,
        out_shape=[jax.ShapeDtypeStruct((r, c), F32)] * 3,
        compiler_params=_vmem(16 * rows * (c + LANE) * 4),
    )(w, g, m, v)


def _rs_adamw(recv, w, m, v, tile, name, by_cols=False):
    _, r, c = recv.shape

    def body(r_ref, w_ref, m_ref, v_ref, g_ref, d_ref, mo_ref, vo_ref):
        g = r_ref[0].astype(F32)
        for k in range(1, N_DEV):
            g = g + r_ref[k].astype(F32)
        g_ref[...] = g
        d_ref[...], mo_ref[...], vo_ref[...] = _adam_math(w_ref[...], g, m_ref[...], v_ref[...])

    if by_cols:
        spec = pl.BlockSpec((r, tile), lambda i: (0, i))
        rspec = pl.BlockSpec((N_DEV, r, tile), lambda i: (0, 0, i))
        steps, elems = c // tile, (r + 16) * tile
    else:
        spec = pl.BlockSpec((tile, c), lambda i: (i, 0))
        rspec = pl.BlockSpec((N_DEV, tile, c), lambda i: (0, i, 0))
        steps, elems = r // tile, tile * (c + LANE)
    return pl.pallas_call(
        body, name=name, grid=(steps,), in_specs=[rspec] + [spec] * 3, out_specs=[spec] * 4,
        out_shape=[jax.ShapeDtypeStruct((r, c), F32)] * 4,
        compiler_params=_vmem(2 * elems * (N_DEV * 2 + 7 * 4) + (4 << 20)),
    )(recv, w, m, v)


def _all_gather(x, name):
    r, c = x.shape

    def body(x_ref, out_ref, send_sems, recv_sems, local_sem):
        px, py, pc = _position()
        me, sibling = (px, py, pc), (px, py, 1 - pc)
        chips = [(1 - px, py), (px, 1 - py), (1 - px, 1 - py)]

        def slot(qx, qy, qc):
            return out_ref.at[4 * qx + 2 * qy + qc]

        def copy(k, block, to, src=None):
            return pltpu.make_async_remote_copy(
                src_ref=slot(*block) if src is None else src, dst_ref=slot(*block),
                send_sem=send_sems.at[k], recv_sem=recv_sems.at[k], device_id=to, device_id_type=MESH)

        mine = pltpu.make_async_copy(x_ref, slot(*me), local_sem)
        mine.start()
        first = [copy(1 + j, me, (*chip, pc), src=x_ref) for j, chip in enumerate(chips)]
        first.append(copy(0, me, sibling, src=x_ref))
        for cp in first:
            cp.start()
        passed = [copy(4 + j, (*chip, pc), sibling) for j, chip in enumerate(chips)]
        for j, chip in enumerate(chips):
            copy(1 + j, (*chip, pc), me).wait_recv()
            passed[j].start()
        copy(0, sibling, me).wait_recv()
        for j, chip in enumerate(chips):
            copy(4 + j, (*chip, 1 - pc), me).wait_recv()
        for cp in first + passed:
            cp.wait_send()
        mine.wait()

    return pl.pallas_call(
        body, name=name, out_shape=jax.ShapeDtypeStruct((N_DEV, r, c), x.dtype),
        in_specs=[pl.BlockSpec(memory_space=pl.ANY)], out_specs=pl.BlockSpec(memory_space=pl.ANY),
        scratch_shapes=[pltpu.SemaphoreType.DMA((7,)), pltpu.SemaphoreType.DMA((7,)), pltpu.SemaphoreType.DMA],
    )(x)


def _local_step(ctx, x, target, modp, lower, wt_u, w_o, w9, conv_b, gate_b, hgw, mlw, ln_g, ln_b, exchange):
    tc = ctx.shape[0]
    tt = tc + x.shape[0]
    nbc, ncc = tc // ROWS, tc // CHUNK
    lower_f, lower_b = lower[0:1], lower[1:2]

    hc = _modulate_fwd(ctx, x, modp)
    tmh = _pick(tt, (1088, 768, 512, 256))
    if exchange:
        u, w_o = _mm(hc, wt_u, "nt", F32, tmh, 1152, D_MODEL, "mm_u", ride=_Ride("gather", w_o))
        w_o = w_o.reshape(D_MODEL, D_MODEL)
    else:
        u = _mm(hc, wt_u, "nt", F32, tmh, 1152, D_MODEL, "mm_u")
    cpre = _conv_fwd(u, w9, conv_b, tc)
    bias = jnp.pad(gate_b.reshape(1, 16), ((0, 0), (0, LANE - 16)))

    o_f, hist_f = _hgrn_fwd(u, lower_f, ncc, False)
    o_b, hist_b = _hgrn_fwd(u, lower_b, ncc, True)
    h_f, ch_f, nh_f, mh_f = _mlstm_fwd(cpre, u, bias, ncc, False)
    h_b, ch_b, nh_b, mh_b = _mlstm_fwd(cpre, u, bias, ncc, True)
    y = _post_fwd(o_f, o_b, h_f, h_b, u, hgw, mlw, nbc)
    dz, dxa, fsum = _final(y, w_o, x, target, modp, ln_g, ln_b)

    dw_o = _mm(y, dz, "tn", BF16, D_MODEL, 1024, _pick(y.shape[0], (512, 256)), "mm_dwo")
    do, dhm, du1, psum = _post_bwd(dz, w_o, o_f, o_b, h_f, h_b, u, hgw, mlw, nbc)
    if exchange:
        dzf_f, dzq, dv_a, dlb_f, dw_o = _hgrn_bwd(
            u, lower_f, hist_f, do, None, ncc, False,
            ride=_Ride("a2a", dw_o.reshape(N_DEV, D_MODEL // N_DEV, D_MODEL)))
    else:
        dzf_f, dzq, dv_a, dlb_f = _hgrn_bwd(u, lower_f, hist_f, do, None, ncc, False)
    dzf_b, dzq, dv_a, dlb_b = _hgrn_bwd(u, lower_b, hist_b, do, (dzq, dv_a), ncc, True)
    dqk, dv_m, dg, _ = _mlstm_bwd(cpre, u, bias, ch_f, nh_f, mh_f, dhm, None, ncc, False)
    dqk, dv_m, dg, gsum = _mlstm_bwd(cpre, u, bias, ch_b, nh_b, mh_b, dhm, (dqk, dv_m, dg), ncc, True)
    du5, gconvw, gconvb = _conv_bwd(dqk, u, w9, tc)
    du = jnp.concatenate([dzq, dzf_f, dzf_b, dv_a, du1[:, 0:1024], du5, dv_m, du1[:, 1024:3072], dg], axis=1)
    half = D_MODEL // 2
    tkw = _pick(tt, (2176, 1088, 768, 512, 256))
    blocks = lambda g: g.reshape(N_DEV, N_IN // N_DEV, half)
    dwu = lambda name, off, ride: _mm(du, hc, "tn", BF16, 1152, half, tkw, name, n_cols=half, n_off=off, ride=ride,
                                      m_out=N_IN)
    dwt_a = dwu("mm_dwu_a", 0, None)
    if exchange:
        dwt_b, got = dwu("mm_dwu_b", 1, _Ride("a2a", blocks(dwt_a), cols=(0, D_MODEL)))
        dh, dwt_u = _mm(du, wt_u, "nn", F32, tmh, half, 3456, "mm_dh",
                        ride=_Ride("a2a", blocks(dwt_b), cols=(half, D_MODEL), into=got))
    else:
        dwt_u = jnp.concatenate([dwt_a, dwu("mm_dwu_b", 1, None)], axis=1)
        dh = _mm(du, wt_u, "nn", F32, tmh, half, 3456, "mm_dh")
    gx, msum = _modulate_bwd(dh, ctx, x, modp, dxa)

    zero_row = jnp.zeros((1, D_MODEL), F32)
    small = dict(
        dmodx=jnp.concatenate([msum[2:3], msum[3:4], fsum[0:1]], axis=0),
        dmodc=jnp.concatenate([msum[0:1], msum[1:2], zero_row], axis=0),
        gconvw=gconvw, gconvb=gconvb, dlower=jnp.concatenate([dlb_f, dlb_b], axis=0),
        ghgw=psum[0:1], gmlw=psum[1:2], glng=fsum[1:2], glnb=fsum[2:3], losssq=fsum[3:4],
        ggate=jnp.concatenate([gsum, jnp.zeros((7, LANE), F32)], axis=0))
    return gx, dwt_u, dw_o, small


def _pack_small(small):
    return jnp.concatenate([small[name].reshape(rows, LANE) for name, rows in _PACK], axis=0)


def _flat_pad(a, rows):
    flat = a.reshape(-1)
    return jnp.pad(flat, (0, rows * LANE - flat.shape[0])).reshape(rows, LANE)


def kernel(x, c, ctx, c_ctx, w_mod, b_mod, w_in, conv_w, conv_b, hg_lb, ml_gate_b, hg_norm_w, ml_norm_w, w_out, ln_g, ln_b, loss_target, m_c_ctx, m_w_mod, m_b_mod, m_w_in, m_conv_w, m_conv_b, m_hg_lb, m_ml_gate_b, m_hg_norm_w, m_ml_norm_w, m_w_out, m_ln_g, m_ln_b, v_c_ctx, v_w_mod, v_b_mod, v_w_in, v_conv_w, v_conv_b, v_hg_lb, v_ml_gate_b, v_hg_norm_w, v_ml_norm_w, v_w_out, v_ln_g, v_ln_b):
    px, py, pc = _position()
    me = 4 * px + 2 * py + pc
    d = D_MODEL
    n_mod = w_mod.shape[2]
    n_wi = w_in.shape[2]
    n_cv = conv_w.shape[3]
    n_lb = hg_lb.shape[2]

    pack0 = jnp.concatenate([c.reshape(-1), conv_w.reshape(-1), hg_lb.reshape(-1)]).reshape(1, -1)
    g0 = _all_gather(pack0, "gather_small_inputs")[:, 0, :]
    c_all = g0[:, :d]
    w9 = jnp.transpose(g0[:, d:d + 9 * n_cv].reshape(N_DEV, 9, n_cv), (1, 0, 2)).reshape(9, N_DEV * n_cv)
    lb4 = jnp.transpose(g0[:, d + 9 * n_cv:].reshape(N_DEV, 4, n_lb), (1, 0, 2)).reshape(4, N_DEV * n_lb)
    lower = _lower_fwd(lb4)

    cs = jnp.concatenate([c_all, c_ctx.reshape(1, d), jnp.zeros((7, d), F32)], axis=0)
    b_cols = lax.dynamic_slice(b_mod, (0, me * n_mod), (1, n_mod))
    slab = _mod_fwd(cs, w_mod[0], b_cols)
    mod_all = jnp.transpose(_all_gather(slab, "gather_mod"), (1, 0, 2)).reshape(16, N_DEV * n_mod)
    mod_x = lax.dynamic_slice(mod_all, (me, 0), (1, 3 * d)).reshape(3, d)
    modp = jnp.stack([mod_all[8].reshape(3, d), mod_x])

    wt = _all_gather(w_in[0].T.astype(BF16), "gather_w_in").reshape(N_DEV * n_wi, d)
    wt_u = jnp.pad(wt, ((0, N_U - N_DEV * n_wi), (0, 0)))

    gx, recv_wi, recv_wo, small = _local_step(ctx[0], x[0], loss_target[0], modp, lower, wt_u, w_out[0].astype(BF16),
                                              w9, conv_b, ml_gate_b[0], hg_norm_w, ml_norm_w, ln_g, ln_b, True)
    g_wi, d_wi, nm_wi, nv_wi = [a.T for a in _rs_adamw(recv_wi, w_in[0].T, m_w_in[0].T, v_w_in[0].T, 256,
                                                       "adamw_w_in", by_cols=True)]
    g_wo, d_wo, nm_wo, nv_wo = _rs_adamw(recv_wo, w_out[0], m_w_out[0], v_w_out[0], 64, "adamw_w_out")

    packs = _all_gather(_pack_small(small), "gather_small_grads")
    total = _reduce8(packs, "reduce_small_grads")
    offs = _pack_offsets()
    piece = lambda name: total[offs[name][0]:offs[name][0] + offs[name][1]]
    g_bmod, g_lb0, g_lb1, loss8 = _small_finish(total, lower[0:2].reshape(16, LANE), float(d))

    ox = offs["dmodx"][0]
    dmodx_all = packs[:, ox:ox + 48, :].reshape(N_DEV, 3 * d)
    dmodc_tot = piece("dmodc").reshape(1, 3 * d)
    d9 = jnp.concatenate([dmodx_all, dmodc_tot, jnp.zeros((7, 3 * d), F32)], axis=0)
    d9_cols = lax.dynamic_slice(d9, (0, me * n_mod), (16, n_mod))
    g_wmod, pc_part = _mod_bwd_w(cs, d9_cols, w_mod[0])
    c_ctx8 = jnp.concatenate([c_ctx.reshape(1, d), jnp.zeros((7, d), F32)], axis=0)
    g_cctx = _cctx_grad(_all_gather(pc_part, "gather_cctx"), c_ctx8)[0]
    d_wmod, nm_wmod, nv_wmod = _adamw(w_mod[0], g_wmod, m_w_mod[0], v_w_mod[0], 256, "adamw_w_mod")

    g_convw_full = piece("gconvw").reshape(9, d)
    g_convw = lax.dynamic_slice(g_convw_full, (0, me * n_cv), (9, n_cv)).reshape(conv_w.shape)
    lb_full = jnp.stack([jnp.stack([g_lb0[0:8].reshape(-1), g_lb1[0:8].reshape(-1)]),
                         jnp.stack([g_lb0[8:16].reshape(-1), g_lb1[8:16].reshape(-1)])])
    g_hglb = lax.dynamic_slice(lb_full, (0, 0, me * n_lb), (2, 2, n_lb))
    grads = dict(
        c_ctx=g_cctx, b_mod=g_bmod.reshape(b_mod.shape), conv_w=g_convw, conv_b=piece("gconvb").reshape(conv_b.shape),
        hg_lb=g_hglb, ml_gate_b=piece("ggate")[0, :16].reshape(ml_gate_b.shape),
        hg_norm_w=piece("ghgw").reshape(hg_norm_w.shape), ml_norm_w=piece("gmlw").reshape(ml_norm_w.shape),
        ln_g=piece("glng").reshape(ln_g.shape), ln_b=piece("glnb").reshape(ln_b.shape))
    params = dict(c_ctx=(c_ctx, m_c_ctx, v_c_ctx), b_mod=(b_mod, m_b_mod, v_b_mod), conv_w=(conv_w, m_conv_w, v_conv_w),
                  conv_b=(conv_b, m_conv_b, v_conv_b), hg_lb=(hg_lb, m_hg_lb, v_hg_lb),
                  ml_gate_b=(ml_gate_b, m_ml_gate_b, v_ml_gate_b), hg_norm_w=(hg_norm_w, m_hg_norm_w, v_hg_norm_w),
                  ml_norm_w=(ml_norm_w, m_ml_norm_w, v_ml_norm_w), ln_g=(ln_g, m_ln_g, v_ln_g), ln_b=(ln_b, m_ln_b, v_ln_b))
    names = list(params)
    rows_of = {n: -(-params[n][0].size // LANE) for n in names}
    rows_tot = -(-sum(rows_of.values()) // 8) * 8
    cat = lambda arrs: jnp.concatenate(
        [_flat_pad(a, rows_of[n]) for n, a in zip(names, arrs)]
        + [jnp.ones((rows_tot - sum(rows_of.values()), LANE), F32)], axis=0)
    d_s, m_s, v_s = _adamw(cat([params[n][0] for n in names]), cat([grads[n] for n in names]),
                           cat([params[n][1] for n in names]), cat([params[n][2] for n in names]), rows_tot, "adamw_small")
    delta, new_m, new_v, off = {}, {}, {}, 0
    for n in names:
        shape, size = params[n][0].shape, params[n][0].size
        take = lambda a: a[off:off + rows_of[n]].reshape(-1)[:size].reshape(shape)
        delta[n], new_m[n], new_v[n] = take(d_s), take(m_s), take(v_s)
        off += rows_of[n]
    grads.update(w_mod=g_wmod[None], w_in=g_wi[None], w_out=g_wo[None])
    delta.update(w_mod=d_wmod[None], w_in=d_wi[None], w_out=d_wo[None])
    new_m.update(w_mod=nm_wmod[None], w_in=nm_wi[None], w_out=nm_wo[None])
    new_v.update(w_mod=nv_wmod[None], w_in=nv_wi[None], w_out=nv_wo[None])

    order = ("c_ctx", "w_mod", "b_mod", "w_in", "conv_w", "conv_b", "hg_lb", "ml_gate_b", "hg_norm_w", "ml_norm_w",
             "w_out", "ln_g", "ln_b")
    return (loss8[0, 0], gx[None], *[grads[n] for n in order], *[delta[n] for n in order],
            *[new_m[n] for n in order], *[new_v[n] for n in order])
```

```python
import functools

import jax
import jax.numpy as jnp
from jax import lax
from jax.experimental import pallas as pl
from jax.experimental.pallas import tpu as pltpu

F32 = jnp.float32
BF16 = jnp.bfloat16

D_MODEL = 2048
W_A = 1024
W_B = 1024
HG_HEADS = 8
HG_D = 128
ML_HEADS = 4
ML_D = 256
CHUNK = 64
N_IN = 10256
LANE = 128
N_U = 81 * LANE
N_DEV = 8
ALPHA = 2.0 ** 0.25
LN_EPS = 1e-5
NORM_EPS = 1e-6
ADAM_LR, ADAM_B1, ADAM_B2, ADAM_EPS, ADAM_WD, ADAM_STEP = 0.001, 0.9, 0.999, 1e-08, 0.01, 10
VMEM_CAP = 60 * 1024 * 1024

SEG_AQ, SEG_AFF, SEG_AFB, SEG_AI, SEG_AZ = range(5)
BLK_QK = 40
SEG_BV, SEG_BO, SEG_BZ = 7, 8, 9
BLK_GATE = 80

MESH = pl.DeviceIdType.MESH


def _vmem(nbytes):
    return pltpu.CompilerParams(vmem_limit_bytes=int(min(VMEM_CAP, max(nbytes, 16 * 1024 * 1024))))


def _sigmoid(x):
    return 1.0 / (1.0 + jnp.exp(-x))


def _silu(x):
    return x * _sigmoid(x)


def _dsilu(x):
    s = _sigmoid(x)
    return s * (1.0 + x * (1.0 - s))


def _silu_both(x):
    s = _sigmoid(x)
    return x * s, s * (1.0 + x * (1.0 - s))


def _bdot(a, b, dims):
    return lax.dot_general(a.astype(BF16), b.astype(BF16), (dims, ((), ())), preferred_element_type=F32)


def _nn(a, b):
    return _bdot(a, b, ((1,), (0,)))


def _nt(a, b):
    return _bdot(a, b, ((1,), (1,)))


def _tn(a, b):
    return _bdot(a, b, ((0,), (0,)))


def _exact_nn(a, b):
    return lax.dot_general(a, b, (((1,), (0,)), ((), ())), precision=lax.Precision.HIGHEST,
                           preferred_element_type=F32)


def _exact_tn(a, b):
    return lax.dot_general(a, b, (((0,), (0,)), ((), ())), precision=lax.Precision.HIGHEST,
                           preferred_element_type=F32)


def _tri(rev):
    t = lax.broadcasted_iota(jnp.int32, (CHUNK, CHUNK), 0)
    s = lax.broadcasted_iota(jnp.int32, (CHUNK, CHUNK), 1)
    return (s >= t) if rev else (s <= t)


def _eye():
    t = lax.broadcasted_iota(jnp.int32, (CHUNK, CHUNK), 0)
    s = lax.broadcasted_iota(jnp.int32, (CHUNK, CHUNK), 1)
    return (s == t).astype(F32)


def _row_to_col(row):
    return jnp.sum(_eye() * row, axis=1, keepdims=True)


def _last_onehot(rev):
    t = lax.broadcasted_iota(jnp.int32, (CHUNK, 1), 0)
    return (t == (0 if rev else CHUNK - 1)).astype(F32)


def _head_slices(width, n_heads):
    hd = width // n_heads
    return [slice(h * hd, (h + 1) * hd) for h in range(n_heads)]


def _scan_sum(x, rev):
    n = x.shape[0]
    t = lax.broadcasted_iota(jnp.int32, x.shape, 0)
    s = 1
    while s < n:
        if rev:
            x = x + jnp.where(t < n - s, pltpu.roll(x, n - s, 0), 0.0)
        else:
            x = x + jnp.where(t >= s, pltpu.roll(x, s, 0), 0.0)
        s *= 2
    return x


def _dot3(a, b, dims):
    a_hi, b_hi = a.astype(BF16), b.astype(BF16)
    a_lo, b_lo = (a - a_hi.astype(F32)).astype(BF16), (b - b_hi.astype(F32)).astype(BF16)
    dot = lambda x, y: lax.dot_general(x, y, (dims, ((), ())), preferred_element_type=F32)
    return dot(a_hi, b_hi) + (dot(a_hi, b_lo) + dot(a_lo, b_hi))


def _hg_common(zq, zf, lb, rev):
    q, dq_dz = _silu_both(zq)
    sg = _sigmoid(zf)
    f = lb + (1.0 - lb) * sg
    g = jnp.log(f)
    k = 1.0 - f
    b = _scan_sum(g, rev)
    b_last = jnp.sum(g, axis=0, keepdims=True)
    r = b[CHUNK // 2:CHUNK // 2 + 1, :]
    e_up = jnp.exp(b - r)
    e_dn = jnp.exp(r - b)
    e_b = e_up * jnp.exp(r)
    e_lb = e_dn * jnp.exp(b_last - r)
    return dict(q=q, dq_dz=dq_dz, sg=sg, f=f, k=k, e_up=e_up, e_dn=e_dn, e_b=e_b, e_lb=e_lb, e_last=jnp.exp(b_last),
                q_t=q * e_up, k_t=k * e_dn, q_s=q * e_b, k_h=k * e_lb, tri=_tri(rev).astype(F32))


def hg_chunk_fwd(zq, zf, v, lb, st, rev):
    c = _hg_common(zq, zf, lb, rev)
    hs = _head_slices(zq.shape[1], zq.shape[1] // HG_D)
    s = [_nt(c["q_t"][:, sl], c["k_t"][:, sl]) for sl in hs]
    oi = [_nt(c["q_s"][:, sl], st[sl, :]) for sl in hs]
    ds = [_tn(v[:, sl], c["k_h"][:, sl]) for sl in hs]
    oa = [_nn(c["tri"] * s_h, v[:, sl]) for s_h, sl in zip(s, hs)]
    o = jnp.concatenate([x + y for x, y in zip(oi, oa)], axis=1)
    st_new = jnp.concatenate([st[sl, :] * c["e_last"][:, sl] + d for sl, d in zip(hs, ds)], axis=0)
    return o, st_new


def hg_chunk_bwd(zq, zf, v, lb, st, do, dst_new, rev):
    c = _hg_common(zq, zf, lb, rev)
    hs = _head_slices(zq.shape[1], zq.shape[1] // HG_D)
    tri, q_t, k_t, q_s, k_h = c["tri"], c["q_t"], c["k_t"], c["q_s"], c["k_h"]
    s = [_nt(q_t[:, sl], k_t[:, sl]) for sl in hs]
    da = [tri * _nt(do[:, sl], v[:, sl]) for sl in hs]
    dq_s = [_nn(do[:, sl], st[sl, :]) for sl in hs]
    dk_h = [_nn(v[:, sl], dst_new[sl, :]) for sl in hs]
    dv_s = [_nt(k_h[:, sl], dst_new[sl, :]) for sl in hs]
    dst_q = [_tn(do[:, sl], q_s[:, sl]) for sl in hs]
    dq_t = [_dot3(da_h, k_t[:, sl], ((1,), (0,))) for da_h, sl in zip(da, hs)]
    dk_t = [_dot3(da_h, q_t[:, sl], ((0,), (0,))) for da_h, sl in zip(da, hs)]
    dv_a = [_tn(tri * s_h, do[:, sl]) for s_h, sl in zip(s, hs)]
    cat = lambda parts: jnp.concatenate(parts, axis=1)
    dq_s, dk_h, dq_t, dk_t = cat(dq_s), cat(dk_h), cat(dq_t), cat(dk_t)
    dv = cat([x + y for x, y in zip(dv_a, dv_s)])
    dst = jnp.concatenate([dst_new[sl, :] * c["e_last"][:, sl] + d for sl, d in zip(hs, dst_q)], axis=0)
    dq = dq_s * c["e_b"] + dq_t * c["e_up"]
    dk = dk_t * c["e_dn"] + dk_h * c["e_lb"]
    db = c["q"] * dq - c["k"] * dk
    ss = cat([jnp.sum(dst_new[sl, :] * st[sl, :], axis=0, keepdims=True) for sl in hs])
    d_all = jnp.sum(dk_h * k_h, axis=0, keepdims=True) + c["e_last"] * ss
    dg = _scan_sum(db, not rev) + d_all
    dzq = dq * c["dq_dz"]
    df = dg / c["f"] - dk
    dzf = df * (1.0 - lb) * c["sg"] * (1.0 - c["sg"])
    dlb = jnp.sum(df * (1.0 - c["sg"]), axis=0, keepdims=True)
    return dzq, dzf, dv, dlb, dst


def _log_sigmoid(x):
    return jnp.minimum(x, 0.0) - jnp.log(1.0 + jnp.exp(-jnp.abs(x)))


def _each(fn, *lists):
    return [fn(*xs) for xs in zip(*lists)]


def _bf(xs):
    return [x.astype(BF16) for x in xs]


def _ml_forward_parts(qp, kp, v, gates, c, n, m, rev):
    hs = _head_slices(qp.shape[1], qp.shape[1] // ML_D)
    q_all, dq_dp = _silu_both(qp)
    k_all, dk_dp = _silu_both(kp)
    k_all = k_all * (ML_D ** -0.5)
    q = [q_all[:, sl] for sl in hs]
    k = [k_all[:, sl] for sl in hs]
    vv = [v[:, sl] for sl in hs]
    cc = [c[sl, :] for sl in hs]
    tri_b = _tri(rev)
    tri = tri_b.astype(F32)
    tri_t = _tri(not rev).astype(F32)
    e_last = _last_onehot(rev)
    qb, kb, vb, cb = _bf(q), _bf(k), _bf(vv), _bf(cc)
    qk = _each(_nt, qb, kb)
    qc = _each(_nt, qb, cb)
    parts = []
    for (gi_c, gi_r, gf_c, gf_r), m_h in zip(gates, m):
        lf_c, lf_r = _log_sigmoid(gf_c), _log_sigmoid(gf_r)
        b_c = jnp.sum(tri * lf_r, axis=1, keepdims=True)
        b_r = jnp.sum(tri_t * lf_c, axis=0, keepdims=True)
        log_w = jnp.where(tri_b, b_c - b_r + gi_r, -jnp.inf)
        m_inter = b_c + m_h
        m_t = jnp.maximum(m_inter, jnp.max(log_w, axis=1, keepdims=True))
        m_new = jnp.sum(m_t * e_last, axis=0, keepdims=True)
        b_last = jnp.sum(b_c * e_last, axis=0, keepdims=True)
        parts.append(dict(a=jnp.exp(m_inter - m_t), p=jnp.exp(log_w - m_t), floor=jnp.exp(-m_t), m_new=m_new,
                          ws=jnp.exp(b_last - b_c + gi_c - m_new), decay=jnp.exp(b_last + m_h - m_new), gf_c=gf_c))
    w = [pt["p"] * x for pt, x in zip(parts, qk)]
    wb = _bf(w)
    wv = _each(_nn, wb, vb)
    for pt, q_h, n_h, w_h, qc_h, wv_h in zip(parts, q, n, w, qc, wv):
        qn = jnp.sum(q_h * n_h, axis=1, keepdims=True)
        num = pt["a"] * qc_h + wv_h
        den = pt["a"] * qn + jnp.sum(w_h, axis=1, keepdims=True)
        pt.update(qn=qn, num=num, den=den, rinv=1.0 / jnp.maximum(jnp.abs(den), pt["floor"]), w=w_h, qc=qc_h)
    return hs, q, k, vv, cc, tri, parts, dict(q=qb, k=kb, v=vb, c=cb, w=wb, dq_dp=dq_dp, dk_dp=dk_dp)


def ml_chunk_fwd(qp, kp, v, gates, c, n, m, rev):
    hs, q, k, vv, cc, tri, parts, bf = _ml_forward_parts(qp, kp, v, gates, c, n, m, rev)
    h = jnp.concatenate([pt["num"] * pt["rinv"] for pt in parts], axis=1)
    upd = _each(_tn, [pt["ws"] * v_h for pt, v_h in zip(parts, vv)], bf["k"])
    c_new = jnp.concatenate([pt["decay"] * c_h + u for pt, c_h, u in zip(parts, cc, upd)], axis=0)
    n_new = [pt["decay"] * n_h + jnp.sum(pt["ws"] * k_h, axis=0, keepdims=True) for pt, n_h, k_h in zip(parts, n, k)]
    return h, c_new, n_new, [pt["m_new"] for pt in parts]


def ml_chunk_bwd(qp, kp, v, gates, c, n, m, dh, dc_new, dn_new, rev):
    hs, q, k, vv, cc, tri, parts, bf = _ml_forward_parts(qp, kp, v, gates, c, n, m, rev)
    dcn = [dc_new[sl, :] for sl in hs]
    dcb = _bf(dcn)
    dnum, dden = [], []
    for pt, sl in zip(parts, hs):
        dh_h = dh[:, sl]
        h = pt["num"] * pt["rinv"]
        signed_live = jnp.where(jnp.abs(pt["den"]) > pt["floor"], jnp.where(pt["den"] >= 0.0, 1.0, -1.0), 0.0)
        dnum.append(dh_h * pt["rinv"])
        dden.append(-jnp.sum(dh_h * h, axis=1, keepdims=True) * pt["rinv"] * signed_live)
    dnb = _bf(dnum)
    dw = [x + y for x, y in zip(_each(_nt, dnb, bf["v"]), dden)]
    kdc = _each(_nt, bf["k"], dcb)
    vdc = _each(_nn, bf["v"], dcb)
    dqk = [x * pt["p"] for x, pt in zip(dw, parts)]
    adn = [pt["a"] * x for pt, x in zip(parts, dnum)]
    dqkb, adnb = _bf(dqk), _bf(adn)
    dv_w = _each(_tn, bf["w"], dnb)
    dq_k = _each(_nn, dqkb, bf["k"])
    dq_c = _each(_nn, adnb, bf["c"])
    dk_q = _each(_tn, dqkb, bf["q"])
    dc_q = _each(_tn, adnb, bf["q"])
    dq, dk, dv, dgi, dgf, dc, dn = [], [], [], [], [], [], []
    for i, pt in enumerate(parts):
        a, ws, decay = pt["a"], pt["ws"], pt["decay"]
        add = a * dden[i]
        e = dw[i] * pt["w"]
        dv.append(dv_w[i] + ws * kdc[i])
        dq.append(dq_k[i] + dq_c[i] + add * n[i])
        dk.append(dk_q[i] + ws * vdc[i] + ws * dn_new[i])
        alpha = (jnp.sum(dnum[i] * pt["qc"], axis=1, keepdims=True) + dden[i] * pt["qn"]) * a
        omega = (jnp.sum(vdc[i] * k[i], axis=1, keepdims=True) + jnp.sum(k[i] * dn_new[i], axis=1, keepdims=True)) * ws
        delta = decay * (jnp.sum(jnp.sum(dcn[i] * cc[i], axis=1, keepdims=True), axis=0, keepdims=True)
                         + jnp.sum(dn_new[i] * n[i], axis=1, keepdims=True))
        dc.append(decay * dcn[i] + dc_q[i])
        dn.append(decay * dn_new[i] + jnp.sum(add * q[i], axis=0, keepdims=True))
        e_rows = jnp.sum(e, axis=1, keepdims=True)
        e_cols = _row_to_col(jnp.sum(e, axis=0, keepdims=True))
        dgi.append(e_cols + omega)
        db = e_rows + alpha - e_cols - omega
        tail = jnp.sum(omega, axis=0, keepdims=True) + delta
        dlf = _row_to_col(jnp.sum(tri * db, axis=0, keepdims=True)) + tail
        dgf.append(dlf * (1.0 - _sigmoid(pt["gf_c"])))
    cat = lambda xs: jnp.concatenate(xs, axis=1)
    dqp = cat(dq) * bf["dq_dp"]
    dkp = cat(dk) * (ML_D ** -0.5) * bf["dk_dp"]
    return dqp, dkp, cat(dv), dgi, dgf, jnp.concatenate(dc, axis=0), dn


def _pick(n, prefs):
    for p in prefs:
        if n % p == 0:
            return p
    raise ValueError(f"no tile for {n} among {prefs}")


def _position():
    return lax.axis_index("x"), lax.axis_index("y"), lax.axis_index("c")


class _Ride:
    def __init__(self, kind, x, cols=None, into=None):
        self.kind, self.x, self.cols, self.into = kind, x, cols, into
        r, c = x.shape[-2:]
        self.out_shape = jax.ShapeDtypeStruct((N_DEV, r, c if cols is None else cols[1]), x.dtype)
        self.width = c

    def _copies(self, x_ref, out_ref, send_sems, recv_sems, local_sem):
        px, py, pc = _position()
        me = 4 * px + 2 * py + pc
        src = (lambda slot: x_ref) if self.kind == "gather" else (lambda slot: x_ref.at[slot])
        dst = ((lambda slot: out_ref.at[slot]) if self.cols is None
               else (lambda slot: out_ref.at[slot, :, pl.ds(self.cols[0], self.width)]))
        mine = pltpu.make_async_copy(src(me), dst(me), local_sem)
        sends, recvs = [], []
        for k, (fx, fy, fc) in enumerate([(1, 0, 0), (0, 1, 0), (1, 1, 0), (1, 0, 1), (0, 1, 1), (1, 1, 1), (0, 0, 1)]):
            qx, qy, qc = (1 - px if fx else px), (1 - py if fy else py), (1 - pc if fc else pc)
            peer = 4 * qx + 2 * qy + qc
            sends.append(pltpu.make_async_remote_copy(
                src_ref=src(peer), dst_ref=dst(me), send_sem=send_sems.at[k], recv_sem=recv_sems.at[k],
                device_id=(qx, qy, qc), device_id_type=MESH))
            recvs.append(pltpu.make_async_remote_copy(
                src_ref=src(me), dst_ref=dst(peer), send_sem=send_sems.at[k], recv_sem=recv_sems.at[k],
                device_id=(qx, qy, qc), device_id_type=MESH))
        return mine, sends, recvs

    def start(self, *refs):
        mine, sends, _ = self._copies(*refs)
        mine.start()
        for cp in sends:
            cp.start()

    def wait(self, *refs):
        mine, sends, recvs = self._copies(*refs)
        for cp in recvs:
            cp.wait_recv()
        for cp in sends:
            cp.wait_send()
        mine.wait()

    def operands(self):
        return [self.x] + ([self.into] if self.into is not None else [])


_RIDE_SCRATCH = [pltpu.SemaphoreType.DMA((7,)), pltpu.SemaphoreType.DMA((7,)), pltpu.SemaphoreType.DMA]
_ANY = pl.BlockSpec(memory_space=pl.ANY)


def _mm(a, b, mode, out_dtype, tm, tn, tk, name, ride=None, b_cols=None, m_out=None):
    if mode == "nn":
        (m, k), (k2, n) = a.shape, b.shape
    elif mode == "nt":
        (m, k), (n, k2) = a.shape, b.shape
    else:
        (k, m), (k2, n) = a.shape, b.shape
    off, n = (0, n) if b_cols is None else b_cols
    assert k == k2 and m % tm == 0 and n % tn == 0 and k % tk == 0, (a.shape, b.shape, mode, tm, tn, tk)
    assert b_cols is None or (mode != "nt" and off % LANE == 0)
    nk = k // tk
    dims = {"nn": ((1,), (0,)), "nt": ((1,), (1,)), "tn": ((0,), (0,))}[mode]
    a_spec = (pl.BlockSpec((tk, tm), lambda j, i, kk: (kk, i)) if mode == "tn"
              else pl.BlockSpec((tm, tk), lambda j, i, kk: (i, kk)))
    if b_cols is not None:
        b_spec = pl.BlockSpec((pl.Element(tk), pl.Element(tn)),
                              lambda j, i, kk: (pl.multiple_of(kk * tk, LANE), pl.multiple_of(off + j * tn, LANE)))
    elif mode == "nt":
        b_spec = pl.BlockSpec((tn, tk), lambda j, i, kk: (j, kk))
    else:
        b_spec = pl.BlockSpec((tk, tn), lambda j, i, kk: (kk, j))

    grid = (n // tn, m // tm, nk)
    n_ride_in = len(ride.operands()) if ride is not None else 0

    def body(a_ref, b_ref, *rest):
        if ride is not None:
            x_ref = rest[0]
            o_ref, got_ref, acc_ref = rest[n_ride_in:n_ride_in + 3]
            comm = (x_ref, got_ref) + tuple(rest[n_ride_in + 3:])
        else:
            o_ref, acc_ref = rest
        kk = pl.program_id(2)
        step = (pl.program_id(0) * grid[1] + pl.program_id(1)) * nk + kk
        if ride is not None:
            @pl.when(step == 0)
            def _():
                ride.start(*comm)

        part = lax.dot_general(a_ref[...], b_ref[...], (dims, ((), ())), preferred_element_type=F32)
        if nk == 1:
            o_ref[...] = part.astype(o_ref.dtype)
        else:
            @pl.when(kk == 0)
            def _():
                acc_ref[...] = part

            @pl.when(jnp.logical_and(kk > 0, kk < nk - 1))
            def _():
                acc_ref[...] += part

            @pl.when(kk == nk - 1)
            def _():
                o_ref[...] = (acc_ref[...] + part).astype(o_ref.dtype)

        if ride is not None:
            @pl.when(step == grid[0] * grid[1] * nk - 1)
            def _():
                ride.wait(*comm)

    osz = jnp.dtype(out_dtype).itemsize
    need = 2 * (tm * tk * a.dtype.itemsize + tk * tn * b.dtype.itemsize + tm * tn * osz) + tm * tn * 4
    o_spec = pl.BlockSpec((tm, tn), lambda j, i, kk: (i, j))
    o_shape = jax.ShapeDtypeStruct((m if m_out is None else m_out, n), out_dtype)
    extra = ride is not None
    return pl.pallas_call(
        body, name=name, grid=grid,
        in_specs=[a_spec, b_spec] + [_ANY] * n_ride_in,
        out_specs=[o_spec, _ANY] if extra else o_spec,
        out_shape=[o_shape, ride.out_shape] if extra else o_shape,
        scratch_shapes=[pltpu.VMEM((tm, tn) if nk > 1 else (8, LANE), F32)] + (_RIDE_SCRATCH if extra else []),
        input_output_aliases={3: 1} if extra and ride.into is not None else {},
        compiler_params=_vmem(need + (12 << 20)),
    )(a, b, *(ride.operands() if extra else []))


ROWS = 256


def _ln_stats(x):
    mu = jnp.mean(x, axis=-1, keepdims=True)
    xc = x - mu
    var = jnp.mean(xc * xc, axis=-1, keepdims=True)
    rstd = lax.rsqrt(var + LN_EPS)
    return xc * rstd, rstd


def _token_specs(nbc, nbx, d):
    return [pl.BlockSpec((ROWS, d), lambda i: (jnp.minimum(i, nbc - 1), 0)),
            pl.BlockSpec((ROWS, d), lambda i: (jnp.maximum(i - nbc, 0), 0))]


def _tokens(c_ref, x_ref, nbc):
    return jnp.where(pl.program_id(0) < nbc, c_ref[...], x_ref[...])


def _modulate_fwd(ctx, x, modp):
    d = x.shape[1]
    nbc, nbx = ctx.shape[0] // ROWS, x.shape[0] // ROWS

    def body(c_ref, x_ref, mod_ref, o_ref):
        n, _ = _ln_stats(_tokens(c_ref, x_ref, nbc))
        o_ref[...] = (n * (1.0 + mod_ref[0, 1:2, :]) + mod_ref[0, 0:1, :]).astype(BF16)

    return pl.pallas_call(
        body, name="modulate_fwd", grid=(nbc + nbx,),
        in_specs=_token_specs(nbc, nbx, d) + [pl.BlockSpec((1, 3, d), lambda i: (jnp.where(i >= nbc, 1, 0), 0, 0))],
        out_specs=pl.BlockSpec((ROWS, d), lambda i: (i, 0)),
        out_shape=jax.ShapeDtypeStruct((ctx.shape[0] + x.shape[0], d), BF16),
    )(ctx, x, modp)


def _modulate_bwd(dh, ctx, x, modp, dxa):
    t, d = x.shape
    nbc, nbx = ctx.shape[0] // ROWS, t // ROWS

    def body(dh_ref, c_ref, x_ref, mod_ref, dxa_ref, gx_ref, sum_ref):
        i = pl.program_id(0)
        n, rstd = _ln_stats(_tokens(c_ref, x_ref, nbc))
        g = dh_ref[...]
        dn = g * (1.0 + mod_ref[0, 1:2, :])
        dx = rstd * (dn - jnp.mean(dn, axis=-1, keepdims=True) - n * jnp.mean(dn * n, axis=-1, keepdims=True))
        gx_ref[...] = dx + dxa_ref[...]
        dshift = jnp.sum(g, axis=0, keepdims=True)
        dscale = jnp.sum(g * n, axis=0, keepdims=True)

        @pl.when(i == 0)
        def _():
            sum_ref[...] = jnp.zeros_like(sum_ref)

        @pl.when(i < nbc)
        def _():
            sum_ref[0:1, :] += dshift
            sum_ref[1:2, :] += dscale

        @pl.when(i >= nbc)
        def _():
            sum_ref[2:3, :] += dshift
            sum_ref[3:4, :] += dscale

    lat = lambda i: (jnp.maximum(i - nbc, 0), 0)
    return pl.pallas_call(
        body, name="modulate_bwd", grid=(nbc + nbx,),
        in_specs=[pl.BlockSpec((ROWS, d), lambda i: (i, 0))] + _token_specs(nbc, nbx, d)
        + [pl.BlockSpec((1, 3, d), lambda i: (jnp.where(i >= nbc, 1, 0), 0, 0)), pl.BlockSpec((ROWS, d), lat)],
        out_specs=[pl.BlockSpec((ROWS, d), lat), pl.BlockSpec((8, d), lambda i: (0, 0))],
        out_shape=[jax.ShapeDtypeStruct((t, d), F32), jax.ShapeDtypeStruct((8, d), F32)],
    )(dh, ctx, x, modp, dxa)


def _post_fwd(o_f, o_b, h_f, h_b, u, hgw, mlw, nbc):
    tt = u.shape[0]
    t = tt - nbc * ROWS

    def body(of_ref, ob_ref, hf_ref, hb_ref, az_ref, bo_ref, bz_ref, hgw_ref, mlw_ref, y_ref):
        o = of_ref[...] + ob_ref[...]
        for sl in _head_slices(W_A, HG_HEADS):
            oh = o[:, sl]
            rs = lax.rsqrt(jnp.mean(oh * oh, axis=-1, keepdims=True) + NORM_EPS)
            y_ref[:, sl] = (oh * rs * hgw_ref[:, sl] * _silu(az_ref[:, sl])).astype(BF16)
        hm = hf_ref[...] + hb_ref[...]
        for sl in _head_slices(W_B, ML_HEADS):
            hh = hm[:, sl]
            mu = jnp.mean(hh, axis=-1, keepdims=True)
            hc = hh - mu
            rstd = lax.rsqrt(jnp.mean(hc * hc, axis=-1, keepdims=True) + NORM_EPS)
            out = hc * rstd * mlw_ref[:, sl] * _sigmoid(bo_ref[:, sl]) * _silu(bz_ref[:, sl])
            y_ref[:, W_A + sl.start:W_A + sl.stop] = out.astype(BF16)

    row = lambda i: (i + nbc, 0)
    seg = lambda s: pl.BlockSpec((ROWS, 1024), lambda i: (i + nbc, s))
    wspec = pl.BlockSpec((1, 1024), lambda i: (0, 0))
    return pl.pallas_call(
        body, name="post_fwd", grid=(t // ROWS,),
        in_specs=[pl.BlockSpec((ROWS, 1024), row)] * 4 + [seg(SEG_AZ), seg(SEG_BO), seg(SEG_BZ), wspec, wspec],
        out_specs=pl.BlockSpec((ROWS, 2048), lambda i: (i, 0)),
        out_shape=jax.ShapeDtypeStruct((t, 2048), BF16),
    )(o_f, o_b, h_f, h_b, u, u, u, hgw, mlw)


def _post_bwd(dz, w_o, o_f, o_b, h_f, h_b, u, hgw, mlw, nbc):
    tt = u.shape[0]
    d = w_o.shape[0]

    def body(dz_ref, w_ref, of_ref, ob_ref, hf_ref, hb_ref, az_ref, bo_ref, bz_ref, hgw_ref, mlw_ref,
             do_ref, dhm_ref, daz_ref, dbo_ref, sum_ref):
        i = pl.program_id(0)
        live = jnp.where(i >= nbc, 1.0, 0.0)
        dy = lax.dot_general(dz_ref[...], w_ref[...], (((1,), (1,)), ((), ())), preferred_element_type=F32) * live

        @pl.when(i == 0)
        def _():
            sum_ref[...] = jnp.zeros_like(sum_ref)

        o = of_ref[...] + ob_ref[...]
        for sl in _head_slices(W_A, HG_HEADS):
            oh = o[:, sl]
            rs = lax.rsqrt(jnp.mean(oh * oh, axis=-1, keepdims=True) + NORM_EPS)
            on = oh * rs
            az = az_ref[:, sl]
            dya = dy[:, sl]
            saz, daz = _silu_both(az)
            doa = dya * saz
            daz_ref[:, sl] = (dya * on * hgw_ref[:, sl] * daz).astype(BF16)
            sum_ref[0:1, sl] += jnp.sum(doa * on, axis=0, keepdims=True)
            don = doa * hgw_ref[:, sl]
            do_ref[:, sl] = rs * (don - on * jnp.mean(don * on, axis=-1, keepdims=True))
        hm = hf_ref[...] + hb_ref[...]
        for sl in _head_slices(W_B, ML_HEADS):
            hh = hm[:, sl]
            mu = jnp.mean(hh, axis=-1, keepdims=True)
            hc = hh - mu
            rstd = lax.rsqrt(jnp.mean(hc * hc, axis=-1, keepdims=True) + NORM_EPS)
            hn = hc * rstd
            hw = hn * mlw_ref[:, sl]
            bo, bz = bo_ref[:, sl], bz_ref[:, sl]
            sbo = _sigmoid(bo)
            sbz, dbz = _silu_both(bz)
            dyb = dy[:, W_A + sl.start:W_A + sl.stop]
            dhw = dyb * sbo * sbz
            dbo_ref[:, sl] = (dyb * hw * sbz * sbo * (1.0 - sbo)).astype(BF16)
            dbo_ref[:, 1024 + sl.start:1024 + sl.stop] = (dyb * hw * sbo * dbz).astype(BF16)
            sum_ref[1:2, sl] += jnp.sum(dhw * hn, axis=0, keepdims=True)
            dhn = dhw * mlw_ref[:, sl]
            dhm_ref[:, sl] = rstd * (dhn - jnp.mean(dhn, axis=-1, keepdims=True)
                                     - hn * jnp.mean(dhn * hn, axis=-1, keepdims=True))

    row = lambda i: (i, 0)
    seg = lambda s: pl.BlockSpec((ROWS, 1024), lambda i: (i, s))
    wspec = pl.BlockSpec((1, 1024), lambda i: (0, 0))
    return pl.pallas_call(
        body, name="post_bwd", grid=(tt // ROWS,),
        in_specs=[pl.BlockSpec((ROWS, 2048), lambda i: (jnp.maximum(i - nbc, 0), 0)), pl.BlockSpec((d, d), lambda i: (0, 0))]
        + [pl.BlockSpec((ROWS, 1024), row)] * 4 + [seg(SEG_AZ), seg(SEG_BO), seg(SEG_BZ), wspec, wspec],
        out_specs=[pl.BlockSpec((ROWS, 1024), row), pl.BlockSpec((ROWS, 1024), row),
                   pl.BlockSpec((ROWS, 1024), row), pl.BlockSpec((ROWS, 2048), row),
                   pl.BlockSpec((8, 1024), lambda i: (0, 0))],
        out_shape=[jax.ShapeDtypeStruct((tt, 1024), F32), jax.ShapeDtypeStruct((tt, 1024), F32),
                   jax.ShapeDtypeStruct((tt, 1024), BF16), jax.ShapeDtypeStruct((tt, 2048), BF16),
                   jax.ShapeDtypeStruct((8, 1024), F32)],
        compiler_params=_vmem(4 * d * d + 30 * ROWS * 2048 * 4),
    )(dz, w_o, o_f, o_b, h_f, h_b, u, u, u, hgw, mlw)


def _final(y, w_o, x, target, modp, ln_g, ln_b):
    t, d = x.shape

    def body(y_ref, w_ref, x_ref, tg_ref, mod_ref, g_ref, b_ref, dz_ref, dxa_ref, sum_ref):
        i = pl.program_id(0)
        zz = lax.dot_general(y_ref[...], w_ref[...], (((1,), (0,)), ((), ())), preferred_element_type=F32)
        gate = mod_ref[0, 2:3, :]
        pre = ALPHA * x_ref[...] + gate * zz
        nh, rstd = _ln_stats(pre)
        err = nh * g_ref[...] + b_ref[...] - tg_ref[...]
        dxo = err * (1.0 / d)
        dnh = dxo * g_ref[...]
        dpre = rstd * (dnh - jnp.mean(dnh, axis=-1, keepdims=True) - nh * jnp.mean(dnh * nh, axis=-1, keepdims=True))
        dz_ref[...] = (gate * dpre).astype(BF16)
        dxa_ref[...] = ALPHA * dpre

        @pl.when(i == 0)
        def _():
            sum_ref[...] = jnp.zeros_like(sum_ref)

        sum_ref[0:1, :] += jnp.sum(dpre * zz, axis=0, keepdims=True)
        sum_ref[1:2, :] += jnp.sum(dxo * nh, axis=0, keepdims=True)
        sum_ref[2:3, :] += jnp.sum(dxo, axis=0, keepdims=True)
        sum_ref[3:4, :] += jnp.sum(err * err, axis=0, keepdims=True)

    row = lambda i: (i, 0)
    vec = pl.BlockSpec((1, d), lambda i: (0, 0))
    return pl.pallas_call(
        body, name="final_ln_loss", grid=(t // ROWS,),
        in_specs=[pl.BlockSpec((ROWS, d), row), pl.BlockSpec((d, d), lambda i: (0, 0)), pl.BlockSpec((ROWS, d), row),
                  pl.BlockSpec((ROWS, d), row), pl.BlockSpec((1, 3, d), lambda i: (1, 0, 0)), vec, vec],
        out_specs=[pl.BlockSpec((ROWS, d), row), pl.BlockSpec((ROWS, d), row), pl.BlockSpec((8, d), lambda i: (0, 0))],
        out_shape=[jax.ShapeDtypeStruct((t, d), BF16), jax.ShapeDtypeStruct((t, d), F32),
                   jax.ShapeDtypeStruct((8, d), F32)],
        compiler_params=_vmem(4 * d * d + 24 * ROWS * d * 4),
    )(y, w_o, x, target, modp, ln_g, ln_b)


GRID_W = 64


def _shift(x, s, ok):
    n = x.shape[0]
    return jnp.where(ok, pltpu.roll(x, s % n, 0), 0.0)


def _grid_masks(n):
    t = lax.broadcasted_iota(jnp.int32, (n, LANE), 0)
    col = t & (GRID_W - 1)
    return dict(left=col >= 1, right=col <= GRID_W - 2, up=t >= GRID_W, down=t < n - GRID_W)


def _seq_masks(n):
    t = lax.broadcasted_iota(jnp.int32, (n, LANE), 0)
    return dict(left=t >= 1, right=t <= n - 2)


def _conv_fwd(u, w9, cb, tc):
    tt = u.shape[0]
    t = tt - tc

    def body(u_ref, w_ref, b_ref, o_ref):
        w = [w_ref[r:r + 1, :] for r in range(9)]
        xc = u_ref[0:tc, :]
        ms = _seq_masks(tc)
        o_ref[0:tc, :] = (w[3] * _shift(xc, 1, ms["left"]) + w[4] * xc + w[5] * _shift(xc, -1, ms["right"])
                          + b_ref[...])
        x = u_ref[tc:tt, :]
        mg = _grid_masks(t)
        taps = (_shift(x, 1, mg["left"]), x, _shift(x, -1, mg["right"]))
        rows = [w[3 * i] * taps[0] + w[3 * i + 1] * taps[1] + w[3 * i + 2] * taps[2] for i in range(3)]
        o_ref[tc:tt, :] = (rows[1] + _shift(rows[0], GRID_W, mg["up"]) + _shift(rows[2], -GRID_W, mg["down"])
                           + b_ref[...])

    return pl.pallas_call(
        body, name="conv_fwd", grid=(2048 // LANE,),
        in_specs=[pl.BlockSpec((tt, LANE), lambda j: (0, BLK_QK + j)), pl.BlockSpec((9, LANE), lambda j: (0, j)),
                  pl.BlockSpec((1, LANE), lambda j: (0, j))],
        out_specs=pl.BlockSpec((tt, LANE), lambda j: (0, j)),
        out_shape=jax.ShapeDtypeStruct((tt, 2048), F32),
        compiler_params=_vmem(40 * tt * LANE * 4),
    )(u, w9, cb)


def _conv_bwd(dcp, u, w9, tc, du):
    tt = u.shape[0]
    t = tt - tc

    def body(d_ref, u_ref, w_ref, du_in_ref, du_ref, gw_ref, gb_ref):
        w = [w_ref[r:r + 1, :] for r in range(9)]
        csum = lambda a: jnp.sum(a, axis=0, keepdims=True)
        dc = d_ref[0:tc, :]
        xc = u_ref[0:tc, :]
        ms = _seq_masks(tc)
        du_ref[0:tc, :] = (w[3] * _shift(dc, -1, ms["right"]) + w[4] * dc + w[5] * _shift(dc, 1, ms["left"])).astype(BF16)
        gmid = [csum(dc * _shift(xc, 1, ms["left"])), csum(dc * xc), csum(dc * _shift(xc, -1, ms["right"]))]
        d = d_ref[tc:tt, :]
        x = u_ref[tc:tt, :]
        mg = _grid_masks(t)
        dtaps = (_shift(d, -1, mg["right"]), d, _shift(d, 1, mg["left"]))
        rows = [w[3 * i] * dtaps[0] + w[3 * i + 1] * dtaps[1] + w[3 * i + 2] * dtaps[2] for i in range(3)]
        du_ref[tc:tt, :] = (rows[1] + _shift(rows[0], -GRID_W, mg["down"]) + _shift(rows[2], GRID_W, mg["up"])).astype(BF16)
        xtaps = (_shift(x, 1, mg["left"]), x, _shift(x, -1, mg["right"]))
        for j in range(3):
            gw_ref[j:j + 1, :] = csum(d * _shift(xtaps[j], GRID_W, mg["up"]))
            gw_ref[3 + j:4 + j, :] = csum(d * xtaps[j]) + gmid[j]
            gw_ref[6 + j:7 + j, :] = csum(d * _shift(xtaps[j], -GRID_W, mg["down"]))
        gb_ref[...] = csum(d) + csum(dc)

    return pl.pallas_call(
        body, name="conv_bwd", grid=(2048 // LANE,),
        in_specs=[pl.BlockSpec((tt, LANE), lambda j: (0, j)), pl.BlockSpec((tt, LANE), lambda j: (0, BLK_QK + j)),
                  pl.BlockSpec((9, LANE), lambda j: (0, j)), _ANY],
        out_specs=[pl.BlockSpec((tt, LANE), lambda j: (0, BLK_QK + j)), pl.BlockSpec((9, LANE), lambda j: (0, j)),
                   pl.BlockSpec((1, LANE), lambda j: (0, j))],
        out_shape=[jax.ShapeDtypeStruct(du.shape, BF16), jax.ShapeDtypeStruct((9, 2048), F32),
                   jax.ShapeDtypeStruct((1, 2048), F32)],
        input_output_aliases={3: 0},
        compiler_params=_vmem(48 * tt * LANE * 4),
    )(dcp, u, w9, du)


SUB = 2
STEP = SUB * CHUNK


def _chunk_of(pos, ncc, nc, rev):
    if not rev:
        return pos
    return jnp.where(pos < ncc, ncc - 1 - pos, nc - 1 - (pos - ncc))


def _sub_rows(rev):
    order = range(SUB - 1, -1, -1) if rev else range(SUB)
    return [(s, slice(s * CHUNK, (s + 1) * CHUNK)) for s in order]


def _hgrn_fwd(u, lower_d, ncc, rev):
    tt = u.shape[0]
    nc, ncc = tt // STEP, ncc // SUB
    seg_f = SEG_AFB if rev else SEG_AFF

    def body(zq_ref, zf_ref, v_ref, lb_ref, o_ref, hist_ref, st_ref):
        @pl.when(pl.program_id(0) == 0)
        def _():
            st_ref[...] = jnp.zeros_like(st_ref)

        st = st_ref[...]
        for s, r in _sub_rows(rev):
            hist_ref[s] = st
            o, st = hg_chunk_fwd(zq_ref[r, :], zf_ref[r, :], v_ref[r, :], lb_ref[...], st, rev)
            o_ref[r, :] = o
        st_ref[...] = st

    seg = lambda s: pl.BlockSpec((STEP, 1024), lambda j: (_chunk_of(j, ncc, nc, rev), s))
    return pl.pallas_call(
        body, name="hgrn_fwd_rev" if rev else "hgrn_fwd", grid=(nc,),
        in_specs=[seg(SEG_AQ), seg(seg_f), seg(SEG_AI), pl.BlockSpec((1, 1024), lambda j: (0, 0))],
        out_specs=[pl.BlockSpec((STEP, 1024), lambda j: (_chunk_of(j, ncc, nc, rev), 0)),
                   pl.BlockSpec((SUB, 1024, HG_D), lambda j: (_chunk_of(j, ncc, nc, rev), 0, 0))],
        out_shape=[jax.ShapeDtypeStruct((tt, 1024), F32), jax.ShapeDtypeStruct((nc * SUB, 1024, HG_D), F32)],
        scratch_shapes=[pltpu.VMEM((1024, HG_D), F32)],
    )(u, u, u, lower_d)


def _hgrn_bwd(u, lower_d, hist, do, ncc, rev, ride=None, final=None):
    tt = u.shape[0]
    nc, ncc = tt // STEP, ncc // SUB
    seg_f = SEG_AFB if rev else SEG_AFF
    is_final = final is not None
    has_a2a = ride is not None
    n_out = 2 if is_final else 4
    width = 5 * 1024

    def body(zq_ref, zf_ref, v_ref, lb_ref, hist_ref, do_ref, *rest):
        if is_final:
            aq_ref, av_ref, af_ref, az_ref = rest[:4]
            rest = rest[4:]
        if has_a2a:
            x_ref, rest = rest[0], rest[1:]
        outs, rest = rest[:n_out], rest[n_out:]
        dlb_ref = outs[-1]
        if has_a2a:
            comm = (x_ref, rest[0]) + tuple(rest[2:])
            dst_ref = rest[1]
        else:
            dst_ref = rest[0]

        @pl.when(pl.program_id(0) == 0)
        def _():
            dst_ref[...] = jnp.zeros_like(dst_ref)
            dlb_ref[...] = jnp.zeros_like(dlb_ref)
            if has_a2a:
                ride.start(*comm)

        dst = dst_ref[...]
        dlb_sum = dlb_ref[...]
        for s, r in reversed(_sub_rows(rev)):
            dzq, dzf, dv, dlb, dst = hg_chunk_bwd(zq_ref[r, :], zf_ref[r, :], v_ref[r, :], lb_ref[...],
                                                  hist_ref[s], do_ref[r, :], dst, rev)
            dlb_sum = dlb_sum + dlb
            if is_final:
                du_ref = outs[0]
                dzf_own, dzf_other = dzf.astype(BF16), af_ref[r, :]
                du_ref[r, 0:1024] = (dzq + aq_ref[r, :]).astype(BF16)
                du_ref[r, 1024:2048] = dzf_other if rev else dzf_own
                du_ref[r, 2048:3072] = dzf_own if rev else dzf_other
                du_ref[r, 3072:4096] = (dv + av_ref[r, :]).astype(BF16)
                du_ref[r, 4096:5120] = az_ref[r, :]
            else:
                dzf_ref, dzq_ref, dv_ref = outs[:3]
                dzf_ref[r, :] = dzf.astype(BF16)
                dzq_ref[r, :] = dzq
                dv_ref[r, :] = dv
        dst_ref[...] = dst
        dlb_ref[...] = dlb_sum

        if has_a2a:
            @pl.when(pl.program_id(0) == nc - 1)
            def _():
                ride.wait(*comm)

    cidx = lambda j: _chunk_of(nc - 1 - j, ncc, nc, rev)
    seg = lambda s: pl.BlockSpec((STEP, 1024), lambda j: (cidx(j), s))
    row = pl.BlockSpec((STEP, 1024), lambda j: (cidx(j), 0))
    dlb_spec = pl.BlockSpec((1, 1024), lambda j: (0, 0))
    dlb_shape = jax.ShapeDtypeStruct((1, 1024), F32)
    if is_final:
        out_specs = [pl.BlockSpec((STEP, width), lambda j: (cidx(j), 0)), dlb_spec]
        out_shape = [jax.ShapeDtypeStruct((tt, N_U), BF16), dlb_shape]
    else:
        out_specs = [row, row, row, dlb_spec]
        out_shape = [jax.ShapeDtypeStruct((tt, 1024), BF16), jax.ShapeDtypeStruct((tt, 1024), F32),
                     jax.ShapeDtypeStruct((tt, 1024), F32), dlb_shape]
    ins = [u, u, u, lower_d, hist, do] + (list(final) if is_final else []) + ([ride.x] if has_a2a else [])
    return pl.pallas_call(
        body, name="hgrn_bwd_rev" if rev else "hgrn_bwd", grid=(nc,),
        in_specs=[seg(SEG_AQ), seg(seg_f), seg(SEG_AI), pl.BlockSpec((1, 1024), lambda j: (0, 0)),
                  pl.BlockSpec((SUB, 1024, HG_D), lambda j: (cidx(j), 0, 0)), row] + ([row] * 4 if is_final else [])
        + ([_ANY] if has_a2a else []),
        out_specs=out_specs + ([_ANY] if has_a2a else []),
        out_shape=out_shape + ([ride.out_shape] if has_a2a else []),
        scratch_shapes=[pltpu.VMEM((1024, HG_D), F32)] + (_RIDE_SCRATCH if has_a2a else []),
    )(*ins)


def _gate_views(g_ref, b_ref, r, head, rev):
    gc = g_ref[r, :] + b_ref[...]
    lane = lax.broadcasted_iota(jnp.int32, (1, LANE), 1)
    eye = _eye()
    d = 1 if rev else 0
    ii, fi = d * ML_HEADS + head, 2 * ML_HEADS + d * ML_HEADS + head
    col = lambda idx: jnp.sum(jnp.where(lane == idx, gc, 0.0), axis=1, keepdims=True)
    row = lambda c: jnp.sum(eye * c, axis=0, keepdims=True)
    gi, gf = col(ii), col(fi)
    return gi, row(gi), gf, row(gf)


def _mlstm_fwd(cpre, u, bias, ncc, rev):
    tt = u.shape[0]
    nc, ncc = tt // STEP, ncc // SUB
    nhd = ML_HEADS

    def body(q_ref, k_ref, v_ref, g_ref, b_ref, h_ref, ch_ref, nh_ref, mh_ref, c_ref, n_ref, m_ref):
        @pl.when(pl.program_id(0) == 0)
        def _():
            c_ref[...] = jnp.zeros_like(c_ref)
            n_ref[...] = jnp.zeros_like(n_ref)
            m_ref[...] = jnp.zeros_like(m_ref)

        c, n_all, m_all = c_ref[...], n_ref[...], m_ref[...]
        n = [n_all[hd:hd + 1, :] for hd in range(nhd)]
        m = [m_all[hd:hd + 1, 0:1] for hd in range(nhd)]
        for s, r in _sub_rows(rev):
            ch_ref[s] = c
            for hd in range(nhd):
                nh_ref[s, hd:hd + 1, :] = n[hd]
                mh_ref[s, hd:hd + 1, :] = jnp.broadcast_to(m[hd], (1, LANE))
            gates = [_gate_views(g_ref, b_ref, r, hd, rev) for hd in range(nhd)]
            h, c, n, m = ml_chunk_fwd(q_ref[r, :], k_ref[r, :], v_ref[r, :], gates, c, n, m, rev)
            h_ref[r, :] = h
        c_ref[...] = c
        for hd in range(nhd):
            n_ref[hd:hd + 1, :] = n[hd]
            m_ref[hd:hd + 1, :] = jnp.broadcast_to(m[hd], (1, LANE))

    cidx = lambda j: _chunk_of(j, ncc, nc, rev)
    row = lambda s: pl.BlockSpec((STEP, 1024), lambda j: (cidx(j), s))
    st3 = lambda a, b: pl.BlockSpec((SUB, a, b), lambda j: (cidx(j), 0, 0))
    return pl.pallas_call(
        body, name="mlstm_fwd_rev" if rev else "mlstm_fwd", grid=(nc,),
        in_specs=[row(0), row(1), row(SEG_BV), pl.BlockSpec((STEP, LANE), lambda j: (cidx(j), BLK_GATE)),
                  pl.BlockSpec((1, LANE), lambda j: (0, 0))],
        out_specs=[row(0), st3(1024, ML_D), st3(8, ML_D), st3(8, LANE)],
        out_shape=[jax.ShapeDtypeStruct((tt, 1024), F32), jax.ShapeDtypeStruct((nc * SUB, 1024, ML_D), F32),
                   jax.ShapeDtypeStruct((nc * SUB, 8, ML_D), F32), jax.ShapeDtypeStruct((nc * SUB, 8, LANE), F32)],
        scratch_shapes=[pltpu.VMEM((1024, ML_D), F32), pltpu.VMEM((8, ML_D), F32), pltpu.VMEM((8, LANE), F32)],
    )(cpre, cpre, u, u, bias)


def _mlstm_bwd(cpre, u, bias, chist, nhist, mhist, dh, ncc, rev, final=None):
    tt = u.shape[0]
    nc, ncc = tt // STEP, ncc // SUB
    nhd = ML_HEADS
    is_final = final is not None
    d = 1 if rev else 0
    col0, width = SEG_BV * 1024, N_U - SEG_BV * 1024

    def body(q_ref, k_ref, v_ref, g_ref, b_ref, ch_ref, nh_ref, mh_ref, dh_ref, *rest):
        if is_final:
            aqk_ref, av_ref, ag_ref, bo_ref = rest[:4]
            dqk_ref, du_ref, gs_ref, dc_ref, dn_ref = rest[5:]
        else:
            dqk_ref, dv_ref, dg_ref, gs_ref, dc_ref, dn_ref = rest

        @pl.when(pl.program_id(0) == 0)
        def _():
            dc_ref[...] = jnp.zeros_like(dc_ref)
            dn_ref[...] = jnp.zeros_like(dn_ref)
            gs_ref[...] = jnp.zeros_like(gs_ref)

        lane = lax.broadcasted_iota(jnp.int32, (1, LANE), 1)
        dc, dn_all, gs = dc_ref[...], dn_ref[...], gs_ref[...]
        dn = [dn_all[hd:hd + 1, :] for hd in range(nhd)]
        for s, r in reversed(_sub_rows(rev)):
            gates = [_gate_views(g_ref, b_ref, r, hd, rev) for hd in range(nhd)]
            n_all, m_all = nh_ref[s], mh_ref[s]
            dqp, dkp, dv, dgi, dgf, dc, dn = ml_chunk_bwd(
                q_ref[r, :], k_ref[r, :], v_ref[r, :], gates, ch_ref[s],
                [n_all[hd:hd + 1, :] for hd in range(nhd)], [m_all[hd:hd + 1, 0:1] for hd in range(nhd)],
                dh_ref[r, :], dc, dn, rev)
            dg = ag_ref[r, :] if is_final else jnp.zeros((CHUNK, LANE), F32)
            for hd in range(nhd):
                dg = dg + jnp.where(lane == d * ML_HEADS + hd, dgi[hd], 0.0)
                dg = dg + jnp.where(lane == 2 * ML_HEADS + d * ML_HEADS + hd, dgf[hd], 0.0)
            if is_final:
                dqp = dqp + aqk_ref[r, 0:W_B]
                dkp = dkp + aqk_ref[r, W_B:2 * W_B]
                du_ref[r, 0:1024] = (dv + av_ref[r, :]).astype(BF16)
                du_ref[r, 1024:3072] = bo_ref[r, :]
                du_ref[r, 3072:3072 + LANE] = dg.astype(BF16)
            else:
                dv_ref[r, :] = dv
                dg_ref[r, :] = dg
            dqk_ref[r, 0:W_B] = dqp
            dqk_ref[r, W_B:2 * W_B] = dkp
            gs = gs + jnp.sum(dg, axis=0, keepdims=True)
        dc_ref[...] = dc
        gs_ref[...] = gs
        for hd in range(nhd):
            dn_ref[hd:hd + 1, :] = dn[hd]

    cidx = lambda j: _chunk_of(nc - 1 - j, ncc, nc, rev)
    row = lambda s: pl.BlockSpec((STEP, 1024), lambda j: (cidx(j), s))
    wide = pl.BlockSpec((STEP, 2048), lambda j: (cidx(j), 0))
    gate = pl.BlockSpec((STEP, LANE), lambda j: (cidx(j), 0))
    st3 = lambda a, b: pl.BlockSpec((SUB, a, b), lambda j: (cidx(j), 0, 0))
    gs_spec, gs_shape = pl.BlockSpec((1, LANE), lambda j: (0, 0)), jax.ShapeDtypeStruct((1, LANE), F32)
    dqk_shape = jax.ShapeDtypeStruct((tt, 2048), F32)
    ins = [cpre, cpre, u, u, bias, chist, nhist, mhist, dh] + (list(final) if is_final else [])
    if is_final:
        out_specs = [wide, pl.BlockSpec((pl.Element(STEP), pl.Element(width)), lambda j: (cidx(j) * STEP, col0)), gs_spec]
        out_shape = [dqk_shape, jax.ShapeDtypeStruct((tt, N_U), BF16), gs_shape]
    else:
        out_specs = [wide, row(0), gate, gs_spec]
        out_shape = [dqk_shape, jax.ShapeDtypeStruct((tt, 1024), F32), jax.ShapeDtypeStruct((tt, LANE), F32), gs_shape]
    return pl.pallas_call(
        body, name="mlstm_bwd_rev" if rev else "mlstm_bwd", grid=(nc,),
        in_specs=[row(0), row(1), row(SEG_BV), pl.BlockSpec((STEP, LANE), lambda j: (cidx(j), BLK_GATE)),
                  pl.BlockSpec((1, LANE), lambda j: (0, 0)),
                  st3(1024, ML_D), st3(8, ML_D), st3(8, LANE), row(0)] + ([wide, row(0), gate, wide, _ANY] if is_final else []),
        out_specs=out_specs, out_shape=out_shape,
        input_output_aliases={13: 1} if is_final else {},
        scratch_shapes=[pltpu.VMEM((1024, ML_D), F32), pltpu.VMEM((8, ML_D), F32)],
    )(*ins)


def _whole(body, out_shape, name, *args, nbytes=0):
    return pl.pallas_call(body, name=name, out_shape=out_shape, compiler_params=_vmem(nbytes))(*args)


def _mod_fwd(cs, w_cols, b_cols):
    def body(c_ref, w_ref, b_ref, o_ref):
        o_ref[...] = _exact_nn(_silu(c_ref[...]), w_ref[...]) + b_ref[...]

    return _whole(body, jax.ShapeDtypeStruct((16, w_cols.shape[1]), F32), "mod_fwd", cs, w_cols, b_cols,
                  nbytes=4 * w_cols.size * 4)


def _mod_bwd_w(cs, d9, w_cols):
    def body(c_ref, d_ref, w_ref, gw_ref, pc_ref):
        gw_ref[...] = _exact_tn(_silu(c_ref[...]), d_ref[...])
        pc = lax.dot_general(d_ref[8:16, :], w_ref[...], (((1,), (1,)), ((), ())), precision=lax.Precision.HIGHEST,
                             preferred_element_type=F32)
        row = lax.broadcasted_iota(jnp.int32, pc.shape, 0)
        pc_ref[...] = jnp.where(row == 0, pc, 0.0)

    return _whole(body, [jax.ShapeDtypeStruct(w_cols.shape, F32), jax.ShapeDtypeStruct((8, w_cols.shape[0]), F32)],
                  "mod_bwd_w", cs, d9, w_cols, nbytes=6 * w_cols.size * 4)


def _lower_fwd(lb4):
    def body(l_ref, o_ref):
        o_ref[...] = jnp.zeros_like(o_ref)
        o_ref[0:1, :] = 1.0 / (1.0 + jnp.exp(l_ref[1:2, :] - l_ref[0:1, :]))
        o_ref[1:2, :] = 1.0 / (1.0 + jnp.exp(l_ref[3:4, :] - l_ref[2:3, :]))

    return _whole(body, jax.ShapeDtypeStruct((8, lb4.shape[1]), F32), "lower_fwd", lb4)


def _reduce8(g, name):
    def body(g_ref, o_ref):
        acc = g_ref[0]
        for k in range(1, N_DEV):
            acc = acc + g_ref[k]
        o_ref[...] = acc

    return _whole(body, jax.ShapeDtypeStruct(g.shape[1:], F32), name, g, nbytes=4 * g.size * 4)


_PACK = (("dmodx", 48), ("dmodc", 48), ("gconvw", 144), ("gconvb", 16), ("dlower", 16), ("ghgw", 8), ("gmlw", 8),
         ("glng", 16), ("glnb", 16), ("losssq", 16), ("ggate", 8))
_PACK_ROWS = sum(r for _, r in _PACK)


def _pack_offsets():
    off, out = 0, {}
    for name, rows in _PACK:
        out[name] = (off, rows)
        off += rows
    return out


def _small_finish(total, p0, d_feat):
    offs = _pack_offsets()

    def body(t_ref, p_ref, gb_ref, a0_ref, a1_ref, loss_ref):
        ox, oc, ol, oq = offs["dmodx"][0], offs["dmodc"][0], offs["dlower"][0], offs["losssq"][0]
        gb_ref[...] = t_ref[ox:ox + 48, :] + t_ref[oc:oc + 48, :]
        p = p_ref[...]
        da0 = t_ref[ol:ol + 16, :] * p * (1.0 - p)
        a0_ref[...] = da0
        a1_ref[...] = -da0
        sq = t_ref[oq:oq + 16, :]
        tot = jnp.sum(jnp.sum(sq, axis=1, keepdims=True), axis=0, keepdims=True)
        loss_ref[...] = jnp.broadcast_to(tot * (0.5 / d_feat), loss_ref.shape)

    s = jax.ShapeDtypeStruct
    return _whole(body, [s((48, LANE), F32), s((16, LANE), F32), s((16, LANE), F32), s((8, LANE), F32)],
                  "small_finish", total, p0)


def _cctx_grad(parts, c_ctx8):
    def body(p_ref, c_ref, o_ref):
        acc = p_ref[0]
        for k in range(1, N_DEV):
            acc = acc + p_ref[k]
        o_ref[...] = acc * _dsilu(c_ref[...])

    return _whole(body, jax.ShapeDtypeStruct(c_ctx8.shape, F32), "cctx_grad", parts, c_ctx8)


def _adam_math(w, g, m, v):
    m = ADAM_B1 * m + (1.0 - ADAM_B1) * g
    v = ADAM_B2 * v + (1.0 - ADAM_B2) * (g * g)
    m_hat = m / (1.0 - ADAM_B1 ** ADAM_STEP)
    v_hat = v / (1.0 - ADAM_B2 ** ADAM_STEP)
    delta = -ADAM_LR * (m_hat / (jnp.sqrt(v_hat) + ADAM_EPS) + ADAM_WD * w)
    return delta, m, v


def _adamw(w, g, m, v, rows, name):
    r, c = w.shape

    def body(w_ref, g_ref, m_ref, v_ref, d_ref, mo_ref, vo_ref):
        d_ref[...], mo_ref[...], vo_ref[...] = _adam_math(w_ref[...], g_ref[...], m_ref[...], v_ref[...])

    spec = pl.BlockSpec((rows, c), lambda i: (i, 0))
    return pl.pallas_call(
        body, name=name, grid=(r // rows,), in_specs=[spec] * 4, out_specs=[spec] * 3,
        out_shape=[jax.ShapeDtypeStruct((r, c), F32)] * 3,
        compiler_params=_vmem(16 * rows * (c + LANE) * 4),
    )(w, g, m, v)


def _rs_adamw(recv, w, m, v, tile, name, by_cols=False):
    _, r, c = recv.shape

    def body(r_ref, w_ref, m_ref, v_ref, g_ref, d_ref, mo_ref, vo_ref):
        g = r_ref[0].astype(F32)
        for k in range(1, N_DEV):
            g = g + r_ref[k].astype(F32)
        g_ref[...] = g
        d_ref[...], mo_ref[...], vo_ref[...] = _adam_math(w_ref[...], g, m_ref[...], v_ref[...])

    if by_cols:
        spec = pl.BlockSpec((r, tile), lambda i: (0, i))
        rspec = pl.BlockSpec((N_DEV, r, tile), lambda i: (0, 0, i))
        steps, elems = c // tile, (r + 16) * tile
    else:
        spec = pl.BlockSpec((tile, c), lambda i: (i, 0))
        rspec = pl.BlockSpec((N_DEV, tile, c), lambda i: (0, i, 0))
        steps, elems = r // tile, tile * (c + LANE)
    return pl.pallas_call(
        body, name=name, grid=(steps,), in_specs=[rspec] + [spec] * 3, out_specs=[spec] * 4,
        out_shape=[jax.ShapeDtypeStruct((r, c), F32)] * 4,
        compiler_params=_vmem(2 * elems * (N_DEV * 2 + 7 * 4) + (4 << 20)),
    )(recv, w, m, v)


def _all_gather(x, name):
    r, c = x.shape

    def body(x_ref, out_ref, send_sems, recv_sems, local_sem):
        px, py, pc = _position()
        me, sibling = (px, py, pc), (px, py, 1 - pc)
        chips = [(1 - px, py), (px, 1 - py), (1 - px, 1 - py)]

        def slot(qx, qy, qc):
            return out_ref.at[4 * qx + 2 * qy + qc]

        def copy(k, block, to, src=None):
            return pltpu.make_async_remote_copy(
                src_ref=slot(*block) if src is None else src, dst_ref=slot(*block),
                send_sem=send_sems.at[k], recv_sem=recv_sems.at[k], device_id=to, device_id_type=MESH)

        mine = pltpu.make_async_copy(x_ref, slot(*me), local_sem)
        mine.start()
        first = [copy(1 + j, me, (*chip, pc), src=x_ref) for j, chip in enumerate(chips)]
        first.append(copy(0, me, sibling, src=x_ref))
        for cp in first:
            cp.start()
        passed = [copy(4 + j, (*chip, pc), sibling) for j, chip in enumerate(chips)]
        for j, chip in enumerate(chips):
            copy(1 + j, (*chip, pc), me).wait_recv()
            passed[j].start()
        copy(0, sibling, me).wait_recv()
        for j, chip in enumerate(chips):
            copy(4 + j, (*chip, 1 - pc), me).wait_recv()
        for cp in first + passed:
            cp.wait_send()
        mine.wait()

    return pl.pallas_call(
        body, name=name, out_shape=jax.ShapeDtypeStruct((N_DEV, r, c), x.dtype),
        in_specs=[pl.BlockSpec(memory_space=pl.ANY)], out_specs=pl.BlockSpec(memory_space=pl.ANY),
        scratch_shapes=[pltpu.SemaphoreType.DMA((7,)), pltpu.SemaphoreType.DMA((7,)), pltpu.SemaphoreType.DMA],
    )(x)


DW_PIECES = ((0, 256), (256, 640), (896, 1152))


def _local_step(ctx, x, target, modp, lower, wt_u, w_o, w9, conv_b, gate_b, hgw, mlw, ln_g, ln_b, exchange):
    tc = ctx.shape[0]
    tt = tc + x.shape[0]
    nbc, ncc = tc // ROWS, tc // CHUNK
    lower_f, lower_b = lower[0:1], lower[1:2]

    hc = _modulate_fwd(ctx, x, modp)
    tmh = _pick(tt, (1088, 768, 512, 256))
    if exchange:
        u, w_o = _mm(hc, wt_u, "nt", F32, tmh, 1152, D_MODEL, "mm_u", ride=_Ride("gather", w_o))
        w_o = w_o.reshape(D_MODEL, D_MODEL)
    else:
        u = _mm(hc, wt_u, "nt", F32, tmh, 1152, D_MODEL, "mm_u")
    cpre = _conv_fwd(u, w9, conv_b, tc)
    bias = jnp.pad(gate_b.reshape(1, 16), ((0, 0), (0, LANE - 16)))

    o_f, hist_f = _hgrn_fwd(u, lower_f, ncc, False)
    o_b, hist_b = _hgrn_fwd(u, lower_b, ncc, True)
    h_f, ch_f, nh_f, mh_f = _mlstm_fwd(cpre, u, bias, ncc, False)
    h_b, ch_b, nh_b, mh_b = _mlstm_fwd(cpre, u, bias, ncc, True)
    y = _post_fwd(o_f, o_b, h_f, h_b, u, hgw, mlw, nbc)
    dz, dxa, fsum = _final(y, w_o, x, target, modp, ln_g, ln_b)

    dw_o = _mm(y, dz, "tn", BF16, D_MODEL, 1024, _pick(y.shape[0], (512, 256)), "mm_dwo")
    do, dhm, daz, dbo, psum = _post_bwd(dz, w_o, o_f, o_b, h_f, h_b, u, hgw, mlw, nbc)
    if exchange:
        dzf_f, dzq, dv_a, dlb_f, dw_o = _hgrn_bwd(
            u, lower_f, hist_f, do, ncc, False, ride=_Ride("a2a", dw_o.reshape(N_DEV, D_MODEL // N_DEV, D_MODEL)))
    else:
        dzf_f, dzq, dv_a, dlb_f = _hgrn_bwd(u, lower_f, hist_f, do, ncc, False)
    du, dlb_b = _hgrn_bwd(u, lower_b, hist_b, do, ncc, True, final=(dzq, dv_a, dzf_f, daz))
    dqk, dv_m, dg, _ = _mlstm_bwd(cpre, u, bias, ch_f, nh_f, mh_f, dhm, ncc, False)
    dqk, du, gsum = _mlstm_bwd(cpre, u, bias, ch_b, nh_b, mh_b, dhm, ncc, True, final=(dqk, dv_m, dg, dbo, du))
    du, gconvw, gconvb = _conv_bwd(dqk, u, w9, tc, du)
    tkw = _pick(tt, (2176, 768, 512, 256))
    blocks = lambda g: g.reshape(N_DEV, N_IN // N_DEV, g.shape[1])
    dwu = lambda name, cols, ride: _mm(du, hc, "tn", BF16, 1152, cols[1], tkw, name, b_cols=cols, ride=ride, m_out=N_IN)
    dwt_a = dwu("mm_dwu_a", DW_PIECES[0], None)
    if exchange:
        whole = lambda piece, into: _Ride("a2a", blocks(piece[1]), cols=(piece[0][0], D_MODEL), into=into)
        dwt_b, got = dwu("mm_dwu_b", DW_PIECES[1], whole((DW_PIECES[0], dwt_a), None))
        dwt_c, got = dwu("mm_dwu_c", DW_PIECES[2], whole((DW_PIECES[1], dwt_b), got))
        dh, dwt_u = _mm(du, wt_u, "nn", F32, tmh, D_MODEL // 2, 3456, "mm_dh", ride=whole((DW_PIECES[2], dwt_c), got))
    else:
        dwt_u = jnp.concatenate([dwt_a, dwu("mm_dwu_b", DW_PIECES[1], None), dwu("mm_dwu_c", DW_PIECES[2], None)], axis=1)
        dh = _mm(du, wt_u, "nn", F32, tmh, D_MODEL // 2, 3456, "mm_dh")
    gx, msum = _modulate_bwd(dh, ctx, x, modp, dxa)

    zero_row = jnp.zeros((1, D_MODEL), F32)
    small = dict(
        dmodx=jnp.concatenate([msum[2:3], msum[3:4], fsum[0:1]], axis=0),
        dmodc=jnp.concatenate([msum[0:1], msum[1:2], zero_row], axis=0),
        gconvw=gconvw, gconvb=gconvb, dlower=jnp.concatenate([dlb_f, dlb_b], axis=0),
        ghgw=psum[0:1], gmlw=psum[1:2], glng=fsum[1:2], glnb=fsum[2:3], losssq=fsum[3:4],
        ggate=jnp.concatenate([gsum, jnp.zeros((7, LANE), F32)], axis=0))
    return gx, dwt_u, dw_o, small


def _pack_small(small):
    return jnp.concatenate([small[name].reshape(rows, LANE) for name, rows in _PACK], axis=0)


def _flat_pad(a, rows):
    flat = a.reshape(-1)
    return jnp.pad(flat, (0, rows * LANE - flat.shape[0])).reshape(rows, LANE)


def kernel(x, c, ctx, c_ctx, w_mod, b_mod, w_in, conv_w, conv_b, hg_lb, ml_gate_b, hg_norm_w, ml_norm_w, w_out, ln_g, ln_b, loss_target, m_c_ctx, m_w_mod, m_b_mod, m_w_in, m_conv_w, m_conv_b, m_hg_lb, m_ml_gate_b, m_hg_norm_w, m_ml_norm_w, m_w_out, m_ln_g, m_ln_b, v_c_ctx, v_w_mod, v_b_mod, v_w_in, v_conv_w, v_conv_b, v_hg_lb, v_ml_gate_b, v_hg_norm_w, v_ml_norm_w, v_w_out, v_ln_g, v_ln_b):
    px, py, pc = _position()
    me = 4 * px + 2 * py + pc
    d = D_MODEL
    n_mod = w_mod.shape[2]
    n_wi = w_in.shape[2]
    n_cv = conv_w.shape[3]
    n_lb = hg_lb.shape[2]

    pack0 = jnp.concatenate([c.reshape(-1), conv_w.reshape(-1), hg_lb.reshape(-1)]).reshape(1, -1)
    g0 = _all_gather(pack0, "gather_small_inputs")[:, 0, :]
    c_all = g0[:, :d]
    w9 = jnp.transpose(g0[:, d:d + 9 * n_cv].reshape(N_DEV, 9, n_cv), (1, 0, 2)).reshape(9, N_DEV * n_cv)
    lb4 = jnp.transpose(g0[:, d + 9 * n_cv:].reshape(N_DEV, 4, n_lb), (1, 0, 2)).reshape(4, N_DEV * n_lb)
    lower = _lower_fwd(lb4)

    cs = jnp.concatenate([c_all, c_ctx.reshape(1, d), jnp.zeros((7, d), F32)], axis=0)
    b_cols = lax.dynamic_slice(b_mod, (0, me * n_mod), (1, n_mod))
    slab = _mod_fwd(cs, w_mod[0], b_cols)
    mod_all = jnp.transpose(_all_gather(slab, "gather_mod"), (1, 0, 2)).reshape(16, N_DEV * n_mod)
    mod_x = lax.dynamic_slice(mod_all, (me, 0), (1, 3 * d)).reshape(3, d)
    modp = jnp.stack([mod_all[8].reshape(3, d), mod_x])

    wt = _all_gather(w_in[0].T.astype(BF16), "gather_w_in").reshape(N_DEV * n_wi, d)
    wt_u = jnp.pad(wt, ((0, N_U - N_DEV * n_wi), (0, 0)))

    gx, recv_wi, recv_wo, small = _local_step(ctx[0], x[0], loss_target[0], modp, lower, wt_u, w_out[0].astype(BF16),
                                              w9, conv_b, ml_gate_b[0], hg_norm_w, ml_norm_w, ln_g, ln_b, True)
    g_wi, d_wi, nm_wi, nv_wi = [a.T for a in _rs_adamw(recv_wi, w_in[0].T, m_w_in[0].T, v_w_in[0].T, 256,
                                                       "adamw_w_in", by_cols=True)]
    g_wo, d_wo, nm_wo, nv_wo = _rs_adamw(recv_wo, w_out[0], m_w_out[0], v_w_out[0], 64, "adamw_w_out")

    packs = _all_gather(_pack_small(small), "gather_small_grads")
    total = _reduce8(packs, "reduce_small_grads")
    offs = _pack_offsets()
    piece = lambda name: total[offs[name][0]:offs[name][0] + offs[name][1]]
    g_bmod, g_lb0, g_lb1, loss8 = _small_finish(total, lower[0:2].reshape(16, LANE), float(d))

    ox = offs["dmodx"][0]
    dmodx_all = packs[:, ox:ox + 48, :].reshape(N_DEV, 3 * d)
    dmodc_tot = piece("dmodc").reshape(1, 3 * d)
    d9 = jnp.concatenate([dmodx_all, dmodc_tot, jnp.zeros((7, 3 * d), F32)], axis=0)
    d9_cols = lax.dynamic_slice(d9, (0, me * n_mod), (16, n_mod))
    g_wmod, pc_part = _mod_bwd_w(cs, d9_cols, w_mod[0])
    c_ctx8 = jnp.concatenate([c_ctx.reshape(1, d), jnp.zeros((7, d), F32)], axis=0)
    g_cctx = _cctx_grad(_all_gather(pc_part, "gather_cctx"), c_ctx8)[0]
    d_wmod, nm_wmod, nv_wmod = _adamw(w_mod[0], g_wmod, m_w_mod[0], v_w_mod[0], 256, "adamw_w_mod")

    g_convw_full = piece("gconvw").reshape(9, d)
    g_convw = lax.dynamic_slice(g_convw_full, (0, me * n_cv), (9, n_cv)).reshape(conv_w.shape)
    lb_full = jnp.stack([jnp.stack([g_lb0[0:8].reshape(-1), g_lb1[0:8].reshape(-1)]),
                         jnp.stack([g_lb0[8:16].reshape(-1), g_lb1[8:16].reshape(-1)])])
    g_hglb = lax.dynamic_slice(lb_full, (0, 0, me * n_lb), (2, 2, n_lb))
    grads = dict(
        c_ctx=g_cctx, b_mod=g_bmod.reshape(b_mod.shape), conv_w=g_convw, conv_b=piece("gconvb").reshape(conv_b.shape),
        hg_lb=g_hglb, ml_gate_b=piece("ggate")[0, :16].reshape(ml_gate_b.shape),
        hg_norm_w=piece("ghgw").reshape(hg_norm_w.shape), ml_norm_w=piece("gmlw").reshape(ml_norm_w.shape),
        ln_g=piece("glng").reshape(ln_g.shape), ln_b=piece("glnb").reshape(ln_b.shape))
    params = dict(c_ctx=(c_ctx, m_c_ctx, v_c_ctx), b_mod=(b_mod, m_b_mod, v_b_mod), conv_w=(conv_w, m_conv_w, v_conv_w),
                  conv_b=(conv_b, m_conv_b, v_conv_b), hg_lb=(hg_lb, m_hg_lb, v_hg_lb),
                  ml_gate_b=(ml_gate_b, m_ml_gate_b, v_ml_gate_b), hg_norm_w=(hg_norm_w, m_hg_norm_w, v_hg_norm_w),
                  ml_norm_w=(ml_norm_w, m_ml_norm_w, v_ml_norm_w), ln_g=(ln_g, m_ln_g, v_ln_g), ln_b=(ln_b, m_ln_b, v_ln_b))
    names = list(params)
    rows_of = {n: -(-params[n][0].size // LANE) for n in names}
    rows_tot = -(-sum(rows_of.values()) // 8) * 8
    cat = lambda arrs: jnp.concatenate(
        [_flat_pad(a, rows_of[n]) for n, a in zip(names, arrs)]
        + [jnp.ones((rows_tot - sum(rows_of.values()), LANE), F32)], axis=0)
    d_s, m_s, v_s = _adamw(cat([params[n][0] for n in names]), cat([grads[n] for n in names]),
                           cat([params[n][1] for n in names]), cat([params[n][2] for n in names]), rows_tot, "adamw_small")
    delta, new_m, new_v, off = {}, {}, {}, 0
    for n in names:
        shape, size = params[n][0].shape, params[n][0].size
        take = lambda a: a[off:off + rows_of[n]].reshape(-1)[:size].reshape(shape)
        delta[n], new_m[n], new_v[n] = take(d_s), take(m_s), take(v_s)
        off += rows_of[n]
    grads.update(w_mod=g_wmod[None], w_in=g_wi[None], w_out=g_wo[None])
    delta.update(w_mod=d_wmod[None], w_in=d_wi[None], w_out=d_wo[None])
    new_m.update(w_mod=nm_wmod[None], w_in=nm_wi[None], w_out=nm_wo[None])
    new_v.update(w_mod=nv_wmod[None], w_in=nv_wi[None], w_out=nv_wo[None])

    order = ("c_ctx", "w_mod", "b_mod", "w_in", "conv_w", "conv_b", "hg_lb", "ml_gate_b", "hg_norm_w", "ml_norm_w",
             "w_out", "ln_g", "ln_b")
    return (loss8[0, 0], gx[None], *[grads[n] for n in order], *[delta[n] for n in order],
            *[new_m[n] for n in order], *[new_v[n] for n in order])
```

```python
import functools

import jax
import jax.numpy as jnp
from jax import lax
from jax.experimental import pallas as pl
from jax.experimental.pallas import tpu as pltpu

F32 = jnp.float32
BF16 = jnp.bfloat16

D_MODEL = 2048
W_A = 1024
W_B = 1024
HG_HEADS = 8
HG_D = 128
ML_HEADS = 4
ML_D = 256
CHUNK = 64
N_IN = 10256
LANE = 128
N_U = 81 * LANE
N_DEV = 8
ALPHA = 2.0 ** 0.25
LN_EPS = 1e-5
NORM_EPS = 1e-6
ADAM_LR, ADAM_B1, ADAM_B2, ADAM_EPS, ADAM_WD, ADAM_STEP = 0.001, 0.9, 0.999, 1e-08, 0.01, 10
VMEM_CAP = 60 * 1024 * 1024

SEG_AQ, SEG_AFF, SEG_AFB, SEG_AI, SEG_AZ = range(5)
BLK_QK = 40
SEG_BV, SEG_BO, SEG_BZ = 7, 8, 9
BLK_GATE = 80

MESH = pl.DeviceIdType.MESH


def _vmem(nbytes):
    return pltpu.CompilerParams(vmem_limit_bytes=int(min(VMEM_CAP, max(nbytes, 16 * 1024 * 1024))))


def _sigmoid(x):
    return 1.0 / (1.0 + jnp.exp(-x))


def _silu(x):
    return x * _sigmoid(x)


def _dsilu(x):
    s = _sigmoid(x)
    return s * (1.0 + x * (1.0 - s))


def _silu_both(x):
    s = _sigmoid(x)
    return x * s, s * (1.0 + x * (1.0 - s))


def _bdot(a, b, dims):
    return lax.dot_general(a.astype(BF16), b.astype(BF16), (dims, ((), ())), preferred_element_type=F32)


def _nn(a, b):
    return _bdot(a, b, ((1,), (0,)))


def _nt(a, b):
    return _bdot(a, b, ((1,), (1,)))


def _tn(a, b):
    return _bdot(a, b, ((0,), (0,)))


def _exact_nn(a, b):
    return lax.dot_general(a, b, (((1,), (0,)), ((), ())), precision=lax.Precision.HIGHEST,
                           preferred_element_type=F32)


def _exact_tn(a, b):
    return lax.dot_general(a, b, (((0,), (0,)), ((), ())), precision=lax.Precision.HIGHEST,
                           preferred_element_type=F32)


def _tri(rev):
    t = lax.broadcasted_iota(jnp.int32, (CHUNK, CHUNK), 0)
    s = lax.broadcasted_iota(jnp.int32, (CHUNK, CHUNK), 1)
    return (s >= t) if rev else (s <= t)


def _eye():
    t = lax.broadcasted_iota(jnp.int32, (CHUNK, CHUNK), 0)
    s = lax.broadcasted_iota(jnp.int32, (CHUNK, CHUNK), 1)
    return (s == t).astype(F32)


def _row_to_col(row):
    return jnp.sum(_eye() * row, axis=1, keepdims=True)


def _last_onehot(rev):
    t = lax.broadcasted_iota(jnp.int32, (CHUNK, 1), 0)
    return (t == (0 if rev else CHUNK - 1)).astype(F32)


def _head_slices(width, n_heads):
    hd = width // n_heads
    return [slice(h * hd, (h + 1) * hd) for h in range(n_heads)]


def _scan_sum(x, rev):
    n = x.shape[0]
    t = lax.broadcasted_iota(jnp.int32, x.shape, 0)
    s = 1
    while s < n:
        if rev:
            x = x + jnp.where(t < n - s, pltpu.roll(x, n - s, 0), 0.0)
        else:
            x = x + jnp.where(t >= s, pltpu.roll(x, s, 0), 0.0)
        s *= 2
    return x


def _dot3(a, b, dims):
    a_hi, b_hi = a.astype(BF16), b.astype(BF16)
    a_lo, b_lo = (a - a_hi.astype(F32)).astype(BF16), (b - b_hi.astype(F32)).astype(BF16)
    dot = lambda x, y: lax.dot_general(x, y, (dims, ((), ())), preferred_element_type=F32)
    return dot(a_hi, b_hi) + (dot(a_hi, b_lo) + dot(a_lo, b_hi))


def _hg_common(zq, zf, lb, rev):
    q, dq_dz = _silu_both(zq)
    sg = _sigmoid(zf)
    f = lb + (1.0 - lb) * sg
    g = jnp.log(f)
    k = 1.0 - f
    b = _scan_sum(g, rev)
    b_last = jnp.sum(g, axis=0, keepdims=True)
    r = b[CHUNK // 2:CHUNK // 2 + 1, :]
    e_up = jnp.exp(b - r)
    e_dn = jnp.exp(r - b)
    e_b = e_up * jnp.exp(r)
    e_lb = e_dn * jnp.exp(b_last - r)
    return dict(q=q, dq_dz=dq_dz, sg=sg, f=f, k=k, e_up=e_up, e_dn=e_dn, e_b=e_b, e_lb=e_lb, e_last=jnp.exp(b_last),
                q_t=q * e_up, k_t=k * e_dn, q_s=q * e_b, k_h=k * e_lb, tri=_tri(rev).astype(F32))


def hg_chunk_fwd(zq, zf, v, lb, st, rev):
    c = _hg_common(zq, zf, lb, rev)
    hs = _head_slices(zq.shape[1], zq.shape[1] // HG_D)
    s = [_nt(c["q_t"][:, sl], c["k_t"][:, sl]) for sl in hs]
    oi = [_nt(c["q_s"][:, sl], st[sl, :]) for sl in hs]
    ds = [_tn(v[:, sl], c["k_h"][:, sl]) for sl in hs]
    oa = [_nn(c["tri"] * s_h, v[:, sl]) for s_h, sl in zip(s, hs)]
    o = jnp.concatenate([x + y for x, y in zip(oi, oa)], axis=1)
    st_new = jnp.concatenate([st[sl, :] * c["e_last"][:, sl] + d for sl, d in zip(hs, ds)], axis=0)
    return o, st_new


def hg_chunk_bwd(zq, zf, v, lb, st, do, dst_new, rev):
    c = _hg_common(zq, zf, lb, rev)
    hs = _head_slices(zq.shape[1], zq.shape[1] // HG_D)
    tri, q_t, k_t, q_s, k_h = c["tri"], c["q_t"], c["k_t"], c["q_s"], c["k_h"]
    s = [_nt(q_t[:, sl], k_t[:, sl]) for sl in hs]
    da = [tri * _nt(do[:, sl], v[:, sl]) for sl in hs]
    dq_s = [_nn(do[:, sl], st[sl, :]) for sl in hs]
    dk_h = [_nn(v[:, sl], dst_new[sl, :]) for sl in hs]
    dv_s = [_nt(k_h[:, sl], dst_new[sl, :]) for sl in hs]
    dst_q = [_tn(do[:, sl], q_s[:, sl]) for sl in hs]
    dq_t = [_dot3(da_h, k_t[:, sl], ((1,), (0,))) for da_h, sl in zip(da, hs)]
    dk_t = [_dot3(da_h, q_t[:, sl], ((0,), (0,))) for da_h, sl in zip(da, hs)]
    dv_a = [_tn(tri * s_h, do[:, sl]) for s_h, sl in zip(s, hs)]
    cat = lambda parts: jnp.concatenate(parts, axis=1)
    dq_s, dk_h, dq_t, dk_t = cat(dq_s), cat(dk_h), cat(dq_t), cat(dk_t)
    dv = cat([x + y for x, y in zip(dv_a, dv_s)])
    dst = jnp.concatenate([dst_new[sl, :] * c["e_last"][:, sl] + d for sl, d in zip(hs, dst_q)], axis=0)
    dq = dq_s * c["e_b"] + dq_t * c["e_up"]
    dk = dk_t * c["e_dn"] + dk_h * c["e_lb"]
    db = c["q"] * dq - c["k"] * dk
    ss = cat([jnp.sum(dst_new[sl, :] * st[sl, :], axis=0, keepdims=True) for sl in hs])
    d_all = jnp.sum(dk_h * k_h, axis=0, keepdims=True) + c["e_last"] * ss
    dg = _scan_sum(db, not rev) + d_all
    dzq = dq * c["dq_dz"]
    df = dg / c["f"] - dk
    dzf = df * (1.0 - lb) * c["sg"] * (1.0 - c["sg"])
    dlb = jnp.sum(df * (1.0 - c["sg"]), axis=0, keepdims=True)
    return dzq, dzf, dv, dlb, dst


def _log_sigmoid(x):
    return jnp.minimum(x, 0.0) - jnp.log(1.0 + jnp.exp(-jnp.abs(x)))


def _each(fn, *lists):
    return [fn(*xs) for xs in zip(*lists)]


def _bf(xs):
    return [x.astype(BF16) for x in xs]


def _ml_forward_parts(qp, kp, v, gates, c, n, m, rev, with_num):
    hs = _head_slices(qp.shape[1], qp.shape[1] // ML_D)
    q_all, dq_dp = _silu_both(qp)
    k_all, dk_dp = _silu_both(kp)
    k_all = k_all * (ML_D ** -0.5)
    q = [q_all[:, sl] for sl in hs]
    k = [k_all[:, sl] for sl in hs]
    vv = [v[:, sl] for sl in hs]
    cc = [c[sl, :] for sl in hs]
    tri_b = _tri(rev)
    tri = tri_b.astype(F32)
    tri_t = _tri(not rev).astype(F32)
    e_last = _last_onehot(rev)
    qb, kb, vb, cb = _bf(q), _bf(k), _bf(vv), _bf(cc)
    qk = _each(_nt, qb, kb)
    parts = []
    for (gi_c, gi_r, gf_c, gf_r), m_h in zip(gates, m):
        lf_c, lf_r = _log_sigmoid(gf_c), _log_sigmoid(gf_r)
        b_c = jnp.sum(tri * lf_r, axis=1, keepdims=True)
        b_r = jnp.sum(tri_t * lf_c, axis=0, keepdims=True)
        log_w = jnp.where(tri_b, b_c - b_r + gi_r, -jnp.inf)
        m_inter = b_c + m_h
        m_t = jnp.maximum(m_inter, jnp.max(log_w, axis=1, keepdims=True))
        m_new = jnp.sum(m_t * e_last, axis=0, keepdims=True)
        b_last = jnp.sum(b_c * e_last, axis=0, keepdims=True)
        parts.append(dict(a=jnp.exp(m_inter - m_t), p=jnp.exp(log_w - m_t), floor=jnp.exp(-m_t), m_new=m_new,
                          ws=jnp.exp(b_last - b_c + gi_c - m_new), decay=jnp.exp(b_last + m_h - m_new), gf_c=gf_c))
    w = [pt["p"] * x for pt, x in zip(parts, qk)]
    wb = _bf(w)
    for pt, q_h, n_h, w_h in zip(parts, q, n, w):
        qn = jnp.sum(q_h * n_h, axis=1, keepdims=True)
        den = pt["a"] * qn + jnp.sum(w_h, axis=1, keepdims=True)
        pt.update(qn=qn, den=den, rinv=1.0 / jnp.maximum(jnp.abs(den), pt["floor"]), w=w_h)
    if with_num:
        qc = _each(_nt, qb, cb)
        wv = _each(_nn, wb, vb)
        for pt, qc_h, wv_h in zip(parts, qc, wv):
            pt.update(num=pt["a"] * qc_h + wv_h)
    return hs, q, k, vv, cc, tri, parts, dict(q=qb, k=kb, v=vb, c=cb, w=wb, dq_dp=dq_dp, dk_dp=dk_dp)


def ml_chunk_fwd(qp, kp, v, gates, c, n, m, rev):
    hs, q, k, vv, cc, tri, parts, bf = _ml_forward_parts(qp, kp, v, gates, c, n, m, rev, True)
    h = jnp.concatenate([pt["num"] * pt["rinv"] for pt in parts], axis=1)
    upd = _each(_tn, [pt["ws"] * v_h for pt, v_h in zip(parts, vv)], bf["k"])
    c_new = jnp.concatenate([pt["decay"] * c_h + u for pt, c_h, u in zip(parts, cc, upd)], axis=0)
    n_new = [pt["decay"] * n_h + jnp.sum(pt["ws"] * k_h, axis=0, keepdims=True) for pt, n_h, k_h in zip(parts, n, k)]
    return h, c_new, n_new, [pt["m_new"] for pt in parts]


def ml_chunk_bwd(qp, kp, v, gates, c, n, m, h_out, dh, dc_new, dn_new, rev):
    hs, q, k, vv, cc, tri, parts, bf = _ml_forward_parts(qp, kp, v, gates, c, n, m, rev, False)
    dcn = [dc_new[sl, :] for sl in hs]
    dcb = _bf(dcn)
    dnum, dden = [], []
    for pt, sl in zip(parts, hs):
        dh_h = dh[:, sl]
        signed_live = jnp.where(jnp.abs(pt["den"]) > pt["floor"], jnp.where(pt["den"] >= 0.0, 1.0, -1.0), 0.0)
        dnum.append(dh_h * pt["rinv"])
        dden.append(-jnp.sum(dh_h * h_out[:, sl], axis=1, keepdims=True) * pt["rinv"] * signed_live)
    dnb = _bf(dnum)
    dw = [x + y for x, y in zip(_each(_nt, dnb, bf["v"]), dden)]
    kdc = _each(_nt, bf["k"], dcb)
    vdc = _each(_nn, bf["v"], dcb)
    dqk = [x * pt["p"] for x, pt in zip(dw, parts)]
    adn = [pt["a"] * x for pt, x in zip(parts, dnum)]
    dqkb, adnb = _bf(dqk), _bf(adn)
    dv_w = _each(_tn, bf["w"], dnb)
    dq_k = _each(_nn, dqkb, bf["k"])
    dq_c = _each(_nn, adnb, bf["c"])
    dk_q = _each(_tn, dqkb, bf["q"])
    dc_q = _each(_tn, adnb, bf["q"])
    dq, dk, dv, dgi, dgf, dc, dn = [], [], [], [], [], [], []
    for i, pt in enumerate(parts):
        a, ws, decay = pt["a"], pt["ws"], pt["decay"]
        add = a * dden[i]
        e = dw[i] * pt["w"]
        dv.append(dv_w[i] + ws * kdc[i])
        dq.append(dq_k[i] + dq_c[i] + add * n[i])
        dk.append(dk_q[i] + ws * vdc[i] + ws * dn_new[i])
        alpha = jnp.sum(q[i] * dq_c[i], axis=1, keepdims=True) + dden[i] * pt["qn"] * a
        omega = (jnp.sum(vdc[i] * k[i], axis=1, keepdims=True) + jnp.sum(k[i] * dn_new[i], axis=1, keepdims=True)) * ws
        delta = decay * (jnp.sum(jnp.sum(dcn[i] * cc[i], axis=1, keepdims=True), axis=0, keepdims=True)
                         + jnp.sum(dn_new[i] * n[i], axis=1, keepdims=True))
        dc.append(decay * dcn[i] + dc_q[i])
        dn.append(decay * dn_new[i] + jnp.sum(add * q[i], axis=0, keepdims=True))
        e_rows = jnp.sum(e, axis=1, keepdims=True)
        e_cols = _row_to_col(jnp.sum(e, axis=0, keepdims=True))
        dgi.append(e_cols + omega)
        db = e_rows + alpha - e_cols - omega
        tail = jnp.sum(omega, axis=0, keepdims=True) + delta
        dlf = _row_to_col(jnp.sum(tri * db, axis=0, keepdims=True)) + tail
        dgf.append(dlf * (1.0 - _sigmoid(pt["gf_c"])))
    cat = lambda xs: jnp.concatenate(xs, axis=1)
    dqp = cat(dq) * bf["dq_dp"]
    dkp = cat(dk) * (ML_D ** -0.5) * bf["dk_dp"]
    return dqp, dkp, cat(dv), dgi, dgf, jnp.concatenate(dc, axis=0), dn


def _pick(n, prefs):
    for p in prefs:
        if n % p == 0:
            return p
    raise ValueError(f"no tile for {n} among {prefs}")


def _position():
    return lax.axis_index("x"), lax.axis_index("y"), lax.axis_index("c")


class _Ride:
    def __init__(self, kind, x, cols=None, into=None):
        self.kind, self.x, self.cols, self.into = kind, x, cols, into
        r, c = x.shape[-2:]
        self.out_shape = jax.ShapeDtypeStruct((N_DEV, r, c if cols is None else cols[1]), x.dtype)
        self.width = c

    def _copies(self, x_ref, out_ref, send_sems, recv_sems, local_sem):
        px, py, pc = _position()
        me = 4 * px + 2 * py + pc
        src = (lambda slot: x_ref) if self.kind == "gather" else (lambda slot: x_ref.at[slot])
        dst = ((lambda slot: out_ref.at[slot]) if self.cols is None
               else (lambda slot: out_ref.at[slot, :, pl.ds(self.cols[0], self.width)]))
        mine = pltpu.make_async_copy(src(me), dst(me), local_sem)
        sends, recvs = [], []
        for k, (fx, fy, fc) in enumerate([(1, 0, 0), (0, 1, 0), (1, 1, 0), (1, 0, 1), (0, 1, 1), (1, 1, 1), (0, 0, 1)]):
            qx, qy, qc = (1 - px if fx else px), (1 - py if fy else py), (1 - pc if fc else pc)
            peer = 4 * qx + 2 * qy + qc
            sends.append(pltpu.make_async_remote_copy(
                src_ref=src(peer), dst_ref=dst(me), send_sem=send_sems.at[k], recv_sem=recv_sems.at[k],
                device_id=(qx, qy, qc), device_id_type=MESH))
            recvs.append(pltpu.make_async_remote_copy(
                src_ref=src(me), dst_ref=dst(peer), send_sem=send_sems.at[k], recv_sem=recv_sems.at[k],
                device_id=(qx, qy, qc), device_id_type=MESH))
        return mine, sends, recvs

    def start(self, *refs):
        mine, sends, _ = self._copies(*refs)
        mine.start()
        for cp in sends:
            cp.start()

    def wait(self, *refs):
        mine, sends, recvs = self._copies(*refs)
        for cp in recvs:
            cp.wait_recv()
        for cp in sends:
            cp.wait_send()
        mine.wait()

    def operands(self):
        return [self.x] + ([self.into] if self.into is not None else [])


_RIDE_SCRATCH = [pltpu.SemaphoreType.DMA((7,)), pltpu.SemaphoreType.DMA((7,)), pltpu.SemaphoreType.DMA]
_ANY = pl.BlockSpec(memory_space=pl.ANY)


def _mm(a, b, mode, out_dtype, tm, tn, tk, name, ride=None, b_cols=None, m_out=None):
    if mode == "nn":
        (m, k), (k2, n) = a.shape, b.shape
    elif mode == "nt":
        (m, k), (n, k2) = a.shape, b.shape
    else:
        (k, m), (k2, n) = a.shape, b.shape
    off, n = (0, n) if b_cols is None else b_cols
    assert k == k2 and m % tm == 0 and n % tn == 0 and k % tk == 0, (a.shape, b.shape, mode, tm, tn, tk)
    assert b_cols is None or (mode != "nt" and off % LANE == 0)
    nk = k // tk
    dims = {"nn": ((1,), (0,)), "nt": ((1,), (1,)), "tn": ((0,), (0,))}[mode]
    a_spec = (pl.BlockSpec((tk, tm), lambda j, i, kk: (kk, i)) if mode == "tn"
              else pl.BlockSpec((tm, tk), lambda j, i, kk: (i, kk)))
    if b_cols is not None:
        b_spec = pl.BlockSpec((pl.Element(tk), pl.Element(tn)),
                              lambda j, i, kk: (pl.multiple_of(kk * tk, LANE), pl.multiple_of(off + j * tn, LANE)))
    elif mode == "nt":
        b_spec = pl.BlockSpec((tn, tk), lambda j, i, kk: (j, kk))
    else:
        b_spec = pl.BlockSpec((tk, tn), lambda j, i, kk: (kk, j))

    grid = (n // tn, m // tm, nk)
    n_ride_in = len(ride.operands()) if ride is not None else 0

    def body(a_ref, b_ref, *rest):
        if ride is not None:
            x_ref = rest[0]
            o_ref, got_ref, acc_ref = rest[n_ride_in:n_ride_in + 3]
            comm = (x_ref, got_ref) + tuple(rest[n_ride_in + 3:])
        else:
            o_ref, acc_ref = rest
        kk = pl.program_id(2)
        step = (pl.program_id(0) * grid[1] + pl.program_id(1)) * nk + kk
        if ride is not None:
            @pl.when(step == 0)
            def _():
                ride.start(*comm)

        part = lax.dot_general(a_ref[...], b_ref[...], (dims, ((), ())), preferred_element_type=F32)
        if nk == 1:
            o_ref[...] = part.astype(o_ref.dtype)
        else:
            @pl.when(kk == 0)
            def _():
                acc_ref[...] = part

            @pl.when(jnp.logical_and(kk > 0, kk < nk - 1))
            def _():
                acc_ref[...] += part

            @pl.when(kk == nk - 1)
            def _():
                o_ref[...] = (acc_ref[...] + part).astype(o_ref.dtype)

        if ride is not None:
            @pl.when(step == grid[0] * grid[1] * nk - 1)
            def _():
                ride.wait(*comm)

    osz = jnp.dtype(out_dtype).itemsize
    need = 2 * (tm * tk * a.dtype.itemsize + tk * tn * b.dtype.itemsize + tm * tn * osz) + tm * tn * 4
    o_spec = pl.BlockSpec((tm, tn), lambda j, i, kk: (i, j))
    o_shape = jax.ShapeDtypeStruct((m if m_out is None else m_out, n), out_dtype)
    extra = ride is not None
    return pl.pallas_call(
        body, name=name, grid=grid,
        in_specs=[a_spec, b_spec] + [_ANY] * n_ride_in,
        out_specs=[o_spec, _ANY] if extra else o_spec,
        out_shape=[o_shape, ride.out_shape] if extra else o_shape,
        scratch_shapes=[pltpu.VMEM((tm, tn) if nk > 1 else (8, LANE), F32)] + (_RIDE_SCRATCH if extra else []),
        input_output_aliases={3: 1} if extra and ride.into is not None else {},
        compiler_params=_vmem(need + (12 << 20)),
    )(a, b, *(ride.operands() if extra else []))


ROWS = 256


def _ln_stats(x):
    mu = jnp.mean(x, axis=-1, keepdims=True)
    xc = x - mu
    var = jnp.mean(xc * xc, axis=-1, keepdims=True)
    rstd = lax.rsqrt(var + LN_EPS)
    return xc * rstd, rstd


def _token_specs(nbc, nbx, d):
    return [pl.BlockSpec((ROWS, d), lambda i: (jnp.minimum(i, nbc - 1), 0)),
            pl.BlockSpec((ROWS, d), lambda i: (jnp.maximum(i - nbc, 0), 0))]


def _tokens(c_ref, x_ref, nbc):
    return jnp.where(pl.program_id(0) < nbc, c_ref[...], x_ref[...])


def _modulate_fwd(ctx, x, modp):
    d = x.shape[1]
    nbc, nbx = ctx.shape[0] // ROWS, x.shape[0] // ROWS

    def body(c_ref, x_ref, mod_ref, o_ref):
        n, _ = _ln_stats(_tokens(c_ref, x_ref, nbc))
        o_ref[...] = (n * (1.0 + mod_ref[0, 1:2, :]) + mod_ref[0, 0:1, :]).astype(BF16)

    return pl.pallas_call(
        body, name="modulate_fwd", grid=(nbc + nbx,),
        in_specs=_token_specs(nbc, nbx, d) + [pl.BlockSpec((1, 3, d), lambda i: (jnp.where(i >= nbc, 1, 0), 0, 0))],
        out_specs=pl.BlockSpec((ROWS, d), lambda i: (i, 0)),
        out_shape=jax.ShapeDtypeStruct((ctx.shape[0] + x.shape[0], d), BF16),
    )(ctx, x, modp)


def _modulate_bwd(dh, ctx, x, modp, dxa):
    t, d = x.shape
    nbc, nbx = ctx.shape[0] // ROWS, t // ROWS

    def body(dh_ref, c_ref, x_ref, mod_ref, dxa_ref, gx_ref, sum_ref):
        i = pl.program_id(0)
        n, rstd = _ln_stats(_tokens(c_ref, x_ref, nbc))
        g = dh_ref[...]
        dn = g * (1.0 + mod_ref[0, 1:2, :])
        dx = rstd * (dn - jnp.mean(dn, axis=-1, keepdims=True) - n * jnp.mean(dn * n, axis=-1, keepdims=True))
        gx_ref[...] = dx + dxa_ref[...]
        dshift = jnp.sum(g, axis=0, keepdims=True)
        dscale = jnp.sum(g * n, axis=0, keepdims=True)

        @pl.when(i == 0)
        def _():
            sum_ref[...] = jnp.zeros_like(sum_ref)

        @pl.when(i < nbc)
        def _():
            sum_ref[0:1, :] += dshift
            sum_ref[1:2, :] += dscale

        @pl.when(i >= nbc)
        def _():
            sum_ref[2:3, :] += dshift
            sum_ref[3:4, :] += dscale

    lat = lambda i: (jnp.maximum(i - nbc, 0), 0)
    return pl.pallas_call(
        body, name="modulate_bwd", grid=(nbc + nbx,),
        in_specs=[pl.BlockSpec((ROWS, d), lambda i: (i, 0))] + _token_specs(nbc, nbx, d)
        + [pl.BlockSpec((1, 3, d), lambda i: (jnp.where(i >= nbc, 1, 0), 0, 0)), pl.BlockSpec((ROWS, d), lat)],
        out_specs=[pl.BlockSpec((ROWS, d), lat), pl.BlockSpec((8, d), lambda i: (0, 0))],
        out_shape=[jax.ShapeDtypeStruct((t, d), F32), jax.ShapeDtypeStruct((8, d), F32)],
    )(dh, ctx, x, modp, dxa)


def _post_fwd(o_f, o_b, h_f, h_b, u, hgw, mlw, nbc):
    tt = u.shape[0]
    t = tt - nbc * ROWS

    def body(of_ref, ob_ref, hf_ref, hb_ref, az_ref, bo_ref, bz_ref, hgw_ref, mlw_ref, y_ref):
        o = of_ref[...] + ob_ref[...]
        for sl in _head_slices(W_A, HG_HEADS):
            oh = o[:, sl]
            rs = lax.rsqrt(jnp.mean(oh * oh, axis=-1, keepdims=True) + NORM_EPS)
            y_ref[:, sl] = (oh * rs * hgw_ref[:, sl] * _silu(az_ref[:, sl])).astype(BF16)
        hm = hf_ref[...] + hb_ref[...]
        for sl in _head_slices(W_B, ML_HEADS):
            hh = hm[:, sl]
            mu = jnp.mean(hh, axis=-1, keepdims=True)
            hc = hh - mu
            rstd = lax.rsqrt(jnp.mean(hc * hc, axis=-1, keepdims=True) + NORM_EPS)
            out = hc * rstd * mlw_ref[:, sl] * _sigmoid(bo_ref[:, sl]) * _silu(bz_ref[:, sl])
            y_ref[:, W_A + sl.start:W_A + sl.stop] = out.astype(BF16)

    row = lambda i: (i + nbc, 0)
    seg = lambda s: pl.BlockSpec((ROWS, 1024), lambda i: (i + nbc, s))
    wspec = pl.BlockSpec((1, 1024), lambda i: (0, 0))
    return pl.pallas_call(
        body, name="post_fwd", grid=(t // ROWS,),
        in_specs=[pl.BlockSpec((ROWS, 1024), row)] * 4 + [seg(SEG_AZ), seg(SEG_BO), seg(SEG_BZ), wspec, wspec],
        out_specs=pl.BlockSpec((ROWS, 2048), lambda i: (i, 0)),
        out_shape=jax.ShapeDtypeStruct((t, 2048), BF16),
    )(o_f, o_b, h_f, h_b, u, u, u, hgw, mlw)


def _post_bwd(dz, w_o, o_f, o_b, h_f, h_b, u, hgw, mlw, nbc):
    tt = u.shape[0]
    d = w_o.shape[0]

    def body(dz_ref, w_ref, of_ref, ob_ref, hf_ref, hb_ref, az_ref, bo_ref, bz_ref, hgw_ref, mlw_ref,
             do_ref, dhm_ref, daz_ref, dbo_ref, sum_ref):
        i = pl.program_id(0)
        live = jnp.where(i >= nbc, 1.0, 0.0)
        dy = lax.dot_general(dz_ref[...], w_ref[...], (((1,), (1,)), ((), ())), preferred_element_type=F32) * live

        @pl.when(i == 0)
        def _():
            sum_ref[...] = jnp.zeros_like(sum_ref)

        o = of_ref[...] + ob_ref[...]
        for sl in _head_slices(W_A, HG_HEADS):
            oh = o[:, sl]
            rs = lax.rsqrt(jnp.mean(oh * oh, axis=-1, keepdims=True) + NORM_EPS)
            on = oh * rs
            az = az_ref[:, sl]
            dya = dy[:, sl]
            saz, daz = _silu_both(az)
            doa = dya * saz
            daz_ref[:, sl] = (dya * on * hgw_ref[:, sl] * daz).astype(BF16)
            sum_ref[0:1, sl] += jnp.sum(doa * on, axis=0, keepdims=True)
            don = doa * hgw_ref[:, sl]
            do_ref[:, sl] = rs * (don - on * jnp.mean(don * on, axis=-1, keepdims=True))
        hm = hf_ref[...] + hb_ref[...]
        for sl in _head_slices(W_B, ML_HEADS):
            hh = hm[:, sl]
            mu = jnp.mean(hh, axis=-1, keepdims=True)
            hc = hh - mu
            rstd = lax.rsqrt(jnp.mean(hc * hc, axis=-1, keepdims=True) + NORM_EPS)
            hn = hc * rstd
            hw = hn * mlw_ref[:, sl]
            bo, bz = bo_ref[:, sl], bz_ref[:, sl]
            sbo = _sigmoid(bo)
            sbz, dbz = _silu_both(bz)
            dyb = dy[:, W_A + sl.start:W_A + sl.stop]
            dhw = dyb * sbo * sbz
            dbo_ref[:, sl] = (dyb * hw * sbz * sbo * (1.0 - sbo)).astype(BF16)
            dbo_ref[:, 1024 + sl.start:1024 + sl.stop] = (dyb * hw * sbo * dbz).astype(BF16)
            sum_ref[1:2, sl] += jnp.sum(dhw * hn, axis=0, keepdims=True)
            dhn = dhw * mlw_ref[:, sl]
            dhm_ref[:, sl] = rstd * (dhn - jnp.mean(dhn, axis=-1, keepdims=True)
                                     - hn * jnp.mean(dhn * hn, axis=-1, keepdims=True))

    row = lambda i: (i, 0)
    seg = lambda s: pl.BlockSpec((ROWS, 1024), lambda i: (i, s))
    wspec = pl.BlockSpec((1, 1024), lambda i: (0, 0))
    return pl.pallas_call(
        body, name="post_bwd", grid=(tt // ROWS,),
        in_specs=[pl.BlockSpec((ROWS, 2048), lambda i: (jnp.maximum(i - nbc, 0), 0)), pl.BlockSpec((d, d), lambda i: (0, 0))]
        + [pl.BlockSpec((ROWS, 1024), row)] * 4 + [seg(SEG_AZ), seg(SEG_BO), seg(SEG_BZ), wspec, wspec],
        out_specs=[pl.BlockSpec((ROWS, 1024), row), pl.BlockSpec((ROWS, 1024), row),
                   pl.BlockSpec((ROWS, 1024), row), pl.BlockSpec((ROWS, 2048), row),
                   pl.BlockSpec((8, 1024), lambda i: (0, 0))],
        out_shape=[jax.ShapeDtypeStruct((tt, 1024), F32), jax.ShapeDtypeStruct((tt, 1024), F32),
                   jax.ShapeDtypeStruct((tt, 1024), BF16), jax.ShapeDtypeStruct((tt, 2048), BF16),
                   jax.ShapeDtypeStruct((8, 1024), F32)],
        compiler_params=_vmem(4 * d * d + 30 * ROWS * 2048 * 4),
    )(dz, w_o, o_f, o_b, h_f, h_b, u, u, u, hgw, mlw)


def _final(y, w_o, x, target, modp, ln_g, ln_b):
    t, d = x.shape

    def body(y_ref, w_ref, x_ref, tg_ref, mod_ref, g_ref, b_ref, dz_ref, dxa_ref, sum_ref):
        i = pl.program_id(0)
        zz = lax.dot_general(y_ref[...], w_ref[...], (((1,), (0,)), ((), ())), preferred_element_type=F32)
        gate = mod_ref[0, 2:3, :]
        pre = ALPHA * x_ref[...] + gate * zz
        nh, rstd = _ln_stats(pre)
        err = nh * g_ref[...] + b_ref[...] - tg_ref[...]
        dxo = err * (1.0 / d)
        dnh = dxo * g_ref[...]
        dpre = rstd * (dnh - jnp.mean(dnh, axis=-1, keepdims=True) - nh * jnp.mean(dnh * nh, axis=-1, keepdims=True))
        dz_ref[...] = (gate * dpre).astype(BF16)
        dxa_ref[...] = ALPHA * dpre

        @pl.when(i == 0)
        def _():
            sum_ref[...] = jnp.zeros_like(sum_ref)

        sum_ref[0:1, :] += jnp.sum(dpre * zz, axis=0, keepdims=True)
        sum_ref[1:2, :] += jnp.sum(dxo * nh, axis=0, keepdims=True)
        sum_ref[2:3, :] += jnp.sum(dxo, axis=0, keepdims=True)
        sum_ref[3:4, :] += jnp.sum(err * err, axis=0, keepdims=True)

    row = lambda i: (i, 0)
    vec = pl.BlockSpec((1, d), lambda i: (0, 0))
    return pl.pallas_call(
        body, name="final_ln_loss", grid=(t // ROWS,),
        in_specs=[pl.BlockSpec((ROWS, d), row), pl.BlockSpec((d, d), lambda i: (0, 0)), pl.BlockSpec((ROWS, d), row),
                  pl.BlockSpec((ROWS, d), row), pl.BlockSpec((1, 3, d), lambda i: (1, 0, 0)), vec, vec],
        out_specs=[pl.BlockSpec((ROWS, d), row), pl.BlockSpec((ROWS, d), row), pl.BlockSpec((8, d), lambda i: (0, 0))],
        out_shape=[jax.ShapeDtypeStruct((t, d), BF16), jax.ShapeDtypeStruct((t, d), F32),
                   jax.ShapeDtypeStruct((8, d), F32)],
        compiler_params=_vmem(4 * d * d + 24 * ROWS * d * 4),
    )(y, w_o, x, target, modp, ln_g, ln_b)


GRID_W = 64


def _shift(x, s, ok):
    n = x.shape[0]
    return jnp.where(ok, pltpu.roll(x, s % n, 0), 0.0)


def _grid_masks(n):
    t = lax.broadcasted_iota(jnp.int32, (n, LANE), 0)
    col = t & (GRID_W - 1)
    return dict(left=col >= 1, right=col <= GRID_W - 2, up=t >= GRID_W, down=t < n - GRID_W)


def _seq_masks(n):
    t = lax.broadcasted_iota(jnp.int32, (n, LANE), 0)
    return dict(left=t >= 1, right=t <= n - 2)


def _conv_fwd(u, w9, cb, tc):
    tt = u.shape[0]
    t = tt - tc

    def body(u_ref, w_ref, b_ref, o_ref):
        w = [w_ref[r:r + 1, :] for r in range(9)]
        xc = u_ref[0:tc, :]
        ms = _seq_masks(tc)
        o_ref[0:tc, :] = (w[3] * _shift(xc, 1, ms["left"]) + w[4] * xc + w[5] * _shift(xc, -1, ms["right"])
                          + b_ref[...])
        x = u_ref[tc:tt, :]
        mg = _grid_masks(t)
        taps = (_shift(x, 1, mg["left"]), x, _shift(x, -1, mg["right"]))
        rows = [w[3 * i] * taps[0] + w[3 * i + 1] * taps[1] + w[3 * i + 2] * taps[2] for i in range(3)]
        o_ref[tc:tt, :] = (rows[1] + _shift(rows[0], GRID_W, mg["up"]) + _shift(rows[2], -GRID_W, mg["down"])
                           + b_ref[...])

    return pl.pallas_call(
        body, name="conv_fwd", grid=(2048 // LANE,),
        in_specs=[pl.BlockSpec((tt, LANE), lambda j: (0, BLK_QK + j)), pl.BlockSpec((9, LANE), lambda j: (0, j)),
                  pl.BlockSpec((1, LANE), lambda j: (0, j))],
        out_specs=pl.BlockSpec((tt, LANE), lambda j: (0, j)),
        out_shape=jax.ShapeDtypeStruct((tt, 2048), F32),
        compiler_params=_vmem(40 * tt * LANE * 4),
    )(u, w9, cb)


def _conv_bwd(dcp, u, w9, tc, du):
    tt = u.shape[0]
    t = tt - tc

    def body(d_ref, u_ref, w_ref, du_in_ref, du_ref, gw_ref, gb_ref):
        w = [w_ref[r:r + 1, :] for r in range(9)]
        csum = lambda a: jnp.sum(a, axis=0, keepdims=True)
        dc = d_ref[0:tc, :]
        xc = u_ref[0:tc, :]
        ms = _seq_masks(tc)
        du_ref[0:tc, :] = (w[3] * _shift(dc, -1, ms["right"]) + w[4] * dc + w[5] * _shift(dc, 1, ms["left"])).astype(BF16)
        gmid = [csum(dc * _shift(xc, 1, ms["left"])), csum(dc * xc), csum(dc * _shift(xc, -1, ms["right"]))]
        d = d_ref[tc:tt, :]
        x = u_ref[tc:tt, :]
        mg = _grid_masks(t)
        dtaps = (_shift(d, -1, mg["right"]), d, _shift(d, 1, mg["left"]))
        rows = [w[3 * i] * dtaps[0] + w[3 * i + 1] * dtaps[1] + w[3 * i + 2] * dtaps[2] for i in range(3)]
        du_ref[tc:tt, :] = (rows[1] + _shift(rows[0], -GRID_W, mg["down"]) + _shift(rows[2], GRID_W, mg["up"])).astype(BF16)
        xtaps = (_shift(x, 1, mg["left"]), x, _shift(x, -1, mg["right"]))
        for j in range(3):
            gw_ref[j:j + 1, :] = csum(d * _shift(xtaps[j], GRID_W, mg["up"]))
            gw_ref[3 + j:4 + j, :] = csum(d * xtaps[j]) + gmid[j]
            gw_ref[6 + j:7 + j, :] = csum(d * _shift(xtaps[j], -GRID_W, mg["down"]))
        gb_ref[...] = csum(d) + csum(dc)

    return pl.pallas_call(
        body, name="conv_bwd", grid=(2048 // LANE,),
        in_specs=[pl.BlockSpec((tt, LANE), lambda j: (0, j)), pl.BlockSpec((tt, LANE), lambda j: (0, BLK_QK + j)),
                  pl.BlockSpec((9, LANE), lambda j: (0, j)), _ANY],
        out_specs=[pl.BlockSpec((tt, LANE), lambda j: (0, BLK_QK + j)), pl.BlockSpec((9, LANE), lambda j: (0, j)),
                   pl.BlockSpec((1, LANE), lambda j: (0, j))],
        out_shape=[jax.ShapeDtypeStruct(du.shape, BF16), jax.ShapeDtypeStruct((9, 2048), F32),
                   jax.ShapeDtypeStruct((1, 2048), F32)],
        input_output_aliases={3: 0},
        compiler_params=_vmem(48 * tt * LANE * 4),
    )(dcp, u, w9, du)


SUB = 2
STEP = SUB * CHUNK


def _chunk_of(pos, ncc, nc, rev):
    if not rev:
        return pos
    return jnp.where(pos < ncc, ncc - 1 - pos, nc - 1 - (pos - ncc))


def _sub_rows(rev):
    order = range(SUB - 1, -1, -1) if rev else range(SUB)
    return [(s, slice(s * CHUNK, (s + 1) * CHUNK)) for s in order]


def _hgrn_fwd(u, lower_d, ncc, rev):
    tt = u.shape[0]
    nc, ncc = tt // STEP, ncc // SUB
    seg_f = SEG_AFB if rev else SEG_AFF

    def body(zq_ref, zf_ref, v_ref, lb_ref, o_ref, hist_ref, st_ref):
        @pl.when(pl.program_id(0) == 0)
        def _():
            st_ref[...] = jnp.zeros_like(st_ref)

        st = st_ref[...]
        for s, r in _sub_rows(rev):
            hist_ref[s] = st
            o, st = hg_chunk_fwd(zq_ref[r, :], zf_ref[r, :], v_ref[r, :], lb_ref[...], st, rev)
            o_ref[r, :] = o
        st_ref[...] = st

    seg = lambda s: pl.BlockSpec((STEP, 1024), lambda j: (_chunk_of(j, ncc, nc, rev), s))
    return pl.pallas_call(
        body, name="hgrn_fwd_rev" if rev else "hgrn_fwd", grid=(nc,),
        in_specs=[seg(SEG_AQ), seg(seg_f), seg(SEG_AI), pl.BlockSpec((1, 1024), lambda j: (0, 0))],
        out_specs=[pl.BlockSpec((STEP, 1024), lambda j: (_chunk_of(j, ncc, nc, rev), 0)),
                   pl.BlockSpec((SUB, 1024, HG_D), lambda j: (_chunk_of(j, ncc, nc, rev), 0, 0))],
        out_shape=[jax.ShapeDtypeStruct((tt, 1024), F32), jax.ShapeDtypeStruct((nc * SUB, 1024, HG_D), F32)],
        scratch_shapes=[pltpu.VMEM((1024, HG_D), F32)],
    )(u, u, u, lower_d)


def _hgrn_bwd(u, lower_d, hist, do, ncc, rev, ride=None, final=None):
    tt = u.shape[0]
    nc, ncc = tt // STEP, ncc // SUB
    seg_f = SEG_AFB if rev else SEG_AFF
    is_final = final is not None
    has_a2a = ride is not None
    n_out = 2 if is_final else 4
    width = 5 * 1024

    def body(zq_ref, zf_ref, v_ref, lb_ref, hist_ref, do_ref, *rest):
        if is_final:
            aq_ref, av_ref, af_ref, az_ref = rest[:4]
            rest = rest[4:]
        if has_a2a:
            x_ref, rest = rest[0], rest[1:]
        outs, rest = rest[:n_out], rest[n_out:]
        dlb_ref = outs[-1]
        if has_a2a:
            comm = (x_ref, rest[0]) + tuple(rest[2:])
            dst_ref = rest[1]
        else:
            dst_ref = rest[0]

        @pl.when(pl.program_id(0) == 0)
        def _():
            dst_ref[...] = jnp.zeros_like(dst_ref)
            dlb_ref[...] = jnp.zeros_like(dlb_ref)
            if has_a2a:
                ride.start(*comm)

        dst = dst_ref[...]
        dlb_sum = dlb_ref[...]
        for s, r in reversed(_sub_rows(rev)):
            dzq, dzf, dv, dlb, dst = hg_chunk_bwd(zq_ref[r, :], zf_ref[r, :], v_ref[r, :], lb_ref[...],
                                                  hist_ref[s], do_ref[r, :], dst, rev)
            dlb_sum = dlb_sum + dlb
            if is_final:
                du_ref = outs[0]
                dzf_own, dzf_other = dzf.astype(BF16), af_ref[r, :]
                du_ref[r, 0:1024] = (dzq + aq_ref[r, :]).astype(BF16)
                du_ref[r, 1024:2048] = dzf_other if rev else dzf_own
                du_ref[r, 2048:3072] = dzf_own if rev else dzf_other
                du_ref[r, 3072:4096] = (dv + av_ref[r, :]).astype(BF16)
                du_ref[r, 4096:5120] = az_ref[r, :]
            else:
                dzf_ref, dzq_ref, dv_ref = outs[:3]
                dzf_ref[r, :] = dzf.astype(BF16)
                dzq_ref[r, :] = dzq
                dv_ref[r, :] = dv
        dst_ref[...] = dst
        dlb_ref[...] = dlb_sum

        if has_a2a:
            @pl.when(pl.program_id(0) == nc - 1)
            def _():
                ride.wait(*comm)

    cidx = lambda j: _chunk_of(nc - 1 - j, ncc, nc, rev)
    seg = lambda s: pl.BlockSpec((STEP, 1024), lambda j: (cidx(j), s))
    row = pl.BlockSpec((STEP, 1024), lambda j: (cidx(j), 0))
    dlb_spec = pl.BlockSpec((1, 1024), lambda j: (0, 0))
    dlb_shape = jax.ShapeDtypeStruct((1, 1024), F32)
    if is_final:
        out_specs = [pl.BlockSpec((STEP, width), lambda j: (cidx(j), 0)), dlb_spec]
        out_shape = [jax.ShapeDtypeStruct((tt, N_U), BF16), dlb_shape]
    else:
        out_specs = [row, row, row, dlb_spec]
        out_shape = [jax.ShapeDtypeStruct((tt, 1024), BF16), jax.ShapeDtypeStruct((tt, 1024), F32),
                     jax.ShapeDtypeStruct((tt, 1024), F32), dlb_shape]
    ins = [u, u, u, lower_d, hist, do] + (list(final) if is_final else []) + ([ride.x] if has_a2a else [])
    return pl.pallas_call(
        body, name="hgrn_bwd_rev" if rev else "hgrn_bwd", grid=(nc,),
        in_specs=[seg(SEG_AQ), seg(seg_f), seg(SEG_AI), pl.BlockSpec((1, 1024), lambda j: (0, 0)),
                  pl.BlockSpec((SUB, 1024, HG_D), lambda j: (cidx(j), 0, 0)), row] + ([row] * 4 if is_final else [])
        + ([_ANY] if has_a2a else []),
        out_specs=out_specs + ([_ANY] if has_a2a else []),
        out_shape=out_shape + ([ride.out_shape] if has_a2a else []),
        scratch_shapes=[pltpu.VMEM((1024, HG_D), F32)] + (_RIDE_SCRATCH if has_a2a else []),
    )(*ins)


def _gate_views(g_ref, b_ref, r, head, rev):
    gc = g_ref[r, :] + b_ref[...]
    lane = lax.broadcasted_iota(jnp.int32, (1, LANE), 1)
    eye = _eye()
    d = 1 if rev else 0
    ii, fi = d * ML_HEADS + head, 2 * ML_HEADS + d * ML_HEADS + head
    col = lambda idx: jnp.sum(jnp.where(lane == idx, gc, 0.0), axis=1, keepdims=True)
    row = lambda c: jnp.sum(eye * c, axis=0, keepdims=True)
    gi, gf = col(ii), col(fi)
    return gi, row(gi), gf, row(gf)


def _mlstm_fwd(cpre, u, bias, ncc, rev):
    tt = u.shape[0]
    nc, ncc = tt // STEP, ncc // SUB
    nhd = ML_HEADS

    def body(q_ref, k_ref, v_ref, g_ref, b_ref, h_ref, ch_ref, nh_ref, mh_ref, c_ref, n_ref, m_ref):
        @pl.when(pl.program_id(0) == 0)
        def _():
            c_ref[...] = jnp.zeros_like(c_ref)
            n_ref[...] = jnp.zeros_like(n_ref)
            m_ref[...] = jnp.zeros_like(m_ref)

        c, n_all, m_all = c_ref[...], n_ref[...], m_ref[...]
        n = [n_all[hd:hd + 1, :] for hd in range(nhd)]
        m = [m_all[hd:hd + 1, 0:1] for hd in range(nhd)]
        for s, r in _sub_rows(rev):
            ch_ref[s] = c
            for hd in range(nhd):
                nh_ref[s, hd:hd + 1, :] = n[hd]
                mh_ref[s, hd:hd + 1, :] = jnp.broadcast_to(m[hd], (1, LANE))
            gates = [_gate_views(g_ref, b_ref, r, hd, rev) for hd in range(nhd)]
            h, c, n, m = ml_chunk_fwd(q_ref[r, :], k_ref[r, :], v_ref[r, :], gates, c, n, m, rev)
            h_ref[r, :] = h
        c_ref[...] = c
        for hd in range(nhd):
            n_ref[hd:hd + 1, :] = n[hd]
            m_ref[hd:hd + 1, :] = jnp.broadcast_to(m[hd], (1, LANE))

    cidx = lambda j: _chunk_of(j, ncc, nc, rev)
    row = lambda s: pl.BlockSpec((STEP, 1024), lambda j: (cidx(j), s))
    st3 = lambda a, b: pl.BlockSpec((SUB, a, b), lambda j: (cidx(j), 0, 0))
    return pl.pallas_call(
        body, name="mlstm_fwd_rev" if rev else "mlstm_fwd", grid=(nc,),
        in_specs=[row(0), row(1), row(SEG_BV), pl.BlockSpec((STEP, LANE), lambda j: (cidx(j), BLK_GATE)),
                  pl.BlockSpec((1, LANE), lambda j: (0, 0))],
        out_specs=[row(0), st3(1024, ML_D), st3(8, ML_D), st3(8, LANE)],
        out_shape=[jax.ShapeDtypeStruct((tt, 1024), F32), jax.ShapeDtypeStruct((nc * SUB, 1024, ML_D), F32),
                   jax.ShapeDtypeStruct((nc * SUB, 8, ML_D), F32), jax.ShapeDtypeStruct((nc * SUB, 8, LANE), F32)],
        scratch_shapes=[pltpu.VMEM((1024, ML_D), F32), pltpu.VMEM((8, ML_D), F32), pltpu.VMEM((8, LANE), F32)],
    )(cpre, cpre, u, u, bias)


def _mlstm_bwd(cpre, u, bias, chist, nhist, mhist, h_out, dh, ncc, rev, final=None):
    tt = u.shape[0]
    nc, ncc = tt // STEP, ncc // SUB
    nhd = ML_HEADS
    is_final = final is not None
    d = 1 if rev else 0
    col0, width = SEG_BV * 1024, N_U - SEG_BV * 1024

    def body(q_ref, k_ref, v_ref, g_ref, b_ref, ch_ref, nh_ref, mh_ref, ho_ref, dh_ref, *rest):
        if is_final:
            aqk_ref, av_ref, ag_ref, bo_ref = rest[:4]
            dqk_ref, du_ref, gs_ref, dc_ref, dn_ref = rest[5:]
        else:
            dqk_ref, dv_ref, dg_ref, gs_ref, dc_ref, dn_ref = rest

        @pl.when(pl.program_id(0) == 0)
        def _():
            dc_ref[...] = jnp.zeros_like(dc_ref)
            dn_ref[...] = jnp.zeros_like(dn_ref)
            gs_ref[...] = jnp.zeros_like(gs_ref)

        lane = lax.broadcasted_iota(jnp.int32, (1, LANE), 1)
        dc, dn_all, gs = dc_ref[...], dn_ref[...], gs_ref[...]
        dn = [dn_all[hd:hd + 1, :] for hd in range(nhd)]
        for s, r in reversed(_sub_rows(rev)):
            gates = [_gate_views(g_ref, b_ref, r, hd, rev) for hd in range(nhd)]
            n_all, m_all = nh_ref[s], mh_ref[s]
            dqp, dkp, dv, dgi, dgf, dc, dn = ml_chunk_bwd(
                q_ref[r, :], k_ref[r, :], v_ref[r, :], gates, ch_ref[s],
                [n_all[hd:hd + 1, :] for hd in range(nhd)], [m_all[hd:hd + 1, 0:1] for hd in range(nhd)],
                ho_ref[r, :], dh_ref[r, :], dc, dn, rev)
            dg = ag_ref[r, :] if is_final else jnp.zeros((CHUNK, LANE), F32)
            for hd in range(nhd):
                dg = dg + jnp.where(lane == d * ML_HEADS + hd, dgi[hd], 0.0)
                dg = dg + jnp.where(lane == 2 * ML_HEADS + d * ML_HEADS + hd, dgf[hd], 0.0)
            if is_final:
                dqp = dqp + aqk_ref[r, 0:W_B]
                dkp = dkp + aqk_ref[r, W_B:2 * W_B]
                du_ref[r, 0:1024] = (dv + av_ref[r, :]).astype(BF16)
                du_ref[r, 1024:3072] = bo_ref[r, :]
                du_ref[r, 3072:3072 + LANE] = dg.astype(BF16)
            else:
                dv_ref[r, :] = dv
                dg_ref[r, :] = dg
            dqk_ref[r, 0:W_B] = dqp
            dqk_ref[r, W_B:2 * W_B] = dkp
            gs = gs + jnp.sum(dg, axis=0, keepdims=True)
        dc_ref[...] = dc
        gs_ref[...] = gs
        for hd in range(nhd):
            dn_ref[hd:hd + 1, :] = dn[hd]

    cidx = lambda j: _chunk_of(nc - 1 - j, ncc, nc, rev)
    row = lambda s: pl.BlockSpec((STEP, 1024), lambda j: (cidx(j), s))
    wide = pl.BlockSpec((STEP, 2048), lambda j: (cidx(j), 0))
    gate = pl.BlockSpec((STEP, LANE), lambda j: (cidx(j), 0))
    st3 = lambda a, b: pl.BlockSpec((SUB, a, b), lambda j: (cidx(j), 0, 0))
    gs_spec, gs_shape = pl.BlockSpec((1, LANE), lambda j: (0, 0)), jax.ShapeDtypeStruct((1, LANE), F32)
    dqk_shape = jax.ShapeDtypeStruct((tt, 2048), F32)
    ins = [cpre, cpre, u, u, bias, chist, nhist, mhist, h_out, dh] + (list(final) if is_final else [])
    if is_final:
        out_specs = [wide, pl.BlockSpec((pl.Element(STEP), pl.Element(width)), lambda j: (cidx(j) * STEP, col0)), gs_spec]
        out_shape = [dqk_shape, jax.ShapeDtypeStruct((tt, N_U), BF16), gs_shape]
    else:
        out_specs = [wide, row(0), gate, gs_spec]
        out_shape = [dqk_shape, jax.ShapeDtypeStruct((tt, 1024), F32), jax.ShapeDtypeStruct((tt, LANE), F32), gs_shape]
    return pl.pallas_call(
        body, name="mlstm_bwd_rev" if rev else "mlstm_bwd", grid=(nc,),
        in_specs=[row(0), row(1), row(SEG_BV), pl.BlockSpec((STEP, LANE), lambda j: (cidx(j), BLK_GATE)),
                  pl.BlockSpec((1, LANE), lambda j: (0, 0)),
                  st3(1024, ML_D), st3(8, ML_D), st3(8, LANE), row(0), row(0)]
        + ([wide, row(0), gate, wide, _ANY] if is_final else []),
        out_specs=out_specs, out_shape=out_shape,
        input_output_aliases={14: 1} if is_final else {},
        scratch_shapes=[pltpu.VMEM((1024, ML_D), F32), pltpu.VMEM((8, ML_D), F32)],
    )(*ins)


def _whole(body, out_shape, name, *args, nbytes=0):
    return pl.pallas_call(body, name=name, out_shape=out_shape, compiler_params=_vmem(nbytes))(*args)


def _mod_fwd(cs, w_cols, b_cols):
    def body(c_ref, w_ref, b_ref, o_ref):
        o_ref[...] = _exact_nn(_silu(c_ref[...]), w_ref[...]) + b_ref[...]

    return _whole(body, jax.ShapeDtypeStruct((16, w_cols.shape[1]), F32), "mod_fwd", cs, w_cols, b_cols,
                  nbytes=4 * w_cols.size * 4)


def _mod_bwd_w(cs, d9, w_cols):
    def body(c_ref, d_ref, w_ref, gw_ref, pc_ref):
        gw_ref[...] = _exact_tn(_silu(c_ref[...]), d_ref[...])
        pc = lax.dot_general(d_ref[8:16, :], w_ref[...], (((1,), (1,)), ((), ())), precision=lax.Precision.HIGHEST,
                             preferred_element_type=F32)
        row = lax.broadcasted_iota(jnp.int32, pc.shape, 0)
        pc_ref[...] = jnp.where(row == 0, pc, 0.0)

    return _whole(body, [jax.ShapeDtypeStruct(w_cols.shape, F32), jax.ShapeDtypeStruct((8, w_cols.shape[0]), F32)],
                  "mod_bwd_w", cs, d9, w_cols, nbytes=6 * w_cols.size * 4)


def _lower_fwd(lb4):
    def body(l_ref, o_ref):
        o_ref[...] = jnp.zeros_like(o_ref)
        o_ref[0:1, :] = 1.0 / (1.0 + jnp.exp(l_ref[1:2, :] - l_ref[0:1, :]))
        o_ref[1:2, :] = 1.0 / (1.0 + jnp.exp(l_ref[3:4, :] - l_ref[2:3, :]))

    return _whole(body, jax.ShapeDtypeStruct((8, lb4.shape[1]), F32), "lower_fwd", lb4)


def _reduce8(g, name):
    def body(g_ref, o_ref):
        acc = g_ref[0]
        for k in range(1, N_DEV):
            acc = acc + g_ref[k]
        o_ref[...] = acc

    return _whole(body, jax.ShapeDtypeStruct(g.shape[1:], F32), name, g, nbytes=4 * g.size * 4)


_PACK = (("dmodx", 48), ("dmodc", 48), ("gconvw", 144), ("gconvb", 16), ("dlower", 16), ("ghgw", 8), ("gmlw", 8),
         ("glng", 16), ("glnb", 16), ("losssq", 16), ("ggate", 8))
_PACK_ROWS = sum(r for _, r in _PACK)


def _pack_offsets():
    off, out = 0, {}
    for name, rows in _PACK:
        out[name] = (off, rows)
        off += rows
    return out


def _small_finish(total, p0, d_feat):
    offs = _pack_offsets()

    def body(t_ref, p_ref, gb_ref, a0_ref, a1_ref, loss_ref):
        ox, oc, ol, oq = offs["dmodx"][0], offs["dmodc"][0], offs["dlower"][0], offs["losssq"][0]
        gb_ref[...] = t_ref[ox:ox + 48, :] + t_ref[oc:oc + 48, :]
        p = p_ref[...]
        da0 = t_ref[ol:ol + 16, :] * p * (1.0 - p)
        a0_ref[...] = da0
        a1_ref[...] = -da0
        sq = t_ref[oq:oq + 16, :]
        tot = jnp.sum(jnp.sum(sq, axis=1, keepdims=True), axis=0, keepdims=True)
        loss_ref[...] = jnp.broadcast_to(tot * (0.5 / d_feat), loss_ref.shape)

    s = jax.ShapeDtypeStruct
    return _whole(body, [s((48, LANE), F32), s((16, LANE), F32), s((16, LANE), F32), s((8, LANE), F32)],
                  "small_finish", total, p0)


def _cctx_grad(parts, c_ctx8):
    def body(p_ref, c_ref, o_ref):
        acc = p_ref[0]
        for k in range(1, N_DEV):
            acc = acc + p_ref[k]
        o_ref[...] = acc * _dsilu(c_ref[...])

    return _whole(body, jax.ShapeDtypeStruct(c_ctx8.shape, F32), "cctx_grad", parts, c_ctx8)


def _adam_math(w, g, m, v):
    m = ADAM_B1 * m + (1.0 - ADAM_B1) * g
    v = ADAM_B2 * v + (1.0 - ADAM_B2) * (g * g)
    m_hat = m / (1.0 - ADAM_B1 ** ADAM_STEP)
    v_hat = v / (1.0 - ADAM_B2 ** ADAM_STEP)
    delta = -ADAM_LR * (m_hat / (jnp.sqrt(v_hat) + ADAM_EPS) + ADAM_WD * w)
    return delta, m, v


def _adamw(w, g, m, v, rows, name):
    r, c = w.shape

    def body(w_ref, g_ref, m_ref, v_ref, d_ref, mo_ref, vo_ref):
        d_ref[...], mo_ref[...], vo_ref[...] = _adam_math(w_ref[...], g_ref[...], m_ref[...], v_ref[...])

    spec = pl.BlockSpec((rows, c), lambda i: (i, 0))
    return pl.pallas_call(
        body, name=name, grid=(r // rows,), in_specs=[spec] * 4, out_specs=[spec] * 3,
        out_shape=[jax.ShapeDtypeStruct((r, c), F32)] * 3,
        compiler_params=_vmem(16 * rows * (c + LANE) * 4),
    )(w, g, m, v)


def _rs_adamw(recv, w, m, v, tile, name, by_cols=False):
    _, r, c = recv.shape

    def body(r_ref, w_ref, m_ref, v_ref, g_ref, d_ref, mo_ref, vo_ref):
        g = r_ref[0].astype(F32)
        for k in range(1, N_DEV):
            g = g + r_ref[k].astype(F32)
        g_ref[...] = g
        d_ref[...], mo_ref[...], vo_ref[...] = _adam_math(w_ref[...], g, m_ref[...], v_ref[...])

    if by_cols:
        spec = pl.BlockSpec((r, tile), lambda i: (0, i))
        rspec = pl.BlockSpec((N_DEV, r, tile), lambda i: (0, 0, i))
        steps, elems = c // tile, (r + 16) * tile
    else:
        spec = pl.BlockSpec((tile, c), lambda i: (i, 0))
        rspec = pl.BlockSpec((N_DEV, tile, c), lambda i: (0, i, 0))
        steps, elems = r // tile, tile * (c + LANE)
    return pl.pallas_call(
        body, name=name, grid=(steps,), in_specs=[rspec] + [spec] * 3, out_specs=[spec] * 4,
        out_shape=[jax.ShapeDtypeStruct((r, c), F32)] * 4,
        compiler_params=_vmem(2 * elems * (N_DEV * 2 + 7 * 4) + (4 << 20)),
    )(recv, w, m, v)


def _all_gather(x, name):
    r, c = x.shape

    def body(x_ref, out_ref, send_sems, recv_sems, local_sem):
        px, py, pc = _position()
        me, sibling = (px, py, pc), (px, py, 1 - pc)
        chips = [(1 - px, py), (px, 1 - py), (1 - px, 1 - py)]

        def slot(qx, qy, qc):
            return out_ref.at[4 * qx + 2 * qy + qc]

        def copy(k, block, to, src=None):
            return pltpu.make_async_remote_copy(
                src_ref=slot(*block) if src is None else src, dst_ref=slot(*block),
                send_sem=send_sems.at[k], recv_sem=recv_sems.at[k], device_id=to, device_id_type=MESH)

        mine = pltpu.make_async_copy(x_ref, slot(*me), local_sem)
        mine.start()
        first = [copy(1 + j, me, (*chip, pc), src=x_ref) for j, chip in enumerate(chips)]
        first.append(copy(0, me, sibling, src=x_ref))
        for cp in first:
            cp.start()
        passed = [copy(4 + j, (*chip, pc), sibling) for j, chip in enumerate(chips)]
        for j, chip in enumerate(chips):
            copy(1 + j, (*chip, pc), me).wait_recv()
            passed[j].start()
        copy(0, sibling, me).wait_recv()
        for j, chip in enumerate(chips):
            copy(4 + j, (*chip, 1 - pc), me).wait_recv()
        for cp in first + passed:
            cp.wait_send()
        mine.wait()

    return pl.pallas_call(
        body, name=name, out_shape=jax.ShapeDtypeStruct((N_DEV, r, c), x.dtype),
        in_specs=[pl.BlockSpec(memory_space=pl.ANY)], out_specs=pl.BlockSpec(memory_space=pl.ANY),
        scratch_shapes=[pltpu.SemaphoreType.DMA((7,)), pltpu.SemaphoreType.DMA((7,)), pltpu.SemaphoreType.DMA],
    )(x)


DW_PIECES = ((0, 256), (256, 640), (896, 1152))


def _local_step(ctx, x, target, modp, lower, wt_u, w_o, w9, conv_b, gate_b, hgw, mlw, ln_g, ln_b, exchange):
    tc = ctx.shape[0]
    tt = tc + x.shape[0]
    nbc, ncc = tc // ROWS, tc // CHUNK
    lower_f, lower_b = lower[0:1], lower[1:2]

    hc = _modulate_fwd(ctx, x, modp)
    tmh = _pick(tt, (1088, 768, 512, 256))
    if exchange:
        u, w_o = _mm(hc, wt_u, "nt", F32, tmh, 1152, D_MODEL, "mm_u", ride=_Ride("gather", w_o))
        w_o = w_o.reshape(D_MODEL, D_MODEL)
    else:
        u = _mm(hc, wt_u, "nt", F32, tmh, 1152, D_MODEL, "mm_u")
    cpre = _conv_fwd(u, w9, conv_b, tc)
    bias = jnp.pad(gate_b.reshape(1, 16), ((0, 0), (0, LANE - 16)))

    o_f, hist_f = _hgrn_fwd(u, lower_f, ncc, False)
    o_b, hist_b = _hgrn_fwd(u, lower_b, ncc, True)
    h_f, ch_f, nh_f, mh_f = _mlstm_fwd(cpre, u, bias, ncc, False)
    h_b, ch_b, nh_b, mh_b = _mlstm_fwd(cpre, u, bias, ncc, True)
    y = _post_fwd(o_f, o_b, h_f, h_b, u, hgw, mlw, nbc)
    dz, dxa, fsum = _final(y, w_o, x, target, modp, ln_g, ln_b)

    dw_o = _mm(y, dz, "tn", BF16, D_MODEL, 1024, _pick(y.shape[0], (512, 256)), "mm_dwo")
    do, dhm, daz, dbo, psum = _post_bwd(dz, w_o, o_f, o_b, h_f, h_b, u, hgw, mlw, nbc)
    if exchange:
        dzf_f, dzq, dv_a, dlb_f, dw_o = _hgrn_bwd(
            u, lower_f, hist_f, do, ncc, False, ride=_Ride("a2a", dw_o.reshape(N_DEV, D_MODEL // N_DEV, D_MODEL)))
    else:
        dzf_f, dzq, dv_a, dlb_f = _hgrn_bwd(u, lower_f, hist_f, do, ncc, False)
    du, dlb_b = _hgrn_bwd(u, lower_b, hist_b, do, ncc, True, final=(dzq, dv_a, dzf_f, daz))
    dqk, dv_m, dg, _ = _mlstm_bwd(cpre, u, bias, ch_f, nh_f, mh_f, h_f, dhm, ncc, False)
    dqk, du, gsum = _mlstm_bwd(cpre, u, bias, ch_b, nh_b, mh_b, h_b, dhm, ncc, True, final=(dqk, dv_m, dg, dbo, du))
    du, gconvw, gconvb = _conv_bwd(dqk, u, w9, tc, du)
    tkw = _pick(tt, (2176, 768, 512, 256))
    blocks = lambda g: g.reshape(N_DEV, N_IN // N_DEV, g.shape[1])
    dwu = lambda name, cols, ride: _mm(du, hc, "tn", BF16, 1152, cols[1], tkw, name, b_cols=cols, ride=ride, m_out=N_IN)
    dwt_a = dwu("mm_dwu_a", DW_PIECES[0], None)
    if exchange:
        whole = lambda piece, into: _Ride("a2a", blocks(piece[1]), cols=(piece[0][0], D_MODEL), into=into)
        dwt_b, got = dwu("mm_dwu_b", DW_PIECES[1], whole((DW_PIECES[0], dwt_a), None))
        dwt_c, got = dwu("mm_dwu_c", DW_PIECES[2], whole((DW_PIECES[1], dwt_b), got))
        dh, dwt_u = _mm(du, wt_u, "nn", F32, tmh, D_MODEL // 2, 3456, "mm_dh", ride=whole((DW_PIECES[2], dwt_c), got))
    else:
        dwt_u = jnp.concatenate([dwt_a, dwu("mm_dwu_b", DW_PIECES[1], None), dwu("mm_dwu_c", DW_PIECES[2], None)], axis=1)
        dh = _mm(du, wt_u, "nn", F32, tmh, D_MODEL // 2, 3456, "mm_dh")
    gx, msum = _modulate_bwd(dh, ctx, x, modp, dxa)

    zero_row = jnp.zeros((1, D_MODEL), F32)
    small = dict(
        dmodx=jnp.concatenate([msum[2:3], msum[3:4], fsum[0:1]], axis=0),
        dmodc=jnp.concatenate([msum[0:1], msum[1:2], zero_row], axis=0),
        gconvw=gconvw, gconvb=gconvb, dlower=jnp.concatenate([dlb_f, dlb_b], axis=0),
        ghgw=psum[0:1], gmlw=psum[1:2], glng=fsum[1:2], glnb=fsum[2:3], losssq=fsum[3:4],
        ggate=jnp.concatenate([gsum, jnp.zeros((7, LANE), F32)], axis=0))
    return gx, dwt_u, dw_o, small


def _pack_small(small):
    return jnp.concatenate([small[name].reshape(rows, LANE) for name, rows in _PACK], axis=0)


def _flat_pad(a, rows):
    flat = a.reshape(-1)
    return jnp.pad(flat, (0, rows * LANE - flat.shape[0])).reshape(rows, LANE)


def kernel(x, c, ctx, c_ctx, w_mod, b_mod, w_in, conv_w, conv_b, hg_lb, ml_gate_b, hg_norm_w, ml_norm_w, w_out, ln_g, ln_b, loss_target, m_c_ctx, m_w_mod, m_b_mod, m_w_in, m_conv_w, m_conv_b, m_hg_lb, m_ml_gate_b, m_hg_norm_w, m_ml_norm_w, m_w_out, m_ln_g, m_ln_b, v_c_ctx, v_w_mod, v_b_mod, v_w_in, v_conv_w, v_conv_b, v_hg_lb, v_ml_gate_b, v_hg_norm_w, v_ml_norm_w, v_w_out, v_ln_g, v_ln_b):
    px, py, pc = _position()
    me = 4 * px + 2 * py + pc
    d = D_MODEL
    n_mod = w_mod.shape[2]
    n_wi = w_in.shape[2]
    n_cv = conv_w.shape[3]
    n_lb = hg_lb.shape[2]

    pack0 = jnp.concatenate([c.reshape(-1), conv_w.reshape(-1), hg_lb.reshape(-1)]).reshape(1, -1)
    g0 = _all_gather(pack0, "gather_small_inputs")[:, 0, :]
    c_all = g0[:, :d]
    w9 = jnp.transpose(g0[:, d:d + 9 * n_cv].reshape(N_DEV, 9, n_cv), (1, 0, 2)).reshape(9, N_DEV * n_cv)
    lb4 = jnp.transpose(g0[:, d + 9 * n_cv:].reshape(N_DEV, 4, n_lb), (1, 0, 2)).reshape(4, N_DEV * n_lb)
    lower = _lower_fwd(lb4)

    cs = jnp.concatenate([c_all, c_ctx.reshape(1, d), jnp.zeros((7, d), F32)], axis=0)
    b_cols = lax.dynamic_slice(b_mod, (0, me * n_mod), (1, n_mod))
    slab = _mod_fwd(cs, w_mod[0], b_cols)
    mod_all = jnp.transpose(_all_gather(slab, "gather_mod"), (1, 0, 2)).reshape(16, N_DEV * n_mod)
    mod_x = lax.dynamic_slice(mod_all, (me, 0), (1, 3 * d)).reshape(3, d)
    modp = jnp.stack([mod_all[8].reshape(3, d), mod_x])

    wt = _all_gather(w_in[0].T.astype(BF16), "gather_w_in").reshape(N_DEV * n_wi, d)
    wt_u = jnp.pad(wt, ((0, N_U - N_DEV * n_wi), (0, 0)))

    gx, recv_wi, recv_wo, small = _local_step(ctx[0], x[0], loss_target[0], modp, lower, wt_u, w_out[0].astype(BF16),
                                              w9, conv_b, ml_gate_b[0], hg_norm_w, ml_norm_w, ln_g, ln_b, True)
    g_wi, d_wi, nm_wi, nv_wi = [a.T for a in _rs_adamw(recv_wi, w_in[0].T, m_w_in[0].T, v_w_in[0].T, 256,
                                                       "adamw_w_in", by_cols=True)]
    g_wo, d_wo, nm_wo, nv_wo = _rs_adamw(recv_wo, w_out[0], m_w_out[0], v_w_out[0], 64, "adamw_w_out")

    packs = _all_gather(_pack_small(small), "gather_small_grads")
    total = _reduce8(packs, "reduce_small_grads")
    offs = _pack_offsets()
    piece = lambda name: total[offs[name][0]:offs[name][0] + offs[name][1]]
    g_bmod, g_lb0, g_lb1, loss8 = _small_finish(total, lower[0:2].reshape(16, LANE), float(d))

    ox = offs["dmodx"][0]
    dmodx_all = packs[:, ox:ox + 48, :].reshape(N_DEV, 3 * d)
    dmodc_tot = piece("dmodc").reshape(1, 3 * d)
    d9 = jnp.concatenate([dmodx_all, dmodc_tot, jnp.zeros((7, 3 * d), F32)], axis=0)
    d9_cols = lax.dynamic_slice(d9, (0, me * n_mod), (16, n_mod))
    g_wmod, pc_part = _mod_bwd_w(cs, d9_cols, w_mod[0])
    c_ctx8 = jnp.concatenate([c_ctx.reshape(1, d), jnp.zeros((7, d), F32)], axis=0)
    g_cctx = _cctx_grad(_all_gather(pc_part, "gather_cctx"), c_ctx8)[0]
    d_wmod, nm_wmod, nv_wmod = _adamw(w_mod[0], g_wmod, m_w_mod[0], v_w_mod[0], 256, "adamw_w_mod")

    g_convw_full = piece("gconvw").reshape(9, d)
    g_convw = lax.dynamic_slice(g_convw_full, (0, me * n_cv), (9, n_cv)).reshape(conv_w.shape)
    lb_full = jnp.stack([jnp.stack([g_lb0[0:8].reshape(-1), g_lb1[0:8].reshape(-1)]),
                         jnp.stack([g_lb0[8:16].reshape(-1), g_lb1[8:16].reshape(-1)])])
    g_hglb = lax.dynamic_slice(lb_full, (0, 0, me * n_lb), (2, 2, n_lb))
    grads = dict(
        c_ctx=g_cctx, b_mod=g_bmod.reshape(b_mod.shape), conv_w=g_convw, conv_b=piece("gconvb").reshape(conv_b.shape),
        hg_lb=g_hglb, ml_gate_b=piece("ggate")[0, :16].reshape(ml_gate_b.shape),
        hg_norm_w=piece("ghgw").reshape(hg_norm_w.shape), ml_norm_w=piece("gmlw").reshape(ml_norm_w.shape),
        ln_g=piece("glng").reshape(ln_g.shape), ln_b=piece("glnb").reshape(ln_b.shape))
    params = dict(c_ctx=(c_ctx, m_c_ctx, v_c_ctx), b_mod=(b_mod, m_b_mod, v_b_mod), conv_w=(conv_w, m_conv_w, v_conv_w),
                  conv_b=(conv_b, m_conv_b, v_conv_b), hg_lb=(hg_lb, m_hg_lb, v_hg_lb),
                  ml_gate_b=(ml_gate_b, m_ml_gate_b, v_ml_gate_b), hg_norm_w=(hg_norm_w, m_hg_norm_w, v_hg_norm_w),
                  ml_norm_w=(ml_norm_w, m_ml_norm_w, v_ml_norm_w), ln_g=(ln_g, m_ln_g, v_ln_g), ln_b=(ln_b, m_ln_b, v_ln_b))
    names = list(params)
    rows_of = {n: -(-params[n][0].size // LANE) for n in names}
    rows_tot = -(-sum(rows_of.values()) // 8) * 8
    cat = lambda arrs: jnp.concatenate(
        [_flat_pad(a, rows_of[n]) for n, a in zip(names, arrs)]
        + [jnp.ones((rows_tot - sum(rows_of.values()), LANE), F32)], axis=0)
    d_s, m_s, v_s = _adamw(cat([params[n][0] for n in names]), cat([grads[n] for n in names]),
                           cat([params[n][1] for n in names]), cat([params[n][2] for n in names]), rows_tot, "adamw_small")
    delta, new_m, new_v, off = {}, {}, {}, 0
    for n in names:
        shape, size = params[n][0].shape, params[n][0].size
        take = lambda a: a[off:off + rows_of[n]].reshape(-1)[:size].reshape(shape)
        delta[n], new_m[n], new_v[n] = take(d_s), take(m_s), take(v_s)
        off += rows_of[n]
    grads.update(w_mod=g_wmod[None], w_in=g_wi[None], w_out=g_wo[None])
    delta.update(w_mod=d_wmod[None], w_in=d_wi[None], w_out=d_wo[None])
    new_m.update(w_mod=nm_wmod[None], w_in=nm_wi[None], w_out=nm_wo[None])
    new_v.update(w_mod=nv_wmod[None], w_in=nv_wi[None], w_out=nv_wo[None])

    order = ("c_ctx", "w_mod", "b_mod", "w_in", "conv_w", "conv_b", "hg_lb", "ml_gate_b", "hg_norm_w", "ml_norm_w",
             "w_out", "ln_g", "ln_b")
    return (loss8[0, 0], gx[None], *[grads[n] for n in order], *[delta[n] for n in order],
            *[new_m[n] for n in order], *[new_v[n] for n in order])
```

```python
import functools

import jax
import jax.numpy as jnp
from jax import lax
from jax.experimental import pallas as pl
from jax.experimental.pallas import tpu as pltpu

F32 = jnp.float32
BF16 = jnp.bfloat16

D_MODEL = 2048
W_A = 1024
W_B = 1024
HG_HEADS = 8
HG_D = 128
ML_HEADS = 4
ML_D = 256
CHUNK = 64
N_IN = 10256
LANE = 128
N_U = 81 * LANE
N_DEV = 8
ALPHA = 2.0 ** 0.25
LN_EPS = 1e-5
NORM_EPS = 1e-6
ADAM_LR, ADAM_B1, ADAM_B2, ADAM_EPS, ADAM_WD, ADAM_STEP = 0.001, 0.9, 0.999, 1e-08, 0.01, 10
VMEM_CAP = 60 * 1024 * 1024

SEG_AQ, SEG_AFF, SEG_AFB, SEG_AI, SEG_AZ = range(5)
BLK_QK = 40
SEG_BV, SEG_BO, SEG_BZ = 7, 8, 9
BLK_GATE = 80

MESH = pl.DeviceIdType.MESH


def _vmem(nbytes):
    return pltpu.CompilerParams(vmem_limit_bytes=int(min(VMEM_CAP, max(nbytes, 16 * 1024 * 1024))))


def _sigmoid(x):
    return 1.0 / (1.0 + jnp.exp(-x))


def _silu(x):
    return x * _sigmoid(x)


def _dsilu(x):
    s = _sigmoid(x)
    return s * (1.0 + x * (1.0 - s))


def _silu_both(x):
    s = _sigmoid(x)
    return x * s, s * (1.0 + x * (1.0 - s))


def _bdot(a, b, dims):
    return lax.dot_general(a.astype(BF16), b.astype(BF16), (dims, ((), ())), preferred_element_type=F32)


def _nn(a, b):
    return _bdot(a, b, ((1,), (0,)))


def _nt(a, b):
    return _bdot(a, b, ((1,), (1,)))


def _tn(a, b):
    return _bdot(a, b, ((0,), (0,)))


def _exact_nn(a, b):
    return lax.dot_general(a, b, (((1,), (0,)), ((), ())), precision=lax.Precision.HIGHEST,
                           preferred_element_type=F32)


def _exact_tn(a, b):
    return lax.dot_general(a, b, (((0,), (0,)), ((), ())), precision=lax.Precision.HIGHEST,
                           preferred_element_type=F32)


def _tri(rev):
    t = lax.broadcasted_iota(jnp.int32, (CHUNK, CHUNK), 0)
    s = lax.broadcasted_iota(jnp.int32, (CHUNK, CHUNK), 1)
    return (s >= t) if rev else (s <= t)


def _eye():
    t = lax.broadcasted_iota(jnp.int32, (CHUNK, CHUNK), 0)
    s = lax.broadcasted_iota(jnp.int32, (CHUNK, CHUNK), 1)
    return (s == t).astype(F32)


def _row_to_col(row):
    return jnp.sum(_eye() * row, axis=1, keepdims=True)


def _last_onehot(rev):
    t = lax.broadcasted_iota(jnp.int32, (CHUNK, 1), 0)
    return (t == (0 if rev else CHUNK - 1)).astype(F32)


def _head_slices(width, n_heads):
    hd = width // n_heads
    return [slice(h * hd, (h + 1) * hd) for h in range(n_heads)]


def _scan_sum(x, rev):
    n = x.shape[0]
    t = lax.broadcasted_iota(jnp.int32, x.shape, 0)
    s = 1
    while s < n:
        if rev:
            x = x + jnp.where(t < n - s, pltpu.roll(x, n - s, 0), 0.0)
        else:
            x = x + jnp.where(t >= s, pltpu.roll(x, s, 0), 0.0)
        s *= 2
    return x


def _dot3(a, b, dims):
    a_hi, b_hi = a.astype(BF16), b.astype(BF16)
    a_lo, b_lo = (a - a_hi.astype(F32)).astype(BF16), (b - b_hi.astype(F32)).astype(BF16)
    dot = lambda x, y: lax.dot_general(x, y, (dims, ((), ())), preferred_element_type=F32)
    return dot(a_hi, b_hi) + (dot(a_hi, b_lo) + dot(a_lo, b_hi))


def _hg_common(zq, zf, lb, rev):
    q, dq_dz = _silu_both(zq)
    sg = _sigmoid(zf)
    f = lb + (1.0 - lb) * sg
    g = jnp.log(f)
    k = 1.0 - f
    b = _scan_sum(g, rev)
    b_last = jnp.sum(g, axis=0, keepdims=True)
    r = b[CHUNK // 2:CHUNK // 2 + 1, :]
    e_up = jnp.exp(b - r)
    e_dn = jnp.exp(r - b)
    e_b = e_up * jnp.exp(r)
    e_lb = e_dn * jnp.exp(b_last - r)
    return dict(q=q, dq_dz=dq_dz, sg=sg, f=f, k=k, e_up=e_up, e_dn=e_dn, e_b=e_b, e_lb=e_lb, e_last=jnp.exp(b_last),
                q_t=q * e_up, k_t=k * e_dn, q_s=q * e_b, k_h=k * e_lb, tri=_tri(rev).astype(F32))


def hg_chunk_fwd(zq, zf, v, lb, st, rev):
    c = _hg_common(zq, zf, lb, rev)
    hs = _head_slices(zq.shape[1], zq.shape[1] // HG_D)
    s = [_nt(c["q_t"][:, sl], c["k_t"][:, sl]) for sl in hs]
    oi = [_nt(c["q_s"][:, sl], st[sl, :]) for sl in hs]
    ds = [_tn(v[:, sl], c["k_h"][:, sl]) for sl in hs]
    oa = [_nn(c["tri"] * s_h, v[:, sl]) for s_h, sl in zip(s, hs)]
    o = jnp.concatenate([x + y for x, y in zip(oi, oa)], axis=1)
    st_new = jnp.concatenate([st[sl, :] * c["e_last"][:, sl] + d for sl, d in zip(hs, ds)], axis=0)
    return o, st_new


def hg_chunk_bwd(zq, zf, v, lb, st, do, dst_new, rev):
    c = _hg_common(zq, zf, lb, rev)
    hs = _head_slices(zq.shape[1], zq.shape[1] // HG_D)
    tri, q_t, k_t, q_s, k_h = c["tri"], c["q_t"], c["k_t"], c["q_s"], c["k_h"]
    s = [_nt(q_t[:, sl], k_t[:, sl]) for sl in hs]
    da = [tri * _nt(do[:, sl], v[:, sl]) for sl in hs]
    dq_s = [_nn(do[:, sl], st[sl, :]) for sl in hs]
    dk_h = [_nn(v[:, sl], dst_new[sl, :]) for sl in hs]
    dv_s = [_nt(k_h[:, sl], dst_new[sl, :]) for sl in hs]
    dst_q = [_tn(do[:, sl], q_s[:, sl]) for sl in hs]
    dq_t = [_dot3(da_h, k_t[:, sl], ((1,), (0,))) for da_h, sl in zip(da, hs)]
    dk_t = [_dot3(da_h, q_t[:, sl], ((0,), (0,))) for da_h, sl in zip(da, hs)]
    dv_a = [_tn(tri * s_h, do[:, sl]) for s_h, sl in zip(s, hs)]
    cat = lambda parts: jnp.concatenate(parts, axis=1)
    dq_s, dk_h, dq_t, dk_t = cat(dq_s), cat(dk_h), cat(dq_t), cat(dk_t)
    dv = cat([x + y for x, y in zip(dv_a, dv_s)])
    dst = jnp.concatenate([dst_new[sl, :] * c["e_last"][:, sl] + d for sl, d in zip(hs, dst_q)], axis=0)
    dq = dq_s * c["e_b"] + dq_t * c["e_up"]
    dk = dk_t * c["e_dn"] + dk_h * c["e_lb"]
    db = c["q"] * dq - c["k"] * dk
    ss = cat([jnp.sum(dst_new[sl, :] * st[sl, :], axis=0, keepdims=True) for sl in hs])
    d_all = jnp.sum(dk_h * k_h, axis=0, keepdims=True) + c["e_last"] * ss
    dg = _scan_sum(db, not rev) + d_all
    dzq = dq * c["dq_dz"]
    df = dg / c["f"] - dk
    dzf = df * (1.0 - lb) * c["sg"] * (1.0 - c["sg"])
    dlb = jnp.sum(df * (1.0 - c["sg"]), axis=0, keepdims=True)
    return dzq, dzf, dv, dlb, dst


def _log_sigmoid(x):
    return jnp.minimum(x, 0.0) - jnp.log(1.0 + jnp.exp(-jnp.abs(x)))


def _each(fn, *lists):
    return [fn(*xs) for xs in zip(*lists)]


def _bf(xs):
    return [x.astype(BF16) for x in xs]


def _ml_forward_parts(qp, kp, v, gates, c, n, m, rev, with_num):
    hs = _head_slices(qp.shape[1], qp.shape[1] // ML_D)
    q_all, dq_dp = _silu_both(qp)
    k_all, dk_dp = _silu_both(kp)
    k_all = k_all * (ML_D ** -0.5)
    q = [q_all[:, sl] for sl in hs]
    k = [k_all[:, sl] for sl in hs]
    vv = [v[:, sl] for sl in hs]
    cc = [c[sl, :] for sl in hs]
    tri_b = _tri(rev)
    tri = tri_b.astype(F32)
    tri_t = _tri(not rev).astype(F32)
    e_last = _last_onehot(rev)
    qb, kb, vb, cb = _bf(q), _bf(k), _bf(vv), _bf(cc)
    qk = _each(_nt, qb, kb)
    parts = []
    for (gi_c, gi_r, gf_c, gf_r), m_h in zip(gates, m):
        lf_c, lf_r = _log_sigmoid(gf_c), _log_sigmoid(gf_r)
        b_c = jnp.sum(tri * lf_r, axis=1, keepdims=True)
        b_r = jnp.sum(tri_t * lf_c, axis=0, keepdims=True)
        log_w = jnp.where(tri_b, b_c - b_r + gi_r, -jnp.inf)
        m_inter = b_c + m_h
        m_t = jnp.maximum(m_inter, jnp.max(log_w, axis=1, keepdims=True))
        m_new = jnp.sum(m_t * e_last, axis=0, keepdims=True)
        b_last = jnp.sum(b_c * e_last, axis=0, keepdims=True)
        parts.append(dict(a=jnp.exp(m_inter - m_t), p=jnp.exp(log_w - m_t), floor=jnp.exp(-m_t), m_new=m_new,
                          ws=jnp.exp(b_last - b_c + gi_c - m_new), decay=jnp.exp(b_last + m_h - m_new), gf_c=gf_c))
    w = [pt["p"] * x for pt, x in zip(parts, qk)]
    wb = _bf(w)
    for pt, q_h, n_h, w_h in zip(parts, q, n, w):
        qn = jnp.sum(q_h * n_h, axis=1, keepdims=True)
        den = pt["a"] * qn + jnp.sum(w_h, axis=1, keepdims=True)
        pt.update(qn=qn, den=den, rinv=1.0 / jnp.maximum(jnp.abs(den), pt["floor"]), w=w_h)
    if with_num:
        qc = _each(_nt, qb, cb)
        wv = _each(_nn, wb, vb)
        for pt, qc_h, wv_h in zip(parts, qc, wv):
            pt.update(num=pt["a"] * qc_h + wv_h)
    return hs, q, k, vv, cc, tri, parts, dict(q=qb, k=kb, v=vb, c=cb, w=wb, dq_dp=dq_dp, dk_dp=dk_dp)


def ml_chunk_fwd(qp, kp, v, gates, c, n, m, rev):
    hs, q, k, vv, cc, tri, parts, bf = _ml_forward_parts(qp, kp, v, gates, c, n, m, rev, True)
    h = jnp.concatenate([pt["num"] * pt["rinv"] for pt in parts], axis=1)
    upd = _each(_tn, [pt["ws"] * v_h for pt, v_h in zip(parts, vv)], bf["k"])
    c_new = jnp.concatenate([pt["decay"] * c_h + u for pt, c_h, u in zip(parts, cc, upd)], axis=0)
    n_new = [pt["decay"] * n_h + jnp.sum(pt["ws"] * k_h, axis=0, keepdims=True) for pt, n_h, k_h in zip(parts, n, k)]
    return h, c_new, n_new, [pt["m_new"] for pt in parts]


def ml_chunk_bwd(qp, kp, v, gates, c, n, m, h_out, dh, dc_new, dn_new, rev):
    hs, q, k, vv, cc, tri, parts, bf = _ml_forward_parts(qp, kp, v, gates, c, n, m, rev, False)
    dcn = [dc_new[sl, :] for sl in hs]
    dcb = _bf(dcn)
    dnum, dden = [], []
    for pt, sl in zip(parts, hs):
        dh_h = dh[:, sl]
        signed_live = jnp.where(jnp.abs(pt["den"]) > pt["floor"], jnp.where(pt["den"] >= 0.0, 1.0, -1.0), 0.0)
        dnum.append(dh_h * pt["rinv"])
        dden.append(-jnp.sum(dh_h * h_out[:, sl], axis=1, keepdims=True) * pt["rinv"] * signed_live)
    dnb = _bf(dnum)
    dw = [x + y for x, y in zip(_each(_nt, dnb, bf["v"]), dden)]
    kdc = _each(_nt, bf["k"], dcb)
    vdc = _each(_nn, bf["v"], dcb)
    dqk = [x * pt["p"] for x, pt in zip(dw, parts)]
    adn = [pt["a"] * x for pt, x in zip(parts, dnum)]
    dqkb, adnb = _bf(dqk), _bf(adn)
    dv_w = _each(_tn, bf["w"], dnb)
    dq_k = _each(_nn, dqkb, bf["k"])
    dq_c = _each(_nn, adnb, bf["c"])
    dk_q = _each(_tn, dqkb, bf["q"])
    dc_q = _each(_tn, adnb, bf["q"])
    dq, dk, dv, dgi, dgf, dc, dn = [], [], [], [], [], [], []
    for i, pt in enumerate(parts):
        a, ws, decay = pt["a"], pt["ws"], pt["decay"]
        add = a * dden[i]
        e = dw[i] * pt["w"]
        dv.append(dv_w[i] + ws * kdc[i])
        dq.append(dq_k[i] + dq_c[i] + add * n[i])
        dk.append(dk_q[i] + ws * vdc[i] + ws * dn_new[i])
        alpha = jnp.sum(q[i] * dq_c[i], axis=1, keepdims=True) + dden[i] * pt["qn"] * a
        omega = (jnp.sum(vdc[i] * k[i], axis=1, keepdims=True) + jnp.sum(k[i] * dn_new[i], axis=1, keepdims=True)) * ws
        delta = decay * (jnp.sum(jnp.sum(dcn[i] * cc[i], axis=1, keepdims=True), axis=0, keepdims=True)
                         + jnp.sum(dn_new[i] * n[i], axis=1, keepdims=True))
        dc.append(decay * dcn[i] + dc_q[i])
        dn.append(decay * dn_new[i] + jnp.sum(add * q[i], axis=0, keepdims=True))
        e_rows = jnp.sum(e, axis=1, keepdims=True)
        e_cols = _row_to_col(jnp.sum(e, axis=0, keepdims=True))
        dgi.append(e_cols + omega)
        db = e_rows + alpha - e_cols - omega
        tail = jnp.sum(omega, axis=0, keepdims=True) + delta
        dlf = _row_to_col(jnp.sum(tri * db, axis=0, keepdims=True)) + tail
        dgf.append(dlf * (1.0 - _sigmoid(pt["gf_c"])))
    cat = lambda xs: jnp.concatenate(xs, axis=1)
    dqp = cat(dq) * bf["dq_dp"]
    dkp = cat(dk) * (ML_D ** -0.5) * bf["dk_dp"]
    return dqp, dkp, cat(dv), dgi, dgf, jnp.concatenate(dc, axis=0), dn


def _pick(n, prefs):
    for p in prefs:
        if n % p == 0:
            return p
    raise ValueError(f"no tile for {n} among {prefs}")


def _position():
    return lax.axis_index("x"), lax.axis_index("y"), lax.axis_index("c")


class _Ride:
    def __init__(self, kind, x, cols=None, into=None):
        self.kind, self.x, self.cols, self.into = kind, x, cols, into
        r, c = x.shape[-2:]
        self.out_shape = jax.ShapeDtypeStruct((N_DEV, r, c if cols is None else cols[1]), x.dtype)
        self.width = c

    def _copies(self, x_ref, out_ref, send_sems, recv_sems, local_sem):
        px, py, pc = _position()
        me = 4 * px + 2 * py + pc
        src = (lambda slot: x_ref) if self.kind == "gather" else (lambda slot: x_ref.at[slot])
        dst = ((lambda slot: out_ref.at[slot]) if self.cols is None
               else (lambda slot: out_ref.at[slot, :, pl.ds(self.cols[0], self.width)]))
        mine = pltpu.make_async_copy(src(me), dst(me), local_sem)
        sends, recvs = [], []
        for k, (fx, fy, fc) in enumerate([(1, 0, 0), (0, 1, 0), (1, 1, 0), (1, 0, 1), (0, 1, 1), (1, 1, 1), (0, 0, 1)]):
            qx, qy, qc = (1 - px if fx else px), (1 - py if fy else py), (1 - pc if fc else pc)
            peer = 4 * qx + 2 * qy + qc
            sends.append(pltpu.make_async_remote_copy(
                src_ref=src(peer), dst_ref=dst(me), send_sem=send_sems.at[k], recv_sem=recv_sems.at[k],
                device_id=(qx, qy, qc), device_id_type=MESH))
            recvs.append(pltpu.make_async_remote_copy(
                src_ref=src(me), dst_ref=dst(peer), send_sem=send_sems.at[k], recv_sem=recv_sems.at[k],
                device_id=(qx, qy, qc), device_id_type=MESH))
        return mine, sends, recvs

    def start(self, *refs):
        mine, sends, _ = self._copies(*refs)
        mine.start()
        for cp in sends:
            cp.start()

    def wait(self, *refs):
        mine, sends, recvs = self._copies(*refs)
        for cp in recvs:
            cp.wait_recv()
        for cp in sends:
            cp.wait_send()
        mine.wait()

    def operands(self):
        return [self.x] + ([self.into] if self.into is not None else [])


_RIDE_SCRATCH = [pltpu.SemaphoreType.DMA((7,)), pltpu.SemaphoreType.DMA((7,)), pltpu.SemaphoreType.DMA]
_ANY = pl.BlockSpec(memory_space=pl.ANY)


def _mm(a, b, mode, out_dtype, tm, tn, tk, name, ride=None, b_cols=None, m_out=None):
    if mode == "nn":
        (m, k), (k2, n) = a.shape, b.shape
    elif mode == "nt":
        (m, k), (n, k2) = a.shape, b.shape
    else:
        (k, m), (k2, n) = a.shape, b.shape
    off, n = (0, n) if b_cols is None else b_cols
    assert k == k2 and m % tm == 0 and n % tn == 0 and k % tk == 0, (a.shape, b.shape, mode, tm, tn, tk)
    assert b_cols is None or (mode != "nt" and off % LANE == 0)
    nk = k // tk
    dims = {"nn": ((1,), (0,)), "nt": ((1,), (1,)), "tn": ((0,), (0,))}[mode]
    a_spec = (pl.BlockSpec((tk, tm), lambda j, i, kk: (kk, i)) if mode == "tn"
              else pl.BlockSpec((tm, tk), lambda j, i, kk: (i, kk)))
    if b_cols is not None:
        b_spec = pl.BlockSpec((pl.Element(tk), pl.Element(tn)),
                              lambda j, i, kk: (pl.multiple_of(kk * tk, LANE), pl.multiple_of(off + j * tn, LANE)))
    elif mode == "nt":
        b_spec = pl.BlockSpec((tn, tk), lambda j, i, kk: (j, kk))
    else:
        b_spec = pl.BlockSpec((tk, tn), lambda j, i, kk: (kk, j))

    grid = (n // tn, m // tm, nk)
    n_ride_in = len(ride.operands()) if ride is not None else 0

    def body(a_ref, b_ref, *rest):
        if ride is not None:
            x_ref = rest[0]
            o_ref, got_ref, acc_ref = rest[n_ride_in:n_ride_in + 3]
            comm = (x_ref, got_ref) + tuple(rest[n_ride_in + 3:])
        else:
            o_ref, acc_ref = rest
        kk = pl.program_id(2)
        step = (pl.program_id(0) * grid[1] + pl.program_id(1)) * nk + kk
        if ride is not None:
            @pl.when(step == 0)
            def _():
                ride.start(*comm)

        part = lax.dot_general(a_ref[...], b_ref[...], (dims, ((), ())), preferred_element_type=F32)
        if nk == 1:
            o_ref[...] = part.astype(o_ref.dtype)
        else:
            @pl.when(kk == 0)
            def _():
                acc_ref[...] = part

            @pl.when(jnp.logical_and(kk > 0, kk < nk - 1))
            def _():
                acc_ref[...] += part

            @pl.when(kk == nk - 1)
            def _():
                o_ref[...] = (acc_ref[...] + part).astype(o_ref.dtype)

        if ride is not None:
            @pl.when(step == grid[0] * grid[1] * nk - 1)
            def _():
                ride.wait(*comm)

    osz = jnp.dtype(out_dtype).itemsize
    need = 2 * (tm * tk * a.dtype.itemsize + tk * tn * b.dtype.itemsize + tm * tn * osz) + tm * tn * 4
    o_spec = pl.BlockSpec((tm, tn), lambda j, i, kk: (i, j))
    o_shape = jax.ShapeDtypeStruct((m if m_out is None else m_out, n), out_dtype)
    extra = ride is not None
    return pl.pallas_call(
        body, name=name, grid=grid,
        in_specs=[a_spec, b_spec] + [_ANY] * n_ride_in,
        out_specs=[o_spec, _ANY] if extra else o_spec,
        out_shape=[o_shape, ride.out_shape] if extra else o_shape,
        scratch_shapes=[pltpu.VMEM((tm, tn) if nk > 1 else (8, LANE), F32)] + (_RIDE_SCRATCH if extra else []),
        input_output_aliases={3: 1} if extra and ride.into is not None else {},
        compiler_params=_vmem(need + (12 << 20)),
    )(a, b, *(ride.operands() if extra else []))


ROWS = 256


def _ln_stats(x):
    mu = jnp.mean(x, axis=-1, keepdims=True)
    xc = x - mu
    var = jnp.mean(xc * xc, axis=-1, keepdims=True)
    rstd = lax.rsqrt(var + LN_EPS)
    return xc * rstd, rstd


def _token_specs(nbc, nbx, d):
    return [pl.BlockSpec((ROWS, d), lambda i: (jnp.minimum(i, nbc - 1), 0)),
            pl.BlockSpec((ROWS, d), lambda i: (jnp.maximum(i - nbc, 0), 0))]


def _tokens(c_ref, x_ref, nbc):
    return jnp.where(pl.program_id(0) < nbc, c_ref[...], x_ref[...])


def _modulate_fwd(ctx, x, modp):
    d = x.shape[1]
    nbc, nbx = ctx.shape[0] // ROWS, x.shape[0] // ROWS

    def body(c_ref, x_ref, mod_ref, o_ref):
        n, _ = _ln_stats(_tokens(c_ref, x_ref, nbc))
        o_ref[...] = (n * (1.0 + mod_ref[0, 1:2, :]) + mod_ref[0, 0:1, :]).astype(BF16)

    return pl.pallas_call(
        body, name="modulate_fwd", grid=(nbc + nbx,),
        in_specs=_token_specs(nbc, nbx, d) + [pl.BlockSpec((1, 3, d), lambda i: (jnp.where(i >= nbc, 1, 0), 0, 0))],
        out_specs=pl.BlockSpec((ROWS, d), lambda i: (i, 0)),
        out_shape=jax.ShapeDtypeStruct((ctx.shape[0] + x.shape[0], d), BF16),
    )(ctx, x, modp)


def _modulate_bwd(dh, ctx, x, modp, dxa):
    t, d = x.shape
    nbc, nbx = ctx.shape[0] // ROWS, t // ROWS

    def body(dh_ref, c_ref, x_ref, mod_ref, dxa_ref, gx_ref, sum_ref):
        i = pl.program_id(0)
        n, rstd = _ln_stats(_tokens(c_ref, x_ref, nbc))
        g = dh_ref[...]
        dn = g * (1.0 + mod_ref[0, 1:2, :])
        dx = rstd * (dn - jnp.mean(dn, axis=-1, keepdims=True) - n * jnp.mean(dn * n, axis=-1, keepdims=True))
        gx_ref[...] = dx + dxa_ref[...]
        dshift = jnp.sum(g, axis=0, keepdims=True)
        dscale = jnp.sum(g * n, axis=0, keepdims=True)

        @pl.when(i == 0)
        def _():
            sum_ref[...] = jnp.zeros_like(sum_ref)

        @pl.when(i < nbc)
        def _():
            sum_ref[0:1, :] += dshift
            sum_ref[1:2, :] += dscale

        @pl.when(i >= nbc)
        def _():
            sum_ref[2:3, :] += dshift
            sum_ref[3:4, :] += dscale

    lat = lambda i: (jnp.maximum(i - nbc, 0), 0)
    return pl.pallas_call(
        body, name="modulate_bwd", grid=(nbc + nbx,),
        in_specs=[pl.BlockSpec((ROWS, d), lambda i: (i, 0))] + _token_specs(nbc, nbx, d)
        + [pl.BlockSpec((1, 3, d), lambda i: (jnp.where(i >= nbc, 1, 0), 0, 0)), pl.BlockSpec((ROWS, d), lat)],
        out_specs=[pl.BlockSpec((ROWS, d), lat), pl.BlockSpec((8, d), lambda i: (0, 0))],
        out_shape=[jax.ShapeDtypeStruct((t, d), F32), jax.ShapeDtypeStruct((8, d), F32)],
    )(dh, ctx, x, modp, dxa)


def _post_fwd(o_f, o_b, h_f, h_b, u, hgw, mlw, nbc):
    tt = u.shape[0]
    t = tt - nbc * ROWS

    def body(of_ref, ob_ref, hf_ref, hb_ref, az_ref, bo_ref, bz_ref, hgw_ref, mlw_ref, y_ref):
        o = of_ref[...] + ob_ref[...]
        for sl in _head_slices(W_A, HG_HEADS):
            oh = o[:, sl]
            rs = lax.rsqrt(jnp.mean(oh * oh, axis=-1, keepdims=True) + NORM_EPS)
            y_ref[:, sl] = (oh * rs * hgw_ref[:, sl] * _silu(az_ref[:, sl])).astype(BF16)
        hm = hf_ref[...] + hb_ref[...]
        for sl in _head_slices(W_B, ML_HEADS):
            hh = hm[:, sl]
            mu = jnp.mean(hh, axis=-1, keepdims=True)
            hc = hh - mu
            rstd = lax.rsqrt(jnp.mean(hc * hc, axis=-1, keepdims=True) + NORM_EPS)
            out = hc * rstd * mlw_ref[:, sl] * _sigmoid(bo_ref[:, sl]) * _silu(bz_ref[:, sl])
            y_ref[:, W_A + sl.start:W_A + sl.stop] = out.astype(BF16)

    row = lambda i: (i + nbc, 0)
    seg = lambda s: pl.BlockSpec((ROWS, 1024), lambda i: (i + nbc, s))
    wspec = pl.BlockSpec((1, 1024), lambda i: (0, 0))
    return pl.pallas_call(
        body, name="post_fwd", grid=(t // ROWS,),
        in_specs=[pl.BlockSpec((ROWS, 1024), row)] * 4 + [seg(SEG_AZ), seg(SEG_BO), seg(SEG_BZ), wspec, wspec],
        out_specs=pl.BlockSpec((ROWS, 2048), lambda i: (i, 0)),
        out_shape=jax.ShapeDtypeStruct((t, 2048), BF16),
    )(o_f, o_b, h_f, h_b, u, u, u, hgw, mlw)


def _post_bwd(dz, w_o, o_f, o_b, h_f, h_b, u, hgw, mlw, nbc):
    tt = u.shape[0]
    d = w_o.shape[0]

    def body(dz_ref, w_ref, of_ref, ob_ref, hf_ref, hb_ref, az_ref, bo_ref, bz_ref, hgw_ref, mlw_ref,
             do_ref, dhm_ref, daz_ref, dbo_ref, sum_ref):
        i = pl.program_id(0)
        live = jnp.where(i >= nbc, 1.0, 0.0)
        dy = lax.dot_general(dz_ref[...], w_ref[...], (((1,), (1,)), ((), ())), preferred_element_type=F32) * live

        @pl.when(i == 0)
        def _():
            sum_ref[...] = jnp.zeros_like(sum_ref)

        o = of_ref[...] + ob_ref[...]
        for sl in _head_slices(W_A, HG_HEADS):
            oh = o[:, sl]
            rs = lax.rsqrt(jnp.mean(oh * oh, axis=-1, keepdims=True) + NORM_EPS)
            on = oh * rs
            az = az_ref[:, sl]
            dya = dy[:, sl]
            saz, daz = _silu_both(az)
            doa = dya * saz
            daz_ref[:, sl] = (dya * on * hgw_ref[:, sl] * daz).astype(BF16)
            sum_ref[0:1, sl] += jnp.sum(doa * on, axis=0, keepdims=True)
            don = doa * hgw_ref[:, sl]
            do_ref[:, sl] = rs * (don - on * jnp.mean(don * on, axis=-1, keepdims=True))
        hm = hf_ref[...] + hb_ref[...]
        for sl in _head_slices(W_B, ML_HEADS):
            hh = hm[:, sl]
            mu = jnp.mean(hh, axis=-1, keepdims=True)
            hc = hh - mu
            rstd = lax.rsqrt(jnp.mean(hc * hc, axis=-1, keepdims=True) + NORM_EPS)
            hn = hc * rstd
            hw = hn * mlw_ref[:, sl]
            bo, bz = bo_ref[:, sl], bz_ref[:, sl]
            sbo = _sigmoid(bo)
            sbz, dbz = _silu_both(bz)
            dyb = dy[:, W_A + sl.start:W_A + sl.stop]
            dhw = dyb * sbo * sbz
            dbo_ref[:, sl] = (dyb * hw * sbz * sbo * (1.0 - sbo)).astype(BF16)
            dbo_ref[:, 1024 + sl.start:1024 + sl.stop] = (dyb * hw * sbo * dbz).astype(BF16)
            sum_ref[1:2, sl] += jnp.sum(dhw * hn, axis=0, keepdims=True)
            dhn = dhw * mlw_ref[:, sl]
            dhm_ref[:, sl] = rstd * (dhn - jnp.mean(dhn, axis=-1, keepdims=True)
                                     - hn * jnp.mean(dhn * hn, axis=-1, keepdims=True))

    row = lambda i: (i, 0)
    seg = lambda s: pl.BlockSpec((ROWS, 1024), lambda i: (i, s))
    wspec = pl.BlockSpec((1, 1024), lambda i: (0, 0))
    return pl.pallas_call(
        body, name="post_bwd", grid=(tt // ROWS,),
        in_specs=[pl.BlockSpec((ROWS, 2048), lambda i: (jnp.maximum(i - nbc, 0), 0)), pl.BlockSpec((d, d), lambda i: (0, 0))]
        + [pl.BlockSpec((ROWS, 1024), row)] * 4 + [seg(SEG_AZ), seg(SEG_BO), seg(SEG_BZ), wspec, wspec],
        out_specs=[pl.BlockSpec((ROWS, 1024), row), pl.BlockSpec((ROWS, 1024), row),
                   pl.BlockSpec((ROWS, 1024), row), pl.BlockSpec((ROWS, 2048), row),
                   pl.BlockSpec((8, 1024), lambda i: (0, 0))],
        out_shape=[jax.ShapeDtypeStruct((tt, 1024), F32), jax.ShapeDtypeStruct((tt, 1024), F32),
                   jax.ShapeDtypeStruct((tt, 1024), BF16), jax.ShapeDtypeStruct((tt, 2048), BF16),
                   jax.ShapeDtypeStruct((8, 1024), F32)],
        compiler_params=_vmem(4 * d * d + 30 * ROWS * 2048 * 4),
    )(dz, w_o, o_f, o_b, h_f, h_b, u, u, u, hgw, mlw)


def _final(y, w_o, x, target, modp, ln_g, ln_b):
    t, d = x.shape

    def body(y_ref, w_ref, x_ref, tg_ref, mod_ref, g_ref, b_ref, dz_ref, dxa_ref, sum_ref):
        i = pl.program_id(0)
        zz = lax.dot_general(y_ref[...], w_ref[...], (((1,), (0,)), ((), ())), preferred_element_type=F32)
        gate = mod_ref[0, 2:3, :]
        pre = ALPHA * x_ref[...] + gate * zz
        nh, rstd = _ln_stats(pre)
        err = nh * g_ref[...] + b_ref[...] - tg_ref[...]
        dxo = err * (1.0 / d)
        dnh = dxo * g_ref[...]
        dpre = rstd * (dnh - jnp.mean(dnh, axis=-1, keepdims=True) - nh * jnp.mean(dnh * nh, axis=-1, keepdims=True))
        dz_ref[...] = (gate * dpre).astype(BF16)
        dxa_ref[...] = ALPHA * dpre

        @pl.when(i == 0)
        def _():
            sum_ref[...] = jnp.zeros_like(sum_ref)

        sum_ref[0:1, :] += jnp.sum(dpre * zz, axis=0, keepdims=True)
        sum_ref[1:2, :] += jnp.sum(dxo * nh, axis=0, keepdims=True)
        sum_ref[2:3, :] += jnp.sum(dxo, axis=0, keepdims=True)
        sum_ref[3:4, :] += jnp.sum(err * err, axis=0, keepdims=True)

    row = lambda i: (i, 0)
    vec = pl.BlockSpec((1, d), lambda i: (0, 0))
    return pl.pallas_call(
        body, name="final_ln_loss", grid=(t // ROWS,),
        in_specs=[pl.BlockSpec((ROWS, d), row), pl.BlockSpec((d, d), lambda i: (0, 0)), pl.BlockSpec((ROWS, d), row),
                  pl.BlockSpec((ROWS, d), row), pl.BlockSpec((1, 3, d), lambda i: (1, 0, 0)), vec, vec],
        out_specs=[pl.BlockSpec((ROWS, d), row), pl.BlockSpec((ROWS, d), row), pl.BlockSpec((8, d), lambda i: (0, 0))],
        out_shape=[jax.ShapeDtypeStruct((t, d), BF16), jax.ShapeDtypeStruct((t, d), F32),
                   jax.ShapeDtypeStruct((8, d), F32)],
        compiler_params=_vmem(4 * d * d + 24 * ROWS * d * 4),
    )(y, w_o, x, target, modp, ln_g, ln_b)


GRID_W = 64


def _shift(x, s, ok):
    n = x.shape[0]
    return jnp.where(ok, pltpu.roll(x, s % n, 0), 0.0)


def _grid_masks(n):
    t = lax.broadcasted_iota(jnp.int32, (n, LANE), 0)
    col = t & (GRID_W - 1)
    return dict(left=col >= 1, right=col <= GRID_W - 2, up=t >= GRID_W, down=t < n - GRID_W)


def _seq_masks(n):
    t = lax.broadcasted_iota(jnp.int32, (n, LANE), 0)
    return dict(left=t >= 1, right=t <= n - 2)


def _conv_fwd(u, w9, cb, tc):
    tt = u.shape[0]
    t = tt - tc

    def body(u_ref, w_ref, b_ref, o_ref):
        w = [w_ref[r:r + 1, :] for r in range(9)]
        xc = u_ref[0:tc, :]
        ms = _seq_masks(tc)
        o_ref[0:tc, :] = (w[3] * _shift(xc, 1, ms["left"]) + w[4] * xc + w[5] * _shift(xc, -1, ms["right"])
                          + b_ref[...])
        x = u_ref[tc:tt, :]
        mg = _grid_masks(t)
        taps = (_shift(x, 1, mg["left"]), x, _shift(x, -1, mg["right"]))
        rows = [w[3 * i] * taps[0] + w[3 * i + 1] * taps[1] + w[3 * i + 2] * taps[2] for i in range(3)]
        o_ref[tc:tt, :] = (rows[1] + _shift(rows[0], GRID_W, mg["up"]) + _shift(rows[2], -GRID_W, mg["down"])
                           + b_ref[...])

    return pl.pallas_call(
        body, name="conv_fwd", grid=(2048 // LANE,),
        in_specs=[pl.BlockSpec((tt, LANE), lambda j: (0, BLK_QK + j)), pl.BlockSpec((9, LANE), lambda j: (0, j)),
                  pl.BlockSpec((1, LANE), lambda j: (0, j))],
        out_specs=pl.BlockSpec((tt, LANE), lambda j: (0, j)),
        out_shape=jax.ShapeDtypeStruct((tt, 2048), F32),
        compiler_params=_vmem(40 * tt * LANE * 4),
    )(u, w9, cb)


def _conv_bwd(dcp, u, w9, tc, du):
    tt = u.shape[0]
    t = tt - tc

    def body(d_ref, u_ref, w_ref, du_in_ref, du_ref, gw_ref, gb_ref):
        w = [w_ref[r:r + 1, :] for r in range(9)]
        csum = lambda a: jnp.sum(a, axis=0, keepdims=True)
        dc = d_ref[0:tc, :]
        xc = u_ref[0:tc, :]
        ms = _seq_masks(tc)
        du_ref[0:tc, :] = (w[3] * _shift(dc, -1, ms["right"]) + w[4] * dc + w[5] * _shift(dc, 1, ms["left"])).astype(BF16)
        gmid = [csum(dc * _shift(xc, 1, ms["left"])), csum(dc * xc), csum(dc * _shift(xc, -1, ms["right"]))]
        d = d_ref[tc:tt, :]
        x = u_ref[tc:tt, :]
        mg = _grid_masks(t)
        dtaps = (_shift(d, -1, mg["right"]), d, _shift(d, 1, mg["left"]))
        rows = [w[3 * i] * dtaps[0] + w[3 * i + 1] * dtaps[1] + w[3 * i + 2] * dtaps[2] for i in range(3)]
        du_ref[tc:tt, :] = (rows[1] + _shift(rows[0], -GRID_W, mg["down"]) + _shift(rows[2], GRID_W, mg["up"])).astype(BF16)
        xtaps = (_shift(x, 1, mg["left"]), x, _shift(x, -1, mg["right"]))
        for j in range(3):
            gw_ref[j:j + 1, :] = csum(d * _shift(xtaps[j], GRID_W, mg["up"]))
            gw_ref[3 + j:4 + j, :] = csum(d * xtaps[j]) + gmid[j]
            gw_ref[6 + j:7 + j, :] = csum(d * _shift(xtaps[j], -GRID_W, mg["down"]))
        gb_ref[...] = csum(d) + csum(dc)

    return pl.pallas_call(
        body, name="conv_bwd", grid=(2048 // LANE,),
        in_specs=[pl.BlockSpec((tt, LANE), lambda j: (0, j)), pl.BlockSpec((tt, LANE), lambda j: (0, BLK_QK + j)),
                  pl.BlockSpec((9, LANE), lambda j: (0, j)), _ANY],
        out_specs=[pl.BlockSpec((tt, LANE), lambda j: (0, BLK_QK + j)), pl.BlockSpec((9, LANE), lambda j: (0, j)),
                   pl.BlockSpec((1, LANE), lambda j: (0, j))],
        out_shape=[jax.ShapeDtypeStruct(du.shape, BF16), jax.ShapeDtypeStruct((9, 2048), F32),
                   jax.ShapeDtypeStruct((1, 2048), F32)],
        input_output_aliases={3: 0},
        compiler_params=_vmem(48 * tt * LANE * 4),
    )(dcp, u, w9, du)


SUB = 2
STEP = SUB * CHUNK


def _chunk_of(pos, ncc, nc, rev):
    if not rev:
        return pos
    return jnp.where(pos < ncc, ncc - 1 - pos, nc - 1 - (pos - ncc))


def _sub_rows(rev):
    order = range(SUB - 1, -1, -1) if rev else range(SUB)
    return [(s, slice(s * CHUNK, (s + 1) * CHUNK)) for s in order]


def _hgrn_fwd(u, lower_d, ncc, rev):
    tt = u.shape[0]
    nc, ncc = tt // STEP, ncc // SUB
    seg_f = SEG_AFB if rev else SEG_AFF

    def body(zq_ref, zf_ref, v_ref, lb_ref, o_ref, hist_ref, st_ref):
        @pl.when(pl.program_id(0) == 0)
        def _():
            st_ref[...] = jnp.zeros_like(st_ref)

        st = st_ref[...]
        for s, r in _sub_rows(rev):
            hist_ref[s] = st
            o, st = hg_chunk_fwd(zq_ref[r, :], zf_ref[r, :], v_ref[r, :], lb_ref[...], st, rev)
            o_ref[r, :] = o
        st_ref[...] = st

    seg = lambda s: pl.BlockSpec((STEP, 1024), lambda j: (_chunk_of(j, ncc, nc, rev), s))
    return pl.pallas_call(
        body, name="hgrn_fwd_rev" if rev else "hgrn_fwd", grid=(nc,),
        in_specs=[seg(SEG_AQ), seg(seg_f), seg(SEG_AI), pl.BlockSpec((1, 1024), lambda j: (0, 0))],
        out_specs=[pl.BlockSpec((STEP, 1024), lambda j: (_chunk_of(j, ncc, nc, rev), 0)),
                   pl.BlockSpec((SUB, 1024, HG_D), lambda j: (_chunk_of(j, ncc, nc, rev), 0, 0))],
        out_shape=[jax.ShapeDtypeStruct((tt, 1024), F32), jax.ShapeDtypeStruct((nc * SUB, 1024, HG_D), F32)],
        scratch_shapes=[pltpu.VMEM((1024, HG_D), F32)],
    )(u, u, u, lower_d)


def _hgrn_bwd(u, lower_d, hist, do, ncc, rev, ride=None, final=None):
    tt = u.shape[0]
    nc, ncc = tt // STEP, ncc // SUB
    seg_f = SEG_AFB if rev else SEG_AFF
    is_final = final is not None
    has_a2a = ride is not None
    n_out = 2 if is_final else 4
    width = 5 * 1024

    def body(zq_ref, zf_ref, v_ref, lb_ref, hist_ref, do_ref, *rest):
        if is_final:
            aq_ref, av_ref, af_ref, az_ref = rest[:4]
            rest = rest[4:]
        if has_a2a:
            x_ref, rest = rest[0], rest[1:]
        outs, rest = rest[:n_out], rest[n_out:]
        dlb_ref = outs[-1]
        if has_a2a:
            comm = (x_ref, rest[0]) + tuple(rest[2:])
            dst_ref = rest[1]
        else:
            dst_ref = rest[0]

        @pl.when(pl.program_id(0) == 0)
        def _():
            dst_ref[...] = jnp.zeros_like(dst_ref)
            dlb_ref[...] = jnp.zeros_like(dlb_ref)
            if has_a2a:
                ride.start(*comm)

        dst = dst_ref[...]
        dlb_sum = dlb_ref[...]
        for s, r in reversed(_sub_rows(rev)):
            dzq, dzf, dv, dlb, dst = hg_chunk_bwd(zq_ref[r, :], zf_ref[r, :], v_ref[r, :], lb_ref[...],
                                                  hist_ref[s], do_ref[r, :], dst, rev)
            dlb_sum = dlb_sum + dlb
            if is_final:
                du_ref = outs[0]
                dzf_own, dzf_other = dzf.astype(BF16), af_ref[r, :]
                du_ref[r, 0:1024] = (dzq + aq_ref[r, :]).astype(BF16)
                du_ref[r, 1024:2048] = dzf_other if rev else dzf_own
                du_ref[r, 2048:3072] = dzf_own if rev else dzf_other
                du_ref[r, 3072:4096] = (dv + av_ref[r, :]).astype(BF16)
                du_ref[r, 4096:5120] = az_ref[r, :]
            else:
                dzf_ref, dzq_ref, dv_ref = outs[:3]
                dzf_ref[r, :] = dzf.astype(BF16)
                dzq_ref[r, :] = dzq
                dv_ref[r, :] = dv
        dst_ref[...] = dst
        dlb_ref[...] = dlb_sum

        if has_a2a:
            @pl.when(pl.program_id(0) == nc - 1)
            def _():
                ride.wait(*comm)

    cidx = lambda j: _chunk_of(nc - 1 - j, ncc, nc, rev)
    seg = lambda s: pl.BlockSpec((STEP, 1024), lambda j: (cidx(j), s))
    row = pl.BlockSpec((STEP, 1024), lambda j: (cidx(j), 0))
    dlb_spec = pl.BlockSpec((1, 1024), lambda j: (0, 0))
    dlb_shape = jax.ShapeDtypeStruct((1, 1024), F32)
    if is_final:
        out_specs = [pl.BlockSpec((STEP, width), lambda j: (cidx(j), 0)), dlb_spec]
        out_shape = [jax.ShapeDtypeStruct((tt, N_U), BF16), dlb_shape]
    else:
        out_specs = [row, row, row, dlb_spec]
        out_shape = [jax.ShapeDtypeStruct((tt, 1024), BF16), jax.ShapeDtypeStruct((tt, 1024), F32),
                     jax.ShapeDtypeStruct((tt, 1024), F32), dlb_shape]
    ins = [u, u, u, lower_d, hist, do] + (list(final) if is_final else []) + ([ride.x] if has_a2a else [])
    return pl.pallas_call(
        body, name="hgrn_bwd_rev" if rev else "hgrn_bwd", grid=(nc,),
        in_specs=[seg(SEG_AQ), seg(seg_f), seg(SEG_AI), pl.BlockSpec((1, 1024), lambda j: (0, 0)),
                  pl.BlockSpec((SUB, 1024, HG_D), lambda j: (cidx(j), 0, 0)), row] + ([row] * 4 if is_final else [])
        + ([_ANY] if has_a2a else []),
        out_specs=out_specs + ([_ANY] if has_a2a else []),
        out_shape=out_shape + ([ride.out_shape] if has_a2a else []),
        scratch_shapes=[pltpu.VMEM((1024, HG_D), F32)] + (_RIDE_SCRATCH if has_a2a else []),
    )(*ins)


def _gate_views(g_ref, b_ref, r, head, rev):
    gc = g_ref[r, :] + b_ref[...]
    lane = lax.broadcasted_iota(jnp.int32, (1, LANE), 1)
    eye = _eye()
    d = 1 if rev else 0
    ii, fi = d * ML_HEADS + head, 2 * ML_HEADS + d * ML_HEADS + head
    col = lambda idx: jnp.sum(jnp.where(lane == idx, gc, 0.0), axis=1, keepdims=True)
    row = lambda c: jnp.sum(eye * c, axis=0, keepdims=True)
    gi, gf = col(ii), col(fi)
    return gi, row(gi), gf, row(gf)


def _mlstm_fwd(cpre, u, bias, ncc, rev):
    tt = u.shape[0]
    nc, ncc = tt // STEP, ncc // SUB
    nhd = ML_HEADS

    def body(q_ref, k_ref, v_ref, g_ref, b_ref, h_ref, ch_ref, nh_ref, mh_ref, c_ref, n_ref, m_ref):
        @pl.when(pl.program_id(0) == 0)
        def _():
            c_ref[...] = jnp.zeros_like(c_ref)
            n_ref[...] = jnp.zeros_like(n_ref)
            m_ref[...] = jnp.zeros_like(m_ref)

        c, n_all, m_all = c_ref[...], n_ref[...], m_ref[...]
        n = [n_all[hd:hd + 1, :] for hd in range(nhd)]
        m = [m_all[hd:hd + 1, 0:1] for hd in range(nhd)]
        for s, r in _sub_rows(rev):
            ch_ref[s] = c
            for hd in range(nhd):
                nh_ref[s, hd:hd + 1, :] = n[hd]
                mh_ref[s, hd:hd + 1, :] = jnp.broadcast_to(m[hd], (1, LANE))
            gates = [_gate_views(g_ref, b_ref, r, hd, rev) for hd in range(nhd)]
            h, c, n, m = ml_chunk_fwd(q_ref[r, :], k_ref[r, :], v_ref[r, :], gates, c, n, m, rev)
            h_ref[r, :] = h
        c_ref[...] = c
        for hd in range(nhd):
            n_ref[hd:hd + 1, :] = n[hd]
            m_ref[hd:hd + 1, :] = jnp.broadcast_to(m[hd], (1, LANE))

    cidx = lambda j: _chunk_of(j, ncc, nc, rev)
    row = lambda s: pl.BlockSpec((STEP, 1024), lambda j: (cidx(j), s))
    st3 = lambda a, b: pl.BlockSpec((SUB, a, b), lambda j: (cidx(j), 0, 0))
    return pl.pallas_call(
        body, name="mlstm_fwd_rev" if rev else "mlstm_fwd", grid=(nc,),
        in_specs=[row(0), row(1), row(SEG_BV), pl.BlockSpec((STEP, LANE), lambda j: (cidx(j), BLK_GATE)),
                  pl.BlockSpec((1, LANE), lambda j: (0, 0))],
        out_specs=[row(0), st3(1024, ML_D), st3(8, ML_D), st3(8, LANE)],
        out_shape=[jax.ShapeDtypeStruct((tt, 1024), F32), jax.ShapeDtypeStruct((nc * SUB, 1024, ML_D), F32),
                   jax.ShapeDtypeStruct((nc * SUB, 8, ML_D), F32), jax.ShapeDtypeStruct((nc * SUB, 8, LANE), F32)],
        scratch_shapes=[pltpu.VMEM((1024, ML_D), F32), pltpu.VMEM((8, ML_D), F32), pltpu.VMEM((8, LANE), F32)],
    )(cpre, cpre, u, u, bias)


def _mlstm_bwd(cpre, u, bias, chist, nhist, mhist, h_out, dh, ncc, rev, final=None):
    tt = u.shape[0]
    nc, ncc = tt // STEP, ncc // SUB
    nhd = ML_HEADS
    is_final = final is not None
    d = 1 if rev else 0
    col0, width = SEG_BV * 1024, N_U - SEG_BV * 1024

    def body(q_ref, k_ref, v_ref, g_ref, b_ref, ch_ref, nh_ref, mh_ref, ho_ref, dh_ref, *rest):
        if is_final:
            aqk_ref, av_ref, ag_ref, bo_ref = rest[:4]
            dqk_ref, du_ref, gs_ref, dc_ref, dn_ref = rest[5:]
        else:
            dqk_ref, dv_ref, dg_ref, gs_ref, dc_ref, dn_ref = rest

        @pl.when(pl.program_id(0) == 0)
        def _():
            dc_ref[...] = jnp.zeros_like(dc_ref)
            dn_ref[...] = jnp.zeros_like(dn_ref)
            gs_ref[...] = jnp.zeros_like(gs_ref)

        lane = lax.broadcasted_iota(jnp.int32, (1, LANE), 1)
        dc, dn_all, gs = dc_ref[...], dn_ref[...], gs_ref[...]
        dn = [dn_all[hd:hd + 1, :] for hd in range(nhd)]
        for s, r in reversed(_sub_rows(rev)):
            gates = [_gate_views(g_ref, b_ref, r, hd, rev) for hd in range(nhd)]
            n_all, m_all = nh_ref[s], mh_ref[s]
            dqp, dkp, dv, dgi, dgf, dc, dn = ml_chunk_bwd(
                q_ref[r, :], k_ref[r, :], v_ref[r, :], gates, ch_ref[s],
                [n_all[hd:hd + 1, :] for hd in range(nhd)], [m_all[hd:hd + 1, 0:1] for hd in range(nhd)],
                ho_ref[r, :], dh_ref[r, :], dc, dn, rev)
            dg = ag_ref[r, :] if is_final else jnp.zeros((CHUNK, LANE), F32)
            for hd in range(nhd):
                dg = dg + jnp.where(lane == d * ML_HEADS + hd, dgi[hd], 0.0)
                dg = dg + jnp.where(lane == 2 * ML_HEADS + d * ML_HEADS + hd, dgf[hd], 0.0)
            if is_final:
                dqp = dqp + aqk_ref[r, 0:W_B]
                dkp = dkp + aqk_ref[r, W_B:2 * W_B]
                du_ref[r, 0:1024] = (dv + av_ref[r, :]).astype(BF16)
                du_ref[r, 1024:3072] = bo_ref[r, :]
                du_ref[r, 3072:3072 + LANE] = dg.astype(BF16)
            else:
                dv_ref[r, :] = dv
                dg_ref[r, :] = dg
            dqk_ref[r, 0:W_B] = dqp
            dqk_ref[r, W_B:2 * W_B] = dkp
            gs = gs + jnp.sum(dg, axis=0, keepdims=True)
        dc_ref[...] = dc
        gs_ref[...] = gs
        for hd in range(nhd):
            dn_ref[hd:hd + 1, :] = dn[hd]

    cidx = lambda j: _chunk_of(nc - 1 - j, ncc, nc, rev)
    row = lambda s: pl.BlockSpec((STEP, 1024), lambda j: (cidx(j), s))
    wide = pl.BlockSpec((STEP, 2048), lambda j: (cidx(j), 0))
    gate = pl.BlockSpec((STEP, LANE), lambda j: (cidx(j), 0))
    st3 = lambda a, b: pl.BlockSpec((SUB, a, b), lambda j: (cidx(j), 0, 0))
    gs_spec, gs_shape = pl.BlockSpec((1, LANE), lambda j: (0, 0)), jax.ShapeDtypeStruct((1, LANE), F32)
    dqk_shape = jax.ShapeDtypeStruct((tt, 2048), F32)
    ins = [cpre, cpre, u, u, bias, chist, nhist, mhist, h_out, dh] + (list(final) if is_final else [])
    if is_final:
        out_specs = [wide, pl.BlockSpec((pl.Element(STEP), pl.Element(width)), lambda j: (cidx(j) * STEP, col0)), gs_spec]
        out_shape = [dqk_shape, jax.ShapeDtypeStruct((tt, N_U), BF16), gs_shape]
    else:
        out_specs = [wide, row(0), gate, gs_spec]
        out_shape = [dqk_shape, jax.ShapeDtypeStruct((tt, 1024), F32), jax.ShapeDtypeStruct((tt, LANE), F32), gs_shape]
    return pl.pallas_call(
        body, name="mlstm_bwd_rev" if rev else "mlstm_bwd", grid=(nc,),
        in_specs=[row(0), row(1), row(SEG_BV), pl.BlockSpec((STEP, LANE), lambda j: (cidx(j), BLK_GATE)),
                  pl.BlockSpec((1, LANE), lambda j: (0, 0)),
                  st3(1024, ML_D), st3(8, ML_D), st3(8, LANE), row(0), row(0)]
        + ([wide, row(0), gate, wide, _ANY] if is_final else []),
        out_specs=out_specs, out_shape=out_shape,
        input_output_aliases={14: 1} if is_final else {},
        scratch_shapes=[pltpu.VMEM((1024, ML_D), F32), pltpu.VMEM((8, ML_D), F32)],
    )(*ins)


def _whole(body, out_shape, name, *args, nbytes=0):
    return pl.pallas_call(body, name=name, out_shape=out_shape, compiler_params=_vmem(nbytes))(*args)


def _mod_fwd(cs, w_cols, b_cols):
    def body(c_ref, w_ref, b_ref, o_ref):
        o_ref[...] = _exact_nn(_silu(c_ref[...]), w_ref[...]) + b_ref[...]

    return _whole(body, jax.ShapeDtypeStruct((16, w_cols.shape[1]), F32), "mod_fwd", cs, w_cols, b_cols,
                  nbytes=4 * w_cols.size * 4)


def _mod_bwd_w(cs, d9, w_cols):
    def body(c_ref, d_ref, w_ref, gw_ref, pc_ref):
        gw_ref[...] = _exact_tn(_silu(c_ref[...]), d_ref[...])
        pc = lax.dot_general(d_ref[8:16, :], w_ref[...], (((1,), (1,)), ((), ())), precision=lax.Precision.HIGHEST,
                             preferred_element_type=F32)
        row = lax.broadcasted_iota(jnp.int32, pc.shape, 0)
        pc_ref[...] = jnp.where(row == 0, pc, 0.0)

    return _whole(body, [jax.ShapeDtypeStruct(w_cols.shape, F32), jax.ShapeDtypeStruct((8, w_cols.shape[0]), F32)],
                  "mod_bwd_w", cs, d9, w_cols, nbytes=6 * w_cols.size * 4)


def _lower_fwd(lb4):
    def body(l_ref, o_ref):
        o_ref[...] = jnp.zeros_like(o_ref)
        o_ref[0:1, :] = 1.0 / (1.0 + jnp.exp(l_ref[1:2, :] - l_ref[0:1, :]))
        o_ref[1:2, :] = 1.0 / (1.0 + jnp.exp(l_ref[3:4, :] - l_ref[2:3, :]))

    return _whole(body, jax.ShapeDtypeStruct((8, lb4.shape[1]), F32), "lower_fwd", lb4)


def _reduce8(g, name):
    def body(g_ref, o_ref):
        acc = g_ref[0]
        for k in range(1, N_DEV):
            acc = acc + g_ref[k]
        o_ref[...] = acc

    return _whole(body, jax.ShapeDtypeStruct(g.shape[1:], F32), name, g, nbytes=4 * g.size * 4)


_PACK = (("dmodx", 48), ("dmodc", 48), ("gconvw", 144), ("gconvb", 16), ("dlower", 16), ("ghgw", 8), ("gmlw", 8),
         ("glng", 16), ("glnb", 16), ("losssq", 16), ("ggate", 8))
_PACK_ROWS = sum(r for _, r in _PACK)


def _pack_offsets():
    off, out = 0, {}
    for name, rows in _PACK:
        out[name] = (off, rows)
        off += rows
    return out


def _small_finish(total, p0, d_feat):
    offs = _pack_offsets()

    def body(t_ref, p_ref, gb_ref, a0_ref, a1_ref, loss_ref):
        ox, oc, ol, oq = offs["dmodx"][0], offs["dmodc"][0], offs["dlower"][0], offs["losssq"][0]
        gb_ref[...] = t_ref[ox:ox + 48, :] + t_ref[oc:oc + 48, :]
        p = p_ref[...]
        da0 = t_ref[ol:ol + 16, :] * p * (1.0 - p)
        a0_ref[...] = da0
        a1_ref[...] = -da0
        sq = t_ref[oq:oq + 16, :]
        tot = jnp.sum(jnp.sum(sq, axis=1, keepdims=True), axis=0, keepdims=True)
        loss_ref[...] = jnp.broadcast_to(tot * (0.5 / d_feat), loss_ref.shape)

    s = jax.ShapeDtypeStruct
    return _whole(body, [s((48, LANE), F32), s((16, LANE), F32), s((16, LANE), F32), s((8, LANE), F32)],
                  "small_finish", total, p0)


def _cctx_grad(parts, c_ctx8):
    def body(p_ref, c_ref, o_ref):
        acc = p_ref[0]
        for k in range(1, N_DEV):
            acc = acc + p_ref[k]
        o_ref[...] = acc * _dsilu(c_ref[...])

    return _whole(body, jax.ShapeDtypeStruct(c_ctx8.shape, F32), "cctx_grad", parts, c_ctx8)


def _adam_math(w, g, m, v):
    m = ADAM_B1 * m + (1.0 - ADAM_B1) * g
    v = ADAM_B2 * v + (1.0 - ADAM_B2) * (g * g)
    m_hat = m / (1.0 - ADAM_B1 ** ADAM_STEP)
    v_hat = v / (1.0 - ADAM_B2 ** ADAM_STEP)
    delta = -ADAM_LR * (m_hat / (jnp.sqrt(v_hat) + ADAM_EPS) + ADAM_WD * w)
    return delta, m, v


def _adamw(w, g, m, v, rows, name):
    r, c = w.shape

    def body(w_ref, g_ref, m_ref, v_ref, d_ref, mo_ref, vo_ref):
        d_ref[...], mo_ref[...], vo_ref[...] = _adam_math(w_ref[...], g_ref[...], m_ref[...], v_ref[...])

    spec = pl.BlockSpec((rows, c), lambda i: (i, 0))
    return pl.pallas_call(
        body, name=name, grid=(r // rows,), in_specs=[spec] * 4, out_specs=[spec] * 3,
        out_shape=[jax.ShapeDtypeStruct((r, c), F32)] * 3,
        compiler_params=_vmem(16 * rows * (c + LANE) * 4),
    )(w, g, m, v)


def _rs_adamw(recv, w, m, v, tile, name, by_cols=False):
    _, r, c = recv.shape

    def body(r_ref, w_ref, m_ref, v_ref, g_ref, d_ref, mo_ref, vo_ref):
        g = r_ref[0].astype(F32)
        for k in range(1, N_DEV):
            g = g + r_ref[k].astype(F32)
        g_ref[...] = g
        d_ref[...], mo_ref[...], vo_ref[...] = _adam_math(w_ref[...], g, m_ref[...], v_ref[...])

    if by_cols:
        spec = pl.BlockSpec((r, tile), lambda i: (0, i))
        rspec = pl.BlockSpec((N_DEV, r, tile), lambda i: (0, 0, i))
        steps, elems = c // tile, (r + 16) * tile
    else:
        spec = pl.BlockSpec((tile, c), lambda i: (i, 0))
        rspec = pl.BlockSpec((N_DEV, tile, c), lambda i: (0, i, 0))
        steps, elems = r // tile, tile * (c + LANE)
    return pl.pallas_call(
        body, name=name, grid=(steps,), in_specs=[rspec] + [spec] * 3, out_specs=[spec] * 4,
        out_shape=[jax.ShapeDtypeStruct((r, c), F32)] * 4,
        compiler_params=_vmem(2 * elems * (N_DEV * 2 + 7 * 4) + (4 << 20)),
    )(recv, w, m, v)


def _all_gather(x, name):
    r, c = x.shape

    def body(x_ref, out_ref, send_sems, recv_sems, local_sem):
        px, py, pc = _position()
        me, sibling = (px, py, pc), (px, py, 1 - pc)
        chips = [(1 - px, py), (px, 1 - py), (1 - px, 1 - py)]

        def slot(qx, qy, qc):
            return out_ref.at[4 * qx + 2 * qy + qc]

        def copy(k, block, to, src=None):
            return pltpu.make_async_remote_copy(
                src_ref=slot(*block) if src is None else src, dst_ref=slot(*block),
                send_sem=send_sems.at[k], recv_sem=recv_sems.at[k], device_id=to, device_id_type=MESH)

        mine = pltpu.make_async_copy(x_ref, slot(*me), local_sem)
        mine.start()
        first = [copy(1 + j, me, (*chip, pc), src=x_ref) for j, chip in enumerate(chips)]
        first.append(copy(0, me, sibling, src=x_ref))
        for cp in first:
            cp.start()
        passed = [copy(4 + j, (*chip, pc), sibling) for j, chip in enumerate(chips)]
        for j, chip in enumerate(chips):
            copy(1 + j, (*chip, pc), me).wait_recv()
            passed[j].start()
        copy(0, sibling, me).wait_recv()
        for j, chip in enumerate(chips):
            copy(4 + j, (*chip, 1 - pc), me).wait_recv()
        for cp in first + passed:
            cp.wait_send()
        mine.wait()

    return pl.pallas_call(
        body, name=name, out_shape=jax.ShapeDtypeStruct((N_DEV, r, c), x.dtype),
        in_specs=[pl.BlockSpec(memory_space=pl.ANY)], out_specs=pl.BlockSpec(memory_space=pl.ANY),
        scratch_shapes=[pltpu.SemaphoreType.DMA((7,)), pltpu.SemaphoreType.DMA((7,)), pltpu.SemaphoreType.DMA],
    )(x)


def _all_gather_relay(x, name):
    r, c = x.shape
    half = c // 2

    def body(x_ref, out_ref, send_sems, recv_sems, local_sem):
        px, py, pc = _position()
        me, sib = (px, py, pc), (px, py, 1 - pc)
        xn, yn, dg = (1 - px, py, pc), (px, 1 - py, pc), (1 - px, 1 - py, pc)
        sib_xn, sib_yn, sib_dg = (1 - px, py, 1 - pc), (px, 1 - py, 1 - pc), (1 - px, 1 - py, 1 - pc)

        def slot(owner, cols=None):
            ref = out_ref.at[4 * owner[0] + 2 * owner[1] + owner[2]]
            return ref if cols is None else ref.at[:, pl.ds(cols, half)]

        def copy(k, owner, to, src=None, cols=None):
            return pltpu.make_async_remote_copy(
                src_ref=slot(owner, cols) if src is None else src, dst_ref=slot(owner, cols),
                send_sem=send_sems.at[k], recv_sem=recv_sems.at[k], device_id=to, device_id_type=MESH)

        mine = pltpu.make_async_copy(x_ref, slot(me), local_sem)
        mine.start()
        own = [copy(1, me, xn, src=x_ref), copy(2, me, yn, src=x_ref), copy(0, me, sib, src=x_ref)]
        for cp in own:
            cp.start()
        copy(1, xn, me).wait_recv()
        relay_x = [copy(3, xn, yn, cols=0), copy(5, xn, sib)]
        for cp in relay_x:
            cp.start()
        copy(2, yn, me).wait_recv()
        relay_y = [copy(4, yn, xn, cols=half), copy(6, yn, sib)]
        for cp in relay_y:
            cp.start()
        copy(3, dg, me, cols=0).wait_recv()
        copy(4, dg, me, cols=half).wait_recv()
        relay_d = copy(7, dg, sib)
        relay_d.start()
        copy(0, sib, me).wait_recv()
        copy(5, sib_xn, me).wait_recv()
        copy(6, sib_yn, me).wait_recv()
        copy(7, sib_dg, me).wait_recv()
        for cp in own + relay_x + relay_y + [relay_d]:
            cp.wait_send()
        mine.wait()

    return pl.pallas_call(
        body, name=name, out_shape=jax.ShapeDtypeStruct((N_DEV, r, c), x.dtype),
        in_specs=[_ANY], out_specs=_ANY,
        scratch_shapes=[pltpu.SemaphoreType.DMA((8,)), pltpu.SemaphoreType.DMA((8,)), pltpu.SemaphoreType.DMA],
    )(x)


DW_PIECES = ((0, 256), (256, 640), (896, 1152))


def _local_step(ctx, x, target, modp, lower, wt_u, w_o, w9, conv_b, gate_b, hgw, mlw, ln_g, ln_b, exchange):
    tc = ctx.shape[0]
    tt = tc + x.shape[0]
    nbc, ncc = tc // ROWS, tc // CHUNK
    lower_f, lower_b = lower[0:1], lower[1:2]

    hc = _modulate_fwd(ctx, x, modp)
    tmh = _pick(tt, (1088, 768, 512, 256))
    if exchange:
        u, w_o = _mm(hc, wt_u, "nt", F32, tmh, 1152, D_MODEL, "mm_u", ride=_Ride("gather", w_o))
        w_o = w_o.reshape(D_MODEL, D_MODEL)
    else:
        u = _mm(hc, wt_u, "nt", F32, tmh, 1152, D_MODEL, "mm_u")
    cpre = _conv_fwd(u, w9, conv_b, tc)
    bias = jnp.pad(gate_b.reshape(1, 16), ((0, 0), (0, LANE - 16)))

    o_f, hist_f = _hgrn_fwd(u, lower_f, ncc, False)
    o_b, hist_b = _hgrn_fwd(u, lower_b, ncc, True)
    h_f, ch_f, nh_f, mh_f = _mlstm_fwd(cpre, u, bias, ncc, False)
    h_b, ch_b, nh_b, mh_b = _mlstm_fwd(cpre, u, bias, ncc, True)
    y = _post_fwd(o_f, o_b, h_f, h_b, u, hgw, mlw, nbc)
    dz, dxa, fsum = _final(y, w_o, x, target, modp, ln_g, ln_b)

    dw_o = _mm(y, dz, "tn", BF16, D_MODEL, 1024, _pick(y.shape[0], (512, 256)), "mm_dwo")
    do, dhm, daz, dbo, psum = _post_bwd(dz, w_o, o_f, o_b, h_f, h_b, u, hgw, mlw, nbc)
    if exchange:
        dzf_f, dzq, dv_a, dlb_f, dw_o = _hgrn_bwd(
            u, lower_f, hist_f, do, ncc, False, ride=_Ride("a2a", dw_o.reshape(N_DEV, D_MODEL // N_DEV, D_MODEL)))
    else:
        dzf_f, dzq, dv_a, dlb_f = _hgrn_bwd(u, lower_f, hist_f, do, ncc, False)
    du, dlb_b = _hgrn_bwd(u, lower_b, hist_b, do, ncc, True, final=(dzq, dv_a, dzf_f, daz))
    dqk, dv_m, dg, _ = _mlstm_bwd(cpre, u, bias, ch_f, nh_f, mh_f, h_f, dhm, ncc, False)
    dqk, du, gsum = _mlstm_bwd(cpre, u, bias, ch_b, nh_b, mh_b, h_b, dhm, ncc, True, final=(dqk, dv_m, dg, dbo, du))
    du, gconvw, gconvb = _conv_bwd(dqk, u, w9, tc, du)
    tkw = _pick(tt, (2176, 768, 512, 256))
    blocks = lambda g: g.reshape(N_DEV, N_IN // N_DEV, g.shape[1])
    dwu = lambda name, cols, ride: _mm(du, hc, "tn", BF16, 1152, cols[1], tkw, name, b_cols=cols, ride=ride, m_out=N_IN)
    dwt_a = dwu("mm_dwu_a", DW_PIECES[0], None)
    if exchange:
        whole = lambda piece, into: _Ride("a2a", blocks(piece[1]), cols=(piece[0][0], D_MODEL), into=into)
        dwt_b, got = dwu("mm_dwu_b", DW_PIECES[1], whole((DW_PIECES[0], dwt_a), None))
        dwt_c, got = dwu("mm_dwu_c", DW_PIECES[2], whole((DW_PIECES[1], dwt_b), got))
        dh, dwt_u = _mm(du, wt_u, "nn", F32, tmh, D_MODEL // 2, 3456, "mm_dh", ride=whole((DW_PIECES[2], dwt_c), got))
    else:
        dwt_u = jnp.concatenate([dwt_a, dwu("mm_dwu_b", DW_PIECES[1], None), dwu("mm_dwu_c", DW_PIECES[2], None)], axis=1)
        dh = _mm(du, wt_u, "nn", F32, tmh, D_MODEL // 2, 3456, "mm_dh")
    gx, msum = _modulate_bwd(dh, ctx, x, modp, dxa)

    zero_row = jnp.zeros((1, D_MODEL), F32)
    small = dict(
        dmodx=jnp.concatenate([msum[2:3], msum[3:4], fsum[0:1]], axis=0),
        dmodc=jnp.concatenate([msum[0:1], msum[1:2], zero_row], axis=0),
        gconvw=gconvw, gconvb=gconvb, dlower=jnp.concatenate([dlb_f, dlb_b], axis=0),
        ghgw=psum[0:1], gmlw=psum[1:2], glng=fsum[1:2], glnb=fsum[2:3], losssq=fsum[3:4],
        ggate=jnp.concatenate([gsum, jnp.zeros((7, LANE), F32)], axis=0))
    return gx, dwt_u, dw_o, small


def _pack_small(small):
    return jnp.concatenate([small[name].reshape(rows, LANE) for name, rows in _PACK], axis=0)


def _flat_pad(a, rows):
    flat = a.reshape(-1)
    return jnp.pad(flat, (0, rows * LANE - flat.shape[0])).reshape(rows, LANE)


def kernel(x, c, ctx, c_ctx, w_mod, b_mod, w_in, conv_w, conv_b, hg_lb, ml_gate_b, hg_norm_w, ml_norm_w, w_out, ln_g, ln_b, loss_target, m_c_ctx, m_w_mod, m_b_mod, m_w_in, m_conv_w, m_conv_b, m_hg_lb, m_ml_gate_b, m_hg_norm_w, m_ml_norm_w, m_w_out, m_ln_g, m_ln_b, v_c_ctx, v_w_mod, v_b_mod, v_w_in, v_conv_w, v_conv_b, v_hg_lb, v_ml_gate_b, v_hg_norm_w, v_ml_norm_w, v_w_out, v_ln_g, v_ln_b):
    px, py, pc = _position()
    me = 4 * px + 2 * py + pc
    d = D_MODEL
    n_mod = w_mod.shape[2]
    n_wi = w_in.shape[2]
    n_cv = conv_w.shape[3]
    n_lb = hg_lb.shape[2]

    pack0 = jnp.concatenate([c.reshape(-1), conv_w.reshape(-1), hg_lb.reshape(-1)]).reshape(1, -1)
    g0 = _all_gather(pack0, "gather_small_inputs")[:, 0, :]
    c_all = g0[:, :d]
    w9 = jnp.transpose(g0[:, d:d + 9 * n_cv].reshape(N_DEV, 9, n_cv), (1, 0, 2)).reshape(9, N_DEV * n_cv)
    lb4 = jnp.transpose(g0[:, d + 9 * n_cv:].reshape(N_DEV, 4, n_lb), (1, 0, 2)).reshape(4, N_DEV * n_lb)
    lower = _lower_fwd(lb4)

    cs = jnp.concatenate([c_all, c_ctx.reshape(1, d), jnp.zeros((7, d), F32)], axis=0)
    b_cols = lax.dynamic_slice(b_mod, (0, me * n_mod), (1, n_mod))
    slab = _mod_fwd(cs, w_mod[0], b_cols)
    mod_all = jnp.transpose(_all_gather(slab, "gather_mod"), (1, 0, 2)).reshape(16, N_DEV * n_mod)
    mod_x = lax.dynamic_slice(mod_all, (me, 0), (1, 3 * d)).reshape(3, d)
    modp = jnp.stack([mod_all[8].reshape(3, d), mod_x])

    wt = _all_gather_relay(w_in[0].T.astype(BF16), "gather_w_in").reshape(N_DEV * n_wi, d)
    wt_u = jnp.pad(wt, ((0, N_U - N_DEV * n_wi), (0, 0)))

    gx, recv_wi, recv_wo, small = _local_step(ctx[0], x[0], loss_target[0], modp, lower, wt_u, w_out[0].astype(BF16),
                                              w9, conv_b, ml_gate_b[0], hg_norm_w, ml_norm_w, ln_g, ln_b, True)
    g_wi, d_wi, nm_wi, nv_wi = [a.T for a in _rs_adamw(recv_wi, w_in[0].T, m_w_in[0].T, v_w_in[0].T, 256,
                                                       "adamw_w_in", by_cols=True)]
    g_wo, d_wo, nm_wo, nv_wo = _rs_adamw(recv_wo, w_out[0], m_w_out[0], v_w_out[0], 64, "adamw_w_out")

    packs = _all_gather(_pack_small(small), "gather_small_grads")
    total = _reduce8(packs, "reduce_small_grads")
    offs = _pack_offsets()
    piece = lambda name: total[offs[name][0]:offs[name][0] + offs[name][1]]
    g_bmod, g_lb0, g_lb1, loss8 = _small_finish(total, lower[0:2].reshape(16, LANE), float(d))

    ox = offs["dmodx"][0]
    dmodx_all = packs[:, ox:ox + 48, :].reshape(N_DEV, 3 * d)
    dmodc_tot = piece("dmodc").reshape(1, 3 * d)
    d9 = jnp.concatenate([dmodx_all, dmodc_tot, jnp.zeros((7, 3 * d), F32)], axis=0)
    d9_cols = lax.dynamic_slice(d9, (0, me * n_mod), (16, n_mod))
    g_wmod, pc_part = _mod_bwd_w(cs, d9_cols, w_mod[0])
    c_ctx8 = jnp.concatenate([c_ctx.reshape(1, d), jnp.zeros((7, d), F32)], axis=0)
    g_cctx = _cctx_grad(_all_gather(pc_part, "gather_cctx"), c_ctx8)[0]
    d_wmod, nm_wmod, nv_wmod = _adamw(w_mod[0], g_wmod, m_w_mod[0], v_w_mod[0], 256, "adamw_w_mod")

    g_convw_full = piece("gconvw").reshape(9, d)
    g_convw = lax.dynamic_slice(g_convw_full, (0, me * n_cv), (9, n_cv)).reshape(conv_w.shape)
    lb_full = jnp.stack([jnp.stack([g_lb0[0:8].reshape(-1), g_lb1[0:8].reshape(-1)]),
                         jnp.stack([g_lb0[8:16].reshape(-1), g_lb1[8:16].reshape(-1)])])
    g_hglb = lax.dynamic_slice(lb_full, (0, 0, me * n_lb), (2, 2, n_lb))
    grads = dict(
        c_ctx=g_cctx, b_mod=g_bmod.reshape(b_mod.shape), conv_w=g_convw, conv_b=piece("gconvb").reshape(conv_b.shape),
        hg_lb=g_hglb, ml_gate_b=piece("ggate")[0, :16].reshape(ml_gate_b.shape),
        hg_norm_w=piece("ghgw").reshape(hg_norm_w.shape), ml_norm_w=piece("gmlw").reshape(ml_norm_w.shape),
        ln_g=piece("glng").reshape(ln_g.shape), ln_b=piece("glnb").reshape(ln_b.shape))
    params = dict(c_ctx=(c_ctx, m_c_ctx, v_c_ctx), b_mod=(b_mod, m_b_mod, v_b_mod), conv_w=(conv_w, m_conv_w, v_conv_w),
                  conv_b=(conv_b, m_conv_b, v_conv_b), hg_lb=(hg_lb, m_hg_lb, v_hg_lb),
                  ml_gate_b=(ml_gate_b, m_ml_gate_b, v_ml_gate_b), hg_norm_w=(hg_norm_w, m_hg_norm_w, v_hg_norm_w),
                  ml_norm_w=(ml_norm_w, m_ml_norm_w, v_ml_norm_w), ln_g=(ln_g, m_ln_g, v_ln_g), ln_b=(ln_b, m_ln_b, v_ln_b))
    names = list(params)
    rows_of = {n: -(-params[n][0].size // LANE) for n in names}
    rows_tot = -(-sum(rows_of.values()) // 8) * 8
    cat = lambda arrs: jnp.concatenate(
        [_flat_pad(a, rows_of[n]) for n, a in zip(names, arrs)]
        + [jnp.ones((rows_tot - sum(rows_of.values()), LANE), F32)], axis=0)
    d_s, m_s, v_s = _adamw(cat([params[n][0] for n in names]), cat([grads[n] for n in names]),
                           cat([params[n][1] for n in names]), cat([params[n][2] for n in names]), rows_tot, "adamw_small")
    delta, new_m, new_v, off = {}, {}, {}, 0
    for n in names:
        shape, size = params[n][0].shape, params[n][0].size
        take = lambda a: a[off:off + rows_of[n]].reshape(-1)[:size].reshape(shape)
        delta[n], new_m[n], new_v[n] = take(d_s), take(m_s), take(v_s)
        off += rows_of[n]
    grads.update(w_mod=g_wmod[None], w_in=g_wi[None], w_out=g_wo[None])
    delta.update(w_mod=d_wmod[None], w_in=d_wi[None], w_out=d_wo[None])
    new_m.update(w_mod=nm_wmod[None], w_in=nm_wi[None], w_out=nm_wo[None])
    new_v.update(w_mod=nv_wmod[None], w_in=nv_wi[None], w_out=nv_wo[None])

    order = ("c_ctx", "w_mod", "b_mod", "w_in", "conv_w", "conv_b", "hg_lb", "ml_gate_b", "hg_norm_w", "ml_norm_w",
             "w_out", "ln_g", "ln_b")
    return (loss8[0, 0], gx[None], *[grads[n] for n in order], *[delta[n] for n in order],
            *[new_m[n] for n in order], *[new_v[n] for n in order])
```

```python
import jax
import jax.numpy as jnp
from jax import lax
from jax.experimental import pallas as pl
from jax.experimental.pallas import tpu as pltpu

F32 = jnp.float32
BF16 = jnp.bfloat16

D_MODEL = 2048
W_A = 1024
W_B = 1024
HG_HEADS = 8
HG_D = 128
ML_HEADS = 4
ML_D = 256
CHUNK = 64
N_IN = 10256
LANE = 128
N_U = 81 * LANE
N_DEV = 8
ALPHA = 2.0 ** 0.25
LN_EPS = 1e-5
NORM_EPS = 1e-6
ADAM_LR, ADAM_B1, ADAM_B2, ADAM_EPS, ADAM_WD, ADAM_STEP = 0.001, 0.9, 0.999, 1e-08, 0.01, 10
VMEM_CAP = 60 * 1024 * 1024

SEG_AQ, SEG_AFF, SEG_AFB, SEG_AI, SEG_AZ = range(5)
BLK_QK = 40
SEG_BV, SEG_BO, SEG_BZ = 7, 8, 9
BLK_GATE = 80

MESH = pl.DeviceIdType.MESH


def _vmem(nbytes):
    return pltpu.CompilerParams(vmem_limit_bytes=int(min(VMEM_CAP, max(nbytes, 16 * 1024 * 1024))))


def _sigmoid(x):
    return 1.0 / (1.0 + jnp.exp(-x))


def _silu(x):
    return x * _sigmoid(x)


def _dsilu(x):
    s = _sigmoid(x)
    return s * (1.0 + x * (1.0 - s))


def _silu_both(x):
    s = _sigmoid(x)
    return x * s, s * (1.0 + x * (1.0 - s))


def _bdot(a, b, dims):
    return lax.dot_general(a.astype(BF16), b.astype(BF16), (dims, ((), ())), preferred_element_type=F32)


def _nn(a, b):
    return _bdot(a, b, ((1,), (0,)))


def _nt(a, b):
    return _bdot(a, b, ((1,), (1,)))


def _tn(a, b):
    return _bdot(a, b, ((0,), (0,)))


def _exact_nn(a, b):
    return lax.dot_general(a, b, (((1,), (0,)), ((), ())), precision=lax.Precision.HIGHEST,
                           preferred_element_type=F32)


def _exact_tn(a, b):
    return lax.dot_general(a, b, (((0,), (0,)), ((), ())), precision=lax.Precision.HIGHEST,
                           preferred_element_type=F32)


def _tri(rev):
    t = lax.broadcasted_iota(jnp.int32, (CHUNK, CHUNK), 0)
    s = lax.broadcasted_iota(jnp.int32, (CHUNK, CHUNK), 1)
    return (s >= t) if rev else (s <= t)


def _eye():
    t = lax.broadcasted_iota(jnp.int32, (CHUNK, CHUNK), 0)
    s = lax.broadcasted_iota(jnp.int32, (CHUNK, CHUNK), 1)
    return (s == t).astype(F32)


def _row_to_col(row):
    return jnp.sum(_eye() * row, axis=1, keepdims=True)


def _last_onehot(rev):
    t = lax.broadcasted_iota(jnp.int32, (CHUNK, 1), 0)
    return (t == (0 if rev else CHUNK - 1)).astype(F32)


def _head_slices(width, n_heads):
    hd = width // n_heads
    return [slice(h * hd, (h + 1) * hd) for h in range(n_heads)]


def _scan_sum(x, rev):
    n = x.shape[0]
    t = lax.broadcasted_iota(jnp.int32, x.shape, 0)
    s = 1
    while s < n:
        if rev:
            x = x + jnp.where(t < n - s, pltpu.roll(x, n - s, 0), 0.0)
        else:
            x = x + jnp.where(t >= s, pltpu.roll(x, s, 0), 0.0)
        s *= 2
    return x


def _dot3(a, b, dims):
    a_hi, b_hi = a.astype(BF16), b.astype(BF16)
    a_lo, b_lo = (a - a_hi.astype(F32)).astype(BF16), (b - b_hi.astype(F32)).astype(BF16)
    dot = lambda x, y: lax.dot_general(x, y, (dims, ((), ())), preferred_element_type=F32)
    return dot(a_hi, b_hi) + (dot(a_hi, b_lo) + dot(a_lo, b_hi))


def _hg_common(zq, zf, lb, rev):
    q, dq_dz = _silu_both(zq)
    sg = _sigmoid(zf)
    f = lb + (1.0 - lb) * sg
    g = jnp.log(f)
    k = 1.0 - f
    b = _scan_sum(g, rev)
    b_last = jnp.sum(g, axis=0, keepdims=True)
    r = b[CHUNK // 2:CHUNK // 2 + 1, :]
    e_up = jnp.exp(b - r)
    e_dn = jnp.exp(r - b)
    e_b = e_up * jnp.exp(r)
    e_lb = e_dn * jnp.exp(b_last - r)
    return dict(q=q, dq_dz=dq_dz, sg=sg, f=f, k=k, e_up=e_up, e_dn=e_dn, e_b=e_b, e_lb=e_lb, e_last=jnp.exp(b_last),
                q_t=q * e_up, k_t=k * e_dn, q_s=q * e_b, k_h=k * e_lb, tri=_tri(rev).astype(F32))


def hg_chunk_fwd(zq, zf, v, lb, st, rev):
    c = _hg_common(zq, zf, lb, rev)
    hs = _head_slices(zq.shape[1], zq.shape[1] // HG_D)
    s = [_nt(c["q_t"][:, sl], c["k_t"][:, sl]) for sl in hs]
    oi = [_nt(c["q_s"][:, sl], st[sl, :]) for sl in hs]
    ds = [_tn(v[:, sl], c["k_h"][:, sl]) for sl in hs]
    oa = [_nn(c["tri"] * s_h, v[:, sl]) for s_h, sl in zip(s, hs)]
    o = jnp.concatenate([x + y for x, y in zip(oi, oa)], axis=1)
    st_new = jnp.concatenate([st[sl, :] * c["e_last"][:, sl] + d for sl, d in zip(hs, ds)], axis=0)
    return o, st_new


def hg_chunk_bwd(zq, zf, v, lb, st, do, dst_new, rev):
    c = _hg_common(zq, zf, lb, rev)
    hs = _head_slices(zq.shape[1], zq.shape[1] // HG_D)
    tri, q_t, k_t, q_s, k_h = c["tri"], c["q_t"], c["k_t"], c["q_s"], c["k_h"]
    s = [_nt(q_t[:, sl], k_t[:, sl]) for sl in hs]
    da = [tri * _nt(do[:, sl], v[:, sl]) for sl in hs]
    dq_s = [_nn(do[:, sl], st[sl, :]) for sl in hs]
    dk_h = [_nn(v[:, sl], dst_new[sl, :]) for sl in hs]
    dv_s = [_nt(k_h[:, sl], dst_new[sl, :]) for sl in hs]
    dst_q = [_tn(do[:, sl], q_s[:, sl]) for sl in hs]
    dq_t = [_dot3(da_h, k_t[:, sl], ((1,), (0,))) for da_h, sl in zip(da, hs)]
    dk_t = [_dot3(da_h, q_t[:, sl], ((0,), (0,))) for da_h, sl in zip(da, hs)]
    dv_a = [_tn(tri * s_h, do[:, sl]) for s_h, sl in zip(s, hs)]
    cat = lambda parts: jnp.concatenate(parts, axis=1)
    dq_s, dk_h, dq_t, dk_t = cat(dq_s), cat(dk_h), cat(dq_t), cat(dk_t)
    dv = cat([x + y for x, y in zip(dv_a, dv_s)])
    dst = jnp.concatenate([dst_new[sl, :] * c["e_last"][:, sl] + d for sl, d in zip(hs, dst_q)], axis=0)
    dq = dq_s * c["e_b"] + dq_t * c["e_up"]
    dk = dk_t * c["e_dn"] + dk_h * c["e_lb"]
    db = c["q"] * dq - c["k"] * dk
    ss = cat([jnp.sum(dst_new[sl, :] * st[sl, :], axis=0, keepdims=True) for sl in hs])
    d_all = jnp.sum(dk_h * k_h, axis=0, keepdims=True) + c["e_last"] * ss
    dg = _scan_sum(db, not rev) + d_all
    dzq = dq * c["dq_dz"]
    df = dg / c["f"] - dk
    dzf = df * (1.0 - lb) * c["sg"] * (1.0 - c["sg"])
    dlb = jnp.sum(df * (1.0 - c["sg"]), axis=0, keepdims=True)
    return dzq, dzf, dv, dlb, dst


def _log_sigmoid(x):
    return jnp.minimum(x, 0.0) - jnp.log(1.0 + jnp.exp(-jnp.abs(x)))


def _each(fn, *lists):
    return [fn(*xs) for xs in zip(*lists)]


def _bf(xs):
    return [x.astype(BF16) for x in xs]


def _ml_forward_parts(qp, kp, v, gates, c, n, m, rev, with_num):
    hs = _head_slices(qp.shape[1], qp.shape[1] // ML_D)
    q_all, dq_dp = _silu_both(qp)
    k_all, dk_dp = _silu_both(kp)
    k_all = k_all * (ML_D ** -0.5)
    q = [q_all[:, sl] for sl in hs]
    k = [k_all[:, sl] for sl in hs]
    vv = [v[:, sl] for sl in hs]
    cc = [c[sl, :] for sl in hs]
    tri_b = _tri(rev)
    tri = tri_b.astype(F32)
    tri_t = _tri(not rev).astype(F32)
    e_last = _last_onehot(rev)
    qb, kb, vb, cb = _bf(q), _bf(k), _bf(vv), _bf(cc)
    qk = _each(_nt, qb, kb)
    parts = []
    for (gi_c, gi_r, gf_c, gf_r), m_h in zip(gates, m):
        lf_c, lf_r = _log_sigmoid(gf_c), _log_sigmoid(gf_r)
        b_c = jnp.sum(tri * lf_r, axis=1, keepdims=True)
        b_r = jnp.sum(tri_t * lf_c, axis=0, keepdims=True)
        log_w = jnp.where(tri_b, b_c - b_r + gi_r, -jnp.inf)
        m_inter = b_c + m_h
        m_t = jnp.maximum(m_inter, jnp.max(log_w, axis=1, keepdims=True))
        m_new = jnp.sum(m_t * e_last, axis=0, keepdims=True)
        b_last = jnp.sum(b_c * e_last, axis=0, keepdims=True)
        parts.append(dict(a=jnp.exp(m_inter - m_t), p=jnp.exp(log_w - m_t), floor=jnp.exp(-m_t), m_new=m_new,
                          ws=jnp.exp(b_last - b_c + gi_c - m_new), decay=jnp.exp(b_last + m_h - m_new), gf_c=gf_c))
    w = [pt["p"] * x for pt, x in zip(parts, qk)]
    wb = _bf(w)
    for pt, q_h, n_h, w_h in zip(parts, q, n, w):
        qn = jnp.sum(q_h * n_h, axis=1, keepdims=True)
        den = pt["a"] * qn + jnp.sum(w_h, axis=1, keepdims=True)
        pt.update(qn=qn, den=den, rinv=1.0 / jnp.maximum(jnp.abs(den), pt["floor"]), w=w_h)
    if with_num:
        qc = _each(_nt, qb, cb)
        wv = _each(_nn, wb, vb)
        for pt, qc_h, wv_h in zip(parts, qc, wv):
            pt.update(num=pt["a"] * qc_h + wv_h)
    return hs, q, k, vv, cc, tri, parts, dict(q=qb, k=kb, v=vb, c=cb, w=wb, dq_dp=dq_dp, dk_dp=dk_dp)


def ml_chunk_fwd(qp, kp, v, gates, c, n, m, rev):
    hs, q, k, vv, cc, tri, parts, bf = _ml_forward_parts(qp, kp, v, gates, c, n, m, rev, True)
    h = jnp.concatenate([pt["num"] * pt["rinv"] for pt in parts], axis=1)
    upd = _each(_tn, [pt["ws"] * v_h for pt, v_h in zip(parts, vv)], bf["k"])
    c_new = jnp.concatenate([pt["decay"] * c_h + u for pt, c_h, u in zip(parts, cc, upd)], axis=0)
    n_new = [pt["decay"] * n_h + jnp.sum(pt["ws"] * k_h, axis=0, keepdims=True) for pt, n_h, k_h in zip(parts, n, k)]
    return h, c_new, n_new, [pt["m_new"] for pt in parts]


def ml_chunk_bwd(qp, kp, v, gates, c, n, m, h_out, dh, dc_new, dn_new, rev):
    hs, q, k, vv, cc, tri, parts, bf = _ml_forward_parts(qp, kp, v, gates, c, n, m, rev, False)
    dcn = [dc_new[sl, :] for sl in hs]
    dcb = _bf(dcn)
    dnum, dden = [], []
    for pt, sl in zip(parts, hs):
        dh_h = dh[:, sl]
        signed_live = jnp.where(jnp.abs(pt["den"]) > pt["floor"], jnp.where(pt["den"] >= 0.0, 1.0, -1.0), 0.0)
        dnum.append(dh_h * pt["rinv"])
        dden.append(-jnp.sum(dh_h * h_out[:, sl], axis=1, keepdims=True) * pt["rinv"] * signed_live)
    dnb = _bf(dnum)
    dw = [x + y for x, y in zip(_each(_nt, dnb, bf["v"]), dden)]
    kdc = _each(_nt, bf["k"], dcb)
    vdc = _each(_nn, bf["v"], dcb)
    dqk = [x * pt["p"] for x, pt in zip(dw, parts)]
    adn = [pt["a"] * x for pt, x in zip(parts, dnum)]
    dqkb, adnb = _bf(dqk), _bf(adn)
    dv_w = _each(_tn, bf["w"], dnb)
    dq_k = _each(_nn, dqkb, bf["k"])
    dq_c = _each(_nn, adnb, bf["c"])
    dk_q = _each(_tn, dqkb, bf["q"])
    dc_q = _each(_tn, adnb, bf["q"])
    dq, dk, dv, dgi, dgf, dc, dn = [], [], [], [], [], [], []
    for i, pt in enumerate(parts):
        a, ws, decay = pt["a"], pt["ws"], pt["decay"]
        add = a * dden[i]
        e = dw[i] * pt["w"]
        dv.append(dv_w[i] + ws * kdc[i])
        dq.append(dq_k[i] + dq_c[i] + add * n[i])
        dk.append(dk_q[i] + ws * vdc[i] + ws * dn_new[i])
        alpha = jnp.sum(q[i] * dq_c[i], axis=1, keepdims=True) + dden[i] * pt["qn"] * a
        omega = (jnp.sum(vdc[i] * k[i], axis=1, keepdims=True) + jnp.sum(k[i] * dn_new[i], axis=1, keepdims=True)) * ws
        delta = decay * (jnp.sum(jnp.sum(dcn[i] * cc[i], axis=1, keepdims=True), axis=0, keepdims=True)
                         + jnp.sum(dn_new[i] * n[i], axis=1, keepdims=True))
        dc.append(decay * dcn[i] + dc_q[i])
        dn.append(decay * dn_new[i] + jnp.sum(add * q[i], axis=0, keepdims=True))
        e_rows = jnp.sum(e, axis=1, keepdims=True)
        e_cols = _row_to_col(jnp.sum(e, axis=0, keepdims=True))
        dgi.append(e_cols + omega)
        db = e_rows + alpha - e_cols - omega
        tail = jnp.sum(omega, axis=0, keepdims=True) + delta
        dlf = _row_to_col(jnp.sum(tri * db, axis=0, keepdims=True)) + tail
        dgf.append(dlf * (1.0 - _sigmoid(pt["gf_c"])))
    cat = lambda xs: jnp.concatenate(xs, axis=1)
    dqp = cat(dq) * bf["dq_dp"]
    dkp = cat(dk) * (ML_D ** -0.5) * bf["dk_dp"]
    return dqp, dkp, cat(dv), dgi, dgf, jnp.concatenate(dc, axis=0), dn


def _pick(n, prefs):
    for p in prefs:
        if n % p == 0:
            return p
    raise ValueError(f"no tile for {n} among {prefs}")


def _position():
    return lax.axis_index("x"), lax.axis_index("y"), lax.axis_index("c")


class _Ride:
    def __init__(self, kind, x, cols=None, into=None):
        self.kind, self.x, self.cols, self.into = kind, x, cols, into
        r, c = x.shape[-2:]
        self.out_shape = jax.ShapeDtypeStruct((N_DEV, r, c if cols is None else cols[1]), x.dtype)
        self.width = c

    def _copies(self, x_ref, out_ref, send_sems, recv_sems, local_sem):
        px, py, pc = _position()
        me = 4 * px + 2 * py + pc
        src = (lambda slot: x_ref) if self.kind == "gather" else (lambda slot: x_ref.at[slot])
        dst = ((lambda slot: out_ref.at[slot]) if self.cols is None
               else (lambda slot: out_ref.at[slot, :, pl.ds(self.cols[0], self.width)]))
        mine = pltpu.make_async_copy(src(me), dst(me), local_sem)
        sends, recvs = [], []
        for k, (fx, fy, fc) in enumerate([(1, 0, 0), (0, 1, 0), (1, 1, 0), (1, 0, 1), (0, 1, 1), (1, 1, 1), (0, 0, 1)]):
            qx, qy, qc = (1 - px if fx else px), (1 - py if fy else py), (1 - pc if fc else pc)
            peer = 4 * qx + 2 * qy + qc
            sends.append(pltpu.make_async_remote_copy(
                src_ref=src(peer), dst_ref=dst(me), send_sem=send_sems.at[k], recv_sem=recv_sems.at[k],
                device_id=(qx, qy, qc), device_id_type=MESH))
            recvs.append(pltpu.make_async_remote_copy(
                src_ref=src(me), dst_ref=dst(peer), send_sem=send_sems.at[k], recv_sem=recv_sems.at[k],
                device_id=(qx, qy, qc), device_id_type=MESH))
        return mine, sends, recvs

    def start(self, *refs):
        mine, sends, _ = self._copies(*refs)
        mine.start()
        for cp in sends:
            cp.start()

    def wait(self, *refs):
        mine, sends, recvs = self._copies(*refs)
        for cp in recvs:
            cp.wait_recv()
        for cp in sends:
            cp.wait_send()
        mine.wait()

    def operands(self):
        return [self.x] + ([self.into] if self.into is not None else [])


_RIDE_SCRATCH = [pltpu.SemaphoreType.DMA((7,)), pltpu.SemaphoreType.DMA((7,)), pltpu.SemaphoreType.DMA]
_ANY = pl.BlockSpec(memory_space=pl.ANY)


def _mm(a, b, mode, out_dtype, tm, tn, tk, name, ride=None, b_cols=None, m_out=None):
    if mode == "nn":
        (m, k), (k2, n) = a.shape, b.shape
    elif mode == "nt":
        (m, k), (n, k2) = a.shape, b.shape
    else:
        (k, m), (k2, n) = a.shape, b.shape
    off, n = (0, n) if b_cols is None else b_cols
    assert k == k2 and m % tm == 0 and n % tn == 0 and k % tk == 0, (a.shape, b.shape, mode, tm, tn, tk)
    assert b_cols is None or (mode != "nt" and off % LANE == 0)
    nk = k // tk
    dims = {"nn": ((1,), (0,)), "nt": ((1,), (1,)), "tn": ((0,), (0,))}[mode]
    a_spec = (pl.BlockSpec((tk, tm), lambda j, i, kk: (kk, i)) if mode == "tn"
              else pl.BlockSpec((tm, tk), lambda j, i, kk: (i, kk)))
    if b_cols is not None:
        b_spec = pl.BlockSpec((pl.Element(tk), pl.Element(tn)),
                              lambda j, i, kk: (pl.multiple_of(kk * tk, LANE), pl.multiple_of(off + j * tn, LANE)))
    elif mode == "nt":
        b_spec = pl.BlockSpec((tn, tk), lambda j, i, kk: (j, kk))
    else:
        b_spec = pl.BlockSpec((tk, tn), lambda j, i, kk: (kk, j))

    grid = (n // tn, m // tm, nk)
    n_ride_in = len(ride.operands()) if ride is not None else 0

    def body(a_ref, b_ref, *rest):
        if ride is not None:
            x_ref = rest[0]
            o_ref, got_ref, acc_ref = rest[n_ride_in:n_ride_in + 3]
            comm = (x_ref, got_ref) + tuple(rest[n_ride_in + 3:])
        else:
            o_ref, acc_ref = rest
        kk = pl.program_id(2)
        step = (pl.program_id(0) * grid[1] + pl.program_id(1)) * nk + kk
        if ride is not None:
            @pl.when(step == 0)
            def _():
                ride.start(*comm)

        part = lax.dot_general(a_ref[...], b_ref[...], (dims, ((), ())), preferred_element_type=F32)
        if nk == 1:
            o_ref[...] = part.astype(o_ref.dtype)
        else:
            @pl.when(kk == 0)
            def _():
                acc_ref[...] = part

            @pl.when(jnp.logical_and(kk > 0, kk < nk - 1))
            def _():
                acc_ref[...] += part

            @pl.when(kk == nk - 1)
            def _():
                o_ref[...] = (acc_ref[...] + part).astype(o_ref.dtype)

        if ride is not None:
            @pl.when(step == grid[0] * grid[1] * nk - 1)
            def _():
                ride.wait(*comm)

    osz = jnp.dtype(out_dtype).itemsize
    need = 2 * (tm * tk * a.dtype.itemsize + tk * tn * b.dtype.itemsize + tm * tn * osz) + tm * tn * 4
    o_spec = pl.BlockSpec((tm, tn), lambda j, i, kk: (i, j))
    o_shape = jax.ShapeDtypeStruct((m if m_out is None else m_out, n), out_dtype)
    extra = ride is not None
    return pl.pallas_call(
        body, name=name, grid=grid,
        in_specs=[a_spec, b_spec] + [_ANY] * n_ride_in,
        out_specs=[o_spec, _ANY] if extra else o_spec,
        out_shape=[o_shape, ride.out_shape] if extra else o_shape,
        scratch_shapes=[pltpu.VMEM((tm, tn) if nk > 1 else (8, LANE), F32)] + (_RIDE_SCRATCH if extra else []),
        input_output_aliases={3: 1} if extra and ride.into is not None else {},
        compiler_params=_vmem(need + (12 << 20)),
    )(a, b, *(ride.operands() if extra else []))


ROWS = 256


def _ln_stats(x):
    mu = jnp.mean(x, axis=-1, keepdims=True)
    xc = x - mu
    var = jnp.mean(xc * xc, axis=-1, keepdims=True)
    rstd = lax.rsqrt(var + LN_EPS)
    return xc * rstd, rstd


def _token_specs(nbc, nbx, d):
    return [pl.BlockSpec((ROWS, d), lambda i: (jnp.minimum(i, nbc - 1), 0)),
            pl.BlockSpec((ROWS, d), lambda i: (jnp.maximum(i - nbc, 0), 0))]


def _tokens(c_ref, x_ref, nbc):
    return jnp.where(pl.program_id(0) < nbc, c_ref[...], x_ref[...])


def _modulate_fwd(ctx, x, modp, gather=None):
    d = x.shape[1]
    nbc, nbx = ctx.shape[0] // ROWS, x.shape[0] // ROWS
    riding = gather is not None

    def body(c_ref, x_ref, mod_ref, *rest):
        if riding:
            comm = (rest[0], rest[2]) + tuple(rest[3:])
            o_ref = rest[1]

            @pl.when(pl.program_id(0) == 0)
            def _():
                gather.start(*comm)
        else:
            o_ref = rest[0]
        n, _ = _ln_stats(_tokens(c_ref, x_ref, nbc))
        o_ref[...] = (n * (1.0 + mod_ref[0, 1:2, :]) + mod_ref[0, 0:1, :]).astype(BF16)
        if riding:
            @pl.when(pl.program_id(0) == nbc + nbx - 1)
            def _():
                gather.finish(*comm)

    o_spec = pl.BlockSpec((ROWS, d), lambda i: (i, 0))
    o_shape = jax.ShapeDtypeStruct((ctx.shape[0] + x.shape[0], d), BF16)
    return pl.pallas_call(
        body, name="modulate_fwd", grid=(nbc + nbx,),
        in_specs=_token_specs(nbc, nbx, d) + [pl.BlockSpec((1, 3, d), lambda i: (jnp.where(i >= nbc, 1, 0), 0, 0))]
        + ([_ANY] if riding else []),
        out_specs=[o_spec, _ANY] if riding else o_spec,
        out_shape=[o_shape, gather.out_shape] if riding else o_shape,
        scratch_shapes=gather.scratch if riding else [],
    )(ctx, x, modp, *([gather.x] if riding else []))


def _modulate_bwd(dh, ctx, x, modp, dxa):
    t, d = x.shape
    nbc, nbx = ctx.shape[0] // ROWS, t // ROWS

    def body(dh_ref, c_ref, x_ref, mod_ref, dxa_ref, gx_ref, sum_ref):
        i = pl.program_id(0)
        n, rstd = _ln_stats(_tokens(c_ref, x_ref, nbc))
        g = dh_ref[...]
        dn = g * (1.0 + mod_ref[0, 1:2, :])
        dx = rstd * (dn - jnp.mean(dn, axis=-1, keepdims=True) - n * jnp.mean(dn * n, axis=-1, keepdims=True))
        gx_ref[...] = dx + dxa_ref[...]
        dshift = jnp.sum(g, axis=0, keepdims=True)
        dscale = jnp.sum(g * n, axis=0, keepdims=True)

        @pl.when(i == 0)
        def _():
            sum_ref[...] = jnp.zeros_like(sum_ref)

        @pl.when(i < nbc)
        def _():
            sum_ref[0:1, :] += dshift
            sum_ref[1:2, :] += dscale

        @pl.when(i >= nbc)
        def _():
            sum_ref[2:3, :] += dshift
            sum_ref[3:4, :] += dscale

    lat = lambda i: (jnp.maximum(i - nbc, 0), 0)
    return pl.pallas_call(
        body, name="modulate_bwd", grid=(nbc + nbx,),
        in_specs=[pl.BlockSpec((ROWS, d), lambda i: (i, 0))] + _token_specs(nbc, nbx, d)
        + [pl.BlockSpec((1, 3, d), lambda i: (jnp.where(i >= nbc, 1, 0), 0, 0)), pl.BlockSpec((ROWS, d), lat)],
        out_specs=[pl.BlockSpec((ROWS, d), lat), pl.BlockSpec((8, d), lambda i: (0, 0))],
        out_shape=[jax.ShapeDtypeStruct((t, d), F32), jax.ShapeDtypeStruct((8, d), F32)],
    )(dh, ctx, x, modp, dxa)


def _post_fwd(o_f, o_b, h_f, h_b, u, hgw, mlw, nbc):
    tt = u.shape[0]
    t = tt - nbc * ROWS

    def body(of_ref, ob_ref, hf_ref, hb_ref, az_ref, bo_ref, bz_ref, hgw_ref, mlw_ref, y_ref):
        o = of_ref[...] + ob_ref[...]
        for sl in _head_slices(W_A, HG_HEADS):
            oh = o[:, sl]
            rs = lax.rsqrt(jnp.mean(oh * oh, axis=-1, keepdims=True) + NORM_EPS)
            y_ref[:, sl] = (oh * rs * hgw_ref[:, sl] * _silu(az_ref[:, sl])).astype(BF16)
        hm = hf_ref[...] + hb_ref[...]
        for sl in _head_slices(W_B, ML_HEADS):
            hh = hm[:, sl]
            mu = jnp.mean(hh, axis=-1, keepdims=True)
            hc = hh - mu
            rstd = lax.rsqrt(jnp.mean(hc * hc, axis=-1, keepdims=True) + NORM_EPS)
            out = hc * rstd * mlw_ref[:, sl] * _sigmoid(bo_ref[:, sl]) * _silu(bz_ref[:, sl])
            y_ref[:, W_A + sl.start:W_A + sl.stop] = out.astype(BF16)

    row = lambda i: (i + nbc, 0)
    seg = lambda s: pl.BlockSpec((ROWS, 1024), lambda i: (i + nbc, s))
    wspec = pl.BlockSpec((1, 1024), lambda i: (0, 0))
    return pl.pallas_call(
        body, name="post_fwd", grid=(t // ROWS,),
        in_specs=[pl.BlockSpec((ROWS, 1024), row)] * 4 + [seg(SEG_AZ), seg(SEG_BO), seg(SEG_BZ), wspec, wspec],
        out_specs=pl.BlockSpec((ROWS, 2048), lambda i: (i, 0)),
        out_shape=jax.ShapeDtypeStruct((t, 2048), BF16),
    )(o_f, o_b, h_f, h_b, u, u, u, hgw, mlw)


def _post_bwd(dz, w_o, o_f, o_b, h_f, h_b, u, hgw, mlw, nbc):
    tt = u.shape[0]
    d = w_o.shape[0]

    def body(dz_ref, w_ref, of_ref, ob_ref, hf_ref, hb_ref, az_ref, bo_ref, bz_ref, hgw_ref, mlw_ref,
             do_ref, dhm_ref, daz_ref, dbo_ref, sum_ref):
        i = pl.program_id(0)
        live = jnp.where(i >= nbc, 1.0, 0.0)
        dy = lax.dot_general(dz_ref[...], w_ref[...], (((1,), (1,)), ((), ())), preferred_element_type=F32) * live

        @pl.when(i == 0)
        def _():
            sum_ref[...] = jnp.zeros_like(sum_ref)

        o = of_ref[...] + ob_ref[...]
        for sl in _head_slices(W_A, HG_HEADS):
            oh = o[:, sl]
            rs = lax.rsqrt(jnp.mean(oh * oh, axis=-1, keepdims=True) + NORM_EPS)
            on = oh * rs
            az = az_ref[:, sl]
            dya = dy[:, sl]
            saz, daz = _silu_both(az)
            doa = dya * saz
            daz_ref[:, sl] = (dya * on * hgw_ref[:, sl] * daz).astype(BF16)
            sum_ref[0:1, sl] += jnp.sum(doa * on, axis=0, keepdims=True)
            don = doa * hgw_ref[:, sl]
            do_ref[:, sl] = rs * (don - on * jnp.mean(don * on, axis=-1, keepdims=True))
        hm = hf_ref[...] + hb_ref[...]
        for sl in _head_slices(W_B, ML_HEADS):
            hh = hm[:, sl]
            mu = jnp.mean(hh, axis=-1, keepdims=True)
            hc = hh - mu
            rstd = lax.rsqrt(jnp.mean(hc * hc, axis=-1, keepdims=True) + NORM_EPS)
            hn = hc * rstd
            hw = hn * mlw_ref[:, sl]
            bo, bz = bo_ref[:, sl], bz_ref[:, sl]
            sbo = _sigmoid(bo)
            sbz, dbz = _silu_both(bz)
            dyb = dy[:, W_A + sl.start:W_A + sl.stop]
            dhw = dyb * sbo * sbz
            dbo_ref[:, sl] = (dyb * hw * sbz * sbo * (1.0 - sbo)).astype(BF16)
            dbo_ref[:, 1024 + sl.start:1024 + sl.stop] = (dyb * hw * sbo * dbz).astype(BF16)
            sum_ref[1:2, sl] += jnp.sum(dhw * hn, axis=0, keepdims=True)
            dhn = dhw * mlw_ref[:, sl]
            dhm_ref[:, sl] = rstd * (dhn - jnp.mean(dhn, axis=-1, keepdims=True)
                                     - hn * jnp.mean(dhn * hn, axis=-1, keepdims=True))

    row = lambda i: (i, 0)
    seg = lambda s: pl.BlockSpec((ROWS, 1024), lambda i: (i, s))
    wspec = pl.BlockSpec((1, 1024), lambda i: (0, 0))
    return pl.pallas_call(
        body, name="post_bwd", grid=(tt // ROWS,),
        in_specs=[pl.BlockSpec((ROWS, 2048), lambda i: (jnp.maximum(i - nbc, 0), 0)), pl.BlockSpec((d, d), lambda i: (0, 0))]
        + [pl.BlockSpec((ROWS, 1024), row)] * 4 + [seg(SEG_AZ), seg(SEG_BO), seg(SEG_BZ), wspec, wspec],
        out_specs=[pl.BlockSpec((ROWS, 1024), row), pl.BlockSpec((ROWS, 1024), row),
                   pl.BlockSpec((ROWS, 1024), row), pl.BlockSpec((ROWS, 2048), row),
                   pl.BlockSpec((8, 1024), lambda i: (0, 0))],
        out_shape=[jax.ShapeDtypeStruct((tt, 1024), F32), jax.ShapeDtypeStruct((tt, 1024), F32),
                   jax.ShapeDtypeStruct((tt, 1024), BF16), jax.ShapeDtypeStruct((tt, 2048), BF16),
                   jax.ShapeDtypeStruct((8, 1024), F32)],
        compiler_params=_vmem(4 * d * d + 30 * ROWS * 2048 * 4),
    )(dz, w_o, o_f, o_b, h_f, h_b, u, u, u, hgw, mlw)


def _final(y, w_o, x, target, modp, ln_g, ln_b):
    t, d = x.shape

    def body(y_ref, w_ref, x_ref, tg_ref, mod_ref, g_ref, b_ref, dz_ref, dxa_ref, sum_ref):
        i = pl.program_id(0)
        zz = lax.dot_general(y_ref[...], w_ref[...], (((1,), (0,)), ((), ())), preferred_element_type=F32)
        gate = mod_ref[0, 2:3, :]
        pre = ALPHA * x_ref[...] + gate * zz
        nh, rstd = _ln_stats(pre)
        err = nh * g_ref[...] + b_ref[...] - tg_ref[...]
        dxo = err * (1.0 / d)
        dnh = dxo * g_ref[...]
        dpre = rstd * (dnh - jnp.mean(dnh, axis=-1, keepdims=True) - nh * jnp.mean(dnh * nh, axis=-1, keepdims=True))
        dz_ref[...] = (gate * dpre).astype(BF16)
        dxa_ref[...] = ALPHA * dpre

        @pl.when(i == 0)
        def _():
            sum_ref[...] = jnp.zeros_like(sum_ref)

        sum_ref[0:1, :] += jnp.sum(dpre * zz, axis=0, keepdims=True)
        sum_ref[1:2, :] += jnp.sum(dxo * nh, axis=0, keepdims=True)
        sum_ref[2:3, :] += jnp.sum(dxo, axis=0, keepdims=True)
        sum_ref[3:4, :] += jnp.sum(err * err, axis=0, keepdims=True)

    row = lambda i: (i, 0)
    vec = pl.BlockSpec((1, d), lambda i: (0, 0))
    return pl.pallas_call(
        body, name="final_ln_loss", grid=(t // ROWS,),
        in_specs=[pl.BlockSpec((ROWS, d), row), pl.BlockSpec((d, d), lambda i: (0, 0)), pl.BlockSpec((ROWS, d), row),
                  pl.BlockSpec((ROWS, d), row), pl.BlockSpec((1, 3, d), lambda i: (1, 0, 0)), vec, vec],
        out_specs=[pl.BlockSpec((ROWS, d), row), pl.BlockSpec((ROWS, d), row), pl.BlockSpec((8, d), lambda i: (0, 0))],
        out_shape=[jax.ShapeDtypeStruct((t, d), BF16), jax.ShapeDtypeStruct((t, d), F32),
                   jax.ShapeDtypeStruct((8, d), F32)],
        compiler_params=_vmem(4 * d * d + 24 * ROWS * d * 4),
    )(y, w_o, x, target, modp, ln_g, ln_b)


GRID_W = 64


def _shift(x, s, ok):
    n = x.shape[0]
    return jnp.where(ok, pltpu.roll(x, s % n, 0), 0.0)


def _grid_masks(n):
    t = lax.broadcasted_iota(jnp.int32, (n, LANE), 0)
    col = t & (GRID_W - 1)
    return dict(left=col >= 1, right=col <= GRID_W - 2, up=t >= GRID_W, down=t < n - GRID_W)


def _seq_masks(n):
    t = lax.broadcasted_iota(jnp.int32, (n, LANE), 0)
    return dict(left=t >= 1, right=t <= n - 2)


def _conv_fwd(u, w9, cb, tc):
    tt = u.shape[0]
    t = tt - tc

    def body(u_ref, w_ref, b_ref, o_ref):
        w = [w_ref[r:r + 1, :] for r in range(9)]
        xc = u_ref[0:tc, :]
        ms = _seq_masks(tc)
        o_ref[0:tc, :] = (w[3] * _shift(xc, 1, ms["left"]) + w[4] * xc + w[5] * _shift(xc, -1, ms["right"])
                          + b_ref[...])
        x = u_ref[tc:tt, :]
        mg = _grid_masks(t)
        taps = (_shift(x, 1, mg["left"]), x, _shift(x, -1, mg["right"]))
        rows = [w[3 * i] * taps[0] + w[3 * i + 1] * taps[1] + w[3 * i + 2] * taps[2] for i in range(3)]
        o_ref[tc:tt, :] = (rows[1] + _shift(rows[0], GRID_W, mg["up"]) + _shift(rows[2], -GRID_W, mg["down"])
                           + b_ref[...])

    return pl.pallas_call(
        body, name="conv_fwd", grid=(2048 // LANE,),
        in_specs=[pl.BlockSpec((tt, LANE), lambda j: (0, BLK_QK + j)), pl.BlockSpec((9, LANE), lambda j: (0, j)),
                  pl.BlockSpec((1, LANE), lambda j: (0, j))],
        out_specs=pl.BlockSpec((tt, LANE), lambda j: (0, j)),
        out_shape=jax.ShapeDtypeStruct((tt, 2048), F32),
        compiler_params=_vmem(40 * tt * LANE * 4),
    )(u, w9, cb)


def _conv_bwd(dcp, u, w9, tc, du):
    tt = u.shape[0]
    t = tt - tc

    def body(d_ref, u_ref, w_ref, du_in_ref, du_ref, gw_ref, gb_ref):
        w = [w_ref[r:r + 1, :] for r in range(9)]
        csum = lambda a: jnp.sum(a, axis=0, keepdims=True)
        dc = d_ref[0:tc, :]
        xc = u_ref[0:tc, :]
        ms = _seq_masks(tc)
        du_ref[0:tc, :] = (w[3] * _shift(dc, -1, ms["right"]) + w[4] * dc + w[5] * _shift(dc, 1, ms["left"])).astype(BF16)
        gmid = [csum(dc * _shift(xc, 1, ms["left"])), csum(dc * xc), csum(dc * _shift(xc, -1, ms["right"]))]
        d = d_ref[tc:tt, :]
        x = u_ref[tc:tt, :]
        mg = _grid_masks(t)
        dtaps = (_shift(d, -1, mg["right"]), d, _shift(d, 1, mg["left"]))
        rows = [w[3 * i] * dtaps[0] + w[3 * i + 1] * dtaps[1] + w[3 * i + 2] * dtaps[2] for i in range(3)]
        du_ref[tc:tt, :] = (rows[1] + _shift(rows[0], -GRID_W, mg["down"]) + _shift(rows[2], GRID_W, mg["up"])).astype(BF16)
        xtaps = (_shift(x, 1, mg["left"]), x, _shift(x, -1, mg["right"]))
        for j in range(3):
            gw_ref[j:j + 1, :] = csum(d * _shift(xtaps[j], GRID_W, mg["up"]))
            gw_ref[3 + j:4 + j, :] = csum(d * xtaps[j]) + gmid[j]
            gw_ref[6 + j:7 + j, :] = csum(d * _shift(xtaps[j], -GRID_W, mg["down"]))
        gb_ref[...] = csum(d) + csum(dc)

    return pl.pallas_call(
        body, name="conv_bwd", grid=(2048 // LANE,),
        in_specs=[pl.BlockSpec((tt, LANE), lambda j: (0, j)), pl.BlockSpec((tt, LANE), lambda j: (0, BLK_QK + j)),
                  pl.BlockSpec((9, LANE), lambda j: (0, j)), _ANY],
        out_specs=[pl.BlockSpec((tt, LANE), lambda j: (0, BLK_QK + j)), pl.BlockSpec((9, LANE), lambda j: (0, j)),
                   pl.BlockSpec((1, LANE), lambda j: (0, j))],
        out_shape=[jax.ShapeDtypeStruct(du.shape, BF16), jax.ShapeDtypeStruct((9, 2048), F32),
                   jax.ShapeDtypeStruct((1, 2048), F32)],
        input_output_aliases={3: 0},
        compiler_params=_vmem(48 * tt * LANE * 4),
    )(dcp, u, w9, du)


SUB = 2
STEP = SUB * CHUNK


def _chunk_of(pos, ncc, nc, rev):
    if not rev:
        return pos
    return jnp.where(pos < ncc, ncc - 1 - pos, nc - 1 - (pos - ncc))


def _sub_rows(rev):
    order = range(SUB - 1, -1, -1) if rev else range(SUB)
    return [(s, slice(s * CHUNK, (s + 1) * CHUNK)) for s in order]


def _hgrn_fwd(u, lower_d, ncc, rev):
    tt = u.shape[0]
    nc, ncc = tt // STEP, ncc // SUB
    seg_f = SEG_AFB if rev else SEG_AFF

    def body(zq_ref, zf_ref, v_ref, lb_ref, o_ref, hist_ref, st_ref):
        @pl.when(pl.program_id(0) == 0)
        def _():
            st_ref[...] = jnp.zeros_like(st_ref)

        st = st_ref[...]
        for s, r in _sub_rows(rev):
            hist_ref[s] = st
            o, st = hg_chunk_fwd(zq_ref[r, :], zf_ref[r, :], v_ref[r, :], lb_ref[...], st, rev)
            o_ref[r, :] = o
        st_ref[...] = st

    seg = lambda s: pl.BlockSpec((STEP, 1024), lambda j: (_chunk_of(j, ncc, nc, rev), s))
    return pl.pallas_call(
        body, name="hgrn_fwd_rev" if rev else "hgrn_fwd", grid=(nc,),
        in_specs=[seg(SEG_AQ), seg(seg_f), seg(SEG_AI), pl.BlockSpec((1, 1024), lambda j: (0, 0))],
        out_specs=[pl.BlockSpec((STEP, 1024), lambda j: (_chunk_of(j, ncc, nc, rev), 0)),
                   pl.BlockSpec((SUB, 1024, HG_D), lambda j: (_chunk_of(j, ncc, nc, rev), 0, 0))],
        out_shape=[jax.ShapeDtypeStruct((tt, 1024), F32), jax.ShapeDtypeStruct((nc * SUB, 1024, HG_D), F32)],
        scratch_shapes=[pltpu.VMEM((1024, HG_D), F32)],
    )(u, u, u, lower_d)


def _hgrn_bwd(u, lower_d, hist, do, ncc, rev, ride=None, final=None):
    tt = u.shape[0]
    nc, ncc = tt // STEP, ncc // SUB
    seg_f = SEG_AFB if rev else SEG_AFF
    is_final = final is not None
    has_a2a = ride is not None
    n_out = 2 if is_final else 4
    width = 5 * 1024

    def body(zq_ref, zf_ref, v_ref, lb_ref, hist_ref, do_ref, *rest):
        if is_final:
            aq_ref, av_ref, af_ref, az_ref = rest[:4]
            rest = rest[4:]
        if has_a2a:
            x_ref, rest = rest[0], rest[1:]
        outs, rest = rest[:n_out], rest[n_out:]
        dlb_ref = outs[-1]
        if has_a2a:
            comm = (x_ref, rest[0]) + tuple(rest[2:])
            dst_ref = rest[1]
        else:
            dst_ref = rest[0]

        @pl.when(pl.program_id(0) == 0)
        def _():
            dst_ref[...] = jnp.zeros_like(dst_ref)
            dlb_ref[...] = jnp.zeros_like(dlb_ref)
            if has_a2a:
                ride.start(*comm)

        dst = dst_ref[...]
        dlb_sum = dlb_ref[...]
        for s, r in reversed(_sub_rows(rev)):
            dzq, dzf, dv, dlb, dst = hg_chunk_bwd(zq_ref[r, :], zf_ref[r, :], v_ref[r, :], lb_ref[...],
                                                  hist_ref[s], do_ref[r, :], dst, rev)
            dlb_sum = dlb_sum + dlb
            if is_final:
                du_ref = outs[0]
                dzf_own, dzf_other = dzf.astype(BF16), af_ref[r, :]
                du_ref[r, 0:1024] = (dzq + aq_ref[r, :]).astype(BF16)
                du_ref[r, 1024:2048] = dzf_other if rev else dzf_own
                du_ref[r, 2048:3072] = dzf_own if rev else dzf_other
                du_ref[r, 3072:4096] = (dv + av_ref[r, :]).astype(BF16)
                du_ref[r, 4096:5120] = az_ref[r, :]
            else:
                dzf_ref, dzq_ref, dv_ref = outs[:3]
                dzf_ref[r, :] = dzf.astype(BF16)
                dzq_ref[r, :] = dzq
                dv_ref[r, :] = dv
        dst_ref[...] = dst
        dlb_ref[...] = dlb_sum

        if has_a2a:
            @pl.when(pl.program_id(0) == nc - 1)
            def _():
                ride.wait(*comm)

    cidx = lambda j: _chunk_of(nc - 1 - j, ncc, nc, rev)
    seg = lambda s: pl.BlockSpec((STEP, 1024), lambda j: (cidx(j), s))
    row = pl.BlockSpec((STEP, 1024), lambda j: (cidx(j), 0))
    dlb_spec = pl.BlockSpec((1, 1024), lambda j: (0, 0))
    dlb_shape = jax.ShapeDtypeStruct((1, 1024), F32)
    if is_final:
        out_specs = [pl.BlockSpec((STEP, width), lambda j: (cidx(j), 0)), dlb_spec]
        out_shape = [jax.ShapeDtypeStruct((tt, N_U), BF16), dlb_shape]
    else:
        out_specs = [row, row, row, dlb_spec]
        out_shape = [jax.ShapeDtypeStruct((tt, 1024), BF16), jax.ShapeDtypeStruct((tt, 1024), F32),
                     jax.ShapeDtypeStruct((tt, 1024), F32), dlb_shape]
    ins = [u, u, u, lower_d, hist, do] + (list(final) if is_final else []) + ([ride.x] if has_a2a else [])
    return pl.pallas_call(
        body, name="hgrn_bwd_rev" if rev else "hgrn_bwd", grid=(nc,),
        in_specs=[seg(SEG_AQ), seg(seg_f), seg(SEG_AI), pl.BlockSpec((1, 1024), lambda j: (0, 0)),
                  pl.BlockSpec((SUB, 1024, HG_D), lambda j: (cidx(j), 0, 0)), row] + ([row] * 4 if is_final else [])
        + ([_ANY] if has_a2a else []),
        out_specs=out_specs + ([_ANY] if has_a2a else []),
        out_shape=out_shape + ([ride.out_shape] if has_a2a else []),
        scratch_shapes=[pltpu.VMEM((1024, HG_D), F32)] + (_RIDE_SCRATCH if has_a2a else []),
    )(*ins)


def _gate_views(g_ref, b_ref, r, head, rev):
    gc = g_ref[r, :] + b_ref[...]
    lane = lax.broadcasted_iota(jnp.int32, (1, LANE), 1)
    eye = _eye()
    d = 1 if rev else 0
    ii, fi = d * ML_HEADS + head, 2 * ML_HEADS + d * ML_HEADS + head
    col = lambda idx: jnp.sum(jnp.where(lane == idx, gc, 0.0), axis=1, keepdims=True)
    row = lambda c: jnp.sum(eye * c, axis=0, keepdims=True)
    gi, gf = col(ii), col(fi)
    return gi, row(gi), gf, row(gf)


def _mlstm_fwd(cpre, u, bias, ncc, rev):
    tt = u.shape[0]
    nc, ncc = tt // STEP, ncc // SUB
    nhd = ML_HEADS

    def body(q_ref, k_ref, v_ref, g_ref, b_ref, h_ref, ch_ref, nh_ref, mh_ref, c_ref, n_ref, m_ref):
        @pl.when(pl.program_id(0) == 0)
        def _():
            c_ref[...] = jnp.zeros_like(c_ref)
            n_ref[...] = jnp.zeros_like(n_ref)
            m_ref[...] = jnp.zeros_like(m_ref)

        c, n_all, m_all = c_ref[...], n_ref[...], m_ref[...]
        n = [n_all[hd:hd + 1, :] for hd in range(nhd)]
        m = [m_all[hd:hd + 1, 0:1] for hd in range(nhd)]
        for s, r in _sub_rows(rev):
            ch_ref[s] = c
            for hd in range(nhd):
                nh_ref[s, hd:hd + 1, :] = n[hd]
                mh_ref[s, hd:hd + 1, :] = jnp.broadcast_to(m[hd], (1, LANE))
            gates = [_gate_views(g_ref, b_ref, r, hd, rev) for hd in range(nhd)]
            h, c, n, m = ml_chunk_fwd(q_ref[r, :], k_ref[r, :], v_ref[r, :], gates, c, n, m, rev)
            h_ref[r, :] = h
        c_ref[...] = c
        for hd in range(nhd):
            n_ref[hd:hd + 1, :] = n[hd]
            m_ref[hd:hd + 1, :] = jnp.broadcast_to(m[hd], (1, LANE))

    cidx = lambda j: _chunk_of(j, ncc, nc, rev)
    row = lambda s: pl.BlockSpec((STEP, 1024), lambda j: (cidx(j), s))
    st3 = lambda a, b: pl.BlockSpec((SUB, a, b), lambda j: (cidx(j), 0, 0))
    return pl.pallas_call(
        body, name="mlstm_fwd_rev" if rev else "mlstm_fwd", grid=(nc,),
        in_specs=[row(0), row(1), row(SEG_BV), pl.BlockSpec((STEP, LANE), lambda j: (cidx(j), BLK_GATE)),
                  pl.BlockSpec((1, LANE), lambda j: (0, 0))],
        out_specs=[row(0), st3(1024, ML_D), st3(8, ML_D), st3(8, LANE)],
        out_shape=[jax.ShapeDtypeStruct((tt, 1024), F32), jax.ShapeDtypeStruct((nc * SUB, 1024, ML_D), F32),
                   jax.ShapeDtypeStruct((nc * SUB, 8, ML_D), F32), jax.ShapeDtypeStruct((nc * SUB, 8, LANE), F32)],
        scratch_shapes=[pltpu.VMEM((1024, ML_D), F32), pltpu.VMEM((8, ML_D), F32), pltpu.VMEM((8, LANE), F32)],
    )(cpre, cpre, u, u, bias)


def _mlstm_bwd(cpre, u, bias, chist, nhist, mhist, h_out, dh, ncc, rev, final=None):
    tt = u.shape[0]
    nc, ncc = tt // STEP, ncc // SUB
    nhd = ML_HEADS
    is_final = final is not None
    d = 1 if rev else 0
    col0, width = SEG_BV * 1024, N_U - SEG_BV * 1024

    def body(q_ref, k_ref, v_ref, g_ref, b_ref, ch_ref, nh_ref, mh_ref, ho_ref, dh_ref, *rest):
        if is_final:
            aqk_ref, av_ref, ag_ref, bo_ref = rest[:4]
            dqk_ref, du_ref, gs_ref, dc_ref, dn_ref = rest[5:]
        else:
            dqk_ref, dv_ref, dg_ref, gs_ref, dc_ref, dn_ref = rest

        @pl.when(pl.program_id(0) == 0)
        def _():
            dc_ref[...] = jnp.zeros_like(dc_ref)
            dn_ref[...] = jnp.zeros_like(dn_ref)
            gs_ref[...] = jnp.zeros_like(gs_ref)

        lane = lax.broadcasted_iota(jnp.int32, (1, LANE), 1)
        dc, dn_all, gs = dc_ref[...], dn_ref[...], gs_ref[...]
        dn = [dn_all[hd:hd + 1, :] for hd in range(nhd)]
        for s, r in reversed(_sub_rows(rev)):
            gates = [_gate_views(g_ref, b_ref, r, hd, rev) for hd in range(nhd)]
            n_all, m_all = nh_ref[s], mh_ref[s]
            dqp, dkp, dv, dgi, dgf, dc, dn = ml_chunk_bwd(
                q_ref[r, :], k_ref[r, :], v_ref[r, :], gates, ch_ref[s],
                [n_all[hd:hd + 1, :] for hd in range(nhd)], [m_all[hd:hd + 1, 0:1] for hd in range(nhd)],
                ho_ref[r, :], dh_ref[r, :], dc, dn, rev)
            dg = ag_ref[r, :] if is_final else jnp.zeros((CHUNK, LANE), F32)
            for hd in range(nhd):
                dg = dg + jnp.where(lane == d * ML_HEADS + hd, dgi[hd], 0.0)
                dg = dg + jnp.where(lane == 2 * ML_HEADS + d * ML_HEADS + hd, dgf[hd], 0.0)
            if is_final:
                dqp = dqp + aqk_ref[r, 0:W_B]
                dkp = dkp + aqk_ref[r, W_B:2 * W_B]
                du_ref[r, 0:1024] = (dv + av_ref[r, :]).astype(BF16)
                du_ref[r, 1024:3072] = bo_ref[r, :]
                du_ref[r, 3072:3072 + LANE] = dg.astype(BF16)
            else:
                dv_ref[r, :] = dv
                dg_ref[r, :] = dg
            dqk_ref[r, 0:W_B] = dqp
            dqk_ref[r, W_B:2 * W_B] = dkp
            gs = gs + jnp.sum(dg, axis=0, keepdims=True)
        dc_ref[...] = dc
        gs_ref[...] = gs
        for hd in range(nhd):
            dn_ref[hd:hd + 1, :] = dn[hd]

    cidx = lambda j: _chunk_of(nc - 1 - j, ncc, nc, rev)
    row = lambda s: pl.BlockSpec((STEP, 1024), lambda j: (cidx(j), s))
    wide = pl.BlockSpec((STEP, 2048), lambda j: (cidx(j), 0))
    gate = pl.BlockSpec((STEP, LANE), lambda j: (cidx(j), 0))
    st3 = lambda a, b: pl.BlockSpec((SUB, a, b), lambda j: (cidx(j), 0, 0))
    gs_spec, gs_shape = pl.BlockSpec((1, LANE), lambda j: (0, 0)), jax.ShapeDtypeStruct((1, LANE), F32)
    dqk_shape = jax.ShapeDtypeStruct((tt, 2048), F32)
    ins = [cpre, cpre, u, u, bias, chist, nhist, mhist, h_out, dh] + (list(final) if is_final else [])
    if is_final:
        out_specs = [wide, pl.BlockSpec((pl.Element(STEP), pl.Element(width)), lambda j: (cidx(j) * STEP, col0)), gs_spec]
        out_shape = [dqk_shape, jax.ShapeDtypeStruct((tt, N_U), BF16), gs_shape]
    else:
        out_specs = [wide, row(0), gate, gs_spec]
        out_shape = [dqk_shape, jax.ShapeDtypeStruct((tt, 1024), F32), jax.ShapeDtypeStruct((tt, LANE), F32), gs_shape]
    return pl.pallas_call(
        body, name="mlstm_bwd_rev" if rev else "mlstm_bwd", grid=(nc,),
        in_specs=[row(0), row(1), row(SEG_BV), pl.BlockSpec((STEP, LANE), lambda j: (cidx(j), BLK_GATE)),
                  pl.BlockSpec((1, LANE), lambda j: (0, 0)),
                  st3(1024, ML_D), st3(8, ML_D), st3(8, LANE), row(0), row(0)]
        + ([wide, row(0), gate, wide, _ANY] if is_final else []),
        out_specs=out_specs, out_shape=out_shape,
        input_output_aliases={14: 1} if is_final else {},
        scratch_shapes=[pltpu.VMEM((1024, ML_D), F32), pltpu.VMEM((8, ML_D), F32)],
    )(*ins)


def _whole(body, out_shape, name, *args, nbytes=0):
    return pl.pallas_call(body, name=name, out_shape=out_shape, compiler_params=_vmem(nbytes))(*args)


def _mod_fwd(cs, w_cols, b_cols):
    def body(c_ref, w_ref, b_ref, o_ref):
        o_ref[...] = _exact_nn(_silu(c_ref[...]), w_ref[...]) + b_ref[...]

    return _whole(body, jax.ShapeDtypeStruct((16, w_cols.shape[1]), F32), "mod_fwd", cs, w_cols, b_cols,
                  nbytes=4 * w_cols.size * 4)


def _mod_bwd_w(cs, d9, w_cols):
    def body(c_ref, d_ref, w_ref, gw_ref, pc_ref):
        gw_ref[...] = _exact_tn(_silu(c_ref[...]), d_ref[...])
        pc = lax.dot_general(d_ref[8:16, :], w_ref[...], (((1,), (1,)), ((), ())), precision=lax.Precision.HIGHEST,
                             preferred_element_type=F32)
        row = lax.broadcasted_iota(jnp.int32, pc.shape, 0)
        pc_ref[...] = jnp.where(row == 0, pc, 0.0)

    return _whole(body, [jax.ShapeDtypeStruct(w_cols.shape, F32), jax.ShapeDtypeStruct((8, w_cols.shape[0]), F32)],
                  "mod_bwd_w", cs, d9, w_cols, nbytes=6 * w_cols.size * 4)


def _lower_fwd(lb4):
    def body(l_ref, o_ref):
        o_ref[...] = jnp.zeros_like(o_ref)
        o_ref[0:1, :] = 1.0 / (1.0 + jnp.exp(l_ref[1:2, :] - l_ref[0:1, :]))
        o_ref[1:2, :] = 1.0 / (1.0 + jnp.exp(l_ref[3:4, :] - l_ref[2:3, :]))

    return _whole(body, jax.ShapeDtypeStruct((8, lb4.shape[1]), F32), "lower_fwd", lb4)


def _reduce8(g, name):
    def body(g_ref, o_ref):
        acc = g_ref[0]
        for k in range(1, N_DEV):
            acc = acc + g_ref[k]
        o_ref[...] = acc

    return _whole(body, jax.ShapeDtypeStruct(g.shape[1:], F32), name, g, nbytes=4 * g.size * 4)


_PACK = (("dmodx", 48), ("dmodc", 48), ("gconvw", 144), ("gconvb", 16), ("dlower", 16), ("ghgw", 8), ("gmlw", 8),
         ("glng", 16), ("glnb", 16), ("losssq", 16), ("ggate", 8))


def _pack_offsets():
    off, out = 0, {}
    for name, rows in _PACK:
        out[name] = (off, rows)
        off += rows
    return out


def _small_finish(total, p0, d_feat):
    offs = _pack_offsets()

    def body(t_ref, p_ref, gb_ref, a0_ref, a1_ref, loss_ref):
        ox, oc, ol, oq = offs["dmodx"][0], offs["dmodc"][0], offs["dlower"][0], offs["losssq"][0]
        gb_ref[...] = t_ref[ox:ox + 48, :] + t_ref[oc:oc + 48, :]
        p = p_ref[...]
        da0 = t_ref[ol:ol + 16, :] * p * (1.0 - p)
        a0_ref[...] = da0
        a1_ref[...] = -da0
        sq = t_ref[oq:oq + 16, :]
        tot = jnp.sum(jnp.sum(sq, axis=1, keepdims=True), axis=0, keepdims=True)
        loss_ref[...] = jnp.broadcast_to(tot * (0.5 / d_feat), loss_ref.shape)

    s = jax.ShapeDtypeStruct
    return _whole(body, [s((48, LANE), F32), s((16, LANE), F32), s((16, LANE), F32), s((8, LANE), F32)],
                  "small_finish", total, p0)


def _cctx_grad(parts, c_ctx8):
    def body(p_ref, c_ref, o_ref):
        acc = p_ref[0]
        for k in range(1, N_DEV):
            acc = acc + p_ref[k]
        o_ref[...] = acc * _dsilu(c_ref[...])

    return _whole(body, jax.ShapeDtypeStruct(c_ctx8.shape, F32), "cctx_grad", parts, c_ctx8)


def _adam_math(w, g, m, v):
    m = ADAM_B1 * m + (1.0 - ADAM_B1) * g
    v = ADAM_B2 * v + (1.0 - ADAM_B2) * (g * g)
    m_hat = m / (1.0 - ADAM_B1 ** ADAM_STEP)
    v_hat = v / (1.0 - ADAM_B2 ** ADAM_STEP)
    delta = -ADAM_LR * (m_hat / (jnp.sqrt(v_hat) + ADAM_EPS) + ADAM_WD * w)
    return delta, m, v


def _adamw(w, g, m, v, rows, name):
    r, c = w.shape

    def body(w_ref, g_ref, m_ref, v_ref, d_ref, mo_ref, vo_ref):
        d_ref[...], mo_ref[...], vo_ref[...] = _adam_math(w_ref[...], g_ref[...], m_ref[...], v_ref[...])

    spec = pl.BlockSpec((rows, c), lambda i: (i, 0))
    return pl.pallas_call(
        body, name=name, grid=(r // rows,), in_specs=[spec] * 4, out_specs=[spec] * 3,
        out_shape=[jax.ShapeDtypeStruct((r, c), F32)] * 3,
        compiler_params=_vmem(16 * rows * (c + LANE) * 4),
    )(w, g, m, v)


def _rs_adamw(recv, w, m, v, tile, name, by_cols=False):
    _, r, c = recv.shape

    def body(r_ref, w_ref, m_ref, v_ref, g_ref, d_ref, mo_ref, vo_ref):
        g = r_ref[0].astype(F32)
        for k in range(1, N_DEV):
            g = g + r_ref[k].astype(F32)
        g_ref[...] = g
        d_ref[...], mo_ref[...], vo_ref[...] = _adam_math(w_ref[...], g, m_ref[...], v_ref[...])

    if by_cols:
        spec = pl.BlockSpec((r, tile), lambda i: (0, i))
        rspec = pl.BlockSpec((N_DEV, r, tile), lambda i: (0, 0, i))
        steps, elems = c // tile, (r + 16) * tile
    else:
        spec = pl.BlockSpec((tile, c), lambda i: (i, 0))
        rspec = pl.BlockSpec((N_DEV, tile, c), lambda i: (0, i, 0))
        steps, elems = r // tile, tile * (c + LANE)
    return pl.pallas_call(
        body, name=name, grid=(steps,), in_specs=[rspec] + [spec] * 3, out_specs=[spec] * 4,
        out_shape=[jax.ShapeDtypeStruct((r, c), F32)] * 4,
        compiler_params=_vmem(2 * elems * (N_DEV * 2 + 7 * 4) + (4 << 20)),
    )(recv, w, m, v)


def _all_gather(x, name):
    r, c = x.shape

    def body(x_ref, out_ref, send_sems, recv_sems, local_sem):
        px, py, pc = _position()
        me, sibling = (px, py, pc), (px, py, 1 - pc)
        chips = [(1 - px, py), (px, 1 - py), (1 - px, 1 - py)]

        def slot(qx, qy, qc):
            return out_ref.at[4 * qx + 2 * qy + qc]

        def copy(k, block, to, src=None):
            return pltpu.make_async_remote_copy(
                src_ref=slot(*block) if src is None else src, dst_ref=slot(*block),
                send_sem=send_sems.at[k], recv_sem=recv_sems.at[k], device_id=to, device_id_type=MESH)

        mine = pltpu.make_async_copy(x_ref, slot(*me), local_sem)
        mine.start()
        first = [copy(1 + j, me, (*chip, pc), src=x_ref) for j, chip in enumerate(chips)]
        first.append(copy(0, me, sibling, src=x_ref))
        for cp in first:
            cp.start()
        passed = [copy(4 + j, (*chip, pc), sibling) for j, chip in enumerate(chips)]
        for j, chip in enumerate(chips):
            copy(1 + j, (*chip, pc), me).wait_recv()
            passed[j].start()
        copy(0, sibling, me).wait_recv()
        for j, chip in enumerate(chips):
            copy(4 + j, (*chip, 1 - pc), me).wait_recv()
        for cp in first + passed:
            cp.wait_send()
        mine.wait()

    return pl.pallas_call(
        body, name=name, out_shape=jax.ShapeDtypeStruct((N_DEV, r, c), x.dtype),
        in_specs=[pl.BlockSpec(memory_space=pl.ANY)], out_specs=pl.BlockSpec(memory_space=pl.ANY),
        scratch_shapes=[pltpu.SemaphoreType.DMA((7,)), pltpu.SemaphoreType.DMA((7,)), pltpu.SemaphoreType.DMA],
    )(x)


class _RelayGather:
    scratch = [pltpu.SemaphoreType.DMA((8,)), pltpu.SemaphoreType.DMA((8,)), pltpu.SemaphoreType.DMA]

    def __init__(self, x):
        self.x = x
        self.half = x.shape[1] // 2
        self.out_shape = jax.ShapeDtypeStruct((N_DEV,) + x.shape, x.dtype)

    def _parts(self, x_ref, out_ref, send_sems, recv_sems, local_sem):
        px, py, pc = _position()
        me, sib = (px, py, pc), (px, py, 1 - pc)
        xn, yn, dg = (1 - px, py, pc), (px, 1 - py, pc), (1 - px, 1 - py, pc)
        half = self.half

        def slot(owner, cols=None):
            ref = out_ref.at[4 * owner[0] + 2 * owner[1] + owner[2]]
            return ref if cols is None else ref.at[:, pl.ds(cols, half)]

        def copy(k, owner, to, src=None, cols=None):
            return pltpu.make_async_remote_copy(
                src_ref=slot(owner, cols) if src is None else src, dst_ref=slot(owner, cols),
                send_sem=send_sems.at[k], recv_sem=recv_sems.at[k], device_id=to, device_id_type=MESH)

        mine = pltpu.make_async_copy(x_ref, slot(me), local_sem)
        own = [copy(1, me, xn, src=x_ref), copy(2, me, yn, src=x_ref), copy(0, me, sib, src=x_ref)]
        return me, sib, xn, yn, dg, copy, mine, own

    def start(self, *refs):
        *_, mine, own = self._parts(*refs)
        mine.start()
        for cp in own:
            cp.start()

    def finish(self, *refs):
        me, sib, xn, yn, dg, copy, mine, own = self._parts(*refs)
        flip = lambda q: (q[0], q[1], 1 - q[2])
        copy(1, xn, me).wait_recv()
        relay_x = [copy(3, xn, yn, cols=0), copy(5, xn, sib)]
        for cp in relay_x:
            cp.start()
        copy(2, yn, me).wait_recv()
        relay_y = [copy(4, yn, xn, cols=self.half), copy(6, yn, sib)]
        for cp in relay_y:
            cp.start()
        copy(3, dg, me, cols=0).wait_recv()
        copy(4, dg, me, cols=self.half).wait_recv()
        relay_d = copy(7, dg, sib)
        relay_d.start()
        copy(0, sib, me).wait_recv()
        copy(5, flip(xn), me).wait_recv()
        copy(6, flip(yn), me).wait_recv()
        copy(7, flip(dg), me).wait_recv()
        for cp in own + relay_x + relay_y + [relay_d]:
            cp.wait_send()
        mine.wait()


DW_PIECES = ((0, 256), (256, 640), (896, 1152))


def _local_step(ctx, x, target, modp, lower, wt_u, w_o, w9, conv_b, gate_b, hgw, mlw, ln_g, ln_b, exchange):
    tc = ctx.shape[0]
    tt = tc + x.shape[0]
    nbc, ncc = tc // ROWS, tc // CHUNK
    lower_f, lower_b = lower[0:1], lower[1:2]

    tmh = _pick(tt, (1088, 768, 512, 256))
    if exchange:
        hc, wt = _modulate_fwd(ctx, x, modp, gather=_RelayGather(wt_u))
        wt_u = jnp.pad(wt.reshape(N_IN, D_MODEL), ((0, N_U - N_IN), (0, 0)))
        u, w_o = _mm(hc, wt_u, "nt", F32, tmh, 1152, D_MODEL, "mm_u", ride=_Ride("gather", w_o))
        w_o = w_o.reshape(D_MODEL, D_MODEL)
    else:
        hc = _modulate_fwd(ctx, x, modp)
        u = _mm(hc, wt_u, "nt", F32, tmh, 1152, D_MODEL, "mm_u")
    cpre = _conv_fwd(u, w9, conv_b, tc)
    bias = jnp.pad(gate_b.reshape(1, 16), ((0, 0), (0, LANE - 16)))

    o_f, hist_f = _hgrn_fwd(u, lower_f, ncc, False)
    o_b, hist_b = _hgrn_fwd(u, lower_b, ncc, True)
    h_f, ch_f, nh_f, mh_f = _mlstm_fwd(cpre, u, bias, ncc, False)
    h_b, ch_b, nh_b, mh_b = _mlstm_fwd(cpre, u, bias, ncc, True)
    y = _post_fwd(o_f, o_b, h_f, h_b, u, hgw, mlw, nbc)
    dz, dxa, fsum = _final(y, w_o, x, target, modp, ln_g, ln_b)

    dw_o = _mm(y, dz, "tn", BF16, D_MODEL, 1024, _pick(y.shape[0], (512, 256)), "mm_dwo")
    do, dhm, daz, dbo, psum = _post_bwd(dz, w_o, o_f, o_b, h_f, h_b, u, hgw, mlw, nbc)
    if exchange:
        dzf_f, dzq, dv_a, dlb_f, dw_o = _hgrn_bwd(
            u, lower_f, hist_f, do, ncc, False, ride=_Ride("a2a", dw_o.reshape(N_DEV, D_MODEL // N_DEV, D_MODEL)))
    else:
        dzf_f, dzq, dv_a, dlb_f = _hgrn_bwd(u, lower_f, hist_f, do, ncc, False)
    du, dlb_b = _hgrn_bwd(u, lower_b, hist_b, do, ncc, True, final=(dzq, dv_a, dzf_f, daz))
    dqk, dv_m, dg, _ = _mlstm_bwd(cpre, u, bias, ch_f, nh_f, mh_f, h_f, dhm, ncc, False)
    dqk, du, gsum = _mlstm_bwd(cpre, u, bias, ch_b, nh_b, mh_b, h_b, dhm, ncc, True, final=(dqk, dv_m, dg, dbo, du))
    du, gconvw, gconvb = _conv_bwd(dqk, u, w9, tc, du)
    tkw = _pick(tt, (2176, 768, 512, 256))
    blocks = lambda g: g.reshape(N_DEV, N_IN // N_DEV, g.shape[1])
    dwu = lambda name, cols, ride: _mm(du, hc, "tn", BF16, 1152, cols[1], tkw, name, b_cols=cols, ride=ride, m_out=N_IN)
    dwt_a = dwu("mm_dwu_a", DW_PIECES[0], None)
    if exchange:
        whole = lambda piece, into: _Ride("a2a", blocks(piece[1]), cols=(piece[0][0], D_MODEL), into=into)
        dwt_b, got = dwu("mm_dwu_b", DW_PIECES[1], whole((DW_PIECES[0], dwt_a), None))
        dwt_c, got = dwu("mm_dwu_c", DW_PIECES[2], whole((DW_PIECES[1], dwt_b), got))
        dh, dwt_u = _mm(du, wt_u, "nn", F32, tmh, D_MODEL // 2, 3456, "mm_dh", ride=whole((DW_PIECES[2], dwt_c), got))
    else:
        dwt_u = jnp.concatenate([dwt_a, dwu("mm_dwu_b", DW_PIECES[1], None), dwu("mm_dwu_c", DW_PIECES[2], None)], axis=1)
        dh = _mm(du, wt_u, "nn", F32, tmh, D_MODEL // 2, 3456, "mm_dh")
    gx, msum = _modulate_bwd(dh, ctx, x, modp, dxa)

    zero_row = jnp.zeros((1, D_MODEL), F32)
    small = dict(
        dmodx=jnp.concatenate([msum[2:3], msum[3:4], fsum[0:1]], axis=0),
        dmodc=jnp.concatenate([msum[0:1], msum[1:2], zero_row], axis=0),
        gconvw=gconvw, gconvb=gconvb, dlower=jnp.concatenate([dlb_f, dlb_b], axis=0),
        ghgw=psum[0:1], gmlw=psum[1:2], glng=fsum[1:2], glnb=fsum[2:3], losssq=fsum[3:4],
        ggate=jnp.concatenate([gsum, jnp.zeros((7, LANE), F32)], axis=0))
    return gx, dwt_u, dw_o, small


def _pack_small(small):
    return jnp.concatenate([small[name].reshape(rows, LANE) for name, rows in _PACK], axis=0)


def _flat_pad(a, rows):
    flat = a.reshape(-1)
    return jnp.pad(flat, (0, rows * LANE - flat.shape[0])).reshape(rows, LANE)


def kernel(x, c, ctx, c_ctx, w_mod, b_mod, w_in, conv_w, conv_b, hg_lb, ml_gate_b, hg_norm_w, ml_norm_w, w_out, ln_g, ln_b, loss_target, m_c_ctx, m_w_mod, m_b_mod, m_w_in, m_conv_w, m_conv_b, m_hg_lb, m_ml_gate_b, m_hg_norm_w, m_ml_norm_w, m_w_out, m_ln_g, m_ln_b, v_c_ctx, v_w_mod, v_b_mod, v_w_in, v_conv_w, v_conv_b, v_hg_lb, v_ml_gate_b, v_hg_norm_w, v_ml_norm_w, v_w_out, v_ln_g, v_ln_b):
    px, py, pc = _position()
    me = 4 * px + 2 * py + pc
    d = D_MODEL
    n_mod = w_mod.shape[2]
    n_cv = conv_w.shape[3]
    n_lb = hg_lb.shape[2]

    pack0 = jnp.concatenate([c.reshape(-1), conv_w.reshape(-1), hg_lb.reshape(-1)]).reshape(1, -1)
    g0 = _all_gather(pack0, "gather_small_inputs")[:, 0, :]
    c_all = g0[:, :d]
    w9 = jnp.transpose(g0[:, d:d + 9 * n_cv].reshape(N_DEV, 9, n_cv), (1, 0, 2)).reshape(9, N_DEV * n_cv)
    lb4 = jnp.transpose(g0[:, d + 9 * n_cv:].reshape(N_DEV, 4, n_lb), (1, 0, 2)).reshape(4, N_DEV * n_lb)
    lower = _lower_fwd(lb4)

    cs = jnp.concatenate([c_all, c_ctx.reshape(1, d), jnp.zeros((7, d), F32)], axis=0)
    b_cols = lax.dynamic_slice(b_mod, (0, me * n_mod), (1, n_mod))
    slab = _mod_fwd(cs, w_mod[0], b_cols)
    mod_all = jnp.transpose(_all_gather(slab, "gather_mod"), (1, 0, 2)).reshape(16, N_DEV * n_mod)
    mod_x = lax.dynamic_slice(mod_all, (me, 0), (1, 3 * d)).reshape(3, d)
    modp = jnp.stack([mod_all[8].reshape(3, d), mod_x])

    gx, recv_wi, recv_wo, small = _local_step(ctx[0], x[0], loss_target[0], modp, lower, w_in[0].T.astype(BF16),
                                              w_out[0].astype(BF16), w9, conv_b, ml_gate_b[0], hg_norm_w, ml_norm_w,
                                              ln_g, ln_b, True)
    g_wi, d_wi, nm_wi, nv_wi = [a.T for a in _rs_adamw(recv_wi, w_in[0].T, m_w_in[0].T, v_w_in[0].T, 256,
                                                       "adamw_w_in", by_cols=True)]
    g_wo, d_wo, nm_wo, nv_wo = _rs_adamw(recv_wo, w_out[0], m_w_out[0], v_w_out[0], 64, "adamw_w_out")

    packs = _all_gather(_pack_small(small), "gather_small_grads")
    total = _reduce8(packs, "reduce_small_grads")
    offs = _pack_offsets()
    piece = lambda name: total[offs[name][0]:offs[name][0] + offs[name][1]]
    g_bmod, g_lb0, g_lb1, loss8 = _small_finish(total, lower[0:2].reshape(16, LANE), float(d))

    ox = offs["dmodx"][0]
    dmodx_all = packs[:, ox:ox + 48, :].reshape(N_DEV, 3 * d)
    dmodc_tot = piece("dmodc").reshape(1, 3 * d)
    d9 = jnp.concatenate([dmodx_all, dmodc_tot, jnp.zeros((7, 3 * d), F32)], axis=0)
    d9_cols = lax.dynamic_slice(d9, (0, me * n_mod), (16, n_mod))
    g_wmod, pc_part = _mod_bwd_w(cs, d9_cols, w_mod[0])
    c_ctx8 = jnp.concatenate([c_ctx.reshape(1, d), jnp.zeros((7, d), F32)], axis=0)
    g_cctx = _cctx_grad(_all_gather(pc_part, "gather_cctx"), c_ctx8)[0]
    d_wmod, nm_wmod, nv_wmod = _adamw(w_mod[0], g_wmod, m_w_mod[0], v_w_mod[0], 256, "adamw_w_mod")

    g_convw_full = piece("gconvw").reshape(9, d)
    g_convw = lax.dynamic_slice(g_convw_full, (0, me * n_cv), (9, n_cv)).reshape(conv_w.shape)
    lb_full = jnp.stack([jnp.stack([g_lb0[0:8].reshape(-1), g_lb1[0:8].reshape(-1)]),
                         jnp.stack([g_lb0[8:16].reshape(-1), g_lb1[8:16].reshape(-1)])])
    g_hglb = lax.dynamic_slice(lb_full, (0, 0, me * n_lb), (2, 2, n_lb))
    grads = dict(
        c_ctx=g_cctx, b_mod=g_bmod.reshape(b_mod.shape), conv_w=g_convw, conv_b=piece("gconvb").reshape(conv_b.shape),
        hg_lb=g_hglb, ml_gate_b=piece("ggate")[0, :16].reshape(ml_gate_b.shape),
        hg_norm_w=piece("ghgw").reshape(hg_norm_w.shape), ml_norm_w=piece("gmlw").reshape(ml_norm_w.shape),
        ln_g=piece("glng").reshape(ln_g.shape), ln_b=piece("glnb").reshape(ln_b.shape))
    params = dict(c_ctx=(c_ctx, m_c_ctx, v_c_ctx), b_mod=(b_mod, m_b_mod, v_b_mod), conv_w=(conv_w, m_conv_w, v_conv_w),
                  conv_b=(conv_b, m_conv_b, v_conv_b), hg_lb=(hg_lb, m_hg_lb, v_hg_lb),
                  ml_gate_b=(ml_gate_b, m_ml_gate_b, v_ml_gate_b), hg_norm_w=(hg_norm_w, m_hg_norm_w, v_hg_norm_w),
                  ml_norm_w=(ml_norm_w, m_ml_norm_w, v_ml_norm_w), ln_g=(ln_g, m_ln_g, v_ln_g), ln_b=(ln_b, m_ln_b, v_ln_b))
    names = list(params)
    rows_of = {n: -(-params[n][0].size // LANE) for n in names}
    rows_tot = -(-sum(rows_of.values()) // 8) * 8
    cat = lambda arrs: jnp.concatenate(
        [_flat_pad(a, rows_of[n]) for n, a in zip(names, arrs)]
        + [jnp.ones((rows_tot - sum(rows_of.values()), LANE), F32)], axis=0)
    d_s, m_s, v_s = _adamw(cat([params[n][0] for n in names]), cat([grads[n] for n in names]),
                           cat([params[n][1] for n in names]), cat([params[n][2] for n in names]), rows_tot, "adamw_small")
    delta, new_m, new_v, off = {}, {}, {}, 0
    for n in names:
        shape, size = params[n][0].shape, params[n][0].size
        take = lambda a: a[off:off + rows_of[n]].reshape(-1)[:size].reshape(shape)
        delta[n], new_m[n], new_v[n] = take(d_s), take(m_s), take(v_s)
        off += rows_of[n]
    grads.update(w_mod=g_wmod[None], w_in=g_wi[None], w_out=g_wo[None])
    delta.update(w_mod=d_wmod[None], w_in=d_wi[None], w_out=d_wo[None])
    new_m.update(w_mod=nm_wmod[None], w_in=nm_wi[None], w_out=nm_wo[None])
    new_v.update(w_mod=nv_wmod[None], w_in=nv_wi[None], w_out=nv_wo[None])

    order = ("c_ctx", "w_mod", "b_mod", "w_in", "conv_w", "conv_b", "hg_lb", "ml_gate_b", "hg_norm_w", "ml_norm_w",
             "w_out", "ln_g", "ln_b")
    return (loss8[0, 0], gx[None], *[grads[n] for n in order], *[delta[n] for n in order],
            *[new_m[n] for n in order], *[new_v[n] for n in order])
```

```python
import jax
import jax.numpy as jnp
from jax import lax
from jax.experimental import pallas as pl
from jax.experimental.pallas import tpu as pltpu

F32 = jnp.float32
BF16 = jnp.bfloat16

D_MODEL = 2048
W_A = 1024
W_B = 1024
HG_HEADS = 8
HG_D = 128
ML_HEADS = 4
ML_D = 256
CHUNK = 64
N_IN = 10256
LANE = 128
N_U = 81 * LANE
N_DEV = 8
ALPHA = 2.0 ** 0.25
LN_EPS = 1e-5
NORM_EPS = 1e-6
ADAM_LR, ADAM_B1, ADAM_B2, ADAM_EPS, ADAM_WD, ADAM_STEP = 0.001, 0.9, 0.999, 1e-08, 0.01, 10
VMEM_CAP = 60 * 1024 * 1024

SEG_AQ, SEG_AFF, SEG_AFB, SEG_AI, SEG_AZ = range(5)
BLK_QK = 40
SEG_BV, SEG_BO, SEG_BZ = 7, 8, 9
BLK_GATE = 80

MESH = pl.DeviceIdType.MESH


def _vmem(nbytes):
    return pltpu.CompilerParams(vmem_limit_bytes=int(min(VMEM_CAP, max(nbytes, 16 * 1024 * 1024))))


def _sigmoid(x):
    return 1.0 / (1.0 + jnp.exp(-x))


def _silu(x):
    return x * _sigmoid(x)


def _dsilu(x):
    s = _sigmoid(x)
    return s * (1.0 + x * (1.0 - s))


def _silu_both(x):
    s = _sigmoid(x)
    return x * s, s * (1.0 + x * (1.0 - s))


def _bdot(a, b, dims):
    return lax.dot_general(a.astype(BF16), b.astype(BF16), (dims, ((), ())), preferred_element_type=F32)


def _nn(a, b):
    return _bdot(a, b, ((1,), (0,)))


def _nt(a, b):
    return _bdot(a, b, ((1,), (1,)))


def _tn(a, b):
    return _bdot(a, b, ((0,), (0,)))


def _exact_nn(a, b):
    return lax.dot_general(a, b, (((1,), (0,)), ((), ())), precision=lax.Precision.HIGHEST,
                           preferred_element_type=F32)


def _exact_tn(a, b):
    return lax.dot_general(a, b, (((0,), (0,)), ((), ())), precision=lax.Precision.HIGHEST,
                           preferred_element_type=F32)


def _tri(rev):
    t = lax.broadcasted_iota(jnp.int32, (CHUNK, CHUNK), 0)
    s = lax.broadcasted_iota(jnp.int32, (CHUNK, CHUNK), 1)
    return (s >= t) if rev else (s <= t)


def _eye():
    t = lax.broadcasted_iota(jnp.int32, (CHUNK, CHUNK), 0)
    s = lax.broadcasted_iota(jnp.int32, (CHUNK, CHUNK), 1)
    return (s == t).astype(F32)


def _row_to_col(row):
    return jnp.sum(_eye() * row, axis=1, keepdims=True)


def _last_onehot(rev):
    t = lax.broadcasted_iota(jnp.int32, (CHUNK, 1), 0)
    return (t == (0 if rev else CHUNK - 1)).astype(F32)


def _head_slices(width, n_heads):
    hd = width // n_heads
    return [slice(h * hd, (h + 1) * hd) for h in range(n_heads)]


def _scan_sum(x, rev):
    n = x.shape[0]
    t = lax.broadcasted_iota(jnp.int32, x.shape, 0)
    s = 1
    while s < n:
        if rev:
            x = x + jnp.where(t < n - s, pltpu.roll(x, n - s, 0), 0.0)
        else:
            x = x + jnp.where(t >= s, pltpu.roll(x, s, 0), 0.0)
        s *= 2
    return x


def _dot3(a, b, dims):
    a_hi, b_hi = a.astype(BF16), b.astype(BF16)
    a_lo, b_lo = (a - a_hi.astype(F32)).astype(BF16), (b - b_hi.astype(F32)).astype(BF16)
    dot = lambda x, y: lax.dot_general(x, y, (dims, ((), ())), preferred_element_type=F32)
    return dot(a_hi, b_hi) + (dot(a_hi, b_lo) + dot(a_lo, b_hi))


def _hg_common(zq, zf, lb, rev):
    q, dq_dz = _silu_both(zq)
    sg = _sigmoid(zf)
    f = lb + (1.0 - lb) * sg
    g = jnp.log(f)
    k = 1.0 - f
    b = _scan_sum(g, rev)
    b_last = jnp.sum(g, axis=0, keepdims=True)
    r = b[CHUNK // 2:CHUNK // 2 + 1, :]
    e_up = jnp.exp(b - r)
    e_dn = jnp.exp(r - b)
    e_b = e_up * jnp.exp(r)
    e_lb = e_dn * jnp.exp(b_last - r)
    return dict(q=q, dq_dz=dq_dz, sg=sg, f=f, k=k, e_up=e_up, e_dn=e_dn, e_b=e_b, e_lb=e_lb, e_last=jnp.exp(b_last),
                q_t=q * e_up, k_t=k * e_dn, q_s=q * e_b, k_h=k * e_lb, tri=_tri(rev).astype(F32))


def hg_chunk_fwd(zq, zf, v, lb, st, rev):
    c = _hg_common(zq, zf, lb, rev)
    hs = _head_slices(zq.shape[1], zq.shape[1] // HG_D)
    s = [_nt(c["q_t"][:, sl], c["k_t"][:, sl]) for sl in hs]
    oi = [_nt(c["q_s"][:, sl], st[sl, :]) for sl in hs]
    ds = [_tn(v[:, sl], c["k_h"][:, sl]) for sl in hs]
    oa = [_nn(c["tri"] * s_h, v[:, sl]) for s_h, sl in zip(s, hs)]
    o = jnp.concatenate([x + y for x, y in zip(oi, oa)], axis=1)
    st_new = jnp.concatenate([st[sl, :] * c["e_last"][:, sl] + d for sl, d in zip(hs, ds)], axis=0)
    return o, st_new


def hg_chunk_bwd(zq, zf, v, lb, st, do, dst_new, rev):
    c = _hg_common(zq, zf, lb, rev)
    hs = _head_slices(zq.shape[1], zq.shape[1] // HG_D)
    tri, q_t, k_t, q_s, k_h = c["tri"], c["q_t"], c["k_t"], c["q_s"], c["k_h"]
    s = [_nt(q_t[:, sl], k_t[:, sl]) for sl in hs]
    da = [tri * _nt(do[:, sl], v[:, sl]) for sl in hs]
    dq_s = [_nn(do[:, sl], st[sl, :]) for sl in hs]
    dk_h = [_nn(v[:, sl], dst_new[sl, :]) for sl in hs]
    dv_s = [_nt(k_h[:, sl], dst_new[sl, :]) for sl in hs]
    dst_q = [_tn(do[:, sl], q_s[:, sl]) for sl in hs]
    dq_t = [_dot3(da_h, k_t[:, sl], ((1,), (0,))) for da_h, sl in zip(da, hs)]
    dk_t = [_dot3(da_h, q_t[:, sl], ((0,), (0,))) for da_h, sl in zip(da, hs)]
    dv_a = [_tn(tri * s_h, do[:, sl]) for s_h, sl in zip(s, hs)]
    cat = lambda parts: jnp.concatenate(parts, axis=1)
    dq_s, dk_h, dq_t, dk_t = cat(dq_s), cat(dk_h), cat(dq_t), cat(dk_t)
    dv = cat([x + y for x, y in zip(dv_a, dv_s)])
    dst = jnp.concatenate([dst_new[sl, :] * c["e_last"][:, sl] + d for sl, d in zip(hs, dst_q)], axis=0)
    dq = dq_s * c["e_b"] + dq_t * c["e_up"]
    dk = dk_t * c["e_dn"] + dk_h * c["e_lb"]
    db = c["q"] * dq - c["k"] * dk
    ss = cat([jnp.sum(dst_new[sl, :] * st[sl, :], axis=0, keepdims=True) for sl in hs])
    d_all = jnp.sum(dk_h * k_h, axis=0, keepdims=True) + c["e_last"] * ss
    dg = _scan_sum(db, not rev) + d_all
    dzq = dq * c["dq_dz"]
    df = dg / c["f"] - dk
    dzf = df * (1.0 - lb) * c["sg"] * (1.0 - c["sg"])
    dlb = jnp.sum(df * (1.0 - c["sg"]), axis=0, keepdims=True)
    return dzq, dzf, dv, dlb, dst


def _log_sigmoid(x):
    return jnp.minimum(x, 0.0) - jnp.log(1.0 + jnp.exp(-jnp.abs(x)))


def _each(fn, *lists):
    return [fn(*xs) for xs in zip(*lists)]


def _bf(xs):
    return [x.astype(BF16) for x in xs]


def _ml_forward_parts(qp, kp, v, gates, c, n, m, rev, with_num):
    hs = _head_slices(qp.shape[1], qp.shape[1] // ML_D)
    q_all, dq_dp = _silu_both(qp)
    k_all, dk_dp = _silu_both(kp)
    k_all = k_all * (ML_D ** -0.5)
    q = [q_all[:, sl] for sl in hs]
    k = [k_all[:, sl] for sl in hs]
    vv = [v[:, sl] for sl in hs]
    cc = [c[sl, :] for sl in hs]
    tri_b = _tri(rev)
    tri = tri_b.astype(F32)
    tri_t = _tri(not rev).astype(F32)
    e_last = _last_onehot(rev)
    qb, kb, vb, cb = _bf(q), _bf(k), _bf(vv), _bf(cc)
    qk = _each(_nt, qb, kb)
    parts = []
    for (gi_c, gi_r, gf_c, gf_r), m_h in zip(gates, m):
        lf_c, lf_r = _log_sigmoid(gf_c), _log_sigmoid(gf_r)
        b_c = jnp.sum(tri * lf_r, axis=1, keepdims=True)
        b_r = jnp.sum(tri_t * lf_c, axis=0, keepdims=True)
        log_w = jnp.where(tri_b, b_c - b_r + gi_r, -jnp.inf)
        m_inter = b_c + m_h
        m_t = jnp.maximum(m_inter, jnp.max(log_w, axis=1, keepdims=True))
        m_new = jnp.sum(m_t * e_last, axis=0, keepdims=True)
        b_last = jnp.sum(b_c * e_last, axis=0, keepdims=True)
        parts.append(dict(a=jnp.exp(m_inter - m_t), p=jnp.exp(log_w - m_t), floor=jnp.exp(-m_t), m_new=m_new,
                          ws=jnp.exp(b_last - b_c + gi_c - m_new), decay=jnp.exp(b_last + m_h - m_new), gf_c=gf_c))
    w = [pt["p"] * x for pt, x in zip(parts, qk)]
    wb = _bf(w)
    for pt, q_h, n_h, w_h in zip(parts, q, n, w):
        qn = jnp.sum(q_h * n_h, axis=1, keepdims=True)
        den = pt["a"] * qn + jnp.sum(w_h, axis=1, keepdims=True)
        pt.update(qn=qn, den=den, rinv=1.0 / jnp.maximum(jnp.abs(den), pt["floor"]), w=w_h)
    if with_num:
        qc = _each(_nt, qb, cb)
        wv = _each(_nn, wb, vb)
        for pt, qc_h, wv_h in zip(parts, qc, wv):
            pt.update(num=pt["a"] * qc_h + wv_h)
    return hs, q, k, vv, cc, tri, parts, dict(q=qb, k=kb, v=vb, c=cb, w=wb, dq_dp=dq_dp, dk_dp=dk_dp)


def ml_chunk_fwd(qp, kp, v, gates, c, n, m, rev):
    hs, q, k, vv, cc, tri, parts, bf = _ml_forward_parts(qp, kp, v, gates, c, n, m, rev, True)
    h = jnp.concatenate([pt["num"] * pt["rinv"] for pt in parts], axis=1)
    upd = _each(_tn, [pt["ws"] * v_h for pt, v_h in zip(parts, vv)], bf["k"])
    c_new = jnp.concatenate([pt["decay"] * c_h + u for pt, c_h, u in zip(parts, cc, upd)], axis=0)
    n_new = [pt["decay"] * n_h + jnp.sum(pt["ws"] * k_h, axis=0, keepdims=True) for pt, n_h, k_h in zip(parts, n, k)]
    return h, c_new, n_new, [pt["m_new"] for pt in parts]


def ml_chunk_bwd(qp, kp, v, gates, c, n, m, h_out, dh, dc_new, dn_new, rev):
    hs, q, k, vv, cc, tri, parts, bf = _ml_forward_parts(qp, kp, v, gates, c, n, m, rev, False)
    dcn = [dc_new[sl, :] for sl in hs]
    dcb = _bf(dcn)
    dnum, dden = [], []
    for pt, sl in zip(parts, hs):
        dh_h = dh[:, sl]
        signed_live = jnp.where(jnp.abs(pt["den"]) > pt["floor"], jnp.where(pt["den"] >= 0.0, 1.0, -1.0), 0.0)
        dnum.append(dh_h * pt["rinv"])
        dden.append(-jnp.sum(dh_h * h_out[:, sl], axis=1, keepdims=True) * pt["rinv"] * signed_live)
    dnb = _bf(dnum)
    dw = [x + y for x, y in zip(_each(_nt, dnb, bf["v"]), dden)]
    kdc = _each(_nt, bf["k"], dcb)
    vdc = _each(_nn, bf["v"], dcb)
    dqk = [x * pt["p"] for x, pt in zip(dw, parts)]
    adn = [pt["a"] * x for pt, x in zip(parts, dnum)]
    dqkb, adnb = _bf(dqk), _bf(adn)
    dv_w = _each(_tn, bf["w"], dnb)
    dq_k = _each(_nn, dqkb, bf["k"])
    dq_c = _each(_nn, adnb, bf["c"])
    dk_q = _each(_tn, dqkb, bf["q"])
    dc_q = _each(_tn, adnb, bf["q"])
    dq, dk, dv, dgi, dgf, dc, dn = [], [], [], [], [], [], []
    for i, pt in enumerate(parts):
        a, ws, decay = pt["a"], pt["ws"], pt["decay"]
        add = a * dden[i]
        e = dw[i] * pt["w"]
        dv.append(dv_w[i] + ws * kdc[i])
        dq.append(dq_k[i] + dq_c[i] + add * n[i])
        dk.append(dk_q[i] + ws * vdc[i] + ws * dn_new[i])
        alpha = jnp.sum(q[i] * dq_c[i], axis=1, keepdims=True) + dden[i] * pt["qn"] * a
        omega = (jnp.sum(vdc[i] * k[i], axis=1, keepdims=True) + jnp.sum(k[i] * dn_new[i], axis=1, keepdims=True)) * ws
        delta = decay * (jnp.sum(jnp.sum(dcn[i] * cc[i], axis=1, keepdims=True), axis=0, keepdims=True)
                         + jnp.sum(dn_new[i] * n[i], axis=1, keepdims=True))
        dc.append(decay * dcn[i] + dc_q[i])
        dn.append(decay * dn_new[i] + jnp.sum(add * q[i], axis=0, keepdims=True))
        e_rows = jnp.sum(e, axis=1, keepdims=True)
        e_cols = _row_to_col(jnp.sum(e, axis=0, keepdims=True))
        dgi.append(e_cols + omega)
        db = e_rows + alpha - e_cols - omega
        tail = jnp.sum(omega, axis=0, keepdims=True) + delta
        dlf = _row_to_col(jnp.sum(tri * db, axis=0, keepdims=True)) + tail
        dgf.append(dlf * (1.0 - _sigmoid(pt["gf_c"])))
    cat = lambda xs: jnp.concatenate(xs, axis=1)
    dqp = cat(dq) * bf["dq_dp"]
    dkp = cat(dk) * (ML_D ** -0.5) * bf["dk_dp"]
    return dqp, dkp, cat(dv), dgi, dgf, jnp.concatenate(dc, axis=0), dn


def _pick(n, prefs):
    for p in prefs:
        if n % p == 0:
            return p
    raise ValueError(f"no tile for {n} among {prefs}")


def _position():
    return lax.axis_index("x"), lax.axis_index("y"), lax.axis_index("c")


class _Ride:
    def __init__(self, kind, x, cols=None, into=None):
        self.kind, self.x, self.cols, self.into = kind, x, cols, into
        r, c = x.shape[-2:]
        self.out_shape = jax.ShapeDtypeStruct((N_DEV, r, c if cols is None else cols[1]), x.dtype)
        self.width = c

    def _copies(self, x_ref, out_ref, send_sems, recv_sems, local_sem):
        px, py, pc = _position()
        me = 4 * px + 2 * py + pc
        src = (lambda slot: x_ref) if self.kind == "gather" else (lambda slot: x_ref.at[slot])
        dst = ((lambda slot: out_ref.at[slot]) if self.cols is None
               else (lambda slot: out_ref.at[slot, :, pl.ds(self.cols[0], self.width)]))
        mine = pltpu.make_async_copy(src(me), dst(me), local_sem)
        sends, recvs = [], []
        for k, (fx, fy, fc) in enumerate([(1, 0, 0), (0, 1, 0), (1, 1, 0), (1, 0, 1), (0, 1, 1), (1, 1, 1), (0, 0, 1)]):
            qx, qy, qc = (1 - px if fx else px), (1 - py if fy else py), (1 - pc if fc else pc)
            peer = 4 * qx + 2 * qy + qc
            sends.append(pltpu.make_async_remote_copy(
                src_ref=src(peer), dst_ref=dst(me), send_sem=send_sems.at[k], recv_sem=recv_sems.at[k],
                device_id=(qx, qy, qc), device_id_type=MESH))
            recvs.append(pltpu.make_async_remote_copy(
                src_ref=src(me), dst_ref=dst(peer), send_sem=send_sems.at[k], recv_sem=recv_sems.at[k],
                device_id=(qx, qy, qc), device_id_type=MESH))
        return mine, sends, recvs

    def start(self, *refs):
        mine, sends, _ = self._copies(*refs)
        mine.start()
        for cp in sends:
            cp.start()

    def wait(self, *refs):
        mine, sends, recvs = self._copies(*refs)
        for cp in recvs:
            cp.wait_recv()
        for cp in sends:
            cp.wait_send()
        mine.wait()

    def operands(self):
        return [self.x] + ([self.into] if self.into is not None else [])


_RIDE_SCRATCH = [pltpu.SemaphoreType.DMA((7,)), pltpu.SemaphoreType.DMA((7,)), pltpu.SemaphoreType.DMA]
_ANY = pl.BlockSpec(memory_space=pl.ANY)


def _mm(a, b, mode, out_dtype, tm, tn, tk, name, ride=None, b_cols=None, m_out=None):
    if mode == "nn":
        (m, k), (k2, n) = a.shape, b.shape
    elif mode == "nt":
        (m, k), (n, k2) = a.shape, b.shape
    else:
        (k, m), (k2, n) = a.shape, b.shape
    off, n = (0, n) if b_cols is None else b_cols
    assert k == k2 and m % tm == 0 and n % tn == 0 and k % tk == 0, (a.shape, b.shape, mode, tm, tn, tk)
    assert b_cols is None or (mode != "nt" and off % LANE == 0)
    nk = k // tk
    dims = {"nn": ((1,), (0,)), "nt": ((1,), (1,)), "tn": ((0,), (0,))}[mode]
    a_spec = (pl.BlockSpec((tk, tm), lambda j, i, kk: (kk, i)) if mode == "tn"
              else pl.BlockSpec((tm, tk), lambda j, i, kk: (i, kk)))
    if b_cols is not None:
        b_spec = pl.BlockSpec((pl.Element(tk), pl.Element(tn)),
                              lambda j, i, kk: (pl.multiple_of(kk * tk, LANE), pl.multiple_of(off + j * tn, LANE)))
    elif mode == "nt":
        b_spec = pl.BlockSpec((tn, tk), lambda j, i, kk: (j, kk))
    else:
        b_spec = pl.BlockSpec((tk, tn), lambda j, i, kk: (kk, j))

    grid = (n // tn, m // tm, nk)
    n_ride_in = len(ride.operands()) if ride is not None else 0

    def body(a_ref, b_ref, *rest):
        if ride is not None:
            x_ref = rest[0]
            o_ref, got_ref, acc_ref = rest[n_ride_in:n_ride_in + 3]
            comm = (x_ref, got_ref) + tuple(rest[n_ride_in + 3:])
        else:
            o_ref, acc_ref = rest
        kk = pl.program_id(2)
        step = (pl.program_id(0) * grid[1] + pl.program_id(1)) * nk + kk
        if ride is not None:
            @pl.when(step == 0)
            def _():
                ride.start(*comm)

        part = lax.dot_general(a_ref[...], b_ref[...], (dims, ((), ())), preferred_element_type=F32)
        if nk == 1:
            o_ref[...] = part.astype(o_ref.dtype)
        else:
            @pl.when(kk == 0)
            def _():
                acc_ref[...] = part

            @pl.when(jnp.logical_and(kk > 0, kk < nk - 1))
            def _():
                acc_ref[...] += part

            @pl.when(kk == nk - 1)
            def _():
                o_ref[...] = (acc_ref[...] + part).astype(o_ref.dtype)

        if ride is not None:
            @pl.when(step == grid[0] * grid[1] * nk - 1)
            def _():
                ride.wait(*comm)

    osz = jnp.dtype(out_dtype).itemsize
    need = 2 * (tm * tk * a.dtype.itemsize + tk * tn * b.dtype.itemsize + tm * tn * osz) + tm * tn * 4
    o_spec = pl.BlockSpec((tm, tn), lambda j, i, kk: (i, j))
    o_shape = jax.ShapeDtypeStruct((m if m_out is None else m_out, n), out_dtype)
    extra = ride is not None
    return pl.pallas_call(
        body, name=name, grid=grid,
        in_specs=[a_spec, b_spec] + [_ANY] * n_ride_in,
        out_specs=[o_spec, _ANY] if extra else o_spec,
        out_shape=[o_shape, ride.out_shape] if extra else o_shape,
        scratch_shapes=[pltpu.VMEM((tm, tn) if nk > 1 else (8, LANE), F32)] + (_RIDE_SCRATCH if extra else []),
        input_output_aliases={3: 1} if extra and ride.into is not None else {},
        compiler_params=_vmem(need + (12 << 20)),
    )(a, b, *(ride.operands() if extra else []))


ROWS = 256


def _ln_stats(x):
    mu = jnp.mean(x, axis=-1, keepdims=True)
    xc = x - mu
    var = jnp.mean(xc * xc, axis=-1, keepdims=True)
    rstd = lax.rsqrt(var + LN_EPS)
    return xc * rstd, rstd


def _token_specs(nbc, nbx, d):
    return [pl.BlockSpec((ROWS, d), lambda i: (jnp.minimum(i, nbc - 1), 0)),
            pl.BlockSpec((ROWS, d), lambda i: (jnp.maximum(i - nbc, 0), 0))]


def _tokens(c_ref, x_ref, nbc):
    return jnp.where(pl.program_id(0) < nbc, c_ref[...], x_ref[...])


def _modulate_fwd(ctx, x, modp, gather=None):
    d = x.shape[1]
    nbc, nbx = ctx.shape[0] // ROWS, x.shape[0] // ROWS
    riding = gather is not None

    def body(c_ref, x_ref, mod_ref, *rest):
        if riding:
            comm = (rest[0], rest[2]) + tuple(rest[3:])
            o_ref = rest[1]

            @pl.when(pl.program_id(0) == 0)
            def _():
                gather.start(*comm)
        else:
            o_ref = rest[0]
        n, _ = _ln_stats(_tokens(c_ref, x_ref, nbc))
        o_ref[...] = (n * (1.0 + mod_ref[0, 1:2, :]) + mod_ref[0, 0:1, :]).astype(BF16)
        if riding:
            @pl.when(pl.program_id(0) == nbc + nbx - 1)
            def _():
                gather.finish(*comm)

    o_spec = pl.BlockSpec((ROWS, d), lambda i: (i, 0))
    o_shape = jax.ShapeDtypeStruct((ctx.shape[0] + x.shape[0], d), BF16)
    return pl.pallas_call(
        body, name="modulate_fwd", grid=(nbc + nbx,),
        in_specs=_token_specs(nbc, nbx, d) + [pl.BlockSpec((1, 3, d), lambda i: (jnp.where(i >= nbc, 1, 0), 0, 0))]
        + ([_ANY] if riding else []),
        out_specs=[o_spec, _ANY] if riding else o_spec,
        out_shape=[o_shape, gather.out_shape] if riding else o_shape,
        scratch_shapes=gather.scratch if riding else [],
    )(ctx, x, modp, *([gather.x] if riding else []))


def _modulate_bwd(dh, ctx, x, modp, dxa):
    t, d = x.shape
    nbc, nbx = ctx.shape[0] // ROWS, t // ROWS

    def body(dh_ref, c_ref, x_ref, mod_ref, dxa_ref, gx_ref, sum_ref):
        i = pl.program_id(0)
        n, rstd = _ln_stats(_tokens(c_ref, x_ref, nbc))
        g = dh_ref[...]
        dn = g * (1.0 + mod_ref[0, 1:2, :])
        dx = rstd * (dn - jnp.mean(dn, axis=-1, keepdims=True) - n * jnp.mean(dn * n, axis=-1, keepdims=True))
        gx_ref[...] = dx + dxa_ref[...]
        dshift = jnp.sum(g, axis=0, keepdims=True)
        dscale = jnp.sum(g * n, axis=0, keepdims=True)

        @pl.when(i == 0)
        def _():
            sum_ref[...] = jnp.zeros_like(sum_ref)

        @pl.when(i < nbc)
        def _():
            sum_ref[0:1, :] += dshift
            sum_ref[1:2, :] += dscale

        @pl.when(i >= nbc)
        def _():
            sum_ref[2:3, :] += dshift
            sum_ref[3:4, :] += dscale

    lat = lambda i: (jnp.maximum(i - nbc, 0), 0)
    return pl.pallas_call(
        body, name="modulate_bwd", grid=(nbc + nbx,),
        in_specs=[pl.BlockSpec((ROWS, d), lambda i: (i, 0))] + _token_specs(nbc, nbx, d)
        + [pl.BlockSpec((1, 3, d), lambda i: (jnp.where(i >= nbc, 1, 0), 0, 0)), pl.BlockSpec((ROWS, d), lat)],
        out_specs=[pl.BlockSpec((ROWS, d), lat), pl.BlockSpec((8, d), lambda i: (0, 0))],
        out_shape=[jax.ShapeDtypeStruct((t, d), F32), jax.ShapeDtypeStruct((8, d), F32)],
    )(dh, ctx, x, modp, dxa)


def _post_fwd(o_f, o_b, h_f, h_b, u, hgw, mlw, nbc):
    tt = u.shape[0]
    t = tt - nbc * ROWS

    def body(of_ref, ob_ref, hf_ref, hb_ref, az_ref, bo_ref, bz_ref, hgw_ref, mlw_ref, y_ref):
        o = of_ref[...] + ob_ref[...]
        for sl in _head_slices(W_A, HG_HEADS):
            oh = o[:, sl]
            rs = lax.rsqrt(jnp.mean(oh * oh, axis=-1, keepdims=True) + NORM_EPS)
            y_ref[:, sl] = (oh * rs * hgw_ref[:, sl] * _silu(az_ref[:, sl])).astype(BF16)
        hm = hf_ref[...] + hb_ref[...]
        for sl in _head_slices(W_B, ML_HEADS):
            hh = hm[:, sl]
            mu = jnp.mean(hh, axis=-1, keepdims=True)
            hc = hh - mu
            rstd = lax.rsqrt(jnp.mean(hc * hc, axis=-1, keepdims=True) + NORM_EPS)
            out = hc * rstd * mlw_ref[:, sl] * _sigmoid(bo_ref[:, sl]) * _silu(bz_ref[:, sl])
            y_ref[:, W_A + sl.start:W_A + sl.stop] = out.astype(BF16)

    row = lambda i: (i + nbc, 0)
    seg = lambda s: pl.BlockSpec((ROWS, 1024), lambda i: (i + nbc, s))
    wspec = pl.BlockSpec((1, 1024), lambda i: (0, 0))
    return pl.pallas_call(
        body, name="post_fwd", grid=(t // ROWS,),
        in_specs=[pl.BlockSpec((ROWS, 1024), row)] * 4 + [seg(SEG_AZ), seg(SEG_BO), seg(SEG_BZ), wspec, wspec],
        out_specs=pl.BlockSpec((ROWS, 2048), lambda i: (i, 0)),
        out_shape=jax.ShapeDtypeStruct((t, 2048), BF16),
    )(o_f, o_b, h_f, h_b, u, u, u, hgw, mlw)


def _post_bwd(dz, w_o, o_f, o_b, h_f, h_b, u, hgw, mlw, nbc):
    tt = u.shape[0]
    d = w_o.shape[0]

    def body(dz_ref, w_ref, of_ref, ob_ref, hf_ref, hb_ref, az_ref, bo_ref, bz_ref, hgw_ref, mlw_ref,
             do_ref, dhm_ref, daz_ref, dbo_ref, sum_ref):
        i = pl.program_id(0)
        live = jnp.where(i >= nbc, 1.0, 0.0)
        dy = lax.dot_general(dz_ref[...], w_ref[...], (((1,), (1,)), ((), ())), preferred_element_type=F32) * live

        @pl.when(i == 0)
        def _():
            sum_ref[...] = jnp.zeros_like(sum_ref)

        o = of_ref[...] + ob_ref[...]
        for sl in _head_slices(W_A, HG_HEADS):
            oh = o[:, sl]
            rs = lax.rsqrt(jnp.mean(oh * oh, axis=-1, keepdims=True) + NORM_EPS)
            on = oh * rs
            az = az_ref[:, sl]
            dya = dy[:, sl]
            saz, daz = _silu_both(az)
            doa = dya * saz
            daz_ref[:, sl] = (dya * on * hgw_ref[:, sl] * daz).astype(BF16)
            sum_ref[0:1, sl] += jnp.sum(doa * on, axis=0, keepdims=True)
            don = doa * hgw_ref[:, sl]
            do_ref[:, sl] = rs * (don - on * jnp.mean(don * on, axis=-1, keepdims=True))
        hm = hf_ref[...] + hb_ref[...]
        for sl in _head_slices(W_B, ML_HEADS):
            hh = hm[:, sl]
            mu = jnp.mean(hh, axis=-1, keepdims=True)
            hc = hh - mu
            rstd = lax.rsqrt(jnp.mean(hc * hc, axis=-1, keepdims=True) + NORM_EPS)
            hn = hc * rstd
            hw = hn * mlw_ref[:, sl]
            bo, bz = bo_ref[:, sl], bz_ref[:, sl]
            sbo = _sigmoid(bo)
            sbz, dbz = _silu_both(bz)
            dyb = dy[:, W_A + sl.start:W_A + sl.stop]
            dhw = dyb * sbo * sbz
            dbo_ref[:, sl] = (dyb * hw * sbz * sbo * (1.0 - sbo)).astype(BF16)
            dbo_ref[:, 1024 + sl.start:1024 + sl.stop] = (dyb * hw * sbo * dbz).astype(BF16)
            sum_ref[1:2, sl] += jnp.sum(dhw * hn, axis=0, keepdims=True)
            dhn = dhw * mlw_ref[:, sl]
            dhm_ref[:, sl] = rstd * (dhn - jnp.mean(dhn, axis=-1, keepdims=True)
                                     - hn * jnp.mean(dhn * hn, axis=-1, keepdims=True))

    row = lambda i: (i, 0)
    seg = lambda s: pl.BlockSpec((ROWS, 1024), lambda i: (i, s))
    wspec = pl.BlockSpec((1, 1024), lambda i: (0, 0))
    return pl.pallas_call(
        body, name="post_bwd", grid=(tt // ROWS,),
        in_specs=[pl.BlockSpec((ROWS, 2048), lambda i: (jnp.maximum(i - nbc, 0), 0)), pl.BlockSpec((d, d), lambda i: (0, 0))]
        + [pl.BlockSpec((ROWS, 1024), row)] * 4 + [seg(SEG_AZ), seg(SEG_BO), seg(SEG_BZ), wspec, wspec],
        out_specs=[pl.BlockSpec((ROWS, 1024), row), pl.BlockSpec((ROWS, 1024), row),
                   pl.BlockSpec((ROWS, 1024), row), pl.BlockSpec((ROWS, 2048), row),
                   pl.BlockSpec((8, 1024), lambda i: (0, 0))],
        out_shape=[jax.ShapeDtypeStruct((tt, 1024), F32), jax.ShapeDtypeStruct((tt, 1024), F32),
                   jax.ShapeDtypeStruct((tt, 1024), BF16), jax.ShapeDtypeStruct((tt, 2048), BF16),
                   jax.ShapeDtypeStruct((8, 1024), F32)],
        compiler_params=_vmem(4 * d * d + 30 * ROWS * 2048 * 4),
    )(dz, w_o, o_f, o_b, h_f, h_b, u, u, u, hgw, mlw)


def _final(y, w_o, x, target, modp, ln_g, ln_b):
    t, d = x.shape

    def body(y_ref, w_ref, x_ref, tg_ref, mod_ref, g_ref, b_ref, dz_ref, dxa_ref, sum_ref):
        i = pl.program_id(0)
        zz = lax.dot_general(y_ref[...], w_ref[...], (((1,), (0,)), ((), ())), preferred_element_type=F32)
        gate = mod_ref[0, 2:3, :]
        pre = ALPHA * x_ref[...] + gate * zz
        nh, rstd = _ln_stats(pre)
        err = nh * g_ref[...] + b_ref[...] - tg_ref[...]
        dxo = err * (1.0 / d)
        dnh = dxo * g_ref[...]
        dpre = rstd * (dnh - jnp.mean(dnh, axis=-1, keepdims=True) - nh * jnp.mean(dnh * nh, axis=-1, keepdims=True))
        dz_ref[...] = (gate * dpre).astype(BF16)
        dxa_ref[...] = ALPHA * dpre

        @pl.when(i == 0)
        def _():
            sum_ref[...] = jnp.zeros_like(sum_ref)

        sum_ref[0:1, :] += jnp.sum(dpre * zz, axis=0, keepdims=True)
        sum_ref[1:2, :] += jnp.sum(dxo * nh, axis=0, keepdims=True)
        sum_ref[2:3, :] += jnp.sum(dxo, axis=0, keepdims=True)
        sum_ref[3:4, :] += jnp.sum(err * err, axis=0, keepdims=True)

    row = lambda i: (i, 0)
    vec = pl.BlockSpec((1, d), lambda i: (0, 0))
    return pl.pallas_call(
        body, name="final_ln_loss", grid=(t // ROWS,),
        in_specs=[pl.BlockSpec((ROWS, d), row), pl.BlockSpec((d, d), lambda i: (0, 0)), pl.BlockSpec((ROWS, d), row),
                  pl.BlockSpec((ROWS, d), row), pl.BlockSpec((1, 3, d), lambda i: (1, 0, 0)), vec, vec],
        out_specs=[pl.BlockSpec((ROWS, d), row), pl.BlockSpec((ROWS, d), row), pl.BlockSpec((8, d), lambda i: (0, 0))],
        out_shape=[jax.ShapeDtypeStruct((t, d), BF16), jax.ShapeDtypeStruct((t, d), F32),
                   jax.ShapeDtypeStruct((8, d), F32)],
        compiler_params=_vmem(4 * d * d + 24 * ROWS * d * 4),
    )(y, w_o, x, target, modp, ln_g, ln_b)


GRID_W = 64


def _shift(x, s, ok):
    n = x.shape[0]
    return jnp.where(ok, pltpu.roll(x, s % n, 0), 0.0)


def _grid_masks(n):
    t = lax.broadcasted_iota(jnp.int32, (n, LANE), 0)
    col = t & (GRID_W - 1)
    return dict(left=col >= 1, right=col <= GRID_W - 2, up=t >= GRID_W, down=t < n - GRID_W)


def _seq_masks(n):
    t = lax.broadcasted_iota(jnp.int32, (n, LANE), 0)
    return dict(left=t >= 1, right=t <= n - 2)


def _conv_fwd(u, w9, cb, tc):
    tt = u.shape[0]
    t = tt - tc

    def body(u_ref, w_ref, b_ref, o_ref):
        w = [w_ref[r:r + 1, :] for r in range(9)]
        xc = u_ref[0:tc, :]
        ms = _seq_masks(tc)
        o_ref[0:tc, :] = (w[3] * _shift(xc, 1, ms["left"]) + w[4] * xc + w[5] * _shift(xc, -1, ms["right"])
                          + b_ref[...])
        x = u_ref[tc:tt, :]
        mg = _grid_masks(t)
        taps = (_shift(x, 1, mg["left"]), x, _shift(x, -1, mg["right"]))
        rows = [w[3 * i] * taps[0] + w[3 * i + 1] * taps[1] + w[3 * i + 2] * taps[2] for i in range(3)]
        o_ref[tc:tt, :] = (rows[1] + _shift(rows[0], GRID_W, mg["up"]) + _shift(rows[2], -GRID_W, mg["down"])
                           + b_ref[...])

    return pl.pallas_call(
        body, name="conv_fwd", grid=(2048 // LANE,),
        in_specs=[pl.BlockSpec((tt, LANE), lambda j: (0, BLK_QK + j)), pl.BlockSpec((9, LANE), lambda j: (0, j)),
                  pl.BlockSpec((1, LANE), lambda j: (0, j))],
        out_specs=pl.BlockSpec((tt, LANE), lambda j: (0, j)),
        out_shape=jax.ShapeDtypeStruct((tt, 2048), F32),
        compiler_params=_vmem(40 * tt * LANE * 4),
    )(u, w9, cb)


def _conv_bwd(dcp, u, w9, tc, du):
    tt = u.shape[0]
    t = tt - tc

    def body(d_ref, u_ref, w_ref, du_in_ref, du_ref, gw_ref, gb_ref):
        w = [w_ref[r:r + 1, :] for r in range(9)]
        csum = lambda a: jnp.sum(a, axis=0, keepdims=True)
        dc = d_ref[0:tc, :]
        xc = u_ref[0:tc, :]
        ms = _seq_masks(tc)
        du_ref[0:tc, :] = (w[3] * _shift(dc, -1, ms["right"]) + w[4] * dc + w[5] * _shift(dc, 1, ms["left"])).astype(BF16)
        gmid = [csum(dc * _shift(xc, 1, ms["left"])), csum(dc * xc), csum(dc * _shift(xc, -1, ms["right"]))]
        d = d_ref[tc:tt, :]
        x = u_ref[tc:tt, :]
        mg = _grid_masks(t)
        dtaps = (_shift(d, -1, mg["right"]), d, _shift(d, 1, mg["left"]))
        rows = [w[3 * i] * dtaps[0] + w[3 * i + 1] * dtaps[1] + w[3 * i + 2] * dtaps[2] for i in range(3)]
        du_ref[tc:tt, :] = (rows[1] + _shift(rows[0], -GRID_W, mg["down"]) + _shift(rows[2], GRID_W, mg["up"])).astype(BF16)
        xtaps = (_shift(x, 1, mg["left"]), x, _shift(x, -1, mg["right"]))
        for j in range(3):
            gw_ref[j:j + 1, :] = csum(d * _shift(xtaps[j], GRID_W, mg["up"]))
            gw_ref[3 + j:4 + j, :] = csum(d * xtaps[j]) + gmid[j]
            gw_ref[6 + j:7 + j, :] = csum(d * _shift(xtaps[j], -GRID_W, mg["down"]))
        gb_ref[...] = csum(d) + csum(dc)

    return pl.pallas_call(
        body, name="conv_bwd", grid=(2048 // LANE,),
        in_specs=[pl.BlockSpec((tt, LANE), lambda j: (0, j)), pl.BlockSpec((tt, LANE), lambda j: (0, BLK_QK + j)),
                  pl.BlockSpec((9, LANE), lambda j: (0, j)), _ANY],
        out_specs=[pl.BlockSpec((tt, LANE), lambda j: (0, BLK_QK + j)), pl.BlockSpec((9, LANE), lambda j: (0, j)),
                   pl.BlockSpec((1, LANE), lambda j: (0, j))],
        out_shape=[jax.ShapeDtypeStruct(du.shape, BF16), jax.ShapeDtypeStruct((9, 2048), F32),
                   jax.ShapeDtypeStruct((1, 2048), F32)],
        input_output_aliases={3: 0},
        compiler_params=_vmem(48 * tt * LANE * 4),
    )(dcp, u, w9, du)


SUB = 4
STEP = SUB * CHUNK


def _chunk_of(pos, ncc, nc, rev):
    if not rev:
        return pos
    return jnp.where(pos < ncc, ncc - 1 - pos, nc - 1 - (pos - ncc))


def _sub_rows(rev):
    order = range(SUB - 1, -1, -1) if rev else range(SUB)
    return [(s, slice(s * CHUNK, (s + 1) * CHUNK)) for s in order]


def _hgrn_fwd(u, lower_d, ncc, rev):
    tt = u.shape[0]
    nc, ncc = tt // STEP, ncc // SUB
    seg_f = SEG_AFB if rev else SEG_AFF

    def body(zq_ref, zf_ref, v_ref, lb_ref, o_ref, hist_ref, st_ref):
        @pl.when(pl.program_id(0) == 0)
        def _():
            st_ref[...] = jnp.zeros_like(st_ref)

        st = st_ref[...]
        for s, r in _sub_rows(rev):
            hist_ref[s] = st
            o, st = hg_chunk_fwd(zq_ref[r, :], zf_ref[r, :], v_ref[r, :], lb_ref[...], st, rev)
            o_ref[r, :] = o
        st_ref[...] = st

    seg = lambda s: pl.BlockSpec((STEP, 1024), lambda j: (_chunk_of(j, ncc, nc, rev), s))
    return pl.pallas_call(
        body, name="hgrn_fwd_rev" if rev else "hgrn_fwd", grid=(nc,),
        in_specs=[seg(SEG_AQ), seg(seg_f), seg(SEG_AI), pl.BlockSpec((1, 1024), lambda j: (0, 0))],
        out_specs=[pl.BlockSpec((STEP, 1024), lambda j: (_chunk_of(j, ncc, nc, rev), 0)),
                   pl.BlockSpec((SUB, 1024, HG_D), lambda j: (_chunk_of(j, ncc, nc, rev), 0, 0))],
        out_shape=[jax.ShapeDtypeStruct((tt, 1024), F32), jax.ShapeDtypeStruct((nc * SUB, 1024, HG_D), F32)],
        scratch_shapes=[pltpu.VMEM((1024, HG_D), F32)],
    )(u, u, u, lower_d)


def _hgrn_bwd(u, lower_d, hist, do, ncc, rev, ride=None, final=None):
    tt = u.shape[0]
    nc, ncc = tt // STEP, ncc // SUB
    seg_f = SEG_AFB if rev else SEG_AFF
    is_final = final is not None
    has_a2a = ride is not None
    n_out = 2 if is_final else 4
    width = 5 * 1024

    def body(zq_ref, zf_ref, v_ref, lb_ref, hist_ref, do_ref, *rest):
        if is_final:
            aq_ref, av_ref, af_ref, az_ref = rest[:4]
            rest = rest[4:]
        if has_a2a:
            x_ref, rest = rest[0], rest[1:]
        outs, rest = rest[:n_out], rest[n_out:]
        dlb_ref = outs[-1]
        if has_a2a:
            comm = (x_ref, rest[0]) + tuple(rest[2:])
            dst_ref = rest[1]
        else:
            dst_ref = rest[0]

        @pl.when(pl.program_id(0) == 0)
        def _():
            dst_ref[...] = jnp.zeros_like(dst_ref)
            dlb_ref[...] = jnp.zeros_like(dlb_ref)
            if has_a2a:
                ride.start(*comm)

        dst = dst_ref[...]
        dlb_sum = dlb_ref[...]
        for s, r in reversed(_sub_rows(rev)):
            dzq, dzf, dv, dlb, dst = hg_chunk_bwd(zq_ref[r, :], zf_ref[r, :], v_ref[r, :], lb_ref[...],
                                                  hist_ref[s], do_ref[r, :], dst, rev)
            dlb_sum = dlb_sum + dlb
            if is_final:
                du_ref = outs[0]
                dzf_own, dzf_other = dzf.astype(BF16), af_ref[r, :]
                du_ref[r, 0:1024] = (dzq + aq_ref[r, :]).astype(BF16)
                du_ref[r, 1024:2048] = dzf_other if rev else dzf_own
                du_ref[r, 2048:3072] = dzf_own if rev else dzf_other
                du_ref[r, 3072:4096] = (dv + av_ref[r, :]).astype(BF16)
                du_ref[r, 4096:5120] = az_ref[r, :]
            else:
                dzf_ref, dzq_ref, dv_ref = outs[:3]
                dzf_ref[r, :] = dzf.astype(BF16)
                dzq_ref[r, :] = dzq
                dv_ref[r, :] = dv
        dst_ref[...] = dst
        dlb_ref[...] = dlb_sum

        if has_a2a:
            @pl.when(pl.program_id(0) == nc - 1)
            def _():
                ride.wait(*comm)

    cidx = lambda j: _chunk_of(nc - 1 - j, ncc, nc, rev)
    seg = lambda s: pl.BlockSpec((STEP, 1024), lambda j: (cidx(j), s))
    row = pl.BlockSpec((STEP, 1024), lambda j: (cidx(j), 0))
    dlb_spec = pl.BlockSpec((1, 1024), lambda j: (0, 0))
    dlb_shape = jax.ShapeDtypeStruct((1, 1024), F32)
    if is_final:
        out_specs = [pl.BlockSpec((STEP, width), lambda j: (cidx(j), 0)), dlb_spec]
        out_shape = [jax.ShapeDtypeStruct((tt, N_U), BF16), dlb_shape]
    else:
        out_specs = [row, row, row, dlb_spec]
        out_shape = [jax.ShapeDtypeStruct((tt, 1024), BF16), jax.ShapeDtypeStruct((tt, 1024), F32),
                     jax.ShapeDtypeStruct((tt, 1024), F32), dlb_shape]
    ins = [u, u, u, lower_d, hist, do] + (list(final) if is_final else []) + ([ride.x] if has_a2a else [])
    return pl.pallas_call(
        body, name="hgrn_bwd_rev" if rev else "hgrn_bwd", grid=(nc,),
        in_specs=[seg(SEG_AQ), seg(seg_f), seg(SEG_AI), pl.BlockSpec((1, 1024), lambda j: (0, 0)),
                  pl.BlockSpec((SUB, 1024, HG_D), lambda j: (cidx(j), 0, 0)), row] + ([row] * 4 if is_final else [])
        + ([_ANY] if has_a2a else []),
        out_specs=out_specs + ([_ANY] if has_a2a else []),
        out_shape=out_shape + ([ride.out_shape] if has_a2a else []),
        scratch_shapes=[pltpu.VMEM((1024, HG_D), F32)] + (_RIDE_SCRATCH if has_a2a else []),
    )(*ins)


def _gate_views(g_ref, b_ref, r, head, rev):
    gc = g_ref[r, :] + b_ref[...]
    lane = lax.broadcasted_iota(jnp.int32, (1, LANE), 1)
    eye = _eye()
    d = 1 if rev else 0
    ii, fi = d * ML_HEADS + head, 2 * ML_HEADS + d * ML_HEADS + head
    col = lambda idx: jnp.sum(jnp.where(lane == idx, gc, 0.0), axis=1, keepdims=True)
    row = lambda c: jnp.sum(eye * c, axis=0, keepdims=True)
    gi, gf = col(ii), col(fi)
    return gi, row(gi), gf, row(gf)


def _mlstm_fwd(cpre, u, bias, ncc, rev):
    tt = u.shape[0]
    nc, ncc = tt // STEP, ncc // SUB
    nhd = ML_HEADS

    def body(q_ref, k_ref, v_ref, g_ref, b_ref, h_ref, ch_ref, nh_ref, mh_ref, c_ref, n_ref, m_ref):
        @pl.when(pl.program_id(0) == 0)
        def _():
            c_ref[...] = jnp.zeros_like(c_ref)
            n_ref[...] = jnp.zeros_like(n_ref)
            m_ref[...] = jnp.zeros_like(m_ref)

        c, n_all, m_all = c_ref[...], n_ref[...], m_ref[...]
        n = [n_all[hd:hd + 1, :] for hd in range(nhd)]
        m = [m_all[hd:hd + 1, 0:1] for hd in range(nhd)]
        for s, r in _sub_rows(rev):
            ch_ref[s] = c
            for hd in range(nhd):
                nh_ref[s, hd:hd + 1, :] = n[hd]
                mh_ref[s, hd:hd + 1, :] = jnp.broadcast_to(m[hd], (1, LANE))
            gates = [_gate_views(g_ref, b_ref, r, hd, rev) for hd in range(nhd)]
            h, c, n, m = ml_chunk_fwd(q_ref[r, :], k_ref[r, :], v_ref[r, :], gates, c, n, m, rev)
            h_ref[r, :] = h
        c_ref[...] = c
        for hd in range(nhd):
            n_ref[hd:hd + 1, :] = n[hd]
            m_ref[hd:hd + 1, :] = jnp.broadcast_to(m[hd], (1, LANE))

    cidx = lambda j: _chunk_of(j, ncc, nc, rev)
    row = lambda s: pl.BlockSpec((STEP, 1024), lambda j: (cidx(j), s))
    st3 = lambda a, b: pl.BlockSpec((SUB, a, b), lambda j: (cidx(j), 0, 0))
    return pl.pallas_call(
        body, name="mlstm_fwd_rev" if rev else "mlstm_fwd", grid=(nc,),
        in_specs=[row(0), row(1), row(SEG_BV), pl.BlockSpec((STEP, LANE), lambda j: (cidx(j), BLK_GATE)),
                  pl.BlockSpec((1, LANE), lambda j: (0, 0))],
        out_specs=[row(0), st3(1024, ML_D), st3(8, ML_D), st3(8, LANE)],
        out_shape=[jax.ShapeDtypeStruct((tt, 1024), F32), jax.ShapeDtypeStruct((nc * SUB, 1024, ML_D), F32),
                   jax.ShapeDtypeStruct((nc * SUB, 8, ML_D), F32), jax.ShapeDtypeStruct((nc * SUB, 8, LANE), F32)],
        scratch_shapes=[pltpu.VMEM((1024, ML_D), F32), pltpu.VMEM((8, ML_D), F32), pltpu.VMEM((8, LANE), F32)],
    )(cpre, cpre, u, u, bias)


def _mlstm_bwd(cpre, u, bias, chist, nhist, mhist, h_out, dh, ncc, rev, final=None):
    tt = u.shape[0]
    nc, ncc = tt // STEP, ncc // SUB
    nhd = ML_HEADS
    is_final = final is not None
    d = 1 if rev else 0
    col0, width = SEG_BV * 1024, N_U - SEG_BV * 1024

    def body(q_ref, k_ref, v_ref, g_ref, b_ref, ch_ref, nh_ref, mh_ref, ho_ref, dh_ref, *rest):
        if is_final:
            aqk_ref, av_ref, ag_ref, bo_ref = rest[:4]
            dqk_ref, du_ref, gs_ref, dc_ref, dn_ref = rest[5:]
        else:
            dqk_ref, dv_ref, dg_ref, gs_ref, dc_ref, dn_ref = rest

        @pl.when(pl.program_id(0) == 0)
        def _():
            dc_ref[...] = jnp.zeros_like(dc_ref)
            dn_ref[...] = jnp.zeros_like(dn_ref)
            gs_ref[...] = jnp.zeros_like(gs_ref)

        lane = lax.broadcasted_iota(jnp.int32, (1, LANE), 1)
        dc, dn_all, gs = dc_ref[...], dn_ref[...], gs_ref[...]
        dn = [dn_all[hd:hd + 1, :] for hd in range(nhd)]
        for s, r in reversed(_sub_rows(rev)):
            gates = [_gate_views(g_ref, b_ref, r, hd, rev) for hd in range(nhd)]
            n_all, m_all = nh_ref[s], mh_ref[s]
            dqp, dkp, dv, dgi, dgf, dc, dn = ml_chunk_bwd(
                q_ref[r, :], k_ref[r, :], v_ref[r, :], gates, ch_ref[s],
                [n_all[hd:hd + 1, :] for hd in range(nhd)], [m_all[hd:hd + 1, 0:1] for hd in range(nhd)],
                ho_ref[r, :], dh_ref[r, :], dc, dn, rev)
            dg = ag_ref[r, :] if is_final else jnp.zeros((CHUNK, LANE), F32)
            for hd in range(nhd):
                dg = dg + jnp.where(lane == d * ML_HEADS + hd, dgi[hd], 0.0)
                dg = dg + jnp.where(lane == 2 * ML_HEADS + d * ML_HEADS + hd, dgf[hd], 0.0)
            if is_final:
                dqp = dqp + aqk_ref[r, 0:W_B]
                dkp = dkp + aqk_ref[r, W_B:2 * W_B]
                du_ref[r, 0:1024] = (dv + av_ref[r, :]).astype(BF16)
                du_ref[r, 1024:3072] = bo_ref[r, :]
                du_ref[r, 3072:3072 + LANE] = dg.astype(BF16)
            else:
                dv_ref[r, :] = dv
                dg_ref[r, :] = dg
            dqk_ref[r, 0:W_B] = dqp
            dqk_ref[r, W_B:2 * W_B] = dkp
            gs = gs + jnp.sum(dg, axis=0, keepdims=True)
        dc_ref[...] = dc
        gs_ref[...] = gs
        for hd in range(nhd):
            dn_ref[hd:hd + 1, :] = dn[hd]

    cidx = lambda j: _chunk_of(nc - 1 - j, ncc, nc, rev)
    row = lambda s: pl.BlockSpec((STEP, 1024), lambda j: (cidx(j), s))
    wide = pl.BlockSpec((STEP, 2048), lambda j: (cidx(j), 0))
    gate = pl.BlockSpec((STEP, LANE), lambda j: (cidx(j), 0))
    st3 = lambda a, b: pl.BlockSpec((SUB, a, b), lambda j: (cidx(j), 0, 0))
    gs_spec, gs_shape = pl.BlockSpec((1, LANE), lambda j: (0, 0)), jax.ShapeDtypeStruct((1, LANE), F32)
    dqk_shape = jax.ShapeDtypeStruct((tt, 2048), F32)
    ins = [cpre, cpre, u, u, bias, chist, nhist, mhist, h_out, dh] + (list(final) if is_final else [])
    if is_final:
        out_specs = [wide, pl.BlockSpec((pl.Element(STEP), pl.Element(width)), lambda j: (cidx(j) * STEP, col0)), gs_spec]
        out_shape = [dqk_shape, jax.ShapeDtypeStruct((tt, N_U), BF16), gs_shape]
    else:
        out_specs = [wide, row(0), gate, gs_spec]
        out_shape = [dqk_shape, jax.ShapeDtypeStruct((tt, 1024), F32), jax.ShapeDtypeStruct((tt, LANE), F32), gs_shape]
    return pl.pallas_call(
        body, name="mlstm_bwd_rev" if rev else "mlstm_bwd", grid=(nc,),
        in_specs=[row(0), row(1), row(SEG_BV), pl.BlockSpec((STEP, LANE), lambda j: (cidx(j), BLK_GATE)),
                  pl.BlockSpec((1, LANE), lambda j: (0, 0)),
                  st3(1024, ML_D), st3(8, ML_D), st3(8, LANE), row(0), row(0)]
        + ([wide, row(0), gate, wide, _ANY] if is_final else []),
        out_specs=out_specs, out_shape=out_shape,
        input_output_aliases={14: 1} if is_final else {},
        scratch_shapes=[pltpu.VMEM((1024, ML_D), F32), pltpu.VMEM((8, ML_D), F32)],
    )(*ins)


def _whole(body, out_shape, name, *args, nbytes=0):
    return pl.pallas_call(body, name=name, out_shape=out_shape, compiler_params=_vmem(nbytes))(*args)


def _mod_fwd(cs, w_cols, b_cols):
    def body(c_ref, w_ref, b_ref, o_ref):
        o_ref[...] = _exact_nn(_silu(c_ref[...]), w_ref[...]) + b_ref[...]

    return _whole(body, jax.ShapeDtypeStruct((16, w_cols.shape[1]), F32), "mod_fwd", cs, w_cols, b_cols,
                  nbytes=4 * w_cols.size * 4)


def _mod_bwd_w(cs, d9, w_cols):
    def body(c_ref, d_ref, w_ref, gw_ref, pc_ref):
        gw_ref[...] = _exact_tn(_silu(c_ref[...]), d_ref[...])
        pc = lax.dot_general(d_ref[8:16, :], w_ref[...], (((1,), (1,)), ((), ())), precision=lax.Precision.HIGHEST,
                             preferred_element_type=F32)
        row = lax.broadcasted_iota(jnp.int32, pc.shape, 0)
        pc_ref[...] = jnp.where(row == 0, pc, 0.0)

    return _whole(body, [jax.ShapeDtypeStruct(w_cols.shape, F32), jax.ShapeDtypeStruct((8, w_cols.shape[0]), F32)],
                  "mod_bwd_w", cs, d9, w_cols, nbytes=6 * w_cols.size * 4)


def _lower_fwd(lb4):
    def body(l_ref, o_ref):
        o_ref[...] = jnp.zeros_like(o_ref)
        o_ref[0:1, :] = 1.0 / (1.0 + jnp.exp(l_ref[1:2, :] - l_ref[0:1, :]))
        o_ref[1:2, :] = 1.0 / (1.0 + jnp.exp(l_ref[3:4, :] - l_ref[2:3, :]))

    return _whole(body, jax.ShapeDtypeStruct((8, lb4.shape[1]), F32), "lower_fwd", lb4)


def _reduce8(g, name):
    def body(g_ref, o_ref):
        acc = g_ref[0]
        for k in range(1, N_DEV):
            acc = acc + g_ref[k]
        o_ref[...] = acc

    return _whole(body, jax.ShapeDtypeStruct(g.shape[1:], F32), name, g, nbytes=4 * g.size * 4)


_PACK = (("dmodx", 48), ("dmodc", 48), ("gconvw", 144), ("gconvb", 16), ("dlower", 16), ("ghgw", 8), ("gmlw", 8),
         ("glng", 16), ("glnb", 16), ("losssq", 16), ("ggate", 8))


def _pack_offsets():
    off, out = 0, {}
    for name, rows in _PACK:
        out[name] = (off, rows)
        off += rows
    return out


def _small_finish(total, p0, d_feat):
    offs = _pack_offsets()

    def body(t_ref, p_ref, gb_ref, a0_ref, a1_ref, loss_ref):
        ox, oc, ol, oq = offs["dmodx"][0], offs["dmodc"][0], offs["dlower"][0], offs["losssq"][0]
        gb_ref[...] = t_ref[ox:ox + 48, :] + t_ref[oc:oc + 48, :]
        p = p_ref[...]
        da0 = t_ref[ol:ol + 16, :] * p * (1.0 - p)
        a0_ref[...] = da0
        a1_ref[...] = -da0
        sq = t_ref[oq:oq + 16, :]
        tot = jnp.sum(jnp.sum(sq, axis=1, keepdims=True), axis=0, keepdims=True)
        loss_ref[...] = jnp.broadcast_to(tot * (0.5 / d_feat), loss_ref.shape)

    s = jax.ShapeDtypeStruct
    return _whole(body, [s((48, LANE), F32), s((16, LANE), F32), s((16, LANE), F32), s((8, LANE), F32)],
                  "small_finish", total, p0)


def _cctx_grad(parts, c_ctx8):
    def body(p_ref, c_ref, o_ref):
        acc = p_ref[0]
        for k in range(1, N_DEV):
            acc = acc + p_ref[k]
        o_ref[...] = acc * _dsilu(c_ref[...])

    return _whole(body, jax.ShapeDtypeStruct(c_ctx8.shape, F32), "cctx_grad", parts, c_ctx8)


def _adam_math(w, g, m, v):
    m = ADAM_B1 * m + (1.0 - ADAM_B1) * g
    v = ADAM_B2 * v + (1.0 - ADAM_B2) * (g * g)
    m_hat = m / (1.0 - ADAM_B1 ** ADAM_STEP)
    v_hat = v / (1.0 - ADAM_B2 ** ADAM_STEP)
    delta = -ADAM_LR * (m_hat / (jnp.sqrt(v_hat) + ADAM_EPS) + ADAM_WD * w)
    return delta, m, v


def _adamw(w, g, m, v, rows, name):
    r, c = w.shape

    def body(w_ref, g_ref, m_ref, v_ref, d_ref, mo_ref, vo_ref):
        d_ref[...], mo_ref[...], vo_ref[...] = _adam_math(w_ref[...], g_ref[...], m_ref[...], v_ref[...])

    spec = pl.BlockSpec((rows, c), lambda i: (i, 0))
    return pl.pallas_call(
        body, name=name, grid=(r // rows,), in_specs=[spec] * 4, out_specs=[spec] * 3,
        out_shape=[jax.ShapeDtypeStruct((r, c), F32)] * 3,
        compiler_params=_vmem(16 * rows * (c + LANE) * 4),
    )(w, g, m, v)


def _rs_adamw(recv, w, m, v, tile, name, by_cols=False):
    _, r, c = recv.shape

    def body(r_ref, w_ref, m_ref, v_ref, g_ref, d_ref, mo_ref, vo_ref):
        g = r_ref[0].astype(F32)
        for k in range(1, N_DEV):
            g = g + r_ref[k].astype(F32)
        g_ref[...] = g
        d_ref[...], mo_ref[...], vo_ref[...] = _adam_math(w_ref[...], g, m_ref[...], v_ref[...])

    if by_cols:
        spec = pl.BlockSpec((r, tile), lambda i: (0, i))
        rspec = pl.BlockSpec((N_DEV, r, tile), lambda i: (0, 0, i))
        steps, elems = c // tile, (r + 16) * tile
    else:
        spec = pl.BlockSpec((tile, c), lambda i: (i, 0))
        rspec = pl.BlockSpec((N_DEV, tile, c), lambda i: (0, i, 0))
        steps, elems = r // tile, tile * (c + LANE)
    return pl.pallas_call(
        body, name=name, grid=(steps,), in_specs=[rspec] + [spec] * 3, out_specs=[spec] * 4,
        out_shape=[jax.ShapeDtypeStruct((r, c), F32)] * 4,
        compiler_params=_vmem(2 * elems * (N_DEV * 2 + 7 * 4) + (4 << 20)),
    )(recv, w, m, v)


def _all_gather(x, name):
    r, c = x.shape

    def body(x_ref, out_ref, send_sems, recv_sems, local_sem):
        px, py, pc = _position()
        me, sibling = (px, py, pc), (px, py, 1 - pc)
        chips = [(1 - px, py), (px, 1 - py), (1 - px, 1 - py)]

        def slot(qx, qy, qc):
            return out_ref.at[4 * qx + 2 * qy + qc]

        def copy(k, block, to, src=None):
            return pltpu.make_async_remote_copy(
                src_ref=slot(*block) if src is None else src, dst_ref=slot(*block),
                send_sem=send_sems.at[k], recv_sem=recv_sems.at[k], device_id=to, device_id_type=MESH)

        mine = pltpu.make_async_copy(x_ref, slot(*me), local_sem)
        mine.start()
        first = [copy(1 + j, me, (*chip, pc), src=x_ref) for j, chip in enumerate(chips)]
        first.append(copy(0, me, sibling, src=x_ref))
        for cp in first:
            cp.start()
        passed = [copy(4 + j, (*chip, pc), sibling) for j, chip in enumerate(chips)]
        for j, chip in enumerate(chips):
            copy(1 + j, (*chip, pc), me).wait_recv()
            passed[j].start()
        copy(0, sibling, me).wait_recv()
        for j, chip in enumerate(chips):
            copy(4 + j, (*chip, 1 - pc), me).wait_recv()
        for cp in first + passed:
            cp.wait_send()
        mine.wait()

    return pl.pallas_call(
        body, name=name, out_shape=jax.ShapeDtypeStruct((N_DEV, r, c), x.dtype),
        in_specs=[pl.BlockSpec(memory_space=pl.ANY)], out_specs=pl.BlockSpec(memory_space=pl.ANY),
        scratch_shapes=[pltpu.SemaphoreType.DMA((7,)), pltpu.SemaphoreType.DMA((7,)), pltpu.SemaphoreType.DMA],
    )(x)


class _RelayGather:
    scratch = [pltpu.SemaphoreType.DMA((8,)), pltpu.SemaphoreType.DMA((8,)), pltpu.SemaphoreType.DMA]

    def __init__(self, x):
        self.x = x
        self.half = x.shape[1] // 2
        self.out_shape = jax.ShapeDtypeStruct((N_DEV,) + x.shape, x.dtype)

    def _parts(self, x_ref, out_ref, send_sems, recv_sems, local_sem):
        px, py, pc = _position()
        me, sib = (px, py, pc), (px, py, 1 - pc)
        xn, yn, dg = (1 - px, py, pc), (px, 1 - py, pc), (1 - px, 1 - py, pc)
        half = self.half

        def slot(owner, cols=None):
            ref = out_ref.at[4 * owner[0] + 2 * owner[1] + owner[2]]
            return ref if cols is None else ref.at[:, pl.ds(cols, half)]

        def copy(k, owner, to, src=None, cols=None):
            return pltpu.make_async_remote_copy(
                src_ref=slot(owner, cols) if src is None else src, dst_ref=slot(owner, cols),
                send_sem=send_sems.at[k], recv_sem=recv_sems.at[k], device_id=to, device_id_type=MESH)

        mine = pltpu.make_async_copy(x_ref, slot(me), local_sem)
        own = [copy(1, me, xn, src=x_ref), copy(2, me, yn, src=x_ref), copy(0, me, sib, src=x_ref)]
        return me, sib, xn, yn, dg, copy, mine, own

    def start(self, *refs):
        *_, mine, own = self._parts(*refs)
        mine.start()
        for cp in own:
            cp.start()

    def finish(self, *refs):
        me, sib, xn, yn, dg, copy, mine, own = self._parts(*refs)
        flip = lambda q: (q[0], q[1], 1 - q[2])
        copy(1, xn, me).wait_recv()
        relay_x = [copy(3, xn, yn, cols=0), copy(5, xn, sib)]
        for cp in relay_x:
            cp.start()
        copy(2, yn, me).wait_recv()
        relay_y = [copy(4, yn, xn, cols=self.half), copy(6, yn, sib)]
        for cp in relay_y:
            cp.start()
        copy(3, dg, me, cols=0).wait_recv()
        copy(4, dg, me, cols=self.half).wait_recv()
        relay_d = copy(7, dg, sib)
        relay_d.start()
        copy(0, sib, me).wait_recv()
        copy(5, flip(xn), me).wait_recv()
        copy(6, flip(yn), me).wait_recv()
        copy(7, flip(dg), me).wait_recv()
        for cp in own + relay_x + relay_y + [relay_d]:
            cp.wait_send()
        mine.wait()


DW_PIECES = ((0, 256), (256, 640), (896, 1152))


def _local_step(ctx, x, target, modp, lower, wt_u, w_o, w9, conv_b, gate_b, hgw, mlw, ln_g, ln_b, exchange):
    tc = ctx.shape[0]
    tt = tc + x.shape[0]
    nbc, ncc = tc // ROWS, tc // CHUNK
    lower_f, lower_b = lower[0:1], lower[1:2]

    tmh = _pick(tt, (1088, 768, 512, 256))
    if exchange:
        hc, wt = _modulate_fwd(ctx, x, modp, gather=_RelayGather(wt_u))
        wt_u = jnp.pad(wt.reshape(N_IN, D_MODEL), ((0, N_U - N_IN), (0, 0)))
        u, w_o = _mm(hc, wt_u, "nt", F32, tmh, 1152, D_MODEL, "mm_u", ride=_Ride("gather", w_o))
        w_o = w_o.reshape(D_MODEL, D_MODEL)
    else:
        hc = _modulate_fwd(ctx, x, modp)
        u = _mm(hc, wt_u, "nt", F32, tmh, 1152, D_MODEL, "mm_u")
    cpre = _conv_fwd(u, w9, conv_b, tc)
    bias = jnp.pad(gate_b.reshape(1, 16), ((0, 0), (0, LANE - 16)))

    o_f, hist_f = _hgrn_fwd(u, lower_f, ncc, False)
    o_b, hist_b = _hgrn_fwd(u, lower_b, ncc, True)
    h_f, ch_f, nh_f, mh_f = _mlstm_fwd(cpre, u, bias, ncc, False)
    h_b, ch_b, nh_b, mh_b = _mlstm_fwd(cpre, u, bias, ncc, True)
    y = _post_fwd(o_f, o_b, h_f, h_b, u, hgw, mlw, nbc)
    dz, dxa, fsum = _final(y, w_o, x, target, modp, ln_g, ln_b)

    dw_o = _mm(y, dz, "tn", BF16, D_MODEL, 1024, _pick(y.shape[0], (512, 256)), "mm_dwo")
    do, dhm, daz, dbo, psum = _post_bwd(dz, w_o, o_f, o_b, h_f, h_b, u, hgw, mlw, nbc)
    if exchange:
        dzf_f, dzq, dv_a, dlb_f, dw_o = _hgrn_bwd(
            u, lower_f, hist_f, do, ncc, False, ride=_Ride("a2a", dw_o.reshape(N_DEV, D_MODEL // N_DEV, D_MODEL)))
    else:
        dzf_f, dzq, dv_a, dlb_f = _hgrn_bwd(u, lower_f, hist_f, do, ncc, False)
    du, dlb_b = _hgrn_bwd(u, lower_b, hist_b, do, ncc, True, final=(dzq, dv_a, dzf_f, daz))
    dqk, dv_m, dg, _ = _mlstm_bwd(cpre, u, bias, ch_f, nh_f, mh_f, h_f, dhm, ncc, False)
    dqk, du, gsum = _mlstm_bwd(cpre, u, bias, ch_b, nh_b, mh_b, h_b, dhm, ncc, True, final=(dqk, dv_m, dg, dbo, du))
    du, gconvw, gconvb = _conv_bwd(dqk, u, w9, tc, du)
    tkw = _pick(tt, (2176, 768, 512, 256))
    blocks = lambda g: g.reshape(N_DEV, N_IN // N_DEV, g.shape[1])
    dwu = lambda name, cols, ride: _mm(du, hc, "tn", BF16, 1152, cols[1], tkw, name, b_cols=cols, ride=ride, m_out=N_IN)
    dwt_a = dwu("mm_dwu_a", DW_PIECES[0], None)
    if exchange:
        whole = lambda piece, into: _Ride("a2a", blocks(piece[1]), cols=(piece[0][0], D_MODEL), into=into)
        dwt_b, got = dwu("mm_dwu_b", DW_PIECES[1], whole((DW_PIECES[0], dwt_a), None))
        dwt_c, got = dwu("mm_dwu_c", DW_PIECES[2], whole((DW_PIECES[1], dwt_b), got))
        dh, dwt_u = _mm(du, wt_u, "nn", F32, tmh, D_MODEL // 2, 3456, "mm_dh", ride=whole((DW_PIECES[2], dwt_c), got))
    else:
        dwt_u = jnp.concatenate([dwt_a, dwu("mm_dwu_b", DW_PIECES[1], None), dwu("mm_dwu_c", DW_PIECES[2], None)], axis=1)
        dh = _mm(du, wt_u, "nn", F32, tmh, D_MODEL // 2, 3456, "mm_dh")
    gx, msum = _modulate_bwd(dh, ctx, x, modp, dxa)

    zero_row = jnp.zeros((1, D_MODEL), F32)
    small = dict(
        dmodx=jnp.concatenate([msum[2:3], msum[3:4], fsum[0:1]], axis=0),
        dmodc=jnp.concatenate([msum[0:1], msum[1:2], zero_row], axis=0),
        gconvw=gconvw, gconvb=gconvb, dlower=jnp.concatenate([dlb_f, dlb_b], axis=0),
        ghgw=psum[0:1], gmlw=psum[1:2], glng=fsum[1:2], glnb=fsum[2:3], losssq=fsum[3:4],
        ggate=jnp.concatenate([gsum, jnp.zeros((7, LANE), F32)], axis=0))
    return gx, dwt_u, dw_o, small


def _pack_small(small):
    return jnp.concatenate([small[name].reshape(rows, LANE) for name, rows in _PACK], axis=0)


def _flat_pad(a, rows):
    flat = a.reshape(-1)
    return jnp.pad(flat, (0, rows * LANE - flat.shape[0])).reshape(rows, LANE)


def kernel(x, c, ctx, c_ctx, w_mod, b_mod, w_in, conv_w, conv_b, hg_lb, ml_gate_b, hg_norm_w, ml_norm_w, w_out, ln_g, ln_b, loss_target, m_c_ctx, m_w_mod, m_b_mod, m_w_in, m_conv_w, m_conv_b, m_hg_lb, m_ml_gate_b, m_hg_norm_w, m_ml_norm_w, m_w_out, m_ln_g, m_ln_b, v_c_ctx, v_w_mod, v_b_mod, v_w_in, v_conv_w, v_conv_b, v_hg_lb, v_ml_gate_b, v_hg_norm_w, v_ml_norm_w, v_w_out, v_ln_g, v_ln_b):
    px, py, pc = _position()
    me = 4 * px + 2 * py + pc
    d = D_MODEL
    n_mod = w_mod.shape[2]
    n_cv = conv_w.shape[3]
    n_lb = hg_lb.shape[2]

    pack0 = jnp.concatenate([c.reshape(-1), conv_w.reshape(-1), hg_lb.reshape(-1)]).reshape(1, -1)
    g0 = _all_gather(pack0, "gather_small_inputs")[:, 0, :]
    c_all = g0[:, :d]
    w9 = jnp.transpose(g0[:, d:d + 9 * n_cv].reshape(N_DEV, 9, n_cv), (1, 0, 2)).reshape(9, N_DEV * n_cv)
    lb4 = jnp.transpose(g0[:, d + 9 * n_cv:].reshape(N_DEV, 4, n_lb), (1, 0, 2)).reshape(4, N_DEV * n_lb)
    lower = _lower_fwd(lb4)

    cs = jnp.concatenate([c_all, c_ctx.reshape(1, d), jnp.zeros((7, d), F32)], axis=0)
    b_cols = lax.dynamic_slice(b_mod, (0, me * n_mod), (1, n_mod))
    slab = _mod_fwd(cs, w_mod[0], b_cols)
    mod_all = jnp.transpose(_all_gather(slab, "gather_mod"), (1, 0, 2)).reshape(16, N_DEV * n_mod)
    mod_x = lax.dynamic_slice(mod_all, (me, 0), (1, 3 * d)).reshape(3, d)
    modp = jnp.stack([mod_all[8].reshape(3, d), mod_x])

    gx, recv_wi, recv_wo, small = _local_step(ctx[0], x[0], loss_target[0], modp, lower, w_in[0].T.astype(BF16),
                                              w_out[0].astype(BF16), w9, conv_b, ml_gate_b[0], hg_norm_w, ml_norm_w,
                                              ln_g, ln_b, True)
    g_wi, d_wi, nm_wi, nv_wi = [a.T for a in _rs_adamw(recv_wi, w_in[0].T, m_w_in[0].T, v_w_in[0].T, 256,
                                                       "adamw_w_in", by_cols=True)]
    g_wo, d_wo, nm_wo, nv_wo = _rs_adamw(recv_wo, w_out[0], m_w_out[0], v_w_out[0], 64, "adamw_w_out")

    packs = _all_gather(_pack_small(small), "gather_small_grads")
    total = _reduce8(packs, "reduce_small_grads")
    offs = _pack_offsets()
    piece = lambda name: total[offs[name][0]:offs[name][0] + offs[name][1]]
    g_bmod, g_lb0, g_lb1, loss8 = _small_finish(total, lower[0:2].reshape(16, LANE), float(d))

    ox = offs["dmodx"][0]
    dmodx_all = packs[:, ox:ox + 48, :].reshape(N_DEV, 3 * d)
    dmodc_tot = piece("dmodc").reshape(1, 3 * d)
    d9 = jnp.concatenate([dmodx_all, dmodc_tot, jnp.zeros((7, 3 * d), F32)], axis=0)
    d9_cols = lax.dynamic_slice(d9, (0, me * n_mod), (16, n_mod))
    g_wmod, pc_part = _mod_bwd_w(cs, d9_cols, w_mod[0])
    c_ctx8 = jnp.concatenate([c_ctx.reshape(1, d), jnp.zeros((7, d), F32)], axis=0)
    g_cctx = _cctx_grad(_all_gather(pc_part, "gather_cctx"), c_ctx8)[0]
    d_wmod, nm_wmod, nv_wmod = _adamw(w_mod[0], g_wmod, m_w_mod[0], v_w_mod[0], 256, "adamw_w_mod")

    g_convw_full = piece("gconvw").reshape(9, d)
    g_convw = lax.dynamic_slice(g_convw_full, (0, me * n_cv), (9, n_cv)).reshape(conv_w.shape)
    lb_full = jnp.stack([jnp.stack([g_lb0[0:8].reshape(-1), g_lb1[0:8].reshape(-1)]),
                         jnp.stack([g_lb0[8:16].reshape(-1), g_lb1[8:16].reshape(-1)])])
    g_hglb = lax.dynamic_slice(lb_full, (0, 0, me * n_lb), (2, 2, n_lb))
    grads = dict(
        c_ctx=g_cctx, b_mod=g_bmod.reshape(b_mod.shape), conv_w=g_convw, conv_b=piece("gconvb").reshape(conv_b.shape),
        hg_lb=g_hglb, ml_gate_b=piece("ggate")[0, :16].reshape(ml_gate_b.shape),
        hg_norm_w=piece("ghgw").reshape(hg_norm_w.shape), ml_norm_w=piece("gmlw").reshape(ml_norm_w.shape),
        ln_g=piece("glng").reshape(ln_g.shape), ln_b=piece("glnb").reshape(ln_b.shape))
    params = dict(c_ctx=(c_ctx, m_c_ctx, v_c_ctx), b_mod=(b_mod, m_b_mod, v_b_mod), conv_w=(conv_w, m_conv_w, v_conv_w),
                  conv_b=(conv_b, m_conv_b, v_conv_b), hg_lb=(hg_lb, m_hg_lb, v_hg_lb),
                  ml_gate_b=(ml_gate_b, m_ml_gate_b, v_ml_gate_b), hg_norm_w=(hg_norm_w, m_hg_norm_w, v_hg_norm_w),
                  ml_norm_w=(ml_norm_w, m_ml_norm_w, v_ml_norm_w), ln_g=(ln_g, m_ln_g, v_ln_g), ln_b=(ln_b, m_ln_b, v_ln_b))
    names = list(params)
    rows_of = {n: -(-params[n][0].size // LANE) for n in names}
    rows_tot = -(-sum(rows_of.values()) // 8) * 8
    cat = lambda arrs: jnp.concatenate(
        [_flat_pad(a, rows_of[n]) for n, a in zip(names, arrs)]
        + [jnp.ones((rows_tot - sum(rows_of.values()), LANE), F32)], axis=0)
    d_s, m_s, v_s = _adamw(cat([params[n][0] for n in names]), cat([grads[n] for n in names]),
                           cat([params[n][1] for n in names]), cat([params[n][2] for n in names]), rows_tot, "adamw_small")
    delta, new_m, new_v, off = {}, {}, {}, 0
    for n in names:
        shape, size = params[n][0].shape, params[n][0].size
        take = lambda a: a[off:off + rows_of[n]].reshape(-1)[:size].reshape(shape)
        delta[n], new_m[n], new_v[n] = take(d_s), take(m_s), take(v_s)
        off += rows_of[n]
    grads.update(w_mod=g_wmod[None], w_in=g_wi[None], w_out=g_wo[None])
    delta.update(w_mod=d_wmod[None], w_in=d_wi[None], w_out=d_wo[None])
    new_m.update(w_mod=nm_wmod[None], w_in=nm_wi[None], w_out=nm_wo[None])
    new_v.update(w_mod=nv_wmod[None], w_in=nv_wi[None], w_out=nv_wo[None])

    order = ("c_ctx", "w_mod", "b_mod", "w_in", "conv_w", "conv_b", "hg_lb", "ml_gate_b", "hg_norm_w", "ml_norm_w",
             "w_out", "ln_g", "ln_b")
    return (loss8[0, 0], gx[None], *[grads[n] for n in order], *[delta[n] for n in order],
            *[new_m[n] for n in order], *[new_v[n] for n in order])
```

```python
import jax
import jax.numpy as jnp
from jax import lax
from jax.experimental import pallas as pl
from jax.experimental.pallas import tpu as pltpu

F32 = jnp.float32
BF16 = jnp.bfloat16

D_MODEL = 2048
W_A = 1024
W_B = 1024
HG_HEADS = 8
HG_D = 128
ML_HEADS = 4
ML_D = 256
CHUNK = 64
N_IN = 10256
LANE = 128
N_U = 81 * LANE
N_DEV = 8
ALPHA = 2.0 ** 0.25
LN_EPS = 1e-5
NORM_EPS = 1e-6
ADAM_LR, ADAM_B1, ADAM_B2, ADAM_EPS, ADAM_WD, ADAM_STEP = 0.001, 0.9, 0.999, 1e-08, 0.01, 10
VMEM_CAP = 60 * 1024 * 1024

SEG_AQ, SEG_AFF, SEG_AFB, SEG_AI, SEG_AZ = range(5)
BLK_QK = 40
SEG_BV, SEG_BO, SEG_BZ = 7, 8, 9
BLK_GATE = 80

MESH = pl.DeviceIdType.MESH


def _vmem(nbytes):
    return pltpu.CompilerParams(vmem_limit_bytes=int(min(VMEM_CAP, max(nbytes, 16 * 1024 * 1024))))


def _sigmoid(x):
    return 1.0 / (1.0 + jnp.exp(-x))


def _silu(x):
    return x * _sigmoid(x)


def _dsilu(x):
    s = _sigmoid(x)
    return s * (1.0 + x * (1.0 - s))


def _silu_both(x):
    s = _sigmoid(x)
    return x * s, s * (1.0 + x * (1.0 - s))


def _bdot(a, b, dims):
    return lax.dot_general(a.astype(BF16), b.astype(BF16), (dims, ((), ())), preferred_element_type=F32)


def _nn(a, b):
    return _bdot(a, b, ((1,), (0,)))


def _nt(a, b):
    return _bdot(a, b, ((1,), (1,)))


def _tn(a, b):
    return _bdot(a, b, ((0,), (0,)))


def _exact_nn(a, b):
    return lax.dot_general(a, b, (((1,), (0,)), ((), ())), precision=lax.Precision.HIGHEST,
                           preferred_element_type=F32)


def _exact_tn(a, b):
    return lax.dot_general(a, b, (((0,), (0,)), ((), ())), precision=lax.Precision.HIGHEST,
                           preferred_element_type=F32)


def _tri(rev):
    t = lax.broadcasted_iota(jnp.int32, (CHUNK, CHUNK), 0)
    s = lax.broadcasted_iota(jnp.int32, (CHUNK, CHUNK), 1)
    return (s >= t) if rev else (s <= t)


def _eye():
    t = lax.broadcasted_iota(jnp.int32, (CHUNK, CHUNK), 0)
    s = lax.broadcasted_iota(jnp.int32, (CHUNK, CHUNK), 1)
    return (s == t).astype(F32)


def _row_to_col(row):
    return jnp.sum(_eye() * row, axis=1, keepdims=True)


def _last_onehot(rev):
    t = lax.broadcasted_iota(jnp.int32, (CHUNK, 1), 0)
    return (t == (0 if rev else CHUNK - 1)).astype(F32)


def _head_slices(width, n_heads):
    hd = width // n_heads
    return [slice(h * hd, (h + 1) * hd) for h in range(n_heads)]


def _scan_sum(x, rev):
    n = x.shape[0]
    t = lax.broadcasted_iota(jnp.int32, x.shape, 0)
    s = 1
    while s < n:
        if rev:
            x = x + jnp.where(t < n - s, pltpu.roll(x, n - s, 0), 0.0)
        else:
            x = x + jnp.where(t >= s, pltpu.roll(x, s, 0), 0.0)
        s *= 2
    return x


def _dot3(a, b, dims):
    a_hi, b_hi = a.astype(BF16), b.astype(BF16)
    a_lo, b_lo = (a - a_hi.astype(F32)).astype(BF16), (b - b_hi.astype(F32)).astype(BF16)
    dot = lambda x, y: lax.dot_general(x, y, (dims, ((), ())), preferred_element_type=F32)
    return dot(a_hi, b_hi) + (dot(a_hi, b_lo) + dot(a_lo, b_hi))


def _hg_common(zq, zf, lb, rev):
    q, dq_dz = _silu_both(zq)
    sg = _sigmoid(zf)
    f = lb + (1.0 - lb) * sg
    g = jnp.log(f)
    k = 1.0 - f
    b = _scan_sum(g, rev)
    b_last = jnp.sum(g, axis=0, keepdims=True)
    r = b[CHUNK // 2:CHUNK // 2 + 1, :]
    e_up = jnp.exp(b - r)
    e_dn = jnp.exp(r - b)
    e_b = e_up * jnp.exp(r)
    e_lb = e_dn * jnp.exp(b_last - r)
    return dict(q=q, dq_dz=dq_dz, sg=sg, f=f, k=k, e_up=e_up, e_dn=e_dn, e_b=e_b, e_lb=e_lb, e_last=jnp.exp(b_last),
                q_t=q * e_up, k_t=k * e_dn, q_s=q * e_b, k_h=k * e_lb, tri=_tri(rev).astype(F32))


def hg_chunk_fwd(zq, zf, v, lb, st, rev):
    c = _hg_common(zq, zf, lb, rev)
    hs = _head_slices(zq.shape[1], zq.shape[1] // HG_D)
    s = [_nt(c["q_t"][:, sl], c["k_t"][:, sl]) for sl in hs]
    oi = [_nt(c["q_s"][:, sl], st[sl, :]) for sl in hs]
    ds = [_tn(v[:, sl], c["k_h"][:, sl]) for sl in hs]
    oa = [_nn(c["tri"] * s_h, v[:, sl]) for s_h, sl in zip(s, hs)]
    o = jnp.concatenate([x + y for x, y in zip(oi, oa)], axis=1)
    st_new = jnp.concatenate([st[sl, :] * c["e_last"][:, sl] + d for sl, d in zip(hs, ds)], axis=0)
    return o, st_new


def hg_chunk_bwd(zq, zf, v, lb, st, do, dst_new, rev):
    c = _hg_common(zq, zf, lb, rev)
    hs = _head_slices(zq.shape[1], zq.shape[1] // HG_D)
    tri, q_t, k_t, q_s, k_h = c["tri"], c["q_t"], c["k_t"], c["q_s"], c["k_h"]
    s = [_nt(q_t[:, sl], k_t[:, sl]) for sl in hs]
    da = [tri * _nt(do[:, sl], v[:, sl]) for sl in hs]
    dq_s = [_nn(do[:, sl], st[sl, :]) for sl in hs]
    dk_h = [_nn(v[:, sl], dst_new[sl, :]) for sl in hs]
    dv_s = [_nt(k_h[:, sl], dst_new[sl, :]) for sl in hs]
    dst_q = [_tn(do[:, sl], q_s[:, sl]) for sl in hs]
    dq_t = [_dot3(da_h, k_t[:, sl], ((1,), (0,))) for da_h, sl in zip(da, hs)]
    dk_t = [_dot3(da_h, q_t[:, sl], ((0,), (0,))) for da_h, sl in zip(da, hs)]
    dv_a = [_tn(tri * s_h, do[:, sl]) for s_h, sl in zip(s, hs)]
    cat = lambda parts: jnp.concatenate(parts, axis=1)
    dq_s, dk_h, dq_t, dk_t = cat(dq_s), cat(dk_h), cat(dq_t), cat(dk_t)
    dv = cat([x + y for x, y in zip(dv_a, dv_s)])
    dst = jnp.concatenate([dst_new[sl, :] * c["e_last"][:, sl] + d for sl, d in zip(hs, dst_q)], axis=0)
    dq = dq_s * c["e_b"] + dq_t * c["e_up"]
    dk = dk_t * c["e_dn"] + dk_h * c["e_lb"]
    db = c["q"] * dq - c["k"] * dk
    ss = cat([jnp.sum(dst_new[sl, :] * st[sl, :], axis=0, keepdims=True) for sl in hs])
    d_all = jnp.sum(dk_h * k_h, axis=0, keepdims=True) + c["e_last"] * ss
    dg = _scan_sum(db, not rev) + d_all
    dzq = dq * c["dq_dz"]
    df = dg / c["f"] - dk
    dzf = df * (1.0 - lb) * c["sg"] * (1.0 - c["sg"])
    dlb = jnp.sum(df * (1.0 - c["sg"]), axis=0, keepdims=True)
    return dzq, dzf, dv, dlb, dst


def _log_sigmoid(x):
    return jnp.minimum(x, 0.0) - jnp.log(1.0 + jnp.exp(-jnp.abs(x)))


def _each(fn, *lists):
    return [fn(*xs) for xs in zip(*lists)]


def _bf(xs):
    return [x.astype(BF16) for x in xs]


def _ml_forward_parts(qp, kp, v, gates, c, n, m, rev, with_num):
    hs = _head_slices(qp.shape[1], qp.shape[1] // ML_D)
    q_all, dq_dp = _silu_both(qp)
    k_all, dk_dp = _silu_both(kp)
    k_all = k_all * (ML_D ** -0.5)
    q = [q_all[:, sl] for sl in hs]
    k = [k_all[:, sl] for sl in hs]
    vv = [v[:, sl] for sl in hs]
    cc = [c[sl, :] for sl in hs]
    tri_b = _tri(rev)
    tri = tri_b.astype(F32)
    tri_t = _tri(not rev).astype(F32)
    e_last = _last_onehot(rev)
    qb, kb, vb, cb = _bf(q), _bf(k), _bf(vv), _bf(cc)
    qk = _each(_nt, qb, kb)
    parts = []
    for (gi_c, gi_r, gf_c, gf_r), m_h in zip(gates, m):
        lf_c, lf_r = _log_sigmoid(gf_c), _log_sigmoid(gf_r)
        b_c = jnp.sum(tri * lf_r, axis=1, keepdims=True)
        b_r = jnp.sum(tri_t * lf_c, axis=0, keepdims=True)
        log_w = jnp.where(tri_b, b_c - b_r + gi_r, -jnp.inf)
        m_inter = b_c + m_h
        m_t = jnp.maximum(m_inter, jnp.max(log_w, axis=1, keepdims=True))
        m_new = jnp.sum(m_t * e_last, axis=0, keepdims=True)
        b_last = jnp.sum(b_c * e_last, axis=0, keepdims=True)
        parts.append(dict(a=jnp.exp(m_inter - m_t), p=jnp.exp(log_w - m_t), floor=jnp.exp(-m_t), m_new=m_new,
                          ws=jnp.exp(b_last - b_c + gi_c - m_new), decay=jnp.exp(b_last + m_h - m_new), gf_c=gf_c))
    w = [pt["p"] * x for pt, x in zip(parts, qk)]
    wb = _bf(w)
    for pt, q_h, n_h, w_h in zip(parts, q, n, w):
        qn = jnp.sum(q_h * n_h, axis=1, keepdims=True)
        den = pt["a"] * qn + jnp.sum(w_h, axis=1, keepdims=True)
        pt.update(qn=qn, den=den, rinv=1.0 / jnp.maximum(jnp.abs(den), pt["floor"]), w=w_h)
    if with_num:
        qc = _each(_nt, qb, cb)
        wv = _each(_nn, wb, vb)
        for pt, qc_h, wv_h in zip(parts, qc, wv):
            pt.update(num=pt["a"] * qc_h + wv_h)
    return hs, q, k, vv, cc, tri, parts, dict(q=qb, k=kb, v=vb, c=cb, w=wb, dq_dp=dq_dp, dk_dp=dk_dp)


def ml_chunk_fwd(qp, kp, v, gates, c, n, m, rev):
    hs, q, k, vv, cc, tri, parts, bf = _ml_forward_parts(qp, kp, v, gates, c, n, m, rev, True)
    h = jnp.concatenate([pt["num"] * pt["rinv"] for pt in parts], axis=1)
    upd = _each(_tn, [pt["ws"] * v_h for pt, v_h in zip(parts, vv)], bf["k"])
    c_new = jnp.concatenate([pt["decay"] * c_h + u for pt, c_h, u in zip(parts, cc, upd)], axis=0)
    n_new = [pt["decay"] * n_h + jnp.sum(pt["ws"] * k_h, axis=0, keepdims=True) for pt, n_h, k_h in zip(parts, n, k)]
    return h, c_new, n_new, [pt["m_new"] for pt in parts]


def ml_chunk_bwd(qp, kp, v, gates, c, n, m, h_out, dh, dc_new, dn_new, rev):
    hs, q, k, vv, cc, tri, parts, bf = _ml_forward_parts(qp, kp, v, gates, c, n, m, rev, False)
    dcn = [dc_new[sl, :] for sl in hs]
    dcb = _bf(dcn)
    dnum, dden = [], []
    for pt, sl in zip(parts, hs):
        dh_h = dh[:, sl]
        signed_live = jnp.where(jnp.abs(pt["den"]) > pt["floor"], jnp.where(pt["den"] >= 0.0, 1.0, -1.0), 0.0)
        dnum.append(dh_h * pt["rinv"])
        dden.append(-jnp.sum(dh_h * h_out[:, sl], axis=1, keepdims=True) * pt["rinv"] * signed_live)
    dnb = _bf(dnum)
    dw = [x + y for x, y in zip(_each(_nt, dnb, bf["v"]), dden)]
    kdc = _each(_nt, bf["k"], dcb)
    vdc = _each(_nn, bf["v"], dcb)
    dqk = [x * pt["p"] for x, pt in zip(dw, parts)]
    adn = [pt["a"] * x for pt, x in zip(parts, dnum)]
    dqkb, adnb = _bf(dqk), _bf(adn)
    dv_w = _each(_tn, bf["w"], dnb)
    dq_k = _each(_nn, dqkb, bf["k"])
    dq_c = _each(_nn, adnb, bf["c"])
    dk_q = _each(_tn, dqkb, bf["q"])
    dc_q = _each(_tn, adnb, bf["q"])
    dq, dk, dv, dgi, dgf, dc, dn = [], [], [], [], [], [], []
    for i, pt in enumerate(parts):
        a, ws, decay = pt["a"], pt["ws"], pt["decay"]
        add = a * dden[i]
        e = dw[i] * pt["w"]
        dv.append(dv_w[i] + ws * kdc[i])
        dq.append(dq_k[i] + dq_c[i] + add * n[i])
        dk.append(dk_q[i] + ws * vdc[i] + ws * dn_new[i])
        alpha = jnp.sum(q[i] * dq_c[i], axis=1, keepdims=True) + dden[i] * pt["qn"] * a
        omega = (jnp.sum(vdc[i] * k[i], axis=1, keepdims=True) + jnp.sum(k[i] * dn_new[i], axis=1, keepdims=True)) * ws
        delta = decay * (jnp.sum(jnp.sum(dcn[i] * cc[i], axis=1, keepdims=True), axis=0, keepdims=True)
                         + jnp.sum(dn_new[i] * n[i], axis=1, keepdims=True))
        dc.append(decay * dcn[i] + dc_q[i])
        dn.append(decay * dn_new[i] + jnp.sum(add * q[i], axis=0, keepdims=True))
        e_rows = jnp.sum(e, axis=1, keepdims=True)
        e_cols = _row_to_col(jnp.sum(e, axis=0, keepdims=True))
        dgi.append(e_cols + omega)
        db = e_rows + alpha - e_cols - omega
        tail = jnp.sum(omega, axis=0, keepdims=True) + delta
        dlf = _row_to_col(jnp.sum(tri * db, axis=0, keepdims=True)) + tail
        dgf.append(dlf * (1.0 - _sigmoid(pt["gf_c"])))
    cat = lambda xs: jnp.concatenate(xs, axis=1)
    dqp = cat(dq) * bf["dq_dp"]
    dkp = cat(dk) * (ML_D ** -0.5) * bf["dk_dp"]
    return dqp, dkp, cat(dv), dgi, dgf, jnp.concatenate(dc, axis=0), dn


def _pick(n, prefs):
    for p in prefs:
        if n % p == 0:
            return p
    raise ValueError(f"no tile for {n} among {prefs}")


def _position():
    return lax.axis_index("x"), lax.axis_index("y"), lax.axis_index("c")


class _Ride:
    def __init__(self, kind, x, cols=None, into=None):
        self.kind, self.x, self.cols, self.into = kind, x, cols, into
        r, c = x.shape[-2:]
        self.out_shape = jax.ShapeDtypeStruct((N_DEV, r, c if cols is None else cols[1]), x.dtype)
        self.width = c

    def _copies(self, x_ref, out_ref, send_sems, recv_sems, local_sem):
        px, py, pc = _position()
        me = 4 * px + 2 * py + pc
        src = (lambda slot: x_ref) if self.kind == "gather" else (lambda slot: x_ref.at[slot])
        dst = ((lambda slot: out_ref.at[slot]) if self.cols is None
               else (lambda slot: out_ref.at[slot, :, pl.ds(self.cols[0], self.width)]))
        mine = pltpu.make_async_copy(src(me), dst(me), local_sem)
        sends, recvs = [], []
        for k, (fx, fy, fc) in enumerate([(1, 0, 0), (0, 1, 0), (1, 1, 0), (1, 0, 1), (0, 1, 1), (1, 1, 1), (0, 0, 1)]):
            qx, qy, qc = (1 - px if fx else px), (1 - py if fy else py), (1 - pc if fc else pc)
            peer = 4 * qx + 2 * qy + qc
            sends.append(pltpu.make_async_remote_copy(
                src_ref=src(peer), dst_ref=dst(me), send_sem=send_sems.at[k], recv_sem=recv_sems.at[k],
                device_id=(qx, qy, qc), device_id_type=MESH))
            recvs.append(pltpu.make_async_remote_copy(
                src_ref=src(me), dst_ref=dst(peer), send_sem=send_sems.at[k], recv_sem=recv_sems.at[k],
                device_id=(qx, qy, qc), device_id_type=MESH))
        return mine, sends, recvs

    def start(self, *refs):
        mine, sends, _ = self._copies(*refs)
        mine.start()
        for cp in sends:
            cp.start()

    def wait(self, *refs):
        mine, sends, recvs = self._copies(*refs)
        for cp in recvs:
            cp.wait_recv()
        for cp in sends:
            cp.wait_send()
        mine.wait()

    def operands(self):
        return [self.x] + ([self.into] if self.into is not None else [])


_RIDE_SCRATCH = [pltpu.SemaphoreType.DMA((7,)), pltpu.SemaphoreType.DMA((7,)), pltpu.SemaphoreType.DMA]
_ANY = pl.BlockSpec(memory_space=pl.ANY)


def _mm(a, b, mode, out_dtype, tm, tn, tk, name, ride=None, b_cols=None, m_out=None):
    if mode == "nn":
        (m, k), (k2, n) = a.shape, b.shape
    elif mode == "nt":
        (m, k), (n, k2) = a.shape, b.shape
    else:
        (k, m), (k2, n) = a.shape, b.shape
    off, n = (0, n) if b_cols is None else b_cols
    assert k == k2 and m % tm == 0 and n % tn == 0 and k % tk == 0, (a.shape, b.shape, mode, tm, tn, tk)
    assert b_cols is None or (mode != "nt" and off % LANE == 0)
    nk = k // tk
    dims = {"nn": ((1,), (0,)), "nt": ((1,), (1,)), "tn": ((0,), (0,))}[mode]
    a_spec = (pl.BlockSpec((tk, tm), lambda j, i, kk: (kk, i)) if mode == "tn"
              else pl.BlockSpec((tm, tk), lambda j, i, kk: (i, kk)))
    if b_cols is not None:
        b_spec = pl.BlockSpec((pl.Element(tk), pl.Element(tn)),
                              lambda j, i, kk: (pl.multiple_of(kk * tk, LANE), pl.multiple_of(off + j * tn, LANE)))
    elif mode == "nt":
        b_spec = pl.BlockSpec((tn, tk), lambda j, i, kk: (j, kk))
    else:
        b_spec = pl.BlockSpec((tk, tn), lambda j, i, kk: (kk, j))

    grid = (n // tn, m // tm, nk)
    n_ride_in = len(ride.operands()) if ride is not None else 0

    def body(a_ref, b_ref, *rest):
        if ride is not None:
            x_ref = rest[0]
            o_ref, got_ref, acc_ref = rest[n_ride_in:n_ride_in + 3]
            comm = (x_ref, got_ref) + tuple(rest[n_ride_in + 3:])
        else:
            o_ref, acc_ref = rest
        kk = pl.program_id(2)
        step = (pl.program_id(0) * grid[1] + pl.program_id(1)) * nk + kk
        if ride is not None:
            @pl.when(step == 0)
            def _():
                ride.start(*comm)

        part = lax.dot_general(a_ref[...], b_ref[...], (dims, ((), ())), preferred_element_type=F32)
        if nk == 1:
            o_ref[...] = part.astype(o_ref.dtype)
        else:
            @pl.when(kk == 0)
            def _():
                acc_ref[...] = part

            @pl.when(jnp.logical_and(kk > 0, kk < nk - 1))
            def _():
                acc_ref[...] += part

            @pl.when(kk == nk - 1)
            def _():
                o_ref[...] = (acc_ref[...] + part).astype(o_ref.dtype)

        if ride is not None:
            @pl.when(step == grid[0] * grid[1] * nk - 1)
            def _():
                ride.wait(*comm)

    osz = jnp.dtype(out_dtype).itemsize
    need = 2 * (tm * tk * a.dtype.itemsize + tk * tn * b.dtype.itemsize + tm * tn * osz) + tm * tn * 4
    o_spec = pl.BlockSpec((tm, tn), lambda j, i, kk: (i, j))
    o_shape = jax.ShapeDtypeStruct((m if m_out is None else m_out, n), out_dtype)
    extra = ride is not None
    return pl.pallas_call(
        body, name=name, grid=grid,
        in_specs=[a_spec, b_spec] + [_ANY] * n_ride_in,
        out_specs=[o_spec, _ANY] if extra else o_spec,
        out_shape=[o_shape, ride.out_shape] if extra else o_shape,
        scratch_shapes=[pltpu.VMEM((tm, tn) if nk > 1 else (8, LANE), F32)] + (_RIDE_SCRATCH if extra else []),
        input_output_aliases={3: 1} if extra and ride.into is not None else {},
        compiler_params=_vmem(need + (12 << 20)),
    )(a, b, *(ride.operands() if extra else []))


ROWS = 256


def _ln_stats(x):
    mu = jnp.mean(x, axis=-1, keepdims=True)
    xc = x - mu
    var = jnp.mean(xc * xc, axis=-1, keepdims=True)
    rstd = lax.rsqrt(var + LN_EPS)
    return xc * rstd, rstd


def _token_specs(nbc, nbx, d):
    return [pl.BlockSpec((ROWS, d), lambda i: (jnp.minimum(i, nbc - 1), 0)),
            pl.BlockSpec((ROWS, d), lambda i: (jnp.maximum(i - nbc, 0), 0))]


def _tokens(c_ref, x_ref, nbc):
    return jnp.where(pl.program_id(0) < nbc, c_ref[...], x_ref[...])


def _modulate_fwd(ctx, x, modp, gather=None):
    d = x.shape[1]
    nbc, nbx = ctx.shape[0] // ROWS, x.shape[0] // ROWS
    riding = gather is not None

    def body(c_ref, x_ref, mod_ref, *rest):
        if riding:
            comm = (rest[0], rest[2]) + tuple(rest[3:])
            o_ref = rest[1]

            @pl.when(pl.program_id(0) == 0)
            def _():
                gather.start(*comm)
        else:
            o_ref = rest[0]
        n, _ = _ln_stats(_tokens(c_ref, x_ref, nbc))
        o_ref[...] = (n * (1.0 + mod_ref[0, 1:2, :]) + mod_ref[0, 0:1, :]).astype(BF16)
        if riding:
            @pl.when(pl.program_id(0) == nbc + nbx - 1)
            def _():
                gather.finish(*comm)

    o_spec = pl.BlockSpec((ROWS, d), lambda i: (i, 0))
    o_shape = jax.ShapeDtypeStruct((ctx.shape[0] + x.shape[0], d), BF16)
    return pl.pallas_call(
        body, name="modulate_fwd", grid=(nbc + nbx,),
        in_specs=_token_specs(nbc, nbx, d) + [pl.BlockSpec((1, 3, d), lambda i: (jnp.where(i >= nbc, 1, 0), 0, 0))]
        + ([_ANY] if riding else []),
        out_specs=[o_spec, _ANY] if riding else o_spec,
        out_shape=[o_shape, gather.out_shape] if riding else o_shape,
        scratch_shapes=gather.scratch if riding else [],
    )(ctx, x, modp, *([gather.x] if riding else []))


def _modulate_bwd(dh, ctx, x, modp, dxa):
    t, d = x.shape
    nbc, nbx = ctx.shape[0] // ROWS, t // ROWS

    def body(dh_ref, c_ref, x_ref, mod_ref, dxa_ref, gx_ref, sum_ref):
        i = pl.program_id(0)
        n, rstd = _ln_stats(_tokens(c_ref, x_ref, nbc))
        g = dh_ref[...]
        dn = g * (1.0 + mod_ref[0, 1:2, :])
        dx = rstd * (dn - jnp.mean(dn, axis=-1, keepdims=True) - n * jnp.mean(dn * n, axis=-1, keepdims=True))
        gx_ref[...] = dx + dxa_ref[...]
        dshift = jnp.sum(g, axis=0, keepdims=True)
        dscale = jnp.sum(g * n, axis=0, keepdims=True)

        @pl.when(i == 0)
        def _():
            sum_ref[...] = jnp.zeros_like(sum_ref)

        @pl.when(i < nbc)
        def _():
            sum_ref[0:1, :] += dshift
            sum_ref[1:2, :] += dscale

        @pl.when(i >= nbc)
        def _():
            sum_ref[2:3, :] += dshift
            sum_ref[3:4, :] += dscale

    lat = lambda i: (jnp.maximum(i - nbc, 0), 0)
    return pl.pallas_call(
        body, name="modulate_bwd", grid=(nbc + nbx,),
        in_specs=[pl.BlockSpec((ROWS, d), lambda i: (i, 0))] + _token_specs(nbc, nbx, d)
        + [pl.BlockSpec((1, 3, d), lambda i: (jnp.where(i >= nbc, 1, 0), 0, 0)), pl.BlockSpec((ROWS, d), lat)],
        out_specs=[pl.BlockSpec((ROWS, d), lat), pl.BlockSpec((8, d), lambda i: (0, 0))],
        out_shape=[jax.ShapeDtypeStruct((t, d), F32), jax.ShapeDtypeStruct((8, d), F32)],
    )(dh, ctx, x, modp, dxa)


def _post_fwd(o_f, o_b, h_f, h_b, u, hgw, mlw, nbc):
    tt = u.shape[0]
    t = tt - nbc * ROWS

    def body(of_ref, ob_ref, hf_ref, hb_ref, az_ref, bo_ref, bz_ref, hgw_ref, mlw_ref, y_ref):
        o = of_ref[...] + ob_ref[...]
        for sl in _head_slices(W_A, HG_HEADS):
            oh = o[:, sl]
            rs = lax.rsqrt(jnp.mean(oh * oh, axis=-1, keepdims=True) + NORM_EPS)
            y_ref[:, sl] = (oh * rs * hgw_ref[:, sl] * _silu(az_ref[:, sl])).astype(BF16)
        hm = hf_ref[...] + hb_ref[...]
        for sl in _head_slices(W_B, ML_HEADS):
            hh = hm[:, sl]
            mu = jnp.mean(hh, axis=-1, keepdims=True)
            hc = hh - mu
            rstd = lax.rsqrt(jnp.mean(hc * hc, axis=-1, keepdims=True) + NORM_EPS)
            out = hc * rstd * mlw_ref[:, sl] * _sigmoid(bo_ref[:, sl]) * _silu(bz_ref[:, sl])
            y_ref[:, W_A + sl.start:W_A + sl.stop] = out.astype(BF16)

    row = lambda i: (i + nbc, 0)
    seg = lambda s: pl.BlockSpec((ROWS, 1024), lambda i: (i + nbc, s))
    wspec = pl.BlockSpec((1, 1024), lambda i: (0, 0))
    return pl.pallas_call(
        body, name="post_fwd", grid=(t // ROWS,),
        in_specs=[pl.BlockSpec((ROWS, 1024), row)] * 4 + [seg(SEG_AZ), seg(SEG_BO), seg(SEG_BZ), wspec, wspec],
        out_specs=pl.BlockSpec((ROWS, 2048), lambda i: (i, 0)),
        out_shape=jax.ShapeDtypeStruct((t, 2048), BF16),
    )(o_f, o_b, h_f, h_b, u, u, u, hgw, mlw)


def _post_bwd(dz, w_o, o_f, o_b, h_f, h_b, u, hgw, mlw, nbc):
    tt = u.shape[0]
    d = w_o.shape[0]

    def body(dz_ref, w_ref, of_ref, ob_ref, hf_ref, hb_ref, az_ref, bo_ref, bz_ref, hgw_ref, mlw_ref,
             do_ref, dhm_ref, daz_ref, dbo_ref, sum_ref):
        i = pl.program_id(0)
        live = jnp.where(i >= nbc, 1.0, 0.0)
        dy = lax.dot_general(dz_ref[...], w_ref[...], (((1,), (1,)), ((), ())), preferred_element_type=F32) * live

        @pl.when(i == 0)
        def _():
            sum_ref[...] = jnp.zeros_like(sum_ref)

        o = of_ref[...] + ob_ref[...]
        for sl in _head_slices(W_A, HG_HEADS):
            oh = o[:, sl]
            rs = lax.rsqrt(jnp.mean(oh * oh, axis=-1, keepdims=True) + NORM_EPS)
            on = oh * rs
            az = az_ref[:, sl]
            dya = dy[:, sl]
            saz, daz = _silu_both(az)
            doa = dya * saz
            daz_ref[:, sl] = (dya * on * hgw_ref[:, sl] * daz).astype(BF16)
            sum_ref[0:1, sl] += jnp.sum(doa * on, axis=0, keepdims=True)
            don = doa * hgw_ref[:, sl]
            do_ref[:, sl] = rs * (don - on * jnp.mean(don * on, axis=-1, keepdims=True))
        hm = hf_ref[...] + hb_ref[...]
        for sl in _head_slices(W_B, ML_HEADS):
            hh = hm[:, sl]
            mu = jnp.mean(hh, axis=-1, keepdims=True)
            hc = hh - mu
            rstd = lax.rsqrt(jnp.mean(hc * hc, axis=-1, keepdims=True) + NORM_EPS)
            hn = hc * rstd
            hw = hn * mlw_ref[:, sl]
            bo, bz = bo_ref[:, sl], bz_ref[:, sl]
            sbo = _sigmoid(bo)
            sbz, dbz = _silu_both(bz)
            dyb = dy[:, W_A + sl.start:W_A + sl.stop]
            dhw = dyb * sbo * sbz
            dbo_ref[:, sl] = (dyb * hw * sbz * sbo * (1.0 - sbo)).astype(BF16)
            dbo_ref[:, 1024 + sl.start:1024 + sl.stop] = (dyb * hw * sbo * dbz).astype(BF16)
            sum_ref[1:2, sl] += jnp.sum(dhw * hn, axis=0, keepdims=True)
            dhn = dhw * mlw_ref[:, sl]
            dhm_ref[:, sl] = rstd * (dhn - jnp.mean(dhn, axis=-1, keepdims=True)
                                     - hn * jnp.mean(dhn * hn, axis=-1, keepdims=True))

    row = lambda i: (i, 0)
    seg = lambda s: pl.BlockSpec((ROWS, 1024), lambda i: (i, s))
    wspec = pl.BlockSpec((1, 1024), lambda i: (0, 0))
    return pl.pallas_call(
        body, name="post_bwd", grid=(tt // ROWS,),
        in_specs=[pl.BlockSpec((ROWS, 2048), lambda i: (jnp.maximum(i - nbc, 0), 0)), pl.BlockSpec((d, d), lambda i: (0, 0))]
        + [pl.BlockSpec((ROWS, 1024), row)] * 4 + [seg(SEG_AZ), seg(SEG_BO), seg(SEG_BZ), wspec, wspec],
        out_specs=[pl.BlockSpec((ROWS, 1024), row), pl.BlockSpec((ROWS, 1024), row),
                   pl.BlockSpec((ROWS, 1024), row), pl.BlockSpec((ROWS, 2048), row),
                   pl.BlockSpec((8, 1024), lambda i: (0, 0))],
        out_shape=[jax.ShapeDtypeStruct((tt, 1024), F32), jax.ShapeDtypeStruct((tt, 1024), F32),
                   jax.ShapeDtypeStruct((tt, 1024), BF16), jax.ShapeDtypeStruct((tt, 2048), BF16),
                   jax.ShapeDtypeStruct((8, 1024), F32)],
        compiler_params=_vmem(4 * d * d + 30 * ROWS * 2048 * 4),
    )(dz, w_o, o_f, o_b, h_f, h_b, u, u, u, hgw, mlw)


def _final(y, w_o, x, target, modp, ln_g, ln_b):
    t, d = x.shape

    def body(y_ref, w_ref, x_ref, tg_ref, mod_ref, g_ref, b_ref, dz_ref, dxa_ref, sum_ref):
        i = pl.program_id(0)
        zz = lax.dot_general(y_ref[...], w_ref[...], (((1,), (0,)), ((), ())), preferred_element_type=F32)
        gate = mod_ref[0, 2:3, :]
        pre = ALPHA * x_ref[...] + gate * zz
        nh, rstd = _ln_stats(pre)
        err = nh * g_ref[...] + b_ref[...] - tg_ref[...]
        dxo = err * (1.0 / d)
        dnh = dxo * g_ref[...]
        dpre = rstd * (dnh - jnp.mean(dnh, axis=-1, keepdims=True) - nh * jnp.mean(dnh * nh, axis=-1, keepdims=True))
        dz_ref[...] = (gate * dpre).astype(BF16)
        dxa_ref[...] = ALPHA * dpre

        @pl.when(i == 0)
        def _():
            sum_ref[...] = jnp.zeros_like(sum_ref)

        sum_ref[0:1, :] += jnp.sum(dpre * zz, axis=0, keepdims=True)
        sum_ref[1:2, :] += jnp.sum(dxo * nh, axis=0, keepdims=True)
        sum_ref[2:3, :] += jnp.sum(dxo, axis=0, keepdims=True)
        sum_ref[3:4, :] += jnp.sum(err * err, axis=0, keepdims=True)

    row = lambda i: (i, 0)
    vec = pl.BlockSpec((1, d), lambda i: (0, 0))
    return pl.pallas_call(
        body, name="final_ln_loss", grid=(t // ROWS,),
        in_specs=[pl.BlockSpec((ROWS, d), row), pl.BlockSpec((d, d), lambda i: (0, 0)), pl.BlockSpec((ROWS, d), row),
                  pl.BlockSpec((ROWS, d), row), pl.BlockSpec((1, 3, d), lambda i: (1, 0, 0)), vec, vec],
        out_specs=[pl.BlockSpec((ROWS, d), row), pl.BlockSpec((ROWS, d), row), pl.BlockSpec((8, d), lambda i: (0, 0))],
        out_shape=[jax.ShapeDtypeStruct((t, d), BF16), jax.ShapeDtypeStruct((t, d), F32),
                   jax.ShapeDtypeStruct((8, d), F32)],
        compiler_params=_vmem(4 * d * d + 24 * ROWS * d * 4),
    )(y, w_o, x, target, modp, ln_g, ln_b)


GRID_W = 64


def _shift(x, s, ok):
    n = x.shape[0]
    return jnp.where(ok, pltpu.roll(x, s % n, 0), 0.0)


def _grid_masks(n):
    t = lax.broadcasted_iota(jnp.int32, (n, LANE), 0)
    col = t & (GRID_W - 1)
    return dict(left=col >= 1, right=col <= GRID_W - 2, up=t >= GRID_W, down=t < n - GRID_W)


def _seq_masks(n):
    t = lax.broadcasted_iota(jnp.int32, (n, LANE), 0)
    return dict(left=t >= 1, right=t <= n - 2)


def _conv_fwd(u, w9, cb, tc):
    tt = u.shape[0]
    t = tt - tc

    def body(u_ref, w_ref, b_ref, o_ref):
        w = [w_ref[r:r + 1, :] for r in range(9)]
        xc = u_ref[0:tc, :]
        ms = _seq_masks(tc)
        o_ref[0:tc, :] = (w[3] * _shift(xc, 1, ms["left"]) + w[4] * xc + w[5] * _shift(xc, -1, ms["right"])
                          + b_ref[...])
        x = u_ref[tc:tt, :]
        mg = _grid_masks(t)
        taps = (_shift(x, 1, mg["left"]), x, _shift(x, -1, mg["right"]))
        rows = [w[3 * i] * taps[0] + w[3 * i + 1] * taps[1] + w[3 * i + 2] * taps[2] for i in range(3)]
        o_ref[tc:tt, :] = (rows[1] + _shift(rows[0], GRID_W, mg["up"]) + _shift(rows[2], -GRID_W, mg["down"])
                           + b_ref[...])

    return pl.pallas_call(
        body, name="conv_fwd", grid=(2048 // LANE,),
        in_specs=[pl.BlockSpec((tt, LANE), lambda j: (0, BLK_QK + j)), pl.BlockSpec((9, LANE), lambda j: (0, j)),
                  pl.BlockSpec((1, LANE), lambda j: (0, j))],
        out_specs=pl.BlockSpec((tt, LANE), lambda j: (0, j)),
        out_shape=jax.ShapeDtypeStruct((tt, 2048), F32),
        compiler_params=_vmem(40 * tt * LANE * 4),
    )(u, w9, cb)


def _conv_bwd(dcp, u, w9, tc, du):
    tt = u.shape[0]
    t = tt - tc

    def body(d_ref, u_ref, w_ref, du_in_ref, du_ref, gw_ref, gb_ref):
        w = [w_ref[r:r + 1, :] for r in range(9)]
        csum = lambda a: jnp.sum(a, axis=0, keepdims=True)
        dc = d_ref[0:tc, :]
        xc = u_ref[0:tc, :]
        ms = _seq_masks(tc)
        du_ref[0:tc, :] = (w[3] * _shift(dc, -1, ms["right"]) + w[4] * dc + w[5] * _shift(dc, 1, ms["left"])).astype(BF16)
        gmid = [csum(dc * _shift(xc, 1, ms["left"])), csum(dc * xc), csum(dc * _shift(xc, -1, ms["right"]))]
        d = d_ref[tc:tt, :]
        x = u_ref[tc:tt, :]
        mg = _grid_masks(t)
        dtaps = (_shift(d, -1, mg["right"]), d, _shift(d, 1, mg["left"]))
        rows = [w[3 * i] * dtaps[0] + w[3 * i + 1] * dtaps[1] + w[3 * i + 2] * dtaps[2] for i in range(3)]
        du_ref[tc:tt, :] = (rows[1] + _shift(rows[0], -GRID_W, mg["down"]) + _shift(rows[2], GRID_W, mg["up"])).astype(BF16)
        xtaps = (_shift(x, 1, mg["left"]), x, _shift(x, -1, mg["right"]))
        for j in range(3):
            gw_ref[j:j + 1, :] = csum(d * _shift(xtaps[j], GRID_W, mg["up"]))
            gw_ref[3 + j:4 + j, :] = csum(d * xtaps[j]) + gmid[j]
            gw_ref[6 + j:7 + j, :] = csum(d * _shift(xtaps[j], -GRID_W, mg["down"]))
        gb_ref[...] = csum(d) + csum(dc)

    return pl.pallas_call(
        body, name="conv_bwd", grid=(2048 // LANE,),
        in_specs=[pl.BlockSpec((tt, LANE), lambda j: (0, j)), pl.BlockSpec((tt, LANE), lambda j: (0, BLK_QK + j)),
                  pl.BlockSpec((9, LANE), lambda j: (0, j)), _ANY],
        out_specs=[pl.BlockSpec((tt, LANE), lambda j: (0, BLK_QK + j)), pl.BlockSpec((9, LANE), lambda j: (0, j)),
                   pl.BlockSpec((1, LANE), lambda j: (0, j))],
        out_shape=[jax.ShapeDtypeStruct(du.shape, BF16), jax.ShapeDtypeStruct((9, 2048), F32),
                   jax.ShapeDtypeStruct((1, 2048), F32)],
        input_output_aliases={3: 0},
        compiler_params=_vmem(48 * tt * LANE * 4),
    )(dcp, u, w9, du)


SUB = 4
STEP = SUB * CHUNK
ML_SUB = 2
ML_STEP = ML_SUB * CHUNK


def _chunk_of(pos, ncc, nc, rev):
    if not rev:
        return pos
    return jnp.where(pos < ncc, ncc - 1 - pos, nc - 1 - (pos - ncc))


def _sub_rows(rev, sub=SUB):
    order = range(sub - 1, -1, -1) if rev else range(sub)
    return [(s, slice(s * CHUNK, (s + 1) * CHUNK)) for s in order]


def _hgrn_fwd(u, lower_d, ncc, rev):
    tt = u.shape[0]
    nc, ncc = tt // STEP, ncc // SUB
    seg_f = SEG_AFB if rev else SEG_AFF

    def body(zq_ref, zf_ref, v_ref, lb_ref, o_ref, hist_ref, st_ref):
        @pl.when(pl.program_id(0) == 0)
        def _():
            st_ref[...] = jnp.zeros_like(st_ref)

        st = st_ref[...]
        for s, r in _sub_rows(rev):
            hist_ref[s] = st
            o, st = hg_chunk_fwd(zq_ref[r, :], zf_ref[r, :], v_ref[r, :], lb_ref[...], st, rev)
            o_ref[r, :] = o
        st_ref[...] = st

    seg = lambda s: pl.BlockSpec((STEP, 1024), lambda j: (_chunk_of(j, ncc, nc, rev), s))
    return pl.pallas_call(
        body, name="hgrn_fwd_rev" if rev else "hgrn_fwd", grid=(nc,),
        in_specs=[seg(SEG_AQ), seg(seg_f), seg(SEG_AI), pl.BlockSpec((1, 1024), lambda j: (0, 0))],
        out_specs=[pl.BlockSpec((STEP, 1024), lambda j: (_chunk_of(j, ncc, nc, rev), 0)),
                   pl.BlockSpec((SUB, 1024, HG_D), lambda j: (_chunk_of(j, ncc, nc, rev), 0, 0))],
        out_shape=[jax.ShapeDtypeStruct((tt, 1024), F32), jax.ShapeDtypeStruct((nc * SUB, 1024, HG_D), F32)],
        scratch_shapes=[pltpu.VMEM((1024, HG_D), F32)],
    )(u, u, u, lower_d)


def _hgrn_bwd(u, lower_d, hist, do, ncc, rev, ride=None, final=None):
    tt = u.shape[0]
    nc, ncc = tt // STEP, ncc // SUB
    seg_f = SEG_AFB if rev else SEG_AFF
    is_final = final is not None
    has_a2a = ride is not None
    n_out = 2 if is_final else 4
    width = 5 * 1024

    def body(zq_ref, zf_ref, v_ref, lb_ref, hist_ref, do_ref, *rest):
        if is_final:
            aq_ref, av_ref, af_ref, az_ref = rest[:4]
            rest = rest[4:]
        if has_a2a:
            x_ref, rest = rest[0], rest[1:]
        outs, rest = rest[:n_out], rest[n_out:]
        dlb_ref = outs[-1]
        if has_a2a:
            comm = (x_ref, rest[0]) + tuple(rest[2:])
            dst_ref = rest[1]
        else:
            dst_ref = rest[0]

        @pl.when(pl.program_id(0) == 0)
        def _():
            dst_ref[...] = jnp.zeros_like(dst_ref)
            dlb_ref[...] = jnp.zeros_like(dlb_ref)
            if has_a2a:
                ride.start(*comm)

        dst = dst_ref[...]
        dlb_sum = dlb_ref[...]
        for s, r in reversed(_sub_rows(rev)):
            dzq, dzf, dv, dlb, dst = hg_chunk_bwd(zq_ref[r, :], zf_ref[r, :], v_ref[r, :], lb_ref[...],
                                                  hist_ref[s], do_ref[r, :], dst, rev)
            dlb_sum = dlb_sum + dlb
            if is_final:
                du_ref = outs[0]
                dzf_own, dzf_other = dzf.astype(BF16), af_ref[r, :]
                du_ref[r, 0:1024] = (dzq + aq_ref[r, :]).astype(BF16)
                du_ref[r, 1024:2048] = dzf_other if rev else dzf_own
                du_ref[r, 2048:3072] = dzf_own if rev else dzf_other
                du_ref[r, 3072:4096] = (dv + av_ref[r, :]).astype(BF16)
                du_ref[r, 4096:5120] = az_ref[r, :]
            else:
                dzf_ref, dzq_ref, dv_ref = outs[:3]
                dzf_ref[r, :] = dzf.astype(BF16)
                dzq_ref[r, :] = dzq
                dv_ref[r, :] = dv
        dst_ref[...] = dst
        dlb_ref[...] = dlb_sum

        if has_a2a:
            @pl.when(pl.program_id(0) == nc - 1)
            def _():
                ride.wait(*comm)

    cidx = lambda j: _chunk_of(nc - 1 - j, ncc, nc, rev)
    seg = lambda s: pl.BlockSpec((STEP, 1024), lambda j: (cidx(j), s))
    row = pl.BlockSpec((STEP, 1024), lambda j: (cidx(j), 0))
    dlb_spec = pl.BlockSpec((1, 1024), lambda j: (0, 0))
    dlb_shape = jax.ShapeDtypeStruct((1, 1024), F32)
    if is_final:
        out_specs = [pl.BlockSpec((STEP, width), lambda j: (cidx(j), 0)), dlb_spec]
        out_shape = [jax.ShapeDtypeStruct((tt, N_U), BF16), dlb_shape]
    else:
        out_specs = [row, row, row, dlb_spec]
        out_shape = [jax.ShapeDtypeStruct((tt, 1024), BF16), jax.ShapeDtypeStruct((tt, 1024), F32),
                     jax.ShapeDtypeStruct((tt, 1024), F32), dlb_shape]
    ins = [u, u, u, lower_d, hist, do] + (list(final) if is_final else []) + ([ride.x] if has_a2a else [])
    return pl.pallas_call(
        body, name="hgrn_bwd_rev" if rev else "hgrn_bwd", grid=(nc,),
        in_specs=[seg(SEG_AQ), seg(seg_f), seg(SEG_AI), pl.BlockSpec((1, 1024), lambda j: (0, 0)),
                  pl.BlockSpec((SUB, 1024, HG_D), lambda j: (cidx(j), 0, 0)), row] + ([row] * 4 if is_final else [])
        + ([_ANY] if has_a2a else []),
        out_specs=out_specs + ([_ANY] if has_a2a else []),
        out_shape=out_shape + ([ride.out_shape] if has_a2a else []),
        scratch_shapes=[pltpu.VMEM((1024, HG_D), F32)] + (_RIDE_SCRATCH if has_a2a else []),
    )(*ins)


def _gate_views(g_ref, b_ref, r, head, rev):
    gc = g_ref[r, :] + b_ref[...]
    lane = lax.broadcasted_iota(jnp.int32, (1, LANE), 1)
    eye = _eye()
    d = 1 if rev else 0
    ii, fi = d * ML_HEADS + head, 2 * ML_HEADS + d * ML_HEADS + head
    col = lambda idx: jnp.sum(jnp.where(lane == idx, gc, 0.0), axis=1, keepdims=True)
    row = lambda c: jnp.sum(eye * c, axis=0, keepdims=True)
    gi, gf = col(ii), col(fi)
    return gi, row(gi), gf, row(gf)


def _mlstm_fwd(cpre, u, bias, ncc, rev):
    tt = u.shape[0]
    nc, ncc = tt // ML_STEP, ncc // ML_SUB
    nhd = ML_HEADS

    def body(q_ref, k_ref, v_ref, g_ref, b_ref, h_ref, ch_ref, nh_ref, mh_ref, c_ref, n_ref, m_ref):
        @pl.when(pl.program_id(0) == 0)
        def _():
            c_ref[...] = jnp.zeros_like(c_ref)
            n_ref[...] = jnp.zeros_like(n_ref)
            m_ref[...] = jnp.zeros_like(m_ref)

        c, n_all, m_all = c_ref[...], n_ref[...], m_ref[...]
        n = [n_all[hd:hd + 1, :] for hd in range(nhd)]
        m = [m_all[hd:hd + 1, 0:1] for hd in range(nhd)]
        for s, r in _sub_rows(rev, ML_SUB):
            ch_ref[s] = c
            for hd in range(nhd):
                nh_ref[s, hd:hd + 1, :] = n[hd]
                mh_ref[s, hd:hd + 1, :] = jnp.broadcast_to(m[hd], (1, LANE))
            gates = [_gate_views(g_ref, b_ref, r, hd, rev) for hd in range(nhd)]
            h, c, n, m = ml_chunk_fwd(q_ref[r, :], k_ref[r, :], v_ref[r, :], gates, c, n, m, rev)
            h_ref[r, :] = h
        c_ref[...] = c
        for hd in range(nhd):
            n_ref[hd:hd + 1, :] = n[hd]
            m_ref[hd:hd + 1, :] = jnp.broadcast_to(m[hd], (1, LANE))

    cidx = lambda j: _chunk_of(j, ncc, nc, rev)
    row = lambda s: pl.BlockSpec((ML_STEP, 1024), lambda j: (cidx(j), s))
    st3 = lambda a, b: pl.BlockSpec((ML_SUB, a, b), lambda j: (cidx(j), 0, 0))
    return pl.pallas_call(
        body, name="mlstm_fwd_rev" if rev else "mlstm_fwd", grid=(nc,),
        in_specs=[row(0), row(1), row(SEG_BV), pl.BlockSpec((ML_STEP, LANE), lambda j: (cidx(j), BLK_GATE)),
                  pl.BlockSpec((1, LANE), lambda j: (0, 0))],
        out_specs=[row(0), st3(1024, ML_D), st3(8, ML_D), st3(8, LANE)],
        out_shape=[jax.ShapeDtypeStruct((tt, 1024), F32), jax.ShapeDtypeStruct((nc * ML_SUB, 1024, ML_D), F32),
                   jax.ShapeDtypeStruct((nc * ML_SUB, 8, ML_D), F32), jax.ShapeDtypeStruct((nc * ML_SUB, 8, LANE), F32)],
        scratch_shapes=[pltpu.VMEM((1024, ML_D), F32), pltpu.VMEM((8, ML_D), F32), pltpu.VMEM((8, LANE), F32)],
    )(cpre, cpre, u, u, bias)


def _mlstm_bwd(cpre, u, bias, chist, nhist, mhist, h_out, dh, ncc, rev, final=None):
    tt = u.shape[0]
    nc, ncc = tt // ML_STEP, ncc // ML_SUB
    nhd = ML_HEADS
    is_final = final is not None
    d = 1 if rev else 0
    col0, width = SEG_BV * 1024, N_U - SEG_BV * 1024

    def body(q_ref, k_ref, v_ref, g_ref, b_ref, ch_ref, nh_ref, mh_ref, ho_ref, dh_ref, *rest):
        if is_final:
            aqk_ref, av_ref, ag_ref, bo_ref = rest[:4]
            dqk_ref, du_ref, gs_ref, dc_ref, dn_ref = rest[5:]
        else:
            dqk_ref, dv_ref, dg_ref, gs_ref, dc_ref, dn_ref = rest

        @pl.when(pl.program_id(0) == 0)
        def _():
            dc_ref[...] = jnp.zeros_like(dc_ref)
            dn_ref[...] = jnp.zeros_like(dn_ref)
            gs_ref[...] = jnp.zeros_like(gs_ref)

        lane = lax.broadcasted_iota(jnp.int32, (1, LANE), 1)
        dc, dn_all, gs = dc_ref[...], dn_ref[...], gs_ref[...]
        dn = [dn_all[hd:hd + 1, :] for hd in range(nhd)]
        for s, r in reversed(_sub_rows(rev, ML_SUB)):
            gates = [_gate_views(g_ref, b_ref, r, hd, rev) for hd in range(nhd)]
            n_all, m_all = nh_ref[s], mh_ref[s]
            dqp, dkp, dv, dgi, dgf, dc, dn = ml_chunk_bwd(
                q_ref[r, :], k_ref[r, :], v_ref[r, :], gates, ch_ref[s],
                [n_all[hd:hd + 1, :] for hd in range(nhd)], [m_all[hd:hd + 1, 0:1] for hd in range(nhd)],
                ho_ref[r, :], dh_ref[r, :], dc, dn, rev)
            dg = ag_ref[r, :] if is_final else jnp.zeros((CHUNK, LANE), F32)
            for hd in range(nhd):
                dg = dg + jnp.where(lane == d * ML_HEADS + hd, dgi[hd], 0.0)
                dg = dg + jnp.where(lane == 2 * ML_HEADS + d * ML_HEADS + hd, dgf[hd], 0.0)
            if is_final:
                dqp = dqp + aqk_ref[r, 0:W_B]
                dkp = dkp + aqk_ref[r, W_B:2 * W_B]
                du_ref[r, 0:1024] = (dv + av_ref[r, :]).astype(BF16)
                du_ref[r, 1024:3072] = bo_ref[r, :]
                du_ref[r, 3072:3072 + LANE] = dg.astype(BF16)
            else:
                dv_ref[r, :] = dv
                dg_ref[r, :] = dg
            dqk_ref[r, 0:W_B] = dqp
            dqk_ref[r, W_B:2 * W_B] = dkp
            gs = gs + jnp.sum(dg, axis=0, keepdims=True)
        dc_ref[...] = dc
        gs_ref[...] = gs
        for hd in range(nhd):
            dn_ref[hd:hd + 1, :] = dn[hd]

    cidx = lambda j: _chunk_of(nc - 1 - j, ncc, nc, rev)
    row = lambda s: pl.BlockSpec((ML_STEP, 1024), lambda j: (cidx(j), s))
    wide = pl.BlockSpec((ML_STEP, 2048), lambda j: (cidx(j), 0))
    gate = pl.BlockSpec((ML_STEP, LANE), lambda j: (cidx(j), 0))
    st3 = lambda a, b: pl.BlockSpec((ML_SUB, a, b), lambda j: (cidx(j), 0, 0))
    gs_spec, gs_shape = pl.BlockSpec((1, LANE), lambda j: (0, 0)), jax.ShapeDtypeStruct((1, LANE), F32)
    dqk_shape = jax.ShapeDtypeStruct((tt, 2048), F32)
    ins = [cpre, cpre, u, u, bias, chist, nhist, mhist, h_out, dh] + (list(final) if is_final else [])
    if is_final:
        out_specs = [wide, pl.BlockSpec((pl.Element(ML_STEP), pl.Element(width)), lambda j: (cidx(j) * ML_STEP, col0)), gs_spec]
        out_shape = [dqk_shape, jax.ShapeDtypeStruct((tt, N_U), BF16), gs_shape]
    else:
        out_specs = [wide, row(0), gate, gs_spec]
        out_shape = [dqk_shape, jax.ShapeDtypeStruct((tt, 1024), F32), jax.ShapeDtypeStruct((tt, LANE), F32), gs_shape]
    return pl.pallas_call(
        body, name="mlstm_bwd_rev" if rev else "mlstm_bwd", grid=(nc,),
        in_specs=[row(0), row(1), row(SEG_BV), pl.BlockSpec((ML_STEP, LANE), lambda j: (cidx(j), BLK_GATE)),
                  pl.BlockSpec((1, LANE), lambda j: (0, 0)),
                  st3(1024, ML_D), st3(8, ML_D), st3(8, LANE), row(0), row(0)]
        + ([wide, row(0), gate, wide, _ANY] if is_final else []),
        out_specs=out_specs, out_shape=out_shape,
        input_output_aliases={14: 1} if is_final else {},
        scratch_shapes=[pltpu.VMEM((1024, ML_D), F32), pltpu.VMEM((8, ML_D), F32)],
    )(*ins)


def _whole(body, out_shape, name, *args, nbytes=0):
    return pl.pallas_call(body, name=name, out_shape=out_shape, compiler_params=_vmem(nbytes))(*args)


def _mod_fwd(cs, w_cols, b_cols):
    def body(c_ref, w_ref, b_ref, o_ref):
        o_ref[...] = _exact_nn(_silu(c_ref[...]), w_ref[...]) + b_ref[...]

    return _whole(body, jax.ShapeDtypeStruct((16, w_cols.shape[1]), F32), "mod_fwd", cs, w_cols, b_cols,
                  nbytes=4 * w_cols.size * 4)


def _mod_bwd_w(cs, d9, w_cols):
    def body(c_ref, d_ref, w_ref, gw_ref, pc_ref):
        gw_ref[...] = _exact_tn(_silu(c_ref[...]), d_ref[...])
        pc = lax.dot_general(d_ref[8:16, :], w_ref[...], (((1,), (1,)), ((), ())), precision=lax.Precision.HIGHEST,
                             preferred_element_type=F32)
        row = lax.broadcasted_iota(jnp.int32, pc.shape, 0)
        pc_ref[...] = jnp.where(row == 0, pc, 0.0)

    return _whole(body, [jax.ShapeDtypeStruct(w_cols.shape, F32), jax.ShapeDtypeStruct((8, w_cols.shape[0]), F32)],
                  "mod_bwd_w", cs, d9, w_cols, nbytes=6 * w_cols.size * 4)


def _lower_fwd(lb4):
    def body(l_ref, o_ref):
        o_ref[...] = jnp.zeros_like(o_ref)
        o_ref[0:1, :] = 1.0 / (1.0 + jnp.exp(l_ref[1:2, :] - l_ref[0:1, :]))
        o_ref[1:2, :] = 1.0 / (1.0 + jnp.exp(l_ref[3:4, :] - l_ref[2:3, :]))

    return _whole(body, jax.ShapeDtypeStruct((8, lb4.shape[1]), F32), "lower_fwd", lb4)


def _reduce8(g, name):
    def body(g_ref, o_ref):
        acc = g_ref[0]
        for k in range(1, N_DEV):
            acc = acc + g_ref[k]
        o_ref[...] = acc

    return _whole(body, jax.ShapeDtypeStruct(g.shape[1:], F32), name, g, nbytes=4 * g.size * 4)


_PACK = (("dmodx", 48), ("dmodc", 48), ("gconvw", 144), ("gconvb", 16), ("dlower", 16), ("ghgw", 8), ("gmlw", 8),
         ("glng", 16), ("glnb", 16), ("losssq", 16), ("ggate", 8))


def _pack_offsets():
    off, out = 0, {}
    for name, rows in _PACK:
        out[name] = (off, rows)
        off += rows
    return out


def _small_finish(total, p0, d_feat):
    offs = _pack_offsets()

    def body(t_ref, p_ref, gb_ref, a0_ref, a1_ref, loss_ref):
        ox, oc, ol, oq = offs["dmodx"][0], offs["dmodc"][0], offs["dlower"][0], offs["losssq"][0]
        gb_ref[...] = t_ref[ox:ox + 48, :] + t_ref[oc:oc + 48, :]
        p = p_ref[...]
        da0 = t_ref[ol:ol + 16, :] * p * (1.0 - p)
        a0_ref[...] = da0
        a1_ref[...] = -da0
        sq = t_ref[oq:oq + 16, :]
        tot = jnp.sum(jnp.sum(sq, axis=1, keepdims=True), axis=0, keepdims=True)
        loss_ref[...] = jnp.broadcast_to(tot * (0.5 / d_feat), loss_ref.shape)

    s = jax.ShapeDtypeStruct
    return _whole(body, [s((48, LANE), F32), s((16, LANE), F32), s((16, LANE), F32), s((8, LANE), F32)],
                  "small_finish", total, p0)


def _cctx_grad(parts, c_ctx8):
    def body(p_ref, c_ref, o_ref):
        acc = p_ref[0]
        for k in range(1, N_DEV):
            acc = acc + p_ref[k]
        o_ref[...] = acc * _dsilu(c_ref[...])

    return _whole(body, jax.ShapeDtypeStruct(c_ctx8.shape, F32), "cctx_grad", parts, c_ctx8)


def _adam_math(w, g, m, v):
    m = ADAM_B1 * m + (1.0 - ADAM_B1) * g
    v = ADAM_B2 * v + (1.0 - ADAM_B2) * (g * g)
    m_hat = m / (1.0 - ADAM_B1 ** ADAM_STEP)
    v_hat = v / (1.0 - ADAM_B2 ** ADAM_STEP)
    delta = -ADAM_LR * (m_hat / (jnp.sqrt(v_hat) + ADAM_EPS) + ADAM_WD * w)
    return delta, m, v


def _adamw(w, g, m, v, rows, name):
    r, c = w.shape

    def body(w_ref, g_ref, m_ref, v_ref, d_ref, mo_ref, vo_ref):
        d_ref[...], mo_ref[...], vo_ref[...] = _adam_math(w_ref[...], g_ref[...], m_ref[...], v_ref[...])

    spec = pl.BlockSpec((rows, c), lambda i: (i, 0))
    return pl.pallas_call(
        body, name=name, grid=(r // rows,), in_specs=[spec] * 4, out_specs=[spec] * 3,
        out_shape=[jax.ShapeDtypeStruct((r, c), F32)] * 3,
        compiler_params=_vmem(16 * rows * (c + LANE) * 4),
    )(w, g, m, v)


def _rs_adamw(recv, w, m, v, tile, name, by_cols=False):
    _, r, c = recv.shape

    def body(r_ref, w_ref, m_ref, v_ref, g_ref, d_ref, mo_ref, vo_ref):
        g = r_ref[0].astype(F32)
        for k in range(1, N_DEV):
            g = g + r_ref[k].astype(F32)
        g_ref[...] = g
        d_ref[...], mo_ref[...], vo_ref[...] = _adam_math(w_ref[...], g, m_ref[...], v_ref[...])

    if by_cols:
        spec = pl.BlockSpec((r, tile), lambda i: (0, i))
        rspec = pl.BlockSpec((N_DEV, r, tile), lambda i: (0, 0, i))
        steps, elems = c // tile, (r + 16) * tile
    else:
        spec = pl.BlockSpec((tile, c), lambda i: (i, 0))
        rspec = pl.BlockSpec((N_DEV, tile, c), lambda i: (0, i, 0))
        steps, elems = r // tile, tile * (c + LANE)
    return pl.pallas_call(
        body, name=name, grid=(steps,), in_specs=[rspec] + [spec] * 3, out_specs=[spec] * 4,
        out_shape=[jax.ShapeDtypeStruct((r, c), F32)] * 4,
        compiler_params=_vmem(2 * elems * (N_DEV * 2 + 7 * 4) + (4 << 20)),
    )(recv, w, m, v)


def _all_gather(x, name):
    r, c = x.shape

    def body(x_ref, out_ref, send_sems, recv_sems, local_sem):
        px, py, pc = _position()
        me, sibling = (px, py, pc), (px, py, 1 - pc)
        chips = [(1 - px, py), (px, 1 - py), (1 - px, 1 - py)]

        def slot(qx, qy, qc):
            return out_ref.at[4 * qx + 2 * qy + qc]

        def copy(k, block, to, src=None):
            return pltpu.make_async_remote_copy(
                src_ref=slot(*block) if src is None else src, dst_ref=slot(*block),
                send_sem=send_sems.at[k], recv_sem=recv_sems.at[k], device_id=to, device_id_type=MESH)

        mine = pltpu.make_async_copy(x_ref, slot(*me), local_sem)
        mine.start()
        first = [copy(1 + j, me, (*chip, pc), src=x_ref) for j, chip in enumerate(chips)]
        first.append(copy(0, me, sibling, src=x_ref))
        for cp in first:
            cp.start()
        passed = [copy(4 + j, (*chip, pc), sibling) for j, chip in enumerate(chips)]
        for j, chip in enumerate(chips):
            copy(1 + j, (*chip, pc), me).wait_recv()
            passed[j].start()
        copy(0, sibling, me).wait_recv()
        for j, chip in enumerate(chips):
            copy(4 + j, (*chip, 1 - pc), me).wait_recv()
        for cp in first + passed:
            cp.wait_send()
        mine.wait()

    return pl.pallas_call(
        body, name=name, out_shape=jax.ShapeDtypeStruct((N_DEV, r, c), x.dtype),
        in_specs=[pl.BlockSpec(memory_space=pl.ANY)], out_specs=pl.BlockSpec(memory_space=pl.ANY),
        scratch_shapes=[pltpu.SemaphoreType.DMA((7,)), pltpu.SemaphoreType.DMA((7,)), pltpu.SemaphoreType.DMA],
    )(x)


class _RelayGather:
    scratch = [pltpu.SemaphoreType.DMA((8,)), pltpu.SemaphoreType.DMA((8,)), pltpu.SemaphoreType.DMA]

    def __init__(self, x):
        self.x = x
        self.half = x.shape[1] // 2
        self.out_shape = jax.ShapeDtypeStruct((N_DEV,) + x.shape, x.dtype)

    def _parts(self, x_ref, out_ref, send_sems, recv_sems, local_sem):
        px, py, pc = _position()
        me, sib = (px, py, pc), (px, py, 1 - pc)
        xn, yn, dg = (1 - px, py, pc), (px, 1 - py, pc), (1 - px, 1 - py, pc)
        half = self.half

        def slot(owner, cols=None):
            ref = out_ref.at[4 * owner[0] + 2 * owner[1] + owner[2]]
            return ref if cols is None else ref.at[:, pl.ds(cols, half)]

        def copy(k, owner, to, src=None, cols=None):
            return pltpu.make_async_remote_copy(
                src_ref=slot(owner, cols) if src is None else src, dst_ref=slot(owner, cols),
                send_sem=send_sems.at[k], recv_sem=recv_sems.at[k], device_id=to, device_id_type=MESH)

        mine = pltpu.make_async_copy(x_ref, slot(me), local_sem)
        own = [copy(1, me, xn, src=x_ref), copy(2, me, yn, src=x_ref), copy(0, me, sib, src=x_ref)]
        return me, sib, xn, yn, dg, copy, mine, own

    def start(self, *refs):
        *_, mine, own = self._parts(*refs)
        mine.start()
        for cp in own:
            cp.start()

    def finish(self, *refs):
        me, sib, xn, yn, dg, copy, mine, own = self._parts(*refs)
        flip = lambda q: (q[0], q[1], 1 - q[2])
        copy(1, xn, me).wait_recv()
        relay_x = [copy(3, xn, yn, cols=0), copy(5, xn, sib)]
        for cp in relay_x:
            cp.start()
        copy(2, yn, me).wait_recv()
        relay_y = [copy(4, yn, xn, cols=self.half), copy(6, yn, sib)]
        for cp in relay_y:
            cp.start()
        copy(3, dg, me, cols=0).wait_recv()
        copy(4, dg, me, cols=self.half).wait_recv()
        relay_d = copy(7, dg, sib)
        relay_d.start()
        copy(0, sib, me).wait_recv()
        copy(5, flip(xn), me).wait_recv()
        copy(6, flip(yn), me).wait_recv()
        copy(7, flip(dg), me).wait_recv()
        for cp in own + relay_x + relay_y + [relay_d]:
            cp.wait_send()
        mine.wait()


DW_PIECES = ((0, 256), (256, 640), (896, 1152))


def _local_step(ctx, x, target, modp, lower, wt_u, w_o, w9, conv_b, gate_b, hgw, mlw, ln_g, ln_b, exchange):
    tc = ctx.shape[0]
    tt = tc + x.shape[0]
    nbc, ncc = tc // ROWS, tc // CHUNK
    lower_f, lower_b = lower[0:1], lower[1:2]

    tmh = _pick(tt, (1088, 768, 512, 256))
    if exchange:
        hc, wt = _modulate_fwd(ctx, x, modp, gather=_RelayGather(wt_u))
        wt_u = jnp.pad(wt.reshape(N_IN, D_MODEL), ((0, N_U - N_IN), (0, 0)))
        u, w_o = _mm(hc, wt_u, "nt", F32, tmh, 1152, D_MODEL, "mm_u", ride=_Ride("gather", w_o))
        w_o = w_o.reshape(D_MODEL, D_MODEL)
    else:
        hc = _modulate_fwd(ctx, x, modp)
        u = _mm(hc, wt_u, "nt", F32, tmh, 1152, D_MODEL, "mm_u")
    cpre = _conv_fwd(u, w9, conv_b, tc)
    bias = jnp.pad(gate_b.reshape(1, 16), ((0, 0), (0, LANE - 16)))

    o_f, hist_f = _hgrn_fwd(u, lower_f, ncc, False)
    o_b, hist_b = _hgrn_fwd(u, lower_b, ncc, True)
    h_f, ch_f, nh_f, mh_f = _mlstm_fwd(cpre, u, bias, ncc, False)
    h_b, ch_b, nh_b, mh_b = _mlstm_fwd(cpre, u, bias, ncc, True)
    y = _post_fwd(o_f, o_b, h_f, h_b, u, hgw, mlw, nbc)
    dz, dxa, fsum = _final(y, w_o, x, target, modp, ln_g, ln_b)

    dw_o = _mm(y, dz, "tn", BF16, D_MODEL, 1024, _pick(y.shape[0], (1024, 512, 256)), "mm_dwo")
    do, dhm, daz, dbo, psum = _post_bwd(dz, w_o, o_f, o_b, h_f, h_b, u, hgw, mlw, nbc)
    if exchange:
        dzf_f, dzq, dv_a, dlb_f, dw_o = _hgrn_bwd(
            u, lower_f, hist_f, do, ncc, False, ride=_Ride("a2a", dw_o.reshape(N_DEV, D_MODEL // N_DEV, D_MODEL)))
    else:
        dzf_f, dzq, dv_a, dlb_f = _hgrn_bwd(u, lower_f, hist_f, do, ncc, False)
    du, dlb_b = _hgrn_bwd(u, lower_b, hist_b, do, ncc, True, final=(dzq, dv_a, dzf_f, daz))
    dqk, dv_m, dg, _ = _mlstm_bwd(cpre, u, bias, ch_f, nh_f, mh_f, h_f, dhm, ncc, False)
    dqk, du, gsum = _mlstm_bwd(cpre, u, bias, ch_b, nh_b, mh_b, h_b, dhm, ncc, True, final=(dqk, dv_m, dg, dbo, du))
    du, gconvw, gconvb = _conv_bwd(dqk, u, w9, tc, du)
    tkw = _pick(tt, (2176, 768, 512, 256))
    blocks = lambda g: g.reshape(N_DEV, N_IN // N_DEV, g.shape[1])
    dwu = lambda name, cols, ride: _mm(du, hc, "tn", BF16, 1152, cols[1], tkw, name, b_cols=cols, ride=ride, m_out=N_IN)
    dwt_a = dwu("mm_dwu_a", DW_PIECES[0], None)
    if exchange:
        whole = lambda piece, into: _Ride("a2a", blocks(piece[1]), cols=(piece[0][0], D_MODEL), into=into)
        dwt_b, got = dwu("mm_dwu_b", DW_PIECES[1], whole((DW_PIECES[0], dwt_a), None))
        dwt_c, got = dwu("mm_dwu_c", DW_PIECES[2], whole((DW_PIECES[1], dwt_b), got))
        dh, dwt_u = _mm(du, wt_u, "nn", F32, tmh, D_MODEL // 2, 3456, "mm_dh", ride=whole((DW_PIECES[2], dwt_c), got))
    else:
        dwt_u = jnp.concatenate([dwt_a, dwu("mm_dwu_b", DW_PIECES[1], None), dwu("mm_dwu_c", DW_PIECES[2], None)], axis=1)
        dh = _mm(du, wt_u, "nn", F32, tmh, D_MODEL // 2, 3456, "mm_dh")
    gx, msum = _modulate_bwd(dh, ctx, x, modp, dxa)

    zero_row = jnp.zeros((1, D_MODEL), F32)
    small = dict(
        dmodx=jnp.concatenate([msum[2:3], msum[3:4], fsum[0:1]], axis=0),
        dmodc=jnp.concatenate([msum[0:1], msum[1:2], zero_row], axis=0),
        gconvw=gconvw, gconvb=gconvb, dlower=jnp.concatenate([dlb_f, dlb_b], axis=0),
        ghgw=psum[0:1], gmlw=psum[1:2], glng=fsum[1:2], glnb=fsum[2:3], losssq=fsum[3:4],
        ggate=jnp.concatenate([gsum, jnp.zeros((7, LANE), F32)], axis=0))
    return gx, dwt_u, dw_o, small


def _pack_small(small):
    return jnp.concatenate([small[name].reshape(rows, LANE) for name, rows in _PACK], axis=0)


def _flat_pad(a, rows):
    flat = a.reshape(-1)
    return jnp.pad(flat, (0, rows * LANE - flat.shape[0])).reshape(rows, LANE)


def kernel(x, c, ctx, c_ctx, w_mod, b_mod, w_in, conv_w, conv_b, hg_lb, ml_gate_b, hg_norm_w, ml_norm_w, w_out, ln_g, ln_b, loss_target, m_c_ctx, m_w_mod, m_b_mod, m_w_in, m_conv_w, m_conv_b, m_hg_lb, m_ml_gate_b, m_hg_norm_w, m_ml_norm_w, m_w_out, m_ln_g, m_ln_b, v_c_ctx, v_w_mod, v_b_mod, v_w_in, v_conv_w, v_conv_b, v_hg_lb, v_ml_gate_b, v_hg_norm_w, v_ml_norm_w, v_w_out, v_ln_g, v_ln_b):
    px, py, pc = _position()
    me = 4 * px + 2 * py + pc
    d = D_MODEL
    n_mod = w_mod.shape[2]
    n_cv = conv_w.shape[3]
    n_lb = hg_lb.shape[2]

    pack0 = jnp.concatenate([c.reshape(-1), conv_w.reshape(-1), hg_lb.reshape(-1)]).reshape(1, -1)
    g0 = _all_gather(pack0, "gather_small_inputs")[:, 0, :]
    c_all = g0[:, :d]
    w9 = jnp.transpose(g0[:, d:d + 9 * n_cv].reshape(N_DEV, 9, n_cv), (1, 0, 2)).reshape(9, N_DEV * n_cv)
    lb4 = jnp.transpose(g0[:, d + 9 * n_cv:].reshape(N_DEV, 4, n_lb), (1, 0, 2)).reshape(4, N_DEV * n_lb)
    lower = _lower_fwd(lb4)

    cs = jnp.concatenate([c_all, c_ctx.reshape(1, d), jnp.zeros((7, d), F32)], axis=0)
    b_cols = lax.dynamic_slice(b_mod, (0, me * n_mod), (1, n_mod))
    slab = _mod_fwd(cs, w_mod[0], b_cols)
    mod_all = jnp.transpose(_all_gather(slab, "gather_mod"), (1, 0, 2)).reshape(16, N_DEV * n_mod)
    mod_x = lax.dynamic_slice(mod_all, (me, 0), (1, 3 * d)).reshape(3, d)
    modp = jnp.stack([mod_all[8].reshape(3, d), mod_x])

    gx, recv_wi, recv_wo, small = _local_step(ctx[0], x[0], loss_target[0], modp, lower, w_in[0].T.astype(BF16),
                                              w_out[0].astype(BF16), w9, conv_b, ml_gate_b[0], hg_norm_w, ml_norm_w,
                                              ln_g, ln_b, True)
    g_wi, d_wi, nm_wi, nv_wi = [a.T for a in _rs_adamw(recv_wi, w_in[0].T, m_w_in[0].T, v_w_in[0].T, 256,
                                                       "adamw_w_in", by_cols=True)]
    g_wo, d_wo, nm_wo, nv_wo = _rs_adamw(recv_wo, w_out[0], m_w_out[0], v_w_out[0], 64, "adamw_w_out")

    packs = _all_gather(_pack_small(small), "gather_small_grads")
    total = _reduce8(packs, "reduce_small_grads")
    offs = _pack_offsets()
    piece = lambda name: total[offs[name][0]:offs[name][0] + offs[name][1]]
    g_bmod, g_lb0, g_lb1, loss8 = _small_finish(total, lower[0:2].reshape(16, LANE), float(d))

    ox = offs["dmodx"][0]
    dmodx_all = packs[:, ox:ox + 48, :].reshape(N_DEV, 3 * d)
    dmodc_tot = piece("dmodc").reshape(1, 3 * d)
    d9 = jnp.concatenate([dmodx_all, dmodc_tot, jnp.zeros((7, 3 * d), F32)], axis=0)
    d9_cols = lax.dynamic_slice(d9, (0, me * n_mod), (16, n_mod))
    g_wmod, pc_part = _mod_bwd_w(cs, d9_cols, w_mod[0])
    c_ctx8 = jnp.concatenate([c_ctx.reshape(1, d), jnp.zeros((7, d), F32)], axis=0)
    g_cctx = _cctx_grad(_all_gather(pc_part, "gather_cctx"), c_ctx8)[0]
    d_wmod, nm_wmod, nv_wmod = _adamw(w_mod[0], g_wmod, m_w_mod[0], v_w_mod[0], 256, "adamw_w_mod")

    g_convw_full = piece("gconvw").reshape(9, d)
    g_convw = lax.dynamic_slice(g_convw_full, (0, me * n_cv), (9, n_cv)).reshape(conv_w.shape)
    lb_full = jnp.stack([jnp.stack([g_lb0[0:8].reshape(-1), g_lb1[0:8].reshape(-1)]),
                         jnp.stack([g_lb0[8:16].reshape(-1), g_lb1[8:16].reshape(-1)])])
    g_hglb = lax.dynamic_slice(lb_full, (0, 0, me * n_lb), (2, 2, n_lb))
    grads = dict(
        c_ctx=g_cctx, b_mod=g_bmod.reshape(b_mod.shape), conv_w=g_convw, conv_b=piece("gconvb").reshape(conv_b.shape),
        hg_lb=g_hglb, ml_gate_b=piece("ggate")[0, :16].reshape(ml_gate_b.shape),
        hg_norm_w=piece("ghgw").reshape(hg_norm_w.shape), ml_norm_w=piece("gmlw").reshape(ml_norm_w.shape),
        ln_g=piece("glng").reshape(ln_g.shape), ln_b=piece("glnb").reshape(ln_b.shape))
    params = dict(c_ctx=(c_ctx, m_c_ctx, v_c_ctx), b_mod=(b_mod, m_b_mod, v_b_mod), conv_w=(conv_w, m_conv_w, v_conv_w),
                  conv_b=(conv_b, m_conv_b, v_conv_b), hg_lb=(hg_lb, m_hg_lb, v_hg_lb),
                  ml_gate_b=(ml_gate_b, m_ml_gate_b, v_ml_gate_b), hg_norm_w=(hg_norm_w, m_hg_norm_w, v_hg_norm_w),
                  ml_norm_w=(ml_norm_w, m_ml_norm_w, v_ml_norm_w), ln_g=(ln_g, m_ln_g, v_ln_g), ln_b=(ln_b, m_ln_b, v_ln_b))
    names = list(params)
    rows_of = {n: -(-params[n][0].size // LANE) for n in names}
    rows_tot = -(-sum(rows_of.values()) // 8) * 8
    cat = lambda arrs: jnp.concatenate(
        [_flat_pad(a, rows_of[n]) for n, a in zip(names, arrs)]
        + [jnp.ones((rows_tot - sum(rows_of.values()), LANE), F32)], axis=0)
    d_s, m_s, v_s = _adamw(cat([params[n][0] for n in names]), cat([grads[n] for n in names]),
                           cat([params[n][1] for n in names]), cat([params[n][2] for n in names]), rows_tot, "adamw_small")
    delta, new_m, new_v, off = {}, {}, {}, 0
    for n in names:
        shape, size = params[n][0].shape, params[n][0].size
        take = lambda a: a[off:off + rows_of[n]].reshape(-1)[:size].reshape(shape)
        delta[n], new_m[n], new_v[n] = take(d_s), take(m_s), take(v_s)
        off += rows_of[n]
    grads.update(w_mod=g_wmod[None], w_in=g_wi[None], w_out=g_wo[None])
    delta.update(w_mod=d_wmod[None], w_in=d_wi[None], w_out=d_wo[None])
    new_m.update(w_mod=nm_wmod[None], w_in=nm_wi[None], w_out=nm_wo[None])
    new_v.update(w_mod=nv_wmod[None], w_in=nv_wi[None], w_out=nv_wo[None])

    order = ("c_ctx", "w_mod", "b_mod", "w_in", "conv_w", "conv_b", "hg_lb", "ml_gate_b", "hg_norm_w", "ml_norm_w",
             "w_out", "ln_g", "ln_b")
    return (loss8[0, 0], gx[None], *[grads[n] for n in order], *[delta[n] for n in order],
            *[new_m[n] for n in order], *[new_v[n] for n in order])
```

```python
import jax
import jax.numpy as jnp
from jax import lax
from jax.experimental import pallas as pl
from jax.experimental.pallas import tpu as pltpu

F32 = jnp.float32
BF16 = jnp.bfloat16

D_MODEL = 2048
W_A = 1024
W_B = 1024
HG_HEADS = 8
HG_D = 128
ML_HEADS = 4
ML_D = 256
CHUNK = 64
N_IN = 10256
LANE = 128
N_U = 81 * LANE
N_DEV = 8
ALPHA = 2.0 ** 0.25
LN_EPS = 1e-5
NORM_EPS = 1e-6
ADAM_LR, ADAM_B1, ADAM_B2, ADAM_EPS, ADAM_WD, ADAM_STEP = 0.001, 0.9, 0.999, 1e-08, 0.01, 10
VMEM_CAP = 60 * 1024 * 1024

SEG_AQ, SEG_AFF, SEG_AFB, SEG_AI, SEG_AZ = range(5)
BLK_QK = 40
SEG_BV, SEG_BO, SEG_BZ = 7, 8, 9
BLK_GATE = 80

MESH = pl.DeviceIdType.MESH


def _vmem(nbytes):
    return pltpu.CompilerParams(vmem_limit_bytes=int(min(VMEM_CAP, max(nbytes, 16 * 1024 * 1024))))


def _sigmoid(x):
    return 1.0 / (1.0 + jnp.exp(-x))


def _silu(x):
    return x * _sigmoid(x)


def _dsilu(x):
    s = _sigmoid(x)
    return s * (1.0 + x * (1.0 - s))


def _silu_both(x):
    s = _sigmoid(x)
    return x * s, s * (1.0 + x * (1.0 - s))


def _bdot(a, b, dims):
    return lax.dot_general(a.astype(BF16), b.astype(BF16), (dims, ((), ())), preferred_element_type=F32)


def _nn(a, b):
    return _bdot(a, b, ((1,), (0,)))


def _nt(a, b):
    return _bdot(a, b, ((1,), (1,)))


def _tn(a, b):
    return _bdot(a, b, ((0,), (0,)))


def _exact_nn(a, b):
    return lax.dot_general(a, b, (((1,), (0,)), ((), ())), precision=lax.Precision.HIGHEST,
                           preferred_element_type=F32)


def _exact_tn(a, b):
    return lax.dot_general(a, b, (((0,), (0,)), ((), ())), precision=lax.Precision.HIGHEST,
                           preferred_element_type=F32)


def _tri(rev):
    t = lax.broadcasted_iota(jnp.int32, (CHUNK, CHUNK), 0)
    s = lax.broadcasted_iota(jnp.int32, (CHUNK, CHUNK), 1)
    return (s >= t) if rev else (s <= t)


def _eye():
    t = lax.broadcasted_iota(jnp.int32, (CHUNK, CHUNK), 0)
    s = lax.broadcasted_iota(jnp.int32, (CHUNK, CHUNK), 1)
    return (s == t).astype(F32)


def _row_to_col(row):
    return jnp.sum(_eye() * row, axis=1, keepdims=True)


def _last_onehot(rev):
    t = lax.broadcasted_iota(jnp.int32, (CHUNK, 1), 0)
    return (t == (0 if rev else CHUNK - 1)).astype(F32)


def _head_slices(width, n_heads):
    hd = width // n_heads
    return [slice(h * hd, (h + 1) * hd) for h in range(n_heads)]


def _scan_sum(x, rev):
    n = x.shape[0]
    t = lax.broadcasted_iota(jnp.int32, x.shape, 0)
    s = 1
    while s < n:
        if rev:
            x = x + jnp.where(t < n - s, pltpu.roll(x, n - s, 0), 0.0)
        else:
            x = x + jnp.where(t >= s, pltpu.roll(x, s, 0), 0.0)
        s *= 2
    return x


def _dot3(a, b, dims):
    a_hi, b_hi = a.astype(BF16), b.astype(BF16)
    a_lo, b_lo = (a - a_hi.astype(F32)).astype(BF16), (b - b_hi.astype(F32)).astype(BF16)
    dot = lambda x, y: lax.dot_general(x, y, (dims, ((), ())), preferred_element_type=F32)
    return dot(a_hi, b_hi) + (dot(a_hi, b_lo) + dot(a_lo, b_hi))


def _hg_common(zq, zf, lb, rev):
    q, dq_dz = _silu_both(zq)
    sg = _sigmoid(zf)
    f = lb + (1.0 - lb) * sg
    g = jnp.log(f)
    k = 1.0 - f
    b = _scan_sum(g, rev)
    b_last = jnp.sum(g, axis=0, keepdims=True)
    r = b[CHUNK // 2:CHUNK // 2 + 1, :]
    e_up = jnp.exp(b - r)
    e_dn = jnp.exp(r - b)
    e_b = e_up * jnp.exp(r)
    e_lb = e_dn * jnp.exp(b_last - r)
    return dict(q=q, dq_dz=dq_dz, sg=sg, f=f, k=k, e_up=e_up, e_dn=e_dn, e_b=e_b, e_lb=e_lb, e_last=jnp.exp(b_last),
                q_t=q * e_up, k_t=k * e_dn, q_s=q * e_b, k_h=k * e_lb, tri=_tri(rev).astype(F32))


def hg_chunk_fwd(zq, zf, v, lb, st, rev):
    c = _hg_common(zq, zf, lb, rev)
    hs = _head_slices(zq.shape[1], zq.shape[1] // HG_D)
    s = [_nt(c["q_t"][:, sl], c["k_t"][:, sl]) for sl in hs]
    oi = [_nt(c["q_s"][:, sl], st[sl, :]) for sl in hs]
    ds = [_tn(v[:, sl], c["k_h"][:, sl]) for sl in hs]
    oa = [_nn(c["tri"] * s_h, v[:, sl]) for s_h, sl in zip(s, hs)]
    o = jnp.concatenate([x + y for x, y in zip(oi, oa)], axis=1)
    st_new = jnp.concatenate([st[sl, :] * c["e_last"][:, sl] + d for sl, d in zip(hs, ds)], axis=0)
    return o, st_new


def hg_chunk_bwd(zq, zf, v, lb, st, do, dst_new, rev):
    c = _hg_common(zq, zf, lb, rev)
    hs = _head_slices(zq.shape[1], zq.shape[1] // HG_D)
    tri, q_t, k_t, q_s, k_h = c["tri"], c["q_t"], c["k_t"], c["q_s"], c["k_h"]
    s = [_nt(q_t[:, sl], k_t[:, sl]) for sl in hs]
    da = [tri * _nt(do[:, sl], v[:, sl]) for sl in hs]
    dq_s = [_nn(do[:, sl], st[sl, :]) for sl in hs]
    dk_h = [_nn(v[:, sl], dst_new[sl, :]) for sl in hs]
    dv_s = [_nt(k_h[:, sl], dst_new[sl, :]) for sl in hs]
    dst_q = [_tn(do[:, sl], q_s[:, sl]) for sl in hs]
    dq_t = [_dot3(da_h, k_t[:, sl], ((1,), (0,))) for da_h, sl in zip(da, hs)]
    dk_t = [_dot3(da_h, q_t[:, sl], ((0,), (0,))) for da_h, sl in zip(da, hs)]
    dv_a = [_tn(tri * s_h, do[:, sl]) for s_h, sl in zip(s, hs)]
    cat = lambda parts: jnp.concatenate(parts, axis=1)
    dq_s, dk_h, dq_t, dk_t = cat(dq_s), cat(dk_h), cat(dq_t), cat(dk_t)
    dv = cat([x + y for x, y in zip(dv_a, dv_s)])
    dst = jnp.concatenate([dst_new[sl, :] * c["e_last"][:, sl] + d for sl, d in zip(hs, dst_q)], axis=0)
    dq = dq_s * c["e_b"] + dq_t * c["e_up"]
    dk = dk_t * c["e_dn"] + dk_h * c["e_lb"]
    db = c["q"] * dq - c["k"] * dk
    ss = cat([jnp.sum(dst_new[sl, :] * st[sl, :], axis=0, keepdims=True) for sl in hs])
    d_all = jnp.sum(dk_h * k_h, axis=0, keepdims=True) + c["e_last"] * ss
    dg = _scan_sum(db, not rev) + d_all
    dzq = dq * c["dq_dz"]
    df = dg / c["f"] - dk
    dzf = df * (1.0 - lb) * c["sg"] * (1.0 - c["sg"])
    dlb = jnp.sum(df * (1.0 - c["sg"]), axis=0, keepdims=True)
    return dzq, dzf, dv, dlb, dst


def _log_sigmoid(x):
    return jnp.minimum(x, 0.0) - jnp.log(1.0 + jnp.exp(-jnp.abs(x)))


def _each(fn, *lists):
    return [fn(*xs) for xs in zip(*lists)]


def _bf(xs):
    return [x.astype(BF16) for x in xs]


def _ml_forward_parts(qp, kp, v, gates, c, n, m, rev, with_num):
    hs = _head_slices(qp.shape[1], qp.shape[1] // ML_D)
    q_all, dq_dp = _silu_both(qp)
    k_all, dk_dp = _silu_both(kp)
    k_all = k_all * (ML_D ** -0.5)
    q = [q_all[:, sl] for sl in hs]
    k = [k_all[:, sl] for sl in hs]
    vv = [v[:, sl] for sl in hs]
    cc = [c[sl, :] for sl in hs]
    tri_b = _tri(rev)
    tri = tri_b.astype(F32)
    tri_t = _tri(not rev).astype(F32)
    e_last = _last_onehot(rev)
    qb, kb, vb, cb = _bf(q), _bf(k), _bf(vv), _bf(cc)
    qk = _each(_nt, qb, kb)
    parts = []
    for (gi_c, gi_r, gf_c, gf_r), m_h in zip(gates, m):
        lf_c, lf_r = _log_sigmoid(gf_c), _log_sigmoid(gf_r)
        b_c = jnp.sum(tri * lf_r, axis=1, keepdims=True)
        b_r = jnp.sum(tri_t * lf_c, axis=0, keepdims=True)
        log_w = jnp.where(tri_b, b_c - b_r + gi_r, -jnp.inf)
        m_inter = b_c + m_h
        m_t = jnp.maximum(m_inter, jnp.max(log_w, axis=1, keepdims=True))
        m_new = jnp.sum(m_t * e_last, axis=0, keepdims=True)
        b_last = jnp.sum(b_c * e_last, axis=0, keepdims=True)
        parts.append(dict(a=jnp.exp(m_inter - m_t), p=jnp.exp(log_w - m_t), floor=jnp.exp(-m_t), m_new=m_new,
                          ws=jnp.exp(b_last - b_c + gi_c - m_new), decay=jnp.exp(b_last + m_h - m_new), gf_c=gf_c))
    w = [pt["p"] * x for pt, x in zip(parts, qk)]
    wb = _bf(w)
    for pt, q_h, n_h, w_h in zip(parts, q, n, w):
        qn = jnp.sum(q_h * n_h, axis=1, keepdims=True)
        den = pt["a"] * qn + jnp.sum(w_h, axis=1, keepdims=True)
        pt.update(qn=qn, den=den, rinv=1.0 / jnp.maximum(jnp.abs(den), pt["floor"]), w=w_h)
    if with_num:
        qc = _each(_nt, qb, cb)
        wv = _each(_nn, wb, vb)
        for pt, qc_h, wv_h in zip(parts, qc, wv):
            pt.update(num=pt["a"] * qc_h + wv_h)
    return hs, q, k, vv, cc, tri, parts, dict(q=qb, k=kb, v=vb, c=cb, w=wb, dq_dp=dq_dp, dk_dp=dk_dp)


def ml_chunk_fwd(qp, kp, v, gates, c, n, m, rev):
    hs, q, k, vv, cc, tri, parts, bf = _ml_forward_parts(qp, kp, v, gates, c, n, m, rev, True)
    h = jnp.concatenate([pt["num"] * pt["rinv"] for pt in parts], axis=1)
    upd = _each(_tn, [pt["ws"] * v_h for pt, v_h in zip(parts, vv)], bf["k"])
    c_new = jnp.concatenate([pt["decay"] * c_h + u for pt, c_h, u in zip(parts, cc, upd)], axis=0)
    n_new = [pt["decay"] * n_h + jnp.sum(pt["ws"] * k_h, axis=0, keepdims=True) for pt, n_h, k_h in zip(parts, n, k)]
    return h, c_new, n_new, [pt["m_new"] for pt in parts]


def ml_chunk_bwd(qp, kp, v, gates, c, n, m, h_out, dh, dc_new, dn_new, rev):
    hs, q, k, vv, cc, tri, parts, bf = _ml_forward_parts(qp, kp, v, gates, c, n, m, rev, False)
    dcn = [dc_new[sl, :] for sl in hs]
    dcb = _bf(dcn)
    dnum, dden = [], []
    for pt, sl in zip(parts, hs):
        dh_h = dh[:, sl]
        signed_live = jnp.where(jnp.abs(pt["den"]) > pt["floor"], jnp.where(pt["den"] >= 0.0, 1.0, -1.0), 0.0)
        dnum.append(dh_h * pt["rinv"])
        dden.append(-jnp.sum(dh_h * h_out[:, sl], axis=1, keepdims=True) * pt["rinv"] * signed_live)
    dnb = _bf(dnum)
    dw = [x + y for x, y in zip(_each(_nt, dnb, bf["v"]), dden)]
    kdc = _each(_nt, bf["k"], dcb)
    vdc = _each(_nn, bf["v"], dcb)
    dqk = [x * pt["p"] for x, pt in zip(dw, parts)]
    adn = [pt["a"] * x for pt, x in zip(parts, dnum)]
    dqkb, adnb = _bf(dqk), _bf(adn)
    dv_w = _each(_tn, bf["w"], dnb)
    dq_k = _each(_nn, dqkb, bf["k"])
    dq_c = _each(_nn, adnb, bf["c"])
    dk_q = _each(_tn, dqkb, bf["q"])
    dc_q = _each(_tn, adnb, bf["q"])
    dq, dk, dv, dgi, dgf, dc, dn = [], [], [], [], [], [], []
    for i, pt in enumerate(parts):
        a, ws, decay = pt["a"], pt["ws"], pt["decay"]
        add = a * dden[i]
        e = dw[i] * pt["w"]
        dv.append(dv_w[i] + ws * kdc[i])
        dq.append(dq_k[i] + dq_c[i] + add * n[i])
        dk.append(dk_q[i] + ws * vdc[i] + ws * dn_new[i])
        alpha = jnp.sum(q[i] * dq_c[i], axis=1, keepdims=True) + dden[i] * pt["qn"] * a
        omega = (jnp.sum(vdc[i] * k[i], axis=1, keepdims=True) + jnp.sum(k[i] * dn_new[i], axis=1, keepdims=True)) * ws
        delta = decay * (jnp.sum(jnp.sum(dcn[i] * cc[i], axis=1, keepdims=True), axis=0, keepdims=True)
                         + jnp.sum(dn_new[i] * n[i], axis=1, keepdims=True))
        dc.append(decay * dcn[i] + dc_q[i])
        dn.append(decay * dn_new[i] + jnp.sum(add * q[i], axis=0, keepdims=True))
        e_rows = jnp.sum(e, axis=1, keepdims=True)
        e_cols = _row_to_col(jnp.sum(e, axis=0, keepdims=True))
        dgi.append(e_cols + omega)
        db = e_rows + alpha - e_cols - omega
        tail = jnp.sum(omega, axis=0, keepdims=True) + delta
        dlf = _row_to_col(jnp.sum(tri * db, axis=0, keepdims=True)) + tail
        dgf.append(dlf * (1.0 - _sigmoid(pt["gf_c"])))
    cat = lambda xs: jnp.concatenate(xs, axis=1)
    dqp = cat(dq) * bf["dq_dp"]
    dkp = cat(dk) * (ML_D ** -0.5) * bf["dk_dp"]
    return dqp, dkp, cat(dv), dgi, dgf, jnp.concatenate(dc, axis=0), dn


def _pick(n, prefs):
    for p in prefs:
        if n % p == 0:
            return p
    raise ValueError(f"no tile for {n} among {prefs}")


def _position():
    return lax.axis_index("x"), lax.axis_index("y"), lax.axis_index("c")


class _Ride:
    def __init__(self, kind, x, cols=None, into=None):
        self.kind, self.x, self.cols, self.into = kind, x, cols, into
        r, c = x.shape[-2:]
        self.out_shape = jax.ShapeDtypeStruct((N_DEV, r, c if cols is None else cols[1]), x.dtype)
        self.width = c

    def _copies(self, x_ref, out_ref, send_sems, recv_sems, local_sem):
        px, py, pc = _position()
        me = 4 * px + 2 * py + pc
        src = (lambda slot: x_ref) if self.kind == "gather" else (lambda slot: x_ref.at[slot])
        dst = ((lambda slot: out_ref.at[slot]) if self.cols is None
               else (lambda slot: out_ref.at[slot, :, pl.ds(self.cols[0], self.width)]))
        mine = pltpu.make_async_copy(src(me), dst(me), local_sem)
        sends, recvs = [], []
        for k, (fx, fy, fc) in enumerate([(1, 0, 0), (0, 1, 0), (1, 1, 0), (1, 0, 1), (0, 1, 1), (1, 1, 1), (0, 0, 1)]):
            qx, qy, qc = (1 - px if fx else px), (1 - py if fy else py), (1 - pc if fc else pc)
            peer = 4 * qx + 2 * qy + qc
            sends.append(pltpu.make_async_remote_copy(
                src_ref=src(peer), dst_ref=dst(me), send_sem=send_sems.at[k], recv_sem=recv_sems.at[k],
                device_id=(qx, qy, qc), device_id_type=MESH))
            recvs.append(pltpu.make_async_remote_copy(
                src_ref=src(me), dst_ref=dst(peer), send_sem=send_sems.at[k], recv_sem=recv_sems.at[k],
                device_id=(qx, qy, qc), device_id_type=MESH))
        return mine, sends, recvs

    def start(self, *refs):
        mine, sends, _ = self._copies(*refs)
        mine.start()
        for cp in sends:
            cp.start()

    def wait(self, *refs):
        mine, sends, recvs = self._copies(*refs)
        for cp in recvs:
            cp.wait_recv()
        for cp in sends:
            cp.wait_send()
        mine.wait()

    def operands(self):
        return [self.x] + ([self.into] if self.into is not None else [])


_RIDE_SCRATCH = [pltpu.SemaphoreType.DMA((7,)), pltpu.SemaphoreType.DMA((7,)), pltpu.SemaphoreType.DMA]
_ANY = pl.BlockSpec(memory_space=pl.ANY)


def _mm(a, b, mode, out_dtype, tm, tn, tk, name, ride=None, b_cols=None, m_out=None, n_rows=None):
    if mode == "nn":
        (m, k), (k2, n) = a.shape, b.shape
    elif mode == "nt":
        (m, k), (n, k2) = a.shape, b.shape
        n = n if n_rows is None else n_rows
    else:
        (k, m), (k2, n) = a.shape, b.shape
    off, n = (0, n) if b_cols is None else b_cols
    assert (k2 >= k if mode == "nn" else k == k2) and m % tm == 0 and n % tn == 0 and k % tk == 0, \
        (a.shape, b.shape, mode, tm, tn, tk)
    assert b_cols is None or (mode != "nt" and off % LANE == 0)
    nk = k // tk
    dims = {"nn": ((1,), (0,)), "nt": ((1,), (1,)), "tn": ((0,), (0,))}[mode]
    a_spec = (pl.BlockSpec((tk, tm), lambda j, i, kk: (kk, i)) if mode == "tn"
              else pl.BlockSpec((tm, tk), lambda j, i, kk: (i, kk)))
    if b_cols is not None:
        b_spec = pl.BlockSpec((pl.Element(tk), pl.Element(tn)),
                              lambda j, i, kk: (pl.multiple_of(kk * tk, LANE), pl.multiple_of(off + j * tn, LANE)))
    elif mode == "nt":
        b_spec = pl.BlockSpec((tn, tk), lambda j, i, kk: (j, kk))
    else:
        b_spec = pl.BlockSpec((tk, tn), lambda j, i, kk: (kk, j))

    grid = (n // tn, m // tm, nk)
    n_ride_in = len(ride.operands()) if ride is not None else 0

    def body(a_ref, b_ref, *rest):
        if ride is not None:
            x_ref = rest[0]
            o_ref, got_ref, acc_ref = rest[n_ride_in:n_ride_in + 3]
            comm = (x_ref, got_ref) + tuple(rest[n_ride_in + 3:])
        else:
            o_ref, acc_ref = rest
        kk = pl.program_id(2)
        step = (pl.program_id(0) * grid[1] + pl.program_id(1)) * nk + kk
        if ride is not None:
            @pl.when(step == 0)
            def _():
                ride.start(*comm)

        part = lax.dot_general(a_ref[...], b_ref[...], (dims, ((), ())), preferred_element_type=F32)
        if nk == 1:
            o_ref[...] = part.astype(o_ref.dtype)
        else:
            @pl.when(kk == 0)
            def _():
                acc_ref[...] = part

            @pl.when(jnp.logical_and(kk > 0, kk < nk - 1))
            def _():
                acc_ref[...] += part

            @pl.when(kk == nk - 1)
            def _():
                o_ref[...] = (acc_ref[...] + part).astype(o_ref.dtype)

        if ride is not None:
            @pl.when(step == grid[0] * grid[1] * nk - 1)
            def _():
                ride.wait(*comm)

    osz = jnp.dtype(out_dtype).itemsize
    need = 2 * (tm * tk * a.dtype.itemsize + tk * tn * b.dtype.itemsize + tm * tn * osz) + tm * tn * 4
    o_spec = pl.BlockSpec((tm, tn), lambda j, i, kk: (i, j))
    o_shape = jax.ShapeDtypeStruct((m if m_out is None else m_out, n), out_dtype)
    extra = ride is not None
    return pl.pallas_call(
        body, name=name, grid=grid,
        in_specs=[a_spec, b_spec] + [_ANY] * n_ride_in,
        out_specs=[o_spec, _ANY] if extra else o_spec,
        out_shape=[o_shape, ride.out_shape] if extra else o_shape,
        scratch_shapes=[pltpu.VMEM((tm, tn) if nk > 1 else (8, LANE), F32)] + (_RIDE_SCRATCH if extra else []),
        input_output_aliases={3: 1} if extra and ride.into is not None else {},
        compiler_params=_vmem(need + (12 << 20)),
    )(a, b, *(ride.operands() if extra else []))


ROWS = 256


def _ln_stats(x):
    mu = jnp.mean(x, axis=-1, keepdims=True)
    xc = x - mu
    var = jnp.mean(xc * xc, axis=-1, keepdims=True)
    rstd = lax.rsqrt(var + LN_EPS)
    return xc * rstd, rstd


def _token_specs(nbc, nbx, d):
    return [pl.BlockSpec((ROWS, d), lambda i: (jnp.minimum(i, nbc - 1), 0)),
            pl.BlockSpec((ROWS, d), lambda i: (jnp.maximum(i - nbc, 0), 0))]


def _tokens(c_ref, x_ref, nbc):
    return jnp.where(pl.program_id(0) < nbc, c_ref[...], x_ref[...])


def _modulate_fwd(ctx, x, modp, gather=None):
    d = x.shape[1]
    nbc, nbx = ctx.shape[0] // ROWS, x.shape[0] // ROWS
    riding = gather is not None

    def body(c_ref, x_ref, mod_ref, *rest):
        if riding:
            comm = (rest[0], rest[2]) + tuple(rest[3:])
            o_ref = rest[1]

            @pl.when(pl.program_id(0) == 0)
            def _():
                gather.start(*comm)
        else:
            o_ref = rest[0]
        n, _ = _ln_stats(_tokens(c_ref, x_ref, nbc))
        o_ref[...] = (n * (1.0 + mod_ref[0, 1:2, :]) + mod_ref[0, 0:1, :]).astype(BF16)
        if riding:
            @pl.when(pl.program_id(0) == nbc + nbx - 1)
            def _():
                gather.finish(*comm)

    o_spec = pl.BlockSpec((ROWS, d), lambda i: (i, 0))
    o_shape = jax.ShapeDtypeStruct((ctx.shape[0] + x.shape[0], d), BF16)
    return pl.pallas_call(
        body, name="modulate_fwd", grid=(nbc + nbx,),
        in_specs=_token_specs(nbc, nbx, d) + [pl.BlockSpec((1, 3, d), lambda i: (jnp.where(i >= nbc, 1, 0), 0, 0))]
        + ([_ANY] if riding else []),
        out_specs=[o_spec, _ANY] if riding else o_spec,
        out_shape=[o_shape, gather.out_shape] if riding else o_shape,
        scratch_shapes=gather.scratch if riding else [],
    )(ctx, x, modp, *([gather.x] if riding else []))


def _modulate_bwd(dh, ctx, x, modp, dxa):
    t, d = x.shape
    nbc, nbx = ctx.shape[0] // ROWS, t // ROWS

    def body(dh_ref, c_ref, x_ref, mod_ref, dxa_ref, gx_ref, sum_ref):
        i = pl.program_id(0)
        n, rstd = _ln_stats(_tokens(c_ref, x_ref, nbc))
        g = dh_ref[...]
        dn = g * (1.0 + mod_ref[0, 1:2, :])
        dx = rstd * (dn - jnp.mean(dn, axis=-1, keepdims=True) - n * jnp.mean(dn * n, axis=-1, keepdims=True))
        gx_ref[...] = dx + dxa_ref[...]
        dshift = jnp.sum(g, axis=0, keepdims=True)
        dscale = jnp.sum(g * n, axis=0, keepdims=True)

        @pl.when(i == 0)
        def _():
            sum_ref[...] = jnp.zeros_like(sum_ref)

        @pl.when(i < nbc)
        def _():
            sum_ref[0:1, :] += dshift
            sum_ref[1:2, :] += dscale

        @pl.when(i >= nbc)
        def _():
            sum_ref[2:3, :] += dshift
            sum_ref[3:4, :] += dscale

    lat = lambda i: (jnp.maximum(i - nbc, 0), 0)
    return pl.pallas_call(
        body, name="modulate_bwd", grid=(nbc + nbx,),
        in_specs=[pl.BlockSpec((ROWS, d), lambda i: (i, 0))] + _token_specs(nbc, nbx, d)
        + [pl.BlockSpec((1, 3, d), lambda i: (jnp.where(i >= nbc, 1, 0), 0, 0)), pl.BlockSpec((ROWS, d), lat)],
        out_specs=[pl.BlockSpec((ROWS, d), lat), pl.BlockSpec((8, d), lambda i: (0, 0))],
        out_shape=[jax.ShapeDtypeStruct((t, d), F32), jax.ShapeDtypeStruct((8, d), F32)],
    )(dh, ctx, x, modp, dxa)


def _post_fwd(o_f, o_b, h_f, h_b, u, hgw, mlw, nbc):
    tt = u.shape[0]
    t = tt - nbc * ROWS

    def body(of_ref, ob_ref, hf_ref, hb_ref, az_ref, bo_ref, bz_ref, hgw_ref, mlw_ref, y_ref):
        o = of_ref[...] + ob_ref[...]
        for sl in _head_slices(W_A, HG_HEADS):
            oh = o[:, sl]
            rs = lax.rsqrt(jnp.mean(oh * oh, axis=-1, keepdims=True) + NORM_EPS)
            y_ref[:, sl] = (oh * rs * hgw_ref[:, sl] * _silu(az_ref[:, sl])).astype(BF16)
        hm = hf_ref[...] + hb_ref[...]
        for sl in _head_slices(W_B, ML_HEADS):
            hh = hm[:, sl]
            mu = jnp.mean(hh, axis=-1, keepdims=True)
            hc = hh - mu
            rstd = lax.rsqrt(jnp.mean(hc * hc, axis=-1, keepdims=True) + NORM_EPS)
            out = hc * rstd * mlw_ref[:, sl] * _sigmoid(bo_ref[:, sl]) * _silu(bz_ref[:, sl])
            y_ref[:, W_A + sl.start:W_A + sl.stop] = out.astype(BF16)

    row = lambda i: (i + nbc, 0)
    seg = lambda s: pl.BlockSpec((ROWS, 1024), lambda i: (i + nbc, s))
    wspec = pl.BlockSpec((1, 1024), lambda i: (0, 0))
    return pl.pallas_call(
        body, name="post_fwd", grid=(t // ROWS,),
        in_specs=[pl.BlockSpec((ROWS, 1024), row)] * 4 + [seg(SEG_AZ), seg(SEG_BO), seg(SEG_BZ), wspec, wspec],
        out_specs=pl.BlockSpec((ROWS, 2048), lambda i: (i, 0)),
        out_shape=jax.ShapeDtypeStruct((t, 2048), BF16),
    )(o_f, o_b, h_f, h_b, u, u, u, hgw, mlw)


def _post_bwd(dz, w_o, o_f, o_b, h_f, h_b, u, hgw, mlw, nbc):
    tt = u.shape[0]
    d = w_o.shape[0]

    def body(dz_ref, w_ref, of_ref, ob_ref, hf_ref, hb_ref, az_ref, bo_ref, bz_ref, hgw_ref, mlw_ref,
             do_ref, dhm_ref, daz_ref, dbo_ref, sum_ref):
        i = pl.program_id(0)
        live = jnp.where(i >= nbc, 1.0, 0.0)
        dy = lax.dot_general(dz_ref[...], w_ref[...], (((1,), (1,)), ((), ())), preferred_element_type=F32) * live

        @pl.when(i == 0)
        def _():
            sum_ref[...] = jnp.zeros_like(sum_ref)

        o = of_ref[...] + ob_ref[...]
        for sl in _head_slices(W_A, HG_HEADS):
            oh = o[:, sl]
            rs = lax.rsqrt(jnp.mean(oh * oh, axis=-1, keepdims=True) + NORM_EPS)
            on = oh * rs
            az = az_ref[:, sl]
            dya = dy[:, sl]
            saz, daz = _silu_both(az)
            doa = dya * saz
            daz_ref[:, sl] = (dya * on * hgw_ref[:, sl] * daz).astype(BF16)
            sum_ref[0:1, sl] += jnp.sum(doa * on, axis=0, keepdims=True)
            don = doa * hgw_ref[:, sl]
            do_ref[:, sl] = rs * (don - on * jnp.mean(don * on, axis=-1, keepdims=True))
        hm = hf_ref[...] + hb_ref[...]
        for sl in _head_slices(W_B, ML_HEADS):
            hh = hm[:, sl]
            mu = jnp.mean(hh, axis=-1, keepdims=True)
            hc = hh - mu
            rstd = lax.rsqrt(jnp.mean(hc * hc, axis=-1, keepdims=True) + NORM_EPS)
            hn = hc * rstd
            hw = hn * mlw_ref[:, sl]
            bo, bz = bo_ref[:, sl], bz_ref[:, sl]
            sbo = _sigmoid(bo)
            sbz, dbz = _silu_both(bz)
            dyb = dy[:, W_A + sl.start:W_A + sl.stop]
            dhw = dyb * sbo * sbz
            dbo_ref[:, sl] = (dyb * hw * sbz * sbo * (1.0 - sbo)).astype(BF16)
            dbo_ref[:, 1024 + sl.start:1024 + sl.stop] = (dyb * hw * sbo * dbz).astype(BF16)
            sum_ref[1:2, sl] += jnp.sum(dhw * hn, axis=0, keepdims=True)
            dhn = dhw * mlw_ref[:, sl]
            dhm_ref[:, sl] = rstd * (dhn - jnp.mean(dhn, axis=-1, keepdims=True)
                                     - hn * jnp.mean(dhn * hn, axis=-1, keepdims=True))

    row = lambda i: (i, 0)
    seg = lambda s: pl.BlockSpec((ROWS, 1024), lambda i: (i, s))
    wspec = pl.BlockSpec((1, 1024), lambda i: (0, 0))
    return pl.pallas_call(
        body, name="post_bwd", grid=(tt // ROWS,),
        in_specs=[pl.BlockSpec((ROWS, 2048), lambda i: (jnp.maximum(i - nbc, 0), 0)), pl.BlockSpec((d, d), lambda i: (0, 0))]
        + [pl.BlockSpec((ROWS, 1024), row)] * 4 + [seg(SEG_AZ), seg(SEG_BO), seg(SEG_BZ), wspec, wspec],
        out_specs=[pl.BlockSpec((ROWS, 1024), row), pl.BlockSpec((ROWS, 1024), row),
                   pl.BlockSpec((ROWS, 1024), row), pl.BlockSpec((ROWS, 2048), row),
                   pl.BlockSpec((8, 1024), lambda i: (0, 0))],
        out_shape=[jax.ShapeDtypeStruct((tt, 1024), F32), jax.ShapeDtypeStruct((tt, 1024), F32),
                   jax.ShapeDtypeStruct((tt, 1024), BF16), jax.ShapeDtypeStruct((tt, 2048), BF16),
                   jax.ShapeDtypeStruct((8, 1024), F32)],
        compiler_params=_vmem(4 * d * d + 30 * ROWS * 2048 * 4),
    )(dz, w_o, o_f, o_b, h_f, h_b, u, u, u, hgw, mlw)


def _final(y, w_o, x, target, modp, ln_g, ln_b):
    t, d = x.shape

    def body(y_ref, w_ref, x_ref, tg_ref, mod_ref, g_ref, b_ref, dz_ref, dxa_ref, sum_ref):
        i = pl.program_id(0)
        zz = lax.dot_general(y_ref[...], w_ref[...], (((1,), (0,)), ((), ())), preferred_element_type=F32)
        gate = mod_ref[0, 2:3, :]
        pre = ALPHA * x_ref[...] + gate * zz
        nh, rstd = _ln_stats(pre)
        err = nh * g_ref[...] + b_ref[...] - tg_ref[...]
        dxo = err * (1.0 / d)
        dnh = dxo * g_ref[...]
        dpre = rstd * (dnh - jnp.mean(dnh, axis=-1, keepdims=True) - nh * jnp.mean(dnh * nh, axis=-1, keepdims=True))
        dz_ref[...] = (gate * dpre).astype(BF16)
        dxa_ref[...] = ALPHA * dpre

        @pl.when(i == 0)
        def _():
            sum_ref[...] = jnp.zeros_like(sum_ref)

        sum_ref[0:1, :] += jnp.sum(dpre * zz, axis=0, keepdims=True)
        sum_ref[1:2, :] += jnp.sum(dxo * nh, axis=0, keepdims=True)
        sum_ref[2:3, :] += jnp.sum(dxo, axis=0, keepdims=True)
        sum_ref[3:4, :] += jnp.sum(err * err, axis=0, keepdims=True)

    row = lambda i: (i, 0)
    vec = pl.BlockSpec((1, d), lambda i: (0, 0))
    return pl.pallas_call(
        body, name="final_ln_loss", grid=(t // ROWS,),
        in_specs=[pl.BlockSpec((ROWS, d), row), pl.BlockSpec((d, d), lambda i: (0, 0)), pl.BlockSpec((ROWS, d), row),
                  pl.BlockSpec((ROWS, d), row), pl.BlockSpec((1, 3, d), lambda i: (1, 0, 0)), vec, vec],
        out_specs=[pl.BlockSpec((ROWS, d), row), pl.BlockSpec((ROWS, d), row), pl.BlockSpec((8, d), lambda i: (0, 0))],
        out_shape=[jax.ShapeDtypeStruct((t, d), BF16), jax.ShapeDtypeStruct((t, d), F32),
                   jax.ShapeDtypeStruct((8, d), F32)],
        compiler_params=_vmem(4 * d * d + 24 * ROWS * d * 4),
    )(y, w_o, x, target, modp, ln_g, ln_b)


GRID_W = 64


def _shift(x, s, ok):
    n = x.shape[0]
    return jnp.where(ok, pltpu.roll(x, s % n, 0), 0.0)


def _grid_masks(n):
    t = lax.broadcasted_iota(jnp.int32, (n, LANE), 0)
    col = t & (GRID_W - 1)
    return dict(left=col >= 1, right=col <= GRID_W - 2, up=t >= GRID_W, down=t < n - GRID_W)


def _seq_masks(n):
    t = lax.broadcasted_iota(jnp.int32, (n, LANE), 0)
    return dict(left=t >= 1, right=t <= n - 2)


def _conv_fwd(u, w9, cb, tc):
    tt = u.shape[0]
    t = tt - tc

    def body(u_ref, w_ref, b_ref, o_ref):
        w = [w_ref[r:r + 1, :] for r in range(9)]
        xc = u_ref[0:tc, :]
        ms = _seq_masks(tc)
        o_ref[0:tc, :] = (w[3] * _shift(xc, 1, ms["left"]) + w[4] * xc + w[5] * _shift(xc, -1, ms["right"])
                          + b_ref[...])
        x = u_ref[tc:tt, :]
        mg = _grid_masks(t)
        taps = (_shift(x, 1, mg["left"]), x, _shift(x, -1, mg["right"]))
        rows = [w[3 * i] * taps[0] + w[3 * i + 1] * taps[1] + w[3 * i + 2] * taps[2] for i in range(3)]
        o_ref[tc:tt, :] = (rows[1] + _shift(rows[0], GRID_W, mg["up"]) + _shift(rows[2], -GRID_W, mg["down"])
                           + b_ref[...])

    return pl.pallas_call(
        body, name="conv_fwd", grid=(2048 // LANE,),
        in_specs=[pl.BlockSpec((tt, LANE), lambda j: (0, BLK_QK + j)), pl.BlockSpec((9, LANE), lambda j: (0, j)),
                  pl.BlockSpec((1, LANE), lambda j: (0, j))],
        out_specs=pl.BlockSpec((tt, LANE), lambda j: (0, j)),
        out_shape=jax.ShapeDtypeStruct((tt, 2048), F32),
        compiler_params=_vmem(40 * tt * LANE * 4),
    )(u, w9, cb)


def _conv_bwd(dcp, u, w9, tc, du):
    tt = u.shape[0]
    t = tt - tc

    def body(d_ref, u_ref, w_ref, du_in_ref, du_ref, gw_ref, gb_ref):
        w = [w_ref[r:r + 1, :] for r in range(9)]
        csum = lambda a: jnp.sum(a, axis=0, keepdims=True)
        dc = d_ref[0:tc, :]
        xc = u_ref[0:tc, :]
        ms = _seq_masks(tc)
        du_ref[0:tc, :] = (w[3] * _shift(dc, -1, ms["right"]) + w[4] * dc + w[5] * _shift(dc, 1, ms["left"])).astype(BF16)
        gmid = [csum(dc * _shift(xc, 1, ms["left"])), csum(dc * xc), csum(dc * _shift(xc, -1, ms["right"]))]
        d = d_ref[tc:tt, :]
        x = u_ref[tc:tt, :]
        mg = _grid_masks(t)
        dtaps = (_shift(d, -1, mg["right"]), d, _shift(d, 1, mg["left"]))
        rows = [w[3 * i] * dtaps[0] + w[3 * i + 1] * dtaps[1] + w[3 * i + 2] * dtaps[2] for i in range(3)]
        du_ref[tc:tt, :] = (rows[1] + _shift(rows[0], -GRID_W, mg["down"]) + _shift(rows[2], GRID_W, mg["up"])).astype(BF16)
        xtaps = (_shift(x, 1, mg["left"]), x, _shift(x, -1, mg["right"]))
        for j in range(3):
            gw_ref[j:j + 1, :] = csum(d * _shift(xtaps[j], GRID_W, mg["up"]))
            gw_ref[3 + j:4 + j, :] = csum(d * xtaps[j]) + gmid[j]
            gw_ref[6 + j:7 + j, :] = csum(d * _shift(xtaps[j], -GRID_W, mg["down"]))
        gb_ref[...] = csum(d) + csum(dc)

    return pl.pallas_call(
        body, name="conv_bwd", grid=(2048 // LANE,),
        in_specs=[pl.BlockSpec((tt, LANE), lambda j: (0, j)), pl.BlockSpec((tt, LANE), lambda j: (0, BLK_QK + j)),
                  pl.BlockSpec((9, LANE), lambda j: (0, j)), _ANY],
        out_specs=[pl.BlockSpec((tt, LANE), lambda j: (0, BLK_QK + j)), pl.BlockSpec((9, LANE), lambda j: (0, j)),
                   pl.BlockSpec((1, LANE), lambda j: (0, j))],
        out_shape=[jax.ShapeDtypeStruct(du.shape, BF16), jax.ShapeDtypeStruct((9, 2048), F32),
                   jax.ShapeDtypeStruct((1, 2048), F32)],
        input_output_aliases={3: 0},
        compiler_params=_vmem(48 * tt * LANE * 4),
    )(dcp, u, w9, du)


SUB = 4
STEP = SUB * CHUNK
ML_SUB = 2
ML_STEP = ML_SUB * CHUNK


def _chunk_of(pos, ncc, nc, rev):
    if not rev:
        return pos
    return jnp.where(pos < ncc, ncc - 1 - pos, nc - 1 - (pos - ncc))


def _sub_rows(rev, sub=SUB):
    order = range(sub - 1, -1, -1) if rev else range(sub)
    return [(s, slice(s * CHUNK, (s + 1) * CHUNK)) for s in order]


def _hgrn_fwd(u, lower_d, ncc, rev):
    tt = u.shape[0]
    nc, ncc = tt // STEP, ncc // SUB
    seg_f = SEG_AFB if rev else SEG_AFF

    def body(zq_ref, zf_ref, v_ref, lb_ref, o_ref, hist_ref, st_ref):
        @pl.when(pl.program_id(0) == 0)
        def _():
            st_ref[...] = jnp.zeros_like(st_ref)

        st = st_ref[...]
        for s, r in _sub_rows(rev):
            hist_ref[s] = st
            o, st = hg_chunk_fwd(zq_ref[r, :], zf_ref[r, :], v_ref[r, :], lb_ref[...], st, rev)
            o_ref[r, :] = o
        st_ref[...] = st

    seg = lambda s: pl.BlockSpec((STEP, 1024), lambda j: (_chunk_of(j, ncc, nc, rev), s))
    return pl.pallas_call(
        body, name="hgrn_fwd_rev" if rev else "hgrn_fwd", grid=(nc,),
        in_specs=[seg(SEG_AQ), seg(seg_f), seg(SEG_AI), pl.BlockSpec((1, 1024), lambda j: (0, 0))],
        out_specs=[pl.BlockSpec((STEP, 1024), lambda j: (_chunk_of(j, ncc, nc, rev), 0)),
                   pl.BlockSpec((SUB, 1024, HG_D), lambda j: (_chunk_of(j, ncc, nc, rev), 0, 0))],
        out_shape=[jax.ShapeDtypeStruct((tt, 1024), F32), jax.ShapeDtypeStruct((nc * SUB, 1024, HG_D), F32)],
        scratch_shapes=[pltpu.VMEM((1024, HG_D), F32)],
    )(u, u, u, lower_d)


def _hgrn_bwd(u, lower_d, hist, do, ncc, rev, ride=None, final=None):
    tt = u.shape[0]
    nc, ncc = tt // STEP, ncc // SUB
    seg_f = SEG_AFB if rev else SEG_AFF
    is_final = final is not None
    has_a2a = ride is not None
    n_out = 2 if is_final else 4
    width = 5 * 1024

    def body(zq_ref, zf_ref, v_ref, lb_ref, hist_ref, do_ref, *rest):
        if is_final:
            aq_ref, av_ref, af_ref, az_ref = rest[:4]
            rest = rest[4:]
        if has_a2a:
            x_ref, rest = rest[0], rest[1:]
        outs, rest = rest[:n_out], rest[n_out:]
        dlb_ref = outs[-1]
        if has_a2a:
            comm = (x_ref, rest[0]) + tuple(rest[2:])
            dst_ref = rest[1]
        else:
            dst_ref = rest[0]

        @pl.when(pl.program_id(0) == 0)
        def _():
            dst_ref[...] = jnp.zeros_like(dst_ref)
            dlb_ref[...] = jnp.zeros_like(dlb_ref)
            if has_a2a:
                ride.start(*comm)

        dst = dst_ref[...]
        dlb_sum = dlb_ref[...]
        for s, r in reversed(_sub_rows(rev)):
            dzq, dzf, dv, dlb, dst = hg_chunk_bwd(zq_ref[r, :], zf_ref[r, :], v_ref[r, :], lb_ref[...],
                                                  hist_ref[s], do_ref[r, :], dst, rev)
            dlb_sum = dlb_sum + dlb
            if is_final:
                du_ref = outs[0]
                dzf_own, dzf_other = dzf.astype(BF16), af_ref[r, :]
                du_ref[r, 0:1024] = (dzq + aq_ref[r, :]).astype(BF16)
                du_ref[r, 1024:2048] = dzf_other if rev else dzf_own
                du_ref[r, 2048:3072] = dzf_own if rev else dzf_other
                du_ref[r, 3072:4096] = (dv + av_ref[r, :]).astype(BF16)
                du_ref[r, 4096:5120] = az_ref[r, :]
            else:
                dzf_ref, dzq_ref, dv_ref = outs[:3]
                dzf_ref[r, :] = dzf.astype(BF16)
                dzq_ref[r, :] = dzq
                dv_ref[r, :] = dv
        dst_ref[...] = dst
        dlb_ref[...] = dlb_sum

        if has_a2a:
            @pl.when(pl.program_id(0) == nc - 1)
            def _():
                ride.wait(*comm)

    cidx = lambda j: _chunk_of(nc - 1 - j, ncc, nc, rev)
    seg = lambda s: pl.BlockSpec((STEP, 1024), lambda j: (cidx(j), s))
    row = pl.BlockSpec((STEP, 1024), lambda j: (cidx(j), 0))
    dlb_spec = pl.BlockSpec((1, 1024), lambda j: (0, 0))
    dlb_shape = jax.ShapeDtypeStruct((1, 1024), F32)
    if is_final:
        out_specs = [pl.BlockSpec((STEP, width), lambda j: (cidx(j), 0)), dlb_spec]
        out_shape = [jax.ShapeDtypeStruct((tt, N_U), BF16), dlb_shape]
    else:
        out_specs = [row, row, row, dlb_spec]
        out_shape = [jax.ShapeDtypeStruct((tt, 1024), BF16), jax.ShapeDtypeStruct((tt, 1024), F32),
                     jax.ShapeDtypeStruct((tt, 1024), F32), dlb_shape]
    ins = [u, u, u, lower_d, hist, do] + (list(final) if is_final else []) + ([ride.x] if has_a2a else [])
    return pl.pallas_call(
        body, name="hgrn_bwd_rev" if rev else "hgrn_bwd", grid=(nc,),
        in_specs=[seg(SEG_AQ), seg(seg_f), seg(SEG_AI), pl.BlockSpec((1, 1024), lambda j: (0, 0)),
                  pl.BlockSpec((SUB, 1024, HG_D), lambda j: (cidx(j), 0, 0)), row] + ([row] * 4 if is_final else [])
        + ([_ANY] if has_a2a else []),
        out_specs=out_specs + ([_ANY] if has_a2a else []),
        out_shape=out_shape + ([ride.out_shape] if has_a2a else []),
        scratch_shapes=[pltpu.VMEM((1024, HG_D), F32)] + (_RIDE_SCRATCH if has_a2a else []),
    )(*ins)


def _gate_views(g_ref, b_ref, r, head, rev):
    gc = g_ref[r, :] + b_ref[...]
    lane = lax.broadcasted_iota(jnp.int32, (1, LANE), 1)
    eye = _eye()
    d = 1 if rev else 0
    ii, fi = d * ML_HEADS + head, 2 * ML_HEADS + d * ML_HEADS + head
    col = lambda idx: jnp.sum(jnp.where(lane == idx, gc, 0.0), axis=1, keepdims=True)
    row = lambda c: jnp.sum(eye * c, axis=0, keepdims=True)
    gi, gf = col(ii), col(fi)
    return gi, row(gi), gf, row(gf)


def _mlstm_fwd(cpre, u, bias, ncc, rev):
    tt = u.shape[0]
    nc, ncc = tt // ML_STEP, ncc // ML_SUB
    nhd = ML_HEADS

    def body(q_ref, k_ref, v_ref, g_ref, b_ref, h_ref, ch_ref, nh_ref, mh_ref, c_ref, n_ref, m_ref):
        @pl.when(pl.program_id(0) == 0)
        def _():
            c_ref[...] = jnp.zeros_like(c_ref)
            n_ref[...] = jnp.zeros_like(n_ref)
            m_ref[...] = jnp.zeros_like(m_ref)

        c, n_all, m_all = c_ref[...], n_ref[...], m_ref[...]
        n = [n_all[hd:hd + 1, :] for hd in range(nhd)]
        m = [m_all[hd:hd + 1, 0:1] for hd in range(nhd)]
        for s, r in _sub_rows(rev, ML_SUB):
            ch_ref[s] = c
            for hd in range(nhd):
                nh_ref[s, hd:hd + 1, :] = n[hd]
                mh_ref[s, hd:hd + 1, :] = jnp.broadcast_to(m[hd], (1, LANE))
            gates = [_gate_views(g_ref, b_ref, r, hd, rev) for hd in range(nhd)]
            h, c, n, m = ml_chunk_fwd(q_ref[r, :], k_ref[r, :], v_ref[r, :], gates, c, n, m, rev)
            h_ref[r, :] = h
        c_ref[...] = c
        for hd in range(nhd):
            n_ref[hd:hd + 1, :] = n[hd]
            m_ref[hd:hd + 1, :] = jnp.broadcast_to(m[hd], (1, LANE))

    cidx = lambda j: _chunk_of(j, ncc, nc, rev)
    row = lambda s: pl.BlockSpec((ML_STEP, 1024), lambda j: (cidx(j), s))
    st3 = lambda a, b: pl.BlockSpec((ML_SUB, a, b), lambda j: (cidx(j), 0, 0))
    return pl.pallas_call(
        body, name="mlstm_fwd_rev" if rev else "mlstm_fwd", grid=(nc,),
        in_specs=[row(0), row(1), row(SEG_BV), pl.BlockSpec((ML_STEP, LANE), lambda j: (cidx(j), BLK_GATE)),
                  pl.BlockSpec((1, LANE), lambda j: (0, 0))],
        out_specs=[row(0), st3(1024, ML_D), st3(8, ML_D), st3(8, LANE)],
        out_shape=[jax.ShapeDtypeStruct((tt, 1024), F32), jax.ShapeDtypeStruct((nc * ML_SUB, 1024, ML_D), F32),
                   jax.ShapeDtypeStruct((nc * ML_SUB, 8, ML_D), F32), jax.ShapeDtypeStruct((nc * ML_SUB, 8, LANE), F32)],
        scratch_shapes=[pltpu.VMEM((1024, ML_D), F32), pltpu.VMEM((8, ML_D), F32), pltpu.VMEM((8, LANE), F32)],
    )(cpre, cpre, u, u, bias)


def _mlstm_bwd(cpre, u, bias, chist, nhist, mhist, h_out, dh, ncc, rev, final=None):
    tt = u.shape[0]
    nc, ncc = tt // ML_STEP, ncc // ML_SUB
    nhd = ML_HEADS
    is_final = final is not None
    d = 1 if rev else 0
    col0, width = SEG_BV * 1024, N_U - SEG_BV * 1024

    def body(q_ref, k_ref, v_ref, g_ref, b_ref, ch_ref, nh_ref, mh_ref, ho_ref, dh_ref, *rest):
        if is_final:
            aqk_ref, av_ref, ag_ref, bo_ref = rest[:4]
            dqk_ref, du_ref, gs_ref, dc_ref, dn_ref = rest[5:]
        else:
            dqk_ref, dv_ref, dg_ref, gs_ref, dc_ref, dn_ref = rest

        @pl.when(pl.program_id(0) == 0)
        def _():
            dc_ref[...] = jnp.zeros_like(dc_ref)
            dn_ref[...] = jnp.zeros_like(dn_ref)
            gs_ref[...] = jnp.zeros_like(gs_ref)

        lane = lax.broadcasted_iota(jnp.int32, (1, LANE), 1)
        dc, dn_all, gs = dc_ref[...], dn_ref[...], gs_ref[...]
        dn = [dn_all[hd:hd + 1, :] for hd in range(nhd)]
        for s, r in reversed(_sub_rows(rev, ML_SUB)):
            gates = [_gate_views(g_ref, b_ref, r, hd, rev) for hd in range(nhd)]
            n_all, m_all = nh_ref[s], mh_ref[s]
            dqp, dkp, dv, dgi, dgf, dc, dn = ml_chunk_bwd(
                q_ref[r, :], k_ref[r, :], v_ref[r, :], gates, ch_ref[s],
                [n_all[hd:hd + 1, :] for hd in range(nhd)], [m_all[hd:hd + 1, 0:1] for hd in range(nhd)],
                ho_ref[r, :], dh_ref[r, :], dc, dn, rev)
            dg = ag_ref[r, :] if is_final else jnp.zeros((CHUNK, LANE), F32)
            for hd in range(nhd):
                dg = dg + jnp.where(lane == d * ML_HEADS + hd, dgi[hd], 0.0)
                dg = dg + jnp.where(lane == 2 * ML_HEADS + d * ML_HEADS + hd, dgf[hd], 0.0)
            if is_final:
                dqp = dqp + aqk_ref[r, 0:W_B]
                dkp = dkp + aqk_ref[r, W_B:2 * W_B]
                du_ref[r, 0:1024] = (dv + av_ref[r, :]).astype(BF16)
                du_ref[r, 1024:3072] = bo_ref[r, :]
                du_ref[r, 3072:3072 + LANE] = dg.astype(BF16)
            else:
                dv_ref[r, :] = dv
                dg_ref[r, :] = dg
            dqk_ref[r, 0:W_B] = dqp
            dqk_ref[r, W_B:2 * W_B] = dkp
            gs = gs + jnp.sum(dg, axis=0, keepdims=True)
        dc_ref[...] = dc
        gs_ref[...] = gs
        for hd in range(nhd):
            dn_ref[hd:hd + 1, :] = dn[hd]

    cidx = lambda j: _chunk_of(nc - 1 - j, ncc, nc, rev)
    row = lambda s: pl.BlockSpec((ML_STEP, 1024), lambda j: (cidx(j), s))
    wide = pl.BlockSpec((ML_STEP, 2048), lambda j: (cidx(j), 0))
    gate = pl.BlockSpec((ML_STEP, LANE), lambda j: (cidx(j), 0))
    st3 = lambda a, b: pl.BlockSpec((ML_SUB, a, b), lambda j: (cidx(j), 0, 0))
    gs_spec, gs_shape = pl.BlockSpec((1, LANE), lambda j: (0, 0)), jax.ShapeDtypeStruct((1, LANE), F32)
    dqk_shape = jax.ShapeDtypeStruct((tt, 2048), F32)
    ins = [cpre, cpre, u, u, bias, chist, nhist, mhist, h_out, dh] + (list(final) if is_final else [])
    if is_final:
        out_specs = [wide, pl.BlockSpec((pl.Element(ML_STEP), pl.Element(width)), lambda j: (cidx(j) * ML_STEP, col0)), gs_spec]
        out_shape = [dqk_shape, jax.ShapeDtypeStruct((tt, N_U), BF16), gs_shape]
    else:
        out_specs = [wide, row(0), gate, gs_spec]
        out_shape = [dqk_shape, jax.ShapeDtypeStruct((tt, 1024), F32), jax.ShapeDtypeStruct((tt, LANE), F32), gs_shape]
    return pl.pallas_call(
        body, name="mlstm_bwd_rev" if rev else "mlstm_bwd", grid=(nc,),
        in_specs=[row(0), row(1), row(SEG_BV), pl.BlockSpec((ML_STEP, LANE), lambda j: (cidx(j), BLK_GATE)),
                  pl.BlockSpec((1, LANE), lambda j: (0, 0)),
                  st3(1024, ML_D), st3(8, ML_D), st3(8, LANE), row(0), row(0)]
        + ([wide, row(0), gate, wide, _ANY] if is_final else []),
        out_specs=out_specs, out_shape=out_shape,
        input_output_aliases={14: 1} if is_final else {},
        scratch_shapes=[pltpu.VMEM((1024, ML_D), F32), pltpu.VMEM((8, ML_D), F32)],
    )(*ins)


def _whole(body, out_shape, name, *args, nbytes=0):
    return pl.pallas_call(body, name=name, out_shape=out_shape, compiler_params=_vmem(nbytes))(*args)


def _mod_fwd(cs, w_cols, b_cols):
    def body(c_ref, w_ref, b_ref, o_ref):
        o_ref[...] = _exact_nn(_silu(c_ref[...]), w_ref[...]) + b_ref[...]

    return _whole(body, jax.ShapeDtypeStruct((16, w_cols.shape[1]), F32), "mod_fwd", cs, w_cols, b_cols,
                  nbytes=4 * w_cols.size * 4)


def _mod_bwd_w(cs, d9, w_cols):
    def body(c_ref, d_ref, w_ref, gw_ref, pc_ref):
        gw_ref[...] = _exact_tn(_silu(c_ref[...]), d_ref[...])
        pc = lax.dot_general(d_ref[8:16, :], w_ref[...], (((1,), (1,)), ((), ())), precision=lax.Precision.HIGHEST,
                             preferred_element_type=F32)
        row = lax.broadcasted_iota(jnp.int32, pc.shape, 0)
        pc_ref[...] = jnp.where(row == 0, pc, 0.0)

    return _whole(body, [jax.ShapeDtypeStruct(w_cols.shape, F32), jax.ShapeDtypeStruct((8, w_cols.shape[0]), F32)],
                  "mod_bwd_w", cs, d9, w_cols, nbytes=6 * w_cols.size * 4)


def _lower_fwd(lb4):
    def body(l_ref, o_ref):
        o_ref[...] = jnp.zeros_like(o_ref)
        o_ref[0:1, :] = 1.0 / (1.0 + jnp.exp(l_ref[1:2, :] - l_ref[0:1, :]))
        o_ref[1:2, :] = 1.0 / (1.0 + jnp.exp(l_ref[3:4, :] - l_ref[2:3, :]))

    return _whole(body, jax.ShapeDtypeStruct((8, lb4.shape[1]), F32), "lower_fwd", lb4)


def _reduce8(g, name):
    def body(g_ref, o_ref):
        acc = g_ref[0]
        for k in range(1, N_DEV):
            acc = acc + g_ref[k]
        o_ref[...] = acc

    return _whole(body, jax.ShapeDtypeStruct(g.shape[1:], F32), name, g, nbytes=4 * g.size * 4)


_PACK = (("dmodx", 48), ("dmodc", 48), ("gconvw", 144), ("gconvb", 16), ("dlower", 16), ("ghgw", 8), ("gmlw", 8),
         ("glng", 16), ("glnb", 16), ("losssq", 16), ("ggate", 8))


def _pack_offsets():
    off, out = 0, {}
    for name, rows in _PACK:
        out[name] = (off, rows)
        off += rows
    return out


def _small_finish(total, p0, d_feat):
    offs = _pack_offsets()

    def body(t_ref, p_ref, gb_ref, a0_ref, a1_ref, loss_ref):
        ox, oc, ol, oq = offs["dmodx"][0], offs["dmodc"][0], offs["dlower"][0], offs["losssq"][0]
        gb_ref[...] = t_ref[ox:ox + 48, :] + t_ref[oc:oc + 48, :]
        p = p_ref[...]
        da0 = t_ref[ol:ol + 16, :] * p * (1.0 - p)
        a0_ref[...] = da0
        a1_ref[...] = -da0
        sq = t_ref[oq:oq + 16, :]
        tot = jnp.sum(jnp.sum(sq, axis=1, keepdims=True), axis=0, keepdims=True)
        loss_ref[...] = jnp.broadcast_to(tot * (0.5 / d_feat), loss_ref.shape)

    s = jax.ShapeDtypeStruct
    return _whole(body, [s((48, LANE), F32), s((16, LANE), F32), s((16, LANE), F32), s((8, LANE), F32)],
                  "small_finish", total, p0)


def _cctx_grad(parts, c_ctx8):
    def body(p_ref, c_ref, o_ref):
        acc = p_ref[0]
        for k in range(1, N_DEV):
            acc = acc + p_ref[k]
        o_ref[...] = acc * _dsilu(c_ref[...])

    return _whole(body, jax.ShapeDtypeStruct(c_ctx8.shape, F32), "cctx_grad", parts, c_ctx8)


def _adam_math(w, g, m, v):
    m = ADAM_B1 * m + (1.0 - ADAM_B1) * g
    v = ADAM_B2 * v + (1.0 - ADAM_B2) * (g * g)
    m_hat = m / (1.0 - ADAM_B1 ** ADAM_STEP)
    v_hat = v / (1.0 - ADAM_B2 ** ADAM_STEP)
    delta = -ADAM_LR * (m_hat / (jnp.sqrt(v_hat) + ADAM_EPS) + ADAM_WD * w)
    return delta, m, v


def _adamw(w, g, m, v, rows, name):
    r, c = w.shape

    def body(w_ref, g_ref, m_ref, v_ref, d_ref, mo_ref, vo_ref):
        d_ref[...], mo_ref[...], vo_ref[...] = _adam_math(w_ref[...], g_ref[...], m_ref[...], v_ref[...])

    spec = pl.BlockSpec((rows, c), lambda i: (i, 0))
    return pl.pallas_call(
        body, name=name, grid=(r // rows,), in_specs=[spec] * 4, out_specs=[spec] * 3,
        out_shape=[jax.ShapeDtypeStruct((r, c), F32)] * 3,
        compiler_params=_vmem(16 * rows * (c + LANE) * 4),
    )(w, g, m, v)


def _rs_adamw(recv, w, m, v, tile, name, by_cols=False):
    _, r, c = recv.shape

    def body(r_ref, w_ref, m_ref, v_ref, g_ref, d_ref, mo_ref, vo_ref):
        g = r_ref[0].astype(F32)
        for k in range(1, N_DEV):
            g = g + r_ref[k].astype(F32)
        g_ref[...] = g
        d_ref[...], mo_ref[...], vo_ref[...] = _adam_math(w_ref[...], g, m_ref[...], v_ref[...])

    if by_cols:
        spec = pl.BlockSpec((r, tile), lambda i: (0, i))
        rspec = pl.BlockSpec((N_DEV, r, tile), lambda i: (0, 0, i))
        steps, elems = c // tile, (r + 16) * tile
    else:
        spec = pl.BlockSpec((tile, c), lambda i: (i, 0))
        rspec = pl.BlockSpec((N_DEV, tile, c), lambda i: (0, i, 0))
        steps, elems = r // tile, tile * (c + LANE)
    return pl.pallas_call(
        body, name=name, grid=(steps,), in_specs=[rspec] + [spec] * 3, out_specs=[spec] * 4,
        out_shape=[jax.ShapeDtypeStruct((r, c), F32)] * 4,
        compiler_params=_vmem(2 * elems * (N_DEV * 2 + 7 * 4) + (4 << 20)),
    )(recv, w, m, v)


def _all_gather(x, name):
    r, c = x.shape

    def body(x_ref, out_ref, send_sems, recv_sems, local_sem):
        px, py, pc = _position()
        me, sibling = (px, py, pc), (px, py, 1 - pc)
        chips = [(1 - px, py), (px, 1 - py), (1 - px, 1 - py)]

        def slot(qx, qy, qc):
            return out_ref.at[4 * qx + 2 * qy + qc]

        def copy(k, block, to, src=None):
            return pltpu.make_async_remote_copy(
                src_ref=slot(*block) if src is None else src, dst_ref=slot(*block),
                send_sem=send_sems.at[k], recv_sem=recv_sems.at[k], device_id=to, device_id_type=MESH)

        mine = pltpu.make_async_copy(x_ref, slot(*me), local_sem)
        mine.start()
        first = [copy(1 + j, me, (*chip, pc), src=x_ref) for j, chip in enumerate(chips)]
        first.append(copy(0, me, sibling, src=x_ref))
        for cp in first:
            cp.start()
        passed = [copy(4 + j, (*chip, pc), sibling) for j, chip in enumerate(chips)]
        for j, chip in enumerate(chips):
            copy(1 + j, (*chip, pc), me).wait_recv()
            passed[j].start()
        copy(0, sibling, me).wait_recv()
        for j, chip in enumerate(chips):
            copy(4 + j, (*chip, 1 - pc), me).wait_recv()
        for cp in first + passed:
            cp.wait_send()
        mine.wait()

    return pl.pallas_call(
        body, name=name, out_shape=jax.ShapeDtypeStruct((N_DEV, r, c), x.dtype),
        in_specs=[pl.BlockSpec(memory_space=pl.ANY)], out_specs=pl.BlockSpec(memory_space=pl.ANY),
        scratch_shapes=[pltpu.SemaphoreType.DMA((7,)), pltpu.SemaphoreType.DMA((7,)), pltpu.SemaphoreType.DMA],
    )(x)


class _RelayGather:
    ZERO_ROWS = 128

    def __init__(self, x):
        self.x = x
        self.half = x.shape[1] // 2
        self.out_shape = jax.ShapeDtypeStruct((N_DEV + 1,) + x.shape, x.dtype)
        self.scratch = [pltpu.SemaphoreType.DMA((8,)), pltpu.SemaphoreType.DMA((8,)), pltpu.SemaphoreType.DMA,
                        pltpu.VMEM((self.ZERO_ROWS, x.shape[1]), x.dtype), pltpu.SemaphoreType.DMA]

    def _zero_tail(self, out_ref, zero_buf, zero_sem):
        return pltpu.make_async_copy(zero_buf, out_ref.at[N_DEV, pl.ds(0, self.ZERO_ROWS), :], zero_sem)

    def _parts(self, x_ref, out_ref, send_sems, recv_sems, local_sem, zero_buf, zero_sem):
        px, py, pc = _position()
        me, sib = (px, py, pc), (px, py, 1 - pc)
        xn, yn, dg = (1 - px, py, pc), (px, 1 - py, pc), (1 - px, 1 - py, pc)
        half = self.half

        def slot(owner, cols=None):
            ref = out_ref.at[4 * owner[0] + 2 * owner[1] + owner[2]]
            return ref if cols is None else ref.at[:, pl.ds(cols, half)]

        def copy(k, owner, to, src=None, cols=None):
            return pltpu.make_async_remote_copy(
                src_ref=slot(owner, cols) if src is None else src, dst_ref=slot(owner, cols),
                send_sem=send_sems.at[k], recv_sem=recv_sems.at[k], device_id=to, device_id_type=MESH)

        mine = pltpu.make_async_copy(x_ref, slot(me), local_sem)
        own = [copy(1, me, xn, src=x_ref), copy(2, me, yn, src=x_ref), copy(0, me, sib, src=x_ref)]
        return me, sib, xn, yn, dg, copy, mine, own

    def start(self, *refs):
        *_, mine, own = self._parts(*refs)
        mine.start()
        for cp in own:
            cp.start()
        refs[5][...] = jnp.zeros_like(refs[5])
        self._zero_tail(refs[1], refs[5], refs[6]).start()

    def finish(self, *refs):
        me, sib, xn, yn, dg, copy, mine, own = self._parts(*refs)
        flip = lambda q: (q[0], q[1], 1 - q[2])
        copy(1, xn, me).wait_recv()
        relay_x = [copy(3, xn, yn, cols=0), copy(5, xn, sib)]
        for cp in relay_x:
            cp.start()
        copy(2, yn, me).wait_recv()
        relay_y = [copy(4, yn, xn, cols=self.half), copy(6, yn, sib)]
        for cp in relay_y:
            cp.start()
        copy(3, dg, me, cols=0).wait_recv()
        copy(4, dg, me, cols=self.half).wait_recv()
        relay_d = copy(7, dg, sib)
        relay_d.start()
        copy(0, sib, me).wait_recv()
        copy(5, flip(xn), me).wait_recv()
        copy(6, flip(yn), me).wait_recv()
        copy(7, flip(dg), me).wait_recv()
        for cp in own + relay_x + relay_y + [relay_d]:
            cp.wait_send()
        mine.wait()
        self._zero_tail(refs[1], refs[5], refs[6]).wait()


DW_PIECES = ((0, 256), (256, 640), (896, 1152))


def _local_step(ctx, x, target, modp, lower, wt_u, w_o, w9, conv_b, gate_b, hgw, mlw, ln_g, ln_b, exchange):
    tc = ctx.shape[0]
    tt = tc + x.shape[0]
    nbc, ncc = tc // ROWS, tc // CHUNK
    lower_f, lower_b = lower[0:1], lower[1:2]

    tmh = _pick(tt, (1088, 768, 512, 256))
    if exchange:
        hc, wt = _modulate_fwd(ctx, x, modp, gather=_RelayGather(wt_u))
        wt_u = wt.reshape(-1, D_MODEL)
        u, w_o = _mm(hc, wt_u, "nt", F32, tmh, 1152, D_MODEL, "mm_u", ride=_Ride("gather", w_o), n_rows=N_U)
        w_o = w_o.reshape(D_MODEL, D_MODEL)
    else:
        hc = _modulate_fwd(ctx, x, modp)
        u = _mm(hc, wt_u, "nt", F32, tmh, 1152, D_MODEL, "mm_u")
    cpre = _conv_fwd(u, w9, conv_b, tc)
    bias = jnp.pad(gate_b.reshape(1, 16), ((0, 0), (0, LANE - 16)))

    o_f, hist_f = _hgrn_fwd(u, lower_f, ncc, False)
    o_b, hist_b = _hgrn_fwd(u, lower_b, ncc, True)
    h_f, ch_f, nh_f, mh_f = _mlstm_fwd(cpre, u, bias, ncc, False)
    h_b, ch_b, nh_b, mh_b = _mlstm_fwd(cpre, u, bias, ncc, True)
    y = _post_fwd(o_f, o_b, h_f, h_b, u, hgw, mlw, nbc)
    dz, dxa, fsum = _final(y, w_o, x, target, modp, ln_g, ln_b)

    dw_o = _mm(y, dz, "tn", BF16, D_MODEL, 1024, _pick(y.shape[0], (1024, 512, 256)), "mm_dwo")
    do, dhm, daz, dbo, psum = _post_bwd(dz, w_o, o_f, o_b, h_f, h_b, u, hgw, mlw, nbc)
    if exchange:
        dzf_f, dzq, dv_a, dlb_f, dw_o = _hgrn_bwd(
            u, lower_f, hist_f, do, ncc, False, ride=_Ride("a2a", dw_o.reshape(N_DEV, D_MODEL // N_DEV, D_MODEL)))
    else:
        dzf_f, dzq, dv_a, dlb_f = _hgrn_bwd(u, lower_f, hist_f, do, ncc, False)
    du, dlb_b = _hgrn_bwd(u, lower_b, hist_b, do, ncc, True, final=(dzq, dv_a, dzf_f, daz))
    dqk, dv_m, dg, _ = _mlstm_bwd(cpre, u, bias, ch_f, nh_f, mh_f, h_f, dhm, ncc, False)
    dqk, du, gsum = _mlstm_bwd(cpre, u, bias, ch_b, nh_b, mh_b, h_b, dhm, ncc, True, final=(dqk, dv_m, dg, dbo, du))
    du, gconvw, gconvb = _conv_bwd(dqk, u, w9, tc, du)
    tkw = _pick(tt, (2176, 768, 512, 256))
    blocks = lambda g: g.reshape(N_DEV, N_IN // N_DEV, g.shape[1])
    dwu = lambda name, cols, ride: _mm(du, hc, "tn", BF16, 1152, cols[1], tkw, name, b_cols=cols, ride=ride, m_out=N_IN)
    dwt_a = dwu("mm_dwu_a", DW_PIECES[0], None)
    if exchange:
        whole = lambda piece, into: _Ride("a2a", blocks(piece[1]), cols=(piece[0][0], D_MODEL), into=into)
        dwt_b, got = dwu("mm_dwu_b", DW_PIECES[1], whole((DW_PIECES[0], dwt_a), None))
        dwt_c, got = dwu("mm_dwu_c", DW_PIECES[2], whole((DW_PIECES[1], dwt_b), got))
        dh, dwt_u = _mm(du, wt_u, "nn", F32, tmh, D_MODEL // 2, 3456, "mm_dh", ride=whole((DW_PIECES[2], dwt_c), got))
    else:
        dwt_u = jnp.concatenate([dwt_a, dwu("mm_dwu_b", DW_PIECES[1], None), dwu("mm_dwu_c", DW_PIECES[2], None)], axis=1)
        dh = _mm(du, wt_u, "nn", F32, tmh, D_MODEL // 2, 3456, "mm_dh")
    gx, msum = _modulate_bwd(dh, ctx, x, modp, dxa)

    zero_row = jnp.zeros((1, D_MODEL), F32)
    small = dict(
        dmodx=jnp.concatenate([msum[2:3], msum[3:4], fsum[0:1]], axis=0),
        dmodc=jnp.concatenate([msum[0:1], msum[1:2], zero_row], axis=0),
        gconvw=gconvw, gconvb=gconvb, dlower=jnp.concatenate([dlb_f, dlb_b], axis=0),
        ghgw=psum[0:1], gmlw=psum[1:2], glng=fsum[1:2], glnb=fsum[2:3], losssq=fsum[3:4],
        ggate=jnp.concatenate([gsum, jnp.zeros((7, LANE), F32)], axis=0))
    return gx, dwt_u, dw_o, small


def _pack_small(small):
    return jnp.concatenate([small[name].reshape(rows, LANE) for name, rows in _PACK], axis=0)


def _flat_pad(a, rows):
    flat = a.reshape(-1)
    return jnp.pad(flat, (0, rows * LANE - flat.shape[0])).reshape(rows, LANE)


def kernel(x, c, ctx, c_ctx, w_mod, b_mod, w_in, conv_w, conv_b, hg_lb, ml_gate_b, hg_norm_w, ml_norm_w, w_out, ln_g, ln_b, loss_target, m_c_ctx, m_w_mod, m_b_mod, m_w_in, m_conv_w, m_conv_b, m_hg_lb, m_ml_gate_b, m_hg_norm_w, m_ml_norm_w, m_w_out, m_ln_g, m_ln_b, v_c_ctx, v_w_mod, v_b_mod, v_w_in, v_conv_w, v_conv_b, v_hg_lb, v_ml_gate_b, v_hg_norm_w, v_ml_norm_w, v_w_out, v_ln_g, v_ln_b):
    px, py, pc = _position()
    me = 4 * px + 2 * py + pc
    d = D_MODEL
    n_mod = w_mod.shape[2]
    n_cv = conv_w.shape[3]
    n_lb = hg_lb.shape[2]

    pack0 = jnp.concatenate([c.reshape(-1), conv_w.reshape(-1), hg_lb.reshape(-1)]).reshape(1, -1)
    g0 = _all_gather(pack0, "gather_small_inputs")[:, 0, :]
    c_all = g0[:, :d]
    w9 = jnp.transpose(g0[:, d:d + 9 * n_cv].reshape(N_DEV, 9, n_cv), (1, 0, 2)).reshape(9, N_DEV * n_cv)
    lb4 = jnp.transpose(g0[:, d + 9 * n_cv:].reshape(N_DEV, 4, n_lb), (1, 0, 2)).reshape(4, N_DEV * n_lb)
    lower = _lower_fwd(lb4)

    cs = jnp.concatenate([c_all, c_ctx.reshape(1, d), jnp.zeros((7, d), F32)], axis=0)
    b_cols = lax.dynamic_slice(b_mod, (0, me * n_mod), (1, n_mod))
    slab = _mod_fwd(cs, w_mod[0], b_cols)
    mod_all = jnp.transpose(_all_gather(slab, "gather_mod"), (1, 0, 2)).reshape(16, N_DEV * n_mod)
    mod_x = lax.dynamic_slice(mod_all, (me, 0), (1, 3 * d)).reshape(3, d)
    modp = jnp.stack([mod_all[8].reshape(3, d), mod_x])

    gx, recv_wi, recv_wo, small = _local_step(ctx[0], x[0], loss_target[0], modp, lower, w_in[0].T.astype(BF16),
                                              w_out[0].astype(BF16), w9, conv_b, ml_gate_b[0], hg_norm_w, ml_norm_w,
                                              ln_g, ln_b, True)
    g_wi, d_wi, nm_wi, nv_wi = [a.T for a in _rs_adamw(recv_wi, w_in[0].T, m_w_in[0].T, v_w_in[0].T, 256,
                                                       "adamw_w_in", by_cols=True)]
    g_wo, d_wo, nm_wo, nv_wo = _rs_adamw(recv_wo, w_out[0], m_w_out[0], v_w_out[0], 64, "adamw_w_out")

    packs = _all_gather(_pack_small(small), "gather_small_grads")
    total = _reduce8(packs, "reduce_small_grads")
    offs = _pack_offsets()
    piece = lambda name: total[offs[name][0]:offs[name][0] + offs[name][1]]
    g_bmod, g_lb0, g_lb1, loss8 = _small_finish(total, lower[0:2].reshape(16, LANE), float(d))

    ox = offs["dmodx"][0]
    dmodx_all = packs[:, ox:ox + 48, :].reshape(N_DEV, 3 * d)
    dmodc_tot = piece("dmodc").reshape(1, 3 * d)
    d9 = jnp.concatenate([dmodx_all, dmodc_tot, jnp.zeros((7, 3 * d), F32)], axis=0)
    d9_cols = lax.dynamic_slice(d9, (0, me * n_mod), (16, n_mod))
    g_wmod, pc_part = _mod_bwd_w(cs, d9_cols, w_mod[0])
    c_ctx8 = jnp.concatenate([c_ctx.reshape(1, d), jnp.zeros((7, d), F32)], axis=0)
    g_cctx = _cctx_grad(_all_gather(pc_part, "gather_cctx"), c_ctx8)[0]
    d_wmod, nm_wmod, nv_wmod = _adamw(w_mod[0], g_wmod, m_w_mod[0], v_w_mod[0], 256, "adamw_w_mod")

    g_convw_full = piece("gconvw").reshape(9, d)
    g_convw = lax.dynamic_slice(g_convw_full, (0, me * n_cv), (9, n_cv)).reshape(conv_w.shape)
    lb_full = jnp.stack([jnp.stack([g_lb0[0:8].reshape(-1), g_lb1[0:8].reshape(-1)]),
                         jnp.stack([g_lb0[8:16].reshape(-1), g_lb1[8:16].reshape(-1)])])
    g_hglb = lax.dynamic_slice(lb_full, (0, 0, me * n_lb), (2, 2, n_lb))
    grads = dict(
        c_ctx=g_cctx, b_mod=g_bmod.reshape(b_mod.shape), conv_w=g_convw, conv_b=piece("gconvb").reshape(conv_b.shape),
        hg_lb=g_hglb, ml_gate_b=piece("ggate")[0, :16].reshape(ml_gate_b.shape),
        hg_norm_w=piece("ghgw").reshape(hg_norm_w.shape), ml_norm_w=piece("gmlw").reshape(ml_norm_w.shape),
        ln_g=piece("glng").reshape(ln_g.shape), ln_b=piece("glnb").reshape(ln_b.shape))
    params = dict(c_ctx=(c_ctx, m_c_ctx, v_c_ctx), b_mod=(b_mod, m_b_mod, v_b_mod), conv_w=(conv_w, m_conv_w, v_conv_w),
                  conv_b=(conv_b, m_conv_b, v_conv_b), hg_lb=(hg_lb, m_hg_lb, v_hg_lb),
                  ml_gate_b=(ml_gate_b, m_ml_gate_b, v_ml_gate_b), hg_norm_w=(hg_norm_w, m_hg_norm_w, v_hg_norm_w),
                  ml_norm_w=(ml_norm_w, m_ml_norm_w, v_ml_norm_w), ln_g=(ln_g, m_ln_g, v_ln_g), ln_b=(ln_b, m_ln_b, v_ln_b))
    names = list(params)
    rows_of = {n: -(-params[n][0].size // LANE) for n in names}
    rows_tot = -(-sum(rows_of.values()) // 8) * 8
    cat = lambda arrs: jnp.concatenate(
        [_flat_pad(a, rows_of[n]) for n, a in zip(names, arrs)]
        + [jnp.ones((rows_tot - sum(rows_of.values()), LANE), F32)], axis=0)
    d_s, m_s, v_s = _adamw(cat([params[n][0] for n in names]), cat([grads[n] for n in names]),
                           cat([params[n][1] for n in names]), cat([params[n][2] for n in names]), rows_tot, "adamw_small")
    delta, new_m, new_v, off = {}, {}, {}, 0
    for n in names:
        shape, size = params[n][0].shape, params[n][0].size
        take = lambda a: a[off:off + rows_of[n]].reshape(-1)[:size].reshape(shape)
        delta[n], new_m[n], new_v[n] = take(d_s), take(m_s), take(v_s)
        off += rows_of[n]
    grads.update(w_mod=g_wmod[None], w_in=g_wi[None], w_out=g_wo[None])
    delta.update(w_mod=d_wmod[None], w_in=d_wi[None], w_out=d_wo[None])
    new_m.update(w_mod=nm_wmod[None], w_in=nm_wi[None], w_out=nm_wo[None])
    new_v.update(w_mod=nv_wmod[None], w_in=nv_wi[None], w_out=nv_wo[None])

    order = ("c_ctx", "w_mod", "b_mod", "w_in", "conv_w", "conv_b", "hg_lb", "ml_gate_b", "hg_norm_w", "ml_norm_w",
             "w_out", "ln_g", "ln_b")
    return (loss8[0, 0], gx[None], *[grads[n] for n in order], *[delta[n] for n in order],
            *[new_m[n] for n in order], *[new_v[n] for n in order])
```

```python
import jax
import jax.numpy as jnp
from jax import lax
from jax.experimental import pallas as pl
from jax.experimental.pallas import tpu as pltpu

F32 = jnp.float32
BF16 = jnp.bfloat16

D_MODEL = 2048
W_A = 1024
W_B = 1024
HG_HEADS = 8
HG_D = 128
ML_HEADS = 4
ML_D = 256
CHUNK = 64
N_IN = 10256
LANE = 128
N_U = 81 * LANE
N_DEV = 8
ALPHA = 2.0 ** 0.25
LN_EPS = 1e-5
NORM_EPS = 1e-6
ADAM_LR, ADAM_B1, ADAM_B2, ADAM_EPS, ADAM_WD, ADAM_STEP = 0.001, 0.9, 0.999, 1e-08, 0.01, 10
VMEM_CAP = 60 * 1024 * 1024

SEG_AQ, SEG_AFF, SEG_AFB, SEG_AI, SEG_AZ = range(5)
BLK_QK = 40
SEG_BV, SEG_BO, SEG_BZ = 7, 8, 9
BLK_GATE = 80

MESH = pl.DeviceIdType.MESH


def _vmem(nbytes):
    return pltpu.CompilerParams(vmem_limit_bytes=int(min(VMEM_CAP, max(nbytes, 16 * 1024 * 1024))))


def _sigmoid(x):
    return 1.0 / (1.0 + jnp.exp(-x))


def _silu(x):
    return x * _sigmoid(x)


def _dsilu(x):
    s = _sigmoid(x)
    return s * (1.0 + x * (1.0 - s))


def _silu_both(x):
    s = _sigmoid(x)
    return x * s, s * (1.0 + x * (1.0 - s))


def _bdot(a, b, dims):
    return lax.dot_general(a.astype(BF16), b.astype(BF16), (dims, ((), ())), preferred_element_type=F32)


def _nn(a, b):
    return _bdot(a, b, ((1,), (0,)))


def _nt(a, b):
    return _bdot(a, b, ((1,), (1,)))


def _tn(a, b):
    return _bdot(a, b, ((0,), (0,)))


def _exact_nn(a, b):
    return lax.dot_general(a, b, (((1,), (0,)), ((), ())), precision=lax.Precision.HIGHEST,
                           preferred_element_type=F32)


def _exact_tn(a, b):
    return lax.dot_general(a, b, (((0,), (0,)), ((), ())), precision=lax.Precision.HIGHEST,
                           preferred_element_type=F32)


def _tri(rev):
    t = lax.broadcasted_iota(jnp.int32, (CHUNK, CHUNK), 0)
    s = lax.broadcasted_iota(jnp.int32, (CHUNK, CHUNK), 1)
    return (s >= t) if rev else (s <= t)


def _eye():
    t = lax.broadcasted_iota(jnp.int32, (CHUNK, CHUNK), 0)
    s = lax.broadcasted_iota(jnp.int32, (CHUNK, CHUNK), 1)
    return (s == t).astype(F32)


def _row_to_col(row):
    return jnp.sum(_eye() * row, axis=1, keepdims=True)


def _last_onehot(rev):
    t = lax.broadcasted_iota(jnp.int32, (CHUNK, 1), 0)
    return (t == (0 if rev else CHUNK - 1)).astype(F32)


def _head_slices(width, n_heads):
    hd = width // n_heads
    return [slice(h * hd, (h + 1) * hd) for h in range(n_heads)]


def _scan_sum(x, rev):
    n = x.shape[0]
    t = lax.broadcasted_iota(jnp.int32, x.shape, 0)
    s = 1
    while s < n:
        if rev:
            x = x + jnp.where(t < n - s, pltpu.roll(x, n - s, 0), 0.0)
        else:
            x = x + jnp.where(t >= s, pltpu.roll(x, s, 0), 0.0)
        s *= 2
    return x


def _dot3(a, b, dims):
    a_hi, b_hi = a.astype(BF16), b.astype(BF16)
    a_lo, b_lo = (a - a_hi.astype(F32)).astype(BF16), (b - b_hi.astype(F32)).astype(BF16)
    dot = lambda x, y: lax.dot_general(x, y, (dims, ((), ())), preferred_element_type=F32)
    return dot(a_hi, b_hi) + (dot(a_hi, b_lo) + dot(a_lo, b_hi))


def _hg_common(zq, zf, lb, rev):
    q, dq_dz = _silu_both(zq)
    sg = _sigmoid(zf)
    f = lb + (1.0 - lb) * sg
    g = jnp.log(f)
    k = 1.0 - f
    b = _scan_sum(g, rev)
    b_last = jnp.sum(g, axis=0, keepdims=True)
    r = b[CHUNK // 2:CHUNK // 2 + 1, :]
    e_up = jnp.exp(b - r)
    e_dn = jnp.exp(r - b)
    e_b = e_up * jnp.exp(r)
    e_lb = e_dn * jnp.exp(b_last - r)
    return dict(q=q, dq_dz=dq_dz, sg=sg, f=f, k=k, e_up=e_up, e_dn=e_dn, e_b=e_b, e_lb=e_lb, e_last=jnp.exp(b_last),
                q_t=q * e_up, k_t=k * e_dn, q_s=q * e_b, k_h=k * e_lb, tri=_tri(rev).astype(F32))


def hg_chunk_fwd(zq, zf, v, lb, st, rev):
    c = _hg_common(zq, zf, lb, rev)
    hs = _head_slices(zq.shape[1], zq.shape[1] // HG_D)
    s = [_nt(c["q_t"][:, sl], c["k_t"][:, sl]) for sl in hs]
    oi = [_nt(c["q_s"][:, sl], st[sl, :]) for sl in hs]
    ds = [_tn(v[:, sl], c["k_h"][:, sl]) for sl in hs]
    oa = [_nn(c["tri"] * s_h, v[:, sl]) for s_h, sl in zip(s, hs)]
    o = jnp.concatenate([x + y for x, y in zip(oi, oa)], axis=1)
    st_new = jnp.concatenate([st[sl, :] * c["e_last"][:, sl] + d for sl, d in zip(hs, ds)], axis=0)
    return o, st_new


def hg_chunk_bwd(zq, zf, v, lb, st, do, dst_new, rev):
    c = _hg_common(zq, zf, lb, rev)
    hs = _head_slices(zq.shape[1], zq.shape[1] // HG_D)
    tri, q_t, k_t, q_s, k_h = c["tri"], c["q_t"], c["k_t"], c["q_s"], c["k_h"]
    s = [_nt(q_t[:, sl], k_t[:, sl]) for sl in hs]
    da = [tri * _nt(do[:, sl], v[:, sl]) for sl in hs]
    dq_s = [_nn(do[:, sl], st[sl, :]) for sl in hs]
    dk_h = [_nn(v[:, sl], dst_new[sl, :]) for sl in hs]
    dv_s = [_nt(k_h[:, sl], dst_new[sl, :]) for sl in hs]
    dst_q = [_tn(do[:, sl], q_s[:, sl]) for sl in hs]
    dq_t = [_dot3(da_h, k_t[:, sl], ((1,), (0,))) for da_h, sl in zip(da, hs)]
    dk_t = [_dot3(da_h, q_t[:, sl], ((0,), (0,))) for da_h, sl in zip(da, hs)]
    dv_a = [_tn(tri * s_h, do[:, sl]) for s_h, sl in zip(s, hs)]
    cat = lambda parts: jnp.concatenate(parts, axis=1)
    dq_s, dk_h, dq_t, dk_t = cat(dq_s), cat(dk_h), cat(dq_t), cat(dk_t)
    dv = cat([x + y for x, y in zip(dv_a, dv_s)])
    dst = jnp.concatenate([dst_new[sl, :] * c["e_last"][:, sl] + d for sl, d in zip(hs, dst_q)], axis=0)
    dq = dq_s * c["e_b"] + dq_t * c["e_up"]
    dk = dk_t * c["e_dn"] + dk_h * c["e_lb"]
    db = c["q"] * dq - c["k"] * dk
    ss = cat([jnp.sum(dst_new[sl, :] * st[sl, :], axis=0, keepdims=True) for sl in hs])
    d_all = jnp.sum(dk_h * k_h, axis=0, keepdims=True) + c["e_last"] * ss
    dg = _scan_sum(db, not rev) + d_all
    dzq = dq * c["dq_dz"]
    df = dg / c["f"] - dk
    dzf = df * (1.0 - lb) * c["sg"] * (1.0 - c["sg"])
    dlb = jnp.sum(df * (1.0 - c["sg"]), axis=0, keepdims=True)
    return dzq, dzf, dv, dlb, dst


def _log_sigmoid(x):
    return jnp.minimum(x, 0.0) - jnp.log(1.0 + jnp.exp(-jnp.abs(x)))


def _each(fn, *lists):
    return [fn(*xs) for xs in zip(*lists)]


def _bf(xs):
    return [x.astype(BF16) for x in xs]


def _ml_forward_parts(qp, kp, v, gates, c, n, m, rev, with_num):
    hs = _head_slices(qp.shape[1], qp.shape[1] // ML_D)
    q_all, dq_dp = _silu_both(qp)
    k_all, dk_dp = _silu_both(kp)
    k_all = k_all * (ML_D ** -0.5)
    q = [q_all[:, sl] for sl in hs]
    k = [k_all[:, sl] for sl in hs]
    vv = [v[:, sl] for sl in hs]
    cc = [c[sl, :] for sl in hs]
    tri_b = _tri(rev)
    tri = tri_b.astype(F32)
    tri_t = _tri(not rev).astype(F32)
    e_last = _last_onehot(rev)
    qb, kb, vb, cb = _bf(q), _bf(k), _bf(vv), _bf(cc)
    qk = _each(_nt, qb, kb)
    parts = []
    for (gi_c, gi_r, gf_c, gf_r), m_h in zip(gates, m):
        lf_c, lf_r = _log_sigmoid(gf_c), _log_sigmoid(gf_r)
        b_c = jnp.sum(tri * lf_r, axis=1, keepdims=True)
        b_r = jnp.sum(tri_t * lf_c, axis=0, keepdims=True)
        log_w = jnp.where(tri_b, b_c - b_r + gi_r, -jnp.inf)
        m_inter = b_c + m_h
        m_t = jnp.maximum(m_inter, jnp.max(log_w, axis=1, keepdims=True))
        m_new = jnp.sum(m_t * e_last, axis=0, keepdims=True)
        b_last = jnp.sum(b_c * e_last, axis=0, keepdims=True)
        parts.append(dict(a=jnp.exp(m_inter - m_t), p=jnp.exp(log_w - m_t), floor=jnp.exp(-m_t), m_new=m_new,
                          ws=jnp.exp(b_last - b_c + gi_c - m_new), decay=jnp.exp(b_last + m_h - m_new), gf_c=gf_c))
    w = [pt["p"] * x for pt, x in zip(parts, qk)]
    wb = _bf(w)
    for pt, q_h, n_h, w_h in zip(parts, q, n, w):
        qn = jnp.sum(q_h * n_h, axis=1, keepdims=True)
        den = pt["a"] * qn + jnp.sum(w_h, axis=1, keepdims=True)
        pt.update(qn=qn, den=den, rinv=1.0 / jnp.maximum(jnp.abs(den), pt["floor"]), w=w_h)
    if with_num:
        qc = _each(_nt, qb, cb)
        wv = _each(_nn, wb, vb)
        for pt, qc_h, wv_h in zip(parts, qc, wv):
            pt.update(num=pt["a"] * qc_h + wv_h)
    return hs, q, k, vv, cc, tri, parts, dict(q=qb, k=kb, v=vb, c=cb, w=wb, dq_dp=dq_dp, dk_dp=dk_dp)


def ml_chunk_fwd(qp, kp, v, gates, c, n, m, rev):
    hs, q, k, vv, cc, tri, parts, bf = _ml_forward_parts(qp, kp, v, gates, c, n, m, rev, True)
    h = jnp.concatenate([pt["num"] * pt["rinv"] for pt in parts], axis=1)
    upd = _each(_tn, [pt["ws"] * v_h for pt, v_h in zip(parts, vv)], bf["k"])
    c_new = jnp.concatenate([pt["decay"] * c_h + u for pt, c_h, u in zip(parts, cc, upd)], axis=0)
    n_new = [pt["decay"] * n_h + jnp.sum(pt["ws"] * k_h, axis=0, keepdims=True) for pt, n_h, k_h in zip(parts, n, k)]
    return h, c_new, n_new, [pt["m_new"] for pt in parts]


def ml_chunk_bwd(qp, kp, v, gates, c, n, m, h_out, dh, dc_new, dn_new, rev):
    hs, q, k, vv, cc, tri, parts, bf = _ml_forward_parts(qp, kp, v, gates, c, n, m, rev, False)
    dcn = [dc_new[sl, :] for sl in hs]
    dcb = _bf(dcn)
    dnum, dden = [], []
    for pt, sl in zip(parts, hs):
        dh_h = dh[:, sl]
        signed_live = jnp.where(jnp.abs(pt["den"]) > pt["floor"], jnp.where(pt["den"] >= 0.0, 1.0, -1.0), 0.0)
        dnum.append(dh_h * pt["rinv"])
        dden.append(-jnp.sum(dh_h * h_out[:, sl], axis=1, keepdims=True) * pt["rinv"] * signed_live)
    dnb = _bf(dnum)
    dw = [x + y for x, y in zip(_each(_nt, dnb, bf["v"]), dden)]
    kdc = _each(_nt, bf["k"], dcb)
    vdc = _each(_nn, bf["v"], dcb)
    dqk = [x * pt["p"] for x, pt in zip(dw, parts)]
    adn = [pt["a"] * x for pt, x in zip(parts, dnum)]
    dqkb, adnb = _bf(dqk), _bf(adn)
    dv_w = _each(_tn, bf["w"], dnb)
    dq_k = _each(_nn, dqkb, bf["k"])
    dq_c = _each(_nn, adnb, bf["c"])
    dk_q = _each(_tn, dqkb, bf["q"])
    dc_q = _each(_tn, adnb, bf["q"])
    dq, dk, dv, dgi, dgf, dc, dn = [], [], [], [], [], [], []
    for i, pt in enumerate(parts):
        a, ws, decay = pt["a"], pt["ws"], pt["decay"]
        add = a * dden[i]
        e = dw[i] * pt["w"]
        dv.append(dv_w[i] + ws * kdc[i])
        dq.append(dq_k[i] + dq_c[i] + add * n[i])
        dk.append(dk_q[i] + ws * vdc[i] + ws * dn_new[i])
        alpha = jnp.sum(q[i] * dq_c[i], axis=1, keepdims=True) + dden[i] * pt["qn"] * a
        omega = (jnp.sum(vdc[i] * k[i], axis=1, keepdims=True) + jnp.sum(k[i] * dn_new[i], axis=1, keepdims=True)) * ws
        delta = decay * (jnp.sum(jnp.sum(dcn[i] * cc[i], axis=1, keepdims=True), axis=0, keepdims=True)
                         + jnp.sum(dn_new[i] * n[i], axis=1, keepdims=True))
        dc.append(decay * dcn[i] + dc_q[i])
        dn.append(decay * dn_new[i] + jnp.sum(add * q[i], axis=0, keepdims=True))
        e_rows = jnp.sum(e, axis=1, keepdims=True)
        e_cols = _row_to_col(jnp.sum(e, axis=0, keepdims=True))
        dgi.append(e_cols + omega)
        db = e_rows + alpha - e_cols - omega
        tail = jnp.sum(omega, axis=0, keepdims=True) + delta
        dlf = _row_to_col(jnp.sum(tri * db, axis=0, keepdims=True)) + tail
        dgf.append(dlf * (1.0 - _sigmoid(pt["gf_c"])))
    cat = lambda xs: jnp.concatenate(xs, axis=1)
    dqp = cat(dq) * bf["dq_dp"]
    dkp = cat(dk) * (ML_D ** -0.5) * bf["dk_dp"]
    return dqp, dkp, cat(dv), dgi, dgf, jnp.concatenate(dc, axis=0), dn


def _pick(n, prefs):
    for p in prefs:
        if n % p == 0:
            return p
    raise ValueError(f"no tile for {n} among {prefs}")


def _position():
    return lax.axis_index("x"), lax.axis_index("y"), lax.axis_index("c")


class _Ride:
    def __init__(self, kind, x, cols=None, into=None):
        self.kind, self.x, self.cols, self.into = kind, x, cols, into
        r, c = x.shape[-2:]
        self.out_shape = jax.ShapeDtypeStruct((N_DEV, r, c if cols is None else cols[1]), x.dtype)
        self.width = c

    def _copies(self, x_ref, out_ref, send_sems, recv_sems, local_sem):
        px, py, pc = _position()
        me = 4 * px + 2 * py + pc
        src = (lambda slot: x_ref) if self.kind == "gather" else (lambda slot: x_ref.at[slot])
        dst = ((lambda slot: out_ref.at[slot]) if self.cols is None
               else (lambda slot: out_ref.at[slot, :, pl.ds(self.cols[0], self.width)]))
        mine = pltpu.make_async_copy(src(me), dst(me), local_sem)
        sends, recvs = [], []
        for k, (fx, fy, fc) in enumerate([(1, 0, 0), (0, 1, 0), (1, 1, 0), (1, 0, 1), (0, 1, 1), (1, 1, 1), (0, 0, 1)]):
            qx, qy, qc = (1 - px if fx else px), (1 - py if fy else py), (1 - pc if fc else pc)
            peer = 4 * qx + 2 * qy + qc
            sends.append(pltpu.make_async_remote_copy(
                src_ref=src(peer), dst_ref=dst(me), send_sem=send_sems.at[k], recv_sem=recv_sems.at[k],
                device_id=(qx, qy, qc), device_id_type=MESH))
            recvs.append(pltpu.make_async_remote_copy(
                src_ref=src(me), dst_ref=dst(peer), send_sem=send_sems.at[k], recv_sem=recv_sems.at[k],
                device_id=(qx, qy, qc), device_id_type=MESH))
        return mine, sends, recvs

    def start(self, *refs):
        mine, sends, _ = self._copies(*refs)
        mine.start()
        for cp in sends:
            cp.start()

    def wait(self, *refs):
        mine, sends, recvs = self._copies(*refs)
        for cp in recvs:
            cp.wait_recv()
        for cp in sends:
            cp.wait_send()
        mine.wait()

    def operands(self):
        return [self.x] + ([self.into] if self.into is not None else [])


_RIDE_SCRATCH = [pltpu.SemaphoreType.DMA((7,)), pltpu.SemaphoreType.DMA((7,)), pltpu.SemaphoreType.DMA]
_ANY = pl.BlockSpec(memory_space=pl.ANY)


def _mm(a, b, mode, out_dtype, tm, tn, tk, name, ride=None, b_cols=None, m_out=None, n_rows=None):
    if mode == "nn":
        (m, k), (k2, n) = a.shape, b.shape
    elif mode == "nt":
        (m, k), (n, k2) = a.shape, b.shape
        n = n if n_rows is None else n_rows
    else:
        (k, m), (k2, n) = a.shape, b.shape
    off, n = (0, n) if b_cols is None else b_cols
    assert (k2 >= k if mode == "nn" else k == k2) and m % tm == 0 and n % tn == 0 and k % tk == 0, \
        (a.shape, b.shape, mode, tm, tn, tk)
    assert b_cols is None or (mode != "nt" and off % LANE == 0)
    nk = k // tk
    dims = {"nn": ((1,), (0,)), "nt": ((1,), (1,)), "tn": ((0,), (0,))}[mode]
    a_spec = (pl.BlockSpec((tk, tm), lambda j, i, kk: (kk, i)) if mode == "tn"
              else pl.BlockSpec((tm, tk), lambda j, i, kk: (i, kk)))
    if b_cols is not None:
        b_spec = pl.BlockSpec((pl.Element(tk), pl.Element(tn)),
                              lambda j, i, kk: (pl.multiple_of(kk * tk, LANE), pl.multiple_of(off + j * tn, LANE)))
    elif mode == "nt":
        b_spec = pl.BlockSpec((tn, tk), lambda j, i, kk: (j, kk))
    else:
        b_spec = pl.BlockSpec((tk, tn), lambda j, i, kk: (kk, j))

    grid = (n // tn, m // tm, nk)
    n_ride_in = len(ride.operands()) if ride is not None else 0

    def body(a_ref, b_ref, *rest):
        if ride is not None:
            x_ref = rest[0]
            o_ref, got_ref, acc_ref = rest[n_ride_in:n_ride_in + 3]
            comm = (x_ref, got_ref) + tuple(rest[n_ride_in + 3:])
        else:
            o_ref, acc_ref = rest
        kk = pl.program_id(2)
        step = (pl.program_id(0) * grid[1] + pl.program_id(1)) * nk + kk
        if ride is not None:
            @pl.when(step == 0)
            def _():
                ride.start(*comm)

        part = lax.dot_general(a_ref[...], b_ref[...], (dims, ((), ())), preferred_element_type=F32)
        if nk == 1:
            o_ref[...] = part.astype(o_ref.dtype)
        else:
            @pl.when(kk == 0)
            def _():
                acc_ref[...] = part

            @pl.when(jnp.logical_and(kk > 0, kk < nk - 1))
            def _():
                acc_ref[...] += part

            @pl.when(kk == nk - 1)
            def _():
                o_ref[...] = (acc_ref[...] + part).astype(o_ref.dtype)

        if ride is not None:
            @pl.when(step == grid[0] * grid[1] * nk - 1)
            def _():
                ride.wait(*comm)

    osz = jnp.dtype(out_dtype).itemsize
    need = 2 * (tm * tk * a.dtype.itemsize + tk * tn * b.dtype.itemsize + tm * tn * osz) + tm * tn * 4
    o_spec = pl.BlockSpec((tm, tn), lambda j, i, kk: (i, j))
    o_shape = jax.ShapeDtypeStruct((m if m_out is None else m_out, n), out_dtype)
    extra = ride is not None
    return pl.pallas_call(
        body, name=name, grid=grid,
        in_specs=[a_spec, b_spec] + [_ANY] * n_ride_in,
        out_specs=[o_spec, _ANY] if extra else o_spec,
        out_shape=[o_shape, ride.out_shape] if extra else o_shape,
        scratch_shapes=[pltpu.VMEM((tm, tn) if nk > 1 else (8, LANE), F32)] + (_RIDE_SCRATCH if extra else []),
        input_output_aliases={3: 1} if extra and ride.into is not None else {},
        compiler_params=_vmem(need + (12 << 20)),
    )(a, b, *(ride.operands() if extra else []))


ROWS = 256


def _ln_stats(x):
    mu = jnp.mean(x, axis=-1, keepdims=True)
    xc = x - mu
    var = jnp.mean(xc * xc, axis=-1, keepdims=True)
    rstd = lax.rsqrt(var + LN_EPS)
    return xc * rstd, rstd


def _token_specs(nbc, nbx, d):
    return [pl.BlockSpec((ROWS, d), lambda i: (jnp.minimum(i, nbc - 1), 0)),
            pl.BlockSpec((ROWS, d), lambda i: (jnp.maximum(i - nbc, 0), 0))]


def _tokens(c_ref, x_ref, nbc):
    return jnp.where(pl.program_id(0) < nbc, c_ref[...], x_ref[...])


def _modulate_fwd(ctx, x, modp, gather=None):
    d = x.shape[1]
    nbc, nbx = ctx.shape[0] // ROWS, x.shape[0] // ROWS
    riding = gather is not None

    def body(c_ref, x_ref, mod_ref, *rest):
        if riding:
            comm = (rest[0], rest[2]) + tuple(rest[3:])
            o_ref = rest[1]

            @pl.when(pl.program_id(0) == 0)
            def _():
                gather.start(*comm)
        else:
            o_ref = rest[0]
        n, _ = _ln_stats(_tokens(c_ref, x_ref, nbc))
        o_ref[...] = (n * (1.0 + mod_ref[0, 1:2, :]) + mod_ref[0, 0:1, :]).astype(BF16)
        if riding:
            @pl.when(pl.program_id(0) == nbc + nbx - 1)
            def _():
                gather.finish(*comm)

    o_spec = pl.BlockSpec((ROWS, d), lambda i: (i, 0))
    o_shape = jax.ShapeDtypeStruct((ctx.shape[0] + x.shape[0], d), BF16)
    return pl.pallas_call(
        body, name="modulate_fwd", grid=(nbc + nbx,),
        in_specs=_token_specs(nbc, nbx, d) + [pl.BlockSpec((1, 3, d), lambda i: (jnp.where(i >= nbc, 1, 0), 0, 0))]
        + ([_ANY] if riding else []),
        out_specs=[o_spec, _ANY] if riding else o_spec,
        out_shape=[o_shape, gather.out_shape] if riding else o_shape,
        scratch_shapes=gather.scratch if riding else [],
    )(ctx, x, modp, *([gather.x] if riding else []))


def _modulate_bwd(dh, ctx, x, modp, dxa):
    t, d = x.shape
    nbc, nbx = ctx.shape[0] // ROWS, t // ROWS

    def body(dh_ref, c_ref, x_ref, mod_ref, dxa_ref, gx_ref, sum_ref):
        i = pl.program_id(0)
        n, rstd = _ln_stats(_tokens(c_ref, x_ref, nbc))
        g = dh_ref[...]
        dn = g * (1.0 + mod_ref[0, 1:2, :])
        dx = rstd * (dn - jnp.mean(dn, axis=-1, keepdims=True) - n * jnp.mean(dn * n, axis=-1, keepdims=True))
        gx_ref[...] = dx + dxa_ref[...]
        dshift = jnp.sum(g, axis=0, keepdims=True)
        dscale = jnp.sum(g * n, axis=0, keepdims=True)

        @pl.when(i == 0)
        def _():
            sum_ref[...] = jnp.zeros_like(sum_ref)

        @pl.when(i < nbc)
        def _():
            sum_ref[0:1, :] += dshift
            sum_ref[1:2, :] += dscale

        @pl.when(i >= nbc)
        def _():
            sum_ref[2:3, :] += dshift
            sum_ref[3:4, :] += dscale

    lat = lambda i: (jnp.maximum(i - nbc, 0), 0)
    return pl.pallas_call(
        body, name="modulate_bwd", grid=(nbc + nbx,),
        in_specs=[pl.BlockSpec((ROWS, d), lambda i: (i, 0))] + _token_specs(nbc, nbx, d)
        + [pl.BlockSpec((1, 3, d), lambda i: (jnp.where(i >= nbc, 1, 0), 0, 0)), pl.BlockSpec((ROWS, d), lat)],
        out_specs=[pl.BlockSpec((ROWS, d), lat), pl.BlockSpec((8, d), lambda i: (0, 0))],
        out_shape=[jax.ShapeDtypeStruct((t, d), F32), jax.ShapeDtypeStruct((8, d), F32)],
    )(dh, ctx, x, modp, dxa)


def _post_fwd(o_f, o_b, h_f, h_b, u, hgw, mlw, nbc):
    tt = u.shape[0]
    t = tt - nbc * ROWS

    def body(of_ref, ob_ref, hf_ref, hb_ref, az_ref, bo_ref, bz_ref, hgw_ref, mlw_ref, y_ref):
        o = of_ref[...] + ob_ref[...]
        for sl in _head_slices(W_A, HG_HEADS):
            oh = o[:, sl]
            rs = lax.rsqrt(jnp.mean(oh * oh, axis=-1, keepdims=True) + NORM_EPS)
            y_ref[:, sl] = (oh * rs * hgw_ref[:, sl] * _silu(az_ref[:, sl])).astype(BF16)
        hm = hf_ref[...] + hb_ref[...]
        for sl in _head_slices(W_B, ML_HEADS):
            hh = hm[:, sl]
            mu = jnp.mean(hh, axis=-1, keepdims=True)
            hc = hh - mu
            rstd = lax.rsqrt(jnp.mean(hc * hc, axis=-1, keepdims=True) + NORM_EPS)
            out = hc * rstd * mlw_ref[:, sl] * _sigmoid(bo_ref[:, sl]) * _silu(bz_ref[:, sl])
            y_ref[:, W_A + sl.start:W_A + sl.stop] = out.astype(BF16)

    row = lambda i: (i + nbc, 0)
    seg = lambda s: pl.BlockSpec((ROWS, 1024), lambda i: (i + nbc, s))
    wspec = pl.BlockSpec((1, 1024), lambda i: (0, 0))
    return pl.pallas_call(
        body, name="post_fwd", grid=(t // ROWS,),
        in_specs=[pl.BlockSpec((ROWS, 1024), row)] * 4 + [seg(SEG_AZ), seg(SEG_BO), seg(SEG_BZ), wspec, wspec],
        out_specs=pl.BlockSpec((ROWS, 2048), lambda i: (i, 0)),
        out_shape=jax.ShapeDtypeStruct((t, 2048), BF16),
    )(o_f, o_b, h_f, h_b, u, u, u, hgw, mlw)


def _post_bwd(dz, w_o, o_f, o_b, h_f, h_b, u, hgw, mlw, nbc):
    tt = u.shape[0]
    d = w_o.shape[0]

    def body(dz_ref, w_ref, of_ref, ob_ref, hf_ref, hb_ref, az_ref, bo_ref, bz_ref, hgw_ref, mlw_ref,
             do_ref, dhm_ref, daz_ref, dbo_ref, sum_ref):
        i = pl.program_id(0)
        live = jnp.where(i >= nbc, 1.0, 0.0)
        dy = lax.dot_general(dz_ref[...], w_ref[...], (((1,), (1,)), ((), ())), preferred_element_type=F32) * live

        @pl.when(i == 0)
        def _():
            sum_ref[...] = jnp.zeros_like(sum_ref)

        o = of_ref[...] + ob_ref[...]
        for sl in _head_slices(W_A, HG_HEADS):
            oh = o[:, sl]
            rs = lax.rsqrt(jnp.mean(oh * oh, axis=-1, keepdims=True) + NORM_EPS)
            on = oh * rs
            az = az_ref[:, sl]
            dya = dy[:, sl]
            saz, daz = _silu_both(az)
            doa = dya * saz
            daz_ref[:, sl] = (dya * on * hgw_ref[:, sl] * daz).astype(BF16)
            sum_ref[0:1, sl] += jnp.sum(doa * on, axis=0, keepdims=True)
            don = doa * hgw_ref[:, sl]
            do_ref[:, sl] = rs * (don - on * jnp.mean(don * on, axis=-1, keepdims=True))
        hm = hf_ref[...] + hb_ref[...]
        for sl in _head_slices(W_B, ML_HEADS):
            hh = hm[:, sl]
            mu = jnp.mean(hh, axis=-1, keepdims=True)
            hc = hh - mu
            rstd = lax.rsqrt(jnp.mean(hc * hc, axis=-1, keepdims=True) + NORM_EPS)
            hn = hc * rstd
            hw = hn * mlw_ref[:, sl]
            bo, bz = bo_ref[:, sl], bz_ref[:, sl]
            sbo = _sigmoid(bo)
            sbz, dbz = _silu_both(bz)
            dyb = dy[:, W_A + sl.start:W_A + sl.stop]
            dhw = dyb * sbo * sbz
            dbo_ref[:, sl] = (dyb * hw * sbz * sbo * (1.0 - sbo)).astype(BF16)
            dbo_ref[:, 1024 + sl.start:1024 + sl.stop] = (dyb * hw * sbo * dbz).astype(BF16)
            sum_ref[1:2, sl] += jnp.sum(dhw * hn, axis=0, keepdims=True)
            dhn = dhw * mlw_ref[:, sl]
            dhm_ref[:, sl] = rstd * (dhn - jnp.mean(dhn, axis=-1, keepdims=True)
                                     - hn * jnp.mean(dhn * hn, axis=-1, keepdims=True))

    row = lambda i: (i, 0)
    seg = lambda s: pl.BlockSpec((ROWS, 1024), lambda i: (i, s))
    wspec = pl.BlockSpec((1, 1024), lambda i: (0, 0))
    return pl.pallas_call(
        body, name="post_bwd", grid=(tt // ROWS,),
        in_specs=[pl.BlockSpec((ROWS, 2048), lambda i: (jnp.maximum(i - nbc, 0), 0)), pl.BlockSpec((d, d), lambda i: (0, 0))]
        + [pl.BlockSpec((ROWS, 1024), row)] * 4 + [seg(SEG_AZ), seg(SEG_BO), seg(SEG_BZ), wspec, wspec],
        out_specs=[pl.BlockSpec((ROWS, 1024), row), pl.BlockSpec((ROWS, 1024), row),
                   pl.BlockSpec((ROWS, 1024), row), pl.BlockSpec((ROWS, 2048), row),
                   pl.BlockSpec((8, 1024), lambda i: (0, 0))],
        out_shape=[jax.ShapeDtypeStruct((tt, 1024), F32), jax.ShapeDtypeStruct((tt, 1024), F32),
                   jax.ShapeDtypeStruct((tt, 1024), BF16), jax.ShapeDtypeStruct((tt, 2048), BF16),
                   jax.ShapeDtypeStruct((8, 1024), F32)],
        compiler_params=_vmem(4 * d * d + 30 * ROWS * 2048 * 4),
    )(dz, w_o, o_f, o_b, h_f, h_b, u, u, u, hgw, mlw)


def _final(y, w_o, x, target, modp, ln_g, ln_b):
    t, d = x.shape

    def body(y_ref, w_ref, x_ref, tg_ref, mod_ref, g_ref, b_ref, dz_ref, dxa_ref, sum_ref):
        i = pl.program_id(0)
        zz = lax.dot_general(y_ref[...], w_ref[...], (((1,), (0,)), ((), ())), preferred_element_type=F32)
        gate = mod_ref[0, 2:3, :]
        pre = ALPHA * x_ref[...] + gate * zz
        nh, rstd = _ln_stats(pre)
        err = nh * g_ref[...] + b_ref[...] - tg_ref[...]
        dxo = err * (1.0 / d)
        dnh = dxo * g_ref[...]
        dpre = rstd * (dnh - jnp.mean(dnh, axis=-1, keepdims=True) - nh * jnp.mean(dnh * nh, axis=-1, keepdims=True))
        dz_ref[...] = (gate * dpre).astype(BF16)
        dxa_ref[...] = ALPHA * dpre

        @pl.when(i == 0)
        def _():
            sum_ref[...] = jnp.zeros_like(sum_ref)

        sum_ref[0:1, :] += jnp.sum(dpre * zz, axis=0, keepdims=True)
        sum_ref[1:2, :] += jnp.sum(dxo * nh, axis=0, keepdims=True)
        sum_ref[2:3, :] += jnp.sum(dxo, axis=0, keepdims=True)
        sum_ref[3:4, :] += jnp.sum(err * err, axis=0, keepdims=True)

    row = lambda i: (i, 0)
    vec = pl.BlockSpec((1, d), lambda i: (0, 0))
    return pl.pallas_call(
        body, name="final_ln_loss", grid=(t // ROWS,),
        in_specs=[pl.BlockSpec((ROWS, d), row), pl.BlockSpec((d, d), lambda i: (0, 0)), pl.BlockSpec((ROWS, d), row),
                  pl.BlockSpec((ROWS, d), row), pl.BlockSpec((1, 3, d), lambda i: (1, 0, 0)), vec, vec],
        out_specs=[pl.BlockSpec((ROWS, d), row), pl.BlockSpec((ROWS, d), row), pl.BlockSpec((8, d), lambda i: (0, 0))],
        out_shape=[jax.ShapeDtypeStruct((t, d), BF16), jax.ShapeDtypeStruct((t, d), F32),
                   jax.ShapeDtypeStruct((8, d), F32)],
        compiler_params=_vmem(4 * d * d + 24 * ROWS * d * 4),
    )(y, w_o, x, target, modp, ln_g, ln_b)


GRID_W = 64


def _shift(x, s, ok):
    n = x.shape[0]
    return jnp.where(ok, pltpu.roll(x, s % n, 0), 0.0)


def _grid_masks(n):
    t = lax.broadcasted_iota(jnp.int32, (n, LANE), 0)
    col = t & (GRID_W - 1)
    return dict(left=col >= 1, right=col <= GRID_W - 2, up=t >= GRID_W, down=t < n - GRID_W)


def _seq_masks(n):
    t = lax.broadcasted_iota(jnp.int32, (n, LANE), 0)
    return dict(left=t >= 1, right=t <= n - 2)


def _conv_fwd(u, w9, cb, tc):
    tt = u.shape[0]
    t = tt - tc

    def body(u_ref, w_ref, b_ref, o_ref):
        w = [w_ref[r:r + 1, :] for r in range(9)]
        xc = u_ref[0:tc, :]
        ms = _seq_masks(tc)
        o_ref[0:tc, :] = (w[3] * _shift(xc, 1, ms["left"]) + w[4] * xc + w[5] * _shift(xc, -1, ms["right"])
                          + b_ref[...])
        x = u_ref[tc:tt, :]
        mg = _grid_masks(t)
        taps = (_shift(x, 1, mg["left"]), x, _shift(x, -1, mg["right"]))
        rows = [w[3 * i] * taps[0] + w[3 * i + 1] * taps[1] + w[3 * i + 2] * taps[2] for i in range(3)]
        o_ref[tc:tt, :] = (rows[1] + _shift(rows[0], GRID_W, mg["up"]) + _shift(rows[2], -GRID_W, mg["down"])
                           + b_ref[...])

    return pl.pallas_call(
        body, name="conv_fwd", grid=(2048 // LANE,),
        in_specs=[pl.BlockSpec((tt, LANE), lambda j: (0, BLK_QK + j)), pl.BlockSpec((9, LANE), lambda j: (0, j)),
                  pl.BlockSpec((1, LANE), lambda j: (0, j))],
        out_specs=pl.BlockSpec((tt, LANE), lambda j: (0, j)),
        out_shape=jax.ShapeDtypeStruct((tt, 2048), F32),
        compiler_params=_vmem(40 * tt * LANE * 4),
    )(u, w9, cb)


def _conv_bwd(dcp, u, w9, tc, du):
    tt = u.shape[0]
    t = tt - tc

    def body(d_ref, u_ref, w_ref, du_in_ref, du_ref, gw_ref, gb_ref):
        w = [w_ref[r:r + 1, :] for r in range(9)]
        csum = lambda a: jnp.sum(a, axis=0, keepdims=True)
        dc = d_ref[0:tc, :]
        xc = u_ref[0:tc, :]
        ms = _seq_masks(tc)
        du_ref[0:tc, :] = (w[3] * _shift(dc, -1, ms["right"]) + w[4] * dc + w[5] * _shift(dc, 1, ms["left"])).astype(BF16)
        gmid = [csum(dc * _shift(xc, 1, ms["left"])), csum(dc * xc), csum(dc * _shift(xc, -1, ms["right"]))]
        d = d_ref[tc:tt, :]
        x = u_ref[tc:tt, :]
        mg = _grid_masks(t)
        dtaps = (_shift(d, -1, mg["right"]), d, _shift(d, 1, mg["left"]))
        rows = [w[3 * i] * dtaps[0] + w[3 * i + 1] * dtaps[1] + w[3 * i + 2] * dtaps[2] for i in range(3)]
        du_ref[tc:tt, :] = (rows[1] + _shift(rows[0], -GRID_W, mg["down"]) + _shift(rows[2], GRID_W, mg["up"])).astype(BF16)
        xtaps = (_shift(x, 1, mg["left"]), x, _shift(x, -1, mg["right"]))
        for j in range(3):
            gw_ref[j:j + 1, :] = csum(d * _shift(xtaps[j], GRID_W, mg["up"]))
            gw_ref[3 + j:4 + j, :] = csum(d * xtaps[j]) + gmid[j]
            gw_ref[6 + j:7 + j, :] = csum(d * _shift(xtaps[j], -GRID_W, mg["down"]))
        gb_ref[...] = csum(d) + csum(dc)

    return pl.pallas_call(
        body, name="conv_bwd", grid=(2048 // LANE,),
        in_specs=[pl.BlockSpec((tt, LANE), lambda j: (0, j)), pl.BlockSpec((tt, LANE), lambda j: (0, BLK_QK + j)),
                  pl.BlockSpec((9, LANE), lambda j: (0, j)), _ANY],
        out_specs=[pl.BlockSpec((tt, LANE), lambda j: (0, BLK_QK + j)), pl.BlockSpec((9, LANE), lambda j: (0, j)),
                   pl.BlockSpec((1, LANE), lambda j: (0, j))],
        out_shape=[jax.ShapeDtypeStruct(du.shape, BF16), jax.ShapeDtypeStruct((9, 2048), F32),
                   jax.ShapeDtypeStruct((1, 2048), F32)],
        input_output_aliases={3: 0},
        compiler_params=_vmem(48 * tt * LANE * 4),
    )(dcp, u, w9, du)


SUB = 4
STEP = SUB * CHUNK
ML_SUB = 2
ML_STEP = ML_SUB * CHUNK


def _chunk_of(pos, ncc, nc, rev):
    if not rev:
        return pos
    return jnp.where(pos < ncc, ncc - 1 - pos, nc - 1 - (pos - ncc))


def _sub_rows(rev, sub=SUB):
    order = range(sub - 1, -1, -1) if rev else range(sub)
    return [(s, slice(s * CHUNK, (s + 1) * CHUNK)) for s in order]


def _hgrn_fwd(u, lower_d, ncc, rev):
    tt = u.shape[0]
    nc, ncc = tt // STEP, ncc // SUB
    seg_f = SEG_AFB if rev else SEG_AFF

    def body(zq_ref, zf_ref, v_ref, lb_ref, o_ref, hist_ref, st_ref):
        @pl.when(pl.program_id(0) == 0)
        def _():
            st_ref[...] = jnp.zeros_like(st_ref)

        st = st_ref[...]
        for s, r in _sub_rows(rev):
            hist_ref[s] = st
            o, st = hg_chunk_fwd(zq_ref[r, :], zf_ref[r, :], v_ref[r, :], lb_ref[...], st, rev)
            o_ref[r, :] = o
        st_ref[...] = st

    seg = lambda s: pl.BlockSpec((STEP, 1024), lambda j: (_chunk_of(j, ncc, nc, rev), s))
    return pl.pallas_call(
        body, name="hgrn_fwd_rev" if rev else "hgrn_fwd", grid=(nc,),
        in_specs=[seg(SEG_AQ), seg(seg_f), seg(SEG_AI), pl.BlockSpec((1, 1024), lambda j: (0, 0))],
        out_specs=[pl.BlockSpec((STEP, 1024), lambda j: (_chunk_of(j, ncc, nc, rev), 0)),
                   pl.BlockSpec((SUB, 1024, HG_D), lambda j: (_chunk_of(j, ncc, nc, rev), 0, 0))],
        out_shape=[jax.ShapeDtypeStruct((tt, 1024), F32), jax.ShapeDtypeStruct((nc * SUB, 1024, HG_D), F32)],
        scratch_shapes=[pltpu.VMEM((1024, HG_D), F32)],
    )(u, u, u, lower_d)


def _hgrn_bwd(u, lower_d, hist, do, ncc, rev, ride=None, final=None):
    tt = u.shape[0]
    nc, ncc = tt // STEP, ncc // SUB
    seg_f = SEG_AFB if rev else SEG_AFF
    is_final = final is not None
    has_a2a = ride is not None
    n_out = 2 if is_final else 4
    width = 5 * 1024

    def body(zq_ref, zf_ref, v_ref, lb_ref, hist_ref, do_ref, *rest):
        if is_final:
            aq_ref, av_ref, af_ref, az_ref = rest[:4]
            rest = rest[4:]
        if has_a2a:
            x_ref, rest = rest[0], rest[1:]
        outs, rest = rest[:n_out], rest[n_out:]
        dlb_ref = outs[-1]
        if has_a2a:
            comm = (x_ref, rest[0]) + tuple(rest[2:])
            dst_ref = rest[1]
        else:
            dst_ref = rest[0]

        @pl.when(pl.program_id(0) == 0)
        def _():
            dst_ref[...] = jnp.zeros_like(dst_ref)
            dlb_ref[...] = jnp.zeros_like(dlb_ref)
            if has_a2a:
                ride.start(*comm)

        dst = dst_ref[...]
        dlb_sum = dlb_ref[...]
        for s, r in reversed(_sub_rows(rev)):
            dzq, dzf, dv, dlb, dst = hg_chunk_bwd(zq_ref[r, :], zf_ref[r, :], v_ref[r, :], lb_ref[...],
                                                  hist_ref[s], do_ref[r, :], dst, rev)
            dlb_sum = dlb_sum + dlb
            if is_final:
                du_ref = outs[0]
                dzf_own, dzf_other = dzf.astype(BF16), af_ref[r, :]
                du_ref[r, 0:1024] = (dzq + aq_ref[r, :]).astype(BF16)
                du_ref[r, 1024:2048] = dzf_other if rev else dzf_own
                du_ref[r, 2048:3072] = dzf_own if rev else dzf_other
                du_ref[r, 3072:4096] = (dv + av_ref[r, :]).astype(BF16)
                du_ref[r, 4096:5120] = az_ref[r, :]
            else:
                dzf_ref, dzq_ref, dv_ref = outs[:3]
                dzf_ref[r, :] = dzf.astype(BF16)
                dzq_ref[r, :] = dzq
                dv_ref[r, :] = dv
        dst_ref[...] = dst
        dlb_ref[...] = dlb_sum

        if has_a2a:
            @pl.when(pl.program_id(0) == nc - 1)
            def _():
                ride.wait(*comm)

    cidx = lambda j: _chunk_of(nc - 1 - j, ncc, nc, rev)
    seg = lambda s: pl.BlockSpec((STEP, 1024), lambda j: (cidx(j), s))
    row = pl.BlockSpec((STEP, 1024), lambda j: (cidx(j), 0))
    dlb_spec = pl.BlockSpec((1, 1024), lambda j: (0, 0))
    dlb_shape = jax.ShapeDtypeStruct((1, 1024), F32)
    if is_final:
        out_specs = [pl.BlockSpec((STEP, width), lambda j: (cidx(j), 0)), dlb_spec]
        out_shape = [jax.ShapeDtypeStruct((tt, N_U), BF16), dlb_shape]
    else:
        out_specs = [row, row, row, dlb_spec]
        out_shape = [jax.ShapeDtypeStruct((tt, 1024), BF16), jax.ShapeDtypeStruct((tt, 1024), F32),
                     jax.ShapeDtypeStruct((tt, 1024), F32), dlb_shape]
    ins = [u, u, u, lower_d, hist, do] + (list(final) if is_final else []) + ([ride.x] if has_a2a else [])
    return pl.pallas_call(
        body, name="hgrn_bwd_rev" if rev else "hgrn_bwd", grid=(nc,),
        in_specs=[seg(SEG_AQ), seg(seg_f), seg(SEG_AI), pl.BlockSpec((1, 1024), lambda j: (0, 0)),
                  pl.BlockSpec((SUB, 1024, HG_D), lambda j: (cidx(j), 0, 0)), row] + ([row] * 4 if is_final else [])
        + ([_ANY] if has_a2a else []),
        out_specs=out_specs + ([_ANY] if has_a2a else []),
        out_shape=out_shape + ([ride.out_shape] if has_a2a else []),
        scratch_shapes=[pltpu.VMEM((1024, HG_D), F32)] + (_RIDE_SCRATCH if has_a2a else []),
    )(*ins)


def _gate_views(g_ref, b_ref, r, head, rev):
    gc = g_ref[r, :] + b_ref[...]
    lane = lax.broadcasted_iota(jnp.int32, (1, LANE), 1)
    eye = _eye()
    d = 1 if rev else 0
    ii, fi = d * ML_HEADS + head, 2 * ML_HEADS + d * ML_HEADS + head
    col = lambda idx: jnp.sum(jnp.where(lane == idx, gc, 0.0), axis=1, keepdims=True)
    row = lambda c: jnp.sum(eye * c, axis=0, keepdims=True)
    gi, gf = col(ii), col(fi)
    return gi, row(gi), gf, row(gf)


def _mlstm_fwd(cpre, u, bias, ncc, rev):
    tt = u.shape[0]
    nc, ncc = tt // ML_STEP, ncc // ML_SUB
    nhd = ML_HEADS

    def body(q_ref, k_ref, v_ref, g_ref, b_ref, h_ref, ch_ref, nh_ref, mh_ref, c_ref, n_ref, m_ref):
        @pl.when(pl.program_id(0) == 0)
        def _():
            c_ref[...] = jnp.zeros_like(c_ref)
            n_ref[...] = jnp.zeros_like(n_ref)
            m_ref[...] = jnp.zeros_like(m_ref)

        c, n_all, m_all = c_ref[...], n_ref[...], m_ref[...]
        n = [n_all[hd:hd + 1, :] for hd in range(nhd)]
        m = [m_all[hd:hd + 1, 0:1] for hd in range(nhd)]
        for s, r in _sub_rows(rev, ML_SUB):
            ch_ref[s] = c
            for hd in range(nhd):
                nh_ref[s, hd:hd + 1, :] = n[hd]
                mh_ref[s, hd:hd + 1, :] = jnp.broadcast_to(m[hd], (1, LANE))
            gates = [_gate_views(g_ref, b_ref, r, hd, rev) for hd in range(nhd)]
            h, c, n, m = ml_chunk_fwd(q_ref[r, :], k_ref[r, :], v_ref[r, :], gates, c, n, m, rev)
            h_ref[r, :] = h
        c_ref[...] = c
        for hd in range(nhd):
            n_ref[hd:hd + 1, :] = n[hd]
            m_ref[hd:hd + 1, :] = jnp.broadcast_to(m[hd], (1, LANE))

    cidx = lambda j: _chunk_of(j, ncc, nc, rev)
    row = lambda s: pl.BlockSpec((ML_STEP, 1024), lambda j: (cidx(j), s))
    st3 = lambda a, b: pl.BlockSpec((ML_SUB, a, b), lambda j: (cidx(j), 0, 0))
    return pl.pallas_call(
        body, name="mlstm_fwd_rev" if rev else "mlstm_fwd", grid=(nc,),
        in_specs=[row(0), row(1), row(SEG_BV), pl.BlockSpec((ML_STEP, LANE), lambda j: (cidx(j), BLK_GATE)),
                  pl.BlockSpec((1, LANE), lambda j: (0, 0))],
        out_specs=[row(0), st3(1024, ML_D), st3(8, ML_D), st3(8, LANE)],
        out_shape=[jax.ShapeDtypeStruct((tt, 1024), F32), jax.ShapeDtypeStruct((nc * ML_SUB, 1024, ML_D), F32),
                   jax.ShapeDtypeStruct((nc * ML_SUB, 8, ML_D), F32), jax.ShapeDtypeStruct((nc * ML_SUB, 8, LANE), F32)],
        scratch_shapes=[pltpu.VMEM((1024, ML_D), F32), pltpu.VMEM((8, ML_D), F32), pltpu.VMEM((8, LANE), F32)],
    )(cpre, cpre, u, u, bias)


def _mlstm_bwd(cpre, u, bias, chist, nhist, mhist, h_out, dh, ncc, rev, final=None):
    tt = u.shape[0]
    nc, ncc = tt // ML_STEP, ncc // ML_SUB
    nhd = ML_HEADS
    is_final = final is not None
    d = 1 if rev else 0
    col0, width = SEG_BV * 1024, N_U - SEG_BV * 1024

    def body(q_ref, k_ref, v_ref, g_ref, b_ref, ch_ref, nh_ref, mh_ref, ho_ref, dh_ref, *rest):
        if is_final:
            aqk_ref, av_ref, ag_ref, bo_ref = rest[:4]
            dqk_ref, du_ref, gs_ref, dc_ref, dn_ref = rest[5:]
        else:
            dqk_ref, dv_ref, dg_ref, gs_ref, dc_ref, dn_ref = rest

        @pl.when(pl.program_id(0) == 0)
        def _():
            dc_ref[...] = jnp.zeros_like(dc_ref)
            dn_ref[...] = jnp.zeros_like(dn_ref)
            gs_ref[...] = jnp.zeros_like(gs_ref)

        lane = lax.broadcasted_iota(jnp.int32, (1, LANE), 1)
        dc, dn_all, gs = dc_ref[...], dn_ref[...], gs_ref[...]
        dn = [dn_all[hd:hd + 1, :] for hd in range(nhd)]
        for s, r in reversed(_sub_rows(rev, ML_SUB)):
            gates = [_gate_views(g_ref, b_ref, r, hd, rev) for hd in range(nhd)]
            n_all, m_all = nh_ref[s], mh_ref[s]
            dqp, dkp, dv, dgi, dgf, dc, dn = ml_chunk_bwd(
                q_ref[r, :], k_ref[r, :], v_ref[r, :], gates, ch_ref[s],
                [n_all[hd:hd + 1, :] for hd in range(nhd)], [m_all[hd:hd + 1, 0:1] for hd in range(nhd)],
                ho_ref[r, :], dh_ref[r, :], dc, dn, rev)
            dg = ag_ref[r, :] if is_final else jnp.zeros((CHUNK, LANE), F32)
            for hd in range(nhd):
                dg = dg + jnp.where(lane == d * ML_HEADS + hd, dgi[hd], 0.0)
                dg = dg + jnp.where(lane == 2 * ML_HEADS + d * ML_HEADS + hd, dgf[hd], 0.0)
            if is_final:
                dqp = dqp + aqk_ref[r, 0:W_B]
                dkp = dkp + aqk_ref[r, W_B:2 * W_B]
                du_ref[r, 0:1024] = (dv + av_ref[r, :]).astype(BF16)
                du_ref[r, 1024:3072] = bo_ref[r, :]
                du_ref[r, 3072:3072 + LANE] = dg.astype(BF16)
            else:
                dv_ref[r, :] = dv
                dg_ref[r, :] = dg
            dqk_ref[r, 0:W_B] = dqp
            dqk_ref[r, W_B:2 * W_B] = dkp
            gs = gs + jnp.sum(dg, axis=0, keepdims=True)
        dc_ref[...] = dc
        gs_ref[...] = gs
        for hd in range(nhd):
            dn_ref[hd:hd + 1, :] = dn[hd]

    cidx = lambda j: _chunk_of(nc - 1 - j, ncc, nc, rev)
    row = lambda s: pl.BlockSpec((ML_STEP, 1024), lambda j: (cidx(j), s))
    wide = pl.BlockSpec((ML_STEP, 2048), lambda j: (cidx(j), 0))
    gate = pl.BlockSpec((ML_STEP, LANE), lambda j: (cidx(j), 0))
    st3 = lambda a, b: pl.BlockSpec((ML_SUB, a, b), lambda j: (cidx(j), 0, 0))
    gs_spec, gs_shape = pl.BlockSpec((1, LANE), lambda j: (0, 0)), jax.ShapeDtypeStruct((1, LANE), F32)
    dqk_shape = jax.ShapeDtypeStruct((tt, 2048), F32)
    ins = [cpre, cpre, u, u, bias, chist, nhist, mhist, h_out, dh] + (list(final) if is_final else [])
    if is_final:
        out_specs = [wide, pl.BlockSpec((pl.Element(ML_STEP), pl.Element(width)), lambda j: (cidx(j) * ML_STEP, col0)), gs_spec]
        out_shape = [dqk_shape, jax.ShapeDtypeStruct((tt, N_U), BF16), gs_shape]
    else:
        out_specs = [wide, row(0), gate, gs_spec]
        out_shape = [dqk_shape, jax.ShapeDtypeStruct((tt, 1024), F32), jax.ShapeDtypeStruct((tt, LANE), F32), gs_shape]
    return pl.pallas_call(
        body, name="mlstm_bwd_rev" if rev else "mlstm_bwd", grid=(nc,),
        in_specs=[row(0), row(1), row(SEG_BV), pl.BlockSpec((ML_STEP, LANE), lambda j: (cidx(j), BLK_GATE)),
                  pl.BlockSpec((1, LANE), lambda j: (0, 0)),
                  st3(1024, ML_D), st3(8, ML_D), st3(8, LANE), row(0), row(0)]
        + ([wide, row(0), gate, wide, _ANY] if is_final else []),
        out_specs=out_specs, out_shape=out_shape,
        input_output_aliases={14: 1} if is_final else {},
        scratch_shapes=[pltpu.VMEM((1024, ML_D), F32), pltpu.VMEM((8, ML_D), F32)],
    )(*ins)


def _whole(body, out_shape, name, *args, nbytes=0):
    return pl.pallas_call(body, name=name, out_shape=out_shape, compiler_params=_vmem(nbytes))(*args)


def _mod_fwd(cs, w_cols, b_cols):
    def body(c_ref, w_ref, b_ref, o_ref):
        o_ref[...] = _exact_nn(_silu(c_ref[...]), w_ref[...]) + b_ref[...]

    return _whole(body, jax.ShapeDtypeStruct((16, w_cols.shape[1]), F32), "mod_fwd", cs, w_cols, b_cols,
                  nbytes=4 * w_cols.size * 4)


def _mod_bwd_w(cs, d9, w_cols):
    def body(c_ref, d_ref, w_ref, gw_ref, pc_ref):
        gw_ref[...] = _exact_tn(_silu(c_ref[...]), d_ref[...])
        pc = lax.dot_general(d_ref[8:16, :], w_ref[...], (((1,), (1,)), ((), ())), precision=lax.Precision.HIGHEST,
                             preferred_element_type=F32)
        row = lax.broadcasted_iota(jnp.int32, pc.shape, 0)
        pc_ref[...] = jnp.where(row == 0, pc, 0.0)

    return _whole(body, [jax.ShapeDtypeStruct(w_cols.shape, F32), jax.ShapeDtypeStruct((8, w_cols.shape[0]), F32)],
                  "mod_bwd_w", cs, d9, w_cols, nbytes=6 * w_cols.size * 4)


def _lower_fwd(lb4):
    def body(l_ref, o_ref):
        o_ref[...] = jnp.zeros_like(o_ref)
        o_ref[0:1, :] = 1.0 / (1.0 + jnp.exp(l_ref[1:2, :] - l_ref[0:1, :]))
        o_ref[1:2, :] = 1.0 / (1.0 + jnp.exp(l_ref[3:4, :] - l_ref[2:3, :]))

    return _whole(body, jax.ShapeDtypeStruct((8, lb4.shape[1]), F32), "lower_fwd", lb4)


def _reduce8(g, name):
    def body(g_ref, o_ref):
        acc = g_ref[0]
        for k in range(1, N_DEV):
            acc = acc + g_ref[k]
        o_ref[...] = acc

    return _whole(body, jax.ShapeDtypeStruct(g.shape[1:], F32), name, g, nbytes=4 * g.size * 4)


_PACK = (("dmodx", 48), ("dmodc", 48), ("gconvw", 144), ("gconvb", 16), ("dlower", 16), ("ghgw", 8), ("gmlw", 8),
         ("glng", 16), ("glnb", 16), ("losssq", 16), ("ggate", 8))


def _pack_offsets():
    off, out = 0, {}
    for name, rows in _PACK:
        out[name] = (off, rows)
        off += rows
    return out


def _small_finish(total, p0, d_feat):
    offs = _pack_offsets()

    def body(t_ref, p_ref, gb_ref, a0_ref, a1_ref, loss_ref):
        ox, oc, ol, oq = offs["dmodx"][0], offs["dmodc"][0], offs["dlower"][0], offs["losssq"][0]
        gb_ref[...] = t_ref[ox:ox + 48, :] + t_ref[oc:oc + 48, :]
        p = p_ref[...]
        da0 = t_ref[ol:ol + 16, :] * p * (1.0 - p)
        a0_ref[...] = da0
        a1_ref[...] = -da0
        sq = t_ref[oq:oq + 16, :]
        tot = jnp.sum(jnp.sum(sq, axis=1, keepdims=True), axis=0, keepdims=True)
        loss_ref[...] = jnp.broadcast_to(tot * (0.5 / d_feat), loss_ref.shape)

    s = jax.ShapeDtypeStruct
    return _whole(body, [s((48, LANE), F32), s((16, LANE), F32), s((16, LANE), F32), s((8, LANE), F32)],
                  "small_finish", total, p0)


def _cctx_grad(parts, c_ctx8):
    def body(p_ref, c_ref, o_ref):
        acc = p_ref[0]
        for k in range(1, N_DEV):
            acc = acc + p_ref[k]
        o_ref[...] = acc * _dsilu(c_ref[...])

    return _whole(body, jax.ShapeDtypeStruct(c_ctx8.shape, F32), "cctx_grad", parts, c_ctx8)


def _adam_math(w, g, m, v):
    m = ADAM_B1 * m + (1.0 - ADAM_B1) * g
    v = ADAM_B2 * v + (1.0 - ADAM_B2) * (g * g)
    m_hat = m / (1.0 - ADAM_B1 ** ADAM_STEP)
    v_hat = v / (1.0 - ADAM_B2 ** ADAM_STEP)
    delta = -ADAM_LR * (m_hat / (jnp.sqrt(v_hat) + ADAM_EPS) + ADAM_WD * w)
    return delta, m, v


def _adamw(w, g, m, v, rows, name):
    r, c = w.shape

    def body(w_ref, g_ref, m_ref, v_ref, d_ref, mo_ref, vo_ref):
        d_ref[...], mo_ref[...], vo_ref[...] = _adam_math(w_ref[...], g_ref[...], m_ref[...], v_ref[...])

    spec = pl.BlockSpec((rows, c), lambda i: (i, 0))
    return pl.pallas_call(
        body, name=name, grid=(r // rows,), in_specs=[spec] * 4, out_specs=[spec] * 3,
        out_shape=[jax.ShapeDtypeStruct((r, c), F32)] * 3,
        compiler_params=_vmem(16 * rows * (c + LANE) * 4),
    )(w, g, m, v)


def _rs_adamw(recv, w, m, v, tile, name, by_cols=False):
    _, r, c = recv.shape

    def body(r_ref, w_ref, m_ref, v_ref, g_ref, d_ref, mo_ref, vo_ref):
        g = r_ref[0].astype(F32)
        for k in range(1, N_DEV):
            g = g + r_ref[k].astype(F32)
        g_ref[...] = g
        d_ref[...], mo_ref[...], vo_ref[...] = _adam_math(w_ref[...], g, m_ref[...], v_ref[...])

    if by_cols:
        spec = pl.BlockSpec((r, tile), lambda i: (0, i))
        rspec = pl.BlockSpec((N_DEV, r, tile), lambda i: (0, 0, i))
        steps, elems = c // tile, (r + 16) * tile
    else:
        spec = pl.BlockSpec((tile, c), lambda i: (i, 0))
        rspec = pl.BlockSpec((N_DEV, tile, c), lambda i: (0, i, 0))
        steps, elems = r // tile, tile * (c + LANE)
    return pl.pallas_call(
        body, name=name, grid=(steps,), in_specs=[rspec] + [spec] * 3, out_specs=[spec] * 4,
        out_shape=[jax.ShapeDtypeStruct((r, c), F32)] * 4,
        compiler_params=_vmem(2 * elems * (N_DEV * 2 + 7 * 4) + (4 << 20)),
    )(recv, w, m, v)


def _all_gather(x, name):
    r, c = x.shape

    def body(x_ref, out_ref, send_sems, recv_sems, local_sem):
        px, py, pc = _position()
        me, sibling = (px, py, pc), (px, py, 1 - pc)
        chips = [(1 - px, py), (px, 1 - py), (1 - px, 1 - py)]

        def slot(qx, qy, qc):
            return out_ref.at[4 * qx + 2 * qy + qc]

        def copy(k, block, to, src=None):
            return pltpu.make_async_remote_copy(
                src_ref=slot(*block) if src is None else src, dst_ref=slot(*block),
                send_sem=send_sems.at[k], recv_sem=recv_sems.at[k], device_id=to, device_id_type=MESH)

        mine = pltpu.make_async_copy(x_ref, slot(*me), local_sem)
        mine.start()
        first = [copy(1 + j, me, (*chip, pc), src=x_ref) for j, chip in enumerate(chips)]
        first.append(copy(0, me, sibling, src=x_ref))
        for cp in first:
            cp.start()
        passed = [copy(4 + j, (*chip, pc), sibling) for j, chip in enumerate(chips)]
        for j, chip in enumerate(chips):
            copy(1 + j, (*chip, pc), me).wait_recv()
            passed[j].start()
        copy(0, sibling, me).wait_recv()
        for j, chip in enumerate(chips):
            copy(4 + j, (*chip, 1 - pc), me).wait_recv()
        for cp in first + passed:
            cp.wait_send()
        mine.wait()

    return pl.pallas_call(
        body, name=name, out_shape=jax.ShapeDtypeStruct((N_DEV, r, c), x.dtype),
        in_specs=[pl.BlockSpec(memory_space=pl.ANY)], out_specs=pl.BlockSpec(memory_space=pl.ANY),
        scratch_shapes=[pltpu.SemaphoreType.DMA((7,)), pltpu.SemaphoreType.DMA((7,)), pltpu.SemaphoreType.DMA],
    )(x)


class _RelayGather:
    ZERO_ROWS = 128

    def __init__(self, x):
        self.x = x
        self.half = x.shape[1] // 2
        self.out_shape = jax.ShapeDtypeStruct((N_DEV + 1,) + x.shape, x.dtype)
        self.scratch = [pltpu.SemaphoreType.DMA((8,)), pltpu.SemaphoreType.DMA((8,)), pltpu.SemaphoreType.DMA,
                        pltpu.VMEM((self.ZERO_ROWS, x.shape[1]), x.dtype), pltpu.SemaphoreType.DMA]

    def _zero_tail(self, out_ref, zero_buf, zero_sem):
        return pltpu.make_async_copy(zero_buf, out_ref.at[N_DEV, pl.ds(0, self.ZERO_ROWS), :], zero_sem)

    def _parts(self, x_ref, out_ref, send_sems, recv_sems, local_sem, zero_buf, zero_sem):
        px, py, pc = _position()
        me, sib = (px, py, pc), (px, py, 1 - pc)
        xn, yn, dg = (1 - px, py, pc), (px, 1 - py, pc), (1 - px, 1 - py, pc)
        half = self.half

        def slot(owner, cols=None):
            ref = out_ref.at[4 * owner[0] + 2 * owner[1] + owner[2]]
            return ref if cols is None else ref.at[:, pl.ds(cols, half)]

        def copy(k, owner, to, src=None, cols=None):
            return pltpu.make_async_remote_copy(
                src_ref=slot(owner, cols) if src is None else src, dst_ref=slot(owner, cols),
                send_sem=send_sems.at[k], recv_sem=recv_sems.at[k], device_id=to, device_id_type=MESH)

        mine = pltpu.make_async_copy(x_ref, slot(me), local_sem)
        own = [copy(1, me, xn, src=x_ref), copy(2, me, yn, src=x_ref), copy(0, me, sib, src=x_ref)]
        return me, sib, xn, yn, dg, copy, mine, own

    def start(self, *refs):
        *_, mine, own = self._parts(*refs)
        mine.start()
        for cp in own:
            cp.start()
        refs[5][...] = jnp.zeros_like(refs[5])
        self._zero_tail(refs[1], refs[5], refs[6]).start()

    def finish(self, *refs):
        me, sib, xn, yn, dg, copy, mine, own = self._parts(*refs)
        flip = lambda q: (q[0], q[1], 1 - q[2])
        copy(1, xn, me).wait_recv()
        relay_x = [copy(3, xn, yn, cols=0), copy(5, xn, sib)]
        for cp in relay_x:
            cp.start()
        copy(2, yn, me).wait_recv()
        relay_y = [copy(4, yn, xn, cols=self.half), copy(6, yn, sib)]
        for cp in relay_y:
            cp.start()
        copy(3, dg, me, cols=0).wait_recv()
        copy(4, dg, me, cols=self.half).wait_recv()
        relay_d = copy(7, dg, sib)
        relay_d.start()
        copy(0, sib, me).wait_recv()
        copy(5, flip(xn), me).wait_recv()
        copy(6, flip(yn), me).wait_recv()
        copy(7, flip(dg), me).wait_recv()
        for cp in own + relay_x + relay_y + [relay_d]:
            cp.wait_send()
        mine.wait()
        self._zero_tail(refs[1], refs[5], refs[6]).wait()


DW_PIECES = ((0, 384), (384, 640), (1024, 1024))


def _local_step(ctx, x, target, modp, lower, wt_u, w_o, w9, conv_b, gate_b, hgw, mlw, ln_g, ln_b, exchange):
    tc = ctx.shape[0]
    tt = tc + x.shape[0]
    nbc, ncc = tc // ROWS, tc // CHUNK
    lower_f, lower_b = lower[0:1], lower[1:2]

    tmh = _pick(tt, (1088, 768, 512, 256))
    if exchange:
        hc, wt = _modulate_fwd(ctx, x, modp, gather=_RelayGather(wt_u))
        wt_u = wt.reshape(-1, D_MODEL)
        u, w_o = _mm(hc, wt_u, "nt", F32, tmh, 1152, D_MODEL, "mm_u", ride=_Ride("gather", w_o), n_rows=N_U)
        w_o = w_o.reshape(D_MODEL, D_MODEL)
    else:
        hc = _modulate_fwd(ctx, x, modp)
        u = _mm(hc, wt_u, "nt", F32, tmh, 1152, D_MODEL, "mm_u")
    cpre = _conv_fwd(u, w9, conv_b, tc)
    bias = jnp.pad(gate_b.reshape(1, 16), ((0, 0), (0, LANE - 16)))

    o_f, hist_f = _hgrn_fwd(u, lower_f, ncc, False)
    o_b, hist_b = _hgrn_fwd(u, lower_b, ncc, True)
    h_f, ch_f, nh_f, mh_f = _mlstm_fwd(cpre, u, bias, ncc, False)
    h_b, ch_b, nh_b, mh_b = _mlstm_fwd(cpre, u, bias, ncc, True)
    y = _post_fwd(o_f, o_b, h_f, h_b, u, hgw, mlw, nbc)
    dz, dxa, fsum = _final(y, w_o, x, target, modp, ln_g, ln_b)

    dw_o = _mm(y, dz, "tn", BF16, D_MODEL, 1024, _pick(y.shape[0], (1024, 512, 256)), "mm_dwo")
    do, dhm, daz, dbo, psum = _post_bwd(dz, w_o, o_f, o_b, h_f, h_b, u, hgw, mlw, nbc)
    if exchange:
        dzf_f, dzq, dv_a, dlb_f, dw_o = _hgrn_bwd(
            u, lower_f, hist_f, do, ncc, False, ride=_Ride("a2a", dw_o.reshape(N_DEV, D_MODEL // N_DEV, D_MODEL)))
    else:
        dzf_f, dzq, dv_a, dlb_f = _hgrn_bwd(u, lower_f, hist_f, do, ncc, False)
    du, dlb_b = _hgrn_bwd(u, lower_b, hist_b, do, ncc, True, final=(dzq, dv_a, dzf_f, daz))
    dqk, dv_m, dg, _ = _mlstm_bwd(cpre, u, bias, ch_f, nh_f, mh_f, h_f, dhm, ncc, False)
    dqk, du, gsum = _mlstm_bwd(cpre, u, bias, ch_b, nh_b, mh_b, h_b, dhm, ncc, True, final=(dqk, dv_m, dg, dbo, du))
    du, gconvw, gconvb = _conv_bwd(dqk, u, w9, tc, du)
    tkw = _pick(tt, (2176, 768, 512, 256))
    blocks = lambda g: g.reshape(N_DEV, N_IN // N_DEV, g.shape[1])
    dwu = lambda name, cols, ride: _mm(du, hc, "tn", BF16, 1152, cols[1], tkw, name, b_cols=cols, ride=ride, m_out=N_IN)
    dwt_a = dwu("mm_dwu_a", DW_PIECES[0], None)
    if exchange:
        whole = lambda piece, into: _Ride("a2a", blocks(piece[1]), cols=(piece[0][0], D_MODEL), into=into)
        dwt_b, got = dwu("mm_dwu_b", DW_PIECES[1], whole((DW_PIECES[0], dwt_a), None))
        dwt_c, got = dwu("mm_dwu_c", DW_PIECES[2], whole((DW_PIECES[1], dwt_b), got))
        dh, dwt_u = _mm(du, wt_u, "nn", F32, tmh, D_MODEL // 2, 3456, "mm_dh", ride=whole((DW_PIECES[2], dwt_c), got))
    else:
        dwt_u = jnp.concatenate([dwt_a, dwu("mm_dwu_b", DW_PIECES[1], None), dwu("mm_dwu_c", DW_PIECES[2], None)], axis=1)
        dh = _mm(du, wt_u, "nn", F32, tmh, D_MODEL // 2, 3456, "mm_dh")
    gx, msum = _modulate_bwd(dh, ctx, x, modp, dxa)

    zero_row = jnp.zeros((1, D_MODEL), F32)
    small = dict(
        dmodx=jnp.concatenate([msum[2:3], msum[3:4], fsum[0:1]], axis=0),
        dmodc=jnp.concatenate([msum[0:1], msum[1:2], zero_row], axis=0),
        gconvw=gconvw, gconvb=gconvb, dlower=jnp.concatenate([dlb_f, dlb_b], axis=0),
        ghgw=psum[0:1], gmlw=psum[1:2], glng=fsum[1:2], glnb=fsum[2:3], losssq=fsum[3:4],
        ggate=jnp.concatenate([gsum, jnp.zeros((7, LANE), F32)], axis=0))
    return gx, dwt_u, dw_o, small


def _pack_small(small):
    return jnp.concatenate([small[name].reshape(rows, LANE) for name, rows in _PACK], axis=0)


def _flat_pad(a, rows):
    flat = a.reshape(-1)
    return jnp.pad(flat, (0, rows * LANE - flat.shape[0])).reshape(rows, LANE)


def kernel(x, c, ctx, c_ctx, w_mod, b_mod, w_in, conv_w, conv_b, hg_lb, ml_gate_b, hg_norm_w, ml_norm_w, w_out, ln_g, ln_b, loss_target, m_c_ctx, m_w_mod, m_b_mod, m_w_in, m_conv_w, m_conv_b, m_hg_lb, m_ml_gate_b, m_hg_norm_w, m_ml_norm_w, m_w_out, m_ln_g, m_ln_b, v_c_ctx, v_w_mod, v_b_mod, v_w_in, v_conv_w, v_conv_b, v_hg_lb, v_ml_gate_b, v_hg_norm_w, v_ml_norm_w, v_w_out, v_ln_g, v_ln_b):
    px, py, pc = _position()
    me = 4 * px + 2 * py + pc
    d = D_MODEL
    n_mod = w_mod.shape[2]
    n_cv = conv_w.shape[3]
    n_lb = hg_lb.shape[2]

    pack0 = jnp.concatenate([c.reshape(-1), conv_w.reshape(-1), hg_lb.reshape(-1)]).reshape(1, -1)
    g0 = _all_gather(pack0, "gather_small_inputs")[:, 0, :]
    c_all = g0[:, :d]
    w9 = jnp.transpose(g0[:, d:d + 9 * n_cv].reshape(N_DEV, 9, n_cv), (1, 0, 2)).reshape(9, N_DEV * n_cv)
    lb4 = jnp.transpose(g0[:, d + 9 * n_cv:].reshape(N_DEV, 4, n_lb), (1, 0, 2)).reshape(4, N_DEV * n_lb)
    lower = _lower_fwd(lb4)

    cs = jnp.concatenate([c_all, c_ctx.reshape(1, d), jnp.zeros((7, d), F32)], axis=0)
    b_cols = lax.dynamic_slice(b_mod, (0, me * n_mod), (1, n_mod))
    slab = _mod_fwd(cs, w_mod[0], b_cols)
    mod_all = jnp.transpose(_all_gather(slab, "gather_mod"), (1, 0, 2)).reshape(16, N_DEV * n_mod)
    mod_x = lax.dynamic_slice(mod_all, (me, 0), (1, 3 * d)).reshape(3, d)
    modp = jnp.stack([mod_all[8].reshape(3, d), mod_x])

    gx, recv_wi, recv_wo, small = _local_step(ctx[0], x[0], loss_target[0], modp, lower, w_in[0].T.astype(BF16),
                                              w_out[0].astype(BF16), w9, conv_b, ml_gate_b[0], hg_norm_w, ml_norm_w,
                                              ln_g, ln_b, True)
    g_wi, d_wi, nm_wi, nv_wi = [a.T for a in _rs_adamw(recv_wi, w_in[0].T, m_w_in[0].T, v_w_in[0].T, 256,
                                                       "adamw_w_in", by_cols=True)]
    g_wo, d_wo, nm_wo, nv_wo = _rs_adamw(recv_wo, w_out[0], m_w_out[0], v_w_out[0], 64, "adamw_w_out")

    packs = _all_gather(_pack_small(small), "gather_small_grads")
    total = _reduce8(packs, "reduce_small_grads")
    offs = _pack_offsets()
    piece = lambda name: total[offs[name][0]:offs[name][0] + offs[name][1]]
    g_bmod, g_lb0, g_lb1, loss8 = _small_finish(total, lower[0:2].reshape(16, LANE), float(d))

    ox = offs["dmodx"][0]
    dmodx_all = packs[:, ox:ox + 48, :].reshape(N_DEV, 3 * d)
    dmodc_tot = piece("dmodc").reshape(1, 3 * d)
    d9 = jnp.concatenate([dmodx_all, dmodc_tot, jnp.zeros((7, 3 * d), F32)], axis=0)
    d9_cols = lax.dynamic_slice(d9, (0, me * n_mod), (16, n_mod))
    g_wmod, pc_part = _mod_bwd_w(cs, d9_cols, w_mod[0])
    c_ctx8 = jnp.concatenate([c_ctx.reshape(1, d), jnp.zeros((7, d), F32)], axis=0)
    g_cctx = _cctx_grad(_all_gather(pc_part, "gather_cctx"), c_ctx8)[0]
    d_wmod, nm_wmod, nv_wmod = _adamw(w_mod[0], g_wmod, m_w_mod[0], v_w_mod[0], 256, "adamw_w_mod")

    g_convw_full = piece("gconvw").reshape(9, d)
    g_convw = lax.dynamic_slice(g_convw_full, (0, me * n_cv), (9, n_cv)).reshape(conv_w.shape)
    lb_full = jnp.stack([jnp.stack([g_lb0[0:8].reshape(-1), g_lb1[0:8].reshape(-1)]),
                         jnp.stack([g_lb0[8:16].reshape(-1), g_lb1[8:16].reshape(-1)])])
    g_hglb = lax.dynamic_slice(lb_full, (0, 0, me * n_lb), (2, 2, n_lb))
    grads = dict(
        c_ctx=g_cctx, b_mod=g_bmod.reshape(b_mod.shape), conv_w=g_convw, conv_b=piece("gconvb").reshape(conv_b.shape),
        hg_lb=g_hglb, ml_gate_b=piece("ggate")[0, :16].reshape(ml_gate_b.shape),
        hg_norm_w=piece("ghgw").reshape(hg_norm_w.shape), ml_norm_w=piece("gmlw").reshape(ml_norm_w.shape),
        ln_g=piece("glng").reshape(ln_g.shape), ln_b=piece("glnb").reshape(ln_b.shape))
    params = dict(c_ctx=(c_ctx, m_c_ctx, v_c_ctx), b_mod=(b_mod, m_b_mod, v_b_mod), conv_w=(conv_w, m_conv_w, v_conv_w),
                  conv_b=(conv_b, m_conv_b, v_conv_b), hg_lb=(hg_lb, m_hg_lb, v_hg_lb),
                  ml_gate_b=(ml_gate_b, m_ml_gate_b, v_ml_gate_b), hg_norm_w=(hg_norm_w, m_hg_norm_w, v_hg_norm_w),
                  ml_norm_w=(ml_norm_w, m_ml_norm_w, v_ml_norm_w), ln_g=(ln_g, m_ln_g, v_ln_g), ln_b=(ln_b, m_ln_b, v_ln_b))
    names = list(params)
    rows_of = {n: -(-params[n][0].size // LANE) for n in names}
    rows_tot = -(-sum(rows_of.values()) // 8) * 8
    cat = lambda arrs: jnp.concatenate(
        [_flat_pad(a, rows_of[n]) for n, a in zip(names, arrs)]
        + [jnp.ones((rows_tot - sum(rows_of.values()), LANE), F32)], axis=0)
    d_s, m_s, v_s = _adamw(cat([params[n][0] for n in names]), cat([grads[n] for n in names]),
                           cat([params[n][1] for n in names]), cat([params[n][2] for n in names]), rows_tot, "adamw_small")
    delta, new_m, new_v, off = {}, {}, {}, 0
    for n in names:
        shape, size = params[n][0].shape, params[n][0].size
        take = lambda a: a[off:off + rows_of[n]].reshape(-1)[:size].reshape(shape)
        delta[n], new_m[n], new_v[n] = take(d_s), take(m_s), take(v_s)
        off += rows_of[n]
    grads.update(w_mod=g_wmod[None], w_in=g_wi[None], w_out=g_wo[None])
    delta.update(w_mod=d_wmod[None], w_in=d_wi[None], w_out=d_wo[None])
    new_m.update(w_mod=nm_wmod[None], w_in=nm_wi[None], w_out=nm_wo[None])
    new_v.update(w_mod=nv_wmod[None], w_in=nv_wi[None], w_out=nv_wo[None])

    order = ("c_ctx", "w_mod", "b_mod", "w_in", "conv_w", "conv_b", "hg_lb", "ml_gate_b", "hg_norm_w", "ml_norm_w",
             "w_out", "ln_g", "ln_b")
    return (loss8[0, 0], gx[None], *[grads[n] for n in order], *[delta[n] for n in order],
            *[new_m[n] for n in order], *[new_v[n] for n in order])
```

```python
import jax
import jax.numpy as jnp
from jax import lax
from jax.experimental import pallas as pl
from jax.experimental.pallas import tpu as pltpu

F32 = jnp.float32
BF16 = jnp.bfloat16

D_MODEL = 2048
W_A = 1024
W_B = 1024
HG_HEADS = 8
HG_D = 128
ML_HEADS = 4
ML_D = 256
CHUNK = 64
N_IN = 10256
LANE = 128
N_U = 81 * LANE
N_DEV = 8
ALPHA = 2.0 ** 0.25
LN_EPS = 1e-5
NORM_EPS = 1e-6
ADAM_LR, ADAM_B1, ADAM_B2, ADAM_EPS, ADAM_WD, ADAM_STEP = 0.001, 0.9, 0.999, 1e-08, 0.01, 10
VMEM_CAP = 60 * 1024 * 1024

SEG_AQ, SEG_AFF, SEG_AFB, SEG_AI, SEG_AZ = range(5)
BLK_QK = 40
SEG_BV, SEG_BO, SEG_BZ = 7, 8, 9
BLK_GATE = 80

MESH = pl.DeviceIdType.MESH


def _vmem(nbytes):
    return pltpu.CompilerParams(vmem_limit_bytes=int(min(VMEM_CAP, max(nbytes, 16 * 1024 * 1024))))


def _sigmoid(x):
    return 1.0 / (1.0 + jnp.exp(-x))


def _silu(x):
    return x * _sigmoid(x)


def _dsilu(x):
    s = _sigmoid(x)
    return s * (1.0 + x * (1.0 - s))


def _silu_both(x):
    s = _sigmoid(x)
    return x * s, s * (1.0 + x * (1.0 - s))


def _bdot(a, b, dims):
    return lax.dot_general(a.astype(BF16), b.astype(BF16), (dims, ((), ())), preferred_element_type=F32)


def _nn(a, b):
    return _bdot(a, b, ((1,), (0,)))


def _nt(a, b):
    return _bdot(a, b, ((1,), (1,)))


def _tn(a, b):
    return _bdot(a, b, ((0,), (0,)))


def _exact_nn(a, b):
    return lax.dot_general(a, b, (((1,), (0,)), ((), ())), precision=lax.Precision.HIGHEST,
                           preferred_element_type=F32)


def _exact_tn(a, b):
    return lax.dot_general(a, b, (((0,), (0,)), ((), ())), precision=lax.Precision.HIGHEST,
                           preferred_element_type=F32)


def _tri(rev):
    t = lax.broadcasted_iota(jnp.int32, (CHUNK, CHUNK), 0)
    s = lax.broadcasted_iota(jnp.int32, (CHUNK, CHUNK), 1)
    return (s >= t) if rev else (s <= t)


def _eye():
    t = lax.broadcasted_iota(jnp.int32, (CHUNK, CHUNK), 0)
    s = lax.broadcasted_iota(jnp.int32, (CHUNK, CHUNK), 1)
    return (s == t).astype(F32)


def _row_to_col(row):
    return jnp.sum(_eye() * row, axis=1, keepdims=True)


def _last_onehot(rev):
    t = lax.broadcasted_iota(jnp.int32, (CHUNK, 1), 0)
    return (t == (0 if rev else CHUNK - 1)).astype(F32)


def _head_slices(width, n_heads):
    hd = width // n_heads
    return [slice(h * hd, (h + 1) * hd) for h in range(n_heads)]


def _scan_sum(x, rev):
    n = x.shape[0]
    t = lax.broadcasted_iota(jnp.int32, x.shape, 0)
    s = 1
    while s < n:
        if rev:
            x = x + jnp.where(t < n - s, pltpu.roll(x, n - s, 0), 0.0)
        else:
            x = x + jnp.where(t >= s, pltpu.roll(x, s, 0), 0.0)
        s *= 2
    return x


def _dot3(a, b, dims):
    a_hi, b_hi = a.astype(BF16), b.astype(BF16)
    a_lo, b_lo = (a - a_hi.astype(F32)).astype(BF16), (b - b_hi.astype(F32)).astype(BF16)
    dot = lambda x, y: lax.dot_general(x, y, (dims, ((), ())), preferred_element_type=F32)
    return dot(a_hi, b_hi) + (dot(a_hi, b_lo) + dot(a_lo, b_hi))


def _hg_common(zq, zf, lb, rev):
    q, dq_dz = _silu_both(zq)
    sg = _sigmoid(zf)
    f = lb + (1.0 - lb) * sg
    g = jnp.log(f)
    k = 1.0 - f
    b = _scan_sum(g, rev)
    b_last = jnp.sum(g, axis=0, keepdims=True)
    r = b[CHUNK // 2:CHUNK // 2 + 1, :]
    e_up = jnp.exp(b - r)
    e_dn = jnp.exp(r - b)
    e_b = e_up * jnp.exp(r)
    e_lb = e_dn * jnp.exp(b_last - r)
    return dict(q=q, dq_dz=dq_dz, sg=sg, f=f, k=k, e_up=e_up, e_dn=e_dn, e_b=e_b, e_lb=e_lb, e_last=jnp.exp(b_last),
                q_t=q * e_up, k_t=k * e_dn, q_s=q * e_b, k_h=k * e_lb, tri=_tri(rev).astype(F32))


def hg_chunk_fwd(zq, zf, v, lb, st, rev):
    c = _hg_common(zq, zf, lb, rev)
    hs = _head_slices(zq.shape[1], zq.shape[1] // HG_D)
    s = [_nt(c["q_t"][:, sl], c["k_t"][:, sl]) for sl in hs]
    oi = [_nt(c["q_s"][:, sl], st[sl, :]) for sl in hs]
    ds = [_tn(v[:, sl], c["k_h"][:, sl]) for sl in hs]
    oa = [_nn(c["tri"] * s_h, v[:, sl]) for s_h, sl in zip(s, hs)]
    o = jnp.concatenate([x + y for x, y in zip(oi, oa)], axis=1)
    st_new = jnp.concatenate([st[sl, :] * c["e_last"][:, sl] + d for sl, d in zip(hs, ds)], axis=0)
    return o, st_new


def hg_chunk_bwd(zq, zf, v, lb, st, do, dst_new, rev):
    c = _hg_common(zq, zf, lb, rev)
    hs = _head_slices(zq.shape[1], zq.shape[1] // HG_D)
    tri, q_t, k_t, q_s, k_h = c["tri"], c["q_t"], c["k_t"], c["q_s"], c["k_h"]
    s = [_nt(q_t[:, sl], k_t[:, sl]) for sl in hs]
    da = [tri * _nt(do[:, sl], v[:, sl]) for sl in hs]
    dq_s = [_nn(do[:, sl], st[sl, :]) for sl in hs]
    dk_h = [_nn(v[:, sl], dst_new[sl, :]) for sl in hs]
    dv_s = [_nt(k_h[:, sl], dst_new[sl, :]) for sl in hs]
    dst_q = [_tn(do[:, sl], q_s[:, sl]) for sl in hs]
    dq_t = [_dot3(da_h, k_t[:, sl], ((1,), (0,))) for da_h, sl in zip(da, hs)]
    dk_t = [_dot3(da_h, q_t[:, sl], ((0,), (0,))) for da_h, sl in zip(da, hs)]
    dv_a = [_tn(tri * s_h, do[:, sl]) for s_h, sl in zip(s, hs)]
    cat = lambda parts: jnp.concatenate(parts, axis=1)
    dq_s, dk_h, dq_t, dk_t = cat(dq_s), cat(dk_h), cat(dq_t), cat(dk_t)
    dv = cat([x + y for x, y in zip(dv_a, dv_s)])
    dst = jnp.concatenate([dst_new[sl, :] * c["e_last"][:, sl] + d for sl, d in zip(hs, dst_q)], axis=0)
    dq = dq_s * c["e_b"] + dq_t * c["e_up"]
    dk = dk_t * c["e_dn"] + dk_h * c["e_lb"]
    db = c["q"] * dq - c["k"] * dk
    ss = cat([jnp.sum(dst_new[sl, :] * st[sl, :], axis=0, keepdims=True) for sl in hs])
    d_all = jnp.sum(dk_h * k_h, axis=0, keepdims=True) + c["e_last"] * ss
    dg = _scan_sum(db, not rev) + d_all
    dzq = dq * c["dq_dz"]
    df = dg / c["f"] - dk
    dzf = df * (1.0 - lb) * c["sg"] * (1.0 - c["sg"])
    dlb = jnp.sum(df * (1.0 - c["sg"]), axis=0, keepdims=True)
    return dzq, dzf, dv, dlb, dst


def _log_sigmoid(x):
    return jnp.minimum(x, 0.0) - jnp.log(1.0 + jnp.exp(-jnp.abs(x)))


def _each(fn, *lists):
    return [fn(*xs) for xs in zip(*lists)]


def _bf(xs):
    return [x.astype(BF16) for x in xs]


def _ml_forward_parts(qp, kp, v, gates, c, n, m, rev, with_num):
    hs = _head_slices(qp.shape[1], qp.shape[1] // ML_D)
    q_all, dq_dp = _silu_both(qp)
    k_all, dk_dp = _silu_both(kp)
    k_all = k_all * (ML_D ** -0.5)
    q = [q_all[:, sl] for sl in hs]
    k = [k_all[:, sl] for sl in hs]
    vv = [v[:, sl] for sl in hs]
    cc = [c[sl, :] for sl in hs]
    tri_b = _tri(rev)
    tri = tri_b.astype(F32)
    tri_t = _tri(not rev).astype(F32)
    e_last = _last_onehot(rev)
    qb, kb, vb, cb = _bf(q), _bf(k), _bf(vv), _bf(cc)
    qk = _each(_nt, qb, kb)
    parts = []
    for (gi_c, gi_r, gf_c, gf_r), m_h in zip(gates, m):
        lf_c, lf_r = _log_sigmoid(gf_c), _log_sigmoid(gf_r)
        b_c = jnp.sum(tri * lf_r, axis=1, keepdims=True)
        b_r = jnp.sum(tri_t * lf_c, axis=0, keepdims=True)
        log_w = jnp.where(tri_b, b_c - b_r + gi_r, -jnp.inf)
        m_inter = b_c + m_h
        m_t = jnp.maximum(m_inter, jnp.max(log_w, axis=1, keepdims=True))
        m_new = jnp.sum(m_t * e_last, axis=0, keepdims=True)
        b_last = jnp.sum(b_c * e_last, axis=0, keepdims=True)
        parts.append(dict(a=jnp.exp(m_inter - m_t), p=jnp.exp(log_w - m_t), floor=jnp.exp(-m_t), m_new=m_new,
                          ws=jnp.exp(b_last - b_c + gi_c - m_new), decay=jnp.exp(b_last + m_h - m_new), gf_c=gf_c))
    w = [pt["p"] * x for pt, x in zip(parts, qk)]
    wb = _bf(w)
    for pt, q_h, n_h, w_h in zip(parts, q, n, w):
        qn = jnp.sum(q_h * n_h, axis=1, keepdims=True)
        den = pt["a"] * qn + jnp.sum(w_h, axis=1, keepdims=True)
        pt.update(qn=qn, den=den, rinv=1.0 / jnp.maximum(jnp.abs(den), pt["floor"]), w=w_h)
    if with_num:
        qc = _each(_nt, qb, cb)
        wv = _each(_nn, wb, vb)
        for pt, qc_h, wv_h in zip(parts, qc, wv):
            pt.update(num=pt["a"] * qc_h + wv_h)
    return hs, q, k, vv, cc, tri, parts, dict(q=qb, k=kb, v=vb, c=cb, w=wb, dq_dp=dq_dp, dk_dp=dk_dp)


def ml_chunk_fwd(qp, kp, v, gates, c, n, m, rev):
    hs, q, k, vv, cc, tri, parts, bf = _ml_forward_parts(qp, kp, v, gates, c, n, m, rev, True)
    h = jnp.concatenate([pt["num"] * pt["rinv"] for pt in parts], axis=1)
    upd = _each(_tn, [pt["ws"] * v_h for pt, v_h in zip(parts, vv)], bf["k"])
    c_new = jnp.concatenate([pt["decay"] * c_h + u for pt, c_h, u in zip(parts, cc, upd)], axis=0)
    n_new = [pt["decay"] * n_h + jnp.sum(pt["ws"] * k_h, axis=0, keepdims=True) for pt, n_h, k_h in zip(parts, n, k)]
    return h, c_new, n_new, [pt["m_new"] for pt in parts]


def ml_chunk_bwd(qp, kp, v, gates, c, n, m, h_out, dh, dc_new, dn_new, rev):
    hs, q, k, vv, cc, tri, parts, bf = _ml_forward_parts(qp, kp, v, gates, c, n, m, rev, False)
    dcn = [dc_new[sl, :] for sl in hs]
    dcb = _bf(dcn)
    dnum, dden = [], []
    for pt, sl in zip(parts, hs):
        dh_h = dh[:, sl]
        signed_live = jnp.where(jnp.abs(pt["den"]) > pt["floor"], jnp.where(pt["den"] >= 0.0, 1.0, -1.0), 0.0)
        dnum.append(dh_h * pt["rinv"])
        dden.append(-jnp.sum(dh_h * h_out[:, sl], axis=1, keepdims=True) * pt["rinv"] * signed_live)
    dnb = _bf(dnum)
    dw = [x + y for x, y in zip(_each(_nt, dnb, bf["v"]), dden)]
    kdc = _each(_nt, bf["k"], dcb)
    vdc = _each(_nn, bf["v"], dcb)
    dqk = [x * pt["p"] for x, pt in zip(dw, parts)]
    adn = [pt["a"] * x for pt, x in zip(parts, dnum)]
    dqkb, adnb = _bf(dqk), _bf(adn)
    dv_w = _each(_tn, bf["w"], dnb)
    dq_k = _each(_nn, dqkb, bf["k"])
    dq_c = _each(_nn, adnb, bf["c"])
    dk_q = _each(_tn, dqkb, bf["q"])
    dc_q = _each(_tn, adnb, bf["q"])
    dq, dk, dv, dgi, dgf, dc, dn = [], [], [], [], [], [], []
    for i, pt in enumerate(parts):
        a, ws, decay = pt["a"], pt["ws"], pt["decay"]
        add = a * dden[i]
        e = dw[i] * pt["w"]
        dv.append(dv_w[i] + ws * kdc[i])
        dq.append(dq_k[i] + dq_c[i] + add * n[i])
        dk.append(dk_q[i] + ws * vdc[i] + ws * dn_new[i])
        alpha = jnp.sum(q[i] * dq_c[i], axis=1, keepdims=True) + dden[i] * pt["qn"] * a
        omega = (jnp.sum(vdc[i] * k[i], axis=1, keepdims=True) + jnp.sum(k[i] * dn_new[i], axis=1, keepdims=True)) * ws
        delta = decay * (jnp.sum(jnp.sum(dcn[i] * cc[i], axis=1, keepdims=True), axis=0, keepdims=True)
                         + jnp.sum(dn_new[i] * n[i], axis=1, keepdims=True))
        dc.append(decay * dcn[i] + dc_q[i])
        dn.append(decay * dn_new[i] + jnp.sum(add * q[i], axis=0, keepdims=True))
        e_rows = jnp.sum(e, axis=1, keepdims=True)
        e_cols = _row_to_col(jnp.sum(e, axis=0, keepdims=True))
        dgi.append(e_cols + omega)
        db = e_rows + alpha - e_cols - omega
        tail = jnp.sum(omega, axis=0, keepdims=True) + delta
        dlf = _row_to_col(jnp.sum(tri * db, axis=0, keepdims=True)) + tail
        dgf.append(dlf * (1.0 - _sigmoid(pt["gf_c"])))
    cat = lambda xs: jnp.concatenate(xs, axis=1)
    dqp = cat(dq) * bf["dq_dp"]
    dkp = cat(dk) * (ML_D ** -0.5) * bf["dk_dp"]
    return dqp, dkp, cat(dv), dgi, dgf, jnp.concatenate(dc, axis=0), dn


def _pick(n, prefs):
    for p in prefs:
        if n % p == 0:
            return p
    raise ValueError(f"no tile for {n} among {prefs}")


def _position():
    return lax.axis_index("x"), lax.axis_index("y"), lax.axis_index("c")


class _Ride:
    def __init__(self, kind, x, cols=None, into=None):
        self.kind, self.x, self.cols, self.into = kind, x, cols, into
        r, c = x.shape[-2:]
        self.out_shape = jax.ShapeDtypeStruct((N_DEV, r, c if cols is None else cols[1]), x.dtype)
        self.width = c

    def _copies(self, x_ref, out_ref, send_sems, recv_sems, local_sem):
        px, py, pc = _position()
        me = 4 * px + 2 * py + pc
        src = (lambda slot: x_ref) if self.kind == "gather" else (lambda slot: x_ref.at[slot])
        dst = ((lambda slot: out_ref.at[slot]) if self.cols is None
               else (lambda slot: out_ref.at[slot, :, pl.ds(self.cols[0], self.width)]))
        mine = pltpu.make_async_copy(src(me), dst(me), local_sem)
        sends, recvs = [], []
        for k, (fx, fy, fc) in enumerate([(1, 0, 0), (0, 1, 0), (1, 1, 0), (1, 0, 1), (0, 1, 1), (1, 1, 1), (0, 0, 1)]):
            qx, qy, qc = (1 - px if fx else px), (1 - py if fy else py), (1 - pc if fc else pc)
            peer = 4 * qx + 2 * qy + qc
            sends.append(pltpu.make_async_remote_copy(
                src_ref=src(peer), dst_ref=dst(me), send_sem=send_sems.at[k], recv_sem=recv_sems.at[k],
                device_id=(qx, qy, qc), device_id_type=MESH))
            recvs.append(pltpu.make_async_remote_copy(
                src_ref=src(me), dst_ref=dst(peer), send_sem=send_sems.at[k], recv_sem=recv_sems.at[k],
                device_id=(qx, qy, qc), device_id_type=MESH))
        return mine, sends, recvs

    def start(self, *refs):
        mine, sends, _ = self._copies(*refs)
        mine.start()
        for cp in sends:
            cp.start()

    def wait(self, *refs):
        mine, sends, recvs = self._copies(*refs)
        for cp in recvs:
            cp.wait_recv()
        for cp in sends:
            cp.wait_send()
        mine.wait()

    def operands(self):
        return [self.x] + ([self.into] if self.into is not None else [])


_RIDE_SCRATCH = [pltpu.SemaphoreType.DMA((7,)), pltpu.SemaphoreType.DMA((7,)), pltpu.SemaphoreType.DMA]
_ANY = pl.BlockSpec(memory_space=pl.ANY)


def _mm(a, b, mode, out_dtype, tm, tn, tk, name, ride=None, b_cols=None, m_out=None, n_rows=None):
    if mode == "nn":
        (m, k), (k2, n) = a.shape, b.shape
    elif mode == "nt":
        (m, k), (n, k2) = a.shape, b.shape
        n = n if n_rows is None else n_rows
    else:
        (k, m), (k2, n) = a.shape, b.shape
    off, n = (0, n) if b_cols is None else b_cols
    assert (k2 >= k if mode == "nn" else k == k2) and m % tm == 0 and n % tn == 0 and k % tk == 0, \
        (a.shape, b.shape, mode, tm, tn, tk)
    assert b_cols is None or (mode != "nt" and off % LANE == 0)
    nk = k // tk
    dims = {"nn": ((1,), (0,)), "nt": ((1,), (1,)), "tn": ((0,), (0,))}[mode]
    a_spec = (pl.BlockSpec((tk, tm), lambda j, i, kk: (kk, i)) if mode == "tn"
              else pl.BlockSpec((tm, tk), lambda j, i, kk: (i, kk)))
    if b_cols is not None:
        b_spec = pl.BlockSpec((pl.Element(tk), pl.Element(tn)),
                              lambda j, i, kk: (pl.multiple_of(kk * tk, LANE), pl.multiple_of(off + j * tn, LANE)))
    elif mode == "nt":
        b_spec = pl.BlockSpec((tn, tk), lambda j, i, kk: (j, kk))
    else:
        b_spec = pl.BlockSpec((tk, tn), lambda j, i, kk: (kk, j))

    grid = (n // tn, m // tm, nk)
    n_ride_in = len(ride.operands()) if ride is not None else 0

    def body(a_ref, b_ref, *rest):
        if ride is not None:
            x_ref = rest[0]
            o_ref, got_ref, acc_ref = rest[n_ride_in:n_ride_in + 3]
            comm = (x_ref, got_ref) + tuple(rest[n_ride_in + 3:])
        else:
            o_ref, acc_ref = rest
        kk = pl.program_id(2)
        step = (pl.program_id(0) * grid[1] + pl.program_id(1)) * nk + kk
        if ride is not None:
            @pl.when(step == 0)
            def _():
                ride.start(*comm)

        part = lax.dot_general(a_ref[...], b_ref[...], (dims, ((), ())), preferred_element_type=F32)
        if nk == 1:
            o_ref[...] = part.astype(o_ref.dtype)
        else:
            @pl.when(kk == 0)
            def _():
                acc_ref[...] = part

            @pl.when(jnp.logical_and(kk > 0, kk < nk - 1))
            def _():
                acc_ref[...] += part

            @pl.when(kk == nk - 1)
            def _():
                o_ref[...] = (acc_ref[...] + part).astype(o_ref.dtype)

        if ride is not None:
            @pl.when(step == grid[0] * grid[1] * nk - 1)
            def _():
                ride.wait(*comm)

    osz = jnp.dtype(out_dtype).itemsize
    need = 2 * (tm * tk * a.dtype.itemsize + tk * tn * b.dtype.itemsize + tm * tn * osz) + tm * tn * 4
    o_spec = pl.BlockSpec((tm, tn), lambda j, i, kk: (i, j))
    o_shape = jax.ShapeDtypeStruct((m if m_out is None else m_out, n), out_dtype)
    extra = ride is not None
    return pl.pallas_call(
        body, name=name, grid=grid,
        in_specs=[a_spec, b_spec] + [_ANY] * n_ride_in,
        out_specs=[o_spec, _ANY] if extra else o_spec,
        out_shape=[o_shape, ride.out_shape] if extra else o_shape,
        scratch_shapes=[pltpu.VMEM((tm, tn) if nk > 1 else (8, LANE), F32)] + (_RIDE_SCRATCH if extra else []),
        input_output_aliases={3: 1} if extra and ride.into is not None else {},
        compiler_params=_vmem(need + (12 << 20)),
    )(a, b, *(ride.operands() if extra else []))


ROWS = 256


def _ln_stats(x):
    mu = jnp.mean(x, axis=-1, keepdims=True)
    xc = x - mu
    var = jnp.mean(xc * xc, axis=-1, keepdims=True)
    rstd = lax.rsqrt(var + LN_EPS)
    return xc * rstd, rstd


def _token_specs(nbc, nbx, d):
    return [pl.BlockSpec((ROWS, d), lambda i: (jnp.minimum(i, nbc - 1), 0)),
            pl.BlockSpec((ROWS, d), lambda i: (jnp.maximum(i - nbc, 0), 0))]


def _tokens(c_ref, x_ref, nbc):
    return jnp.where(pl.program_id(0) < nbc, c_ref[...], x_ref[...])


def _modulate_fwd(ctx, x, modp, gather=None):
    d = x.shape[1]
    nbc, nbx = ctx.shape[0] // ROWS, x.shape[0] // ROWS
    riding = gather is not None

    def body(c_ref, x_ref, mod_ref, *rest):
        if riding:
            comm = (rest[0], rest[2]) + tuple(rest[3:])
            o_ref = rest[1]

            @pl.when(pl.program_id(0) == 0)
            def _():
                gather.start(*comm)
        else:
            o_ref = rest[0]
        n, _ = _ln_stats(_tokens(c_ref, x_ref, nbc))
        o_ref[...] = (n * (1.0 + mod_ref[0, 1:2, :]) + mod_ref[0, 0:1, :]).astype(BF16)
        if riding:
            @pl.when(pl.program_id(0) == nbc + nbx - 1)
            def _():
                gather.finish(*comm)

    o_spec = pl.BlockSpec((ROWS, d), lambda i: (i, 0))
    o_shape = jax.ShapeDtypeStruct((ctx.shape[0] + x.shape[0], d), BF16)
    return pl.pallas_call(
        body, name="modulate_fwd", grid=(nbc + nbx,),
        in_specs=_token_specs(nbc, nbx, d) + [pl.BlockSpec((1, 3, d), lambda i: (jnp.where(i >= nbc, 1, 0), 0, 0))]
        + ([_ANY] if riding else []),
        out_specs=[o_spec, _ANY] if riding else o_spec,
        out_shape=[o_shape, gather.out_shape] if riding else o_shape,
        scratch_shapes=gather.scratch if riding else [],
    )(ctx, x, modp, *([gather.x] if riding else []))


def _modulate_bwd(dh, ctx, x, modp, dxa):
    t, d = x.shape
    nbc, nbx = ctx.shape[0] // ROWS, t // ROWS

    def body(dh_ref, c_ref, x_ref, mod_ref, dxa_ref, gx_ref, sum_ref):
        i = pl.program_id(0)
        n, rstd = _ln_stats(_tokens(c_ref, x_ref, nbc))
        g = dh_ref[...]
        dn = g * (1.0 + mod_ref[0, 1:2, :])
        dx = rstd * (dn - jnp.mean(dn, axis=-1, keepdims=True) - n * jnp.mean(dn * n, axis=-1, keepdims=True))
        gx_ref[...] = dx + dxa_ref[...]
        dshift = jnp.sum(g, axis=0, keepdims=True)
        dscale = jnp.sum(g * n, axis=0, keepdims=True)

        @pl.when(i == 0)
        def _():
            sum_ref[...] = jnp.zeros_like(sum_ref)

        @pl.when(i < nbc)
        def _():
            sum_ref[0:1, :] += dshift
            sum_ref[1:2, :] += dscale

        @pl.when(i >= nbc)
        def _():
            sum_ref[2:3, :] += dshift
            sum_ref[3:4, :] += dscale

    lat = lambda i: (jnp.maximum(i - nbc, 0), 0)
    return pl.pallas_call(
        body, name="modulate_bwd", grid=(nbc + nbx,),
        in_specs=[pl.BlockSpec((ROWS, d), lambda i: (i, 0))] + _token_specs(nbc, nbx, d)
        + [pl.BlockSpec((1, 3, d), lambda i: (jnp.where(i >= nbc, 1, 0), 0, 0)), pl.BlockSpec((ROWS, d), lat)],
        out_specs=[pl.BlockSpec((ROWS, d), lat), pl.BlockSpec((8, d), lambda i: (0, 0))],
        out_shape=[jax.ShapeDtypeStruct((t, d), F32), jax.ShapeDtypeStruct((8, d), F32)],
    )(dh, ctx, x, modp, dxa)


def _post_fwd(o_f, o_b, h_f, h_b, u, hgw, mlw, nbc):
    tt = u.shape[0]
    t = tt - nbc * ROWS

    def body(of_ref, ob_ref, hf_ref, hb_ref, az_ref, bo_ref, bz_ref, hgw_ref, mlw_ref, y_ref):
        o = of_ref[...] + ob_ref[...]
        for sl in _head_slices(W_A, HG_HEADS):
            oh = o[:, sl]
            rs = lax.rsqrt(jnp.mean(oh * oh, axis=-1, keepdims=True) + NORM_EPS)
            y_ref[:, sl] = (oh * rs * hgw_ref[:, sl] * _silu(az_ref[:, sl])).astype(BF16)
        hm = hf_ref[...] + hb_ref[...]
        for sl in _head_slices(W_B, ML_HEADS):
            hh = hm[:, sl]
            mu = jnp.mean(hh, axis=-1, keepdims=True)
            hc = hh - mu
            rstd = lax.rsqrt(jnp.mean(hc * hc, axis=-1, keepdims=True) + NORM_EPS)
            out = hc * rstd * mlw_ref[:, sl] * _sigmoid(bo_ref[:, sl]) * _silu(bz_ref[:, sl])
            y_ref[:, W_A + sl.start:W_A + sl.stop] = out.astype(BF16)

    row = lambda i: (i + nbc, 0)
    seg = lambda s: pl.BlockSpec((ROWS, 1024), lambda i: (i + nbc, s))
    wspec = pl.BlockSpec((1, 1024), lambda i: (0, 0))
    return pl.pallas_call(
        body, name="post_fwd", grid=(t // ROWS,),
        in_specs=[pl.BlockSpec((ROWS, 1024), row)] * 4 + [seg(SEG_AZ), seg(SEG_BO), seg(SEG_BZ), wspec, wspec],
        out_specs=pl.BlockSpec((ROWS, 2048), lambda i: (i, 0)),
        out_shape=jax.ShapeDtypeStruct((t, 2048), BF16),
    )(o_f, o_b, h_f, h_b, u, u, u, hgw, mlw)


def _post_bwd(dz, w_o, o_f, o_b, h_f, h_b, u, hgw, mlw, nbc):
    tt = u.shape[0]
    d = w_o.shape[0]

    def body(dz_ref, w_ref, of_ref, ob_ref, hf_ref, hb_ref, az_ref, bo_ref, bz_ref, hgw_ref, mlw_ref,
             do_ref, dhm_ref, daz_ref, dbo_ref, sum_ref):
        i = pl.program_id(0)
        live = jnp.where(i >= nbc, 1.0, 0.0)
        dy = lax.dot_general(dz_ref[...], w_ref[...], (((1,), (1,)), ((), ())), preferred_element_type=F32) * live

        @pl.when(i == 0)
        def _():
            sum_ref[...] = jnp.zeros_like(sum_ref)

        o = of_ref[...] + ob_ref[...]
        for sl in _head_slices(W_A, HG_HEADS):
            oh = o[:, sl]
            rs = lax.rsqrt(jnp.mean(oh * oh, axis=-1, keepdims=True) + NORM_EPS)
            on = oh * rs
            az = az_ref[:, sl]
            dya = dy[:, sl]
            saz, daz = _silu_both(az)
            doa = dya * saz
            daz_ref[:, sl] = (dya * on * hgw_ref[:, sl] * daz).astype(BF16)
            sum_ref[0:1, sl] += jnp.sum(doa * on, axis=0, keepdims=True)
            don = doa * hgw_ref[:, sl]
            do_ref[:, sl] = rs * (don - on * jnp.mean(don * on, axis=-1, keepdims=True))
        hm = hf_ref[...] + hb_ref[...]
        for sl in _head_slices(W_B, ML_HEADS):
            hh = hm[:, sl]
            mu = jnp.mean(hh, axis=-1, keepdims=True)
            hc = hh - mu
            rstd = lax.rsqrt(jnp.mean(hc * hc, axis=-1, keepdims=True) + NORM_EPS)
            hn = hc * rstd
            hw = hn * mlw_ref[:, sl]
            bo, bz = bo_ref[:, sl], bz_ref[:, sl]
            sbo = _sigmoid(bo)
            sbz, dbz = _silu_both(bz)
            dyb = dy[:, W_A + sl.start:W_A + sl.stop]
            dhw = dyb * sbo * sbz
            dbo_ref[:, sl] = (dyb * hw * sbz * sbo * (1.0 - sbo)).astype(BF16)
            dbo_ref[:, 1024 + sl.start:1024 + sl.stop] = (dyb * hw * sbo * dbz).astype(BF16)
            sum_ref[1:2, sl] += jnp.sum(dhw * hn, axis=0, keepdims=True)
            dhn = dhw * mlw_ref[:, sl]
            dhm_ref[:, sl] = rstd * (dhn - jnp.mean(dhn, axis=-1, keepdims=True)
                                     - hn * jnp.mean(dhn * hn, axis=-1, keepdims=True))

    row = lambda i: (i, 0)
    seg = lambda s: pl.BlockSpec((ROWS, 1024), lambda i: (i, s))
    wspec = pl.BlockSpec((1, 1024), lambda i: (0, 0))
    return pl.pallas_call(
        body, name="post_bwd", grid=(tt // ROWS,),
        in_specs=[pl.BlockSpec((ROWS, 2048), lambda i: (jnp.maximum(i - nbc, 0), 0)), pl.BlockSpec((d, d), lambda i: (0, 0))]
        + [pl.BlockSpec((ROWS, 1024), row)] * 4 + [seg(SEG_AZ), seg(SEG_BO), seg(SEG_BZ), wspec, wspec],
        out_specs=[pl.BlockSpec((ROWS, 1024), row), pl.BlockSpec((ROWS, 1024), row),
                   pl.BlockSpec((ROWS, 1024), row), pl.BlockSpec((ROWS, 2048), row),
                   pl.BlockSpec((8, 1024), lambda i: (0, 0))],
        out_shape=[jax.ShapeDtypeStruct((tt, 1024), F32), jax.ShapeDtypeStruct((tt, 1024), F32),
                   jax.ShapeDtypeStruct((tt, 1024), BF16), jax.ShapeDtypeStruct((tt, 2048), BF16),
                   jax.ShapeDtypeStruct((8, 1024), F32)],
        compiler_params=_vmem(4 * d * d + 30 * ROWS * 2048 * 4),
    )(dz, w_o, o_f, o_b, h_f, h_b, u, u, u, hgw, mlw)


def _final(y, w_o, x, target, modp, ln_g, ln_b):
    t, d = x.shape

    def body(y_ref, w_ref, x_ref, tg_ref, mod_ref, g_ref, b_ref, dz_ref, dxa_ref, sum_ref):
        i = pl.program_id(0)
        zz = lax.dot_general(y_ref[...], w_ref[...], (((1,), (0,)), ((), ())), preferred_element_type=F32)
        gate = mod_ref[0, 2:3, :]
        pre = ALPHA * x_ref[...] + gate * zz
        nh, rstd = _ln_stats(pre)
        err = nh * g_ref[...] + b_ref[...] - tg_ref[...]
        dxo = err * (1.0 / d)
        dnh = dxo * g_ref[...]
        dpre = rstd * (dnh - jnp.mean(dnh, axis=-1, keepdims=True) - nh * jnp.mean(dnh * nh, axis=-1, keepdims=True))
        dz_ref[...] = (gate * dpre).astype(BF16)
        dxa_ref[...] = ALPHA * dpre

        @pl.when(i == 0)
        def _():
            sum_ref[...] = jnp.zeros_like(sum_ref)

        sum_ref[0:1, :] += jnp.sum(dpre * zz, axis=0, keepdims=True)
        sum_ref[1:2, :] += jnp.sum(dxo * nh, axis=0, keepdims=True)
        sum_ref[2:3, :] += jnp.sum(dxo, axis=0, keepdims=True)
        sum_ref[3:4, :] += jnp.sum(err * err, axis=0, keepdims=True)

    row = lambda i: (i, 0)
    vec = pl.BlockSpec((1, d), lambda i: (0, 0))
    return pl.pallas_call(
        body, name="final_ln_loss", grid=(t // ROWS,),
        in_specs=[pl.BlockSpec((ROWS, d), row), pl.BlockSpec((d, d), lambda i: (0, 0)), pl.BlockSpec((ROWS, d), row),
                  pl.BlockSpec((ROWS, d), row), pl.BlockSpec((1, 3, d), lambda i: (1, 0, 0)), vec, vec],
        out_specs=[pl.BlockSpec((ROWS, d), row), pl.BlockSpec((ROWS, d), row), pl.BlockSpec((8, d), lambda i: (0, 0))],
        out_shape=[jax.ShapeDtypeStruct((t, d), BF16), jax.ShapeDtypeStruct((t, d), F32),
                   jax.ShapeDtypeStruct((8, d), F32)],
        compiler_params=_vmem(4 * d * d + 24 * ROWS * d * 4),
    )(y, w_o, x, target, modp, ln_g, ln_b)


GRID_W = 64


def _shift(x, s, ok):
    n = x.shape[0]
    return jnp.where(ok, pltpu.roll(x, s % n, 0), 0.0)


def _grid_masks(n):
    t = lax.broadcasted_iota(jnp.int32, (n, LANE), 0)
    col = t & (GRID_W - 1)
    return dict(left=col >= 1, right=col <= GRID_W - 2, up=t >= GRID_W, down=t < n - GRID_W)


def _seq_masks(n):
    t = lax.broadcasted_iota(jnp.int32, (n, LANE), 0)
    return dict(left=t >= 1, right=t <= n - 2)


def _conv_fwd(u, w9, cb, tc):
    tt = u.shape[0]
    t = tt - tc

    def body(u_ref, w_ref, b_ref, o_ref):
        w = [w_ref[r:r + 1, :] for r in range(9)]
        xc = u_ref[0:tc, :]
        ms = _seq_masks(tc)
        o_ref[0:tc, :] = (w[3] * _shift(xc, 1, ms["left"]) + w[4] * xc + w[5] * _shift(xc, -1, ms["right"])
                          + b_ref[...])
        x = u_ref[tc:tt, :]
        mg = _grid_masks(t)
        taps = (_shift(x, 1, mg["left"]), x, _shift(x, -1, mg["right"]))
        rows = [w[3 * i] * taps[0] + w[3 * i + 1] * taps[1] + w[3 * i + 2] * taps[2] for i in range(3)]
        o_ref[tc:tt, :] = (rows[1] + _shift(rows[0], GRID_W, mg["up"]) + _shift(rows[2], -GRID_W, mg["down"])
                           + b_ref[...])

    return pl.pallas_call(
        body, name="conv_fwd", grid=(2048 // LANE,),
        in_specs=[pl.BlockSpec((tt, LANE), lambda j: (0, BLK_QK + j)), pl.BlockSpec((9, LANE), lambda j: (0, j)),
                  pl.BlockSpec((1, LANE), lambda j: (0, j))],
        out_specs=pl.BlockSpec((tt, LANE), lambda j: (0, j)),
        out_shape=jax.ShapeDtypeStruct((tt, 2048), F32),
        compiler_params=_vmem(40 * tt * LANE * 4),
    )(u, w9, cb)


def _conv_bwd(dcp, u, w9, tc, du):
    tt = u.shape[0]
    t = tt - tc

    def body(d_ref, u_ref, w_ref, du_in_ref, du_ref, gw_ref, gb_ref):
        w = [w_ref[r:r + 1, :] for r in range(9)]
        csum = lambda a: jnp.sum(a, axis=0, keepdims=True)
        dc = d_ref[0:tc, :]
        xc = u_ref[0:tc, :]
        ms = _seq_masks(tc)
        du_ref[0:tc, :] = (w[3] * _shift(dc, -1, ms["right"]) + w[4] * dc + w[5] * _shift(dc, 1, ms["left"])).astype(BF16)
        gmid = [csum(dc * _shift(xc, 1, ms["left"])), csum(dc * xc), csum(dc * _shift(xc, -1, ms["right"]))]
        d = d_ref[tc:tt, :]
        x = u_ref[tc:tt, :]
        mg = _grid_masks(t)
        dtaps = (_shift(d, -1, mg["right"]), d, _shift(d, 1, mg["left"]))
        rows = [w[3 * i] * dtaps[0] + w[3 * i + 1] * dtaps[1] + w[3 * i + 2] * dtaps[2] for i in range(3)]
        du_ref[tc:tt, :] = (rows[1] + _shift(rows[0], -GRID_W, mg["down"]) + _shift(rows[2], GRID_W, mg["up"])).astype(BF16)
        xtaps = (_shift(x, 1, mg["left"]), x, _shift(x, -1, mg["right"]))
        for j in range(3):
            gw_ref[j:j + 1, :] = csum(d * _shift(xtaps[j], GRID_W, mg["up"]))
            gw_ref[3 + j:4 + j, :] = csum(d * xtaps[j]) + gmid[j]
            gw_ref[6 + j:7 + j, :] = csum(d * _shift(xtaps[j], -GRID_W, mg["down"]))
        gb_ref[...] = csum(d) + csum(dc)

    return pl.pallas_call(
        body, name="conv_bwd", grid=(2048 // LANE,),
        in_specs=[pl.BlockSpec((tt, LANE), lambda j: (0, j)), pl.BlockSpec((tt, LANE), lambda j: (0, BLK_QK + j)),
                  pl.BlockSpec((9, LANE), lambda j: (0, j)), _ANY],
        out_specs=[pl.BlockSpec((tt, LANE), lambda j: (0, BLK_QK + j)), pl.BlockSpec((9, LANE), lambda j: (0, j)),
                   pl.BlockSpec((1, LANE), lambda j: (0, j))],
        out_shape=[jax.ShapeDtypeStruct(du.shape, BF16), jax.ShapeDtypeStruct((9, 2048), F32),
                   jax.ShapeDtypeStruct((1, 2048), F32)],
        input_output_aliases={3: 0},
        compiler_params=_vmem(48 * tt * LANE * 4),
    )(dcp, u, w9, du)


SUB = 4
STEP = SUB * CHUNK
ML_SUB = 2
ML_STEP = ML_SUB * CHUNK


def _chunk_of(pos, ncc, nc, rev):
    if not rev:
        return pos
    return jnp.where(pos < ncc, ncc - 1 - pos, nc - 1 - (pos - ncc))


def _sub_rows(rev, sub=SUB):
    order = range(sub - 1, -1, -1) if rev else range(sub)
    return [(s, slice(s * CHUNK, (s + 1) * CHUNK)) for s in order]


def _hgrn_fwd(u, lower_d, ncc, rev):
    tt = u.shape[0]
    nc, ncc = tt // STEP, ncc // SUB
    seg_f = SEG_AFB if rev else SEG_AFF

    def body(zq_ref, zf_ref, v_ref, lb_ref, o_ref, hist_ref, st_ref):
        @pl.when(pl.program_id(0) == 0)
        def _():
            st_ref[...] = jnp.zeros_like(st_ref)

        st = st_ref[...]
        for s, r in _sub_rows(rev):
            hist_ref[s] = st
            o, st = hg_chunk_fwd(zq_ref[r, :], zf_ref[r, :], v_ref[r, :], lb_ref[...], st, rev)
            o_ref[r, :] = o
        st_ref[...] = st

    seg = lambda s: pl.BlockSpec((STEP, 1024), lambda j: (_chunk_of(j, ncc, nc, rev), s))
    return pl.pallas_call(
        body, name="hgrn_fwd_rev" if rev else "hgrn_fwd", grid=(nc,),
        in_specs=[seg(SEG_AQ), seg(seg_f), seg(SEG_AI), pl.BlockSpec((1, 1024), lambda j: (0, 0))],
        out_specs=[pl.BlockSpec((STEP, 1024), lambda j: (_chunk_of(j, ncc, nc, rev), 0)),
                   pl.BlockSpec((SUB, 1024, HG_D), lambda j: (_chunk_of(j, ncc, nc, rev), 0, 0))],
        out_shape=[jax.ShapeDtypeStruct((tt, 1024), F32), jax.ShapeDtypeStruct((nc * SUB, 1024, HG_D), F32)],
        scratch_shapes=[pltpu.VMEM((1024, HG_D), F32)],
    )(u, u, u, lower_d)


def _hgrn_bwd(u, lower_d, hist, do, ncc, rev, ride=None, final=None):
    tt = u.shape[0]
    nc, ncc = tt // STEP, ncc // SUB
    seg_f = SEG_AFB if rev else SEG_AFF
    is_final = final is not None
    has_a2a = ride is not None
    n_out = 2 if is_final else 4
    width = 5 * 1024

    def body(zq_ref, zf_ref, v_ref, lb_ref, hist_ref, do_ref, *rest):
        if is_final:
            aq_ref, av_ref, af_ref, az_ref = rest[:4]
            rest = rest[4:]
        if has_a2a:
            x_ref, rest = rest[0], rest[1:]
        outs, rest = rest[:n_out], rest[n_out:]
        dlb_ref = outs[-1]
        if has_a2a:
            comm = (x_ref, rest[0]) + tuple(rest[2:])
            dst_ref = rest[1]
        else:
            dst_ref = rest[0]

        @pl.when(pl.program_id(0) == 0)
        def _():
            dst_ref[...] = jnp.zeros_like(dst_ref)
            dlb_ref[...] = jnp.zeros_like(dlb_ref)
            if has_a2a:
                ride.start(*comm)

        dst = dst_ref[...]
        dlb_sum = dlb_ref[...]
        for s, r in reversed(_sub_rows(rev)):
            dzq, dzf, dv, dlb, dst = hg_chunk_bwd(zq_ref[r, :], zf_ref[r, :], v_ref[r, :], lb_ref[...],
                                                  hist_ref[s], do_ref[r, :], dst, rev)
            dlb_sum = dlb_sum + dlb
            if is_final:
                du_ref = outs[0]
                dzf_own, dzf_other = dzf.astype(BF16), af_ref[r, :]
                du_ref[r, 0:1024] = (dzq + aq_ref[r, :]).astype(BF16)
                du_ref[r, 1024:2048] = dzf_other if rev else dzf_own
                du_ref[r, 2048:3072] = dzf_own if rev else dzf_other
                du_ref[r, 3072:4096] = (dv + av_ref[r, :]).astype(BF16)
                du_ref[r, 4096:5120] = az_ref[r, :]
            else:
                dzf_ref, dzq_ref, dv_ref = outs[:3]
                dzf_ref[r, :] = dzf.astype(BF16)
                dzq_ref[r, :] = dzq
                dv_ref[r, :] = dv
        dst_ref[...] = dst
        dlb_ref[...] = dlb_sum

        if has_a2a:
            @pl.when(pl.program_id(0) == nc - 1)
            def _():
                ride.wait(*comm)

    cidx = lambda j: _chunk_of(nc - 1 - j, ncc, nc, rev)
    seg = lambda s: pl.BlockSpec((STEP, 1024), lambda j: (cidx(j), s))
    row = pl.BlockSpec((STEP, 1024), lambda j: (cidx(j), 0))
    dlb_spec = pl.BlockSpec((1, 1024), lambda j: (0, 0))
    dlb_shape = jax.ShapeDtypeStruct((1, 1024), F32)
    if is_final:
        out_specs = [pl.BlockSpec((STEP, width), lambda j: (cidx(j), 0)), dlb_spec]
        out_shape = [jax.ShapeDtypeStruct((tt, N_U), BF16), dlb_shape]
    else:
        out_specs = [row, row, row, dlb_spec]
        out_shape = [jax.ShapeDtypeStruct((tt, 1024), BF16), jax.ShapeDtypeStruct((tt, 1024), F32),
                     jax.ShapeDtypeStruct((tt, 1024), F32), dlb_shape]
    ins = [u, u, u, lower_d, hist, do] + (list(final) if is_final else []) + ([ride.x] if has_a2a else [])
    return pl.pallas_call(
        body, name="hgrn_bwd_rev" if rev else "hgrn_bwd", grid=(nc,),
        in_specs=[seg(SEG_AQ), seg(seg_f), seg(SEG_AI), pl.BlockSpec((1, 1024), lambda j: (0, 0)),
                  pl.BlockSpec((SUB, 1024, HG_D), lambda j: (cidx(j), 0, 0)), row] + ([row] * 4 if is_final else [])
        + ([_ANY] if has_a2a else []),
        out_specs=out_specs + ([_ANY] if has_a2a else []),
        out_shape=out_shape + ([ride.out_shape] if has_a2a else []),
        scratch_shapes=[pltpu.VMEM((1024, HG_D), F32)] + (_RIDE_SCRATCH if has_a2a else []),
    )(*ins)


def _gate_views(g_ref, b_ref, r, head, rev):
    gc = g_ref[r, :] + b_ref[...]
    lane = lax.broadcasted_iota(jnp.int32, (1, LANE), 1)
    eye = _eye()
    d = 1 if rev else 0
    ii, fi = d * ML_HEADS + head, 2 * ML_HEADS + d * ML_HEADS + head
    col = lambda idx: jnp.sum(jnp.where(lane == idx, gc, 0.0), axis=1, keepdims=True)
    row = lambda c: jnp.sum(eye * c, axis=0, keepdims=True)
    gi, gf = col(ii), col(fi)
    return gi, row(gi), gf, row(gf)


def _mlstm_fwd(cpre, u, bias, ncc, rev):
    tt = u.shape[0]
    nc, ncc = tt // ML_STEP, ncc // ML_SUB
    nhd = ML_HEADS

    def body(q_ref, k_ref, v_ref, g_ref, b_ref, h_ref, ch_ref, nh_ref, mh_ref, c_ref, n_ref, m_ref):
        @pl.when(pl.program_id(0) == 0)
        def _():
            c_ref[...] = jnp.zeros_like(c_ref)
            n_ref[...] = jnp.zeros_like(n_ref)
            m_ref[...] = jnp.zeros_like(m_ref)

        c, n_all, m_all = c_ref[...], n_ref[...], m_ref[...]
        n = [n_all[hd:hd + 1, :] for hd in range(nhd)]
        m = [m_all[hd:hd + 1, 0:1] for hd in range(nhd)]
        for s, r in _sub_rows(rev, ML_SUB):
            ch_ref[s] = c
            for hd in range(nhd):
                nh_ref[s, hd:hd + 1, :] = n[hd]
                mh_ref[s, hd:hd + 1, :] = jnp.broadcast_to(m[hd], (1, LANE))
            gates = [_gate_views(g_ref, b_ref, r, hd, rev) for hd in range(nhd)]
            h, c, n, m = ml_chunk_fwd(q_ref[r, :], k_ref[r, :], v_ref[r, :], gates, c, n, m, rev)
            h_ref[r, :] = h
        c_ref[...] = c
        for hd in range(nhd):
            n_ref[hd:hd + 1, :] = n[hd]
            m_ref[hd:hd + 1, :] = jnp.broadcast_to(m[hd], (1, LANE))

    cidx = lambda j: _chunk_of(j, ncc, nc, rev)
    row = lambda s: pl.BlockSpec((ML_STEP, 1024), lambda j: (cidx(j), s))
    st3 = lambda a, b: pl.BlockSpec((ML_SUB, a, b), lambda j: (cidx(j), 0, 0))
    return pl.pallas_call(
        body, name="mlstm_fwd_rev" if rev else "mlstm_fwd", grid=(nc,),
        in_specs=[row(0), row(1), row(SEG_BV), pl.BlockSpec((ML_STEP, LANE), lambda j: (cidx(j), BLK_GATE)),
                  pl.BlockSpec((1, LANE), lambda j: (0, 0))],
        out_specs=[row(0), st3(1024, ML_D), st3(8, ML_D), st3(8, LANE)],
        out_shape=[jax.ShapeDtypeStruct((tt, 1024), F32), jax.ShapeDtypeStruct((nc * ML_SUB, 1024, ML_D), F32),
                   jax.ShapeDtypeStruct((nc * ML_SUB, 8, ML_D), F32), jax.ShapeDtypeStruct((nc * ML_SUB, 8, LANE), F32)],
        scratch_shapes=[pltpu.VMEM((1024, ML_D), F32), pltpu.VMEM((8, ML_D), F32), pltpu.VMEM((8, LANE), F32)],
    )(cpre, cpre, u, u, bias)


def _mlstm_bwd(cpre, u, bias, chist, nhist, mhist, h_out, dh, ncc, rev, final=None):
    tt = u.shape[0]
    nc, ncc = tt // ML_STEP, ncc // ML_SUB
    nhd = ML_HEADS
    is_final = final is not None
    d = 1 if rev else 0
    col0, width = SEG_BV * 1024, N_U - SEG_BV * 1024

    def body(q_ref, k_ref, v_ref, g_ref, b_ref, ch_ref, nh_ref, mh_ref, ho_ref, dh_ref, *rest):
        if is_final:
            aqk_ref, av_ref, ag_ref, bo_ref = rest[:4]
            dqk_ref, du_ref, gs_ref, dc_ref, dn_ref = rest[5:]
        else:
            dqk_ref, dv_ref, dg_ref, gs_ref, dc_ref, dn_ref = rest

        @pl.when(pl.program_id(0) == 0)
        def _():
            dc_ref[...] = jnp.zeros_like(dc_ref)
            dn_ref[...] = jnp.zeros_like(dn_ref)
            gs_ref[...] = jnp.zeros_like(gs_ref)

        lane = lax.broadcasted_iota(jnp.int32, (1, LANE), 1)
        dc, dn_all, gs = dc_ref[...], dn_ref[...], gs_ref[...]
        dn = [dn_all[hd:hd + 1, :] for hd in range(nhd)]
        for s, r in reversed(_sub_rows(rev, ML_SUB)):
            gates = [_gate_views(g_ref, b_ref, r, hd, rev) for hd in range(nhd)]
            n_all, m_all = nh_ref[s], mh_ref[s]
            dqp, dkp, dv, dgi, dgf, dc, dn = ml_chunk_bwd(
                q_ref[r, :], k_ref[r, :], v_ref[r, :], gates, ch_ref[s],
                [n_all[hd:hd + 1, :] for hd in range(nhd)], [m_all[hd:hd + 1, 0:1] for hd in range(nhd)],
                ho_ref[r, :], dh_ref[r, :], dc, dn, rev)
            dg = ag_ref[r, :] if is_final else jnp.zeros((CHUNK, LANE), F32)
            for hd in range(nhd):
                dg = dg + jnp.where(lane == d * ML_HEADS + hd, dgi[hd], 0.0)
                dg = dg + jnp.where(lane == 2 * ML_HEADS + d * ML_HEADS + hd, dgf[hd], 0.0)
            if is_final:
                dqp = dqp + aqk_ref[r, 0:W_B]
                dkp = dkp + aqk_ref[r, W_B:2 * W_B]
                du_ref[r, 0:1024] = (dv + av_ref[r, :]).astype(BF16)
                du_ref[r, 1024:3072] = bo_ref[r, :]
                du_ref[r, 3072:3072 + LANE] = dg.astype(BF16)
            else:
                dv_ref[r, :] = dv
                dg_ref[r, :] = dg
            dqk_ref[r, 0:W_B] = dqp
            dqk_ref[r, W_B:2 * W_B] = dkp
            gs = gs + jnp.sum(dg, axis=0, keepdims=True)
        dc_ref[...] = dc
        gs_ref[...] = gs
        for hd in range(nhd):
            dn_ref[hd:hd + 1, :] = dn[hd]

    cidx = lambda j: _chunk_of(nc - 1 - j, ncc, nc, rev)
    row = lambda s: pl.BlockSpec((ML_STEP, 1024), lambda j: (cidx(j), s))
    wide = pl.BlockSpec((ML_STEP, 2048), lambda j: (cidx(j), 0))
    gate = pl.BlockSpec((ML_STEP, LANE), lambda j: (cidx(j), 0))
    st3 = lambda a, b: pl.BlockSpec((ML_SUB, a, b), lambda j: (cidx(j), 0, 0))
    gs_spec, gs_shape = pl.BlockSpec((1, LANE), lambda j: (0, 0)), jax.ShapeDtypeStruct((1, LANE), F32)
    dqk_shape = jax.ShapeDtypeStruct((tt, 2048), F32)
    ins = [cpre, cpre, u, u, bias, chist, nhist, mhist, h_out, dh] + (list(final) if is_final else [])
    if is_final:
        out_specs = [wide, pl.BlockSpec((pl.Element(ML_STEP), pl.Element(width)), lambda j: (cidx(j) * ML_STEP, col0)), gs_spec]
        out_shape = [dqk_shape, jax.ShapeDtypeStruct((tt, N_U), BF16), gs_shape]
    else:
        out_specs = [wide, row(0), gate, gs_spec]
        out_shape = [dqk_shape, jax.ShapeDtypeStruct((tt, 1024), F32), jax.ShapeDtypeStruct((tt, LANE), F32), gs_shape]
    return pl.pallas_call(
        body, name="mlstm_bwd_rev" if rev else "mlstm_bwd", grid=(nc,),
        in_specs=[row(0), row(1), row(SEG_BV), pl.BlockSpec((ML_STEP, LANE), lambda j: (cidx(j), BLK_GATE)),
                  pl.BlockSpec((1, LANE), lambda j: (0, 0)),
                  st3(1024, ML_D), st3(8, ML_D), st3(8, LANE), row(0), row(0)]
        + ([wide, row(0), gate, wide, _ANY] if is_final else []),
        out_specs=out_specs, out_shape=out_shape,
        input_output_aliases={14: 1} if is_final else {},
        scratch_shapes=[pltpu.VMEM((1024, ML_D), F32), pltpu.VMEM((8, ML_D), F32)],
    )(*ins)


def _whole(body, out_shape, name, *args, nbytes=0):
    return pl.pallas_call(body, name=name, out_shape=out_shape, compiler_params=_vmem(nbytes))(*args)


def _mod_fwd(cs, w_cols, b_cols):
    def body(c_ref, w_ref, b_ref, o_ref):
        o_ref[...] = _exact_nn(_silu(c_ref[...]), w_ref[...]) + b_ref[...]

    return _whole(body, jax.ShapeDtypeStruct((16, w_cols.shape[1]), F32), "mod_fwd", cs, w_cols, b_cols,
                  nbytes=4 * w_cols.size * 4)


def _mod_bwd_w(cs, d9, w_cols):
    def body(c_ref, d_ref, w_ref, gw_ref, pc_ref):
        gw_ref[...] = _exact_tn(_silu(c_ref[...]), d_ref[...])
        pc = lax.dot_general(d_ref[8:16, :], w_ref[...], (((1,), (1,)), ((), ())), precision=lax.Precision.HIGHEST,
                             preferred_element_type=F32)
        row = lax.broadcasted_iota(jnp.int32, pc.shape, 0)
        pc_ref[...] = jnp.where(row == 0, pc, 0.0)

    return _whole(body, [jax.ShapeDtypeStruct(w_cols.shape, F32), jax.ShapeDtypeStruct((8, w_cols.shape[0]), F32)],
                  "mod_bwd_w", cs, d9, w_cols, nbytes=6 * w_cols.size * 4)


def _lower_fwd(lb4):
    def body(l_ref, o_ref):
        o_ref[...] = jnp.zeros_like(o_ref)
        o_ref[0:1, :] = 1.0 / (1.0 + jnp.exp(l_ref[1:2, :] - l_ref[0:1, :]))
        o_ref[1:2, :] = 1.0 / (1.0 + jnp.exp(l_ref[3:4, :] - l_ref[2:3, :]))

    return _whole(body, jax.ShapeDtypeStruct((8, lb4.shape[1]), F32), "lower_fwd", lb4)


def _reduce8(g, name):
    def body(g_ref, o_ref):
        acc = g_ref[0]
        for k in range(1, N_DEV):
            acc = acc + g_ref[k]
        o_ref[...] = acc

    return _whole(body, jax.ShapeDtypeStruct(g.shape[1:], F32), name, g, nbytes=4 * g.size * 4)


_PACK = (("dmodx", 48), ("dmodc", 48), ("gconvw", 144), ("gconvb", 16), ("dlower", 16), ("ghgw", 8), ("gmlw", 8),
         ("glng", 16), ("glnb", 16), ("losssq", 16), ("ggate", 8))


def _pack_offsets():
    off, out = 0, {}
    for name, rows in _PACK:
        out[name] = (off, rows)
        off += rows
    return out


def _small_finish(total, p0, d_feat):
    offs = _pack_offsets()

    def body(t_ref, p_ref, gb_ref, a0_ref, a1_ref, loss_ref):
        ox, oc, ol, oq = offs["dmodx"][0], offs["dmodc"][0], offs["dlower"][0], offs["losssq"][0]
        gb_ref[...] = t_ref[ox:ox + 48, :] + t_ref[oc:oc + 48, :]
        p = p_ref[...]
        da0 = t_ref[ol:ol + 16, :] * p * (1.0 - p)
        a0_ref[...] = da0
        a1_ref[...] = -da0
        sq = t_ref[oq:oq + 16, :]
        tot = jnp.sum(jnp.sum(sq, axis=1, keepdims=True), axis=0, keepdims=True)
        loss_ref[...] = jnp.broadcast_to(tot * (0.5 / d_feat), loss_ref.shape)

    s = jax.ShapeDtypeStruct
    return _whole(body, [s((48, LANE), F32), s((16, LANE), F32), s((16, LANE), F32), s((8, LANE), F32)],
                  "small_finish", total, p0)


def _cctx_grad(parts, c_ctx8):
    def body(p_ref, c_ref, o_ref):
        acc = p_ref[0]
        for k in range(1, N_DEV):
            acc = acc + p_ref[k]
        o_ref[...] = acc * _dsilu(c_ref[...])

    return _whole(body, jax.ShapeDtypeStruct(c_ctx8.shape, F32), "cctx_grad", parts, c_ctx8)


def _adam_math(w, g, m, v):
    m = ADAM_B1 * m + (1.0 - ADAM_B1) * g
    v = ADAM_B2 * v + (1.0 - ADAM_B2) * (g * g)
    m_hat = m / (1.0 - ADAM_B1 ** ADAM_STEP)
    v_hat = v / (1.0 - ADAM_B2 ** ADAM_STEP)
    delta = -ADAM_LR * (m_hat / (jnp.sqrt(v_hat) + ADAM_EPS) + ADAM_WD * w)
    return delta, m, v


def _adamw(w, g, m, v, rows, name):
    r, c = w.shape

    def body(w_ref, g_ref, m_ref, v_ref, d_ref, mo_ref, vo_ref):
        d_ref[...], mo_ref[...], vo_ref[...] = _adam_math(w_ref[...], g_ref[...], m_ref[...], v_ref[...])

    spec = pl.BlockSpec((rows, c), lambda i: (i, 0))
    return pl.pallas_call(
        body, name=name, grid=(r // rows,), in_specs=[spec] * 4, out_specs=[spec] * 3,
        out_shape=[jax.ShapeDtypeStruct((r, c), F32)] * 3,
        compiler_params=_vmem(16 * rows * (c + LANE) * 4),
    )(w, g, m, v)


def _rs_adamw(recv, w, m, v, tile, name, by_cols=False):
    _, r, c = recv.shape

    def body(r_ref, w_ref, m_ref, v_ref, g_ref, d_ref, mo_ref, vo_ref):
        g = r_ref[0].astype(F32)
        for k in range(1, N_DEV):
            g = g + r_ref[k].astype(F32)
        g_ref[...] = g
        d_ref[...], mo_ref[...], vo_ref[...] = _adam_math(w_ref[...], g, m_ref[...], v_ref[...])

    if by_cols:
        spec = pl.BlockSpec((r, tile), lambda i: (0, i))
        rspec = pl.BlockSpec((N_DEV, r, tile), lambda i: (0, 0, i))
        steps, elems = c // tile, (r + 16) * tile
    else:
        spec = pl.BlockSpec((tile, c), lambda i: (i, 0))
        rspec = pl.BlockSpec((N_DEV, tile, c), lambda i: (0, i, 0))
        steps, elems = r // tile, tile * (c + LANE)
    return pl.pallas_call(
        body, name=name, grid=(steps,), in_specs=[rspec] + [spec] * 3, out_specs=[spec] * 4,
        out_shape=[jax.ShapeDtypeStruct((r, c), F32)] * 4,
        compiler_params=_vmem(2 * elems * (N_DEV * 2 + 7 * 4) + (4 << 20)),
    )(recv, w, m, v)


def _all_gather(x, name):
    r, c = x.shape

    def body(x_ref, out_ref, send_sems, recv_sems, local_sem):
        px, py, pc = _position()
        me, sibling = (px, py, pc), (px, py, 1 - pc)
        chips = [(1 - px, py), (px, 1 - py), (1 - px, 1 - py)]

        def slot(qx, qy, qc):
            return out_ref.at[4 * qx + 2 * qy + qc]

        def copy(k, block, to, src=None):
            return pltpu.make_async_remote_copy(
                src_ref=slot(*block) if src is None else src, dst_ref=slot(*block),
                send_sem=send_sems.at[k], recv_sem=recv_sems.at[k], device_id=to, device_id_type=MESH)

        mine = pltpu.make_async_copy(x_ref, slot(*me), local_sem)
        mine.start()
        first = [copy(1 + j, me, (*chip, pc), src=x_ref) for j, chip in enumerate(chips)]
        first.append(copy(0, me, sibling, src=x_ref))
        for cp in first:
            cp.start()
        passed = [copy(4 + j, (*chip, pc), sibling) for j, chip in enumerate(chips)]
        for j, chip in enumerate(chips):
            copy(1 + j, (*chip, pc), me).wait_recv()
            passed[j].start()
        copy(0, sibling, me).wait_recv()
        for j, chip in enumerate(chips):
            copy(4 + j, (*chip, 1 - pc), me).wait_recv()
        for cp in first + passed:
            cp.wait_send()
        mine.wait()

    return pl.pallas_call(
        body, name=name, out_shape=jax.ShapeDtypeStruct((N_DEV, r, c), x.dtype),
        in_specs=[pl.BlockSpec(memory_space=pl.ANY)], out_specs=pl.BlockSpec(memory_space=pl.ANY),
        scratch_shapes=[pltpu.SemaphoreType.DMA((7,)), pltpu.SemaphoreType.DMA((7,)), pltpu.SemaphoreType.DMA],
    )(x)


class _RelayGather:
    ZERO_ROWS = 128

    def __init__(self, x):
        self.x = x
        self.half = x.shape[1] // 2
        self.out_shape = jax.ShapeDtypeStruct((N_DEV + 1,) + x.shape, x.dtype)
        self.scratch = [pltpu.SemaphoreType.DMA((10,)), pltpu.SemaphoreType.DMA((10,)), pltpu.SemaphoreType.DMA,
                        pltpu.VMEM((self.ZERO_ROWS, x.shape[1]), x.dtype), pltpu.SemaphoreType.DMA]

    def _zero_tail(self, out_ref, zero_buf, zero_sem):
        return pltpu.make_async_copy(zero_buf, out_ref.at[N_DEV, pl.ds(0, self.ZERO_ROWS), :], zero_sem)

    def _parts(self, x_ref, out_ref, send_sems, recv_sems, local_sem, zero_buf, zero_sem):
        px, py, pc = _position()
        me, sib = (px, py, pc), (px, py, 1 - pc)
        xn, yn, dg = (1 - px, py, pc), (px, 1 - py, pc), (1 - px, 1 - py, pc)
        half = self.half

        def slot(owner, cols=None):
            ref = out_ref.at[4 * owner[0] + 2 * owner[1] + owner[2]]
            return ref if cols is None else ref.at[:, pl.ds(cols, half)]

        def copy(k, owner, to, own=False, cols=None):
            src = slot(owner, cols) if not own else (x_ref if cols is None else x_ref.at[:, pl.ds(cols, half)])
            return pltpu.make_async_remote_copy(
                src_ref=src, dst_ref=slot(owner, cols),
                send_sem=send_sems.at[k], recv_sem=recv_sems.at[k], device_id=to, device_id_type=MESH)

        mine = pltpu.make_async_copy(x_ref, slot(me), local_sem)
        own = [copy(1, me, xn, own=True, cols=0), copy(9, me, yn, own=True, cols=half),
               copy(8, me, xn, own=True, cols=half), copy(2, me, yn, own=True, cols=0), copy(0, me, sib, own=True)]
        return me, sib, xn, yn, dg, copy, mine, own

    def start(self, *refs):
        *_, mine, own = self._parts(*refs)
        mine.start()
        for cp in own:
            cp.start()
        refs[5][...] = jnp.zeros_like(refs[5])
        self._zero_tail(refs[1], refs[5], refs[6]).start()

    def finish(self, *refs):
        me, sib, xn, yn, dg, copy, mine, own = self._parts(*refs)
        flip = lambda q: (q[0], q[1], 1 - q[2])
        half = self.half
        copy(1, xn, me, cols=0).wait_recv()
        relay_x = [copy(3, xn, yn, cols=0)]
        relay_x[0].start()
        copy(9, yn, me, cols=half).wait_recv()
        relay_y = [copy(4, yn, xn, cols=half)]
        relay_y[0].start()
        copy(8, xn, me, cols=half).wait_recv()
        relay_x.append(copy(5, xn, sib))
        relay_x[1].start()
        copy(2, yn, me, cols=0).wait_recv()
        relay_y.append(copy(6, yn, sib))
        relay_y[1].start()
        copy(3, dg, me, cols=0).wait_recv()
        copy(4, dg, me, cols=self.half).wait_recv()
        relay_d = copy(7, dg, sib)
        relay_d.start()
        copy(0, sib, me).wait_recv()
        copy(5, flip(xn), me).wait_recv()
        copy(6, flip(yn), me).wait_recv()
        copy(7, flip(dg), me).wait_recv()
        for cp in own + relay_x + relay_y + [relay_d]:
            cp.wait_send()
        mine.wait()
        self._zero_tail(refs[1], refs[5], refs[6]).wait()


DW_PIECES = ((0, 256), (256, 640), (896, 1152))


def _local_step(ctx, x, target, modp, lower, wt_u, w_o, w9, conv_b, gate_b, hgw, mlw, ln_g, ln_b, exchange):
    tc = ctx.shape[0]
    tt = tc + x.shape[0]
    nbc, ncc = tc // ROWS, tc // CHUNK
    lower_f, lower_b = lower[0:1], lower[1:2]

    tmh = _pick(tt, (1088, 768, 512, 256))
    if exchange:
        hc, wt = _modulate_fwd(ctx, x, modp, gather=_RelayGather(wt_u))
        wt_u = wt.reshape(-1, D_MODEL)
        u, w_o = _mm(hc, wt_u, "nt", F32, tmh, 1152, D_MODEL, "mm_u", ride=_Ride("gather", w_o), n_rows=N_U)
        w_o = w_o.reshape(D_MODEL, D_MODEL)
    else:
        hc = _modulate_fwd(ctx, x, modp)
        u = _mm(hc, wt_u, "nt", F32, tmh, 1152, D_MODEL, "mm_u")
    cpre = _conv_fwd(u, w9, conv_b, tc)
    bias = jnp.pad(gate_b.reshape(1, 16), ((0, 0), (0, LANE - 16)))

    o_f, hist_f = _hgrn_fwd(u, lower_f, ncc, False)
    o_b, hist_b = _hgrn_fwd(u, lower_b, ncc, True)
    h_f, ch_f, nh_f, mh_f = _mlstm_fwd(cpre, u, bias, ncc, False)
    h_b, ch_b, nh_b, mh_b = _mlstm_fwd(cpre, u, bias, ncc, True)
    y = _post_fwd(o_f, o_b, h_f, h_b, u, hgw, mlw, nbc)
    dz, dxa, fsum = _final(y, w_o, x, target, modp, ln_g, ln_b)

    dw_o = _mm(y, dz, "tn", BF16, D_MODEL, 1024, _pick(y.shape[0], (1024, 512, 256)), "mm_dwo")
    do, dhm, daz, dbo, psum = _post_bwd(dz, w_o, o_f, o_b, h_f, h_b, u, hgw, mlw, nbc)
    if exchange:
        dzf_f, dzq, dv_a, dlb_f, dw_o = _hgrn_bwd(
            u, lower_f, hist_f, do, ncc, False, ride=_Ride("a2a", dw_o.reshape(N_DEV, D_MODEL // N_DEV, D_MODEL)))
    else:
        dzf_f, dzq, dv_a, dlb_f = _hgrn_bwd(u, lower_f, hist_f, do, ncc, False)
    du, dlb_b = _hgrn_bwd(u, lower_b, hist_b, do, ncc, True, final=(dzq, dv_a, dzf_f, daz))
    dqk, dv_m, dg, _ = _mlstm_bwd(cpre, u, bias, ch_f, nh_f, mh_f, h_f, dhm, ncc, False)
    dqk, du, gsum = _mlstm_bwd(cpre, u, bias, ch_b, nh_b, mh_b, h_b, dhm, ncc, True, final=(dqk, dv_m, dg, dbo, du))
    du, gconvw, gconvb = _conv_bwd(dqk, u, w9, tc, du)
    tkw = _pick(tt, (2176, 768, 512, 256))
    blocks = lambda g: g.reshape(N_DEV, N_IN // N_DEV, g.shape[1])
    dwu = lambda name, cols, ride: _mm(du, hc, "tn", BF16, 1152, cols[1], tkw, name, b_cols=cols, ride=ride, m_out=N_IN)
    dwt_a = dwu("mm_dwu_a", DW_PIECES[0], None)
    if exchange:
        whole = lambda piece, into: _Ride("a2a", blocks(piece[1]), cols=(piece[0][0], D_MODEL), into=into)
        dwt_b, got = dwu("mm_dwu_b", DW_PIECES[1], whole((DW_PIECES[0], dwt_a), None))
        dwt_c, got = dwu("mm_dwu_c", DW_PIECES[2], whole((DW_PIECES[1], dwt_b), got))
        dh, dwt_u = _mm(du, wt_u, "nn", F32, tmh, D_MODEL // 2, 3456, "mm_dh", ride=whole((DW_PIECES[2], dwt_c), got))
    else:
        dwt_u = jnp.concatenate([dwt_a, dwu("mm_dwu_b", DW_PIECES[1], None), dwu("mm_dwu_c", DW_PIECES[2], None)], axis=1)
        dh = _mm(du, wt_u, "nn", F32, tmh, D_MODEL // 2, 3456, "mm_dh")
    gx, msum = _modulate_bwd(dh, ctx, x, modp, dxa)

    zero_row = jnp.zeros((1, D_MODEL), F32)
    small = dict(
        dmodx=jnp.concatenate([msum[2:3], msum[3:4], fsum[0:1]], axis=0),
        dmodc=jnp.concatenate([msum[0:1], msum[1:2], zero_row], axis=0),
        gconvw=gconvw, gconvb=gconvb, dlower=jnp.concatenate([dlb_f, dlb_b], axis=0),
        ghgw=psum[0:1], gmlw=psum[1:2], glng=fsum[1:2], glnb=fsum[2:3], losssq=fsum[3:4],
        ggate=jnp.concatenate([gsum, jnp.zeros((7, LANE), F32)], axis=0))
    return gx, dwt_u, dw_o, small


def _pack_small(small):
    return jnp.concatenate([small[name].reshape(rows, LANE) for name, rows in _PACK], axis=0)


def _flat_pad(a, rows):
    flat = a.reshape(-1)
    return jnp.pad(flat, (0, rows * LANE - flat.shape[0])).reshape(rows, LANE)


def kernel(x, c, ctx, c_ctx, w_mod, b_mod, w_in, conv_w, conv_b, hg_lb, ml_gate_b, hg_norm_w, ml_norm_w, w_out, ln_g, ln_b, loss_target, m_c_ctx, m_w_mod, m_b_mod, m_w_in, m_conv_w, m_conv_b, m_hg_lb, m_ml_gate_b, m_hg_norm_w, m_ml_norm_w, m_w_out, m_ln_g, m_ln_b, v_c_ctx, v_w_mod, v_b_mod, v_w_in, v_conv_w, v_conv_b, v_hg_lb, v_ml_gate_b, v_hg_norm_w, v_ml_norm_w, v_w_out, v_ln_g, v_ln_b):
    px, py, pc = _position()
    me = 4 * px + 2 * py + pc
    d = D_MODEL
    n_mod = w_mod.shape[2]
    n_cv = conv_w.shape[3]
    n_lb = hg_lb.shape[2]

    pack0 = jnp.concatenate([c.reshape(-1), conv_w.reshape(-1), hg_lb.reshape(-1)]).reshape(1, -1)
    g0 = _all_gather(pack0, "gather_small_inputs")[:, 0, :]
    c_all = g0[:, :d]
    w9 = jnp.transpose(g0[:, d:d + 9 * n_cv].reshape(N_DEV, 9, n_cv), (1, 0, 2)).reshape(9, N_DEV * n_cv)
    lb4 = jnp.transpose(g0[:, d + 9 * n_cv:].reshape(N_DEV, 4, n_lb), (1, 0, 2)).reshape(4, N_DEV * n_lb)
    lower = _lower_fwd(lb4)

    cs = jnp.concatenate([c_all, c_ctx.reshape(1, d), jnp.zeros((7, d), F32)], axis=0)
    b_cols = lax.dynamic_slice(b_mod, (0, me * n_mod), (1, n_mod))
    slab = _mod_fwd(cs, w_mod[0], b_cols)
    mod_all = jnp.transpose(_all_gather(slab, "gather_mod"), (1, 0, 2)).reshape(16, N_DEV * n_mod)
    mod_x = lax.dynamic_slice(mod_all, (me, 0), (1, 3 * d)).reshape(3, d)
    modp = jnp.stack([mod_all[8].reshape(3, d), mod_x])

    gx, recv_wi, recv_wo, small = _local_step(ctx[0], x[0], loss_target[0], modp, lower, w_in[0].T.astype(BF16),
                                              w_out[0].astype(BF16), w9, conv_b, ml_gate_b[0], hg_norm_w, ml_norm_w,
                                              ln_g, ln_b, True)
    g_wi, d_wi, nm_wi, nv_wi = [a.T for a in _rs_adamw(recv_wi, w_in[0].T, m_w_in[0].T, v_w_in[0].T, 256,
                                                       "adamw_w_in", by_cols=True)]
    g_wo, d_wo, nm_wo, nv_wo = _rs_adamw(recv_wo, w_out[0], m_w_out[0], v_w_out[0], 64, "adamw_w_out")

    packs = _all_gather(_pack_small(small), "gather_small_grads")
    total = _reduce8(packs, "reduce_small_grads")
    offs = _pack_offsets()
    piece = lambda name: total[offs[name][0]:offs[name][0] + offs[name][1]]
    g_bmod, g_lb0, g_lb1, loss8 = _small_finish(total, lower[0:2].reshape(16, LANE), float(d))

    ox = offs["dmodx"][0]
    dmodx_all = packs[:, ox:ox + 48, :].reshape(N_DEV, 3 * d)
    dmodc_tot = piece("dmodc").reshape(1, 3 * d)
    d9 = jnp.concatenate([dmodx_all, dmodc_tot, jnp.zeros((7, 3 * d), F32)], axis=0)
    d9_cols = lax.dynamic_slice(d9, (0, me * n_mod), (16, n_mod))
    g_wmod, pc_part = _mod_bwd_w(cs, d9_cols, w_mod[0])
    c_ctx8 = jnp.concatenate([c_ctx.reshape(1, d), jnp.zeros((7, d), F32)], axis=0)
    g_cctx = _cctx_grad(_all_gather(pc_part, "gather_cctx"), c_ctx8)[0]
    d_wmod, nm_wmod, nv_wmod = _adamw(w_mod[0], g_wmod, m_w_mod[0], v_w_mod[0], 256, "adamw_w_mod")

    g_convw_full = piece("gconvw").reshape(9, d)
    g_convw = lax.dynamic_slice(g_convw_full, (0, me * n_cv), (9, n_cv)).reshape(conv_w.shape)
    lb_full = jnp.stack([jnp.stack([g_lb0[0:8].reshape(-1), g_lb1[0:8].reshape(-1)]),
                         jnp.stack([g_lb0[8:16].reshape(-1), g_lb1[8:16].reshape(-1)])])
    g_hglb = lax.dynamic_slice(lb_full, (0, 0, me * n_lb), (2, 2, n_lb))
    grads = dict(
        c_ctx=g_cctx, b_mod=g_bmod.reshape(b_mod.shape), conv_w=g_convw, conv_b=piece("gconvb").reshape(conv_b.shape),
        hg_lb=g_hglb, ml_gate_b=piece("ggate")[0, :16].reshape(ml_gate_b.shape),
        hg_norm_w=piece("ghgw").reshape(hg_norm_w.shape), ml_norm_w=piece("gmlw").reshape(ml_norm_w.shape),
        ln_g=piece("glng").reshape(ln_g.shape), ln_b=piece("glnb").reshape(ln_b.shape))
    params = dict(c_ctx=(c_ctx, m_c_ctx, v_c_ctx), b_mod=(b_mod, m_b_mod, v_b_mod), conv_w=(conv_w, m_conv_w, v_conv_w),
                  conv_b=(conv_b, m_conv_b, v_conv_b), hg_lb=(hg_lb, m_hg_lb, v_hg_lb),
                  ml_gate_b=(ml_gate_b, m_ml_gate_b, v_ml_gate_b), hg_norm_w=(hg_norm_w, m_hg_norm_w, v_hg_norm_w),
                  ml_norm_w=(ml_norm_w, m_ml_norm_w, v_ml_norm_w), ln_g=(ln_g, m_ln_g, v_ln_g), ln_b=(ln_b, m_ln_b, v_ln_b))
    names = list(params)
    rows_of = {n: -(-params[n][0].size // LANE) for n in names}
    rows_tot = -(-sum(rows_of.values()) // 8) * 8
    cat = lambda arrs: jnp.concatenate(
        [_flat_pad(a, rows_of[n]) for n, a in zip(names, arrs)]
        + [jnp.ones((rows_tot - sum(rows_of.values()), LANE), F32)], axis=0)
    d_s, m_s, v_s = _adamw(cat([params[n][0] for n in names]), cat([grads[n] for n in names]),
                           cat([params[n][1] for n in names]), cat([params[n][2] for n in names]), rows_tot, "adamw_small")
    delta, new_m, new_v, off = {}, {}, {}, 0
    for n in names:
        shape, size = params[n][0].shape, params[n][0].size
        take = lambda a: a[off:off + rows_of[n]].reshape(-1)[:size].reshape(shape)
        delta[n], new_m[n], new_v[n] = take(d_s), take(m_s), take(v_s)
        off += rows_of[n]
    grads.update(w_mod=g_wmod[None], w_in=g_wi[None], w_out=g_wo[None])
    delta.update(w_mod=d_wmod[None], w_in=d_wi[None], w_out=d_wo[None])
    new_m.update(w_mod=nm_wmod[None], w_in=nm_wi[None], w_out=nm_wo[None])
    new_v.update(w_mod=nv_wmod[None], w_in=nv_wi[None], w_out=nv_wo[None])

    order = ("c_ctx", "w_mod", "b_mod", "w_in", "conv_w", "conv_b", "hg_lb", "ml_gate_b", "hg_norm_w", "ml_norm_w",
             "w_out", "ln_g", "ln_b")
    return (loss8[0, 0], gx[None], *[grads[n] for n in order], *[delta[n] for n in order],
            *[new_m[n] for n in order], *[new_v[n] for n in order])
```

```python
import jax
import jax.numpy as jnp
from jax import lax
from jax.experimental import pallas as pl
from jax.experimental.pallas import tpu as pltpu

F32 = jnp.float32
BF16 = jnp.bfloat16

D_MODEL = 2048
W_A = 1024
W_B = 1024
HG_HEADS = 8
HG_D = 128
ML_HEADS = 4
ML_D = 256
CHUNK = 64
N_IN = 10256
LANE = 128
N_U = 81 * LANE
N_DEV = 8
ALPHA = 2.0 ** 0.25
LN_EPS = 1e-5
NORM_EPS = 1e-6
ADAM_LR, ADAM_B1, ADAM_B2, ADAM_EPS, ADAM_WD, ADAM_STEP = 0.001, 0.9, 0.999, 1e-08, 0.01, 10
VMEM_CAP = 60 * 1024 * 1024

SEG_AQ, SEG_AFF, SEG_AFB, SEG_AI, SEG_AZ = range(5)
BLK_QK = 40
SEG_BV, SEG_BO, SEG_BZ = 7, 8, 9
BLK_GATE = 80

MESH = pl.DeviceIdType.MESH


def _vmem(nbytes):
    return pltpu.CompilerParams(vmem_limit_bytes=int(min(VMEM_CAP, max(nbytes, 16 * 1024 * 1024))))


def _sigmoid(x):
    return 1.0 / (1.0 + jnp.exp(-x))


def _silu(x):
    return x * _sigmoid(x)


def _dsilu(x):
    s = _sigmoid(x)
    return s * (1.0 + x * (1.0 - s))


def _silu_both(x):
    s = _sigmoid(x)
    return x * s, s * (1.0 + x * (1.0 - s))


def _bdot(a, b, dims):
    return lax.dot_general(a.astype(BF16), b.astype(BF16), (dims, ((), ())), preferred_element_type=F32)


def _nn(a, b):
    return _bdot(a, b, ((1,), (0,)))


def _nt(a, b):
    return _bdot(a, b, ((1,), (1,)))


def _tn(a, b):
    return _bdot(a, b, ((0,), (0,)))


def _exact_nn(a, b):
    return lax.dot_general(a, b, (((1,), (0,)), ((), ())), precision=lax.Precision.HIGHEST,
                           preferred_element_type=F32)


def _exact_tn(a, b):
    return lax.dot_general(a, b, (((0,), (0,)), ((), ())), precision=lax.Precision.HIGHEST,
                           preferred_element_type=F32)


def _tri(rev):
    t = lax.broadcasted_iota(jnp.int32, (CHUNK, CHUNK), 0)
    s = lax.broadcasted_iota(jnp.int32, (CHUNK, CHUNK), 1)
    return (s >= t) if rev else (s <= t)


def _eye():
    t = lax.broadcasted_iota(jnp.int32, (CHUNK, CHUNK), 0)
    s = lax.broadcasted_iota(jnp.int32, (CHUNK, CHUNK), 1)
    return (s == t).astype(F32)


def _row_to_col(row):
    return jnp.sum(_eye() * row, axis=1, keepdims=True)


def _last_onehot(rev):
    t = lax.broadcasted_iota(jnp.int32, (CHUNK, 1), 0)
    return (t == (0 if rev else CHUNK - 1)).astype(F32)


def _head_slices(width, n_heads):
    hd = width // n_heads
    return [slice(h * hd, (h + 1) * hd) for h in range(n_heads)]


def _scan_sum(x, rev):
    tri = _tri(rev).astype(BF16)
    hi = x.astype(BF16)
    rest = x - hi.astype(F32)
    mid = rest.astype(BF16)
    lo = (rest - mid.astype(F32)).astype(BF16)
    dot = lambda p: lax.dot_general(tri, p, (((1,), (0,)), ((), ())), preferred_element_type=F32)
    return dot(hi) + (dot(mid) + dot(lo))


def _dot3(a, b, dims):
    a_hi, b_hi = a.astype(BF16), b.astype(BF16)
    a_lo, b_lo = (a - a_hi.astype(F32)).astype(BF16), (b - b_hi.astype(F32)).astype(BF16)
    dot = lambda x, y: lax.dot_general(x, y, (dims, ((), ())), preferred_element_type=F32)
    return dot(a_hi, b_hi) + (dot(a_hi, b_lo) + dot(a_lo, b_hi))


def _hg_common(zq, zf, lb, rev):
    q, dq_dz = _silu_both(zq)
    sg = _sigmoid(zf)
    f = lb + (1.0 - lb) * sg
    g = jnp.log(f)
    k = 1.0 - f
    b = _scan_sum(g, rev)
    b_last = jnp.sum(g, axis=0, keepdims=True)
    r = b[CHUNK // 2:CHUNK // 2 + 1, :]
    e_up = jnp.exp(b - r)
    e_dn = jnp.exp(r - b)
    e_b = e_up * jnp.exp(r)
    e_lb = e_dn * jnp.exp(b_last - r)
    return dict(q=q, dq_dz=dq_dz, sg=sg, f=f, k=k, e_up=e_up, e_dn=e_dn, e_b=e_b, e_lb=e_lb, e_last=jnp.exp(b_last),
                q_t=q * e_up, k_t=k * e_dn, q_s=q * e_b, k_h=k * e_lb, tri=_tri(rev).astype(F32))


def hg_chunk_fwd(zq, zf, v, lb, st, rev):
    hs = _head_slices(zq.shape[1], zq.shape[1] // HG_D)
    cs = [_hg_common(zq[:, sl], zf[:, sl], lb[:, sl], rev) for sl in hs]
    s = [_nt(c["q_t"], c["k_t"]) for c in cs]
    oi = [_nt(c["q_s"], st[sl, :]) for c, sl in zip(cs, hs)]
    ds = [_tn(v[:, sl], c["k_h"]) for c, sl in zip(cs, hs)]
    oa = [_nn(c["tri"] * s_h, v[:, sl]) for c, s_h, sl in zip(cs, s, hs)]
    o = jnp.concatenate([x + y for x, y in zip(oi, oa)], axis=1)
    st_new = jnp.concatenate([st[sl, :] * c["e_last"] + d for c, sl, d in zip(cs, hs, ds)], axis=0)
    return o, st_new


def hg_chunk_bwd(zq, zf, v, lb, st, do, dst_new, rev):
    hs = _head_slices(zq.shape[1], zq.shape[1] // HG_D)
    cs = [_hg_common(zq[:, sl], zf[:, sl], lb[:, sl], rev) for sl in hs]
    tri = cs[0]["tri"]
    s = [_nt(c["q_t"], c["k_t"]) for c in cs]
    da = [tri * _nt(do[:, sl], v[:, sl]) for sl in hs]
    dq_s = [_nn(do[:, sl], st[sl, :]) for sl in hs]
    dk_h = [_nn(v[:, sl], dst_new[sl, :]) for sl in hs]
    dv_s = [_nt(c["k_h"], dst_new[sl, :]) for c, sl in zip(cs, hs)]
    dst_q = [_tn(do[:, sl], c["q_s"]) for c, sl in zip(cs, hs)]
    dq_t = [_dot3(da_h, c["k_t"], ((1,), (0,))) for da_h, c in zip(da, cs)]
    dk_t = [_dot3(da_h, c["q_t"], ((0,), (0,))) for da_h, c in zip(da, cs)]
    dv_a = [_tn(tri * s_h, do[:, sl]) for s_h, sl in zip(s, hs)]
    dzq, dzf, dlb, dst = [], [], [], []
    for h, (c, sl) in enumerate(zip(cs, hs)):
        dq = dq_s[h] * c["e_b"] + dq_t[h] * c["e_up"]
        dk = dk_t[h] * c["e_dn"] + dk_h[h] * c["e_lb"]
        db = c["q"] * dq - c["k"] * dk
        ss = jnp.sum(dst_new[sl, :] * st[sl, :], axis=0, keepdims=True)
        d_all = jnp.sum(dk_h[h] * c["k_h"], axis=0, keepdims=True) + c["e_last"] * ss
        dg = _scan_sum(db, not rev) + d_all
        df = dg / c["f"] - dk
        dzq.append(dq * c["dq_dz"])
        dzf.append(df * (1.0 - lb[:, sl]) * c["sg"] * (1.0 - c["sg"]))
        dlb.append(jnp.sum(df * (1.0 - c["sg"]), axis=0, keepdims=True))
        dst.append(dst_new[sl, :] * c["e_last"] + dst_q[h])
    cat = lambda parts: jnp.concatenate(parts, axis=1)
    dv = cat([x + y for x, y in zip(dv_a, dv_s)])
    return cat(dzq), cat(dzf), dv, cat(dlb), jnp.concatenate(dst, axis=0)


def _log_sigmoid(x):
    return jnp.minimum(x, 0.0) - jnp.log(1.0 + jnp.exp(-jnp.abs(x)))


def _each(fn, *lists):
    return [fn(*xs) for xs in zip(*lists)]


def _bf(xs):
    return [x.astype(BF16) for x in xs]


def _ml_forward_parts(qp, kp, v, gates, c, n, m, rev, with_num):
    hs = _head_slices(qp.shape[1], qp.shape[1] // ML_D)
    q_all, dq_dp = _silu_both(qp)
    k_all, dk_dp = _silu_both(kp)
    k_all = k_all * (ML_D ** -0.5)
    q = [q_all[:, sl] for sl in hs]
    k = [k_all[:, sl] for sl in hs]
    vv = [v[:, sl] for sl in hs]
    cc = [c[sl, :] for sl in hs]
    tri_b = _tri(rev)
    tri = tri_b.astype(F32)
    tri_t = _tri(not rev).astype(F32)
    e_last = _last_onehot(rev)
    qb, kb, vb, cb = _bf(q), _bf(k), _bf(vv), _bf(cc)
    qk = _each(_nt, qb, kb)
    parts = []
    for (gi_c, gi_r, gf_c, gf_r), m_h in zip(gates, m):
        lf_c, lf_r = _log_sigmoid(gf_c), _log_sigmoid(gf_r)
        b_c = jnp.sum(tri * lf_r, axis=1, keepdims=True)
        b_r = jnp.sum(tri_t * lf_c, axis=0, keepdims=True)
        log_w = jnp.where(tri_b, b_c - b_r + gi_r, -jnp.inf)
        m_inter = b_c + m_h
        m_t = jnp.maximum(m_inter, jnp.max(log_w, axis=1, keepdims=True))
        m_new = jnp.sum(m_t * e_last, axis=0, keepdims=True)
        b_last = jnp.sum(b_c * e_last, axis=0, keepdims=True)
        parts.append(dict(a=jnp.exp(m_inter - m_t), p=jnp.exp(log_w - m_t), floor=jnp.exp(-m_t), m_new=m_new,
                          ws=jnp.exp(b_last - b_c + gi_c - m_new), decay=jnp.exp(b_last + m_h - m_new), gf_c=gf_c))
    w = [pt["p"] * x for pt, x in zip(parts, qk)]
    wb = _bf(w)
    for pt, q_h, n_h, w_h in zip(parts, q, n, w):
        qn = jnp.sum(q_h * n_h, axis=1, keepdims=True)
        den = pt["a"] * qn + jnp.sum(w_h, axis=1, keepdims=True)
        pt.update(qn=qn, den=den, rinv=1.0 / jnp.maximum(jnp.abs(den), pt["floor"]), w=w_h)
    if with_num:
        qc = _each(_nt, qb, cb)
        wv = _each(_nn, wb, vb)
        for pt, qc_h, wv_h in zip(parts, qc, wv):
            pt.update(num=pt["a"] * qc_h + wv_h)
    return hs, q, k, vv, cc, tri, parts, dict(q=qb, k=kb, v=vb, c=cb, w=wb, dq_dp=dq_dp, dk_dp=dk_dp)


def ml_chunk_fwd(qp, kp, v, gates, c, n, m, rev):
    hs, q, k, vv, cc, tri, parts, bf = _ml_forward_parts(qp, kp, v, gates, c, n, m, rev, True)
    h = jnp.concatenate([pt["num"] * pt["rinv"] for pt in parts], axis=1)
    upd = _each(_tn, [pt["ws"] * v_h for pt, v_h in zip(parts, vv)], bf["k"])
    c_new = jnp.concatenate([pt["decay"] * c_h + u for pt, c_h, u in zip(parts, cc, upd)], axis=0)
    n_new = [pt["decay"] * n_h + jnp.sum(pt["ws"] * k_h, axis=0, keepdims=True) for pt, n_h, k_h in zip(parts, n, k)]
    return h, c_new, n_new, [pt["m_new"] for pt in parts]


def ml_chunk_bwd(qp, kp, v, gates, c, n, m, h_out, dh, dc_new, dn_new, rev):
    hs, q, k, vv, cc, tri, parts, bf = _ml_forward_parts(qp, kp, v, gates, c, n, m, rev, False)
    dcn = [dc_new[sl, :] for sl in hs]
    dcb = _bf(dcn)
    dnum, dden = [], []
    for pt, sl in zip(parts, hs):
        dh_h = dh[:, sl]
        signed_live = jnp.where(jnp.abs(pt["den"]) > pt["floor"], jnp.where(pt["den"] >= 0.0, 1.0, -1.0), 0.0)
        dnum.append(dh_h * pt["rinv"])
        dden.append(-jnp.sum(dh_h * h_out[:, sl], axis=1, keepdims=True) * pt["rinv"] * signed_live)
    dnb = _bf(dnum)
    dw = [x + y for x, y in zip(_each(_nt, dnb, bf["v"]), dden)]
    kdc = _each(_nt, bf["k"], dcb)
    vdc = _each(_nn, bf["v"], dcb)
    dqk = [x * pt["p"] for x, pt in zip(dw, parts)]
    adn = [pt["a"] * x for pt, x in zip(parts, dnum)]
    dqkb, adnb = _bf(dqk), _bf(adn)
    dv_w = _each(_tn, bf["w"], dnb)
    dq_k = _each(_nn, dqkb, bf["k"])
    dq_c = _each(_nn, adnb, bf["c"])
    dk_q = _each(_tn, dqkb, bf["q"])
    dc_q = _each(_tn, adnb, bf["q"])
    dq, dk, dv, dgi, dgf, dc, dn = [], [], [], [], [], [], []
    for i, pt in enumerate(parts):
        a, ws, decay = pt["a"], pt["ws"], pt["decay"]
        add = a * dden[i]
        e = dw[i] * pt["w"]
        dv.append(dv_w[i] + ws * kdc[i])
        dq.append(dq_k[i] + dq_c[i] + add * n[i])
        dk.append(dk_q[i] + ws * vdc[i] + ws * dn_new[i])
        alpha = jnp.sum(q[i] * dq_c[i], axis=1, keepdims=True) + dden[i] * pt["qn"] * a
        omega = (jnp.sum(vdc[i] * k[i], axis=1, keepdims=True) + jnp.sum(k[i] * dn_new[i], axis=1, keepdims=True)) * ws
        delta = decay * (jnp.sum(jnp.sum(dcn[i] * cc[i], axis=1, keepdims=True), axis=0, keepdims=True)
                         + jnp.sum(dn_new[i] * n[i], axis=1, keepdims=True))
        dc.append(decay * dcn[i] + dc_q[i])
        dn.append(decay * dn_new[i] + jnp.sum(add * q[i], axis=0, keepdims=True))
        e_rows = jnp.sum(e, axis=1, keepdims=True)
        e_cols = _row_to_col(jnp.sum(e, axis=0, keepdims=True))
        dgi.append(e_cols + omega)
        db = e_rows + alpha - e_cols - omega
        tail = jnp.sum(omega, axis=0, keepdims=True) + delta
        dlf = _row_to_col(jnp.sum(tri * db, axis=0, keepdims=True)) + tail
        dgf.append(dlf * (1.0 - _sigmoid(pt["gf_c"])))
    cat = lambda xs: jnp.concatenate(xs, axis=1)
    dqp = cat(dq) * bf["dq_dp"]
    dkp = cat(dk) * (ML_D ** -0.5) * bf["dk_dp"]
    return dqp, dkp, cat(dv), dgi, dgf, jnp.concatenate(dc, axis=0), dn


def _pick(n, prefs):
    for p in prefs:
        if n % p == 0:
            return p
    raise ValueError(f"no tile for {n} among {prefs}")


def _position():
    return lax.axis_index("x"), lax.axis_index("y"), lax.axis_index("c")


class _Ride:
    def __init__(self, kind, x, cols=None, into=None):
        self.kind, self.x, self.cols, self.into = kind, x, cols, into
        r, c = x.shape[-2:]
        self.out_shape = jax.ShapeDtypeStruct((N_DEV, r, c if cols is None else cols[1]), x.dtype)
        self.width = c

    def _copies(self, x_ref, out_ref, send_sems, recv_sems, local_sem):
        px, py, pc = _position()
        me = 4 * px + 2 * py + pc
        src = (lambda slot: x_ref) if self.kind == "gather" else (lambda slot: x_ref.at[slot])
        dst = ((lambda slot: out_ref.at[slot]) if self.cols is None
               else (lambda slot: out_ref.at[slot, :, pl.ds(self.cols[0], self.width)]))
        mine = pltpu.make_async_copy(src(me), dst(me), local_sem)
        sends, recvs = [], []
        for k, (fx, fy, fc) in enumerate([(1, 0, 0), (0, 1, 0), (1, 1, 0), (1, 0, 1), (0, 1, 1), (1, 1, 1), (0, 0, 1)]):
            qx, qy, qc = (1 - px if fx else px), (1 - py if fy else py), (1 - pc if fc else pc)
            peer = 4 * qx + 2 * qy + qc
            sends.append(pltpu.make_async_remote_copy(
                src_ref=src(peer), dst_ref=dst(me), send_sem=send_sems.at[k], recv_sem=recv_sems.at[k],
                device_id=(qx, qy, qc), device_id_type=MESH))
            recvs.append(pltpu.make_async_remote_copy(
                src_ref=src(me), dst_ref=dst(peer), send_sem=send_sems.at[k], recv_sem=recv_sems.at[k],
                device_id=(qx, qy, qc), device_id_type=MESH))
        return mine, sends, recvs

    def start(self, *refs):
        mine, sends, _ = self._copies(*refs)
        mine.start()
        for cp in sends:
            cp.start()

    def wait(self, *refs):
        mine, sends, recvs = self._copies(*refs)
        for cp in recvs:
            cp.wait_recv()
        for cp in sends:
            cp.wait_send()
        mine.wait()

    def operands(self):
        return [self.x] + ([self.into] if self.into is not None else [])


_RIDE_SCRATCH = [pltpu.SemaphoreType.DMA((7,)), pltpu.SemaphoreType.DMA((7,)), pltpu.SemaphoreType.DMA]
_ANY = pl.BlockSpec(memory_space=pl.ANY)


def _mm(a, b, mode, out_dtype, tm, tn, tk, name, ride=None, b_cols=None, m_out=None, n_rows=None):
    if mode == "nn":
        (m, k), (k2, n) = a.shape, b.shape
    elif mode == "nt":
        (m, k), (n, k2) = a.shape, b.shape
        n = n if n_rows is None else n_rows
    else:
        (k, m), (k2, n) = a.shape, b.shape
    off, n = (0, n) if b_cols is None else b_cols
    assert (k2 >= k if mode == "nn" else k == k2) and m % tm == 0 and n % tn == 0 and k % tk == 0, \
        (a.shape, b.shape, mode, tm, tn, tk)
    assert b_cols is None or (mode != "nt" and off % LANE == 0)
    nk = k // tk
    dims = {"nn": ((1,), (0,)), "nt": ((1,), (1,)), "tn": ((0,), (0,))}[mode]
    a_spec = (pl.BlockSpec((tk, tm), lambda j, i, kk: (kk, i)) if mode == "tn"
              else pl.BlockSpec((tm, tk), lambda j, i, kk: (i, kk)))
    if b_cols is not None:
        b_spec = pl.BlockSpec((pl.Element(tk), pl.Element(tn)),
                              lambda j, i, kk: (pl.multiple_of(kk * tk, LANE), pl.multiple_of(off + j * tn, LANE)))
    elif mode == "nt":
        b_spec = pl.BlockSpec((tn, tk), lambda j, i, kk: (j, kk))
    else:
        b_spec = pl.BlockSpec((tk, tn), lambda j, i, kk: (kk, j))

    grid = (n // tn, m // tm, nk)
    n_ride_in = len(ride.operands()) if ride is not None else 0

    def body(a_ref, b_ref, *rest):
        if ride is not None:
            x_ref = rest[0]
            o_ref, got_ref, acc_ref = rest[n_ride_in:n_ride_in + 3]
            comm = (x_ref, got_ref) + tuple(rest[n_ride_in + 3:])
        else:
            o_ref, acc_ref = rest
        kk = pl.program_id(2)
        step = (pl.program_id(0) * grid[1] + pl.program_id(1)) * nk + kk
        if ride is not None:
            @pl.when(step == 0)
            def _():
                ride.start(*comm)

        part = lax.dot_general(a_ref[...], b_ref[...], (dims, ((), ())), preferred_element_type=F32)
        if nk == 1:
            o_ref[...] = part.astype(o_ref.dtype)
        else:
            @pl.when(kk == 0)
            def _():
                acc_ref[...] = part

            @pl.when(jnp.logical_and(kk > 0, kk < nk - 1))
            def _():
                acc_ref[...] += part

            @pl.when(kk == nk - 1)
            def _():
                o_ref[...] = (acc_ref[...] + part).astype(o_ref.dtype)

        if ride is not None:
            @pl.when(step == grid[0] * grid[1] * nk - 1)
            def _():
                ride.wait(*comm)

    osz = jnp.dtype(out_dtype).itemsize
    need = 2 * (tm * tk * a.dtype.itemsize + tk * tn * b.dtype.itemsize + tm * tn * osz) + tm * tn * 4
    o_spec = pl.BlockSpec((tm, tn), lambda j, i, kk: (i, j))
    o_shape = jax.ShapeDtypeStruct((m if m_out is None else m_out, n), out_dtype)
    extra = ride is not None
    return pl.pallas_call(
        body, name=name, grid=grid,
        in_specs=[a_spec, b_spec] + [_ANY] * n_ride_in,
        out_specs=[o_spec, _ANY] if extra else o_spec,
        out_shape=[o_shape, ride.out_shape] if extra else o_shape,
        scratch_shapes=[pltpu.VMEM((tm, tn) if nk > 1 else (8, LANE), F32)] + (_RIDE_SCRATCH if extra else []),
        input_output_aliases={3: 1} if extra and ride.into is not None else {},
        compiler_params=_vmem(need + (12 << 20)),
    )(a, b, *(ride.operands() if extra else []))


ROWS = 256


def _ln_stats(x):
    mu = jnp.mean(x, axis=-1, keepdims=True)
    xc = x - mu
    var = jnp.mean(xc * xc, axis=-1, keepdims=True)
    rstd = lax.rsqrt(var + LN_EPS)
    return xc * rstd, rstd


def _token_specs(nbc, nbx, d):
    return [pl.BlockSpec((ROWS, d), lambda i: (jnp.minimum(i, nbc - 1), 0)),
            pl.BlockSpec((ROWS, d), lambda i: (jnp.maximum(i - nbc, 0), 0))]


def _tokens(c_ref, x_ref, nbc):
    return jnp.where(pl.program_id(0) < nbc, c_ref[...], x_ref[...])


def _modulate_fwd(ctx, x, modp, gather=None):
    d = x.shape[1]
    nbc, nbx = ctx.shape[0] // ROWS, x.shape[0] // ROWS
    riding = gather is not None

    def body(c_ref, x_ref, mod_ref, *rest):
        if riding:
            comm = (rest[0], rest[2]) + tuple(rest[3:])
            o_ref = rest[1]

            @pl.when(pl.program_id(0) == 0)
            def _():
                gather.start(*comm)
        else:
            o_ref = rest[0]
        n, _ = _ln_stats(_tokens(c_ref, x_ref, nbc))
        o_ref[...] = (n * (1.0 + mod_ref[0, 1:2, :]) + mod_ref[0, 0:1, :]).astype(BF16)
        if riding:
            @pl.when(pl.program_id(0) == nbc + nbx - 1)
            def _():
                gather.finish(*comm)

    o_spec = pl.BlockSpec((ROWS, d), lambda i: (i, 0))
    o_shape = jax.ShapeDtypeStruct((ctx.shape[0] + x.shape[0], d), BF16)
    return pl.pallas_call(
        body, name="modulate_fwd", grid=(nbc + nbx,),
        in_specs=_token_specs(nbc, nbx, d) + [pl.BlockSpec((1, 3, d), lambda i: (jnp.where(i >= nbc, 1, 0), 0, 0))]
        + ([_ANY] if riding else []),
        out_specs=[o_spec, _ANY] if riding else o_spec,
        out_shape=[o_shape, gather.out_shape] if riding else o_shape,
        scratch_shapes=gather.scratch if riding else [],
    )(ctx, x, modp, *([gather.x] if riding else []))


def _modulate_bwd(dh, ctx, x, modp, dxa):
    t, d = x.shape
    nbc, nbx = ctx.shape[0] // ROWS, t // ROWS

    def body(dh_ref, c_ref, x_ref, mod_ref, dxa_ref, gx_ref, sum_ref):
        i = pl.program_id(0)
        n, rstd = _ln_stats(_tokens(c_ref, x_ref, nbc))
        g = dh_ref[...]
        dn = g * (1.0 + mod_ref[0, 1:2, :])
        dx = rstd * (dn - jnp.mean(dn, axis=-1, keepdims=True) - n * jnp.mean(dn * n, axis=-1, keepdims=True))
        gx_ref[...] = dx + dxa_ref[...]
        dshift = jnp.sum(g, axis=0, keepdims=True)
        dscale = jnp.sum(g * n, axis=0, keepdims=True)

        @pl.when(i == 0)
        def _():
            sum_ref[...] = jnp.zeros_like(sum_ref)

        @pl.when(i < nbc)
        def _():
            sum_ref[0:1, :] += dshift
            sum_ref[1:2, :] += dscale

        @pl.when(i >= nbc)
        def _():
            sum_ref[2:3, :] += dshift
            sum_ref[3:4, :] += dscale

    lat = lambda i: (jnp.maximum(i - nbc, 0), 0)
    return pl.pallas_call(
        body, name="modulate_bwd", grid=(nbc + nbx,),
        in_specs=[pl.BlockSpec((ROWS, d), lambda i: (i, 0))] + _token_specs(nbc, nbx, d)
        + [pl.BlockSpec((1, 3, d), lambda i: (jnp.where(i >= nbc, 1, 0), 0, 0)), pl.BlockSpec((ROWS, d), lat)],
        out_specs=[pl.BlockSpec((ROWS, d), lat), pl.BlockSpec((8, d), lambda i: (0, 0))],
        out_shape=[jax.ShapeDtypeStruct((t, d), F32), jax.ShapeDtypeStruct((8, d), F32)],
    )(dh, ctx, x, modp, dxa)


def _post_fwd(o_f, o_b, h_f, h_b, u, hgw, mlw, nbc):
    tt = u.shape[0]
    t = tt - nbc * ROWS

    def body(of_ref, ob_ref, hf_ref, hb_ref, az_ref, bo_ref, bz_ref, hgw_ref, mlw_ref, y_ref):
        o = of_ref[...] + ob_ref[...]
        for sl in _head_slices(W_A, HG_HEADS):
            oh = o[:, sl]
            rs = lax.rsqrt(jnp.mean(oh * oh, axis=-1, keepdims=True) + NORM_EPS)
            y_ref[:, sl] = (oh * rs * hgw_ref[:, sl] * _silu(az_ref[:, sl])).astype(BF16)
        hm = hf_ref[...] + hb_ref[...]
        for sl in _head_slices(W_B, ML_HEADS):
            hh = hm[:, sl]
            mu = jnp.mean(hh, axis=-1, keepdims=True)
            hc = hh - mu
            rstd = lax.rsqrt(jnp.mean(hc * hc, axis=-1, keepdims=True) + NORM_EPS)
            out = hc * rstd * mlw_ref[:, sl] * _sigmoid(bo_ref[:, sl]) * _silu(bz_ref[:, sl])
            y_ref[:, W_A + sl.start:W_A + sl.stop] = out.astype(BF16)

    row = lambda i: (i + nbc, 0)
    seg = lambda s: pl.BlockSpec((ROWS, 1024), lambda i: (i + nbc, s))
    wspec = pl.BlockSpec((1, 1024), lambda i: (0, 0))
    return pl.pallas_call(
        body, name="post_fwd", grid=(t // ROWS,),
        in_specs=[pl.BlockSpec((ROWS, 1024), row)] * 4 + [seg(SEG_AZ), seg(SEG_BO), seg(SEG_BZ), wspec, wspec],
        out_specs=pl.BlockSpec((ROWS, 2048), lambda i: (i, 0)),
        out_shape=jax.ShapeDtypeStruct((t, 2048), BF16),
    )(o_f, o_b, h_f, h_b, u, u, u, hgw, mlw)


def _post_bwd(dz, w_o, o_f, o_b, h_f, h_b, u, hgw, mlw, nbc):
    tt = u.shape[0]
    d = w_o.shape[0]

    def body(dz_ref, w_ref, of_ref, ob_ref, hf_ref, hb_ref, az_ref, bo_ref, bz_ref, hgw_ref, mlw_ref,
             do_ref, dhm_ref, daz_ref, dbo_ref, sum_ref):
        i = pl.program_id(0)
        live = jnp.where(i >= nbc, 1.0, 0.0)
        dy = lax.dot_general(dz_ref[...], w_ref[...], (((1,), (1,)), ((), ())), preferred_element_type=F32) * live

        @pl.when(i == 0)
        def _():
            sum_ref[...] = jnp.zeros_like(sum_ref)

        o = of_ref[...] + ob_ref[...]
        for sl in _head_slices(W_A, HG_HEADS):
            oh = o[:, sl]
            rs = lax.rsqrt(jnp.mean(oh * oh, axis=-1, keepdims=True) + NORM_EPS)
            on = oh * rs
            az = az_ref[:, sl]
            dya = dy[:, sl]
            saz, daz = _silu_both(az)
            doa = dya * saz
            daz_ref[:, sl] = (dya * on * hgw_ref[:, sl] * daz).astype(BF16)
            sum_ref[0:1, sl] += jnp.sum(doa * on, axis=0, keepdims=True)
            don = doa * hgw_ref[:, sl]
            do_ref[:, sl] = rs * (don - on * jnp.mean(don * on, axis=-1, keepdims=True))
        hm = hf_ref[...] + hb_ref[...]
        for sl in _head_slices(W_B, ML_HEADS):
            hh = hm[:, sl]
            mu = jnp.mean(hh, axis=-1, keepdims=True)
            hc = hh - mu
            rstd = lax.rsqrt(jnp.mean(hc * hc, axis=-1, keepdims=True) + NORM_EPS)
            hn = hc * rstd
            hw = hn * mlw_ref[:, sl]
            bo, bz = bo_ref[:, sl], bz_ref[:, sl]
            sbo = _sigmoid(bo)
            sbz, dbz = _silu_both(bz)
            dyb = dy[:, W_A + sl.start:W_A + sl.stop]
            dhw = dyb * sbo * sbz
            dbo_ref[:, sl] = (dyb * hw * sbz * sbo * (1.0 - sbo)).astype(BF16)
            dbo_ref[:, 1024 + sl.start:1024 + sl.stop] = (dyb * hw * sbo * dbz).astype(BF16)
            sum_ref[1:2, sl] += jnp.sum(dhw * hn, axis=0, keepdims=True)
            dhn = dhw * mlw_ref[:, sl]
            dhm_ref[:, sl] = rstd * (dhn - jnp.mean(dhn, axis=-1, keepdims=True)
                                     - hn * jnp.mean(dhn * hn, axis=-1, keepdims=True))

    row = lambda i: (i, 0)
    seg = lambda s: pl.BlockSpec((ROWS, 1024), lambda i: (i, s))
    wspec = pl.BlockSpec((1, 1024), lambda i: (0, 0))
    return pl.pallas_call(
        body, name="post_bwd", grid=(tt // ROWS,),
        in_specs=[pl.BlockSpec((ROWS, 2048), lambda i: (jnp.maximum(i - nbc, 0), 0)), pl.BlockSpec((d, d), lambda i: (0, 0))]
        + [pl.BlockSpec((ROWS, 1024), row)] * 4 + [seg(SEG_AZ), seg(SEG_BO), seg(SEG_BZ), wspec, wspec],
        out_specs=[pl.BlockSpec((ROWS, 1024), row), pl.BlockSpec((ROWS, 1024), row),
                   pl.BlockSpec((ROWS, 1024), row), pl.BlockSpec((ROWS, 2048), row),
                   pl.BlockSpec((8, 1024), lambda i: (0, 0))],
        out_shape=[jax.ShapeDtypeStruct((tt, 1024), F32), jax.ShapeDtypeStruct((tt, 1024), F32),
                   jax.ShapeDtypeStruct((tt, 1024), BF16), jax.ShapeDtypeStruct((tt, 2048), BF16),
                   jax.ShapeDtypeStruct((8, 1024), F32)],
        compiler_params=_vmem(4 * d * d + 30 * ROWS * 2048 * 4),
    )(dz, w_o, o_f, o_b, h_f, h_b, u, u, u, hgw, mlw)


def _final(y, w_o, x, target, modp, ln_g, ln_b):
    t, d = x.shape

    def body(y_ref, w_ref, x_ref, tg_ref, mod_ref, g_ref, b_ref, dz_ref, dxa_ref, sum_ref):
        i = pl.program_id(0)
        zz = lax.dot_general(y_ref[...], w_ref[...], (((1,), (0,)), ((), ())), preferred_element_type=F32)
        gate = mod_ref[0, 2:3, :]
        pre = ALPHA * x_ref[...] + gate * zz
        nh, rstd = _ln_stats(pre)
        err = nh * g_ref[...] + b_ref[...] - tg_ref[...]
        dxo = err * (1.0 / d)
        dnh = dxo * g_ref[...]
        dpre = rstd * (dnh - jnp.mean(dnh, axis=-1, keepdims=True) - nh * jnp.mean(dnh * nh, axis=-1, keepdims=True))
        dz_ref[...] = (gate * dpre).astype(BF16)
        dxa_ref[...] = ALPHA * dpre

        @pl.when(i == 0)
        def _():
            sum_ref[...] = jnp.zeros_like(sum_ref)

        sum_ref[0:1, :] += jnp.sum(dpre * zz, axis=0, keepdims=True)
        sum_ref[1:2, :] += jnp.sum(dxo * nh, axis=0, keepdims=True)
        sum_ref[2:3, :] += jnp.sum(dxo, axis=0, keepdims=True)
        sum_ref[3:4, :] += jnp.sum(err * err, axis=0, keepdims=True)

    row = lambda i: (i, 0)
    vec = pl.BlockSpec((1, d), lambda i: (0, 0))
    return pl.pallas_call(
        body, name="final_ln_loss", grid=(t // ROWS,),
        in_specs=[pl.BlockSpec((ROWS, d), row), pl.BlockSpec((d, d), lambda i: (0, 0)), pl.BlockSpec((ROWS, d), row),
                  pl.BlockSpec((ROWS, d), row), pl.BlockSpec((1, 3, d), lambda i: (1, 0, 0)), vec, vec],
        out_specs=[pl.BlockSpec((ROWS, d), row), pl.BlockSpec((ROWS, d), row), pl.BlockSpec((8, d), lambda i: (0, 0))],
        out_shape=[jax.ShapeDtypeStruct((t, d), BF16), jax.ShapeDtypeStruct((t, d), F32),
                   jax.ShapeDtypeStruct((8, d), F32)],
        compiler_params=_vmem(4 * d * d + 24 * ROWS * d * 4),
    )(y, w_o, x, target, modp, ln_g, ln_b)


GRID_W = 64


def _shift(x, s, ok):
    n = x.shape[0]
    return jnp.where(ok, pltpu.roll(x, s % n, 0), 0.0)


def _grid_masks(n):
    t = lax.broadcasted_iota(jnp.int32, (n, LANE), 0)
    col = t & (GRID_W - 1)
    return dict(left=col >= 1, right=col <= GRID_W - 2, up=t >= GRID_W, down=t < n - GRID_W)


def _seq_masks(n):
    t = lax.broadcasted_iota(jnp.int32, (n, LANE), 0)
    return dict(left=t >= 1, right=t <= n - 2)


def _conv_fwd(u, w9, cb, tc):
    tt = u.shape[0]
    t = tt - tc

    def body(u_ref, w_ref, b_ref, o_ref):
        w = [w_ref[r:r + 1, :] for r in range(9)]
        xc = u_ref[0:tc, :]
        ms = _seq_masks(tc)
        o_ref[0:tc, :] = (w[3] * _shift(xc, 1, ms["left"]) + w[4] * xc + w[5] * _shift(xc, -1, ms["right"])
                          + b_ref[...])
        x = u_ref[tc:tt, :]
        mg = _grid_masks(t)
        taps = (_shift(x, 1, mg["left"]), x, _shift(x, -1, mg["right"]))
        rows = [w[3 * i] * taps[0] + w[3 * i + 1] * taps[1] + w[3 * i + 2] * taps[2] for i in range(3)]
        o_ref[tc:tt, :] = (rows[1] + _shift(rows[0], GRID_W, mg["up"]) + _shift(rows[2], -GRID_W, mg["down"])
                           + b_ref[...])

    return pl.pallas_call(
        body, name="conv_fwd", grid=(2048 // LANE,),
        in_specs=[pl.BlockSpec((tt, LANE), lambda j: (0, BLK_QK + j)), pl.BlockSpec((9, LANE), lambda j: (0, j)),
                  pl.BlockSpec((1, LANE), lambda j: (0, j))],
        out_specs=pl.BlockSpec((tt, LANE), lambda j: (0, j)),
        out_shape=jax.ShapeDtypeStruct((tt, 2048), F32),
        compiler_params=_vmem(40 * tt * LANE * 4),
    )(u, w9, cb)


def _conv_bwd(dcp, u, w9, tc, du):
    tt = u.shape[0]
    t = tt - tc

    def body(d_ref, u_ref, w_ref, du_in_ref, du_ref, gw_ref, gb_ref):
        w = [w_ref[r:r + 1, :] for r in range(9)]
        csum = lambda a: jnp.sum(a, axis=0, keepdims=True)
        dc = d_ref[0:tc, :]
        xc = u_ref[0:tc, :]
        ms = _seq_masks(tc)
        du_ref[0:tc, :] = (w[3] * _shift(dc, -1, ms["right"]) + w[4] * dc + w[5] * _shift(dc, 1, ms["left"])).astype(BF16)
        gmid = [csum(dc * _shift(xc, 1, ms["left"])), csum(dc * xc), csum(dc * _shift(xc, -1, ms["right"]))]
        d = d_ref[tc:tt, :]
        x = u_ref[tc:tt, :]
        mg = _grid_masks(t)
        dtaps = (_shift(d, -1, mg["right"]), d, _shift(d, 1, mg["left"]))
        rows = [w[3 * i] * dtaps[0] + w[3 * i + 1] * dtaps[1] + w[3 * i + 2] * dtaps[2] for i in range(3)]
        du_ref[tc:tt, :] = (rows[1] + _shift(rows[0], -GRID_W, mg["down"]) + _shift(rows[2], GRID_W, mg["up"])).astype(BF16)
        xtaps = (_shift(x, 1, mg["left"]), x, _shift(x, -1, mg["right"]))
        for j in range(3):
            gw_ref[j:j + 1, :] = csum(d * _shift(xtaps[j], GRID_W, mg["up"]))
            gw_ref[3 + j:4 + j, :] = csum(d * xtaps[j]) + gmid[j]
            gw_ref[6 + j:7 + j, :] = csum(d * _shift(xtaps[j], -GRID_W, mg["down"]))
        gb_ref[...] = csum(d) + csum(dc)

    return pl.pallas_call(
        body, name="conv_bwd", grid=(2048 // LANE,),
        in_specs=[pl.BlockSpec((tt, LANE), lambda j: (0, j)), pl.BlockSpec((tt, LANE), lambda j: (0, BLK_QK + j)),
                  pl.BlockSpec((9, LANE), lambda j: (0, j)), _ANY],
        out_specs=[pl.BlockSpec((tt, LANE), lambda j: (0, BLK_QK + j)), pl.BlockSpec((9, LANE), lambda j: (0, j)),
                   pl.BlockSpec((1, LANE), lambda j: (0, j))],
        out_shape=[jax.ShapeDtypeStruct(du.shape, BF16), jax.ShapeDtypeStruct((9, 2048), F32),
                   jax.ShapeDtypeStruct((1, 2048), F32)],
        input_output_aliases={3: 0},
        compiler_params=_vmem(48 * tt * LANE * 4),
    )(dcp, u, w9, du)


SUB = 4
STEP = SUB * CHUNK
ML_SUB = 2
ML_STEP = ML_SUB * CHUNK


def _chunk_of(pos, ncc, nc, rev):
    if not rev:
        return pos
    return jnp.where(pos < ncc, ncc - 1 - pos, nc - 1 - (pos - ncc))


def _sub_rows(rev, sub=SUB):
    order = range(sub - 1, -1, -1) if rev else range(sub)
    return [(s, slice(s * CHUNK, (s + 1) * CHUNK)) for s in order]


def _hgrn_fwd(u, lower_d, ncc, rev):
    tt = u.shape[0]
    nc, ncc = tt // STEP, ncc // SUB
    seg_f = SEG_AFB if rev else SEG_AFF

    def body(zq_ref, zf_ref, v_ref, lb_ref, o_ref, hist_ref, st_ref):
        @pl.when(pl.program_id(0) == 0)
        def _():
            st_ref[...] = jnp.zeros_like(st_ref)

        st = st_ref[...]
        for s, r in _sub_rows(rev):
            hist_ref[s] = st
            o, st = hg_chunk_fwd(zq_ref[r, :], zf_ref[r, :], v_ref[r, :], lb_ref[...], st, rev)
            o_ref[r, :] = o
        st_ref[...] = st

    seg = lambda s: pl.BlockSpec((STEP, 1024), lambda j: (_chunk_of(j, ncc, nc, rev), s))
    return pl.pallas_call(
        body, name="hgrn_fwd_rev" if rev else "hgrn_fwd", grid=(nc,),
        in_specs=[seg(SEG_AQ), seg(seg_f), seg(SEG_AI), pl.BlockSpec((1, 1024), lambda j: (0, 0))],
        out_specs=[pl.BlockSpec((STEP, 1024), lambda j: (_chunk_of(j, ncc, nc, rev), 0)),
                   pl.BlockSpec((SUB, 1024, HG_D), lambda j: (_chunk_of(j, ncc, nc, rev), 0, 0))],
        out_shape=[jax.ShapeDtypeStruct((tt, 1024), F32), jax.ShapeDtypeStruct((nc * SUB, 1024, HG_D), F32)],
        scratch_shapes=[pltpu.VMEM((1024, HG_D), F32)],
    )(u, u, u, lower_d)


def _hgrn_bwd(u, lower_d, hist, do, ncc, rev, ride=None, final=None):
    tt = u.shape[0]
    nc, ncc = tt // STEP, ncc // SUB
    seg_f = SEG_AFB if rev else SEG_AFF
    is_final = final is not None
    has_a2a = ride is not None
    n_out = 2 if is_final else 4
    width = 5 * 1024

    def body(zq_ref, zf_ref, v_ref, lb_ref, hist_ref, do_ref, *rest):
        if is_final:
            aq_ref, av_ref, af_ref, az_ref = rest[:4]
            rest = rest[4:]
        if has_a2a:
            x_ref, rest = rest[0], rest[1:]
        outs, rest = rest[:n_out], rest[n_out:]
        dlb_ref = outs[-1]
        if has_a2a:
            comm = (x_ref, rest[0]) + tuple(rest[2:])
            dst_ref = rest[1]
        else:
            dst_ref = rest[0]

        @pl.when(pl.program_id(0) == 0)
        def _():
            dst_ref[...] = jnp.zeros_like(dst_ref)
            dlb_ref[...] = jnp.zeros_like(dlb_ref)
            if has_a2a:
                ride.start(*comm)

        dst = dst_ref[...]
        dlb_sum = dlb_ref[...]
        for s, r in reversed(_sub_rows(rev)):
            dzq, dzf, dv, dlb, dst = hg_chunk_bwd(zq_ref[r, :], zf_ref[r, :], v_ref[r, :], lb_ref[...],
                                                  hist_ref[s], do_ref[r, :], dst, rev)
            dlb_sum = dlb_sum + dlb
            if is_final:
                du_ref = outs[0]
                dzf_own, dzf_other = dzf.astype(BF16), af_ref[r, :]
                du_ref[r, 0:1024] = (dzq + aq_ref[r, :]).astype(BF16)
                du_ref[r, 1024:2048] = dzf_other if rev else dzf_own
                du_ref[r, 2048:3072] = dzf_own if rev else dzf_other
                du_ref[r, 3072:4096] = (dv + av_ref[r, :]).astype(BF16)
                du_ref[r, 4096:5120] = az_ref[r, :]
            else:
                dzf_ref, dzq_ref, dv_ref = outs[:3]
                dzf_ref[r, :] = dzf.astype(BF16)
                dzq_ref[r, :] = dzq
                dv_ref[r, :] = dv
        dst_ref[...] = dst
        dlb_ref[...] = dlb_sum

        if has_a2a:
            @pl.when(pl.program_id(0) == nc - 1)
            def _():
                ride.wait(*comm)

    cidx = lambda j: _chunk_of(nc - 1 - j, ncc, nc, rev)
    seg = lambda s: pl.BlockSpec((STEP, 1024), lambda j: (cidx(j), s))
    row = pl.BlockSpec((STEP, 1024), lambda j: (cidx(j), 0))
    dlb_spec = pl.BlockSpec((1, 1024), lambda j: (0, 0))
    dlb_shape = jax.ShapeDtypeStruct((1, 1024), F32)
    if is_final:
        out_specs = [pl.BlockSpec((STEP, width), lambda j: (cidx(j), 0)), dlb_spec]
        out_shape = [jax.ShapeDtypeStruct((tt, N_U), BF16), dlb_shape]
    else:
        out_specs = [row, row, row, dlb_spec]
        out_shape = [jax.ShapeDtypeStruct((tt, 1024), BF16), jax.ShapeDtypeStruct((tt, 1024), F32),
                     jax.ShapeDtypeStruct((tt, 1024), F32), dlb_shape]
    ins = [u, u, u, lower_d, hist, do] + (list(final) if is_final else []) + ([ride.x] if has_a2a else [])
    return pl.pallas_call(
        body, name="hgrn_bwd_rev" if rev else "hgrn_bwd", grid=(nc,),
        in_specs=[seg(SEG_AQ), seg(seg_f), seg(SEG_AI), pl.BlockSpec((1, 1024), lambda j: (0, 0)),
                  pl.BlockSpec((SUB, 1024, HG_D), lambda j: (cidx(j), 0, 0)), row] + ([row] * 4 if is_final else [])
        + ([_ANY] if has_a2a else []),
        out_specs=out_specs + ([_ANY] if has_a2a else []),
        out_shape=out_shape + ([ride.out_shape] if has_a2a else []),
        scratch_shapes=[pltpu.VMEM((1024, HG_D), F32)] + (_RIDE_SCRATCH if has_a2a else []),
    )(*ins)


def _gate_views(g_ref, b_ref, r, head, rev):
    gc = g_ref[r, :] + b_ref[...]
    lane = lax.broadcasted_iota(jnp.int32, (1, LANE), 1)
    eye = _eye()
    d = 1 if rev else 0
    ii, fi = d * ML_HEADS + head, 2 * ML_HEADS + d * ML_HEADS + head
    col = lambda idx: jnp.sum(jnp.where(lane == idx, gc, 0.0), axis=1, keepdims=True)
    row = lambda c: jnp.sum(eye * c, axis=0, keepdims=True)
    gi, gf = col(ii), col(fi)
    return gi, row(gi), gf, row(gf)


def _mlstm_fwd(cpre, u, bias, ncc, rev):
    tt = u.shape[0]
    nc, ncc = tt // ML_STEP, ncc // ML_SUB
    nhd = ML_HEADS

    def body(q_ref, k_ref, v_ref, g_ref, b_ref, h_ref, ch_ref, nh_ref, mh_ref, c_ref, n_ref, m_ref):
        @pl.when(pl.program_id(0) == 0)
        def _():
            c_ref[...] = jnp.zeros_like(c_ref)
            n_ref[...] = jnp.zeros_like(n_ref)
            m_ref[...] = jnp.zeros_like(m_ref)

        c, n_all, m_all = c_ref[...], n_ref[...], m_ref[...]
        n = [n_all[hd:hd + 1, :] for hd in range(nhd)]
        m = [m_all[hd:hd + 1, 0:1] for hd in range(nhd)]
        for s, r in _sub_rows(rev, ML_SUB):
            ch_ref[s] = c
            for hd in range(nhd):
                nh_ref[s, hd:hd + 1, :] = n[hd]
                mh_ref[s, hd:hd + 1, :] = jnp.broadcast_to(m[hd], (1, LANE))
            gates = [_gate_views(g_ref, b_ref, r, hd, rev) for hd in range(nhd)]
            h, c, n, m = ml_chunk_fwd(q_ref[r, :], k_ref[r, :], v_ref[r, :], gates, c, n, m, rev)
            h_ref[r, :] = h
        c_ref[...] = c
        for hd in range(nhd):
            n_ref[hd:hd + 1, :] = n[hd]
            m_ref[hd:hd + 1, :] = jnp.broadcast_to(m[hd], (1, LANE))

    cidx = lambda j: _chunk_of(j, ncc, nc, rev)
    row = lambda s: pl.BlockSpec((ML_STEP, 1024), lambda j: (cidx(j), s))
    st3 = lambda a, b: pl.BlockSpec((ML_SUB, a, b), lambda j: (cidx(j), 0, 0))
    return pl.pallas_call(
        body, name="mlstm_fwd_rev" if rev else "mlstm_fwd", grid=(nc,),
        in_specs=[row(0), row(1), row(SEG_BV), pl.BlockSpec((ML_STEP, LANE), lambda j: (cidx(j), BLK_GATE)),
                  pl.BlockSpec((1, LANE), lambda j: (0, 0))],
        out_specs=[row(0), st3(1024, ML_D), st3(8, ML_D), st3(8, LANE)],
        out_shape=[jax.ShapeDtypeStruct((tt, 1024), F32), jax.ShapeDtypeStruct((nc * ML_SUB, 1024, ML_D), F32),
                   jax.ShapeDtypeStruct((nc * ML_SUB, 8, ML_D), F32), jax.ShapeDtypeStruct((nc * ML_SUB, 8, LANE), F32)],
        scratch_shapes=[pltpu.VMEM((1024, ML_D), F32), pltpu.VMEM((8, ML_D), F32), pltpu.VMEM((8, LANE), F32)],
    )(cpre, cpre, u, u, bias)


def _mlstm_bwd(cpre, u, bias, chist, nhist, mhist, h_out, dh, ncc, rev, final=None):
    tt = u.shape[0]
    nc, ncc = tt // ML_STEP, ncc // ML_SUB
    nhd = ML_HEADS
    is_final = final is not None
    d = 1 if rev else 0
    col0, width = SEG_BV * 1024, N_U - SEG_BV * 1024

    def body(q_ref, k_ref, v_ref, g_ref, b_ref, ch_ref, nh_ref, mh_ref, ho_ref, dh_ref, *rest):
        if is_final:
            aqk_ref, av_ref, ag_ref, bo_ref = rest[:4]
            dqk_ref, du_ref, gs_ref, dc_ref, dn_ref = rest[5:]
        else:
            dqk_ref, dv_ref, dg_ref, gs_ref, dc_ref, dn_ref = rest

        @pl.when(pl.program_id(0) == 0)
        def _():
            dc_ref[...] = jnp.zeros_like(dc_ref)
            dn_ref[...] = jnp.zeros_like(dn_ref)
            gs_ref[...] = jnp.zeros_like(gs_ref)

        lane = lax.broadcasted_iota(jnp.int32, (1, LANE), 1)
        dc, dn_all, gs = dc_ref[...], dn_ref[...], gs_ref[...]
        dn = [dn_all[hd:hd + 1, :] for hd in range(nhd)]
        for s, r in reversed(_sub_rows(rev, ML_SUB)):
            gates = [_gate_views(g_ref, b_ref, r, hd, rev) for hd in range(nhd)]
            n_all, m_all = nh_ref[s], mh_ref[s]
            dqp, dkp, dv, dgi, dgf, dc, dn = ml_chunk_bwd(
                q_ref[r, :], k_ref[r, :], v_ref[r, :], gates, ch_ref[s],
                [n_all[hd:hd + 1, :] for hd in range(nhd)], [m_all[hd:hd + 1, 0:1] for hd in range(nhd)],
                ho_ref[r, :], dh_ref[r, :], dc, dn, rev)
            dg = ag_ref[r, :] if is_final else jnp.zeros((CHUNK, LANE), F32)
            for hd in range(nhd):
                dg = dg + jnp.where(lane == d * ML_HEADS + hd, dgi[hd], 0.0)
                dg = dg + jnp.where(lane == 2 * ML_HEADS + d * ML_HEADS + hd, dgf[hd], 0.0)
            if is_final:
                dqp = dqp + aqk_ref[r, 0:W_B]
                dkp = dkp + aqk_ref[r, W_B:2 * W_B]
                du_ref[r, 0:1024] = (dv + av_ref[r, :]).astype(BF16)
                du_ref[r, 1024:3072] = bo_ref[r, :]
                du_ref[r, 3072:3072 + LANE] = dg.astype(BF16)
            else:
                dv_ref[r, :] = dv
                dg_ref[r, :] = dg
            dqk_ref[r, 0:W_B] = dqp
            dqk_ref[r, W_B:2 * W_B] = dkp
            gs = gs + jnp.sum(dg, axis=0, keepdims=True)
        dc_ref[...] = dc
        gs_ref[...] = gs
        for hd in range(nhd):
            dn_ref[hd:hd + 1, :] = dn[hd]

    cidx = lambda j: _chunk_of(nc - 1 - j, ncc, nc, rev)
    row = lambda s: pl.BlockSpec((ML_STEP, 1024), lambda j: (cidx(j), s))
    wide = pl.BlockSpec((ML_STEP, 2048), lambda j: (cidx(j), 0))
    gate = pl.BlockSpec((ML_STEP, LANE), lambda j: (cidx(j), 0))
    st3 = lambda a, b: pl.BlockSpec((ML_SUB, a, b), lambda j: (cidx(j), 0, 0))
    gs_spec, gs_shape = pl.BlockSpec((1, LANE), lambda j: (0, 0)), jax.ShapeDtypeStruct((1, LANE), F32)
    dqk_shape = jax.ShapeDtypeStruct((tt, 2048), F32)
    ins = [cpre, cpre, u, u, bias, chist, nhist, mhist, h_out, dh] + (list(final) if is_final else [])
    if is_final:
        out_specs = [wide, pl.BlockSpec((pl.Element(ML_STEP), pl.Element(width)), lambda j: (cidx(j) * ML_STEP, col0)), gs_spec]
        out_shape = [dqk_shape, jax.ShapeDtypeStruct((tt, N_U), BF16), gs_shape]
    else:
        out_specs = [wide, row(0), gate, gs_spec]
        out_shape = [dqk_shape, jax.ShapeDtypeStruct((tt, 1024), F32), jax.ShapeDtypeStruct((tt, LANE), F32), gs_shape]
    return pl.pallas_call(
        body, name="mlstm_bwd_rev" if rev else "mlstm_bwd", grid=(nc,),
        in_specs=[row(0), row(1), row(SEG_BV), pl.BlockSpec((ML_STEP, LANE), lambda j: (cidx(j), BLK_GATE)),
                  pl.BlockSpec((1, LANE), lambda j: (0, 0)),
                  st3(1024, ML_D), st3(8, ML_D), st3(8, LANE), row(0), row(0)]
        + ([wide, row(0), gate, wide, _ANY] if is_final else []),
        out_specs=out_specs, out_shape=out_shape,
        input_output_aliases={14: 1} if is_final else {},
        scratch_shapes=[pltpu.VMEM((1024, ML_D), F32), pltpu.VMEM((8, ML_D), F32)],
    )(*ins)


def _whole(body, out_shape, name, *args, nbytes=0):
    return pl.pallas_call(body, name=name, out_shape=out_shape, compiler_params=_vmem(nbytes))(*args)


def _mod_fwd(cs, w_cols, b_cols):
    def body(c_ref, w_ref, b_ref, o_ref):
        o_ref[...] = _exact_nn(_silu(c_ref[...]), w_ref[...]) + b_ref[...]

    return _whole(body, jax.ShapeDtypeStruct((16, w_cols.shape[1]), F32), "mod_fwd", cs, w_cols, b_cols,
                  nbytes=4 * w_cols.size * 4)


def _mod_bwd_w(cs, d9, w_cols):
    def body(c_ref, d_ref, w_ref, gw_ref, pc_ref):
        gw_ref[...] = _exact_tn(_silu(c_ref[...]), d_ref[...])
        pc = lax.dot_general(d_ref[8:16, :], w_ref[...], (((1,), (1,)), ((), ())), precision=lax.Precision.HIGHEST,
                             preferred_element_type=F32)
        row = lax.broadcasted_iota(jnp.int32, pc.shape, 0)
        pc_ref[...] = jnp.where(row == 0, pc, 0.0)

    return _whole(body, [jax.ShapeDtypeStruct(w_cols.shape, F32), jax.ShapeDtypeStruct((8, w_cols.shape[0]), F32)],
                  "mod_bwd_w", cs, d9, w_cols, nbytes=6 * w_cols.size * 4)


def _lower_fwd(lb4):
    def body(l_ref, o_ref):
        o_ref[...] = jnp.zeros_like(o_ref)
        o_ref[0:1, :] = 1.0 / (1.0 + jnp.exp(l_ref[1:2, :] - l_ref[0:1, :]))
        o_ref[1:2, :] = 1.0 / (1.0 + jnp.exp(l_ref[3:4, :] - l_ref[2:3, :]))

    return _whole(body, jax.ShapeDtypeStruct((8, lb4.shape[1]), F32), "lower_fwd", lb4)


def _reduce8(g, name):
    def body(g_ref, o_ref):
        acc = g_ref[0]
        for k in range(1, N_DEV):
            acc = acc + g_ref[k]
        o_ref[...] = acc

    return _whole(body, jax.ShapeDtypeStruct(g.shape[1:], F32), name, g, nbytes=4 * g.size * 4)


_PACK = (("dmodx", 48), ("dmodc", 48), ("gconvw", 144), ("gconvb", 16), ("dlower", 16), ("ghgw", 8), ("gmlw", 8),
         ("glng", 16), ("glnb", 16), ("losssq", 16), ("ggate", 8))


def _pack_offsets():
    off, out = 0, {}
    for name, rows in _PACK:
        out[name] = (off, rows)
        off += rows
    return out


def _small_finish(total, p0, d_feat):
    offs = _pack_offsets()

    def body(t_ref, p_ref, gb_ref, a0_ref, a1_ref, loss_ref):
        ox, oc, ol, oq = offs["dmodx"][0], offs["dmodc"][0], offs["dlower"][0], offs["losssq"][0]
        gb_ref[...] = t_ref[ox:ox + 48, :] + t_ref[oc:oc + 48, :]
        p = p_ref[...]
        da0 = t_ref[ol:ol + 16, :] * p * (1.0 - p)
        a0_ref[...] = da0
        a1_ref[...] = -da0
        sq = t_ref[oq:oq + 16, :]
        tot = jnp.sum(jnp.sum(sq, axis=1, keepdims=True), axis=0, keepdims=True)
        loss_ref[...] = jnp.broadcast_to(tot * (0.5 / d_feat), loss_ref.shape)

    s = jax.ShapeDtypeStruct
    return _whole(body, [s((48, LANE), F32), s((16, LANE), F32), s((16, LANE), F32), s((8, LANE), F32)],
                  "small_finish", total, p0)


def _cctx_grad(parts, c_ctx8):
    def body(p_ref, c_ref, o_ref):
        acc = p_ref[0]
        for k in range(1, N_DEV):
            acc = acc + p_ref[k]
        o_ref[...] = acc * _dsilu(c_ref[...])

    return _whole(body, jax.ShapeDtypeStruct(c_ctx8.shape, F32), "cctx_grad", parts, c_ctx8)


def _adam_math(w, g, m, v):
    m = ADAM_B1 * m + (1.0 - ADAM_B1) * g
    v = ADAM_B2 * v + (1.0 - ADAM_B2) * (g * g)
    m_hat = m / (1.0 - ADAM_B1 ** ADAM_STEP)
    v_hat = v / (1.0 - ADAM_B2 ** ADAM_STEP)
    delta = -ADAM_LR * (m_hat / (jnp.sqrt(v_hat) + ADAM_EPS) + ADAM_WD * w)
    return delta, m, v


def _adamw(w, g, m, v, rows, name):
    r, c = w.shape

    def body(w_ref, g_ref, m_ref, v_ref, d_ref, mo_ref, vo_ref):
        d_ref[...], mo_ref[...], vo_ref[...] = _adam_math(w_ref[...], g_ref[...], m_ref[...], v_ref[...])

    spec = pl.BlockSpec((rows, c), lambda i: (i, 0))
    return pl.pallas_call(
        body, name=name, grid=(r // rows,), in_specs=[spec] * 4, out_specs=[spec] * 3,
        out_shape=[jax.ShapeDtypeStruct((r, c), F32)] * 3,
        compiler_params=_vmem(16 * rows * (c + LANE) * 4),
    )(w, g, m, v)


def _rs_adamw(recv, w, m, v, tile, name, by_cols=False):
    _, r, c = recv.shape

    def body(r_ref, w_ref, m_ref, v_ref, g_ref, d_ref, mo_ref, vo_ref):
        g = r_ref[0].astype(F32)
        for k in range(1, N_DEV):
            g = g + r_ref[k].astype(F32)
        g_ref[...] = g
        d_ref[...], mo_ref[...], vo_ref[...] = _adam_math(w_ref[...], g, m_ref[...], v_ref[...])

    if by_cols:
        spec = pl.BlockSpec((r, tile), lambda i: (0, i))
        rspec = pl.BlockSpec((N_DEV, r, tile), lambda i: (0, 0, i))
        steps, elems = c // tile, (r + 16) * tile
    else:
        spec = pl.BlockSpec((tile, c), lambda i: (i, 0))
        rspec = pl.BlockSpec((N_DEV, tile, c), lambda i: (0, i, 0))
        steps, elems = r // tile, tile * (c + LANE)
    return pl.pallas_call(
        body, name=name, grid=(steps,), in_specs=[rspec] + [spec] * 3, out_specs=[spec] * 4,
        out_shape=[jax.ShapeDtypeStruct((r, c), F32)] * 4,
        compiler_params=_vmem(2 * elems * (N_DEV * 2 + 7 * 4) + (4 << 20)),
    )(recv, w, m, v)


def _all_gather(x, name):
    r, c = x.shape

    def body(x_ref, out_ref, send_sems, recv_sems, local_sem):
        px, py, pc = _position()
        me, sibling = (px, py, pc), (px, py, 1 - pc)
        chips = [(1 - px, py), (px, 1 - py), (1 - px, 1 - py)]

        def slot(qx, qy, qc):
            return out_ref.at[4 * qx + 2 * qy + qc]

        def copy(k, block, to, src=None):
            return pltpu.make_async_remote_copy(
                src_ref=slot(*block) if src is None else src, dst_ref=slot(*block),
                send_sem=send_sems.at[k], recv_sem=recv_sems.at[k], device_id=to, device_id_type=MESH)

        mine = pltpu.make_async_copy(x_ref, slot(*me), local_sem)
        mine.start()
        first = [copy(1 + j, me, (*chip, pc), src=x_ref) for j, chip in enumerate(chips)]
        first.append(copy(0, me, sibling, src=x_ref))
        for cp in first:
            cp.start()
        passed = [copy(4 + j, (*chip, pc), sibling) for j, chip in enumerate(chips)]
        for j, chip in enumerate(chips):
            copy(1 + j, (*chip, pc), me).wait_recv()
            passed[j].start()
        copy(0, sibling, me).wait_recv()
        for j, chip in enumerate(chips):
            copy(4 + j, (*chip, 1 - pc), me).wait_recv()
        for cp in first + passed:
            cp.wait_send()
        mine.wait()

    return pl.pallas_call(
        body, name=name, out_shape=jax.ShapeDtypeStruct((N_DEV, r, c), x.dtype),
        in_specs=[pl.BlockSpec(memory_space=pl.ANY)], out_specs=pl.BlockSpec(memory_space=pl.ANY),
        scratch_shapes=[pltpu.SemaphoreType.DMA((7,)), pltpu.SemaphoreType.DMA((7,)), pltpu.SemaphoreType.DMA],
    )(x)


class _RelayGather:
    ZERO_ROWS = 128

    def __init__(self, x):
        self.x = x
        self.half = x.shape[1] // 2
        self.out_shape = jax.ShapeDtypeStruct((N_DEV + 1,) + x.shape, x.dtype)
        self.scratch = [pltpu.SemaphoreType.DMA((10,)), pltpu.SemaphoreType.DMA((10,)), pltpu.SemaphoreType.DMA,
                        pltpu.VMEM((self.ZERO_ROWS, x.shape[1]), x.dtype), pltpu.SemaphoreType.DMA]

    def _zero_tail(self, out_ref, zero_buf, zero_sem):
        return pltpu.make_async_copy(zero_buf, out_ref.at[N_DEV, pl.ds(0, self.ZERO_ROWS), :], zero_sem)

    def _parts(self, x_ref, out_ref, send_sems, recv_sems, local_sem, zero_buf, zero_sem):
        px, py, pc = _position()
        me, sib = (px, py, pc), (px, py, 1 - pc)
        xn, yn, dg = (1 - px, py, pc), (px, 1 - py, pc), (1 - px, 1 - py, pc)
        half = self.half

        def slot(owner, cols=None):
            ref = out_ref.at[4 * owner[0] + 2 * owner[1] + owner[2]]
            return ref if cols is None else ref.at[:, pl.ds(cols, half)]

        def copy(k, owner, to, own=False, cols=None):
            src = slot(owner, cols) if not own else (x_ref if cols is None else x_ref.at[:, pl.ds(cols, half)])
            return pltpu.make_async_remote_copy(
                src_ref=src, dst_ref=slot(owner, cols),
                send_sem=send_sems.at[k], recv_sem=recv_sems.at[k], device_id=to, device_id_type=MESH)

        mine = pltpu.make_async_copy(x_ref, slot(me), local_sem)
        own = [copy(1, me, xn, own=True, cols=0), copy(9, me, yn, own=True, cols=half),
               copy(8, me, xn, own=True, cols=half), copy(2, me, yn, own=True, cols=0), copy(0, me, sib, own=True)]
        return me, sib, xn, yn, dg, copy, mine, own

    def start(self, *refs):
        *_, mine, own = self._parts(*refs)
        mine.start()
        for cp in own:
            cp.start()
        refs[5][...] = jnp.zeros_like(refs[5])
        self._zero_tail(refs[1], refs[5], refs[6]).start()

    def finish(self, *refs):
        me, sib, xn, yn, dg, copy, mine, own = self._parts(*refs)
        flip = lambda q: (q[0], q[1], 1 - q[2])
        half = self.half
        copy(1, xn, me, cols=0).wait_recv()
        relay_x = [copy(3, xn, yn, cols=0)]
        relay_x[0].start()
        copy(9, yn, me, cols=half).wait_recv()
        relay_y = [copy(4, yn, xn, cols=half)]
        relay_y[0].start()
        copy(8, xn, me, cols=half).wait_recv()
        relay_x.append(copy(5, xn, sib))
        relay_x[1].start()
        copy(2, yn, me, cols=0).wait_recv()
        relay_y.append(copy(6, yn, sib))
        relay_y[1].start()
        copy(3, dg, me, cols=0).wait_recv()
        copy(4, dg, me, cols=self.half).wait_recv()
        relay_d = copy(7, dg, sib)
        relay_d.start()
        copy(0, sib, me).wait_recv()
        copy(5, flip(xn), me).wait_recv()
        copy(6, flip(yn), me).wait_recv()
        copy(7, flip(dg), me).wait_recv()
        for cp in own + relay_x + relay_y + [relay_d]:
            cp.wait_send()
        mine.wait()
        self._zero_tail(refs[1], refs[5], refs[6]).wait()


DW_PIECES = ((0, 256), (256, 640), (896, 1152))


def _local_step(ctx, x, target, modp, lower, wt_u, w_o, w9, conv_b, gate_b, hgw, mlw, ln_g, ln_b, exchange):
    tc = ctx.shape[0]
    tt = tc + x.shape[0]
    nbc, ncc = tc // ROWS, tc // CHUNK
    lower_f, lower_b = lower[0:1], lower[1:2]

    tmh = _pick(tt, (1088, 768, 512, 256))
    if exchange:
        hc, wt = _modulate_fwd(ctx, x, modp, gather=_RelayGather(wt_u))
        wt_u = wt.reshape(-1, D_MODEL)
        u, w_o = _mm(hc, wt_u, "nt", F32, tmh, 1152, D_MODEL, "mm_u", ride=_Ride("gather", w_o), n_rows=N_U)
        w_o = w_o.reshape(D_MODEL, D_MODEL)
    else:
        hc = _modulate_fwd(ctx, x, modp)
        u = _mm(hc, wt_u, "nt", F32, tmh, 1152, D_MODEL, "mm_u")
    cpre = _conv_fwd(u, w9, conv_b, tc)
    bias = jnp.pad(gate_b.reshape(1, 16), ((0, 0), (0, LANE - 16)))

    o_f, hist_f = _hgrn_fwd(u, lower_f, ncc, False)
    o_b, hist_b = _hgrn_fwd(u, lower_b, ncc, True)
    h_f, ch_f, nh_f, mh_f = _mlstm_fwd(cpre, u, bias, ncc, False)
    h_b, ch_b, nh_b, mh_b = _mlstm_fwd(cpre, u, bias, ncc, True)
    y = _post_fwd(o_f, o_b, h_f, h_b, u, hgw, mlw, nbc)
    dz, dxa, fsum = _final(y, w_o, x, target, modp, ln_g, ln_b)

    dw_o = _mm(y, dz, "tn", BF16, D_MODEL, 1024, _pick(y.shape[0], (1024, 512, 256)), "mm_dwo")
    do, dhm, daz, dbo, psum = _post_bwd(dz, w_o, o_f, o_b, h_f, h_b, u, hgw, mlw, nbc)
    if exchange:
        dzf_f, dzq, dv_a, dlb_f, dw_o = _hgrn_bwd(
            u, lower_f, hist_f, do, ncc, False, ride=_Ride("a2a", dw_o.reshape(N_DEV, D_MODEL // N_DEV, D_MODEL)))
    else:
        dzf_f, dzq, dv_a, dlb_f = _hgrn_bwd(u, lower_f, hist_f, do, ncc, False)
    du, dlb_b = _hgrn_bwd(u, lower_b, hist_b, do, ncc, True, final=(dzq, dv_a, dzf_f, daz))
    dqk, dv_m, dg, _ = _mlstm_bwd(cpre, u, bias, ch_f, nh_f, mh_f, h_f, dhm, ncc, False)
    dqk, du, gsum = _mlstm_bwd(cpre, u, bias, ch_b, nh_b, mh_b, h_b, dhm, ncc, True, final=(dqk, dv_m, dg, dbo, du))
    du, gconvw, gconvb = _conv_bwd(dqk, u, w9, tc, du)
    tkw = _pick(tt, (2176, 768, 512, 256))
    blocks = lambda g: g.reshape(N_DEV, N_IN // N_DEV, g.shape[1])
    dwu = lambda name, cols, ride: _mm(du, hc, "tn", BF16, 1152, cols[1], tkw, name, b_cols=cols, ride=ride, m_out=N_IN)
    dwt_a = dwu("mm_dwu_a", DW_PIECES[0], None)
    if exchange:
        whole = lambda piece, into: _Ride("a2a", blocks(piece[1]), cols=(piece[0][0], D_MODEL), into=into)
        dwt_b, got = dwu("mm_dwu_b", DW_PIECES[1], whole((DW_PIECES[0], dwt_a), None))
        dwt_c, got = dwu("mm_dwu_c", DW_PIECES[2], whole((DW_PIECES[1], dwt_b), got))
        dh, dwt_u = _mm(du, wt_u, "nn", F32, tmh, D_MODEL // 2, 3456, "mm_dh", ride=whole((DW_PIECES[2], dwt_c), got))
    else:
        dwt_u = jnp.concatenate([dwt_a, dwu("mm_dwu_b", DW_PIECES[1], None), dwu("mm_dwu_c", DW_PIECES[2], None)], axis=1)
        dh = _mm(du, wt_u, "nn", F32, tmh, D_MODEL // 2, 3456, "mm_dh")
    gx, msum = _modulate_bwd(dh, ctx, x, modp, dxa)

    zero_row = jnp.zeros((1, D_MODEL), F32)
    small = dict(
        dmodx=jnp.concatenate([msum[2:3], msum[3:4], fsum[0:1]], axis=0),
        dmodc=jnp.concatenate([msum[0:1], msum[1:2], zero_row], axis=0),
        gconvw=gconvw, gconvb=gconvb, dlower=jnp.concatenate([dlb_f, dlb_b], axis=0),
        ghgw=psum[0:1], gmlw=psum[1:2], glng=fsum[1:2], glnb=fsum[2:3], losssq=fsum[3:4],
        ggate=jnp.concatenate([gsum, jnp.zeros((7, LANE), F32)], axis=0))
    return gx, dwt_u, dw_o, small


def _pack_small(small):
    return jnp.concatenate([small[name].reshape(rows, LANE) for name, rows in _PACK], axis=0)


def _flat_pad(a, rows):
    flat = a.reshape(-1)
    return jnp.pad(flat, (0, rows * LANE - flat.shape[0])).reshape(rows, LANE)


def kernel(x, c, ctx, c_ctx, w_mod, b_mod, w_in, conv_w, conv_b, hg_lb, ml_gate_b, hg_norm_w, ml_norm_w, w_out, ln_g, ln_b, loss_target, m_c_ctx, m_w_mod, m_b_mod, m_w_in, m_conv_w, m_conv_b, m_hg_lb, m_ml_gate_b, m_hg_norm_w, m_ml_norm_w, m_w_out, m_ln_g, m_ln_b, v_c_ctx, v_w_mod, v_b_mod, v_w_in, v_conv_w, v_conv_b, v_hg_lb, v_ml_gate_b, v_hg_norm_w, v_ml_norm_w, v_w_out, v_ln_g, v_ln_b):
    px, py, pc = _position()
    me = 4 * px + 2 * py + pc
    d = D_MODEL
    n_mod = w_mod.shape[2]
    n_cv = conv_w.shape[3]
    n_lb = hg_lb.shape[2]

    pack0 = jnp.concatenate([c.reshape(-1), conv_w.reshape(-1), hg_lb.reshape(-1)]).reshape(1, -1)
    g0 = _all_gather(pack0, "gather_small_inputs")[:, 0, :]
    c_all = g0[:, :d]
    w9 = jnp.transpose(g0[:, d:d + 9 * n_cv].reshape(N_DEV, 9, n_cv), (1, 0, 2)).reshape(9, N_DEV * n_cv)
    lb4 = jnp.transpose(g0[:, d + 9 * n_cv:].reshape(N_DEV, 4, n_lb), (1, 0, 2)).reshape(4, N_DEV * n_lb)
    lower = _lower_fwd(lb4)

    cs = jnp.concatenate([c_all, c_ctx.reshape(1, d), jnp.zeros((7, d), F32)], axis=0)
    b_cols = lax.dynamic_slice(b_mod, (0, me * n_mod), (1, n_mod))
    slab = _mod_fwd(cs, w_mod[0], b_cols)
    mod_all = jnp.transpose(_all_gather(slab, "gather_mod"), (1, 0, 2)).reshape(16, N_DEV * n_mod)
    mod_x = lax.dynamic_slice(mod_all, (me, 0), (1, 3 * d)).reshape(3, d)
    modp = jnp.stack([mod_all[8].reshape(3, d), mod_x])

    gx, recv_wi, recv_wo, small = _local_step(ctx[0], x[0], loss_target[0], modp, lower, w_in[0].T.astype(BF16),
                                              w_out[0].astype(BF16), w9, conv_b, ml_gate_b[0], hg_norm_w, ml_norm_w,
                                              ln_g, ln_b, True)
    g_wi, d_wi, nm_wi, nv_wi = [a.T for a in _rs_adamw(recv_wi, w_in[0].T, m_w_in[0].T, v_w_in[0].T, 256,
                                                       "adamw_w_in", by_cols=True)]
    g_wo, d_wo, nm_wo, nv_wo = _rs_adamw(recv_wo, w_out[0], m_w_out[0], v_w_out[0], 64, "adamw_w_out")

    packs = _all_gather(_pack_small(small), "gather_small_grads")
    total = _reduce8(packs, "reduce_small_grads")
    offs = _pack_offsets()
    piece = lambda name: total[offs[name][0]:offs[name][0] + offs[name][1]]
    g_bmod, g_lb0, g_lb1, loss8 = _small_finish(total, lower[0:2].reshape(16, LANE), float(d))

    ox = offs["dmodx"][0]
    dmodx_all = packs[:, ox:ox + 48, :].reshape(N_DEV, 3 * d)
    dmodc_tot = piece("dmodc").reshape(1, 3 * d)
    d9 = jnp.concatenate([dmodx_all, dmodc_tot, jnp.zeros((7, 3 * d), F32)], axis=0)
    d9_cols = lax.dynamic_slice(d9, (0, me * n_mod), (16, n_mod))
    g_wmod, pc_part = _mod_bwd_w(cs, d9_cols, w_mod[0])
    c_ctx8 = jnp.concatenate([c_ctx.reshape(1, d), jnp.zeros((7, d), F32)], axis=0)
    g_cctx = _cctx_grad(_all_gather(pc_part, "gather_cctx"), c_ctx8)[0]
    d_wmod, nm_wmod, nv_wmod = _adamw(w_mod[0], g_wmod, m_w_mod[0], v_w_mod[0], 256, "adamw_w_mod")

    g_convw_full = piece("gconvw").reshape(9, d)
    g_convw = lax.dynamic_slice(g_convw_full, (0, me * n_cv), (9, n_cv)).reshape(conv_w.shape)
    lb_full = jnp.stack([jnp.stack([g_lb0[0:8].reshape(-1), g_lb1[0:8].reshape(-1)]),
                         jnp.stack([g_lb0[8:16].reshape(-1), g_lb1[8:16].reshape(-1)])])
    g_hglb = lax.dynamic_slice(lb_full, (0, 0, me * n_lb), (2, 2, n_lb))
    grads = dict(
        c_ctx=g_cctx, b_mod=g_bmod.reshape(b_mod.shape), conv_w=g_convw, conv_b=piece("gconvb").reshape(conv_b.shape),
        hg_lb=g_hglb, ml_gate_b=piece("ggate")[0, :16].reshape(ml_gate_b.shape),
        hg_norm_w=piece("ghgw").reshape(hg_norm_w.shape), ml_norm_w=piece("gmlw").reshape(ml_norm_w.shape),
        ln_g=piece("glng").reshape(ln_g.shape), ln_b=piece("glnb").reshape(ln_b.shape))
    params = dict(c_ctx=(c_ctx, m_c_ctx, v_c_ctx), b_mod=(b_mod, m_b_mod, v_b_mod), conv_w=(conv_w, m_conv_w, v_conv_w),
                  conv_b=(conv_b, m_conv_b, v_conv_b), hg_lb=(hg_lb, m_hg_lb, v_hg_lb),
                  ml_gate_b=(ml_gate_b, m_ml_gate_b, v_ml_gate_b), hg_norm_w=(hg_norm_w, m_hg_norm_w, v_hg_norm_w),
                  ml_norm_w=(ml_norm_w, m_ml_norm_w, v_ml_norm_w), ln_g=(ln_g, m_ln_g, v_ln_g), ln_b=(ln_b, m_ln_b, v_ln_b))
    names = list(params)
    rows_of = {n: -(-params[n][0].size // LANE) for n in names}
    rows_tot = -(-sum(rows_of.values()) // 8) * 8
    cat = lambda arrs: jnp.concatenate(
        [_flat_pad(a, rows_of[n]) for n, a in zip(names, arrs)]
        + [jnp.ones((rows_tot - sum(rows_of.values()), LANE), F32)], axis=0)
    d_s, m_s, v_s = _adamw(cat([params[n][0] for n in names]), cat([grads[n] for n in names]),
                           cat([params[n][1] for n in names]), cat([params[n][2] for n in names]), rows_tot, "adamw_small")
    delta, new_m, new_v, off = {}, {}, {}, 0
    for n in names:
        shape, size = params[n][0].shape, params[n][0].size
        take = lambda a: a[off:off + rows_of[n]].reshape(-1)[:size].reshape(shape)
        delta[n], new_m[n], new_v[n] = take(d_s), take(m_s), take(v_s)
        off += rows_of[n]
    grads.update(w_mod=g_wmod[None], w_in=g_wi[None], w_out=g_wo[None])
    delta.update(w_mod=d_wmod[None], w_in=d_wi[None], w_out=d_wo[None])
    new_m.update(w_mod=nm_wmod[None], w_in=nm_wi[None], w_out=nm_wo[None])
    new_v.update(w_mod=nv_wmod[None], w_in=nv_wi[None], w_out=nv_wo[None])

    order = ("c_ctx", "w_mod", "b_mod", "w_in", "conv_w", "conv_b", "hg_lb", "ml_gate_b", "hg_norm_w", "ml_norm_w",
             "w_out", "ln_g", "ln_b")
    return (loss8[0, 0], gx[None], *[grads[n] for n in order], *[delta[n] for n in order],
            *[new_m[n] for n in order], *[new_v[n] for n in order])
```
